```python
import math
import jax
import jax.numpy as jnp
from jax import lax
import numpy as np

D_MODEL = 1024
BATCH = 32
SEQ = 256
DEPTH = 2
DEC_BATCH = 2
DEC_SEQ = 4096
PAST_LEN = 256

GRID_W = 64
HEAD_DIM = 128
N_HEADS = D_MODEL // HEAD_DIM
N_KV_HEADS = N_HEADS // 4
Q_PER_KV = N_HEADS // N_KV_HEADS
Q_BLOCK = 128
ROPE_THETA = 10000.0
SSM_INNER = D_MODEL
SSM_HEADDIM = 64
SSM_HEADS = SSM_INNER // SSM_HEADDIM
SSM_GROUPS = 2
SSM_STATE = 128
SSM_BC = SSM_GROUPS * SSM_STATE
SSM_XBC = SSM_INNER + 2 * SSM_BC
SSM_CONV = 3
SSM_CHUNK = 128
DT_MIN = 0.001
DT_MAX = 0.1
CONV_WIDTH = D_MODEL
SHORT_CONV = 3
N_BRANCHES = 3
D_FF = 2816
N_EXPERTS = 8
TOP_K = 2
EPS = 1e-6
IN_WIDTHS = (N_HEADS * HEAD_DIM, N_KV_HEADS * HEAD_DIM, N_KV_HEADS * HEAD_DIM,
             SSM_INNER, SSM_XBC, SSM_HEADS, SSM_HEADS,
             CONV_WIDTH, CONV_WIDTH, CONV_WIDTH, N_BRANCHES * D_MODEL)
IN_COLS = sum(IN_WIDTHS)

kernel_name = 'hybrid_diffusion_ssd_gqa_shortconv_step'


def rmsnorm(x, g):
    xf = x.astype(jnp.float32)
    y = xf * lax.rsqrt(jnp.mean(xf * xf, axis=-1, keepdims=True) + EPS)
    return y.astype(x.dtype) * g.astype(x.dtype)


def adaln(cond, w, b):
    m = (jax.nn.silu(cond) @ w + b).reshape(cond.shape[0], 1, 6, D_MODEL)
    return tuple(m[:, :, i] for i in range(6))


def conv_centred(x, w):
    k = w.shape[0]
    pad = k // 2
    return lax.conv_general_dilated(
        x, w[:, None, :].astype(x.dtype), window_strides=(1,), padding=((pad, pad),),
        dimension_numbers=('NWC', 'WIO', 'NWC'), feature_group_count=x.shape[-1])


def _rotate(x, pos):
    n = x.shape[-1] // 2
    inv = ROPE_THETA ** (-jnp.arange(n, dtype=jnp.float32) / n)
    ang = pos.astype(jnp.float32)[:, None] * inv[None, :]
    cos = jnp.cos(ang)[:, None, :]
    sin = jnp.sin(ang)[:, None, :]
    x1 = x[..., :n].astype(jnp.float32)
    x2 = x[..., n:].astype(jnp.float32)
    return jnp.concatenate([x1 * cos - x2 * sin, x1 * sin + x2 * cos], axis=-1).astype(x.dtype)


def rope_2d(x, rows, cols):
    ax = HEAD_DIM // 2
    return jnp.concatenate([_rotate(x[..., :ax], rows), _rotate(x[..., ax:], cols)], axis=-1)


def blocked_attention(q, k, v):
    bsz, sq = q.shape[:2]
    nb = sq // Q_BLOCK
    qb = jnp.moveaxis(q.reshape(bsz, nb, Q_BLOCK, N_KV_HEADS, Q_PER_KV, HEAD_DIM), 1, 0)
    scale = HEAD_DIM ** -0.5

    def one_block(q_blk):
        s = jnp.einsum('bqhgd,bkhd->bhgqk', q_blk, k).astype(jnp.float32) * scale
        p = jax.nn.softmax(s, axis=-1).astype(v.dtype)
        return jnp.einsum('bhgqk,bkhd->bqhgd', p, v)

    out = lax.map(one_block, qb)
    return jnp.moveaxis(out, 0, 1).reshape(bsz, sq, N_HEADS * HEAD_DIM)


def ssd_chunked(x, dt, a, bm, cm, h0):
    bsz, length = x.shape[:2]
    nc = length // SSM_CHUNK
    hg = SSM_HEADS // SSM_GROUPS
    f32 = jnp.float32
    xdt = (x.astype(f32) * dt[..., None]).reshape(bsz, nc, SSM_CHUNK, SSM_GROUPS, hg, SSM_HEADDIM)
    acs = jnp.cumsum((dt * a).reshape(bsz, nc, SSM_CHUNK, SSM_GROUPS, hg), axis=2)
    acs = jnp.transpose(acs, (0, 1, 3, 4, 2))
    bc = bm.astype(f32).reshape(bsz, nc, SSM_CHUNK, SSM_GROUPS, SSM_STATE)
    cc = cm.astype(f32).reshape(bsz, nc, SSM_CHUNK, SSM_GROUPS, SSM_STATE)
    diff = acs[..., :, None] - acs[..., None, :]
    lower = jnp.tril(jnp.ones((SSM_CHUNK, SSM_CHUNK), dtype=bool))
    decay = jnp.exp(jnp.where(lower, diff, -jnp.inf))
    cb = jnp.einsum('bcign,bcjgn->bcgij', cc, bc)
    y_diag = jnp.einsum('bcghij,bcjghp->bcighp', cb[:, :, :, None] * decay, xdt)
    decay_to_end = jnp.exp(acs[..., -1:] - acs)
    chunk_states = jnp.einsum('bcjgn,bcghj,bcjghp->bcghpn', bc, decay_to_end, xdt)
    chunk_decay = jnp.exp(acs[..., -1])

    def step(h, inp):
        dec, st = inp
        return dec[..., None, None] * h + st, h

    h_last, h_in = lax.scan(step, h0.astype(f32).reshape(bsz, SSM_GROUPS, hg, SSM_HEADDIM, SSM_STATE),
                            (jnp.moveaxis(chunk_decay, 1, 0), jnp.moveaxis(chunk_states, 1, 0)))
    h_in = jnp.moveaxis(h_in, 0, 1)
    y_off = jnp.einsum('bcign,bcghpn,bcghi->bcighp', cc, h_in, jnp.exp(acs))
    y = (y_diag + y_off).reshape(bsz, length, SSM_HEADS, SSM_HEADDIM)
    return y, h_last.reshape(bsz, SSM_HEADS, SSM_HEADDIM, SSM_STATE)


def ssd_branch(z, xbc, dt_f, dt_b, h0_f, h0_b, lp):
    bsz, length, _ = z.shape
    xbc = jax.nn.silu(conv_centred(xbc, lp['ssm_conv_w']) + lp['ssm_conv_b'])
    xs, bm, cm = jnp.split(xbc, [SSM_INNER, SSM_INNER + SSM_BC], axis=-1)
    xs = xs.reshape(bsz, length, SSM_HEADS, SSM_HEADDIM)
    bm = bm.reshape(bsz, length, SSM_GROUPS, SSM_STATE)
    cm = cm.reshape(bsz, length, SSM_GROUPS, SSM_STATE)
    a = -jnp.exp(lp['ssm_a_log'].astype(jnp.float32))
    dt = jax.nn.softplus(jnp.stack([dt_f, dt_b]).astype(jnp.float32)
                         + lp['ssm_dt_bias'].astype(jnp.float32)[:, None, None, :])
    y_f, h_f = ssd_chunked(xs, dt[0], a[0], bm, cm, h0_f)
    rev = lambda t: jnp.flip(t, axis=1)
    y_b, h_b = ssd_chunked(rev(xs), rev(dt[1]), a[1], rev(bm), rev(cm), h0_b)
    y = y_f + rev(y_b) + lp['ssm_d'].astype(jnp.float32)[:, None] * xs.astype(jnp.float32)
    y = y.reshape(bsz, length, SSM_INNER) * jax.nn.silu(z.astype(jnp.float32))
    y = rmsnorm(y, lp['ssm_norm_g']).astype(z.dtype)
    return y, h_f, h_b


def swiglu(h, w1, w3, w2):
    return (jax.nn.silu(h @ w1) * (h @ w3)) @ w2


def moe_swiglu(h, router, w1, w3, w2):
    logits = (h @ router).astype(jnp.float32)
    top_v, top_i = lax.top_k(logits, TOP_K)
    top_p = jax.nn.softmax(top_v, axis=-1)
    gate = jnp.sum(jax.nn.one_hot(top_i, N_EXPERTS, dtype=jnp.float32) * top_p[..., None], axis=-2).astype(h.dtype)
    out = jnp.zeros_like(h)
    for e in range(N_EXPERTS):
        out = out + gate[..., e:e + 1] * swiglu(h, w1[e], w3[e], w2[e])
    return out


def trunk_layer(x, mod, lp, ctx):
    shift_a, scale_a, gate_a, shift_f, scale_f, gate_f = mod
    bsz, length, _ = x.shape
    h = rmsnorm(x, lp['norm_mix_g']) * (1 + scale_a) + shift_a
    split_at = [int(s) for s in np.cumsum(IN_WIDTHS)[:-1]]
    q, k, v, z, xbc, dt_f, dt_b, sc_b, sc_c, sc_h, gate_logits = jnp.split(h @ lp['w_in'], split_at, axis=-1)
    q = rmsnorm(q.reshape(bsz, length, N_HEADS, HEAD_DIM), lp['q_norm_g'])
    k = rmsnorm(k.reshape(bsz, length, N_KV_HEADS, HEAD_DIM), lp['k_norm_g'])
    v = v.reshape(bsz, length, N_KV_HEADS, HEAD_DIM)
    if ctx is None:
        k_all, v_all = k, v
        h0_f = jnp.zeros((bsz, SSM_HEADS, SSM_HEADDIM, SSM_STATE), jnp.float32)
        h0_b = h0_f
    else:
        k_ctx, v_ctx, h0_f, h0_b, (rows, cols) = ctx
        q = rope_2d(q, rows, cols)
        k = rope_2d(k, rows, cols)
        k_all = jnp.concatenate([k_ctx.astype(k.dtype), k], axis=1)
        v_all = jnp.concatenate([v_ctx.astype(v.dtype), v], axis=1)
    attn = blocked_attention(q.reshape(bsz, length, N_KV_HEADS, Q_PER_KV, HEAD_DIM), k_all, v_all)
    attn = attn @ lp['w_attn_out']
    ssm_y, h_f, h_b = ssd_branch(z, xbc, dt_f, dt_b, h0_f, h0_b, lp)
    ssm = ssm_y @ lp['w_ssm_out']
    sconv = (sc_b * conv_centred(sc_c * sc_h, lp['sconv_w'])) @ lp['w_sconv_out']
    g = jax.nn.sigmoid(gate_logits.reshape(bsz, length, N_BRANCHES, D_MODEL))
    merged = g[:, :, 0] * attn + g[:, :, 1] * ssm + g[:, :, 2] * sconv
    x = x + gate_a * (merged @ lp['w_merge'])
    h2 = rmsnorm(x, lp['norm_ffn_g']) * (1 + scale_f) + shift_f
    router, w1, w3, w2 = lp['ffn']
    f = swiglu(h2, w1, w3, w2) if router is None else moe_swiglu(h2, router, w1, w3, w2)
    x = x + gate_f * f
    return x, k, v, h_f, h_b


def setup_inputs(seed: int = 0) -> dict:
    key = jax.random.key(seed)
    keys = iter(jax.random.split(key, 48))

    def nrm(shape, scale):
        return jax.random.normal(next(keys), shape, jnp.float32) * scale

    def gain(shape):
        return 1.0 + nrm(shape, 0.02)

    n_dense = (DEPTH + 1) // 2
    n_moe = DEPTH // 2
    dsc = D_MODEL ** -0.5
    dt0 = jnp.exp(jax.random.uniform(next(keys), (DEPTH, 2, SSM_HEADS), jnp.float32,
                                     math.log(DT_MIN), math.log(DT_MAX)))
    dt_bias = dt0 + jnp.log(-jnp.expm1(-dt0))
    a_log = jnp.log(jax.random.uniform(next(keys), (DEPTH, 2, SSM_HEADS), jnp.float32, 1.0, 16.0))
    return {
        'x_prompt': nrm((BATCH, SEQ, D_MODEL), 1.0),
        'x_sample': nrm((DEC_BATCH, DEC_SEQ, D_MODEL), 1.0),
        'cache_k': nrm((DEC_BATCH, DEPTH, PAST_LEN, N_KV_HEADS, HEAD_DIM), 1.0),
        'cache_v': nrm((DEC_BATCH, DEPTH, PAST_LEN, N_KV_HEADS, HEAD_DIM), 1.0),
        'state_ssm_fwd': nrm((DEC_BATCH, DEPTH, SSM_HEADS, SSM_HEADDIM, SSM_STATE), 0.1),
        'state_ssm_bwd': nrm((DEC_BATCH, DEPTH, SSM_HEADS, SSM_HEADDIM, SSM_STATE), 0.1),
        'c': nrm((DEC_BATCH, D_MODEL), 1.0),
        'c_ctx': nrm((D_MODEL,), 1.0),
        'w_mod': nrm((DEPTH, D_MODEL, 6 * D_MODEL), 0.5 * dsc),
        'b_mod': nrm((DEPTH, 6 * D_MODEL), 0.01),
        'norm_mix_g': gain((DEPTH, D_MODEL)),
        'norm_ffn_g': gain((DEPTH, D_MODEL)),
        'w_in': nrm((DEPTH, D_MODEL, IN_COLS), dsc),
        'q_norm_g': gain((DEPTH, HEAD_DIM)),
        'k_norm_g': gain((DEPTH, HEAD_DIM)),
        'w_attn_out': nrm((DEPTH, N_HEADS * HEAD_DIM, D_MODEL), (N_HEADS * HEAD_DIM) ** -0.5),
        'ssm_conv_w': nrm((DEPTH, SSM_CONV, SSM_XBC), SSM_CONV ** -0.5),
        'ssm_conv_b': nrm((DEPTH, SSM_XBC), 0.01),
        'ssm_dt_bias': dt_bias,
        'ssm_a_log': a_log,
        'ssm_d': 1.0 + nrm((DEPTH, SSM_HEADS), 0.1),
        'ssm_norm_g': gain((DEPTH, SSM_INNER)),
        'w_ssm_out': nrm((DEPTH, SSM_INNER, D_MODEL), SSM_INNER ** -0.5),
        'sconv_w': nrm((DEPTH, SHORT_CONV, CONV_WIDTH), SHORT_CONV ** -0.5),
        'w_sconv_out': nrm((DEPTH, CONV_WIDTH, D_MODEL), CONV_WIDTH ** -0.5),
        'w_merge': nrm((DEPTH, D_MODEL, D_MODEL), dsc),
        'ffn_w1': nrm((n_dense, D_MODEL, D_FF), dsc),
        'ffn_w3': nrm((n_dense, D_MODEL, D_FF), dsc),
        'ffn_w2': nrm((n_dense, D_FF, D_MODEL), D_FF ** -0.5),
        'moe_router': nrm((n_moe, D_MODEL, N_EXPERTS), dsc),
        'moe_w1': nrm((n_moe, N_EXPERTS, D_MODEL, D_FF), dsc),
        'moe_w3': nrm((n_moe, N_EXPERTS, D_MODEL, D_FF), dsc),
        'moe_w2': nrm((n_moe, N_EXPERTS, D_FF, D_MODEL), D_FF ** -0.5),
    }


def reference(x_prompt, x_sample, cache_k, cache_v, state_ssm_fwd, state_ssm_bwd, c, c_ctx,
              w_mod, b_mod, norm_mix_g, norm_ffn_g, w_in, q_norm_g, k_norm_g, w_attn_out,
              ssm_conv_w, ssm_conv_b, ssm_dt_bias, ssm_a_log, ssm_d, ssm_norm_g, w_ssm_out,
              sconv_w, w_sconv_out, w_merge, ffn_w1, ffn_w3, ffn_w2,
              moe_router, moe_w1, moe_w3, moe_w2):
    n_rows = x_sample.shape[1] // GRID_W
    t = jnp.arange(n_rows * GRID_W)
    pos = (t // GRID_W, t % GRID_W)
    y_prompt = x_prompt
    y_sample = x_sample
    ks, vs, hfs, hbs = [], [], [], []
    for l in range(DEPTH):
        if l % 2 == 0:
            ffn = (None, ffn_w1[l // 2], ffn_w3[l // 2], ffn_w2[l // 2])
        else:
            ffn = (moe_router[l // 2], moe_w1[l // 2], moe_w3[l // 2], moe_w2[l // 2])
        lp = {
            'norm_mix_g': norm_mix_g[l], 'norm_ffn_g': norm_ffn_g[l], 'w_in': w_in[l],
            'q_norm_g': q_norm_g[l], 'k_norm_g': k_norm_g[l], 'w_attn_out': w_attn_out[l],
            'ssm_conv_w': ssm_conv_w[l], 'ssm_conv_b': ssm_conv_b[l], 'ssm_dt_bias': ssm_dt_bias[l],
            'ssm_a_log': ssm_a_log[l], 'ssm_d': ssm_d[l], 'ssm_norm_g': ssm_norm_g[l],
            'w_ssm_out': w_ssm_out[l], 'sconv_w': sconv_w[l], 'w_sconv_out': w_sconv_out[l],
            'w_merge': w_merge[l], 'ffn': ffn,
        }
        mod_ctx = adaln(c_ctx[None, :], w_mod[l], b_mod[l])
        y_prompt, k_l, v_l, hf_l, hb_l = trunk_layer(y_prompt, mod_ctx, lp, None)
        ks.append(k_l)
        vs.append(v_l)
        hfs.append(hf_l.astype(y_prompt.dtype))
        hbs.append(hb_l.astype(y_prompt.dtype))
        mod_lat = adaln(c, w_mod[l], b_mod[l])
        y_sample = trunk_layer(y_sample, mod_lat, lp,
                               (cache_k[:, l], cache_v[:, l], state_ssm_fwd[:, l], state_ssm_bwd[:, l], pos))[0]
    new_cache_k = jnp.stack(ks, axis=1)
    new_cache_v = jnp.stack(vs, axis=1)
    new_state_ssm_fwd = jnp.stack(hfs, axis=1)
    new_state_ssm_bwd = jnp.stack(hbs, axis=1)
    return (y_prompt, y_sample, new_cache_k, new_cache_v, new_state_ssm_fwd, new_state_ssm_bwd)
```

```python
import functools

import jax
import jax.numpy as jnp
from jax import lax
from jax.experimental import pallas as pl
from jax.experimental.pallas import tpu as pltpu

F32 = jnp.float32
BF16 = jnp.bfloat16
HIGHEST = lax.Precision.HIGHEST

D_MODEL = 1024
HEAD_DIM = 128
N_HEADS = 8
N_KV_HEADS = 2
Q_PER_KV = N_HEADS // N_KV_HEADS
ROPE_THETA = 10000.0
GRID_W = 64
SSM_HEADS = 16
SSM_HEADDIM = 64
SSM_STATE = 128
SSM_CHUNK = 128
D_FF = 2816
N_EXPERTS = 8
EPS = 1e-6

COL_Q, COL_Z, COL_SCB, COL_SCC, COL_SCH, COL_G = 0, 1024, 2048, 3072, 4096, 5120
COL_XS, COL_BC, COL_K, COL_V, COL_DT = 8192, 9216, 9728, 9984, 10240
N_PROJ = 10368
LANES = 128
SUBLANES = 8
MIB = 1024 * 1024


def _params(sem, vmem_mib):
    return pltpu.CompilerParams(dimension_semantics=sem, vmem_limit_bytes=vmem_mib * MIB)


def _sigmoid(t):
    return 1.0 / (1.0 + jnp.exp(-t))


def _silu(t):
    return t * _sigmoid(t)


def _rms(t):
    return t * lax.rsqrt(jnp.mean(t * t, axis=-1, keepdims=True) + EPS)


def _bdot(a, b):
    return jnp.dot(a.astype(BF16), b.astype(BF16), preferred_element_type=F32)


def _hdot(a, b):
    return jnp.dot(a, b, precision=HIGHEST, preferred_element_type=F32)


def _mod_kernel(c_ref, w_ref, b_ref, o_ref):
    o_ref[...] = _hdot(_silu(c_ref[...]), w_ref[...]) + b_ref[...]


def _modulation(cond8, w, b):
    n = w.shape[1] // D_MODEL
    return pl.pallas_call(
        _mod_kernel,
        out_shape=jax.ShapeDtypeStruct((SUBLANES, w.shape[1]), F32),
        grid=(n,),
        in_specs=[pl.BlockSpec((SUBLANES, D_MODEL), lambda j: (0, 0)),
                  pl.BlockSpec((D_MODEL, D_MODEL), lambda j: (0, j)),
                  pl.BlockSpec((1, D_MODEL), lambda j: (0, j))],
        out_specs=pl.BlockSpec((SUBLANES, D_MODEL), lambda j: (0, j)),
        compiler_params=_params(("parallel",), 32),
        name="modulation",
    )(cond8, w, b)


def _inproj_kernel(x_ref, mod_ref, g_ref, w_ref, o_ref, h_scr):
    @pl.when(pl.program_id(1) == 0)
    def _():
        h = _rms(x_ref[...]) * g_ref[...]
        h = h * (1.0 + mod_ref[0, 1:2, :]) + mod_ref[0, 0:1, :]
        h_scr[...] = h.astype(BF16)

    o_ref[...] = jnp.dot(h_scr[...], w_ref[...], preferred_element_type=F32)


def _inproj(x, mod, g, w, tm, tn):
    rows = x.shape[0]
    per_mod = rows // mod.shape[0]
    tm = min(tm, per_mod)
    return pl.pallas_call(
        _inproj_kernel,
        out_shape=jax.ShapeDtypeStruct((rows, N_PROJ), F32),
        grid=(rows // tm, N_PROJ // tn),
        in_specs=[pl.BlockSpec((tm, D_MODEL), lambda i, j: (i, 0)),
                  pl.BlockSpec((1, SUBLANES, D_MODEL), lambda i, j: (i * tm // per_mod, 0, 0)),
                  pl.BlockSpec((1, D_MODEL), lambda i, j: (0, 0)),
                  pl.BlockSpec((D_MODEL, tn), lambda i, j: (0, j))],
        out_specs=pl.BlockSpec((tm, tn), lambda i, j: (i, j)),
        scratch_shapes=[pltpu.VMEM((tm, D_MODEL), BF16)],
        compiler_params=_params(("parallel", "arbitrary"), 48),
        name="inproj",
    )(x, mod, g, w)


def _softmax_pv(s, vb):
    m = jnp.max(s, axis=-1, keepdims=True)
    p = jnp.exp(s - m)
    l = jnp.sum(p, axis=-1, keepdims=True)
    return jnp.dot(p.astype(BF16), vb, preferred_element_type=F32) / l


def _qk(qb, kb):
    return lax.dot_general(qb, kb, (((1,), (1,)), ((), ())), preferred_element_type=F32)


def _attn_ctx_kernel(q_ref, k_ref, v_ref, qg_ref, kg_ref, a_ref, ko_ref, vo_ref):
    scale = HEAD_DIM ** -0.5
    v = v_ref[...]
    vo_ref[...] = v
    for hk in range(N_KV_HEADS):
        ks = slice(hk * HEAD_DIM, (hk + 1) * HEAD_DIM)
        kn = _rms(k_ref[:, ks]) * kg_ref[...]
        ko_ref[:, ks] = kn
        knb = kn.astype(BF16)
        vb = v[:, ks].astype(BF16)
        for g in range(Q_PER_KV):
            hs = slice((hk * Q_PER_KV + g) * HEAD_DIM, (hk * Q_PER_KV + g + 1) * HEAD_DIM)
            qn = _rms(q_ref[:, hs]) * qg_ref[...]
            s = _qk(qn.astype(BF16), knb) * scale
            a_ref[:, hs] = _softmax_pv(s, vb).astype(BF16)


def _attn_ctx(proj, qg, kg, nseq, seq):
    kvw = N_KV_HEADS * HEAD_DIM
    return pl.pallas_call(
        _attn_ctx_kernel,
        out_shape=(jax.ShapeDtypeStruct((nseq * seq, D_MODEL), BF16),
                   jax.ShapeDtypeStruct((nseq, seq, kvw), F32),
                   jax.ShapeDtypeStruct((nseq, seq, kvw), F32)),
        grid=(nseq,),
        in_specs=[pl.BlockSpec((seq, D_MODEL), lambda b: (b, COL_Q // D_MODEL)),
                  pl.BlockSpec((seq, kvw), lambda b: (b, COL_K // kvw)),
                  pl.BlockSpec((seq, kvw), lambda b: (b, COL_V // kvw)),
                  pl.BlockSpec((1, HEAD_DIM), lambda b: (0, 0)),
                  pl.BlockSpec((1, HEAD_DIM), lambda b: (0, 0))],
        out_specs=(pl.BlockSpec((seq, D_MODEL), lambda b: (b, 0)),
                   pl.BlockSpec((None, seq, kvw), lambda b: (b, 0, 0)),
                   pl.BlockSpec((None, seq, kvw), lambda b: (b, 0, 0))),
        compiler_params=_params(("parallel",), 32),
        name="attn_ctx",
    )(proj, proj, proj, qg, kg)


def _rope(t, cos, sa, sb):
    return t * cos + pltpu.roll(t, 96, 1) * sa + pltpu.roll(t, 32, 1) * sb


def _prep_lat_kernel(q_ref, k_ref, v_ref, ck_ref, cv_ref, cos_ref, sa_ref, sb_ref, qg_ref, kg_ref,
                     qo_ref, ko_ref, vo_ref):
    t = pl.program_id(1)

    @pl.when(t == 0)
    def _():
        ko_ref[...] = ck_ref[...].astype(BF16)
        vo_ref[...] = cv_ref[...].astype(BF16)

    @pl.when(t > 0)
    def _():
        cos, sa, sb = cos_ref[...], sa_ref[...], sb_ref[...]
        vo_ref[...] = v_ref[...].astype(BF16)
        for h in range(N_KV_HEADS):
            hs = slice(h * HEAD_DIM, (h + 1) * HEAD_DIM)
            ko_ref[:, hs] = _rope(_rms(k_ref[:, hs]) * kg_ref[...], cos, sa, sb).astype(BF16)
        for h in range(N_HEADS):
            hs = slice(h * HEAD_DIM, (h + 1) * HEAD_DIM)
            qo_ref[:, hs] = _rope(_rms(q_ref[:, hs]) * qg_ref[...], cos, sa, sb).astype(BF16)


def _prep_lat(proj, cache_k, cache_v, cos, sa, sb, qg, kg, nb, seq):
    past = cache_k.shape[1]
    tr = past
    kvw = N_KV_HEADS * HEAD_DIM
    nt = seq // tr
    prev = lambda t: jnp.maximum(t - 1, 0)
    return pl.pallas_call(
        _prep_lat_kernel,
        out_shape=(jax.ShapeDtypeStruct((nb, seq, D_MODEL), BF16),
                   jax.ShapeDtypeStruct((nb, past + seq, kvw), BF16),
                   jax.ShapeDtypeStruct((nb, past + seq, kvw), BF16)),
        grid=(nb, nt + 1),
        in_specs=[pl.BlockSpec((tr, D_MODEL), lambda b, t: (b * nt + prev(t), COL_Q // D_MODEL)),
                  pl.BlockSpec((tr, kvw), lambda b, t: (b * nt + prev(t), COL_K // kvw)),
                  pl.BlockSpec((tr, kvw), lambda b, t: (b * nt + prev(t), COL_V // kvw)),
                  pl.BlockSpec((None, past, kvw), lambda b, t: (b, 0, 0)),
                  pl.BlockSpec((None, past, kvw), lambda b, t: (b, 0, 0)),
                  pl.BlockSpec((tr, HEAD_DIM), lambda b, t: (prev(t), 0)),
                  pl.BlockSpec((tr, HEAD_DIM), lambda b, t: (prev(t), 0)),
                  pl.BlockSpec((tr, HEAD_DIM), lambda b, t: (prev(t), 0)),
                  pl.BlockSpec((1, HEAD_DIM), lambda b, t: (0, 0)),
                  pl.BlockSpec((1, HEAD_DIM), lambda b, t: (0, 0))],
        out_specs=(pl.BlockSpec((None, tr, D_MODEL), lambda b, t: (b, prev(t), 0)),
                   pl.BlockSpec((None, tr, kvw), lambda b, t: (b, t, 0)),
                   pl.BlockSpec((None, tr, kvw), lambda b, t: (b, t, 0))),
        compiler_params=_params(("parallel", "arbitrary"), 32),
        name="prep_lat",
    )(proj, proj, proj, cache_k, cache_v, cos, sa, sb, qg, kg)


def _attn_lat_kernel(q_ref, k_ref, v_ref, o_ref):
    scale = HEAD_DIM ** -0.5
    kb = k_ref[...]
    vb = v_ref[...]
    for g in range(Q_PER_KV):
        hs = slice(g * HEAD_DIM, (g + 1) * HEAD_DIM)
        s = _qk(q_ref[:, hs], kb) * scale
        o_ref[:, hs] = _softmax_pv(s, vb).astype(BF16)


def _attn_lat(qr, kall, vall, tq):
    nb, seq, _ = qr.shape
    nk = kall.shape[1]
    gw = Q_PER_KV * HEAD_DIM
    return pl.pallas_call(
        _attn_lat_kernel,
        out_shape=jax.ShapeDtypeStruct((nb, seq, D_MODEL), BF16),
        grid=(nb, N_KV_HEADS, seq // tq),
        in_specs=[pl.BlockSpec((None, tq, gw), lambda b, h, i: (b, i, h)),
                  pl.BlockSpec((None, nk, HEAD_DIM), lambda b, h, i: (b, 0, h)),
                  pl.BlockSpec((None, nk, HEAD_DIM), lambda b, h, i: (b, 0, h))],
        out_specs=pl.BlockSpec((None, tq, gw), lambda b, h, i: (b, i, h)),
        compiler_params=_params(("parallel", "parallel", "arbitrary"), 48),
        name="attn_lat",
    )(qr, kall, vall)


def _conv3_rows(x, prev_row, next_row, w, tm):
    r = lax.broadcasted_iota(jnp.int32, (tm, 1), 0)
    xm1 = jnp.where(r == 0, prev_row, pltpu.roll(x, 1, 0))
    xp1 = jnp.where(r == tm - 1, next_row, pltpu.roll(x, tm - 1, 0))
    return w[0:1, :] * xm1 + w[1:2, :] * x + w[2:3, :] * xp1


def _ssd_kernel(*refs, reverse, has_h0, nc):
    it = iter(refs)
    xs_ref, bc_ref, dt_ref, xsp_ref, xsn_ref, bcp_ref, bcn_ref = (next(it) for _ in range(7))
    cwx_ref, cbx_ref, cwb_ref, cbb_ref, alog_ref, dtb_ref, e_ref = (next(it) for _ in range(7))
    h0_ref = next(it) if has_h0 else None
    if reverse:
        z_ref, yf_ref, d_ref, g_ref = (next(it) for _ in range(4))
    y_ref, hout_ref, ht_scr = next(it), next(it), next(it)

    q = SSM_CHUNK
    npair = SSM_HEADS // 2
    c = pl.program_id(1)
    cc = nc - 1 - c if reverse else c
    first = cc == 0
    last = cc == nc - 1

    @pl.when(c == 0)
    def _():
        for p in range(npair):
            if has_h0:
                ht_scr[p] = h0_ref[p].T
            else:
                ht_scr[p] = jnp.zeros((q, q), F32)

    def conv_silu(x_ref_, p_ref_, n_ref_, w_ref_, b_ref_):
        pr = jnp.where(first, 0.0, p_ref_[SUBLANES - 1:SUBLANES, :])
        nx = jnp.where(last, 0.0, n_ref_[0:1, :])
        return _silu(_conv3_rows(x_ref_[...], pr, nx, w_ref_[...], q) + b_ref_[...])

    xs = conv_silu(xs_ref, xsp_ref, xsn_ref, cwx_ref, cbx_ref)
    bc = conv_silu(bc_ref, bcp_ref, bcn_ref, cwb_ref, cbb_ref)

    off = SSM_HEADS if reverse else 0
    dt = jax.nn.softplus(dt_ref[...] + dtb_ref[...])
    da = dt * (-jnp.exp(alog_ref[...]))
    ii = lax.broadcasted_iota(jnp.int32, (q, q), 0)
    jj = lax.broadcasted_iota(jnp.int32, (q, q), 1)
    tri = (jj >= ii) if reverse else (ii >= jj)
    tri_f = tri.astype(F32)
    tri_t = ((ii >= jj) if reverse else (jj >= ii)).astype(F32)
    acs = _hdot(tri_f, da)
    acs_t = _hdot(da.T, tri_t)
    e = e_ref[...]
    acs_x = _hdot(acs, e)
    tot_x = acs_x[0:1, :] if reverse else acs_x[q - 1:q, :]
    xdt = xs * _hdot(dt, e)
    xw = xdt * jnp.exp(tot_x - acs_x)
    eacs_x = jnp.exp(acs_x)
    etot_x = jnp.exp(tot_x)
    lo = lax.broadcasted_iota(jnp.int32, (q, LANES), 1) < SSM_HEADDIM

    parts = []
    for g in range(2):
        bg = bc[:, g * q:(g + 1) * q]
        cg = bc[:, (2 + g) * q:(3 + g) * q].astype(BF16)
        cb = _qk(cg, bg.astype(BF16))
        bgt = bg.T.astype(BF16)
        for pp in range(npair // 2):
            p = g * (npair // 2) + pp
            ps = slice(p * LANES, (p + 1) * LANES)
            ms = []
            for hh in range(2):
                ln = off + 2 * p + hh
                dec = jnp.where(tri, jnp.exp(acs[:, ln:ln + 1] - acs_t[ln:ln + 1, :]), 0.0)
                ms.append((cb * dec).astype(BF16))
            m = jnp.concatenate(ms, axis=1)
            xp = xdt[:, ps]
            bd = jnp.concatenate([jnp.where(lo, xp, 0.0), jnp.where(lo, 0.0, xp)], axis=0).astype(BF16)
            ht = ht_scr[p]
            yp = jnp.dot(m, bd, preferred_element_type=F32)
            yp = yp + jnp.dot(cg, ht.astype(BF16), preferred_element_type=F32) * eacs_x[:, ps]
            ht_scr[p] = etot_x[:, ps] * ht + jnp.dot(bgt, xw[:, ps].astype(BF16), preferred_element_type=F32)
            parts.append(yp)
    y = jnp.concatenate(parts, axis=1)

    if reverse:
        y = y + yf_ref[...] + d_ref[...] * xs
        y = y * _silu(z_ref[...])
        y_ref[...] = (_rms(y) * g_ref[...]).astype(y_ref.dtype)
    else:
        y_ref[...] = y

    @pl.when(c == nc - 1)
    def _():
        for p in range(npair):
            hout_ref[p] = ht_scr[p].T


def _ssd_pass(proj, lp, d, nseq, seq, h0, yf):
    reverse = yf is not None
    has_h0 = h0 is not None
    q = SSM_CHUNK
    nc = seq // q
    rows = nseq * seq
    npair = SSM_HEADS // 2
    hb = q // SUBLANES
    bcw = 4 * SSM_STATE
    chunk = (lambda c: nc - 1 - c) if reverse else (lambda c: c)
    blk = lambda s, c: s * nc + chunk(c)
    prev8 = lambda s, c: jnp.maximum(blk(s, c) * hb - 1, 0)
    next8 = lambda s, c: jnp.minimum((blk(s, c) + 1) * hb, rows // SUBLANES - 1)
    const = lambda s, c: (0, 0)
    in_specs = [
        pl.BlockSpec((q, D_MODEL), lambda s, c: (blk(s, c), COL_XS // D_MODEL)),
        pl.BlockSpec((q, bcw), lambda s, c: (blk(s, c), COL_BC // bcw)),
        pl.BlockSpec((q, LANES), lambda s, c: (blk(s, c), COL_DT // LANES)),
        pl.BlockSpec((SUBLANES, D_MODEL), lambda s, c: (prev8(s, c), COL_XS // D_MODEL)),
        pl.BlockSpec((SUBLANES, D_MODEL), lambda s, c: (next8(s, c), COL_XS // D_MODEL)),
        pl.BlockSpec((SUBLANES, bcw), lambda s, c: (prev8(s, c), COL_BC // bcw)),
        pl.BlockSpec((SUBLANES, bcw), lambda s, c: (next8(s, c), COL_BC // bcw)),
        pl.BlockSpec((3, D_MODEL), const), pl.BlockSpec((1, D_MODEL), const),
        pl.BlockSpec((3, bcw), const), pl.BlockSpec((1, bcw), const),
        pl.BlockSpec((1, LANES), const), pl.BlockSpec((1, LANES), const),
        pl.BlockSpec((LANES, D_MODEL), const),
    ]
    args = [proj] * 7 + [lp["cw_x"], lp["cb_x"], lp["cw_bc"], lp["cb_bc"], lp["alog"][d], lp["dtb"][d],
                         lp["expand"][d]]
    if has_h0:
        in_specs.append(pl.BlockSpec((None, npair, q, q), lambda s, c: (s, 0, 0, 0)))
        args.append(h0)
    if reverse:
        in_specs += [pl.BlockSpec((q, D_MODEL), lambda s, c: (blk(s, c), COL_Z // D_MODEL)),
                     pl.BlockSpec((q, D_MODEL), lambda s, c: (blk(s, c), 0)),
                     pl.BlockSpec((1, D_MODEL), const), pl.BlockSpec((1, D_MODEL), const)]
        args += [proj, yf, lp["d_row"], lp["ssm_norm_g"]]
    return pl.pallas_call(
        functools.partial(_ssd_kernel, reverse=reverse, has_h0=has_h0, nc=nc),
        out_shape=(jax.ShapeDtypeStruct((rows, D_MODEL), BF16 if reverse else F32),
                   jax.ShapeDtypeStruct((nseq, npair, q, q), F32)),
        grid=(nseq, nc),
        in_specs=in_specs,
        out_specs=(pl.BlockSpec((q, D_MODEL), lambda s, c: (blk(s, c), 0)),
                   pl.BlockSpec((None, npair, q, q), lambda s, c: (s, 0, 0, 0))),
        scratch_shapes=[pltpu.VMEM((npair, q, q), F32)],
        compiler_params=_params(("parallel", "arbitrary"), 32),
        name="ssd_bwd" if reverse else "ssd_fwd",
    )(*args)


def _top2_gate(logits):
    lane = lax.broadcasted_iota(jnp.int32, logits.shape, 1).astype(F32)
    lg = jnp.where(lane < N_EXPERTS, logits, -jnp.inf)
    m1 = jnp.max(lg, axis=-1, keepdims=True)
    i1 = jnp.min(jnp.where(lg == m1, lane, float(LANES)), axis=-1, keepdims=True)
    lg2 = jnp.where(lane == i1, -jnp.inf, lg)
    m2 = jnp.max(lg2, axis=-1, keepdims=True)
    i2 = jnp.min(jnp.where(lg2 == m2, lane, float(LANES)), axis=-1, keepdims=True)
    e = jnp.exp(m2 - m1)
    p1 = 1.0 / (1.0 + e)
    return jnp.where(lane == i1, p1, jnp.where(lane == i2, e * p1, 0.0))


def _merge_kernel(*refs, seq, tm, moe):
    it = iter(refs)
    attn_ref, ssm_ref, scb_ref, scc_ref, sch_ref = (next(it) for _ in range(5))
    sccp_ref, sccn_ref, schp_ref, schn_ref = (next(it) for _ in range(4))
    g0_ref, g1_ref, g2_ref, x_ref, mod_ref = (next(it) for _ in range(5))
    wa_ref, ws_ref, wc_ref, wm_ref, scw_ref, ng_ref = (next(it) for _ in range(6))
    router_ref = next(it) if moe else None
    xo_ref, h2_ref = next(it), next(it)
    gate_ref = next(it) if moe else None

    i = pl.program_id(0)
    pos = (lax.broadcasted_iota(jnp.int32, (tm, 1), 0) + i * tm) & (seq - 1)
    u = scc_ref[...] * sch_ref[...]
    up = sccp_ref[SUBLANES - 1:SUBLANES, :] * schp_ref[SUBLANES - 1:SUBLANES, :]
    un = sccn_ref[0:1, :] * schn_ref[0:1, :]
    r = lax.broadcasted_iota(jnp.int32, (tm, 1), 0)
    um1 = jnp.where(pos == 0, 0.0, jnp.where(r == 0, up, pltpu.roll(u, 1, 0)))
    up1 = jnp.where(pos == seq - 1, 0.0, jnp.where(r == tm - 1, un, pltpu.roll(u, tm - 1, 0)))
    w = scw_ref[...]
    sc = scb_ref[...] * (w[0:1, :] * um1 + w[1:2, :] * u + w[2:3, :] * up1)

    merged = _sigmoid(g0_ref[...]) * jnp.dot(attn_ref[...], wa_ref[...], preferred_element_type=F32)
    merged = merged + _sigmoid(g1_ref[...]) * jnp.dot(ssm_ref[...], ws_ref[...], preferred_element_type=F32)
    merged = merged + _sigmoid(g2_ref[...]) * jnp.dot(sc.astype(BF16), wc_ref[...], preferred_element_type=F32)
    mod = mod_ref[0]
    xn = x_ref[...] + mod[2:3, :] * jnp.dot(merged.astype(BF16), wm_ref[...], preferred_element_type=F32)
    xo_ref[...] = xn
    h2 = _rms(xn) * ng_ref[...] * (1.0 + mod[4:5, :]) + mod[3:4, :]
    h2_ref[...] = h2.astype(BF16)
    if moe:
        gate_ref[...] = _top2_gate(_hdot(h2, router_ref[...]))


def _merge(attn, ssm, proj, x, mod, lp, seq, tm):
    rows = x.shape[0]
    per_mod = rows // mod.shape[0]
    tm = min(tm, per_mod)
    moe = lp["router"] is not None
    hb = tm // SUBLANES
    prev8 = lambda i: jnp.maximum(i * hb - 1, 0)
    next8 = lambda i: jnp.minimum((i + 1) * hb, rows // SUBLANES - 1)
    tile = lambda col: pl.BlockSpec((tm, D_MODEL), lambda i: (i, col // D_MODEL))
    const = lambda shape: pl.BlockSpec(shape, lambda i: (0, 0))
    in_specs = [tile(0), tile(0), tile(COL_SCB), tile(COL_SCC), tile(COL_SCH),
                pl.BlockSpec((SUBLANES, D_MODEL), lambda i: (prev8(i), COL_SCC // D_MODEL)),
                pl.BlockSpec((SUBLANES, D_MODEL), lambda i: (next8(i), COL_SCC // D_MODEL)),
                pl.BlockSpec((SUBLANES, D_MODEL), lambda i: (prev8(i), COL_SCH // D_MODEL)),
                pl.BlockSpec((SUBLANES, D_MODEL), lambda i: (next8(i), COL_SCH // D_MODEL)),
                tile(COL_G), tile(COL_G + D_MODEL), tile(COL_G + 2 * D_MODEL), tile(0),
                pl.BlockSpec((1, SUBLANES, D_MODEL), lambda i: (i * tm // per_mod, 0, 0)),
                const((D_MODEL, D_MODEL)), const((D_MODEL, D_MODEL)), const((D_MODEL, D_MODEL)),
                const((D_MODEL, D_MODEL)), const((3, D_MODEL)), const((1, D_MODEL))]
    args = [attn, ssm, proj, proj, proj, proj, proj, proj, proj, proj, proj, proj, x, mod,
            lp["w_attn_out"], lp["w_ssm_out"], lp["w_sconv_out"], lp["w_merge"], lp["sconv_w"], lp["norm_ffn_g"]]
    out_shape = [jax.ShapeDtypeStruct((rows, D_MODEL), F32), jax.ShapeDtypeStruct((rows, D_MODEL), BF16)]
    out_specs = [tile(0), tile(0)]
    if moe:
        in_specs.append(const((D_MODEL, LANES)))
        args.append(lp["router"])
        out_shape.append(jax.ShapeDtypeStruct((rows, LANES), F32))
        out_specs.append(pl.BlockSpec((tm, LANES), lambda i: (i, 0)))
    return pl.pallas_call(
        functools.partial(_merge_kernel, seq=seq, tm=tm, moe=moe),
        out_shape=tuple(out_shape),
        grid=(rows // tm,),
        in_specs=in_specs,
        out_specs=tuple(out_specs),
        compiler_params=_params(("parallel",), 56),
        name="merge",
    )(*args)


def _ffn_kernel(*refs, moe):
    it = iter(refs)
    h_ref, x_ref, mod_ref = next(it), next(it), next(it)
    gate_ref = next(it) if moe else None
    w1_ref, w3_ref, w2_ref, o_ref, acc = (next(it) for _ in range(5))
    f = pl.program_id(2)
    e = pl.program_id(1)

    @pl.when((f == 0) & (e == 0))
    def _():
        acc[...] = jnp.zeros(acc.shape, F32)

    h = h_ref[...]
    a1 = jnp.dot(h, w1_ref[...], preferred_element_type=F32)
    a3 = jnp.dot(h, w3_ref[...], preferred_element_type=F32)
    a = _silu(a1) * a3
    if moe:
        lane = lax.broadcasted_iota(jnp.int32, gate_ref.shape, 1)
        a = a * jnp.sum(jnp.where(lane == e, gate_ref[...], 0.0), axis=-1, keepdims=True)
    acc[...] += jnp.dot(a.astype(BF16), w2_ref[...], preferred_element_type=F32)

    @pl.when((f == pl.num_programs(2) - 1) & (e == pl.num_programs(1) - 1))
    def _():
        o_ref[...] = x_ref[...] + mod_ref[0, 5:6, :] * acc[...]


def _ffn(h2, x, mod, gate, w1, w3, w2, tm, tf):
    rows = x.shape[0]
    per_mod = rows // mod.shape[0]
    tm = min(tm, per_mod)
    moe = gate is not None
    ne = w1.shape[0]
    in_specs = [pl.BlockSpec((tm, D_MODEL), lambda i, e, f: (i, 0)),
                pl.BlockSpec((tm, D_MODEL), lambda i, e, f: (i, 0)),
                pl.BlockSpec((1, SUBLANES, D_MODEL), lambda i, e, f: (i * tm // per_mod, 0, 0))]
    args = [h2, x, mod]
    if moe:
        in_specs.append(pl.BlockSpec((tm, LANES), lambda i, e, f: (i, 0)))
        args.append(gate)
    in_specs += [pl.BlockSpec((None, D_MODEL, tf), lambda i, e, f: (e, 0, f)),
                 pl.BlockSpec((None, D_MODEL, tf), lambda i, e, f: (e, 0, f)),
                 pl.BlockSpec((None, tf, D_MODEL), lambda i, e, f: (e, f, 0))]
    args += [w1, w3, w2]
    return pl.pallas_call(
        functools.partial(_ffn_kernel, moe=moe),
        out_shape=jax.ShapeDtypeStruct((rows, D_MODEL), F32),
        grid=(rows // tm, ne, D_FF // tf),
        in_specs=in_specs,
        out_specs=pl.BlockSpec((tm, D_MODEL), lambda i, e, f: (i, 0)),
        scratch_shapes=[pltpu.VMEM((tm, D_MODEL), F32)],
        compiler_params=_params(("parallel", "arbitrary", "arbitrary"), 56),
        name="moe" if moe else "ffn",
    )(*args)


def _layer_params(l, w_in, q_norm_g, k_norm_g, w_attn_out, ssm_conv_w, ssm_conv_b, ssm_dt_bias, ssm_a_log,
                  ssm_d, ssm_norm_g, w_ssm_out, sconv_w, w_sconv_out, w_merge, norm_mix_g, norm_ffn_g,
                  ffn_w1, ffn_w3, ffn_w2, moe_router, moe_w1, moe_w3, moe_w2):
    w = w_in[l]
    seg = lambda a, n: w[:, a:a + n]
    w_perm = jnp.concatenate(
        [seg(0, 1024), seg(1536, 1024), seg(4128, 1024), seg(5152, 1024), seg(6176, 1024), seg(7200, 3072),
         seg(2560, 1024), seg(3584, 512), seg(1024, 256), seg(1280, 256), seg(4096, 32),
         jnp.zeros((D_MODEL, N_PROJ - COL_DT - 2 * SSM_HEADS), w.dtype)], axis=1).astype(BF16)
    row = lambda v: v.reshape(1, -1)
    lane_row = lambda v, off: jnp.zeros((1, LANES), F32).at[0, off:off + SSM_HEADS].set(v)
    head_of_col = jnp.arange(D_MODEL) // SSM_HEADDIM
    expand = [(jnp.arange(LANES)[:, None] == head_of_col[None, :] + off).astype(F32) for off in (0, SSM_HEADS)]
    lp = {
        "w_in": w_perm,
        "norm_mix_g": row(norm_mix_g[l]), "norm_ffn_g": row(norm_ffn_g[l]),
        "q_norm_g": row(q_norm_g[l]), "k_norm_g": row(k_norm_g[l]),
        "w_attn_out": w_attn_out[l].astype(BF16), "w_ssm_out": w_ssm_out[l].astype(BF16),
        "w_sconv_out": w_sconv_out[l].astype(BF16), "w_merge": w_merge[l].astype(BF16),
        "cw_x": ssm_conv_w[l][:, :D_MODEL], "cw_bc": ssm_conv_w[l][:, D_MODEL:],
        "cb_x": row(ssm_conv_b[l][:D_MODEL]), "cb_bc": row(ssm_conv_b[l][D_MODEL:]),
        "alog": [lane_row(ssm_a_log[l, d], d * SSM_HEADS) for d in range(2)],
        "dtb": [lane_row(ssm_dt_bias[l, d], d * SSM_HEADS) for d in range(2)],
        "expand": expand,
        "d_row": row(jnp.repeat(ssm_d[l], SSM_HEADDIM)),
        "ssm_norm_g": row(ssm_norm_g[l]),
        "sconv_w": sconv_w[l],
    }
    if l % 2 == 0:
        lp["router"] = None
        lp["ffn"] = tuple(t[l // 2][None].astype(BF16) for t in (ffn_w1, ffn_w3, ffn_w2))
    else:
        lp["router"] = jnp.pad(moe_router[l // 2], ((0, 0), (0, LANES - N_EXPERTS)))
        lp["ffn"] = tuple(t[l // 2].astype(BF16) for t in (moe_w1, moe_w3, moe_w2))
    return lp


def _rope_tables(seq):
    t = jnp.arange(seq)
    n = HEAD_DIM // 4
    inv = ROPE_THETA ** (-jnp.arange(n, dtype=F32) / n)
    ang_r = (t // GRID_W).astype(F32)[:, None] * inv[None, :]
    ang_c = (t % GRID_W).astype(F32)[:, None] * inv[None, :]
    zero = jnp.zeros_like(ang_r)
    cos = jnp.concatenate([jnp.cos(ang_r)] * 2 + [jnp.cos(ang_c)] * 2, axis=1)
    sa = jnp.concatenate([-jnp.sin(ang_r), zero, -jnp.sin(ang_c), zero], axis=1)
    sb = jnp.concatenate([zero, jnp.sin(ang_r), zero, jnp.sin(ang_c)], axis=1)
    return cos, sa, sb


def _trunk(x, mod, lp, nseq, seq, lat):
    rows = nseq * seq
    proj = _inproj(x, mod, lp["norm_mix_g"], lp["w_in"], tm=min(1024, rows), tn=1152)
    if lat is None:
        attn, k_new, v_new = _attn_ctx(proj, lp["q_norm_g"], lp["k_norm_g"], nseq, seq)
        h0_f = h0_b = None
    else:
        cache_k, cache_v, h0_f, h0_b, tables = lat
        qr, kall, vall = _prep_lat(proj, cache_k, cache_v, *tables, lp["q_norm_g"], lp["k_norm_g"], nseq, seq)
        attn = _attn_lat(qr, kall, vall, tq=256).reshape(rows, D_MODEL)
        k_new = v_new = None
    yf, h_f = _ssd_pass(proj, lp, 0, nseq, seq, h0_f, None)
    ssm, h_b = _ssd_pass(proj, lp, 1, nseq, seq, h0_b, yf)
    outs = _merge(attn, ssm, proj, x, mod, lp, seq, tm=256)
    gate = outs[2] if lp["router"] is not None else None
    w1, w3, w2 = lp["ffn"]
    x = _ffn(outs[1], outs[0], mod, gate, w1, w3, w2, tm=min(512, rows), tf=1408)
    return x, k_new, v_new, h_f, h_b


def kernel(x_prompt, x_sample, cache_k, cache_v, state_ssm_fwd, state_ssm_bwd, c, c_ctx, w_mod, b_mod, norm_mix_g, norm_ffn_g, w_in, q_norm_g, k_norm_g, w_attn_out, ssm_conv_w, ssm_conv_b, ssm_dt_bias, ssm_a_log, ssm_d, ssm_norm_g, w_ssm_out, sconv_w, w_sconv_out, w_merge, ffn_w1, ffn_w3, ffn_w2, moe_router, moe_w1, moe_w3, moe_w2):
    batch, seq, _ = x_prompt.shape
    nb, dseq, _ = x_sample.shape
    depth = w_in.shape[0]
    past = cache_k.shape[2]
    kvw = N_KV_HEADS * HEAD_DIM
    npair = SSM_HEADS // 2
    assert nb + 1 <= SUBLANES

    cond = jnp.zeros((SUBLANES, D_MODEL), F32).at[0].set(c_ctx).at[1:1 + nb].set(c)
    tables = _rope_tables(dseq)
    y_ctx = x_prompt.reshape(batch * seq, D_MODEL)
    y_lat = x_sample.reshape(nb * dseq, D_MODEL)
    ks, vs, hfs, hbs = [], [], [], []
    for l in range(depth):
        lp = _layer_params(l, w_in, q_norm_g, k_norm_g, w_attn_out, ssm_conv_w, ssm_conv_b, ssm_dt_bias,
                           ssm_a_log, ssm_d, ssm_norm_g, w_ssm_out, sconv_w, w_sconv_out, w_merge,
                           norm_mix_g, norm_ffn_g, ffn_w1, ffn_w3, ffn_w2, moe_router, moe_w1, moe_w3, moe_w2)
        m = _modulation(cond, w_mod[l], b_mod[l].reshape(1, -1)).reshape(SUBLANES, 6, D_MODEL)
        m = jnp.pad(m, ((0, 0), (0, SUBLANES - 6), (0, 0)))
        y_ctx, k_l, v_l, hf_l, hb_l = _trunk(y_ctx, m[0:1], lp, batch, seq, None)
        ks.append(k_l.reshape(batch, seq, N_KV_HEADS, HEAD_DIM))
        vs.append(v_l.reshape(batch, seq, N_KV_HEADS, HEAD_DIM))
        hfs.append(hf_l.reshape(batch, SSM_HEADS, SSM_HEADDIM, SSM_STATE))
        hbs.append(hb_l.reshape(batch, SSM_HEADS, SSM_HEADDIM, SSM_STATE))
        lat = (cache_k[:, l].reshape(nb, past, kvw), cache_v[:, l].reshape(nb, past, kvw),
               state_ssm_fwd[:, l].reshape(nb, npair, SSM_CHUNK, SSM_STATE),
               state_ssm_bwd[:, l].reshape(nb, npair, SSM_CHUNK, SSM_STATE), tables)
        y_lat = _trunk(y_lat, m[1:1 + nb], lp, nb, dseq, lat)[0]
    return (y_ctx.reshape(batch, seq, D_MODEL), y_lat.reshape(nb, dseq, D_MODEL),
            jnp.stack(ks, axis=1), jnp.stack(vs, axis=1), jnp.stack(hfs, axis=1), jnp.stack(hbs, axis=1))
```

```python
import functools

import jax
import jax.numpy as jnp
from jax import lax
from jax.experimental import pallas as pl
from jax.experimental.pallas import tpu as pltpu
from jax.experimental.pallas import tpu_sc as plsc

F32 = jnp.float32
BF16 = jnp.bfloat16
HIGHEST = lax.Precision.HIGHEST

D_MODEL = 1024
HEAD_DIM = 128
N_HEADS = 8
N_KV_HEADS = 2
Q_PER_KV = N_HEADS // N_KV_HEADS
ROPE_THETA = 10000.0
GRID_W = 64
SSM_HEADS = 16
SSM_HEADDIM = 64
SSM_STATE = 128
SSM_CHUNK = 128
D_FF = 2816
N_EXPERTS = 8
EPS = 1e-6

COL_Q, COL_Z, COL_SCB, COL_SCC, COL_SCH, COL_G = 0, 1024, 2048, 3072, 4096, 5120
COL_XS, COL_BC, COL_K, COL_V, COL_DT = 8192, 9216, 9728, 9984, 10240
N_PROJ = 10368
LANES = 128
SUBLANES = 8
MIB = 1024 * 1024


def _params(sem, vmem_mib):
    return pltpu.CompilerParams(dimension_semantics=sem, vmem_limit_bytes=vmem_mib * MIB)


def _sigmoid(t):
    return 1.0 / (1.0 + jnp.exp(-t))


def _silu(t):
    return t * _sigmoid(t)


def _rms(t):
    return t * lax.rsqrt(jnp.mean(t * t, axis=-1, keepdims=True) + EPS)


def _bdot(a, b):
    return jnp.dot(a.astype(BF16), b.astype(BF16), preferred_element_type=F32)


def _hdot(a, b):
    return jnp.dot(a, b, precision=HIGHEST, preferred_element_type=F32)


def _mod_kernel(c_ref, w_ref, b_ref, o_ref):
    o_ref[...] = _hdot(_silu(c_ref[...]), w_ref[...]) + b_ref[...]


def _modulation(cond8, w, b):
    n = w.shape[1] // D_MODEL
    return pl.pallas_call(
        _mod_kernel,
        out_shape=jax.ShapeDtypeStruct((SUBLANES, w.shape[1]), F32),
        grid=(n,),
        in_specs=[pl.BlockSpec((SUBLANES, D_MODEL), lambda j: (0, 0)),
                  pl.BlockSpec((D_MODEL, D_MODEL), lambda j: (0, j)),
                  pl.BlockSpec((1, D_MODEL), lambda j: (0, j))],
        out_specs=pl.BlockSpec((SUBLANES, D_MODEL), lambda j: (0, j)),
        compiler_params=_params(("parallel",), 32),
        name="modulation",
    )(cond8, w, b)


def _inproj_kernel(x_ref, mod_ref, g_ref, w_ref, o_ref, h_scr):
    @pl.when(pl.program_id(1) == 0)
    def _():
        h = _rms(x_ref[...]) * g_ref[...]
        h = h * (1.0 + mod_ref[0, 1:2, :]) + mod_ref[0, 0:1, :]
        h_scr[...] = h.astype(BF16)

    o_ref[...] = jnp.dot(h_scr[...], w_ref[...], preferred_element_type=F32)


def _inproj(x, mod, g, w, tm, tn):
    rows = x.shape[0]
    per_mod = rows // mod.shape[0]
    tm = min(tm, per_mod)
    return pl.pallas_call(
        _inproj_kernel,
        out_shape=jax.ShapeDtypeStruct((rows, N_PROJ), F32),
        grid=(rows // tm, N_PROJ // tn),
        in_specs=[pl.BlockSpec((tm, D_MODEL), lambda i, j: (i, 0)),
                  pl.BlockSpec((1, SUBLANES, D_MODEL), lambda i, j: (i * tm // per_mod, 0, 0)),
                  pl.BlockSpec((1, D_MODEL), lambda i, j: (0, 0)),
                  pl.BlockSpec((D_MODEL, tn), lambda i, j: (0, j))],
        out_specs=pl.BlockSpec((tm, tn), lambda i, j: (i, j)),
        scratch_shapes=[pltpu.VMEM((tm, D_MODEL), BF16)],
        compiler_params=_params(("parallel", "arbitrary"), 48),
        name="inproj",
    )(x, mod, g, w)


def _softmax_pv(s, vb):
    m = jnp.max(s, axis=-1, keepdims=True)
    p = jnp.exp(s - m)
    l = jnp.sum(p, axis=-1, keepdims=True)
    return jnp.dot(p.astype(BF16), vb, preferred_element_type=F32) / l


def _qk(qb, kb):
    return lax.dot_general(qb, kb, (((1,), (1,)), ((), ())), preferred_element_type=F32)


def _attn_ctx_kernel(q_ref, k_ref, v_ref, qg_ref, kg_ref, a_ref, ko_ref, vo_ref):
    scale = HEAD_DIM ** -0.5
    v = v_ref[...]
    vo_ref[...] = v
    for hk in range(N_KV_HEADS):
        ks = slice(hk * HEAD_DIM, (hk + 1) * HEAD_DIM)
        kn = _rms(k_ref[:, ks]) * kg_ref[...]
        ko_ref[:, ks] = kn
        knb = kn.astype(BF16)
        vb = v[:, ks].astype(BF16)
        for g in range(Q_PER_KV):
            hs = slice((hk * Q_PER_KV + g) * HEAD_DIM, (hk * Q_PER_KV + g + 1) * HEAD_DIM)
            qn = _rms(q_ref[:, hs]) * qg_ref[...]
            s = _qk(qn.astype(BF16), knb) * scale
            a_ref[:, hs] = _softmax_pv(s, vb).astype(BF16)


def _attn_ctx(proj, qg, kg, nseq, seq):
    kvw = N_KV_HEADS * HEAD_DIM
    return pl.pallas_call(
        _attn_ctx_kernel,
        out_shape=(jax.ShapeDtypeStruct((nseq * seq, D_MODEL), BF16),
                   jax.ShapeDtypeStruct((nseq, seq, kvw), F32),
                   jax.ShapeDtypeStruct((nseq, seq, kvw), F32)),
        grid=(nseq,),
        in_specs=[pl.BlockSpec((seq, D_MODEL), lambda b: (b, COL_Q // D_MODEL)),
                  pl.BlockSpec((seq, kvw), lambda b: (b, COL_K // kvw)),
                  pl.BlockSpec((seq, kvw), lambda b: (b, COL_V // kvw)),
                  pl.BlockSpec((1, HEAD_DIM), lambda b: (0, 0)),
                  pl.BlockSpec((1, HEAD_DIM), lambda b: (0, 0))],
        out_specs=(pl.BlockSpec((seq, D_MODEL), lambda b: (b, 0)),
                   pl.BlockSpec((None, seq, kvw), lambda b: (b, 0, 0)),
                   pl.BlockSpec((None, seq, kvw), lambda b: (b, 0, 0))),
        compiler_params=_params(("parallel",), 32),
        name="attn_ctx",
    )(proj, proj, proj, qg, kg)


def _rope(t, cos, sa, sb):
    return t * cos + pltpu.roll(t, 96, 1) * sa + pltpu.roll(t, 32, 1) * sb


def _prep_lat_kernel(q_ref, k_ref, v_ref, ck_ref, cv_ref, cos_ref, sa_ref, sb_ref, qg_ref, kg_ref,
                     qo_ref, ko_ref, vo_ref):
    t = pl.program_id(1)

    @pl.when(t == 0)
    def _():
        ko_ref[...] = ck_ref[...].astype(BF16)
        vo_ref[...] = cv_ref[...].astype(BF16)

    @pl.when(t > 0)
    def _():
        cos, sa, sb = cos_ref[...], sa_ref[...], sb_ref[...]
        vo_ref[...] = v_ref[...].astype(BF16)
        for h in range(N_KV_HEADS):
            hs = slice(h * HEAD_DIM, (h + 1) * HEAD_DIM)
            ko_ref[:, hs] = _rope(_rms(k_ref[:, hs]) * kg_ref[...], cos, sa, sb).astype(BF16)
        for h in range(N_HEADS):
            hs = slice(h * HEAD_DIM, (h + 1) * HEAD_DIM)
            qo_ref[:, hs] = _rope(_rms(q_ref[:, hs]) * qg_ref[...], cos, sa, sb).astype(BF16)


def _prep_lat(proj, cache_k, cache_v, cos, sa, sb, qg, kg, nb, seq):
    past = cache_k.shape[1]
    tr = past
    kvw = N_KV_HEADS * HEAD_DIM
    nt = seq // tr
    prev = lambda t: jnp.maximum(t - 1, 0)
    return pl.pallas_call(
        _prep_lat_kernel,
        out_shape=(jax.ShapeDtypeStruct((nb, seq, D_MODEL), BF16),
                   jax.ShapeDtypeStruct((nb, past + seq, kvw), BF16),
                   jax.ShapeDtypeStruct((nb, past + seq, kvw), BF16)),
        grid=(nb, nt + 1),
        in_specs=[pl.BlockSpec((tr, D_MODEL), lambda b, t: (b * nt + prev(t), COL_Q // D_MODEL)),
                  pl.BlockSpec((tr, kvw), lambda b, t: (b * nt + prev(t), COL_K // kvw)),
                  pl.BlockSpec((tr, kvw), lambda b, t: (b * nt + prev(t), COL_V // kvw)),
                  pl.BlockSpec((None, past, kvw), lambda b, t: (b, 0, 0)),
                  pl.BlockSpec((None, past, kvw), lambda b, t: (b, 0, 0)),
                  pl.BlockSpec((tr, HEAD_DIM), lambda b, t: (prev(t), 0)),
                  pl.BlockSpec((tr, HEAD_DIM), lambda b, t: (prev(t), 0)),
                  pl.BlockSpec((tr, HEAD_DIM), lambda b, t: (prev(t), 0)),
                  pl.BlockSpec((1, HEAD_DIM), lambda b, t: (0, 0)),
                  pl.BlockSpec((1, HEAD_DIM), lambda b, t: (0, 0))],
        out_specs=(pl.BlockSpec((None, tr, D_MODEL), lambda b, t: (b, prev(t), 0)),
                   pl.BlockSpec((None, tr, kvw), lambda b, t: (b, t, 0)),
                   pl.BlockSpec((None, tr, kvw), lambda b, t: (b, t, 0))),
        compiler_params=_params(("parallel", "arbitrary"), 32),
        name="prep_lat",
    )(proj, proj, proj, cache_k, cache_v, cos, sa, sb, qg, kg)


def _attn_lat_kernel(q_ref, k_ref, v_ref, o_ref):
    scale = HEAD_DIM ** -0.5
    kb = k_ref[...]
    vb = v_ref[...]
    for g in range(Q_PER_KV):
        hs = slice(g * HEAD_DIM, (g + 1) * HEAD_DIM)
        s = _qk(q_ref[:, hs], kb) * scale
        o_ref[:, hs] = _softmax_pv(s, vb).astype(BF16)


def _attn_lat(qr, kall, vall, tq):
    nb, seq, _ = qr.shape
    nk = kall.shape[1]
    gw = Q_PER_KV * HEAD_DIM
    return pl.pallas_call(
        _attn_lat_kernel,
        out_shape=jax.ShapeDtypeStruct((nb, seq, D_MODEL), BF16),
        grid=(nb, N_KV_HEADS, seq // tq),
        in_specs=[pl.BlockSpec((None, tq, gw), lambda b, h, i: (b, i, h)),
                  pl.BlockSpec((None, nk, HEAD_DIM), lambda b, h, i: (b, 0, h)),
                  pl.BlockSpec((None, nk, HEAD_DIM), lambda b, h, i: (b, 0, h))],
        out_specs=pl.BlockSpec((None, tq, gw), lambda b, h, i: (b, i, h)),
        compiler_params=_params(("parallel", "parallel", "arbitrary"), 48),
        name="attn_lat",
    )(qr, kall, vall)


def _conv3_rows(x, prev_row, next_row, w, tm):
    r = lax.broadcasted_iota(jnp.int32, (tm, 1), 0)
    xm1 = jnp.where(r == 0, prev_row, pltpu.roll(x, 1, 0))
    xp1 = jnp.where(r == tm - 1, next_row, pltpu.roll(x, tm - 1, 0))
    return w[0:1, :] * xm1 + w[1:2, :] * x + w[2:3, :] * xp1


def _ssd_kernel(*refs, reverse, has_h0, nc):
    it = iter(refs)
    xs_ref, bc_ref, dt_ref, xsp_ref, xsn_ref, bcp_ref, bcn_ref = (next(it) for _ in range(7))
    cwx_ref, cbx_ref, cwb_ref, cbb_ref, alog_ref, dtb_ref, e_ref = (next(it) for _ in range(7))
    h0_ref = next(it) if has_h0 else None
    if reverse:
        z_ref, yf_ref, d_ref, g_ref = (next(it) for _ in range(4))
    y_ref, hout_ref, ht_scr = next(it), next(it), next(it)

    q = SSM_CHUNK
    npair = SSM_HEADS // 2
    c = pl.program_id(1)
    cc = nc - 1 - c if reverse else c
    first = cc == 0
    last = cc == nc - 1

    @pl.when(c == 0)
    def _():
        for p in range(npair):
            if has_h0:
                ht_scr[p] = h0_ref[p].T
            else:
                ht_scr[p] = jnp.zeros((q, q), F32)

    def conv_silu(x_ref_, p_ref_, n_ref_, w_ref_, b_ref_):
        pr = jnp.where(first, 0.0, p_ref_[SUBLANES - 1:SUBLANES, :])
        nx = jnp.where(last, 0.0, n_ref_[0:1, :])
        return _silu(_conv3_rows(x_ref_[...], pr, nx, w_ref_[...], q) + b_ref_[...])

    xs = conv_silu(xs_ref, xsp_ref, xsn_ref, cwx_ref, cbx_ref)
    bc = conv_silu(bc_ref, bcp_ref, bcn_ref, cwb_ref, cbb_ref)

    off = SSM_HEADS if reverse else 0
    dt = jax.nn.softplus(dt_ref[...] + dtb_ref[...])
    da = dt * (-jnp.exp(alog_ref[...]))
    ii = lax.broadcasted_iota(jnp.int32, (q, q), 0)
    jj = lax.broadcasted_iota(jnp.int32, (q, q), 1)
    tri = (jj >= ii) if reverse else (ii >= jj)
    tri_f = tri.astype(F32)
    tri_t = ((ii >= jj) if reverse else (jj >= ii)).astype(F32)
    acs = _hdot(tri_f, da)
    acs_t = _hdot(da.T, tri_t)
    e = e_ref[...]
    acs_x = _hdot(acs, e)
    tot_x = acs_x[0:1, :] if reverse else acs_x[q - 1:q, :]
    xdt = xs * _hdot(dt, e)
    xw = xdt * jnp.exp(tot_x - acs_x)
    eacs_x = jnp.exp(acs_x)
    etot_x = jnp.exp(tot_x)
    lo = lax.broadcasted_iota(jnp.int32, (q, LANES), 1) < SSM_HEADDIM

    parts = []
    for g in range(2):
        bg = bc[:, g * q:(g + 1) * q]
        cg = bc[:, (2 + g) * q:(3 + g) * q].astype(BF16)
        cb = _qk(cg, bg.astype(BF16))
        bgt = bg.T.astype(BF16)
        for pp in range(npair // 2):
            p = g * (npair // 2) + pp
            ps = slice(p * LANES, (p + 1) * LANES)
            ms = []
            for hh in range(2):
                ln = off + 2 * p + hh
                dec = jnp.where(tri, jnp.exp(acs[:, ln:ln + 1] - acs_t[ln:ln + 1, :]), 0.0)
                ms.append((cb * dec).astype(BF16))
            m = jnp.concatenate(ms, axis=1)
            xp = xdt[:, ps]
            bd = jnp.concatenate([jnp.where(lo, xp, 0.0), jnp.where(lo, 0.0, xp)], axis=0).astype(BF16)
            ht = ht_scr[p]
            yp = jnp.dot(m, bd, preferred_element_type=F32)
            yp = yp + jnp.dot(cg, ht.astype(BF16), preferred_element_type=F32) * eacs_x[:, ps]
            ht_scr[p] = etot_x[:, ps] * ht + jnp.dot(bgt, xw[:, ps].astype(BF16), preferred_element_type=F32)
            parts.append(yp)
    y = jnp.concatenate(parts, axis=1)

    if reverse:
        y = y + yf_ref[...] + d_ref[...] * xs
        y = y * _silu(z_ref[...])
        y_ref[...] = (_rms(y) * g_ref[...]).astype(y_ref.dtype)
    else:
        y_ref[...] = y

    @pl.when(c == nc - 1)
    def _():
        for p in range(npair):
            hout_ref[p] = ht_scr[p].T


def _ssd_pass(proj, lp, d, nseq, seq, h0, yf):
    reverse = yf is not None
    has_h0 = h0 is not None
    q = SSM_CHUNK
    nc = seq // q
    rows = nseq * seq
    npair = SSM_HEADS // 2
    hb = q // SUBLANES
    bcw = 4 * SSM_STATE
    chunk = (lambda c: nc - 1 - c) if reverse else (lambda c: c)
    blk = lambda s, c: s * nc + chunk(c)
    prev8 = lambda s, c: jnp.maximum(blk(s, c) * hb - 1, 0)
    next8 = lambda s, c: jnp.minimum((blk(s, c) + 1) * hb, rows // SUBLANES - 1)
    const = lambda s, c: (0, 0)
    in_specs = [
        pl.BlockSpec((q, D_MODEL), lambda s, c: (blk(s, c), COL_XS // D_MODEL)),
        pl.BlockSpec((q, bcw), lambda s, c: (blk(s, c), COL_BC // bcw)),
        pl.BlockSpec((q, LANES), lambda s, c: (blk(s, c), COL_DT // LANES)),
        pl.BlockSpec((SUBLANES, D_MODEL), lambda s, c: (prev8(s, c), COL_XS // D_MODEL)),
        pl.BlockSpec((SUBLANES, D_MODEL), lambda s, c: (next8(s, c), COL_XS // D_MODEL)),
        pl.BlockSpec((SUBLANES, bcw), lambda s, c: (prev8(s, c), COL_BC // bcw)),
        pl.BlockSpec((SUBLANES, bcw), lambda s, c: (next8(s, c), COL_BC // bcw)),
        pl.BlockSpec((3, D_MODEL), const), pl.BlockSpec((1, D_MODEL), const),
        pl.BlockSpec((3, bcw), const), pl.BlockSpec((1, bcw), const),
        pl.BlockSpec((1, LANES), const), pl.BlockSpec((1, LANES), const),
        pl.BlockSpec((LANES, D_MODEL), const),
    ]
    args = [proj] * 7 + [lp["cw_x"], lp["cb_x"], lp["cw_bc"], lp["cb_bc"], lp["alog"][d], lp["dtb"][d],
                         lp["expand"][d]]
    if has_h0:
        in_specs.append(pl.BlockSpec((None, npair, q, q), lambda s, c: (s, 0, 0, 0)))
        args.append(h0)
    if reverse:
        in_specs += [pl.BlockSpec((q, D_MODEL), lambda s, c: (blk(s, c), COL_Z // D_MODEL)),
                     pl.BlockSpec((q, D_MODEL), lambda s, c: (blk(s, c), 0)),
                     pl.BlockSpec((1, D_MODEL), const), pl.BlockSpec((1, D_MODEL), const)]
        args += [proj, yf, lp["d_row"], lp["ssm_norm_g"]]
    return pl.pallas_call(
        functools.partial(_ssd_kernel, reverse=reverse, has_h0=has_h0, nc=nc),
        out_shape=(jax.ShapeDtypeStruct((rows, D_MODEL), BF16 if reverse else F32),
                   jax.ShapeDtypeStruct((nseq, npair, q, q), F32)),
        grid=(nseq, nc),
        in_specs=in_specs,
        out_specs=(pl.BlockSpec((q, D_MODEL), lambda s, c: (blk(s, c), 0)),
                   pl.BlockSpec((None, npair, q, q), lambda s, c: (s, 0, 0, 0))),
        scratch_shapes=[pltpu.VMEM((npair, q, q), F32)],
        compiler_params=_params(("parallel", "arbitrary"), 32),
        name="ssd_bwd" if reverse else "ssd_fwd",
    )(*args)


ROUTE_I1, ROUTE_I2, ROUTE_P1, ROUTE_P2 = 0, 1, 2, 3


def _top2_route(logits):
    lane = lax.broadcasted_iota(jnp.int32, logits.shape, 1).astype(F32)
    lg = jnp.where(lane < N_EXPERTS, logits, -jnp.inf)
    m1 = jnp.max(lg, axis=-1, keepdims=True)
    i1 = jnp.min(jnp.where(lg == m1, lane, float(LANES)), axis=-1, keepdims=True)
    lg2 = jnp.where(lane == i1, -jnp.inf, lg)
    m2 = jnp.max(lg2, axis=-1, keepdims=True)
    i2 = jnp.min(jnp.where(lg2 == m2, lane, float(LANES)), axis=-1, keepdims=True)
    e = jnp.exp(m2 - m1)
    p1 = 1.0 / (1.0 + e)
    rec = jnp.where(lane == ROUTE_I1, i1, jnp.where(lane == ROUTE_I2, i2, 0.0))
    return jnp.where(lane == ROUTE_P1, p1, jnp.where(lane == ROUTE_P2, e * p1, rec))


def _merge_kernel(*refs, seq, tm, moe):
    it = iter(refs)
    attn_ref, ssm_ref, scb_ref, scc_ref, sch_ref = (next(it) for _ in range(5))
    sccp_ref, sccn_ref, schp_ref, schn_ref = (next(it) for _ in range(4))
    g0_ref, g1_ref, g2_ref, x_ref, mod_ref = (next(it) for _ in range(5))
    wa_ref, ws_ref, wc_ref, wm_ref, scw_ref, ng_ref = (next(it) for _ in range(6))
    router_ref = next(it) if moe else None
    xo_ref, h2_ref = next(it), next(it)
    route_ref = next(it) if moe else None

    i = pl.program_id(0)
    pos = (lax.broadcasted_iota(jnp.int32, (tm, 1), 0) + i * tm) & (seq - 1)
    u = scc_ref[...] * sch_ref[...]
    up = sccp_ref[SUBLANES - 1:SUBLANES, :] * schp_ref[SUBLANES - 1:SUBLANES, :]
    un = sccn_ref[0:1, :] * schn_ref[0:1, :]
    r = lax.broadcasted_iota(jnp.int32, (tm, 1), 0)
    um1 = jnp.where(pos == 0, 0.0, jnp.where(r == 0, up, pltpu.roll(u, 1, 0)))
    up1 = jnp.where(pos == seq - 1, 0.0, jnp.where(r == tm - 1, un, pltpu.roll(u, tm - 1, 0)))
    w = scw_ref[...]
    sc = scb_ref[...] * (w[0:1, :] * um1 + w[1:2, :] * u + w[2:3, :] * up1)

    merged = _sigmoid(g0_ref[...]) * jnp.dot(attn_ref[...], wa_ref[...], preferred_element_type=F32)
    merged = merged + _sigmoid(g1_ref[...]) * jnp.dot(ssm_ref[...], ws_ref[...], preferred_element_type=F32)
    merged = merged + _sigmoid(g2_ref[...]) * jnp.dot(sc.astype(BF16), wc_ref[...], preferred_element_type=F32)
    mod = mod_ref[0]
    xn = x_ref[...] + mod[2:3, :] * jnp.dot(merged.astype(BF16), wm_ref[...], preferred_element_type=F32)
    xo_ref[...] = xn
    h2 = _rms(xn) * ng_ref[...] * (1.0 + mod[4:5, :]) + mod[3:4, :]
    h2_ref[...] = h2.astype(BF16)
    if moe:
        route_ref[...] = _top2_route(_hdot(h2, router_ref[...]))


def _merge(attn, ssm, proj, x, mod, lp, seq, tm):
    rows = x.shape[0]
    per_mod = rows // mod.shape[0]
    tm = min(tm, per_mod)
    moe = lp["router"] is not None
    hb = tm // SUBLANES
    prev8 = lambda i: jnp.maximum(i * hb - 1, 0)
    next8 = lambda i: jnp.minimum((i + 1) * hb, rows // SUBLANES - 1)
    tile = lambda col: pl.BlockSpec((tm, D_MODEL), lambda i: (i, col // D_MODEL))
    const = lambda shape: pl.BlockSpec(shape, lambda i: (0, 0))
    in_specs = [tile(0), tile(0), tile(COL_SCB), tile(COL_SCC), tile(COL_SCH),
                pl.BlockSpec((SUBLANES, D_MODEL), lambda i: (prev8(i), COL_SCC // D_MODEL)),
                pl.BlockSpec((SUBLANES, D_MODEL), lambda i: (next8(i), COL_SCC // D_MODEL)),
                pl.BlockSpec((SUBLANES, D_MODEL), lambda i: (prev8(i), COL_SCH // D_MODEL)),
                pl.BlockSpec((SUBLANES, D_MODEL), lambda i: (next8(i), COL_SCH // D_MODEL)),
                tile(COL_G), tile(COL_G + D_MODEL), tile(COL_G + 2 * D_MODEL), tile(0),
                pl.BlockSpec((1, SUBLANES, D_MODEL), lambda i: (i * tm // per_mod, 0, 0)),
                const((D_MODEL, D_MODEL)), const((D_MODEL, D_MODEL)), const((D_MODEL, D_MODEL)),
                const((D_MODEL, D_MODEL)), const((3, D_MODEL)), const((1, D_MODEL))]
    args = [attn, ssm, proj, proj, proj, proj, proj, proj, proj, proj, proj, proj, x, mod,
            lp["w_attn_out"], lp["w_ssm_out"], lp["w_sconv_out"], lp["w_merge"], lp["sconv_w"], lp["norm_ffn_g"]]
    out_shape = [jax.ShapeDtypeStruct((rows, D_MODEL), F32), jax.ShapeDtypeStruct((rows, D_MODEL), BF16)]
    out_specs = [tile(0), tile(0)]
    if moe:
        in_specs.append(const((D_MODEL, LANES)))
        args.append(lp["router"])
        out_shape.append(jax.ShapeDtypeStruct((rows, LANES), F32))
        out_specs.append(pl.BlockSpec((tm, LANES), lambda i: (i, 0)))
    return pl.pallas_call(
        functools.partial(_merge_kernel, seq=seq, tm=tm, moe=moe),
        out_shape=tuple(out_shape),
        grid=(rows // tm,),
        in_specs=in_specs,
        out_specs=tuple(out_specs),
        compiler_params=_params(("parallel",), 56),
        name="merge",
    )(*args)


def _swiglu_acc(h, w1_ref, w3_ref, w2_ref, fs):
    a = _silu(jnp.dot(h, w1_ref[:, fs], preferred_element_type=F32)) * jnp.dot(h, w3_ref[:, fs],
                                                                                preferred_element_type=F32)
    return jnp.dot(a.astype(BF16), w2_ref[fs, :], preferred_element_type=F32)


def _ffn_kernel(h_ref, x_ref, mod_ref, w1_ref, w3_ref, w2_ref, o_ref, acc):
    f = pl.program_id(1)

    @pl.when(f == 0)
    def _():
        acc[...] = jnp.zeros(acc.shape, F32)

    acc[...] += _swiglu_acc(h_ref[...], w1_ref, w3_ref, w2_ref, slice(None))

    @pl.when(f == pl.num_programs(1) - 1)
    def _():
        o_ref[...] = x_ref[...] + mod_ref[0, 5:6, :] * acc[...]


def _ffn(h2, x, mod, w1, w3, w2, tm, tf):
    rows = x.shape[0]
    per_mod = rows // mod.shape[0]
    tm = min(tm, per_mod)
    return pl.pallas_call(
        _ffn_kernel,
        out_shape=jax.ShapeDtypeStruct((rows, D_MODEL), F32),
        grid=(rows // tm, D_FF // tf),
        in_specs=[pl.BlockSpec((tm, D_MODEL), lambda i, f: (i, 0)),
                  pl.BlockSpec((tm, D_MODEL), lambda i, f: (i, 0)),
                  pl.BlockSpec((1, SUBLANES, D_MODEL), lambda i, f: (i * tm // per_mod, 0, 0)),
                  pl.BlockSpec((D_MODEL, tf), lambda i, f: (0, f)),
                  pl.BlockSpec((D_MODEL, tf), lambda i, f: (0, f)),
                  pl.BlockSpec((tf, D_MODEL), lambda i, f: (f, 0))],
        out_specs=pl.BlockSpec((tm, D_MODEL), lambda i, f: (i, 0)),
        scratch_shapes=[pltpu.VMEM((tm, D_MODEL), F32)],
        compiler_params=_params(("parallel", "arbitrary"), 56),
        name="ffn",
    )(h2, x, mod, w1, w3, w2)


SC_CORES = 2
SC_SUBCORES = 16
SC_CHUNK = 128


def _sc_gather(table, idx):
    nw = SC_CORES * SC_SUBCORES
    b, d = idx.shape[0], table.shape[1]
    per_w = b // nw
    assert per_w * nw == b and per_w % SC_CHUNK == 0
    mesh = plsc.VectorSubcoreMesh(core_axis_name="c", subcore_axis_name="s")

    def body(table_hbm, idx_hbm, out_hbm, idx_v, rows_v, sem):
        wid = lax.axis_index("s") * SC_CORES + lax.axis_index("c")

        @pl.loop(0, per_w // SC_CHUNK)
        def _(j):
            off = pl.multiple_of(wid * per_w + j * SC_CHUNK, SC_CHUNK)
            pltpu.sync_copy(idx_hbm.at[pl.ds(off, SC_CHUNK)], idx_v)
            pltpu.async_copy(table_hbm.at[idx_v], rows_v, sem).wait()
            pltpu.sync_copy(rows_v, out_hbm.at[pl.ds(off, SC_CHUNK)])

    return pl.kernel(
        body,
        out_type=jax.ShapeDtypeStruct((b, d), table.dtype),
        mesh=mesh,
        scratch_types=[pltpu.VMEM((SC_CHUNK,), jnp.int32), pltpu.VMEM((SC_CHUNK, d), table.dtype),
                       pltpu.SemaphoreType.DMA],
        name="sc_gather",
    )(table, idx)


def _pack_rows(t):
    return lax.bitcast_convert_type(t.reshape(t.shape[0], t.shape[1] // 2, 2), jnp.int32)


def _unpack_rows(t):
    return lax.bitcast_convert_type(t, BF16).reshape(t.shape[0], t.shape[1] * 2)


def _route_tables(route, tile):
    t = route.shape[0]
    e = route[:, ROUTE_I1:ROUTE_I2 + 1].astype(jnp.int32)
    chosen = (e[:, :, None] == jnp.arange(N_EXPERTS)[None, None, :]).any(axis=1).astype(jnp.int32)
    incl = jnp.cumsum(chosen, axis=0)
    rank = incl - chosen
    padded = (incl[-1] + tile - 1) // tile * tile
    gend = jnp.cumsum(padded)
    slot = (gend - padded)[e] + jnp.take_along_axis(rank, e, axis=1)
    n_slots = 2 * t + N_EXPERTS * tile
    tok_of_slot = jnp.zeros((n_slots,), jnp.int32).at[slot.reshape(-1)].set(jnp.repeat(jnp.arange(t), 2))
    n_valid = gend[-1] // tile
    n_tiles = n_slots // tile
    tile_expert = jnp.sum(jnp.arange(n_tiles)[:, None] * tile >= gend[None, :], axis=1)
    last = jnp.take(tile_expert, jnp.maximum(n_valid - 1, 0))
    tile_expert = jnp.where(jnp.arange(n_tiles) < n_valid, tile_expert, last).astype(jnp.int32)
    return slot.reshape(-1).astype(jnp.int32), tok_of_slot, tile_expert, n_valid.reshape(1).astype(jnp.int32)


def _moe_ffn_kernel(te_ref, nv_ref, h_ref, w1_ref, w3_ref, w2_ref, o_ref):
    i = pl.program_id(0)

    @pl.when(i < nv_ref[0])
    def _():
        h = h_ref[...]
        half = D_FF // 2
        y = _swiglu_acc(h, w1_ref, w3_ref, w2_ref, slice(0, half))
        y = y + _swiglu_acc(h, w1_ref, w3_ref, w2_ref, slice(half, D_FF))
        o_ref[...] = y.astype(BF16)

    @pl.when(i >= nv_ref[0])
    def _():
        o_ref[...] = jnp.zeros(o_ref.shape, BF16)


def _moe_ffn(hs, tile_expert, n_valid, w1, w3, w2, tile):
    n_slots = hs.shape[0]
    grid_spec = pltpu.PrefetchScalarGridSpec(
        num_scalar_prefetch=2,
        grid=(n_slots // tile,),
        in_specs=[pl.BlockSpec((tile, D_MODEL), lambda i, te, nv: (i, 0)),
                  pl.BlockSpec((None, D_MODEL, D_FF), lambda i, te, nv: (te[i], 0, 0)),
                  pl.BlockSpec((None, D_MODEL, D_FF), lambda i, te, nv: (te[i], 0, 0)),
                  pl.BlockSpec((None, D_FF, D_MODEL), lambda i, te, nv: (te[i], 0, 0))],
        out_specs=pl.BlockSpec((tile, D_MODEL), lambda i, te, nv: (i, 0)),
    )
    return pl.pallas_call(
        _moe_ffn_kernel,
        out_shape=jax.ShapeDtypeStruct((n_slots, D_MODEL), BF16),
        grid_spec=grid_spec,
        compiler_params=_params(("arbitrary",), 56),
        name="moe_ffn",
    )(tile_expert, n_valid, hs, w1, w3, w2)


def _combine_kernel(x_ref, y_ref, route_ref, mod_ref, o_ref):
    p1 = route_ref[:, ROUTE_P1:ROUTE_P1 + 1]
    p2 = route_ref[:, ROUTE_P2:ROUTE_P2 + 1]
    f = p1 * y_ref[:, :D_MODEL].astype(F32) + p2 * y_ref[:, D_MODEL:].astype(F32)
    o_ref[...] = x_ref[...] + mod_ref[0, 5:6, :] * f


def _combine(x, yg, route, mod, tm):
    rows = x.shape[0]
    per_mod = rows // mod.shape[0]
    tm = min(tm, per_mod)
    return pl.pallas_call(
        _combine_kernel,
        out_shape=jax.ShapeDtypeStruct((rows, D_MODEL), F32),
        grid=(rows // tm,),
        in_specs=[pl.BlockSpec((tm, D_MODEL), lambda i: (i, 0)),
                  pl.BlockSpec((tm, 2 * D_MODEL), lambda i: (i, 0)),
                  pl.BlockSpec((tm, LANES), lambda i: (i, 0)),
                  pl.BlockSpec((1, SUBLANES, D_MODEL), lambda i: (i * tm // per_mod, 0, 0))],
        out_specs=pl.BlockSpec((tm, D_MODEL), lambda i: (i, 0)),
        compiler_params=_params(("parallel",), 32),
        name="moe_combine",
    )(x, yg, route, mod)


def _moe(h2, x, mod, route, w1, w3, w2, tile):
    rows = x.shape[0]
    slot, tok_of_slot, tile_expert, n_valid = _route_tables(route, tile)
    hs = _unpack_rows(_sc_gather(_pack_rows(h2), tok_of_slot))
    ys = _moe_ffn(hs, tile_expert, n_valid, w1, w3, w2, tile)
    yg = _unpack_rows(_sc_gather(_pack_rows(ys), slot)).reshape(rows, 2 * D_MODEL)
    return _combine(x, yg, route, mod, tm=512)


def _layer_params(l, w_in, q_norm_g, k_norm_g, w_attn_out, ssm_conv_w, ssm_conv_b, ssm_dt_bias, ssm_a_log,
                  ssm_d, ssm_norm_g, w_ssm_out, sconv_w, w_sconv_out, w_merge, norm_mix_g, norm_ffn_g,
                  ffn_w1, ffn_w3, ffn_w2, moe_router, moe_w1, moe_w3, moe_w2):
    w = w_in[l]
    seg = lambda a, n: w[:, a:a + n]
    w_perm = jnp.concatenate(
        [seg(0, 1024), seg(1536, 1024), seg(4128, 1024), seg(5152, 1024), seg(6176, 1024), seg(7200, 3072),
         seg(2560, 1024), seg(3584, 512), seg(1024, 256), seg(1280, 256), seg(4096, 32),
         jnp.zeros((D_MODEL, N_PROJ - COL_DT - 2 * SSM_HEADS), w.dtype)], axis=1).astype(BF16)
    row = lambda v: v.reshape(1, -1)
    lane_row = lambda v, off: jnp.zeros((1, LANES), F32).at[0, off:off + SSM_HEADS].set(v)
    head_of_col = jnp.arange(D_MODEL) // SSM_HEADDIM
    expand = [(jnp.arange(LANES)[:, None] == head_of_col[None, :] + off).astype(F32) for off in (0, SSM_HEADS)]
    lp = {
        "w_in": w_perm,
        "norm_mix_g": row(norm_mix_g[l]), "norm_ffn_g": row(norm_ffn_g[l]),
        "q_norm_g": row(q_norm_g[l]), "k_norm_g": row(k_norm_g[l]),
        "w_attn_out": w_attn_out[l].astype(BF16), "w_ssm_out": w_ssm_out[l].astype(BF16),
        "w_sconv_out": w_sconv_out[l].astype(BF16), "w_merge": w_merge[l].astype(BF16),
        "cw_x": ssm_conv_w[l][:, :D_MODEL], "cw_bc": ssm_conv_w[l][:, D_MODEL:],
        "cb_x": row(ssm_conv_b[l][:D_MODEL]), "cb_bc": row(ssm_conv_b[l][D_MODEL:]),
        "alog": [lane_row(ssm_a_log[l, d], d * SSM_HEADS) for d in range(2)],
        "dtb": [lane_row(ssm_dt_bias[l, d], d * SSM_HEADS) for d in range(2)],
        "expand": expand,
        "d_row": row(jnp.repeat(ssm_d[l], SSM_HEADDIM)),
        "ssm_norm_g": row(ssm_norm_g[l]),
        "sconv_w": sconv_w[l],
    }
    if l % 2 == 0:
        lp["router"] = None
        lp["ffn"] = tuple(t[l // 2].astype(BF16) for t in (ffn_w1, ffn_w3, ffn_w2))
    else:
        lp["router"] = jnp.pad(moe_router[l // 2], ((0, 0), (0, LANES - N_EXPERTS)))
        lp["ffn"] = tuple(t[l // 2].astype(BF16) for t in (moe_w1, moe_w3, moe_w2))
    return lp


def _rope_tables(seq):
    t = jnp.arange(seq)
    n = HEAD_DIM // 4
    inv = ROPE_THETA ** (-jnp.arange(n, dtype=F32) / n)
    ang_r = (t // GRID_W).astype(F32)[:, None] * inv[None, :]
    ang_c = (t % GRID_W).astype(F32)[:, None] * inv[None, :]
    zero = jnp.zeros_like(ang_r)
    cos = jnp.concatenate([jnp.cos(ang_r)] * 2 + [jnp.cos(ang_c)] * 2, axis=1)
    sa = jnp.concatenate([-jnp.sin(ang_r), zero, -jnp.sin(ang_c), zero], axis=1)
    sb = jnp.concatenate([zero, jnp.sin(ang_r), zero, jnp.sin(ang_c)], axis=1)
    return cos, sa, sb


def _trunk(x, mod, lp, nseq, seq, lat):
    rows = nseq * seq
    proj = _inproj(x, mod, lp["norm_mix_g"], lp["w_in"], tm=min(1024, rows), tn=1152)
    if lat is None:
        attn, k_new, v_new = _attn_ctx(proj, lp["q_norm_g"], lp["k_norm_g"], nseq, seq)
        h0_f = h0_b = None
    else:
        cache_k, cache_v, h0_f, h0_b, tables = lat
        qr, kall, vall = _prep_lat(proj, cache_k, cache_v, *tables, lp["q_norm_g"], lp["k_norm_g"], nseq, seq)
        attn = _attn_lat(qr, kall, vall, tq=256).reshape(rows, D_MODEL)
        k_new = v_new = None
    yf, h_f = _ssd_pass(proj, lp, 0, nseq, seq, h0_f, None)
    ssm, h_b = _ssd_pass(proj, lp, 1, nseq, seq, h0_b, yf)
    outs = _merge(attn, ssm, proj, x, mod, lp, seq, tm=256)
    w1, w3, w2 = lp["ffn"]
    if lp["router"] is None:
        x = _ffn(outs[1], outs[0], mod, w1, w3, w2, tm=512, tf=1408)
    else:
        x = _moe(outs[1], outs[0], mod, outs[2], w1, w3, w2, tile=512)
    return x, k_new, v_new, h_f, h_b


def kernel(x_prompt, x_sample, cache_k, cache_v, state_ssm_fwd, state_ssm_bwd, c, c_ctx, w_mod, b_mod, norm_mix_g, norm_ffn_g, w_in, q_norm_g, k_norm_g, w_attn_out, ssm_conv_w, ssm_conv_b, ssm_dt_bias, ssm_a_log, ssm_d, ssm_norm_g, w_ssm_out, sconv_w, w_sconv_out, w_merge, ffn_w1, ffn_w3, ffn_w2, moe_router, moe_w1, moe_w3, moe_w2):
    batch, seq, _ = x_prompt.shape
    nb, dseq, _ = x_sample.shape
    depth = w_in.shape[0]
    past = cache_k.shape[2]
    kvw = N_KV_HEADS * HEAD_DIM
    npair = SSM_HEADS // 2
    assert nb + 1 <= SUBLANES

    cond = jnp.zeros((SUBLANES, D_MODEL), F32).at[0].set(c_ctx).at[1:1 + nb].set(c)
    tables = _rope_tables(dseq)
    y_ctx = x_prompt.reshape(batch * seq, D_MODEL)
    y_lat = x_sample.reshape(nb * dseq, D_MODEL)
    ks, vs, hfs, hbs = [], [], [], []
    for l in range(depth):
        lp = _layer_params(l, w_in, q_norm_g, k_norm_g, w_attn_out, ssm_conv_w, ssm_conv_b, ssm_dt_bias,
                           ssm_a_log, ssm_d, ssm_norm_g, w_ssm_out, sconv_w, w_sconv_out, w_merge,
                           norm_mix_g, norm_ffn_g, ffn_w1, ffn_w3, ffn_w2, moe_router, moe_w1, moe_w3, moe_w2)
        m = _modulation(cond, w_mod[l], b_mod[l].reshape(1, -1)).reshape(SUBLANES, 6, D_MODEL)
        m = jnp.pad(m, ((0, 0), (0, SUBLANES - 6), (0, 0)))
        y_ctx, k_l, v_l, hf_l, hb_l = _trunk(y_ctx, m[0:1], lp, batch, seq, None)
        ks.append(k_l.reshape(batch, seq, N_KV_HEADS, HEAD_DIM))
        vs.append(v_l.reshape(batch, seq, N_KV_HEADS, HEAD_DIM))
        hfs.append(hf_l.reshape(batch, SSM_HEADS, SSM_HEADDIM, SSM_STATE))
        hbs.append(hb_l.reshape(batch, SSM_HEADS, SSM_HEADDIM, SSM_STATE))
        lat = (cache_k[:, l].reshape(nb, past, kvw), cache_v[:, l].reshape(nb, past, kvw),
               state_ssm_fwd[:, l].reshape(nb, npair, SSM_CHUNK, SSM_STATE),
               state_ssm_bwd[:, l].reshape(nb, npair, SSM_CHUNK, SSM_STATE), tables)
        y_lat = _trunk(y_lat, m[1:1 + nb], lp, nb, dseq, lat)[0]
    return (y_ctx.reshape(batch, seq, D_MODEL), y_lat.reshape(nb, dseq, D_MODEL),
            jnp.stack(ks, axis=1), jnp.stack(vs, axis=1), jnp.stack(hfs, axis=1), jnp.stack(hbs, axis=1))
```

```python
import functools

import jax
import jax.numpy as jnp
from jax import lax
from jax.experimental import pallas as pl
from jax.experimental.pallas import tpu as pltpu
from jax.experimental.pallas import tpu_sc as plsc

F32 = jnp.float32
BF16 = jnp.bfloat16
HIGHEST = lax.Precision.HIGHEST

D_MODEL = 1024
HEAD_DIM = 128
N_HEADS = 8
N_KV_HEADS = 2
Q_PER_KV = N_HEADS // N_KV_HEADS
ROPE_THETA = 10000.0
GRID_W = 64
SSM_HEADS = 16
SSM_HEADDIM = 64
SSM_STATE = 128
SSM_CHUNK = 128
D_FF = 2816
N_EXPERTS = 8
EPS = 1e-6

COL_Q, COL_Z, COL_SCB, COL_SCC, COL_SCH, COL_G = 0, 1024, 2048, 3072, 4096, 5120
COL_XS, COL_BC, COL_K, COL_V, COL_DT = 8192, 9216, 9728, 9984, 10240
N_PROJ = 10368
LANES = 128
SUBLANES = 8
MIB = 1024 * 1024


def _params(sem, vmem_mib):
    return pltpu.CompilerParams(dimension_semantics=sem, vmem_limit_bytes=vmem_mib * MIB)


def _sigmoid(t):
    return 1.0 / (1.0 + jnp.exp(-t))


def _silu(t):
    return t * _sigmoid(t)


def _rms(t):
    return t * lax.rsqrt(jnp.mean(t * t, axis=-1, keepdims=True) + EPS)


def _bdot(a, b):
    return jnp.dot(a.astype(BF16), b.astype(BF16), preferred_element_type=F32)


def _hdot(a, b):
    return jnp.dot(a, b, precision=HIGHEST, preferred_element_type=F32)


def _mod_kernel(c_ref, w_ref, b_ref, o_ref):
    o_ref[...] = _hdot(_silu(c_ref[...]), w_ref[...]) + b_ref[...]


def _modulation(cond8, w, b):
    n = w.shape[1] // D_MODEL
    return pl.pallas_call(
        _mod_kernel,
        out_shape=jax.ShapeDtypeStruct((SUBLANES, w.shape[1]), F32),
        grid=(n,),
        in_specs=[pl.BlockSpec((SUBLANES, D_MODEL), lambda j: (0, 0)),
                  pl.BlockSpec((D_MODEL, D_MODEL), lambda j: (0, j)),
                  pl.BlockSpec((1, D_MODEL), lambda j: (0, j))],
        out_specs=pl.BlockSpec((SUBLANES, D_MODEL), lambda j: (0, j)),
        compiler_params=_params(("parallel",), 32),
        name="modulation",
    )(cond8, w, b)


def _inproj_kernel(x_ref, mod_ref, g_ref, w_ref, o_ref, h_scr):
    @pl.when(pl.program_id(1) == 0)
    def _():
        h = _rms(x_ref[...]) * g_ref[...]
        h = h * (1.0 + mod_ref[0, 1:2, :]) + mod_ref[0, 0:1, :]
        h_scr[...] = h.astype(BF16)

    o_ref[...] = jnp.dot(h_scr[...], w_ref[...], preferred_element_type=F32)


def _inproj(x, mod, g, w, tm, tn):
    rows = x.shape[0]
    per_mod = rows // mod.shape[0]
    tm = min(tm, per_mod)
    return pl.pallas_call(
        _inproj_kernel,
        out_shape=jax.ShapeDtypeStruct((rows, N_PROJ), F32),
        grid=(rows // tm, N_PROJ // tn),
        in_specs=[pl.BlockSpec((tm, D_MODEL), lambda i, j: (i, 0)),
                  pl.BlockSpec((1, SUBLANES, D_MODEL), lambda i, j: (i * tm // per_mod, 0, 0)),
                  pl.BlockSpec((1, D_MODEL), lambda i, j: (0, 0)),
                  pl.BlockSpec((D_MODEL, tn), lambda i, j: (0, j))],
        out_specs=pl.BlockSpec((tm, tn), lambda i, j: (i, j)),
        scratch_shapes=[pltpu.VMEM((tm, D_MODEL), BF16)],
        compiler_params=_params(("parallel", "arbitrary"), 48),
        name="inproj",
    )(x, mod, g, w)


def _softmax_pv(s, vb):
    m = jnp.max(s, axis=-1, keepdims=True)
    p = jnp.exp(s - m)
    l = jnp.sum(p, axis=-1, keepdims=True)
    return jnp.dot(p.astype(BF16), vb, preferred_element_type=F32) / l


def _qk(qb, kb):
    return lax.dot_general(qb, kb, (((1,), (1,)), ((), ())), preferred_element_type=F32)


def _attn_ctx_kernel(q_ref, k_ref, v_ref, qg_ref, kg_ref, a_ref, ko_ref, vo_ref):
    scale = HEAD_DIM ** -0.5
    v = v_ref[...]
    vo_ref[...] = v
    for hk in range(N_KV_HEADS):
        ks = slice(hk * HEAD_DIM, (hk + 1) * HEAD_DIM)
        kn = _rms(k_ref[:, ks]) * kg_ref[...]
        ko_ref[:, ks] = kn
        knb = kn.astype(BF16)
        vb = v[:, ks].astype(BF16)
        for g in range(Q_PER_KV):
            hs = slice((hk * Q_PER_KV + g) * HEAD_DIM, (hk * Q_PER_KV + g + 1) * HEAD_DIM)
            qn = _rms(q_ref[:, hs]) * qg_ref[...]
            s = _qk(qn.astype(BF16), knb) * scale
            a_ref[:, hs] = _softmax_pv(s, vb).astype(BF16)


def _attn_ctx(proj, qg, kg, nseq, seq):
    kvw = N_KV_HEADS * HEAD_DIM
    return pl.pallas_call(
        _attn_ctx_kernel,
        out_shape=(jax.ShapeDtypeStruct((nseq * seq, D_MODEL), BF16),
                   jax.ShapeDtypeStruct((nseq, seq, kvw), F32),
                   jax.ShapeDtypeStruct((nseq, seq, kvw), F32)),
        grid=(nseq,),
        in_specs=[pl.BlockSpec((seq, D_MODEL), lambda b: (b, COL_Q // D_MODEL)),
                  pl.BlockSpec((seq, kvw), lambda b: (b, COL_K // kvw)),
                  pl.BlockSpec((seq, kvw), lambda b: (b, COL_V // kvw)),
                  pl.BlockSpec((1, HEAD_DIM), lambda b: (0, 0)),
                  pl.BlockSpec((1, HEAD_DIM), lambda b: (0, 0))],
        out_specs=(pl.BlockSpec((seq, D_MODEL), lambda b: (b, 0)),
                   pl.BlockSpec((None, seq, kvw), lambda b: (b, 0, 0)),
                   pl.BlockSpec((None, seq, kvw), lambda b: (b, 0, 0))),
        compiler_params=_params(("parallel",), 32),
        name="attn_ctx",
    )(proj, proj, proj, qg, kg)


def _rope(t, cos, sa, sb):
    return t * cos + pltpu.roll(t, 96, 1) * sa + pltpu.roll(t, 32, 1) * sb


def _prep_lat_kernel(q_ref, k_ref, v_ref, ck_ref, cv_ref, cos_ref, sa_ref, sb_ref, qg_ref, kg_ref,
                     qo_ref, ko_ref, vo_ref):
    t = pl.program_id(1)

    @pl.when(t == 0)
    def _():
        ko_ref[...] = ck_ref[...].astype(BF16)
        vo_ref[...] = cv_ref[...].astype(BF16)

    @pl.when(t > 0)
    def _():
        cos, sa, sb = cos_ref[...], sa_ref[...], sb_ref[...]
        vo_ref[...] = v_ref[...].astype(BF16)
        for h in range(N_KV_HEADS):
            hs = slice(h * HEAD_DIM, (h + 1) * HEAD_DIM)
            ko_ref[:, hs] = _rope(_rms(k_ref[:, hs]) * kg_ref[...], cos, sa, sb).astype(BF16)
        for h in range(N_HEADS):
            hs = slice(h * HEAD_DIM, (h + 1) * HEAD_DIM)
            qo_ref[:, hs] = _rope(_rms(q_ref[:, hs]) * qg_ref[...], cos, sa, sb).astype(BF16)


def _prep_lat(proj, cache_k, cache_v, cos, sa, sb, qg, kg, nb, seq):
    past = cache_k.shape[1]
    tr = past
    kvw = N_KV_HEADS * HEAD_DIM
    nt = seq // tr
    prev = lambda t: jnp.maximum(t - 1, 0)
    return pl.pallas_call(
        _prep_lat_kernel,
        out_shape=(jax.ShapeDtypeStruct((nb, seq, D_MODEL), BF16),
                   jax.ShapeDtypeStruct((nb, past + seq, kvw), BF16),
                   jax.ShapeDtypeStruct((nb, past + seq, kvw), BF16)),
        grid=(nb, nt + 1),
        in_specs=[pl.BlockSpec((tr, D_MODEL), lambda b, t: (b * nt + prev(t), COL_Q // D_MODEL)),
                  pl.BlockSpec((tr, kvw), lambda b, t: (b * nt + prev(t), COL_K // kvw)),
                  pl.BlockSpec((tr, kvw), lambda b, t: (b * nt + prev(t), COL_V // kvw)),
                  pl.BlockSpec((None, past, kvw), lambda b, t: (b, 0, 0)),
                  pl.BlockSpec((None, past, kvw), lambda b, t: (b, 0, 0)),
                  pl.BlockSpec((tr, HEAD_DIM), lambda b, t: (prev(t), 0)),
                  pl.BlockSpec((tr, HEAD_DIM), lambda b, t: (prev(t), 0)),
                  pl.BlockSpec((tr, HEAD_DIM), lambda b, t: (prev(t), 0)),
                  pl.BlockSpec((1, HEAD_DIM), lambda b, t: (0, 0)),
                  pl.BlockSpec((1, HEAD_DIM), lambda b, t: (0, 0))],
        out_specs=(pl.BlockSpec((None, tr, D_MODEL), lambda b, t: (b, prev(t), 0)),
                   pl.BlockSpec((None, tr, kvw), lambda b, t: (b, t, 0)),
                   pl.BlockSpec((None, tr, kvw), lambda b, t: (b, t, 0))),
        compiler_params=_params(("parallel", "arbitrary"), 32),
        name="prep_lat",
    )(proj, proj, proj, cache_k, cache_v, cos, sa, sb, qg, kg)


def _attn_lat_kernel(q_ref, k_ref, v_ref, o_ref):
    scale = HEAD_DIM ** -0.5
    kb = k_ref[...]
    vb = v_ref[...]
    for g in range(Q_PER_KV):
        hs = slice(g * HEAD_DIM, (g + 1) * HEAD_DIM)
        s = _qk(q_ref[:, hs], kb) * scale
        o_ref[:, hs] = _softmax_pv(s, vb).astype(BF16)


def _attn_lat(qr, kall, vall, tq):
    nb, seq, _ = qr.shape
    nk = kall.shape[1]
    gw = Q_PER_KV * HEAD_DIM
    return pl.pallas_call(
        _attn_lat_kernel,
        out_shape=jax.ShapeDtypeStruct((nb, seq, D_MODEL), BF16),
        grid=(nb, N_KV_HEADS, seq // tq),
        in_specs=[pl.BlockSpec((None, tq, gw), lambda b, h, i: (b, i, h)),
                  pl.BlockSpec((None, nk, HEAD_DIM), lambda b, h, i: (b, 0, h)),
                  pl.BlockSpec((None, nk, HEAD_DIM), lambda b, h, i: (b, 0, h))],
        out_specs=pl.BlockSpec((None, tq, gw), lambda b, h, i: (b, i, h)),
        compiler_params=_params(("parallel", "parallel", "arbitrary"), 48),
        name="attn_lat",
    )(qr, kall, vall)


def _conv3_rows(x, prev_row, next_row, w, tm):
    r = lax.broadcasted_iota(jnp.int32, (tm, 1), 0)
    xm1 = jnp.where(r == 0, prev_row, pltpu.roll(x, 1, 0))
    xp1 = jnp.where(r == tm - 1, next_row, pltpu.roll(x, tm - 1, 0))
    return w[0:1, :] * xm1 + w[1:2, :] * x + w[2:3, :] * xp1


def _ssd_kernel(*refs, reverse, has_h0, nc):
    it = iter(refs)
    xs_ref, bc_ref, dt_ref, xsp_ref, xsn_ref, bcp_ref, bcn_ref = (next(it) for _ in range(7))
    cwx_ref, cbx_ref, cwb_ref, cbb_ref, alog_ref, dtb_ref, e_ref = (next(it) for _ in range(7))
    h0_ref = next(it) if has_h0 else None
    if reverse:
        z_ref, yf_ref, d_ref, g_ref = (next(it) for _ in range(4))
    y_ref, hout_ref, ht_scr = next(it), next(it), next(it)

    q = SSM_CHUNK
    npair = SSM_HEADS // 2
    c = pl.program_id(1)
    cc = nc - 1 - c if reverse else c
    first = cc == 0
    last = cc == nc - 1

    @pl.when(c == 0)
    def _():
        for p in range(npair):
            if has_h0:
                ht_scr[p] = h0_ref[p].T
            else:
                ht_scr[p] = jnp.zeros((q, q), F32)

    def conv_silu(x_ref_, p_ref_, n_ref_, w_ref_, b_ref_):
        pr = jnp.where(first, 0.0, p_ref_[SUBLANES - 1:SUBLANES, :])
        nx = jnp.where(last, 0.0, n_ref_[0:1, :])
        return _silu(_conv3_rows(x_ref_[...], pr, nx, w_ref_[...], q) + b_ref_[...])

    xs = conv_silu(xs_ref, xsp_ref, xsn_ref, cwx_ref, cbx_ref)
    bc = conv_silu(bc_ref, bcp_ref, bcn_ref, cwb_ref, cbb_ref)

    off = SSM_HEADS if reverse else 0
    dt = jax.nn.softplus(dt_ref[...] + dtb_ref[...])
    da = dt * (-jnp.exp(alog_ref[...]))
    ii = lax.broadcasted_iota(jnp.int32, (q, q), 0)
    jj = lax.broadcasted_iota(jnp.int32, (q, q), 1)
    tri = (jj >= ii) if reverse else (ii >= jj)
    tri_f = tri.astype(F32)
    tri_t = ((ii >= jj) if reverse else (jj >= ii)).astype(F32)
    acs = _hdot(tri_f, da)
    acs_t = _hdot(da.T, tri_t)
    e = e_ref[...]
    acs_x = _hdot(acs, e)
    tot_x = acs_x[0:1, :] if reverse else acs_x[q - 1:q, :]
    xdt = xs * _hdot(dt, e)
    xw = xdt * jnp.exp(tot_x - acs_x)
    eacs_x = jnp.exp(acs_x)
    etot_x = jnp.exp(tot_x)
    lo = lax.broadcasted_iota(jnp.int32, (q, LANES), 1) < SSM_HEADDIM

    parts = []
    for g in range(2):
        bg = bc[:, g * q:(g + 1) * q]
        cg = bc[:, (2 + g) * q:(3 + g) * q].astype(BF16)
        cb = _qk(cg, bg.astype(BF16))
        bgt = bg.T.astype(BF16)
        for pp in range(npair // 2):
            p = g * (npair // 2) + pp
            ps = slice(p * LANES, (p + 1) * LANES)
            ms = []
            for hh in range(2):
                ln = off + 2 * p + hh
                dec = jnp.where(tri, jnp.exp(acs[:, ln:ln + 1] - acs_t[ln:ln + 1, :]), 0.0)
                ms.append((cb * dec).astype(BF16))
            m = jnp.concatenate(ms, axis=1)
            xp = xdt[:, ps]
            bd = jnp.concatenate([jnp.where(lo, xp, 0.0), jnp.where(lo, 0.0, xp)], axis=0).astype(BF16)
            ht = ht_scr[p]
            yp = jnp.dot(m, bd, preferred_element_type=F32)
            yp = yp + jnp.dot(cg, ht.astype(BF16), preferred_element_type=F32) * eacs_x[:, ps]
            ht_scr[p] = etot_x[:, ps] * ht + jnp.dot(bgt, xw[:, ps].astype(BF16), preferred_element_type=F32)
            parts.append(yp)
    y = jnp.concatenate(parts, axis=1)

    if reverse:
        y = y + yf_ref[...] + d_ref[...] * xs
        y = y * _silu(z_ref[...])
        y_ref[...] = (_rms(y) * g_ref[...]).astype(y_ref.dtype)
    else:
        y_ref[...] = y

    @pl.when(c == nc - 1)
    def _():
        for p in range(npair):
            hout_ref[p] = ht_scr[p].T


def _ssd_pass(proj, lp, d, nseq, seq, h0, yf):
    reverse = yf is not None
    has_h0 = h0 is not None
    q = SSM_CHUNK
    nc = seq // q
    rows = nseq * seq
    npair = SSM_HEADS // 2
    hb = q // SUBLANES
    bcw = 4 * SSM_STATE
    chunk = (lambda c: nc - 1 - c) if reverse else (lambda c: c)
    blk = lambda s, c: s * nc + chunk(c)
    prev8 = lambda s, c: jnp.maximum(blk(s, c) * hb - 1, 0)
    next8 = lambda s, c: jnp.minimum((blk(s, c) + 1) * hb, rows // SUBLANES - 1)
    const = lambda s, c: (0, 0)
    in_specs = [
        pl.BlockSpec((q, D_MODEL), lambda s, c: (blk(s, c), COL_XS // D_MODEL)),
        pl.BlockSpec((q, bcw), lambda s, c: (blk(s, c), COL_BC // bcw)),
        pl.BlockSpec((q, LANES), lambda s, c: (blk(s, c), COL_DT // LANES)),
        pl.BlockSpec((SUBLANES, D_MODEL), lambda s, c: (prev8(s, c), COL_XS // D_MODEL)),
        pl.BlockSpec((SUBLANES, D_MODEL), lambda s, c: (next8(s, c), COL_XS // D_MODEL)),
        pl.BlockSpec((SUBLANES, bcw), lambda s, c: (prev8(s, c), COL_BC // bcw)),
        pl.BlockSpec((SUBLANES, bcw), lambda s, c: (next8(s, c), COL_BC // bcw)),
        pl.BlockSpec((3, D_MODEL), const), pl.BlockSpec((1, D_MODEL), const),
        pl.BlockSpec((3, bcw), const), pl.BlockSpec((1, bcw), const),
        pl.BlockSpec((1, LANES), const), pl.BlockSpec((1, LANES), const),
        pl.BlockSpec((LANES, D_MODEL), const),
    ]
    args = [proj] * 7 + [lp["cw_x"], lp["cb_x"], lp["cw_bc"], lp["cb_bc"], lp["alog"][d], lp["dtb"][d],
                         lp["expand"][d]]
    if has_h0:
        in_specs.append(pl.BlockSpec((None, npair, q, q), lambda s, c: (s, 0, 0, 0)))
        args.append(h0)
    if reverse:
        in_specs += [pl.BlockSpec((q, D_MODEL), lambda s, c: (blk(s, c), COL_Z // D_MODEL)),
                     pl.BlockSpec((q, D_MODEL), lambda s, c: (blk(s, c), 0)),
                     pl.BlockSpec((1, D_MODEL), const), pl.BlockSpec((1, D_MODEL), const)]
        args += [proj, yf, lp["d_row"], lp["ssm_norm_g"]]
    return pl.pallas_call(
        functools.partial(_ssd_kernel, reverse=reverse, has_h0=has_h0, nc=nc),
        out_shape=(jax.ShapeDtypeStruct((rows, D_MODEL), BF16 if reverse else F32),
                   jax.ShapeDtypeStruct((nseq, npair, q, q), F32)),
        grid=(nseq, nc),
        in_specs=in_specs,
        out_specs=(pl.BlockSpec((q, D_MODEL), lambda s, c: (blk(s, c), 0)),
                   pl.BlockSpec((None, npair, q, q), lambda s, c: (s, 0, 0, 0))),
        scratch_shapes=[pltpu.VMEM((npair, q, q), F32)],
        compiler_params=_params(("parallel", "arbitrary"), 32),
        name="ssd_bwd" if reverse else "ssd_fwd",
    )(*args)


ROUTE_I1, ROUTE_I2, ROUTE_P1, ROUTE_P2, ROUTE_R1, ROUTE_R2 = 0, 1, 2, 3, 4, 5


def _pack_bf16_pairs(v):
    n = v.shape[1] // 2
    bits = lax.bitcast_convert_type(v.astype(BF16).astype(F32), jnp.uint32)
    return lax.bitcast_convert_type((bits[:, :n] >> 16) | bits[:, n:], jnp.int32)


def _unpack_bf16_pairs(p):
    u = lax.bitcast_convert_type(p, jnp.uint32)
    lo = lax.bitcast_convert_type(u << 16, F32)
    hi = lax.bitcast_convert_type(u & jnp.uint32(0xFFFF0000), F32)
    return jnp.concatenate([lo, hi], axis=1)


def _top2_route(logits):
    tm = logits.shape[0]
    lane = lax.broadcasted_iota(jnp.int32, logits.shape, 1).astype(F32)
    lg = jnp.where(lane < N_EXPERTS, logits, -jnp.inf)
    m1 = jnp.max(lg, axis=-1, keepdims=True)
    i1 = jnp.min(jnp.where(lg == m1, lane, float(LANES)), axis=-1, keepdims=True)
    lg2 = jnp.where(lane == i1, -jnp.inf, lg)
    m2 = jnp.max(lg2, axis=-1, keepdims=True)
    i2 = jnp.min(jnp.where(lg2 == m2, lane, float(LANES)), axis=-1, keepdims=True)
    e = jnp.exp(m2 - m1)
    p1 = 1.0 / (1.0 + e)
    chosen = jnp.where(lane == i1, 1.0, jnp.where(lane == i2, 1.0, 0.0))
    earlier = (lax.broadcasted_iota(jnp.int32, (tm, tm), 0) > lax.broadcasted_iota(jnp.int32, (tm, tm), 1))
    ranks = _bdot(jnp.where(earlier, 1.0, 0.0), chosen)
    r1 = jnp.sum(jnp.where(lane == i1, ranks, 0.0), axis=-1, keepdims=True)
    r2 = jnp.sum(jnp.where(lane == i2, ranks, 0.0), axis=-1, keepdims=True)
    rec = jnp.where(lane == ROUTE_I1, i1, jnp.where(lane == ROUTE_I2, i2, 0.0))
    rec = jnp.where(lane == ROUTE_P1, p1, jnp.where(lane == ROUTE_P2, e * p1, rec))
    rec = jnp.where(lane == ROUTE_R1, r1, jnp.where(lane == ROUTE_R2, r2, rec))
    return rec, jnp.sum(chosen, axis=0, keepdims=True)


def _merge_kernel(*refs, seq, tm, moe):
    it = iter(refs)
    attn_ref, ssm_ref, scb_ref, scc_ref, sch_ref = (next(it) for _ in range(5))
    sccp_ref, sccn_ref, schp_ref, schn_ref = (next(it) for _ in range(4))
    g0_ref, g1_ref, g2_ref, x_ref, mod_ref = (next(it) for _ in range(5))
    wa_ref, ws_ref, wc_ref, wm_ref, scw_ref, ng_ref = (next(it) for _ in range(6))
    router_ref = next(it) if moe else None
    xo_ref, h2_ref = next(it), next(it)
    route_ref, cnt_ref = (next(it), next(it)) if moe else (None, None)

    i = pl.program_id(0)
    pos = (lax.broadcasted_iota(jnp.int32, (tm, 1), 0) + i * tm) & (seq - 1)
    u = scc_ref[...] * sch_ref[...]
    up = sccp_ref[SUBLANES - 1:SUBLANES, :] * schp_ref[SUBLANES - 1:SUBLANES, :]
    un = sccn_ref[0:1, :] * schn_ref[0:1, :]
    r = lax.broadcasted_iota(jnp.int32, (tm, 1), 0)
    um1 = jnp.where(pos == 0, 0.0, jnp.where(r == 0, up, pltpu.roll(u, 1, 0)))
    up1 = jnp.where(pos == seq - 1, 0.0, jnp.where(r == tm - 1, un, pltpu.roll(u, tm - 1, 0)))
    w = scw_ref[...]
    sc = scb_ref[...] * (w[0:1, :] * um1 + w[1:2, :] * u + w[2:3, :] * up1)

    merged = _sigmoid(g0_ref[...]) * jnp.dot(attn_ref[...], wa_ref[...], preferred_element_type=F32)
    merged = merged + _sigmoid(g1_ref[...]) * jnp.dot(ssm_ref[...], ws_ref[...], preferred_element_type=F32)
    merged = merged + _sigmoid(g2_ref[...]) * jnp.dot(sc.astype(BF16), wc_ref[...], preferred_element_type=F32)
    mod = mod_ref[0]
    xn = x_ref[...] + mod[2:3, :] * jnp.dot(merged.astype(BF16), wm_ref[...], preferred_element_type=F32)
    xo_ref[...] = xn
    h2 = _rms(xn) * ng_ref[...] * (1.0 + mod[4:5, :]) + mod[3:4, :]
    if moe:
        h2_ref[...] = _pack_bf16_pairs(h2)
        rec, counts = _top2_route(_hdot(h2, router_ref[...]))
        route_ref[...] = rec
        cnt_ref[...] = jnp.broadcast_to(counts, cnt_ref.shape)
    else:
        h2_ref[...] = h2.astype(BF16)


def _merge(attn, ssm, proj, x, mod, lp, seq, tm):
    rows = x.shape[0]
    per_mod = rows // mod.shape[0]
    tm = min(tm, per_mod)
    moe = lp["router"] is not None
    hb = tm // SUBLANES
    prev8 = lambda i: jnp.maximum(i * hb - 1, 0)
    next8 = lambda i: jnp.minimum((i + 1) * hb, rows // SUBLANES - 1)
    tile = lambda col: pl.BlockSpec((tm, D_MODEL), lambda i: (i, col // D_MODEL))
    const = lambda shape: pl.BlockSpec(shape, lambda i: (0, 0))
    in_specs = [tile(0), tile(0), tile(COL_SCB), tile(COL_SCC), tile(COL_SCH),
                pl.BlockSpec((SUBLANES, D_MODEL), lambda i: (prev8(i), COL_SCC // D_MODEL)),
                pl.BlockSpec((SUBLANES, D_MODEL), lambda i: (next8(i), COL_SCC // D_MODEL)),
                pl.BlockSpec((SUBLANES, D_MODEL), lambda i: (prev8(i), COL_SCH // D_MODEL)),
                pl.BlockSpec((SUBLANES, D_MODEL), lambda i: (next8(i), COL_SCH // D_MODEL)),
                tile(COL_G), tile(COL_G + D_MODEL), tile(COL_G + 2 * D_MODEL), tile(0),
                pl.BlockSpec((1, SUBLANES, D_MODEL), lambda i: (i * tm // per_mod, 0, 0)),
                const((D_MODEL, D_MODEL)), const((D_MODEL, D_MODEL)), const((D_MODEL, D_MODEL)),
                const((D_MODEL, D_MODEL)), const((3, D_MODEL)), const((1, D_MODEL))]
    args = [attn, ssm, proj, proj, proj, proj, proj, proj, proj, proj, proj, proj, x, mod,
            lp["w_attn_out"], lp["w_ssm_out"], lp["w_sconv_out"], lp["w_merge"], lp["sconv_w"], lp["norm_ffn_g"]]
    out_shape = [jax.ShapeDtypeStruct((rows, D_MODEL), F32), jax.ShapeDtypeStruct((rows, D_MODEL), BF16)]
    out_specs = [tile(0), tile(0)]
    if moe:
        in_specs.append(const((D_MODEL, LANES)))
        args.append(lp["router"])
        out_shape[1] = jax.ShapeDtypeStruct((rows, D_MODEL // 2), jnp.int32)
        out_specs[1] = pl.BlockSpec((tm, D_MODEL // 2), lambda i: (i, 0))
        out_shape += [jax.ShapeDtypeStruct((rows, LANES), F32),
                      jax.ShapeDtypeStruct((rows // tm * SUBLANES, LANES), F32)]
        out_specs += [pl.BlockSpec((tm, LANES), lambda i: (i, 0)), pl.BlockSpec((SUBLANES, LANES), lambda i: (i, 0))]
    return pl.pallas_call(
        functools.partial(_merge_kernel, seq=seq, tm=tm, moe=moe),
        out_shape=tuple(out_shape),
        grid=(rows // tm,),
        in_specs=in_specs,
        out_specs=tuple(out_specs),
        compiler_params=_params(("parallel",), 56),
        name="merge",
    )(*args)


def _swiglu_acc(h, w1_ref, w3_ref, w2_ref, fs):
    a = _silu(jnp.dot(h, w1_ref[:, fs], preferred_element_type=F32)) * jnp.dot(h, w3_ref[:, fs],
                                                                                preferred_element_type=F32)
    return jnp.dot(a.astype(BF16), w2_ref[fs, :], preferred_element_type=F32)


def _ffn_kernel(h_ref, x_ref, mod_ref, w1_ref, w3_ref, w2_ref, o_ref, acc):
    f = pl.program_id(1)

    @pl.when(f == 0)
    def _():
        acc[...] = jnp.zeros(acc.shape, F32)

    acc[...] += _swiglu_acc(h_ref[...], w1_ref, w3_ref, w2_ref, slice(None))

    @pl.when(f == pl.num_programs(1) - 1)
    def _():
        o_ref[...] = x_ref[...] + mod_ref[0, 5:6, :] * acc[...]


def _ffn(h2, x, mod, w1, w3, w2, tm, tf):
    rows = x.shape[0]
    per_mod = rows // mod.shape[0]
    tm = min(tm, per_mod)
    return pl.pallas_call(
        _ffn_kernel,
        out_shape=jax.ShapeDtypeStruct((rows, D_MODEL), F32),
        grid=(rows // tm, D_FF // tf),
        in_specs=[pl.BlockSpec((tm, D_MODEL), lambda i, f: (i, 0)),
                  pl.BlockSpec((tm, D_MODEL), lambda i, f: (i, 0)),
                  pl.BlockSpec((1, SUBLANES, D_MODEL), lambda i, f: (i * tm // per_mod, 0, 0)),
                  pl.BlockSpec((D_MODEL, tf), lambda i, f: (0, f)),
                  pl.BlockSpec((D_MODEL, tf), lambda i, f: (0, f)),
                  pl.BlockSpec((tf, D_MODEL), lambda i, f: (f, 0))],
        out_specs=pl.BlockSpec((tm, D_MODEL), lambda i, f: (i, 0)),
        scratch_shapes=[pltpu.VMEM((tm, D_MODEL), F32)],
        compiler_params=_params(("parallel", "arbitrary"), 56),
        name="ffn",
    )(h2, x, mod, w1, w3, w2)


SC_CORES = 2
SC_SUBCORES = 16
SC_CHUNK = 128


def _sc_gather(table, idx):
    nw = SC_CORES * SC_SUBCORES
    b, d = idx.shape[0], table.shape[1]
    per_w = b // nw
    assert per_w * nw == b and per_w % SC_CHUNK == 0
    mesh = plsc.VectorSubcoreMesh(core_axis_name="c", subcore_axis_name="s")

    def body(table_hbm, idx_hbm, out_hbm, idx_v, rows_v, sem):
        wid = lax.axis_index("s") * SC_CORES + lax.axis_index("c")

        @pl.loop(0, per_w // SC_CHUNK)
        def _(j):
            off = pl.multiple_of(wid * per_w + j * SC_CHUNK, SC_CHUNK)
            pltpu.sync_copy(idx_hbm.at[pl.ds(off, SC_CHUNK)], idx_v)
            pltpu.async_copy(table_hbm.at[idx_v], rows_v, sem).wait()
            pltpu.sync_copy(rows_v, out_hbm.at[pl.ds(off, SC_CHUNK)])

    return pl.kernel(
        body,
        out_type=jax.ShapeDtypeStruct((b, d), table.dtype),
        mesh=mesh,
        scratch_types=[pltpu.VMEM((SC_CHUNK,), jnp.int32), pltpu.VMEM((SC_CHUNK, d), table.dtype),
                       pltpu.SemaphoreType.DMA],
        name="sc_gather",
    )(table, idx)


def _route_tables(route, counts, tile):
    t = route.shape[0]
    n_mt = counts.shape[0] // SUBLANES
    cnt = counts.reshape(n_mt, SUBLANES, LANES)[:, 0, :N_EXPERTS].astype(jnp.int32)
    incl = jnp.cumsum(cnt, axis=0)
    padded = (incl[-1] + tile - 1) // tile * tile
    gend = jnp.cumsum(padded)
    base = jnp.repeat((gend - padded)[None, :] + incl - cnt, t // n_mt, axis=0)
    e = route[:, ROUTE_I1:ROUTE_I2 + 1].astype(jnp.int32)
    r = route[:, ROUTE_R1:ROUTE_R2 + 1].astype(jnp.int32)
    pick = e[:, :, None] == jnp.arange(N_EXPERTS)[None, None, :]
    slot = jnp.sum(jnp.where(pick, base[:, None, :], 0), axis=-1) + r
    n_slots = 2 * t + N_EXPERTS * tile
    tok_of_slot = jnp.zeros((n_slots,), jnp.int32).at[slot.reshape(-1)].set(jnp.repeat(jnp.arange(t), 2))
    n_valid = gend[-1] // tile
    n_tiles = n_slots // tile
    tile_expert = jnp.sum(jnp.arange(n_tiles)[:, None] * tile >= gend[None, :], axis=1)
    last = jnp.take(tile_expert, jnp.maximum(n_valid - 1, 0))
    tile_expert = jnp.where(jnp.arange(n_tiles) < n_valid, tile_expert, last).astype(jnp.int32)
    return slot.reshape(-1).astype(jnp.int32), tok_of_slot, tile_expert, n_valid.reshape(1).astype(jnp.int32)


def _moe_ffn_kernel(te_ref, nv_ref, h_ref, w1_ref, w3_ref, w2_ref, o_ref):
    i = pl.program_id(0)

    @pl.when(i < nv_ref[0])
    def _():
        h = _unpack_bf16_pairs(h_ref[...]).astype(BF16)
        half = D_FF // 2
        y = _swiglu_acc(h, w1_ref, w3_ref, w2_ref, slice(0, half))
        y = y + _swiglu_acc(h, w1_ref, w3_ref, w2_ref, slice(half, D_FF))
        o_ref[...] = _pack_bf16_pairs(y)

    @pl.when(i >= nv_ref[0])
    def _():
        o_ref[...] = jnp.zeros(o_ref.shape, jnp.int32)


def _moe_ffn(hs, tile_expert, n_valid, w1, w3, w2, tile):
    n_slots = hs.shape[0]
    grid_spec = pltpu.PrefetchScalarGridSpec(
        num_scalar_prefetch=2,
        grid=(n_slots // tile,),
        in_specs=[pl.BlockSpec((tile, D_MODEL // 2), lambda i, te, nv: (i, 0)),
                  pl.BlockSpec((None, D_MODEL, D_FF), lambda i, te, nv: (te[i], 0, 0)),
                  pl.BlockSpec((None, D_MODEL, D_FF), lambda i, te, nv: (te[i], 0, 0)),
                  pl.BlockSpec((None, D_FF, D_MODEL), lambda i, te, nv: (te[i], 0, 0))],
        out_specs=pl.BlockSpec((tile, D_MODEL // 2), lambda i, te, nv: (i, 0)),
    )
    return pl.pallas_call(
        _moe_ffn_kernel,
        out_shape=jax.ShapeDtypeStruct((n_slots, D_MODEL // 2), jnp.int32),
        grid_spec=grid_spec,
        compiler_params=_params(("arbitrary",), 56),
        name="moe_ffn",
    )(tile_expert, n_valid, hs, w1, w3, w2)


def _combine_kernel(x_ref, y_ref, route_ref, mod_ref, o_ref):
    p1 = route_ref[:, ROUTE_P1:ROUTE_P1 + 1]
    p2 = route_ref[:, ROUTE_P2:ROUTE_P2 + 1]
    half = D_MODEL // 2
    f = p1 * _unpack_bf16_pairs(y_ref[:, :half]) + p2 * _unpack_bf16_pairs(y_ref[:, half:])
    o_ref[...] = x_ref[...] + mod_ref[0, 5:6, :] * f


def _combine(x, yg, route, mod, tm):
    rows = x.shape[0]
    per_mod = rows // mod.shape[0]
    tm = min(tm, per_mod)
    return pl.pallas_call(
        _combine_kernel,
        out_shape=jax.ShapeDtypeStruct((rows, D_MODEL), F32),
        grid=(rows // tm,),
        in_specs=[pl.BlockSpec((tm, D_MODEL), lambda i: (i, 0)),
                  pl.BlockSpec((tm, D_MODEL), lambda i: (i, 0)),
                  pl.BlockSpec((tm, LANES), lambda i: (i, 0)),
                  pl.BlockSpec((1, SUBLANES, D_MODEL), lambda i: (i * tm // per_mod, 0, 0))],
        out_specs=pl.BlockSpec((tm, D_MODEL), lambda i: (i, 0)),
        compiler_params=_params(("parallel",), 32),
        name="moe_combine",
    )(x, yg, route, mod)


def _moe(h2p, x, mod, route, counts, w1, w3, w2, tile):
    rows = x.shape[0]
    slot, tok_of_slot, tile_expert, n_valid = _route_tables(route, counts, tile)
    hs = _sc_gather(h2p, tok_of_slot)
    ys = _moe_ffn(hs, tile_expert, n_valid, w1, w3, w2, tile)
    yg = _sc_gather(ys, slot).reshape(rows, D_MODEL)
    return _combine(x, yg, route, mod, tm=512)


def _layer_params(l, w_in, q_norm_g, k_norm_g, w_attn_out, ssm_conv_w, ssm_conv_b, ssm_dt_bias, ssm_a_log,
                  ssm_d, ssm_norm_g, w_ssm_out, sconv_w, w_sconv_out, w_merge, norm_mix_g, norm_ffn_g,
                  ffn_w1, ffn_w3, ffn_w2, moe_router, moe_w1, moe_w3, moe_w2):
    w = w_in[l]
    seg = lambda a, n: w[:, a:a + n]
    w_perm = jnp.concatenate(
        [seg(0, 1024), seg(1536, 1024), seg(4128, 1024), seg(5152, 1024), seg(6176, 1024), seg(7200, 3072),
         seg(2560, 1024), seg(3584, 512), seg(1024, 256), seg(1280, 256), seg(4096, 32),
         jnp.zeros((D_MODEL, N_PROJ - COL_DT - 2 * SSM_HEADS), w.dtype)], axis=1).astype(BF16)
    row = lambda v: v.reshape(1, -1)
    lane_row = lambda v, off: jnp.zeros((1, LANES), F32).at[0, off:off + SSM_HEADS].set(v)
    head_of_col = jnp.arange(D_MODEL) // SSM_HEADDIM
    expand = [(jnp.arange(LANES)[:, None] == head_of_col[None, :] + off).astype(F32) for off in (0, SSM_HEADS)]
    lp = {
        "w_in": w_perm,
        "norm_mix_g": row(norm_mix_g[l]), "norm_ffn_g": row(norm_ffn_g[l]),
        "q_norm_g": row(q_norm_g[l]), "k_norm_g": row(k_norm_g[l]),
        "w_attn_out": w_attn_out[l].astype(BF16), "w_ssm_out": w_ssm_out[l].astype(BF16),
        "w_sconv_out": w_sconv_out[l].astype(BF16), "w_merge": w_merge[l].astype(BF16),
        "cw_x": ssm_conv_w[l][:, :D_MODEL], "cw_bc": ssm_conv_w[l][:, D_MODEL:],
        "cb_x": row(ssm_conv_b[l][:D_MODEL]), "cb_bc": row(ssm_conv_b[l][D_MODEL:]),
        "alog": [lane_row(ssm_a_log[l, d], d * SSM_HEADS) for d in range(2)],
        "dtb": [lane_row(ssm_dt_bias[l, d], d * SSM_HEADS) for d in range(2)],
        "expand": expand,
        "d_row": row(jnp.repeat(ssm_d[l], SSM_HEADDIM)),
        "ssm_norm_g": row(ssm_norm_g[l]),
        "sconv_w": sconv_w[l],
    }
    if l % 2 == 0:
        lp["router"] = None
        lp["ffn"] = tuple(t[l // 2].astype(BF16) for t in (ffn_w1, ffn_w3, ffn_w2))
    else:
        lp["router"] = jnp.pad(moe_router[l // 2], ((0, 0), (0, LANES - N_EXPERTS)))
        lp["ffn"] = tuple(t[l // 2].astype(BF16) for t in (moe_w1, moe_w3, moe_w2))
    return lp


def _rope_tables(seq):
    t = jnp.arange(seq)
    n = HEAD_DIM // 4
    inv = ROPE_THETA ** (-jnp.arange(n, dtype=F32) / n)
    ang_r = (t // GRID_W).astype(F32)[:, None] * inv[None, :]
    ang_c = (t % GRID_W).astype(F32)[:, None] * inv[None, :]
    zero = jnp.zeros_like(ang_r)
    cos = jnp.concatenate([jnp.cos(ang_r)] * 2 + [jnp.cos(ang_c)] * 2, axis=1)
    sa = jnp.concatenate([-jnp.sin(ang_r), zero, -jnp.sin(ang_c), zero], axis=1)
    sb = jnp.concatenate([zero, jnp.sin(ang_r), zero, jnp.sin(ang_c)], axis=1)
    return cos, sa, sb


def _trunk(x, mod, lp, nseq, seq, lat):
    rows = nseq * seq
    proj = _inproj(x, mod, lp["norm_mix_g"], lp["w_in"], tm=min(1024, rows), tn=1152)
    if lat is None:
        attn, k_new, v_new = _attn_ctx(proj, lp["q_norm_g"], lp["k_norm_g"], nseq, seq)
        h0_f = h0_b = None
    else:
        cache_k, cache_v, h0_f, h0_b, tables = lat
        qr, kall, vall = _prep_lat(proj, cache_k, cache_v, *tables, lp["q_norm_g"], lp["k_norm_g"], nseq, seq)
        attn = _attn_lat(qr, kall, vall, tq=256).reshape(rows, D_MODEL)
        k_new = v_new = None
    yf, h_f = _ssd_pass(proj, lp, 0, nseq, seq, h0_f, None)
    ssm, h_b = _ssd_pass(proj, lp, 1, nseq, seq, h0_b, yf)
    outs = _merge(attn, ssm, proj, x, mod, lp, seq, tm=256)
    w1, w3, w2 = lp["ffn"]
    if lp["router"] is None:
        x = _ffn(outs[1], outs[0], mod, w1, w3, w2, tm=512, tf=1408)
    else:
        x = _moe(outs[1], outs[0], mod, outs[2], outs[3], w1, w3, w2, tile=512)
    return x, k_new, v_new, h_f, h_b


def kernel(x_prompt, x_sample, cache_k, cache_v, state_ssm_fwd, state_ssm_bwd, c, c_ctx, w_mod, b_mod, norm_mix_g, norm_ffn_g, w_in, q_norm_g, k_norm_g, w_attn_out, ssm_conv_w, ssm_conv_b, ssm_dt_bias, ssm_a_log, ssm_d, ssm_norm_g, w_ssm_out, sconv_w, w_sconv_out, w_merge, ffn_w1, ffn_w3, ffn_w2, moe_router, moe_w1, moe_w3, moe_w2):
    batch, seq, _ = x_prompt.shape
    nb, dseq, _ = x_sample.shape
    depth = w_in.shape[0]
    past = cache_k.shape[2]
    kvw = N_KV_HEADS * HEAD_DIM
    npair = SSM_HEADS // 2
    assert nb + 1 <= SUBLANES

    cond = jnp.zeros((SUBLANES, D_MODEL), F32).at[0].set(c_ctx).at[1:1 + nb].set(c)
    tables = _rope_tables(dseq)
    y_ctx = x_prompt.reshape(batch * seq, D_MODEL)
    y_lat = x_sample.reshape(nb * dseq, D_MODEL)
    ks, vs, hfs, hbs = [], [], [], []
    for l in range(depth):
        lp = _layer_params(l, w_in, q_norm_g, k_norm_g, w_attn_out, ssm_conv_w, ssm_conv_b, ssm_dt_bias,
                           ssm_a_log, ssm_d, ssm_norm_g, w_ssm_out, sconv_w, w_sconv_out, w_merge,
                           norm_mix_g, norm_ffn_g, ffn_w1, ffn_w3, ffn_w2, moe_router, moe_w1, moe_w3, moe_w2)
        m = _modulation(cond, w_mod[l], b_mod[l].reshape(1, -1)).reshape(SUBLANES, 6, D_MODEL)
        m = jnp.pad(m, ((0, 0), (0, SUBLANES - 6), (0, 0)))
        y_ctx, k_l, v_l, hf_l, hb_l = _trunk(y_ctx, m[0:1], lp, batch, seq, None)
        ks.append(k_l.reshape(batch, seq, N_KV_HEADS, HEAD_DIM))
        vs.append(v_l.reshape(batch, seq, N_KV_HEADS, HEAD_DIM))
        hfs.append(hf_l.reshape(batch, SSM_HEADS, SSM_HEADDIM, SSM_STATE))
        hbs.append(hb_l.reshape(batch, SSM_HEADS, SSM_HEADDIM, SSM_STATE))
        lat = (cache_k[:, l].reshape(nb, past, kvw), cache_v[:, l].reshape(nb, past, kvw),
               state_ssm_fwd[:, l].reshape(nb, npair, SSM_CHUNK, SSM_STATE),
               state_ssm_bwd[:, l].reshape(nb, npair, SSM_CHUNK, SSM_STATE), tables)
        y_lat = _trunk(y_lat, m[1:1 + nb], lp, nb, dseq, lat)[0]
    return (y_ctx.reshape(batch, seq, D_MODEL), y_lat.reshape(nb, dseq, D_MODEL),
            jnp.stack(ks, axis=1), jnp.stack(vs, axis=1), jnp.stack(hfs, axis=1), jnp.stack(hbs, axis=1))
```

```python
import functools

import jax
import jax.numpy as jnp
from jax import lax
from jax.experimental import pallas as pl
from jax.experimental.pallas import tpu as pltpu
from jax.experimental.pallas import tpu_sc as plsc

F32 = jnp.float32
BF16 = jnp.bfloat16
HIGHEST = lax.Precision.HIGHEST

D_MODEL = 1024
HEAD_DIM = 128
N_HEADS = 8
N_KV_HEADS = 2
Q_PER_KV = N_HEADS // N_KV_HEADS
ROPE_THETA = 10000.0
GRID_W = 64
SSM_HEADS = 16
SSM_HEADDIM = 64
SSM_STATE = 128
SSM_CHUNK = 128
D_FF = 2816
N_EXPERTS = 8
EPS = 1e-6
Q_SCALE = 1.4426950408889634 * HEAD_DIM ** -0.5

COL_Q, COL_Z, COL_SCB, COL_SCC, COL_SCH, COL_G = 0, 1024, 2048, 3072, 4096, 5120
COL_XS, COL_BC, COL_K, COL_V, COL_DT = 8192, 9216, 9728, 9984, 10240
N_PROJ = 10368
LANES = 128
SUBLANES = 8
HALO = 16
MIB = 1024 * 1024


def _params(sem, vmem_mib):
    return pltpu.CompilerParams(dimension_semantics=sem, vmem_limit_bytes=vmem_mib * MIB)


def _sigmoid(t):
    return 1.0 / (1.0 + jnp.exp(-t))


def _silu(t):
    return t * _sigmoid(t)


def _rms(t):
    return t * lax.rsqrt(jnp.mean(t * t, axis=-1, keepdims=True) + EPS)


def _bdot(a, b):
    return jnp.dot(a.astype(BF16), b.astype(BF16), preferred_element_type=F32)


def _hdot(a, b):
    return jnp.dot(a, b, precision=HIGHEST, preferred_element_type=F32)


def _split3(t):
    a = t.astype(BF16)
    r = t - a.astype(F32)
    b = r.astype(BF16)
    return a, b, (r - b.astype(F32)).astype(BF16)


def _mod_kernel(c_ref, w_ref, b_ref, o_ref):
    o_ref[...] = _hdot(_silu(c_ref[...]), w_ref[...]) + b_ref[...]


def _modulation(cond8, w, b):
    n = w.shape[1] // D_MODEL
    return pl.pallas_call(
        _mod_kernel,
        out_shape=jax.ShapeDtypeStruct((SUBLANES, w.shape[1]), F32),
        grid=(n,),
        in_specs=[pl.BlockSpec((SUBLANES, D_MODEL), lambda j: (0, 0)),
                  pl.BlockSpec((D_MODEL, D_MODEL), lambda j: (0, j)),
                  pl.BlockSpec((1, D_MODEL), lambda j: (0, j))],
        out_specs=pl.BlockSpec((SUBLANES, D_MODEL), lambda j: (0, j)),
        compiler_params=_params(("parallel",), 32),
        name="modulation",
    )(cond8, w, b)


def _inproj_kernel(x_ref, mod_ref, g_ref, w_ref, o_ref, h_scr):
    @pl.when(pl.program_id(1) == 0)
    def _():
        h = _rms(x_ref[...]) * g_ref[...]
        h = h * (1.0 + mod_ref[0, 1:2, :]) + mod_ref[0, 0:1, :]
        h_scr[...] = h.astype(BF16)

    o_ref[...] = jnp.dot(h_scr[...], w_ref[...], preferred_element_type=F32).astype(o_ref.dtype)


def _inproj(x, mod, g, w, tm, tn):
    rows = x.shape[0]
    per_mod = rows // mod.shape[0]
    tm = min(tm, per_mod)
    return pl.pallas_call(
        _inproj_kernel,
        out_shape=jax.ShapeDtypeStruct((rows, N_PROJ), BF16),
        grid=(rows // tm, N_PROJ // tn),
        in_specs=[pl.BlockSpec((tm, D_MODEL), lambda i, j: (i, 0)),
                  pl.BlockSpec((1, SUBLANES, D_MODEL), lambda i, j: (i * tm // per_mod, 0, 0)),
                  pl.BlockSpec((1, D_MODEL), lambda i, j: (0, 0)),
                  pl.BlockSpec((D_MODEL, tn), lambda i, j: (0, j))],
        out_specs=pl.BlockSpec((tm, tn), lambda i, j: (i, j)),
        scratch_shapes=[pltpu.VMEM((tm, D_MODEL), BF16)],
        compiler_params=_params(("parallel", "arbitrary"), 48),
        name="inproj",
    )(x, mod, g, w)


def _softmax_pv(s, vb):
    m = jnp.max(s, axis=-1, keepdims=True)
    p = jnp.exp2(s - m)
    l = jnp.sum(p, axis=-1, keepdims=True)
    return jnp.dot(p.astype(BF16), vb, preferred_element_type=F32) / l


def _qk(qb, kb):
    return lax.dot_general(qb, kb, (((1,), (1,)), ((), ())), preferred_element_type=F32)


def _attn_ctx_kernel(q_ref, k_ref, v_ref, qg_ref, kg_ref, a_ref, ko_ref, vo_ref):
    v = v_ref[...].astype(F32)
    vo_ref[...] = v
    for hk in range(N_KV_HEADS):
        ks = slice(hk * HEAD_DIM, (hk + 1) * HEAD_DIM)
        kn = _rms(k_ref[:, ks].astype(F32)) * kg_ref[...]
        ko_ref[:, ks] = kn
        knb = kn.astype(BF16)
        vb = v[:, ks].astype(BF16)
        for g in range(Q_PER_KV):
            hs = slice((hk * Q_PER_KV + g) * HEAD_DIM, (hk * Q_PER_KV + g + 1) * HEAD_DIM)
            qn = _rms(q_ref[:, hs].astype(F32)) * (qg_ref[...] * Q_SCALE)
            s = _qk(qn.astype(BF16), knb)
            a_ref[:, hs] = _softmax_pv(s, vb).astype(BF16)


def _attn_ctx(proj, qg, kg, nseq, seq):
    kvw = N_KV_HEADS * HEAD_DIM
    return pl.pallas_call(
        _attn_ctx_kernel,
        out_shape=(jax.ShapeDtypeStruct((nseq * seq, D_MODEL), BF16),
                   jax.ShapeDtypeStruct((nseq, seq, kvw), F32),
                   jax.ShapeDtypeStruct((nseq, seq, kvw), F32)),
        grid=(nseq,),
        in_specs=[pl.BlockSpec((seq, D_MODEL), lambda b: (b, COL_Q // D_MODEL)),
                  pl.BlockSpec((seq, kvw), lambda b: (b, COL_K // kvw)),
                  pl.BlockSpec((seq, kvw), lambda b: (b, COL_V // kvw)),
                  pl.BlockSpec((1, HEAD_DIM), lambda b: (0, 0)),
                  pl.BlockSpec((1, HEAD_DIM), lambda b: (0, 0))],
        out_specs=(pl.BlockSpec((seq, D_MODEL), lambda b: (b, 0)),
                   pl.BlockSpec((None, seq, kvw), lambda b: (b, 0, 0)),
                   pl.BlockSpec((None, seq, kvw), lambda b: (b, 0, 0))),
        compiler_params=_params(("parallel",), 32),
        name="attn_ctx",
    )(proj, proj, proj, qg, kg)


def _rope(t, cos, sa, sb):
    return t * cos + pltpu.roll(t, 96, 1) * sa + pltpu.roll(t, 32, 1) * sb


def _prep_lat_kernel(q_ref, k_ref, v_ref, ck_ref, cv_ref, cos_ref, sa_ref, sb_ref, qg_ref, kg_ref,
                     qo_ref, ko_ref, vo_ref):
    t = pl.program_id(1)

    @pl.when(t == 0)
    def _():
        ko_ref[...] = ck_ref[...].astype(BF16)
        vo_ref[...] = cv_ref[...].astype(BF16)

    @pl.when(t > 0)
    def _():
        cos, sa, sb = cos_ref[...], sa_ref[...], sb_ref[...]
        vo_ref[...] = v_ref[...]
        for h in range(N_KV_HEADS):
            hs = slice(h * HEAD_DIM, (h + 1) * HEAD_DIM)
            ko_ref[:, hs] = _rope(_rms(k_ref[:, hs].astype(F32)) * kg_ref[...], cos, sa, sb).astype(BF16)
        for h in range(N_HEADS):
            hs = slice(h * HEAD_DIM, (h + 1) * HEAD_DIM)
            qn = _rms(q_ref[:, hs].astype(F32)) * (qg_ref[...] * Q_SCALE)
            qo_ref[:, hs] = _rope(qn, cos, sa, sb).astype(BF16)


def _prep_lat(proj, cache_k, cache_v, cos, sa, sb, qg, kg, nb, seq):
    past = cache_k.shape[1]
    tr = past
    kvw = N_KV_HEADS * HEAD_DIM
    nt = seq // tr
    prev = lambda t: jnp.maximum(t - 1, 0)
    return pl.pallas_call(
        _prep_lat_kernel,
        out_shape=(jax.ShapeDtypeStruct((nb, seq, D_MODEL), BF16),
                   jax.ShapeDtypeStruct((nb, past + seq, kvw), BF16),
                   jax.ShapeDtypeStruct((nb, past + seq, kvw), BF16)),
        grid=(nb, nt + 1),
        in_specs=[pl.BlockSpec((tr, D_MODEL), lambda b, t: (b * nt + prev(t), COL_Q // D_MODEL)),
                  pl.BlockSpec((tr, kvw), lambda b, t: (b * nt + prev(t), COL_K // kvw)),
                  pl.BlockSpec((tr, kvw), lambda b, t: (b * nt + prev(t), COL_V // kvw)),
                  pl.BlockSpec((None, past, kvw), lambda b, t: (b, 0, 0)),
                  pl.BlockSpec((None, past, kvw), lambda b, t: (b, 0, 0)),
                  pl.BlockSpec((tr, HEAD_DIM), lambda b, t: (prev(t), 0)),
                  pl.BlockSpec((tr, HEAD_DIM), lambda b, t: (prev(t), 0)),
                  pl.BlockSpec((tr, HEAD_DIM), lambda b, t: (prev(t), 0)),
                  pl.BlockSpec((1, HEAD_DIM), lambda b, t: (0, 0)),
                  pl.BlockSpec((1, HEAD_DIM), lambda b, t: (0, 0))],
        out_specs=(pl.BlockSpec((None, tr, D_MODEL), lambda b, t: (b, prev(t), 0)),
                   pl.BlockSpec((None, tr, kvw), lambda b, t: (b, t, 0)),
                   pl.BlockSpec((None, tr, kvw), lambda b, t: (b, t, 0))),
        compiler_params=_params(("parallel", "arbitrary"), 32),
        name="prep_lat",
    )(proj, proj, proj, cache_k, cache_v, cos, sa, sb, qg, kg)


def _attn_lat_kernel(q_ref, k_ref, v_ref, o_ref):
    kb = k_ref[...]
    vb = v_ref[...]
    for g in range(Q_PER_KV):
        hs = slice(g * HEAD_DIM, (g + 1) * HEAD_DIM)
        s = _qk(q_ref[:, hs], kb)
        o_ref[:, hs] = _softmax_pv(s, vb).astype(BF16)


def _attn_lat(qr, kall, vall, tq):
    nb, seq, _ = qr.shape
    nk = kall.shape[1]
    gw = Q_PER_KV * HEAD_DIM
    return pl.pallas_call(
        _attn_lat_kernel,
        out_shape=jax.ShapeDtypeStruct((nb, seq, D_MODEL), BF16),
        grid=(nb, N_KV_HEADS, seq // tq),
        in_specs=[pl.BlockSpec((None, tq, gw), lambda b, h, i: (b, i, h)),
                  pl.BlockSpec((None, nk, HEAD_DIM), lambda b, h, i: (b, 0, h)),
                  pl.BlockSpec((None, nk, HEAD_DIM), lambda b, h, i: (b, 0, h))],
        out_specs=pl.BlockSpec((None, tq, gw), lambda b, h, i: (b, i, h)),
        compiler_params=_params(("parallel", "parallel", "arbitrary"), 48),
        name="attn_lat",
    )(qr, kall, vall)


def _conv3_rows(x, prev_row, next_row, w, tm):
    r = lax.broadcasted_iota(jnp.int32, (tm, 1), 0)
    xm1 = jnp.where(r == 0, prev_row, pltpu.roll(x, 1, 0))
    xp1 = jnp.where(r == tm - 1, next_row, pltpu.roll(x, tm - 1, 0))
    return w[0:1, :] * xm1 + w[1:2, :] * x + w[2:3, :] * xp1


def _ssd_kernel(*refs, reverse, has_h0, nc):
    it = iter(refs)
    xs_ref, bc_ref, dt_ref, xsp_ref, xsn_ref, bcp_ref, bcn_ref = (next(it) for _ in range(7))
    cwx_ref, cbx_ref, cwb_ref, cbb_ref, alog_ref, dtb_ref, e_ref = (next(it) for _ in range(7))
    h0_ref = next(it) if has_h0 else None
    if reverse:
        z_ref, yf_ref, d_ref, g_ref = (next(it) for _ in range(4))
    y_ref, hout_ref, ht_scr = next(it), next(it), next(it)

    q = SSM_CHUNK
    npair = SSM_HEADS // 2
    c = pl.program_id(1)
    cc = nc - 1 - c if reverse else c
    first = cc == 0
    last = cc == nc - 1

    @pl.when(c == 0)
    def _():
        for p in range(npair):
            if has_h0:
                ht_scr[p] = h0_ref[p].T
            else:
                ht_scr[p] = jnp.zeros((q, q), F32)

    def conv_silu(x_ref_, p_ref_, n_ref_, w_ref_, b_ref_):
        pr = jnp.where(first, 0.0, p_ref_[HALO - 1:HALO, :].astype(F32))
        nx = jnp.where(last, 0.0, n_ref_[0:1, :].astype(F32))
        return _silu(_conv3_rows(x_ref_[...].astype(F32), pr, nx, w_ref_[...], q) + b_ref_[...])

    xs = conv_silu(xs_ref, xsp_ref, xsn_ref, cwx_ref, cbx_ref)
    bc = conv_silu(bc_ref, bcp_ref, bcn_ref, cwb_ref, cbb_ref)

    off = SSM_HEADS if reverse else 0
    dt = jax.nn.softplus(dt_ref[...].astype(F32) + dtb_ref[...])
    da = dt * (-jnp.exp(alog_ref[...]))
    ii = lax.broadcasted_iota(jnp.int32, (q, q), 0)
    jj = lax.broadcasted_iota(jnp.int32, (q, q), 1)
    tri = (jj >= ii) if reverse else (ii >= jj)
    tri_b = jnp.where(tri, 1.0, 0.0).astype(BF16)
    tri_tb = jnp.where((ii >= jj) if reverse else (jj >= ii), 1.0, 0.0).astype(BF16)
    acs3 = jnp.dot(tri_b, jnp.concatenate(_split3(da), axis=1), preferred_element_type=F32)
    acs = acs3[:, :LANES] + acs3[:, LANES:2 * LANES] + acs3[:, 2 * LANES:]
    acs_t = jnp.dot(jnp.concatenate(_split3(da.T), axis=1), jnp.concatenate([tri_tb] * 3, axis=0),
                    preferred_element_type=F32)
    spread = jnp.dot(jnp.concatenate(_split3(jnp.concatenate([dt, acs], axis=0)), axis=1), e_ref[...],
                     preferred_element_type=F32)
    acs_x = spread[q:, :]
    tot_x = acs_x[0:1, :] if reverse else acs_x[q - 1:q, :]
    xdt = xs * spread[:q, :]
    xw = xdt * jnp.exp(tot_x - acs_x)
    eacs_x = jnp.exp(acs_x)
    etot_x = jnp.exp(tot_x)
    lo = lax.broadcasted_iota(jnp.int32, (q, LANES), 1) < SSM_HEADDIM

    parts = []
    for g in range(2):
        bg = bc[:, g * q:(g + 1) * q]
        cg = bc[:, (2 + g) * q:(3 + g) * q].astype(BF16)
        cb = _qk(cg, bg.astype(BF16))
        bgt = bg.T.astype(BF16)
        for pp in range(npair // 2):
            p = g * (npair // 2) + pp
            ps = slice(p * LANES, (p + 1) * LANES)
            ms = []
            for hh in range(2):
                ln = off + 2 * p + hh
                dec = jnp.where(tri, jnp.exp(acs[:, ln:ln + 1] - acs_t[ln:ln + 1, :]), 0.0)
                ms.append((cb * dec).astype(BF16))
            m = jnp.concatenate(ms, axis=1)
            xp = xdt[:, ps]
            bd = jnp.concatenate([jnp.where(lo, xp, 0.0), jnp.where(lo, 0.0, xp)], axis=0).astype(BF16)
            ht = ht_scr[p]
            yp = jnp.dot(m, bd, preferred_element_type=F32)
            yp = yp + jnp.dot(cg, ht.astype(BF16), preferred_element_type=F32) * eacs_x[:, ps]
            ht_scr[p] = etot_x[:, ps] * ht + jnp.dot(bgt, xw[:, ps].astype(BF16), preferred_element_type=F32)
            parts.append(yp)
    y = jnp.concatenate(parts, axis=1)

    if reverse:
        y = y + yf_ref[...] + d_ref[...] * xs
        y = y * _silu(z_ref[...].astype(F32))
        y_ref[...] = (_rms(y) * g_ref[...]).astype(y_ref.dtype)
    else:
        y_ref[...] = y

    @pl.when(c == nc - 1)
    def _():
        for p in range(npair):
            hout_ref[p] = ht_scr[p].T


def _ssd_pass(proj, lp, d, nseq, seq, h0, yf):
    reverse = yf is not None
    has_h0 = h0 is not None
    q = SSM_CHUNK
    nc = seq // q
    rows = nseq * seq
    npair = SSM_HEADS // 2
    hb = q // HALO
    bcw = 4 * SSM_STATE
    chunk = (lambda c: nc - 1 - c) if reverse else (lambda c: c)
    blk = lambda s, c: s * nc + chunk(c)
    prev8 = lambda s, c: jnp.maximum(blk(s, c) * hb - 1, 0)
    next8 = lambda s, c: jnp.minimum((blk(s, c) + 1) * hb, rows // HALO - 1)
    const = lambda s, c: (0, 0)
    in_specs = [
        pl.BlockSpec((q, D_MODEL), lambda s, c: (blk(s, c), COL_XS // D_MODEL)),
        pl.BlockSpec((q, bcw), lambda s, c: (blk(s, c), COL_BC // bcw)),
        pl.BlockSpec((q, LANES), lambda s, c: (blk(s, c), COL_DT // LANES)),
        pl.BlockSpec((HALO, D_MODEL), lambda s, c: (prev8(s, c), COL_XS // D_MODEL)),
        pl.BlockSpec((HALO, D_MODEL), lambda s, c: (next8(s, c), COL_XS // D_MODEL)),
        pl.BlockSpec((HALO, bcw), lambda s, c: (prev8(s, c), COL_BC // bcw)),
        pl.BlockSpec((HALO, bcw), lambda s, c: (next8(s, c), COL_BC // bcw)),
        pl.BlockSpec((3, D_MODEL), const), pl.BlockSpec((1, D_MODEL), const),
        pl.BlockSpec((3, bcw), const), pl.BlockSpec((1, bcw), const),
        pl.BlockSpec((1, LANES), const), pl.BlockSpec((1, LANES), const),
        pl.BlockSpec((3 * LANES, D_MODEL), const),
    ]
    args = [proj] * 7 + [lp["cw_x"], lp["cb_x"], lp["cw_bc"], lp["cb_bc"], lp["alog"][d], lp["dtb"][d],
                         lp["expand"][d]]
    if has_h0:
        in_specs.append(pl.BlockSpec((None, npair, q, q), lambda s, c: (s, 0, 0, 0)))
        args.append(h0)
    if reverse:
        in_specs += [pl.BlockSpec((q, D_MODEL), lambda s, c: (blk(s, c), COL_Z // D_MODEL)),
                     pl.BlockSpec((q, D_MODEL), lambda s, c: (blk(s, c), 0)),
                     pl.BlockSpec((1, D_MODEL), const), pl.BlockSpec((1, D_MODEL), const)]
        args += [proj, yf, lp["d_row"], lp["ssm_norm_g"]]
    return pl.pallas_call(
        functools.partial(_ssd_kernel, reverse=reverse, has_h0=has_h0, nc=nc),
        out_shape=(jax.ShapeDtypeStruct((rows, D_MODEL), BF16 if reverse else F32),
                   jax.ShapeDtypeStruct((nseq, npair, q, q), F32)),
        grid=(nseq, nc),
        in_specs=in_specs,
        out_specs=(pl.BlockSpec((q, D_MODEL), lambda s, c: (blk(s, c), 0)),
                   pl.BlockSpec((None, npair, q, q), lambda s, c: (s, 0, 0, 0))),
        scratch_shapes=[pltpu.VMEM((npair, q, q), F32)],
        compiler_params=_params(("parallel", "arbitrary"), 32),
        name="ssd_bwd" if reverse else "ssd_fwd",
    )(*args)


ROUTE_I1, ROUTE_I2, ROUTE_P1, ROUTE_P2, ROUTE_R1, ROUTE_R2 = 0, 1, 2, 3, 4, 5


def _pack_bf16_pairs(v):
    n = v.shape[1] // 2
    bits = lax.bitcast_convert_type(v.astype(BF16).astype(F32), jnp.uint32)
    return lax.bitcast_convert_type((bits[:, :n] >> 16) | bits[:, n:], jnp.int32)


def _unpack_bf16_pairs(p):
    u = lax.bitcast_convert_type(p, jnp.uint32)
    lo = lax.bitcast_convert_type(u << 16, F32)
    hi = lax.bitcast_convert_type(u & jnp.uint32(0xFFFF0000), F32)
    return jnp.concatenate([lo, hi], axis=1)


def _top2_route(logits):
    tm = logits.shape[0]
    lane = lax.broadcasted_iota(jnp.int32, logits.shape, 1).astype(F32)
    lg = jnp.where(lane < N_EXPERTS, logits, -jnp.inf)
    m1 = jnp.max(lg, axis=-1, keepdims=True)
    i1 = jnp.min(jnp.where(lg == m1, lane, float(LANES)), axis=-1, keepdims=True)
    lg2 = jnp.where(lane == i1, -jnp.inf, lg)
    m2 = jnp.max(lg2, axis=-1, keepdims=True)
    i2 = jnp.min(jnp.where(lg2 == m2, lane, float(LANES)), axis=-1, keepdims=True)
    e = jnp.exp(m2 - m1)
    p1 = 1.0 / (1.0 + e)
    chosen = jnp.where(lane == i1, 1.0, jnp.where(lane == i2, 1.0, 0.0))
    earlier = (lax.broadcasted_iota(jnp.int32, (tm, tm), 0) > lax.broadcasted_iota(jnp.int32, (tm, tm), 1))
    ranks = _bdot(jnp.where(earlier, 1.0, 0.0), chosen)
    r1 = jnp.sum(jnp.where(lane == i1, ranks, 0.0), axis=-1, keepdims=True)
    r2 = jnp.sum(jnp.where(lane == i2, ranks, 0.0), axis=-1, keepdims=True)
    rec = jnp.where(lane == ROUTE_I1, i1, jnp.where(lane == ROUTE_I2, i2, 0.0))
    rec = jnp.where(lane == ROUTE_P1, p1, jnp.where(lane == ROUTE_P2, e * p1, rec))
    rec = jnp.where(lane == ROUTE_R1, r1, jnp.where(lane == ROUTE_R2, r2, rec))
    return rec, jnp.sum(chosen, axis=0, keepdims=True)


def _merge_kernel(*refs, seq, tm, moe):
    it = iter(refs)
    attn_ref, ssm_ref, scb_ref, scc_ref, sch_ref = (next(it) for _ in range(5))
    sccp_ref, sccn_ref, schp_ref, schn_ref = (next(it) for _ in range(4))
    g0_ref, g1_ref, g2_ref, x_ref, mod_ref = (next(it) for _ in range(5))
    wa_ref, ws_ref, wc_ref, wm_ref, scw_ref, ng_ref = (next(it) for _ in range(6))
    router_ref = next(it) if moe else None
    xo_ref, h2_ref = next(it), next(it)
    route_ref, cnt_ref = (next(it), next(it)) if moe else (None, None)

    i = pl.program_id(0)
    pos = (lax.broadcasted_iota(jnp.int32, (tm, 1), 0) + i * tm) & (seq - 1)
    u = scc_ref[...].astype(F32) * sch_ref[...].astype(F32)
    up = sccp_ref[HALO - 1:HALO, :].astype(F32) * schp_ref[HALO - 1:HALO, :].astype(F32)
    un = sccn_ref[0:1, :].astype(F32) * schn_ref[0:1, :].astype(F32)
    r = lax.broadcasted_iota(jnp.int32, (tm, 1), 0)
    um1 = jnp.where(pos == 0, 0.0, jnp.where(r == 0, up, pltpu.roll(u, 1, 0)))
    up1 = jnp.where(pos == seq - 1, 0.0, jnp.where(r == tm - 1, un, pltpu.roll(u, tm - 1, 0)))
    w = scw_ref[...]
    sc = scb_ref[...].astype(F32) * (w[0:1, :] * um1 + w[1:2, :] * u + w[2:3, :] * up1)

    gate = lambda g_ref: _sigmoid(g_ref[...].astype(F32))
    merged = gate(g0_ref) * jnp.dot(attn_ref[...], wa_ref[...], preferred_element_type=F32)
    merged = merged + gate(g1_ref) * jnp.dot(ssm_ref[...], ws_ref[...], preferred_element_type=F32)
    merged = merged + gate(g2_ref) * jnp.dot(sc.astype(BF16), wc_ref[...], preferred_element_type=F32)
    mod = mod_ref[0]
    xn = x_ref[...] + mod[2:3, :] * jnp.dot(merged.astype(BF16), wm_ref[...], preferred_element_type=F32)
    xo_ref[...] = xn
    h2 = _rms(xn) * ng_ref[...] * (1.0 + mod[4:5, :]) + mod[3:4, :]
    if moe:
        h2_ref[...] = _pack_bf16_pairs(h2)
        rec, counts = _top2_route(_hdot(h2, router_ref[...]))
        route_ref[...] = rec
        cnt_ref[...] = jnp.broadcast_to(counts, cnt_ref.shape)
    else:
        h2_ref[...] = h2.astype(BF16)


def _merge(attn, ssm, proj, x, mod, lp, seq, tm):
    rows = x.shape[0]
    per_mod = rows // mod.shape[0]
    tm = min(tm, per_mod)
    moe = lp["router"] is not None
    hb = tm // HALO
    prev8 = lambda i: jnp.maximum(i * hb - 1, 0)
    next8 = lambda i: jnp.minimum((i + 1) * hb, rows // HALO - 1)
    tile = lambda col: pl.BlockSpec((tm, D_MODEL), lambda i: (i, col // D_MODEL))
    const = lambda shape: pl.BlockSpec(shape, lambda i: (0, 0))
    in_specs = [tile(0), tile(0), tile(COL_SCB), tile(COL_SCC), tile(COL_SCH),
                pl.BlockSpec((HALO, D_MODEL), lambda i: (prev8(i), COL_SCC // D_MODEL)),
                pl.BlockSpec((HALO, D_MODEL), lambda i: (next8(i), COL_SCC // D_MODEL)),
                pl.BlockSpec((HALO, D_MODEL), lambda i: (prev8(i), COL_SCH // D_MODEL)),
                pl.BlockSpec((HALO, D_MODEL), lambda i: (next8(i), COL_SCH // D_MODEL)),
                tile(COL_G), tile(COL_G + D_MODEL), tile(COL_G + 2 * D_MODEL), tile(0),
                pl.BlockSpec((1, SUBLANES, D_MODEL), lambda i: (i * tm // per_mod, 0, 0)),
                const((D_MODEL, D_MODEL)), const((D_MODEL, D_MODEL)), const((D_MODEL, D_MODEL)),
                const((D_MODEL, D_MODEL)), const((3, D_MODEL)), const((1, D_MODEL))]
    args = [attn, ssm, proj, proj, proj, proj, proj, proj, proj, proj, proj, proj, x, mod,
            lp["w_attn_out"], lp["w_ssm_out"], lp["w_sconv_out"], lp["w_merge"], lp["sconv_w"], lp["norm_ffn_g"]]
    out_shape = [jax.ShapeDtypeStruct((rows, D_MODEL), F32), jax.ShapeDtypeStruct((rows, D_MODEL), BF16)]
    out_specs = [tile(0), tile(0)]
    if moe:
        in_specs.append(const((D_MODEL, LANES)))
        args.append(lp["router"])
        out_shape[1] = jax.ShapeDtypeStruct((rows, D_MODEL // 2), jnp.int32)
        out_specs[1] = pl.BlockSpec((tm, D_MODEL // 2), lambda i: (i, 0))
        out_shape += [jax.ShapeDtypeStruct((rows, LANES), F32),
                      jax.ShapeDtypeStruct((rows // tm * SUBLANES, LANES), F32)]
        out_specs += [pl.BlockSpec((tm, LANES), lambda i: (i, 0)), pl.BlockSpec((SUBLANES, LANES), lambda i: (i, 0))]
    return pl.pallas_call(
        functools.partial(_merge_kernel, seq=seq, tm=tm, moe=moe),
        out_shape=tuple(out_shape),
        grid=(rows // tm,),
        in_specs=in_specs,
        out_specs=tuple(out_specs),
        compiler_params=_params(("parallel",), 56),
        name="merge",
    )(*args)


def _swiglu_acc(h, w1_ref, w3_ref, w2_ref, fs):
    a = _silu(jnp.dot(h, w1_ref[:, fs], preferred_element_type=F32)) * jnp.dot(h, w3_ref[:, fs],
                                                                                preferred_element_type=F32)
    return jnp.dot(a.astype(BF16), w2_ref[fs, :], preferred_element_type=F32)


def _ffn_kernel(h_ref, x_ref, mod_ref, w1_ref, w3_ref, w2_ref, o_ref, acc):
    f = pl.program_id(1)

    @pl.when(f == 0)
    def _():
        acc[...] = jnp.zeros(acc.shape, F32)

    acc[...] += _swiglu_acc(h_ref[...], w1_ref, w3_ref, w2_ref, slice(None))

    @pl.when(f == pl.num_programs(1) - 1)
    def _():
        o_ref[...] = x_ref[...] + mod_ref[0, 5:6, :] * acc[...]


def _ffn(h2, x, mod, w1, w3, w2, tm, tf):
    rows = x.shape[0]
    per_mod = rows // mod.shape[0]
    tm = min(tm, per_mod)
    return pl.pallas_call(
        _ffn_kernel,
        out_shape=jax.ShapeDtypeStruct((rows, D_MODEL), F32),
        grid=(rows // tm, D_FF // tf),
        in_specs=[pl.BlockSpec((tm, D_MODEL), lambda i, f: (i, 0)),
                  pl.BlockSpec((tm, D_MODEL), lambda i, f: (i, 0)),
                  pl.BlockSpec((1, SUBLANES, D_MODEL), lambda i, f: (i * tm // per_mod, 0, 0)),
                  pl.BlockSpec((D_MODEL, tf), lambda i, f: (0, f)),
                  pl.BlockSpec((D_MODEL, tf), lambda i, f: (0, f)),
                  pl.BlockSpec((tf, D_MODEL), lambda i, f: (f, 0))],
        out_specs=pl.BlockSpec((tm, D_MODEL), lambda i, f: (i, 0)),
        scratch_shapes=[pltpu.VMEM((tm, D_MODEL), F32)],
        compiler_params=_params(("parallel", "arbitrary"), 56),
        name="ffn",
    )(h2, x, mod, w1, w3, w2)


SC_CORES = 2
SC_SUBCORES = 16
SC_CHUNK = 128


def _sc_gather(table, idx):
    nw = SC_CORES * SC_SUBCORES
    b, d = idx.shape[0], table.shape[1]
    per_w = b // nw
    assert per_w * nw == b and per_w % SC_CHUNK == 0
    mesh = plsc.VectorSubcoreMesh(core_axis_name="c", subcore_axis_name="s")

    def body(table_hbm, idx_hbm, out_hbm, idx_v, rows_v, sem):
        wid = lax.axis_index("s") * SC_CORES + lax.axis_index("c")

        @pl.loop(0, per_w // SC_CHUNK)
        def _(j):
            off = pl.multiple_of(wid * per_w + j * SC_CHUNK, SC_CHUNK)
            pltpu.sync_copy(idx_hbm.at[pl.ds(off, SC_CHUNK)], idx_v)
            pltpu.async_copy(table_hbm.at[idx_v], rows_v, sem).wait()
            pltpu.sync_copy(rows_v, out_hbm.at[pl.ds(off, SC_CHUNK)])

    return pl.kernel(
        body,
        out_type=jax.ShapeDtypeStruct((b, d), table.dtype),
        mesh=mesh,
        scratch_types=[pltpu.VMEM((SC_CHUNK,), jnp.int32), pltpu.VMEM((SC_CHUNK, d), table.dtype),
                       pltpu.SemaphoreType.DMA],
        name="sc_gather",
    )(table, idx)


def _route_tables(route, counts, tile):
    t = route.shape[0]
    n_mt = counts.shape[0] // SUBLANES
    cnt = counts.reshape(n_mt, SUBLANES, LANES)[:, 0, :N_EXPERTS].astype(jnp.int32)
    incl = jnp.cumsum(cnt, axis=0)
    padded = (incl[-1] + tile - 1) // tile * tile
    gend = jnp.cumsum(padded)
    base = jnp.repeat((gend - padded)[None, :] + incl - cnt, t // n_mt, axis=0)
    e = route[:, ROUTE_I1:ROUTE_I2 + 1].astype(jnp.int32)
    r = route[:, ROUTE_R1:ROUTE_R2 + 1].astype(jnp.int32)
    pick = e[:, :, None] == jnp.arange(N_EXPERTS)[None, None, :]
    slot = jnp.sum(jnp.where(pick, base[:, None, :], 0), axis=-1) + r
    n_slots = 2 * t + N_EXPERTS * tile
    tok_of_slot = jnp.zeros((n_slots,), jnp.int32).at[slot.reshape(-1)].set(jnp.repeat(jnp.arange(t), 2))
    n_valid = gend[-1] // tile
    n_tiles = n_slots // tile
    tile_expert = jnp.sum(jnp.arange(n_tiles)[:, None] * tile >= gend[None, :], axis=1)
    last = jnp.take(tile_expert, jnp.maximum(n_valid - 1, 0))
    tile_expert = jnp.where(jnp.arange(n_tiles) < n_valid, tile_expert, last).astype(jnp.int32)
    return slot.reshape(-1).astype(jnp.int32), tok_of_slot, tile_expert, n_valid.reshape(1).astype(jnp.int32)


def _moe_ffn_kernel(te_ref, nv_ref, h_ref, w1_ref, w3_ref, w2_ref, o_ref):
    i = pl.program_id(0)

    @pl.when(i < nv_ref[0])
    def _():
        h = _unpack_bf16_pairs(h_ref[...]).astype(BF16)
        half = D_FF // 2
        y = _swiglu_acc(h, w1_ref, w3_ref, w2_ref, slice(0, half))
        y = y + _swiglu_acc(h, w1_ref, w3_ref, w2_ref, slice(half, D_FF))
        o_ref[...] = _pack_bf16_pairs(y)

    @pl.when(i >= nv_ref[0])
    def _():
        o_ref[...] = jnp.zeros(o_ref.shape, jnp.int32)


def _moe_ffn(hs, tile_expert, n_valid, w1, w3, w2, tile):
    n_slots = hs.shape[0]
    grid_spec = pltpu.PrefetchScalarGridSpec(
        num_scalar_prefetch=2,
        grid=(n_slots // tile,),
        in_specs=[pl.BlockSpec((tile, D_MODEL // 2), lambda i, te, nv: (i, 0)),
                  pl.BlockSpec((None, D_MODEL, D_FF), lambda i, te, nv: (te[i], 0, 0)),
                  pl.BlockSpec((None, D_MODEL, D_FF), lambda i, te, nv: (te[i], 0, 0)),
                  pl.BlockSpec((None, D_FF, D_MODEL), lambda i, te, nv: (te[i], 0, 0))],
        out_specs=pl.BlockSpec((tile, D_MODEL // 2), lambda i, te, nv: (i, 0)),
    )
    return pl.pallas_call(
        _moe_ffn_kernel,
        out_shape=jax.ShapeDtypeStruct((n_slots, D_MODEL // 2), jnp.int32),
        grid_spec=grid_spec,
        compiler_params=_params(("arbitrary",), 56),
        name="moe_ffn",
    )(tile_expert, n_valid, hs, w1, w3, w2)


def _combine_kernel(x_ref, y_ref, route_ref, mod_ref, o_ref):
    p1 = route_ref[:, ROUTE_P1:ROUTE_P1 + 1]
    p2 = route_ref[:, ROUTE_P2:ROUTE_P2 + 1]
    half = D_MODEL // 2
    f = p1 * _unpack_bf16_pairs(y_ref[:, :half]) + p2 * _unpack_bf16_pairs(y_ref[:, half:])
    o_ref[...] = x_ref[...] + mod_ref[0, 5:6, :] * f


def _combine(x, yg, route, mod, tm):
    rows = x.shape[0]
    per_mod = rows // mod.shape[0]
    tm = min(tm, per_mod)
    return pl.pallas_call(
        _combine_kernel,
        out_shape=jax.ShapeDtypeStruct((rows, D_MODEL), F32),
        grid=(rows // tm,),
        in_specs=[pl.BlockSpec((tm, D_MODEL), lambda i: (i, 0)),
                  pl.BlockSpec((tm, D_MODEL), lambda i: (i, 0)),
                  pl.BlockSpec((tm, LANES), lambda i: (i, 0)),
                  pl.BlockSpec((1, SUBLANES, D_MODEL), lambda i: (i * tm // per_mod, 0, 0))],
        out_specs=pl.BlockSpec((tm, D_MODEL), lambda i: (i, 0)),
        compiler_params=_params(("parallel",), 32),
        name="moe_combine",
    )(x, yg, route, mod)


def _moe(h2p, x, mod, route, counts, w1, w3, w2, tile):
    rows = x.shape[0]
    slot, tok_of_slot, tile_expert, n_valid = _route_tables(route, counts, tile)
    hs = _sc_gather(h2p, tok_of_slot)
    ys = _moe_ffn(hs, tile_expert, n_valid, w1, w3, w2, tile)
    yg = _sc_gather(ys, slot).reshape(rows, D_MODEL)
    return _combine(x, yg, route, mod, tm=512)


def _layer_params(l, w_in, q_norm_g, k_norm_g, w_attn_out, ssm_conv_w, ssm_conv_b, ssm_dt_bias, ssm_a_log,
                  ssm_d, ssm_norm_g, w_ssm_out, sconv_w, w_sconv_out, w_merge, norm_mix_g, norm_ffn_g,
                  ffn_w1, ffn_w3, ffn_w2, moe_router, moe_w1, moe_w3, moe_w2):
    w = w_in[l]
    seg = lambda a, n: w[:, a:a + n]
    w_perm = jnp.concatenate(
        [seg(0, 1024), seg(1536, 1024), seg(4128, 1024), seg(5152, 1024), seg(6176, 1024), seg(7200, 3072),
         seg(2560, 1024), seg(3584, 512), seg(1024, 256), seg(1280, 256), seg(4096, 32),
         jnp.zeros((D_MODEL, N_PROJ - COL_DT - 2 * SSM_HEADS), w.dtype)], axis=1).astype(BF16)
    row = lambda v: v.reshape(1, -1)
    lane_row = lambda v, off: jnp.zeros((1, LANES), F32).at[0, off:off + SSM_HEADS].set(v)
    head_of_col = jnp.arange(D_MODEL) // SSM_HEADDIM
    expand = [jnp.tile((jnp.arange(LANES)[:, None] == head_of_col[None, :] + off).astype(BF16), (3, 1))
              for off in (0, SSM_HEADS)]
    lp = {
        "w_in": w_perm,
        "norm_mix_g": row(norm_mix_g[l]), "norm_ffn_g": row(norm_ffn_g[l]),
        "q_norm_g": row(q_norm_g[l]), "k_norm_g": row(k_norm_g[l]),
        "w_attn_out": w_attn_out[l].astype(BF16), "w_ssm_out": w_ssm_out[l].astype(BF16),
        "w_sconv_out": w_sconv_out[l].astype(BF16), "w_merge": w_merge[l].astype(BF16),
        "cw_x": ssm_conv_w[l][:, :D_MODEL], "cw_bc": ssm_conv_w[l][:, D_MODEL:],
        "cb_x": row(ssm_conv_b[l][:D_MODEL]), "cb_bc": row(ssm_conv_b[l][D_MODEL:]),
        "alog": [lane_row(ssm_a_log[l, d], d * SSM_HEADS) for d in range(2)],
        "dtb": [lane_row(ssm_dt_bias[l, d], d * SSM_HEADS) for d in range(2)],
        "expand": expand,
        "d_row": row(jnp.repeat(ssm_d[l], SSM_HEADDIM)),
        "ssm_norm_g": row(ssm_norm_g[l]),
        "sconv_w": sconv_w[l],
    }
    if l % 2 == 0:
        lp["router"] = None
        lp["ffn"] = tuple(t[l // 2].astype(BF16) for t in (ffn_w1, ffn_w3, ffn_w2))
    else:
        lp["router"] = jnp.pad(moe_router[l // 2], ((0, 0), (0, LANES - N_EXPERTS)))
        lp["ffn"] = tuple(t[l // 2].astype(BF16) for t in (moe_w1, moe_w3, moe_w2))
    return lp


def _rope_tables(seq):
    t = jnp.arange(seq)
    n = HEAD_DIM // 4
    inv = ROPE_THETA ** (-jnp.arange(n, dtype=F32) / n)
    ang_r = (t // GRID_W).astype(F32)[:, None] * inv[None, :]
    ang_c = (t % GRID_W).astype(F32)[:, None] * inv[None, :]
    zero = jnp.zeros_like(ang_r)
    cos = jnp.concatenate([jnp.cos(ang_r)] * 2 + [jnp.cos(ang_c)] * 2, axis=1)
    sa = jnp.concatenate([-jnp.sin(ang_r), zero, -jnp.sin(ang_c), zero], axis=1)
    sb = jnp.concatenate([zero, jnp.sin(ang_r), zero, jnp.sin(ang_c)], axis=1)
    return cos, sa, sb


def _trunk(x, mod, lp, nseq, seq, lat):
    rows = nseq * seq
    proj = _inproj(x, mod, lp["norm_mix_g"], lp["w_in"], tm=min(1024, rows), tn=1152)
    if lat is None:
        attn, k_new, v_new = _attn_ctx(proj, lp["q_norm_g"], lp["k_norm_g"], nseq, seq)
        h0_f = h0_b = None
    else:
        cache_k, cache_v, h0_f, h0_b, tables = lat
        qr, kall, vall = _prep_lat(proj, cache_k, cache_v, *tables, lp["q_norm_g"], lp["k_norm_g"], nseq, seq)
        attn = _attn_lat(qr, kall, vall, tq=256).reshape(rows, D_MODEL)
        k_new = v_new = None
    yf, h_f = _ssd_pass(proj, lp, 0, nseq, seq, h0_f, None)
    ssm, h_b = _ssd_pass(proj, lp, 1, nseq, seq, h0_b, yf)
    outs = _merge(attn, ssm, proj, x, mod, lp, seq, tm=256)
    w1, w3, w2 = lp["ffn"]
    if lp["router"] is None:
        x = _ffn(outs[1], outs[0], mod, w1, w3, w2, tm=512, tf=1408)
    else:
        x = _moe(outs[1], outs[0], mod, outs[2], outs[3], w1, w3, w2, tile=512)
    return x, k_new, v_new, h_f, h_b


def kernel(x_prompt, x_sample, cache_k, cache_v, state_ssm_fwd, state_ssm_bwd, c, c_ctx, w_mod, b_mod, norm_mix_g, norm_ffn_g, w_in, q_norm_g, k_norm_g, w_attn_out, ssm_conv_w, ssm_conv_b, ssm_dt_bias, ssm_a_log, ssm_d, ssm_norm_g, w_ssm_out, sconv_w, w_sconv_out, w_merge, ffn_w1, ffn_w3, ffn_w2, moe_router, moe_w1, moe_w3, moe_w2):
    batch, seq, _ = x_prompt.shape
    nb, dseq, _ = x_sample.shape
    depth = w_in.shape[0]
    past = cache_k.shape[2]
    kvw = N_KV_HEADS * HEAD_DIM
    npair = SSM_HEADS // 2
    assert nb + 1 <= SUBLANES

    cond = jnp.zeros((SUBLANES, D_MODEL), F32).at[0].set(c_ctx).at[1:1 + nb].set(c)
    tables = _rope_tables(dseq)
    y_ctx = x_prompt.reshape(batch * seq, D_MODEL)
    y_lat = x_sample.reshape(nb * dseq, D_MODEL)
    ks, vs, hfs, hbs = [], [], [], []
    for l in range(depth):
        lp = _layer_params(l, w_in, q_norm_g, k_norm_g, w_attn_out, ssm_conv_w, ssm_conv_b, ssm_dt_bias,
                           ssm_a_log, ssm_d, ssm_norm_g, w_ssm_out, sconv_w, w_sconv_out, w_merge,
                           norm_mix_g, norm_ffn_g, ffn_w1, ffn_w3, ffn_w2, moe_router, moe_w1, moe_w3, moe_w2)
        m = _modulation(cond, w_mod[l], b_mod[l].reshape(1, -1)).reshape(SUBLANES, 6, D_MODEL)
        m = jnp.pad(m, ((0, 0), (0, SUBLANES - 6), (0, 0)))
        y_ctx, k_l, v_l, hf_l, hb_l = _trunk(y_ctx, m[0:1], lp, batch, seq, None)
        ks.append(k_l.reshape(batch, seq, N_KV_HEADS, HEAD_DIM))
        vs.append(v_l.reshape(batch, seq, N_KV_HEADS, HEAD_DIM))
        hfs.append(hf_l.reshape(batch, SSM_HEADS, SSM_HEADDIM, SSM_STATE))
        hbs.append(hb_l.reshape(batch, SSM_HEADS, SSM_HEADDIM, SSM_STATE))
        lat = (cache_k[:, l].reshape(nb, past, kvw), cache_v[:, l].reshape(nb, past, kvw),
               state_ssm_fwd[:, l].reshape(nb, npair, SSM_CHUNK, SSM_STATE),
               state_ssm_bwd[:, l].reshape(nb, npair, SSM_CHUNK, SSM_STATE), tables)
        y_lat = _trunk(y_lat, m[1:1 + nb], lp, nb, dseq, lat)[0]
    return (y_ctx.reshape(batch, seq, D_MODEL), y_lat.reshape(nb, dseq, D_MODEL),
            jnp.stack(ks, axis=1), jnp.stack(vs, axis=1), jnp.stack(hfs, axis=1), jnp.stack(hbs, axis=1))
```

```python
import functools

import jax
import jax.numpy as jnp
from jax import lax
from jax.experimental import pallas as pl
from jax.experimental.pallas import tpu as pltpu
from jax.experimental.pallas import tpu_sc as plsc

F32 = jnp.float32
BF16 = jnp.bfloat16
HIGHEST = lax.Precision.HIGHEST

D_MODEL = 1024
HEAD_DIM = 128
N_HEADS = 8
N_KV_HEADS = 2
Q_PER_KV = N_HEADS // N_KV_HEADS
ROPE_THETA = 10000.0
GRID_W = 64
SSM_HEADS = 16
SSM_HEADDIM = 64
SSM_STATE = 128
SSM_CHUNK = 128
D_FF = 2816
N_EXPERTS = 8
EPS = 1e-6
Q_SCALE = 1.4426950408889634 * HEAD_DIM ** -0.5

COL_Q, COL_Z, COL_SCB, COL_SCC, COL_SCH, COL_G = 0, 1024, 2048, 3072, 4096, 5120
COL_XS, COL_BC, COL_K, COL_V, COL_DT = 8192, 9216, 9728, 9984, 10240
N_PROJ = 10368
LANES = 128
SUBLANES = 8
HALO = 16
MIB = 1024 * 1024


def _params(sem, vmem_mib):
    return pltpu.CompilerParams(dimension_semantics=sem, vmem_limit_bytes=vmem_mib * MIB)


def _sigmoid(t):
    return 1.0 / (1.0 + jnp.exp(-t))


def _silu(t):
    return t * _sigmoid(t)


def _rms(t):
    return t * lax.rsqrt(jnp.mean(t * t, axis=-1, keepdims=True) + EPS)


def _bdot(a, b):
    return jnp.dot(a.astype(BF16), b.astype(BF16), preferred_element_type=F32)


def _hdot(a, b):
    return jnp.dot(a, b, precision=HIGHEST, preferred_element_type=F32)


def _split3(t):
    a = t.astype(BF16)
    r = t - a.astype(F32)
    b = r.astype(BF16)
    return a, b, (r - b.astype(F32)).astype(BF16)


def _mod_kernel(c_ref, w_ref, b_ref, o_ref):
    o_ref[...] = _hdot(_silu(c_ref[...]), w_ref[...]) + b_ref[...]


def _modulation(cond8, w, b):
    n = w.shape[1] // D_MODEL
    return pl.pallas_call(
        _mod_kernel,
        out_shape=jax.ShapeDtypeStruct((SUBLANES, w.shape[1]), F32),
        grid=(n,),
        in_specs=[pl.BlockSpec((SUBLANES, D_MODEL), lambda j: (0, 0)),
                  pl.BlockSpec((D_MODEL, D_MODEL), lambda j: (0, j)),
                  pl.BlockSpec((1, D_MODEL), lambda j: (0, j))],
        out_specs=pl.BlockSpec((SUBLANES, D_MODEL), lambda j: (0, j)),
        compiler_params=_params(("parallel",), 32),
        name="modulation",
    )(cond8, w, b)


def _inproj_kernel(x_ref, mod_ref, g_ref, w_ref, o_ref, h_scr):
    @pl.when(pl.program_id(1) == 0)
    def _():
        h = _rms(x_ref[...]) * g_ref[...]
        h = h * (1.0 + mod_ref[0, 1:2, :]) + mod_ref[0, 0:1, :]
        h_scr[...] = h.astype(BF16)

    o_ref[...] = jnp.dot(h_scr[...], w_ref[...], preferred_element_type=F32).astype(o_ref.dtype)


def _inproj(x, mod, g, w, tm, tn):
    rows = x.shape[0]
    per_mod = rows // mod.shape[0]
    tm = min(tm, per_mod)
    return pl.pallas_call(
        _inproj_kernel,
        out_shape=jax.ShapeDtypeStruct((rows, N_PROJ), BF16),
        grid=(rows // tm, N_PROJ // tn),
        in_specs=[pl.BlockSpec((tm, D_MODEL), lambda i, j: (i, 0)),
                  pl.BlockSpec((1, SUBLANES, D_MODEL), lambda i, j: (i * tm // per_mod, 0, 0)),
                  pl.BlockSpec((1, D_MODEL), lambda i, j: (0, 0)),
                  pl.BlockSpec((D_MODEL, tn), lambda i, j: (0, j))],
        out_specs=pl.BlockSpec((tm, tn), lambda i, j: (i, j)),
        scratch_shapes=[pltpu.VMEM((tm, D_MODEL), BF16)],
        compiler_params=_params(("parallel", "arbitrary"), 48),
        name="inproj",
    )(x, mod, g, w)


def _softmax_pv(s, vb):
    m = jnp.max(s, axis=-1, keepdims=True)
    p = jnp.exp2(s - m)
    l = jnp.sum(p, axis=-1, keepdims=True)
    return jnp.dot(p.astype(BF16), vb, preferred_element_type=F32) / l


def _qk(qb, kb):
    return lax.dot_general(qb, kb, (((1,), (1,)), ((), ())), preferred_element_type=F32)


def _attn_ctx_kernel(q_ref, k_ref, v_ref, qg_ref, kg_ref, *rest):
    a_ref, ko_ref, vo_ref = rest[-3:]
    v = v_ref[...].astype(F32)
    vo_ref[...] = v
    for hk in range(N_KV_HEADS):
        ks = slice(hk * HEAD_DIM, (hk + 1) * HEAD_DIM)
        kn = _rms(k_ref[:, ks].astype(F32)) * kg_ref[...]
        ko_ref[:, ks] = kn
        knb = kn.astype(BF16)
        vb = v[:, ks].astype(BF16)
        for g in range(Q_PER_KV):
            hs = slice((hk * Q_PER_KV + g) * HEAD_DIM, (hk * Q_PER_KV + g + 1) * HEAD_DIM)
            qn = _rms(q_ref[:, hs].astype(F32)) * (qg_ref[...] * Q_SCALE)
            s = _qk(qn.astype(BF16), knb)
            a_ref[:, hs] = _softmax_pv(s, vb).astype(BF16)


def _attn_ctx(proj, qg, kg, nseq, seq, layer, depth, earlier):
    kvw = N_KV_HEADS * HEAD_DIM
    cache = jax.ShapeDtypeStruct((nseq, depth, seq, kvw), F32)
    cache_spec = pl.BlockSpec((None, None, seq, kvw), lambda b: (b, layer, 0, 0))
    n_in = 5
    return pl.pallas_call(
        _attn_ctx_kernel,
        out_shape=(jax.ShapeDtypeStruct((nseq * seq, D_MODEL), BF16), cache, cache),
        grid=(nseq,),
        in_specs=[pl.BlockSpec((seq, D_MODEL), lambda b: (b, COL_Q // D_MODEL)),
                  pl.BlockSpec((seq, kvw), lambda b: (b, COL_K // kvw)),
                  pl.BlockSpec((seq, kvw), lambda b: (b, COL_V // kvw)),
                  pl.BlockSpec((1, HEAD_DIM), lambda b: (0, 0)),
                  pl.BlockSpec((1, HEAD_DIM), lambda b: (0, 0))] + [pl.BlockSpec(memory_space=pl.ANY)] * len(earlier),
        out_specs=(pl.BlockSpec((seq, D_MODEL), lambda b: (b, 0)), cache_spec, cache_spec),
        input_output_aliases={n_in + i: 1 + i for i in range(len(earlier))},
        compiler_params=_params(("parallel",), 32),
        name="attn_ctx",
    )(proj, proj, proj, qg, kg, *earlier)


def _rope(t, cos, sa, sb):
    return t * cos + pltpu.roll(t, 96, 1) * sa + pltpu.roll(t, 32, 1) * sb


def _prep_lat_kernel(q_ref, k_ref, v_ref, ck_ref, cv_ref, cos_ref, sa_ref, sb_ref, qg_ref, kg_ref,
                     qo_ref, ko_ref, vo_ref):
    t = pl.program_id(1)

    @pl.when(t == 0)
    def _():
        ko_ref[...] = ck_ref[...].astype(BF16)
        vo_ref[...] = cv_ref[...].astype(BF16)

    @pl.when(t > 0)
    def _():
        cos, sa, sb = cos_ref[...], sa_ref[...], sb_ref[...]
        vo_ref[...] = v_ref[...]
        for h in range(N_KV_HEADS):
            hs = slice(h * HEAD_DIM, (h + 1) * HEAD_DIM)
            ko_ref[:, hs] = _rope(_rms(k_ref[:, hs].astype(F32)) * kg_ref[...], cos, sa, sb).astype(BF16)
        for h in range(N_HEADS):
            hs = slice(h * HEAD_DIM, (h + 1) * HEAD_DIM)
            qn = _rms(q_ref[:, hs].astype(F32)) * (qg_ref[...] * Q_SCALE)
            qo_ref[:, hs] = _rope(qn, cos, sa, sb).astype(BF16)


def _prep_lat(proj, cache_k, cache_v, cos, sa, sb, qg, kg, nb, seq):
    past = cache_k.shape[1]
    tr = past
    kvw = N_KV_HEADS * HEAD_DIM
    nt = seq // tr
    prev = lambda t: jnp.maximum(t - 1, 0)
    return pl.pallas_call(
        _prep_lat_kernel,
        out_shape=(jax.ShapeDtypeStruct((nb, seq, D_MODEL), BF16),
                   jax.ShapeDtypeStruct((nb, past + seq, kvw), BF16),
                   jax.ShapeDtypeStruct((nb, past + seq, kvw), BF16)),
        grid=(nb, nt + 1),
        in_specs=[pl.BlockSpec((tr, D_MODEL), lambda b, t: (b * nt + prev(t), COL_Q // D_MODEL)),
                  pl.BlockSpec((tr, kvw), lambda b, t: (b * nt + prev(t), COL_K // kvw)),
                  pl.BlockSpec((tr, kvw), lambda b, t: (b * nt + prev(t), COL_V // kvw)),
                  pl.BlockSpec((None, past, kvw), lambda b, t: (b, 0, 0)),
                  pl.BlockSpec((None, past, kvw), lambda b, t: (b, 0, 0)),
                  pl.BlockSpec((tr, HEAD_DIM), lambda b, t: (prev(t), 0)),
                  pl.BlockSpec((tr, HEAD_DIM), lambda b, t: (prev(t), 0)),
                  pl.BlockSpec((tr, HEAD_DIM), lambda b, t: (prev(t), 0)),
                  pl.BlockSpec((1, HEAD_DIM), lambda b, t: (0, 0)),
                  pl.BlockSpec((1, HEAD_DIM), lambda b, t: (0, 0))],
        out_specs=(pl.BlockSpec((None, tr, D_MODEL), lambda b, t: (b, prev(t), 0)),
                   pl.BlockSpec((None, tr, kvw), lambda b, t: (b, t, 0)),
                   pl.BlockSpec((None, tr, kvw), lambda b, t: (b, t, 0))),
        compiler_params=_params(("parallel", "arbitrary"), 32),
        name="prep_lat",
    )(proj, proj, proj, cache_k, cache_v, cos, sa, sb, qg, kg)


def _attn_lat_kernel(q_ref, k_ref, v_ref, o_ref):
    kb = k_ref[...]
    vb = v_ref[...]
    for g in range(Q_PER_KV):
        hs = slice(g * HEAD_DIM, (g + 1) * HEAD_DIM)
        s = _qk(q_ref[:, hs], kb)
        o_ref[:, hs] = _softmax_pv(s, vb).astype(BF16)


def _attn_lat(qr, kall, vall, tq):
    nb, seq, _ = qr.shape
    nk = kall.shape[1]
    gw = Q_PER_KV * HEAD_DIM
    return pl.pallas_call(
        _attn_lat_kernel,
        out_shape=jax.ShapeDtypeStruct((nb, seq, D_MODEL), BF16),
        grid=(nb, N_KV_HEADS, seq // tq),
        in_specs=[pl.BlockSpec((None, tq, gw), lambda b, h, i: (b, i, h)),
                  pl.BlockSpec((None, nk, HEAD_DIM), lambda b, h, i: (b, 0, h)),
                  pl.BlockSpec((None, nk, HEAD_DIM), lambda b, h, i: (b, 0, h))],
        out_specs=pl.BlockSpec((None, tq, gw), lambda b, h, i: (b, i, h)),
        compiler_params=_params(("parallel", "parallel", "arbitrary"), 48),
        name="attn_lat",
    )(qr, kall, vall)


def _conv3_rows(x, prev_row, next_row, w, tm):
    r = lax.broadcasted_iota(jnp.int32, (tm, 1), 0)
    xm1 = jnp.where(r == 0, prev_row, pltpu.roll(x, 1, 0))
    xp1 = jnp.where(r == tm - 1, next_row, pltpu.roll(x, tm - 1, 0))
    return w[0:1, :] * xm1 + w[1:2, :] * x + w[2:3, :] * xp1


def _ssd_kernel(*refs, reverse, has_h0, nc):
    it = iter(refs)
    xs_ref, bc_ref, dt_ref, xsp_ref, xsn_ref, bcp_ref, bcn_ref = (next(it) for _ in range(7))
    cwx_ref, cbx_ref, cwb_ref, cbb_ref, alog_ref, dtb_ref, e_ref = (next(it) for _ in range(7))
    h0_ref = next(it) if has_h0 else None
    if reverse:
        z_ref, yf_ref, d_ref, g_ref = (next(it) for _ in range(4))
    y_ref, hout_ref, ht_scr = refs[-3:]

    q = SSM_CHUNK
    npair = SSM_HEADS // 2
    c = pl.program_id(1)
    cc = nc - 1 - c if reverse else c
    first = cc == 0
    last = cc == nc - 1

    @pl.when(c == 0)
    def _():
        for p in range(npair):
            if has_h0:
                ht_scr[p] = h0_ref[p].T
            else:
                ht_scr[p] = jnp.zeros((q, q), F32)

    def conv_silu(x_ref_, p_ref_, n_ref_, w_ref_, b_ref_):
        pr = jnp.where(first, 0.0, p_ref_[HALO - 1:HALO, :].astype(F32))
        nx = jnp.where(last, 0.0, n_ref_[0:1, :].astype(F32))
        return _silu(_conv3_rows(x_ref_[...].astype(F32), pr, nx, w_ref_[...], q) + b_ref_[...])

    xs = conv_silu(xs_ref, xsp_ref, xsn_ref, cwx_ref, cbx_ref)
    bc = conv_silu(bc_ref, bcp_ref, bcn_ref, cwb_ref, cbb_ref)

    off = SSM_HEADS if reverse else 0
    dt = jax.nn.softplus(dt_ref[...].astype(F32) + dtb_ref[...])
    da = dt * (-jnp.exp(alog_ref[...]))
    ii = lax.broadcasted_iota(jnp.int32, (q, q), 0)
    jj = lax.broadcasted_iota(jnp.int32, (q, q), 1)
    tri = (jj >= ii) if reverse else (ii >= jj)
    tri_b = jnp.where(tri, 1.0, 0.0).astype(BF16)
    tri_tb = jnp.where((ii >= jj) if reverse else (jj >= ii), 1.0, 0.0).astype(BF16)
    acs3 = jnp.dot(tri_b, jnp.concatenate(_split3(da), axis=1), preferred_element_type=F32)
    acs = acs3[:, :LANES] + acs3[:, LANES:2 * LANES] + acs3[:, 2 * LANES:]
    acs_t = jnp.dot(jnp.concatenate(_split3(da.T), axis=1), jnp.concatenate([tri_tb] * 3, axis=0),
                    preferred_element_type=F32)
    spread = jnp.dot(jnp.concatenate(_split3(jnp.concatenate([dt, acs], axis=0)), axis=1), e_ref[...],
                     preferred_element_type=F32)
    acs_x = spread[q:, :]
    tot_x = acs_x[0:1, :] if reverse else acs_x[q - 1:q, :]
    xdt = xs * spread[:q, :]
    xw = xdt * jnp.exp(tot_x - acs_x)
    eacs_x = jnp.exp(acs_x)
    etot_x = jnp.exp(tot_x)
    lo = lax.broadcasted_iota(jnp.int32, (q, LANES), 1) < SSM_HEADDIM

    parts = []
    for g in range(2):
        bg = bc[:, g * q:(g + 1) * q]
        cg = bc[:, (2 + g) * q:(3 + g) * q].astype(BF16)
        cb = _qk(cg, bg.astype(BF16))
        bgt = bg.T.astype(BF16)
        for pp in range(npair // 2):
            p = g * (npair // 2) + pp
            ps = slice(p * LANES, (p + 1) * LANES)
            ms = []
            for hh in range(2):
                ln = off + 2 * p + hh
                dec = jnp.where(tri, jnp.exp(acs[:, ln:ln + 1] - acs_t[ln:ln + 1, :]), 0.0)
                ms.append((cb * dec).astype(BF16))
            m = jnp.concatenate(ms, axis=1)
            xp = xdt[:, ps]
            bd = jnp.concatenate([jnp.where(lo, xp, 0.0), jnp.where(lo, 0.0, xp)], axis=0).astype(BF16)
            ht = ht_scr[p]
            yp = jnp.dot(m, bd, preferred_element_type=F32)
            yp = yp + jnp.dot(cg, ht.astype(BF16), preferred_element_type=F32) * eacs_x[:, ps]
            ht_scr[p] = etot_x[:, ps] * ht + jnp.dot(bgt, xw[:, ps].astype(BF16), preferred_element_type=F32)
            parts.append(yp)
    y = jnp.concatenate(parts, axis=1)

    if reverse:
        y = y + yf_ref[...] + d_ref[...] * xs
        y = y * _silu(z_ref[...].astype(F32))
        y_ref[...] = (_rms(y) * g_ref[...]).astype(y_ref.dtype)
    else:
        y_ref[...] = y

    @pl.when(c == nc - 1)
    def _():
        for p in range(npair):
            hout_ref[p] = ht_scr[p].T


def _ssd_pass(proj, lp, d, nseq, seq, h0, yf, layer=0, depth=1, earlier=None):
    reverse = yf is not None
    has_h0 = h0 is not None
    q = SSM_CHUNK
    nc = seq // q
    rows = nseq * seq
    npair = SSM_HEADS // 2
    hb = q // HALO
    bcw = 4 * SSM_STATE
    chunk = (lambda c: nc - 1 - c) if reverse else (lambda c: c)
    blk = lambda s, c: s * nc + chunk(c)
    prev8 = lambda s, c: jnp.maximum(blk(s, c) * hb - 1, 0)
    next8 = lambda s, c: jnp.minimum((blk(s, c) + 1) * hb, rows // HALO - 1)
    const = lambda s, c: (0, 0)
    in_specs = [
        pl.BlockSpec((q, D_MODEL), lambda s, c: (blk(s, c), COL_XS // D_MODEL)),
        pl.BlockSpec((q, bcw), lambda s, c: (blk(s, c), COL_BC // bcw)),
        pl.BlockSpec((q, LANES), lambda s, c: (blk(s, c), COL_DT // LANES)),
        pl.BlockSpec((HALO, D_MODEL), lambda s, c: (prev8(s, c), COL_XS // D_MODEL)),
        pl.BlockSpec((HALO, D_MODEL), lambda s, c: (next8(s, c), COL_XS // D_MODEL)),
        pl.BlockSpec((HALO, bcw), lambda s, c: (prev8(s, c), COL_BC // bcw)),
        pl.BlockSpec((HALO, bcw), lambda s, c: (next8(s, c), COL_BC // bcw)),
        pl.BlockSpec((3, D_MODEL), const), pl.BlockSpec((1, D_MODEL), const),
        pl.BlockSpec((3, bcw), const), pl.BlockSpec((1, bcw), const),
        pl.BlockSpec((1, LANES), const), pl.BlockSpec((1, LANES), const),
        pl.BlockSpec((3 * LANES, D_MODEL), const),
    ]
    args = [proj] * 7 + [lp["cw_x"], lp["cb_x"], lp["cw_bc"], lp["cb_bc"], lp["alog"][d], lp["dtb"][d],
                         lp["expand"][d]]
    if has_h0:
        in_specs.append(pl.BlockSpec((None, npair, q, q), lambda s, c: (s, 0, 0, 0)))
        args.append(h0)
    if reverse:
        in_specs += [pl.BlockSpec((q, D_MODEL), lambda s, c: (blk(s, c), COL_Z // D_MODEL)),
                     pl.BlockSpec((q, D_MODEL), lambda s, c: (blk(s, c), 0)),
                     pl.BlockSpec((1, D_MODEL), const), pl.BlockSpec((1, D_MODEL), const)]
        args += [proj, yf, lp["d_row"], lp["ssm_norm_g"]]
    aliases = {}
    if earlier is not None:
        aliases = {len(args): 1}
        in_specs.append(pl.BlockSpec(memory_space=pl.ANY))
        args.append(earlier)
    return pl.pallas_call(
        functools.partial(_ssd_kernel, reverse=reverse, has_h0=has_h0, nc=nc),
        out_shape=(jax.ShapeDtypeStruct((rows, D_MODEL), BF16 if reverse else F32),
                   jax.ShapeDtypeStruct((nseq, depth, npair, q, q), F32)),
        grid=(nseq, nc),
        in_specs=in_specs,
        out_specs=(pl.BlockSpec((q, D_MODEL), lambda s, c: (blk(s, c), 0)),
                   pl.BlockSpec((None, None, npair, q, q), lambda s, c: (s, layer, 0, 0, 0))),
        input_output_aliases=aliases,
        scratch_shapes=[pltpu.VMEM((npair, q, q), F32)],
        compiler_params=_params(("parallel", "arbitrary"), 32),
        name="ssd_bwd" if reverse else "ssd_fwd",
    )(*args)


ROUTE_I1, ROUTE_I2, ROUTE_P1, ROUTE_P2, ROUTE_R1, ROUTE_R2 = 0, 1, 2, 3, 4, 5


def _pack_bf16_pairs(v):
    n = v.shape[1] // 2
    bits = lax.bitcast_convert_type(v.astype(BF16).astype(F32), jnp.uint32)
    return lax.bitcast_convert_type((bits[:, :n] >> 16) | bits[:, n:], jnp.int32)


def _unpack_bf16_pairs(p):
    u = lax.bitcast_convert_type(p, jnp.uint32)
    lo = lax.bitcast_convert_type(u << 16, F32)
    hi = lax.bitcast_convert_type(u & jnp.uint32(0xFFFF0000), F32)
    return jnp.concatenate([lo, hi], axis=1)


def _top2_route(logits):
    tm = logits.shape[0]
    lane = lax.broadcasted_iota(jnp.int32, logits.shape, 1).astype(F32)
    lg = jnp.where(lane < N_EXPERTS, logits, -jnp.inf)
    m1 = jnp.max(lg, axis=-1, keepdims=True)
    i1 = jnp.min(jnp.where(lg == m1, lane, float(LANES)), axis=-1, keepdims=True)
    lg2 = jnp.where(lane == i1, -jnp.inf, lg)
    m2 = jnp.max(lg2, axis=-1, keepdims=True)
    i2 = jnp.min(jnp.where(lg2 == m2, lane, float(LANES)), axis=-1, keepdims=True)
    e = jnp.exp(m2 - m1)
    p1 = 1.0 / (1.0 + e)
    chosen = jnp.where(lane == i1, 1.0, jnp.where(lane == i2, 1.0, 0.0))
    earlier = (lax.broadcasted_iota(jnp.int32, (tm, tm), 0) > lax.broadcasted_iota(jnp.int32, (tm, tm), 1))
    ranks = _bdot(jnp.where(earlier, 1.0, 0.0), chosen)
    r1 = jnp.sum(jnp.where(lane == i1, ranks, 0.0), axis=-1, keepdims=True)
    r2 = jnp.sum(jnp.where(lane == i2, ranks, 0.0), axis=-1, keepdims=True)
    rec = jnp.where(lane == ROUTE_I1, i1, jnp.where(lane == ROUTE_I2, i2, 0.0))
    rec = jnp.where(lane == ROUTE_P1, p1, jnp.where(lane == ROUTE_P2, e * p1, rec))
    rec = jnp.where(lane == ROUTE_R1, r1, jnp.where(lane == ROUTE_R2, r2, rec))
    return rec, jnp.sum(chosen, axis=0, keepdims=True)


def _merge_kernel(*refs, seq, tm, moe):
    it = iter(refs)
    attn_ref, ssm_ref, scb_ref, scc_ref, sch_ref = (next(it) for _ in range(5))
    sccp_ref, sccn_ref, schp_ref, schn_ref = (next(it) for _ in range(4))
    g0_ref, g1_ref, g2_ref, x_ref, mod_ref = (next(it) for _ in range(5))
    wa_ref, ws_ref, wc_ref, wm_ref, scw_ref, ng_ref = (next(it) for _ in range(6))
    router_ref = next(it) if moe else None
    xo_ref, h2_ref = next(it), next(it)
    route_ref, cnt_ref = (next(it), next(it)) if moe else (None, None)

    i = pl.program_id(0)
    pos = (lax.broadcasted_iota(jnp.int32, (tm, 1), 0) + i * tm) & (seq - 1)
    u = scc_ref[...].astype(F32) * sch_ref[...].astype(F32)
    up = sccp_ref[HALO - 1:HALO, :].astype(F32) * schp_ref[HALO - 1:HALO, :].astype(F32)
    un = sccn_ref[0:1, :].astype(F32) * schn_ref[0:1, :].astype(F32)
    r = lax.broadcasted_iota(jnp.int32, (tm, 1), 0)
    um1 = jnp.where(pos == 0, 0.0, jnp.where(r == 0, up, pltpu.roll(u, 1, 0)))
    up1 = jnp.where(pos == seq - 1, 0.0, jnp.where(r == tm - 1, un, pltpu.roll(u, tm - 1, 0)))
    w = scw_ref[...]
    sc = scb_ref[...].astype(F32) * (w[0:1, :] * um1 + w[1:2, :] * u + w[2:3, :] * up1)

    gate = lambda g_ref: _sigmoid(g_ref[...].astype(F32))
    merged = gate(g0_ref) * jnp.dot(attn_ref[...], wa_ref[...], preferred_element_type=F32)
    merged = merged + gate(g1_ref) * jnp.dot(ssm_ref[...], ws_ref[...], preferred_element_type=F32)
    merged = merged + gate(g2_ref) * jnp.dot(sc.astype(BF16), wc_ref[...], preferred_element_type=F32)
    mod = mod_ref[0]
    xn = x_ref[...] + mod[2:3, :] * jnp.dot(merged.astype(BF16), wm_ref[...], preferred_element_type=F32)
    xo_ref[...] = xn
    h2 = _rms(xn) * ng_ref[...] * (1.0 + mod[4:5, :]) + mod[3:4, :]
    if moe:
        h2_ref[...] = _pack_bf16_pairs(h2)
        h_hi = h2.astype(BF16)
        h_lo = (h2 - h_hi.astype(F32)).astype(BF16)
        both = jnp.dot(h_hi, router_ref[...], preferred_element_type=F32)
        logits = both[:, :LANES] + both[:, LANES:] + jnp.dot(h_lo, router_ref[:, :LANES], preferred_element_type=F32)
        rec, counts = _top2_route(logits)
        route_ref[...] = rec
        cnt_ref[...] = jnp.broadcast_to(counts, cnt_ref.shape)
    else:
        h2_ref[...] = h2.astype(BF16)


def _merge(attn, ssm, proj, x, mod, lp, seq, tm):
    rows = x.shape[0]
    per_mod = rows // mod.shape[0]
    tm = min(tm, per_mod)
    moe = lp["router"] is not None
    hb = tm // HALO
    prev8 = lambda i: jnp.maximum(i * hb - 1, 0)
    next8 = lambda i: jnp.minimum((i + 1) * hb, rows // HALO - 1)
    tile = lambda col: pl.BlockSpec((tm, D_MODEL), lambda i: (i, col // D_MODEL))
    const = lambda shape: pl.BlockSpec(shape, lambda i: (0, 0))
    in_specs = [tile(0), tile(0), tile(COL_SCB), tile(COL_SCC), tile(COL_SCH),
                pl.BlockSpec((HALO, D_MODEL), lambda i: (prev8(i), COL_SCC // D_MODEL)),
                pl.BlockSpec((HALO, D_MODEL), lambda i: (next8(i), COL_SCC // D_MODEL)),
                pl.BlockSpec((HALO, D_MODEL), lambda i: (prev8(i), COL_SCH // D_MODEL)),
                pl.BlockSpec((HALO, D_MODEL), lambda i: (next8(i), COL_SCH // D_MODEL)),
                tile(COL_G), tile(COL_G + D_MODEL), tile(COL_G + 2 * D_MODEL), tile(0),
                pl.BlockSpec((1, SUBLANES, D_MODEL), lambda i: (i * tm // per_mod, 0, 0)),
                _resident((D_MODEL, D_MODEL)), _resident((D_MODEL, D_MODEL)), _resident((D_MODEL, D_MODEL)),
                _resident((D_MODEL, D_MODEL)), const((3, D_MODEL)), const((1, D_MODEL))]
    args = [attn, ssm, proj, proj, proj, proj, proj, proj, proj, proj, proj, proj, x, mod,
            lp["w_attn_out"], lp["w_ssm_out"], lp["w_sconv_out"], lp["w_merge"], lp["sconv_w"], lp["norm_ffn_g"]]
    out_shape = [jax.ShapeDtypeStruct((rows, D_MODEL), F32), jax.ShapeDtypeStruct((rows, D_MODEL), BF16)]
    out_specs = [tile(0), tile(0)]
    if moe:
        in_specs.append(_resident((D_MODEL, 2 * LANES)))
        args.append(lp["router"])
        out_shape[1] = jax.ShapeDtypeStruct((rows, D_MODEL // 2), jnp.int32)
        out_specs[1] = pl.BlockSpec((tm, D_MODEL // 2), lambda i: (i, 0))
        out_shape += [jax.ShapeDtypeStruct((rows, LANES), F32),
                      jax.ShapeDtypeStruct((rows // tm * SUBLANES, LANES), F32)]
        out_specs += [pl.BlockSpec((tm, LANES), lambda i: (i, 0)), pl.BlockSpec((SUBLANES, LANES), lambda i: (i, 0))]
    return pl.pallas_call(
        functools.partial(_merge_kernel, seq=seq, tm=tm, moe=moe),
        out_shape=tuple(out_shape),
        grid=(rows // tm,),
        in_specs=in_specs,
        out_specs=tuple(out_specs),
        compiler_params=_params(("parallel",), 56),
        name="merge",
    )(*args)


def _swiglu_acc(h, w1_ref, w3_ref, w2_ref, fs):
    a = _silu(jnp.dot(h, w1_ref[:, fs], preferred_element_type=F32)) * jnp.dot(h, w3_ref[:, fs],
                                                                                preferred_element_type=F32)
    return jnp.dot(a.astype(BF16), w2_ref[fs, :], preferred_element_type=F32)


def _swiglu(h, w1_ref, w3_ref, w2_ref):
    half = D_FF // 2
    return (_swiglu_acc(h, w1_ref, w3_ref, w2_ref, slice(0, half))
            + _swiglu_acc(h, w1_ref, w3_ref, w2_ref, slice(half, D_FF)))


def _ffn_kernel(h_ref, x_ref, mod_ref, w1_ref, w3_ref, w2_ref, o_ref):
    o_ref[...] = x_ref[...] + mod_ref[0, 5:6, :] * _swiglu(h_ref[...], w1_ref, w3_ref, w2_ref)


def _resident(shape):
    return pl.BlockSpec(shape, lambda *_: (0,) * len(shape), pipeline_mode=pl.Buffered(1))


def _ffn(h2, x, mod, w1, w3, w2, tm):
    rows = x.shape[0]
    per_mod = rows // mod.shape[0]
    tm = min(tm, per_mod)
    return pl.pallas_call(
        _ffn_kernel,
        out_shape=jax.ShapeDtypeStruct((rows, D_MODEL), F32),
        grid=(rows // tm,),
        in_specs=[pl.BlockSpec((tm, D_MODEL), lambda i: (i, 0)),
                  pl.BlockSpec((tm, D_MODEL), lambda i: (i, 0)),
                  pl.BlockSpec((1, SUBLANES, D_MODEL), lambda i: (i * tm // per_mod, 0, 0)),
                  _resident((D_MODEL, D_FF)), _resident((D_MODEL, D_FF)), _resident((D_FF, D_MODEL))],
        out_specs=pl.BlockSpec((tm, D_MODEL), lambda i: (i, 0)),
        compiler_params=_params(("parallel",), 56),
        name="ffn",
    )(h2, x, mod, w1, w3, w2)


def _cast_kernel(*refs):
    n = len(refs) // 2
    for src, dst in zip(refs[:n], refs[n:]):
        dst[...] = src[...].astype(dst.dtype)


def _cast_bf16(ws):
    e, r, c = ws[0].shape
    rb, cb = (r, c // 2) if c >= r else (r // 2, c)
    spec = pl.BlockSpec((None, rb, cb), lambda i, j, k: (i, j, k))
    return pl.pallas_call(
        _cast_kernel,
        out_shape=tuple(jax.ShapeDtypeStruct(w.shape, BF16) for w in ws),
        grid=(e, r // rb, c // cb),
        in_specs=[spec] * len(ws),
        out_specs=tuple([spec] * len(ws)),
        compiler_params=_params(("parallel", "parallel", "parallel"), 48),
        name="cast_bf16",
    )(*ws)


SC_CORES = 2
SC_SUBCORES = 16
SC_CHUNK = 128


def _sc_gather(table, idx):
    nw = SC_CORES * SC_SUBCORES
    b, d = idx.shape[0], table.shape[1]
    per_w = b // nw
    assert per_w * nw == b and per_w % SC_CHUNK == 0
    mesh = plsc.VectorSubcoreMesh(core_axis_name="c", subcore_axis_name="s")

    def body(table_hbm, idx_hbm, out_hbm, idx_v, rows_v, sem):
        wid = lax.axis_index("s") * SC_CORES + lax.axis_index("c")

        @pl.loop(0, per_w // SC_CHUNK)
        def _(j):
            off = pl.multiple_of(wid * per_w + j * SC_CHUNK, SC_CHUNK)
            pltpu.sync_copy(idx_hbm.at[pl.ds(off, SC_CHUNK)], idx_v)
            pltpu.async_copy(table_hbm.at[idx_v], rows_v, sem).wait()
            pltpu.sync_copy(rows_v, out_hbm.at[pl.ds(off, SC_CHUNK)])

    return pl.kernel(
        body,
        out_type=jax.ShapeDtypeStruct((b, d), table.dtype),
        mesh=mesh,
        scratch_types=[pltpu.VMEM((SC_CHUNK,), jnp.int32), pltpu.VMEM((SC_CHUNK, d), table.dtype),
                       pltpu.SemaphoreType.DMA],
        name="sc_gather",
    )(table, idx)


def _route_tables(route, counts, tile):
    t = route.shape[0]
    n_mt = counts.shape[0] // SUBLANES
    cnt = counts.reshape(n_mt, SUBLANES, LANES)[:, 0, :N_EXPERTS].astype(jnp.int32)
    incl = jnp.cumsum(cnt, axis=0)
    padded = (incl[-1] + tile - 1) // tile * tile
    gend = jnp.cumsum(padded)
    base = jnp.repeat((gend - padded)[None, :] + incl - cnt, t // n_mt, axis=0)
    e = route[:, ROUTE_I1:ROUTE_I2 + 1].astype(jnp.int32)
    r = route[:, ROUTE_R1:ROUTE_R2 + 1].astype(jnp.int32)
    pick = e[:, :, None] == jnp.arange(N_EXPERTS)[None, None, :]
    slot = jnp.sum(jnp.where(pick, base[:, None, :], 0), axis=-1) + r
    n_slots = 2 * t + N_EXPERTS * tile
    tok_of_slot = jnp.zeros((n_slots,), jnp.int32).at[slot.reshape(-1)].set(jnp.repeat(jnp.arange(t), 2))
    n_valid = gend[-1] // tile
    n_tiles = n_slots // tile
    tile_expert = jnp.sum(jnp.arange(n_tiles)[:, None] * tile >= gend[None, :], axis=1)
    last = jnp.take(tile_expert, jnp.maximum(n_valid - 1, 0))
    tile_expert = jnp.where(jnp.arange(n_tiles) < n_valid, tile_expert, last).astype(jnp.int32)
    return slot.reshape(-1).astype(jnp.int32), tok_of_slot, tile_expert, n_valid.reshape(1).astype(jnp.int32)


def _moe_ffn_kernel(te_ref, nv_ref, h_ref, w1_ref, w3_ref, w2_ref, o_ref):
    i = pl.program_id(0)

    @pl.when(i < nv_ref[0])
    def _():
        h = _unpack_bf16_pairs(h_ref[...]).astype(BF16)
        o_ref[...] = _pack_bf16_pairs(_swiglu(h, w1_ref, w3_ref, w2_ref))

    @pl.when(i >= nv_ref[0])
    def _():
        o_ref[...] = jnp.zeros(o_ref.shape, jnp.int32)


def _moe_ffn(hs, tile_expert, n_valid, w1, w3, w2, tile):
    n_slots = hs.shape[0]
    grid_spec = pltpu.PrefetchScalarGridSpec(
        num_scalar_prefetch=2,
        grid=(n_slots // tile,),
        in_specs=[pl.BlockSpec((tile, D_MODEL // 2), lambda i, te, nv: (i, 0)),
                  pl.BlockSpec((None, D_MODEL, D_FF), lambda i, te, nv: (te[i], 0, 0)),
                  pl.BlockSpec((None, D_MODEL, D_FF), lambda i, te, nv: (te[i], 0, 0)),
                  pl.BlockSpec((None, D_FF, D_MODEL), lambda i, te, nv: (te[i], 0, 0))],
        out_specs=pl.BlockSpec((tile, D_MODEL // 2), lambda i, te, nv: (i, 0)),
    )
    return pl.pallas_call(
        _moe_ffn_kernel,
        out_shape=jax.ShapeDtypeStruct((n_slots, D_MODEL // 2), jnp.int32),
        grid_spec=grid_spec,
        compiler_params=_params(("arbitrary",), 56),
        name="moe_ffn",
    )(tile_expert, n_valid, hs, w1, w3, w2)


def _combine_kernel(x_ref, y_ref, route_ref, mod_ref, o_ref):
    p1 = route_ref[:, ROUTE_P1:ROUTE_P1 + 1]
    p2 = route_ref[:, ROUTE_P2:ROUTE_P2 + 1]
    half = D_MODEL // 2
    f = p1 * _unpack_bf16_pairs(y_ref[:, :half]) + p2 * _unpack_bf16_pairs(y_ref[:, half:])
    o_ref[...] = x_ref[...] + mod_ref[0, 5:6, :] * f


def _combine(x, yg, route, mod, tm):
    rows = x.shape[0]
    per_mod = rows // mod.shape[0]
    tm = min(tm, per_mod)
    return pl.pallas_call(
        _combine_kernel,
        out_shape=jax.ShapeDtypeStruct((rows, D_MODEL), F32),
        grid=(rows // tm,),
        in_specs=[pl.BlockSpec((tm, D_MODEL), lambda i: (i, 0)),
                  pl.BlockSpec((tm, D_MODEL), lambda i: (i, 0)),
                  pl.BlockSpec((tm, LANES), lambda i: (i, 0)),
                  pl.BlockSpec((1, SUBLANES, D_MODEL), lambda i: (i * tm // per_mod, 0, 0))],
        out_specs=pl.BlockSpec((tm, D_MODEL), lambda i: (i, 0)),
        compiler_params=_params(("parallel",), 32),
        name="moe_combine",
    )(x, yg, route, mod)


def _moe(h2p, x, mod, route, counts, w1, w3, w2, tile):
    rows = x.shape[0]
    slot, tok_of_slot, tile_expert, n_valid = _route_tables(route, counts, tile)
    hs = _sc_gather(h2p, tok_of_slot)
    ys = _moe_ffn(hs, tile_expert, n_valid, w1, w3, w2, tile)
    yg = _sc_gather(ys, slot).reshape(rows, D_MODEL)
    return _combine(x, yg, route, mod, tm=512)


def _layer_params(l, w_in, q_norm_g, k_norm_g, w_attn_out, ssm_conv_w, ssm_conv_b, ssm_dt_bias, ssm_a_log,
                  ssm_d, ssm_norm_g, w_ssm_out, sconv_w, w_sconv_out, w_merge, norm_mix_g, norm_ffn_g,
                  ffn_w1, ffn_w3, ffn_w2, moe_router, moe_w1, moe_w3, moe_w2):
    w = w_in[l]
    seg = lambda a, n: w[:, a:a + n]
    w_perm = jnp.concatenate(
        [seg(0, 1024), seg(1536, 1024), seg(4128, 1024), seg(5152, 1024), seg(6176, 1024), seg(7200, 3072),
         seg(2560, 1024), seg(3584, 512), seg(1024, 256), seg(1280, 256), seg(4096, 32),
         jnp.zeros((D_MODEL, N_PROJ - COL_DT - 2 * SSM_HEADS), w.dtype)], axis=1).astype(BF16)
    row = lambda v: v.reshape(1, -1)
    lane_row = lambda v, off: jnp.zeros((1, LANES), F32).at[0, off:off + SSM_HEADS].set(v)
    head_of_col = jnp.arange(D_MODEL) // SSM_HEADDIM
    expand = [jnp.tile((jnp.arange(LANES)[:, None] == head_of_col[None, :] + off).astype(BF16), (3, 1))
              for off in (0, SSM_HEADS)]
    lp = {
        "w_in": w_perm,
        "norm_mix_g": row(norm_mix_g[l]), "norm_ffn_g": row(norm_ffn_g[l]),
        "q_norm_g": row(q_norm_g[l]), "k_norm_g": row(k_norm_g[l]),
        "w_attn_out": w_attn_out[l].astype(BF16), "w_ssm_out": w_ssm_out[l].astype(BF16),
        "w_sconv_out": w_sconv_out[l].astype(BF16), "w_merge": w_merge[l].astype(BF16),
        "cw_x": ssm_conv_w[l][:, :D_MODEL], "cw_bc": ssm_conv_w[l][:, D_MODEL:],
        "cb_x": row(ssm_conv_b[l][:D_MODEL]), "cb_bc": row(ssm_conv_b[l][D_MODEL:]),
        "alog": [lane_row(ssm_a_log[l, d], d * SSM_HEADS) for d in range(2)],
        "dtb": [lane_row(ssm_dt_bias[l, d], d * SSM_HEADS) for d in range(2)],
        "expand": expand,
        "d_row": row(jnp.repeat(ssm_d[l], SSM_HEADDIM)),
        "ssm_norm_g": row(ssm_norm_g[l]),
        "sconv_w": sconv_w[l],
    }
    if l % 2 == 0:
        lp["router"] = None
        lp["ffn"] = tuple(t[l // 2].astype(BF16) for t in (ffn_w1, ffn_w3, ffn_w2))
    else:
        router = jnp.pad(moe_router[l // 2], ((0, 0), (0, LANES - N_EXPERTS)))
        r_hi = router.astype(BF16)
        lp["router"] = jnp.concatenate([r_hi, (router - r_hi.astype(F32)).astype(BF16)], axis=1)
        lp["ffn"] = _cast_bf16((moe_w1[l // 2], moe_w3[l // 2])) + _cast_bf16((moe_w2[l // 2],))
    return lp


def _rope_tables(seq):
    t = jnp.arange(seq)
    n = HEAD_DIM // 4
    inv = ROPE_THETA ** (-jnp.arange(n, dtype=F32) / n)
    ang_r = (t // GRID_W).astype(F32)[:, None] * inv[None, :]
    ang_c = (t % GRID_W).astype(F32)[:, None] * inv[None, :]
    zero = jnp.zeros_like(ang_r)
    cos = jnp.concatenate([jnp.cos(ang_r)] * 2 + [jnp.cos(ang_c)] * 2, axis=1)
    sa = jnp.concatenate([-jnp.sin(ang_r), zero, -jnp.sin(ang_c), zero], axis=1)
    sb = jnp.concatenate([zero, jnp.sin(ang_r), zero, jnp.sin(ang_c)], axis=1)
    return cos, sa, sb


def _trunk(x, mod, lp, nseq, seq, lat, layer=0, depth=1, carry=None):
    rows = nseq * seq
    proj = _inproj(x, mod, lp["norm_mix_g"], lp["w_in"], tm=min(2048, rows), tn=1152)
    if lat is None:
        attn, k_new, v_new = _attn_ctx(proj, lp["q_norm_g"], lp["k_norm_g"], nseq, seq, layer, depth,
                                       () if carry is None else carry[:2])
        yf, h_f = _ssd_pass(proj, lp, 0, nseq, seq, None, None, layer, depth, None if carry is None else carry[2])
        ssm, h_b = _ssd_pass(proj, lp, 1, nseq, seq, None, yf, layer, depth, None if carry is None else carry[3])
    else:
        cache_k, cache_v, h0_f, h0_b, tables = lat
        qr, kall, vall = _prep_lat(proj, cache_k, cache_v, *tables, lp["q_norm_g"], lp["k_norm_g"], nseq, seq)
        attn = _attn_lat(qr, kall, vall, tq=256).reshape(rows, D_MODEL)
        k_new = v_new = None
        yf, h_f = _ssd_pass(proj, lp, 0, nseq, seq, h0_f, None)
        ssm, h_b = _ssd_pass(proj, lp, 1, nseq, seq, h0_b, yf)
    outs = _merge(attn, ssm, proj, x, mod, lp, seq, tm=256)
    w1, w3, w2 = lp["ffn"]
    if lp["router"] is None:
        x = _ffn(outs[1], outs[0], mod, w1, w3, w2, tm=512)
    else:
        x = _moe(outs[1], outs[0], mod, outs[2], outs[3], w1, w3, w2, tile=512)
    return x, (k_new, v_new, h_f, h_b)


def kernel(x_prompt, x_sample, cache_k, cache_v, state_ssm_fwd, state_ssm_bwd, c, c_ctx, w_mod, b_mod, norm_mix_g, norm_ffn_g, w_in, q_norm_g, k_norm_g, w_attn_out, ssm_conv_w, ssm_conv_b, ssm_dt_bias, ssm_a_log, ssm_d, ssm_norm_g, w_ssm_out, sconv_w, w_sconv_out, w_merge, ffn_w1, ffn_w3, ffn_w2, moe_router, moe_w1, moe_w3, moe_w2):
    batch, seq, _ = x_prompt.shape
    nb, dseq, _ = x_sample.shape
    depth = w_in.shape[0]
    past = cache_k.shape[2]
    kvw = N_KV_HEADS * HEAD_DIM
    npair = SSM_HEADS // 2
    assert nb + 1 <= SUBLANES

    cond = jnp.zeros((SUBLANES, D_MODEL), F32).at[0].set(c_ctx).at[1:1 + nb].set(c)
    tables = _rope_tables(dseq)
    y_ctx = x_prompt.reshape(batch * seq, D_MODEL)
    y_lat = x_sample.reshape(nb * dseq, D_MODEL)
    carry = None
    for l in range(depth):
        lp = _layer_params(l, w_in, q_norm_g, k_norm_g, w_attn_out, ssm_conv_w, ssm_conv_b, ssm_dt_bias,
                           ssm_a_log, ssm_d, ssm_norm_g, w_ssm_out, sconv_w, w_sconv_out, w_merge,
                           norm_mix_g, norm_ffn_g, ffn_w1, ffn_w3, ffn_w2, moe_router, moe_w1, moe_w3, moe_w2)
        m = _modulation(cond, w_mod[l], b_mod[l].reshape(1, -1)).reshape(SUBLANES, 6, D_MODEL)
        m = jnp.pad(m, ((0, 0), (0, SUBLANES - 6), (0, 0)))
        y_ctx, carry = _trunk(y_ctx, m[0:1], lp, batch, seq, None, l, depth, carry)
        lat = (cache_k[:, l].reshape(nb, past, kvw), cache_v[:, l].reshape(nb, past, kvw),
               state_ssm_fwd[:, l].reshape(nb, npair, SSM_CHUNK, SSM_STATE),
               state_ssm_bwd[:, l].reshape(nb, npair, SSM_CHUNK, SSM_STATE), tables)
        y_lat = _trunk(y_lat, m[1:1 + nb], lp, nb, dseq, lat)[0]
    k_all, v_all, hf_all, hb_all = carry
    cache_shape = (batch, depth, seq, N_KV_HEADS, HEAD_DIM)
    state_shape = (batch, depth, SSM_HEADS, SSM_HEADDIM, SSM_STATE)
    return (y_ctx.reshape(batch, seq, D_MODEL), y_lat.reshape(nb, dseq, D_MODEL),
            k_all.reshape(cache_shape), v_all.reshape(cache_shape),
            hf_all.reshape(state_shape), hb_all.reshape(state_shape))
```

```python
import functools

import jax
import jax.numpy as jnp
from jax import lax
from jax.experimental import pallas as pl
from jax.experimental.pallas import tpu as pltpu
from jax.experimental.pallas import tpu_sc as plsc

F32 = jnp.float32
BF16 = jnp.bfloat16
HIGHEST = lax.Precision.HIGHEST

D_MODEL = 1024
HEAD_DIM = 128
N_HEADS = 8
N_KV_HEADS = 2
Q_PER_KV = N_HEADS // N_KV_HEADS
ROPE_THETA = 10000.0
GRID_W = 64
SSM_HEADS = 16
SSM_HEADDIM = 64
SSM_STATE = 128
SSM_CHUNK = 128
D_FF = 2816
N_EXPERTS = 8
EPS = 1e-6
Q_SCALE = 1.4426950408889634 * HEAD_DIM ** -0.5

COL_Q, COL_Z, COL_SCB, COL_SCC, COL_SCH, COL_G = 0, 1024, 2048, 3072, 4096, 5120
COL_XS, COL_BC, COL_K, COL_V, COL_DT = 8192, 9216, 9728, 9984, 10240
N_PROJ = 10368
LANES = 128
SUBLANES = 8
HALO = 16
MIB = 1024 * 1024


def _params(sem, vmem_mib):
    return pltpu.CompilerParams(dimension_semantics=sem, vmem_limit_bytes=vmem_mib * MIB)


def _sigmoid(t):
    return 1.0 / (1.0 + jnp.exp(-t))


def _silu(t):
    return t * _sigmoid(t)


def _rms(t):
    return t * lax.rsqrt(jnp.mean(t * t, axis=-1, keepdims=True) + EPS)


def _bdot(a, b):
    return jnp.dot(a.astype(BF16), b.astype(BF16), preferred_element_type=F32)


def _hdot(a, b):
    return jnp.dot(a, b, precision=HIGHEST, preferred_element_type=F32)


def _split3(t):
    a = t.astype(BF16)
    r = t - a.astype(F32)
    b = r.astype(BF16)
    return a, b, (r - b.astype(F32)).astype(BF16)


def _mod_kernel(c_ref, w_ref, b_ref, o_ref):
    o_ref[...] = _hdot(_silu(c_ref[...]), w_ref[...]) + b_ref[...]


def _modulation(cond8, w, b):
    depth, _, width = w.shape
    return pl.pallas_call(
        _mod_kernel,
        out_shape=jax.ShapeDtypeStruct((depth, SUBLANES, width), F32),
        grid=(depth, width // D_MODEL),
        in_specs=[pl.BlockSpec((SUBLANES, D_MODEL), lambda l, j: (0, 0)),
                  pl.BlockSpec((None, D_MODEL, D_MODEL), lambda l, j: (l, 0, j)),
                  pl.BlockSpec((None, 1, D_MODEL), lambda l, j: (l, 0, j))],
        out_specs=pl.BlockSpec((None, SUBLANES, D_MODEL), lambda l, j: (l, 0, j)),
        compiler_params=_params(("parallel", "parallel"), 32),
        name="modulation",
    )(cond8, w, b)


def _inproj_kernel(x_ref, mod_ref, g_ref, w_ref, o_ref, h_scr):
    @pl.when(pl.program_id(1) == 0)
    def _():
        h = _rms(x_ref[...]) * g_ref[...]
        h = h * (1.0 + mod_ref[0, 1:2, :]) + mod_ref[0, 0:1, :]
        h_scr[...] = h.astype(BF16)

    o_ref[...] = jnp.dot(h_scr[...], w_ref[...], preferred_element_type=F32).astype(o_ref.dtype)


def _inproj(x, mod, g, w, tm, tn):
    rows = x.shape[0]
    per_mod = rows // mod.shape[0]
    tm = min(tm, per_mod)
    return pl.pallas_call(
        _inproj_kernel,
        out_shape=jax.ShapeDtypeStruct((rows, N_PROJ), BF16),
        grid=(rows // tm, N_PROJ // tn),
        in_specs=[pl.BlockSpec((tm, D_MODEL), lambda i, j: (i, 0)),
                  pl.BlockSpec((1, SUBLANES, D_MODEL), lambda i, j: (i * tm // per_mod, 0, 0)),
                  pl.BlockSpec((1, D_MODEL), lambda i, j: (0, 0)),
                  pl.BlockSpec((D_MODEL, tn), lambda i, j: (0, j))],
        out_specs=pl.BlockSpec((tm, tn), lambda i, j: (i, j)),
        scratch_shapes=[pltpu.VMEM((tm, D_MODEL), BF16)],
        compiler_params=_params(("parallel", "arbitrary"), 48),
        name="inproj",
    )(x, mod, g, w)


def _softmax_pv(s, vb):
    m = jnp.max(s, axis=-1, keepdims=True)
    p = jnp.exp2(s - m)
    l = jnp.sum(p, axis=-1, keepdims=True)
    return jnp.dot(p.astype(BF16), vb, preferred_element_type=F32) / l


def _qk(qb, kb):
    return lax.dot_general(qb, kb, (((1,), (1,)), ((), ())), preferred_element_type=F32)


def _attn_ctx_kernel(q_ref, k_ref, v_ref, qg_ref, kg_ref, *rest):
    a_ref, ko_ref, vo_ref = rest[-3:]
    v = v_ref[...].astype(F32)
    for hk in range(N_KV_HEADS):
        ks = slice(hk * HEAD_DIM, (hk + 1) * HEAD_DIM)
        kn = _rms(k_ref[:, ks].astype(F32)) * kg_ref[...]
        ko_ref[:, hk, :] = kn
        vo_ref[:, hk, :] = v[:, ks]
        knb = kn.astype(BF16)
        vb = v[:, ks].astype(BF16)
        for g in range(Q_PER_KV):
            hs = slice((hk * Q_PER_KV + g) * HEAD_DIM, (hk * Q_PER_KV + g + 1) * HEAD_DIM)
            qn = _rms(q_ref[:, hs].astype(F32)) * (qg_ref[...] * Q_SCALE)
            s = _qk(qn.astype(BF16), knb)
            a_ref[:, hs] = _softmax_pv(s, vb).astype(BF16)


def _attn_ctx(proj, qg, kg, nseq, seq, layer, depth, earlier):
    kvw = N_KV_HEADS * HEAD_DIM
    cache = jax.ShapeDtypeStruct((nseq, depth, seq, N_KV_HEADS, HEAD_DIM), F32)
    cache_spec = pl.BlockSpec((None, None, seq, N_KV_HEADS, HEAD_DIM), lambda b: (b, layer, 0, 0, 0))
    n_in = 5
    return pl.pallas_call(
        _attn_ctx_kernel,
        out_shape=(jax.ShapeDtypeStruct((nseq * seq, D_MODEL), BF16), cache, cache),
        grid=(nseq,),
        in_specs=[pl.BlockSpec((seq, D_MODEL), lambda b: (b, COL_Q // D_MODEL)),
                  pl.BlockSpec((seq, kvw), lambda b: (b, COL_K // kvw)),
                  pl.BlockSpec((seq, kvw), lambda b: (b, COL_V // kvw)),
                  pl.BlockSpec((1, HEAD_DIM), lambda b: (0, 0)),
                  pl.BlockSpec((1, HEAD_DIM), lambda b: (0, 0))] + [pl.BlockSpec(memory_space=pl.ANY)] * len(earlier),
        out_specs=(pl.BlockSpec((seq, D_MODEL), lambda b: (b, 0)), cache_spec, cache_spec),
        input_output_aliases={n_in + i: 1 + i for i in range(len(earlier))},
        compiler_params=_params(("parallel",), 32),
        name="attn_ctx",
    )(proj, proj, proj, qg, kg, *earlier)


def _rope(t, cos, sa, sb):
    return t * cos + pltpu.roll(t, 96, 1) * sa + pltpu.roll(t, 32, 1) * sb


def _prep_lat_kernel(q_ref, k_ref, v_ref, ck_ref, cv_ref, cos_ref, sa_ref, sb_ref, qg_ref, kg_ref,
                     qo_ref, ko_ref, vo_ref):
    t = pl.program_id(1)

    @pl.when(t == 0)
    def _():
        ko_ref[...] = ck_ref[...].astype(BF16)
        vo_ref[...] = cv_ref[...].astype(BF16)

    @pl.when(t > 0)
    def _():
        cos, sa, sb = cos_ref[...], sa_ref[...], sb_ref[...]
        vo_ref[...] = v_ref[...]
        for h in range(N_KV_HEADS):
            hs = slice(h * HEAD_DIM, (h + 1) * HEAD_DIM)
            ko_ref[:, hs] = _rope(_rms(k_ref[:, hs].astype(F32)) * kg_ref[...], cos, sa, sb).astype(BF16)
        for h in range(N_HEADS):
            hs = slice(h * HEAD_DIM, (h + 1) * HEAD_DIM)
            qn = _rms(q_ref[:, hs].astype(F32)) * (qg_ref[...] * Q_SCALE)
            qo_ref[:, hs] = _rope(qn, cos, sa, sb).astype(BF16)


def _prep_lat(proj, cache_k, cache_v, cos, sa, sb, qg, kg, nb, seq):
    past = cache_k.shape[1]
    tr = past
    kvw = N_KV_HEADS * HEAD_DIM
    nt = seq // tr
    prev = lambda t: jnp.maximum(t - 1, 0)
    return pl.pallas_call(
        _prep_lat_kernel,
        out_shape=(jax.ShapeDtypeStruct((nb, seq, D_MODEL), BF16),
                   jax.ShapeDtypeStruct((nb, past + seq, kvw), BF16),
                   jax.ShapeDtypeStruct((nb, past + seq, kvw), BF16)),
        grid=(nb, nt + 1),
        in_specs=[pl.BlockSpec((tr, D_MODEL), lambda b, t: (b * nt + prev(t), COL_Q // D_MODEL)),
                  pl.BlockSpec((tr, kvw), lambda b, t: (b * nt + prev(t), COL_K // kvw)),
                  pl.BlockSpec((tr, kvw), lambda b, t: (b * nt + prev(t), COL_V // kvw)),
                  pl.BlockSpec((None, past, kvw), lambda b, t: (b, 0, 0)),
                  pl.BlockSpec((None, past, kvw), lambda b, t: (b, 0, 0)),
                  pl.BlockSpec((tr, HEAD_DIM), lambda b, t: (prev(t), 0)),
                  pl.BlockSpec((tr, HEAD_DIM), lambda b, t: (prev(t), 0)),
                  pl.BlockSpec((tr, HEAD_DIM), lambda b, t: (prev(t), 0)),
                  pl.BlockSpec((1, HEAD_DIM), lambda b, t: (0, 0)),
                  pl.BlockSpec((1, HEAD_DIM), lambda b, t: (0, 0))],
        out_specs=(pl.BlockSpec((None, tr, D_MODEL), lambda b, t: (b, prev(t), 0)),
                   pl.BlockSpec((None, tr, kvw), lambda b, t: (b, t, 0)),
                   pl.BlockSpec((None, tr, kvw), lambda b, t: (b, t, 0))),
        compiler_params=_params(("parallel", "arbitrary"), 32),
        name="prep_lat",
    )(proj, proj, proj, cache_k, cache_v, cos, sa, sb, qg, kg)


def _attn_lat_kernel(q_ref, k_ref, v_ref, o_ref):
    kb = k_ref[...]
    vb = v_ref[...]
    for g in range(Q_PER_KV):
        hs = slice(g * HEAD_DIM, (g + 1) * HEAD_DIM)
        s = _qk(q_ref[:, hs], kb)
        o_ref[:, hs] = _softmax_pv(s, vb).astype(BF16)


def _attn_lat(qr, kall, vall, tq):
    nb, seq, _ = qr.shape
    nk = kall.shape[1]
    gw = Q_PER_KV * HEAD_DIM
    return pl.pallas_call(
        _attn_lat_kernel,
        out_shape=jax.ShapeDtypeStruct((nb, seq, D_MODEL), BF16),
        grid=(nb, N_KV_HEADS, seq // tq),
        in_specs=[pl.BlockSpec((None, tq, gw), lambda b, h, i: (b, i, h)),
                  pl.BlockSpec((None, nk, HEAD_DIM), lambda b, h, i: (b, 0, h)),
                  pl.BlockSpec((None, nk, HEAD_DIM), lambda b, h, i: (b, 0, h))],
        out_specs=pl.BlockSpec((None, tq, gw), lambda b, h, i: (b, i, h)),
        compiler_params=_params(("parallel", "parallel", "arbitrary"), 48),
        name="attn_lat",
    )(qr, kall, vall)


def _conv3_rows(x, prev_row, next_row, w, tm):
    r = lax.broadcasted_iota(jnp.int32, (tm, 1), 0)
    xm1 = jnp.where(r == 0, prev_row, pltpu.roll(x, 1, 0))
    xp1 = jnp.where(r == tm - 1, next_row, pltpu.roll(x, tm - 1, 0))
    return w[0:1, :] * xm1 + w[1:2, :] * x + w[2:3, :] * xp1


def _ssd_kernel(*refs, reverse, has_h0, nc):
    it = iter(refs)
    xs_ref, bc_ref, dt_ref, xsp_ref, xsn_ref, bcp_ref, bcn_ref = (next(it) for _ in range(7))
    cwx_ref, cbx_ref, cwb_ref, cbb_ref, alog_ref, dtb_ref, e_ref = (next(it) for _ in range(7))
    h0_ref = next(it) if has_h0 else None
    if reverse:
        z_ref, yf_ref, d_ref, g_ref = (next(it) for _ in range(4))
    y_ref, hout_ref, ht_scr = refs[-3:]

    q = SSM_CHUNK
    npair = SSM_HEADS // 2
    c = pl.program_id(1)
    cc = nc - 1 - c if reverse else c
    first = cc == 0
    last = cc == nc - 1

    @pl.when(c == 0)
    def _():
        for p in range(npair):
            if has_h0:
                ht_scr[p] = h0_ref[p].T
            else:
                ht_scr[p] = jnp.zeros((q, q), F32)

    def conv_silu(x_ref_, p_ref_, n_ref_, w_ref_, b_ref_):
        pr = jnp.where(first, 0.0, p_ref_[HALO - 1:HALO, :].astype(F32))
        nx = jnp.where(last, 0.0, n_ref_[0:1, :].astype(F32))
        return _silu(_conv3_rows(x_ref_[...].astype(F32), pr, nx, w_ref_[...], q) + b_ref_[...])

    xs = conv_silu(xs_ref, xsp_ref, xsn_ref, cwx_ref, cbx_ref)
    bc = conv_silu(bc_ref, bcp_ref, bcn_ref, cwb_ref, cbb_ref)

    off = SSM_HEADS if reverse else 0
    dt = jax.nn.softplus(dt_ref[...].astype(F32) + dtb_ref[...])
    da = dt * (-jnp.exp(alog_ref[...]))
    ii = lax.broadcasted_iota(jnp.int32, (q, q), 0)
    jj = lax.broadcasted_iota(jnp.int32, (q, q), 1)
    tri = (jj >= ii) if reverse else (ii >= jj)
    tri_b = jnp.where(tri, 1.0, 0.0).astype(BF16)
    tri_tb = jnp.where((ii >= jj) if reverse else (jj >= ii), 1.0, 0.0).astype(BF16)
    acs3 = jnp.dot(tri_b, jnp.concatenate(_split3(da), axis=1), preferred_element_type=F32)
    acs = acs3[:, :LANES] + acs3[:, LANES:2 * LANES] + acs3[:, 2 * LANES:]
    acs_t = jnp.dot(jnp.concatenate(_split3(da.T), axis=1), jnp.concatenate([tri_tb] * 3, axis=0),
                    preferred_element_type=F32)
    spread = jnp.dot(jnp.concatenate(_split3(jnp.concatenate([dt, acs], axis=0)), axis=1), e_ref[...],
                     preferred_element_type=F32)
    acs_x = spread[q:, :]
    tot_x = acs_x[0:1, :] if reverse else acs_x[q - 1:q, :]
    xdt = xs * spread[:q, :]
    xw = xdt * jnp.exp(tot_x - acs_x)
    eacs_x = jnp.exp(acs_x)
    etot_x = jnp.exp(tot_x)
    lo = lax.broadcasted_iota(jnp.int32, (q, LANES), 1) < SSM_HEADDIM

    parts = []
    for g in range(2):
        bg = bc[:, g * q:(g + 1) * q]
        cg = bc[:, (2 + g) * q:(3 + g) * q].astype(BF16)
        cb = _qk(cg, bg.astype(BF16))
        bgt = bg.T.astype(BF16)
        for pp in range(npair // 2):
            p = g * (npair // 2) + pp
            ps = slice(p * LANES, (p + 1) * LANES)
            ms = []
            for hh in range(2):
                ln = off + 2 * p + hh
                dec = jnp.where(tri, jnp.exp(acs[:, ln:ln + 1] - acs_t[ln:ln + 1, :]), 0.0)
                ms.append((cb * dec).astype(BF16))
            m = jnp.concatenate(ms, axis=1)
            xp = xdt[:, ps]
            bd = jnp.concatenate([jnp.where(lo, xp, 0.0), jnp.where(lo, 0.0, xp)], axis=0).astype(BF16)
            ht = ht_scr[p]
            yp = jnp.dot(m, bd, preferred_element_type=F32)
            yp = yp + jnp.dot(cg, ht.astype(BF16), preferred_element_type=F32) * eacs_x[:, ps]
            ht_scr[p] = etot_x[:, ps] * ht + jnp.dot(bgt, xw[:, ps].astype(BF16), preferred_element_type=F32)
            parts.append(yp)
    y = jnp.concatenate(parts, axis=1)

    if reverse:
        y = y + yf_ref[...] + d_ref[...] * xs
        y = y * _silu(z_ref[...].astype(F32))
        y_ref[...] = (_rms(y) * g_ref[...]).astype(y_ref.dtype)
    else:
        y_ref[...] = y

    @pl.when(c == nc - 1)
    def _():
        for p in range(npair):
            hout_ref[p] = ht_scr[p].T


def _ssd_pass(proj, lp, d, nseq, seq, h0, yf, layer=0, depth=1, earlier=None):
    reverse = yf is not None
    has_h0 = h0 is not None
    q = SSM_CHUNK
    nc = seq // q
    rows = nseq * seq
    npair = SSM_HEADS // 2
    hb = q // HALO
    bcw = 4 * SSM_STATE
    chunk = (lambda c: nc - 1 - c) if reverse else (lambda c: c)
    blk = lambda s, c: s * nc + chunk(c)
    prev8 = lambda s, c: jnp.maximum(blk(s, c) * hb - 1, 0)
    next8 = lambda s, c: jnp.minimum((blk(s, c) + 1) * hb, rows // HALO - 1)
    const = lambda s, c: (0, 0)
    in_specs = [
        pl.BlockSpec((q, D_MODEL), lambda s, c: (blk(s, c), COL_XS // D_MODEL)),
        pl.BlockSpec((q, bcw), lambda s, c: (blk(s, c), COL_BC // bcw)),
        pl.BlockSpec((q, LANES), lambda s, c: (blk(s, c), COL_DT // LANES)),
        pl.BlockSpec((HALO, D_MODEL), lambda s, c: (prev8(s, c), COL_XS // D_MODEL)),
        pl.BlockSpec((HALO, D_MODEL), lambda s, c: (next8(s, c), COL_XS // D_MODEL)),
        pl.BlockSpec((HALO, bcw), lambda s, c: (prev8(s, c), COL_BC // bcw)),
        pl.BlockSpec((HALO, bcw), lambda s, c: (next8(s, c), COL_BC // bcw)),
        pl.BlockSpec((3, D_MODEL), const), pl.BlockSpec((1, D_MODEL), const),
        pl.BlockSpec((3, bcw), const), pl.BlockSpec((1, bcw), const),
        pl.BlockSpec((1, LANES), const), pl.BlockSpec((1, LANES), const),
        pl.BlockSpec((3 * LANES, D_MODEL), const),
    ]
    args = [proj] * 7 + [lp["cw_x"], lp["cb_x"], lp["cw_bc"], lp["cb_bc"], lp["alog"][d], lp["dtb"][d],
                         lp["expand"][d]]
    if has_h0:
        in_specs.append(pl.BlockSpec((None, npair, q, q), lambda s, c: (s, 0, 0, 0)))
        args.append(h0)
    if reverse:
        in_specs += [pl.BlockSpec((q, D_MODEL), lambda s, c: (blk(s, c), COL_Z // D_MODEL)),
                     pl.BlockSpec((q, D_MODEL), lambda s, c: (blk(s, c), 0)),
                     pl.BlockSpec((1, D_MODEL), const), pl.BlockSpec((1, D_MODEL), const)]
        args += [proj, yf, lp["d_row"], lp["ssm_norm_g"]]
    aliases = {}
    if earlier is not None:
        aliases = {len(args): 1}
        in_specs.append(pl.BlockSpec(memory_space=pl.ANY))
        args.append(earlier)
    return pl.pallas_call(
        functools.partial(_ssd_kernel, reverse=reverse, has_h0=has_h0, nc=nc),
        out_shape=(jax.ShapeDtypeStruct((rows, D_MODEL), BF16 if reverse else F32),
                   jax.ShapeDtypeStruct((nseq, depth, npair, q, q), F32)),
        grid=(nseq, nc),
        in_specs=in_specs,
        out_specs=(pl.BlockSpec((q, D_MODEL), lambda s, c: (blk(s, c), 0)),
                   pl.BlockSpec((None, None, npair, q, q), lambda s, c: (s, layer, 0, 0, 0))),
        input_output_aliases=aliases,
        scratch_shapes=[pltpu.VMEM((npair, q, q), F32)],
        compiler_params=_params(("parallel", "arbitrary"), 32),
        name="ssd_bwd" if reverse else "ssd_fwd",
    )(*args)


ROUTE_I1, ROUTE_I2, ROUTE_P1, ROUTE_P2, ROUTE_R1, ROUTE_R2 = 0, 1, 2, 3, 4, 5


def _pack_bf16_pairs(v):
    n = v.shape[1] // 2
    bits = lax.bitcast_convert_type(v.astype(BF16).astype(F32), jnp.uint32)
    return lax.bitcast_convert_type((bits[:, :n] >> 16) | bits[:, n:], jnp.int32)


def _unpack_bf16_pairs(p):
    u = lax.bitcast_convert_type(p, jnp.uint32)
    lo = lax.bitcast_convert_type(u << 16, F32)
    hi = lax.bitcast_convert_type(u & jnp.uint32(0xFFFF0000), F32)
    return jnp.concatenate([lo, hi], axis=1)


def _top2_route(logits):
    tm = logits.shape[0]
    lane = lax.broadcasted_iota(jnp.int32, logits.shape, 1).astype(F32)
    lg = jnp.where(lane < N_EXPERTS, logits, -jnp.inf)
    m1 = jnp.max(lg, axis=-1, keepdims=True)
    i1 = jnp.min(jnp.where(lg == m1, lane, float(LANES)), axis=-1, keepdims=True)
    lg2 = jnp.where(lane == i1, -jnp.inf, lg)
    m2 = jnp.max(lg2, axis=-1, keepdims=True)
    i2 = jnp.min(jnp.where(lg2 == m2, lane, float(LANES)), axis=-1, keepdims=True)
    e = jnp.exp(m2 - m1)
    p1 = 1.0 / (1.0 + e)
    chosen = jnp.where(lane == i1, 1.0, jnp.where(lane == i2, 1.0, 0.0))
    earlier = (lax.broadcasted_iota(jnp.int32, (tm, tm), 0) > lax.broadcasted_iota(jnp.int32, (tm, tm), 1))
    ranks = _bdot(jnp.where(earlier, 1.0, 0.0), chosen)
    r1 = jnp.sum(jnp.where(lane == i1, ranks, 0.0), axis=-1, keepdims=True)
    r2 = jnp.sum(jnp.where(lane == i2, ranks, 0.0), axis=-1, keepdims=True)
    rec = jnp.where(lane == ROUTE_I1, i1, jnp.where(lane == ROUTE_I2, i2, 0.0))
    rec = jnp.where(lane == ROUTE_P1, p1, jnp.where(lane == ROUTE_P2, e * p1, rec))
    rec = jnp.where(lane == ROUTE_R1, r1, jnp.where(lane == ROUTE_R2, r2, rec))
    return rec, jnp.sum(chosen, axis=0, keepdims=True)


def _merge_kernel(*refs, seq, tm, moe):
    it = iter(refs)
    attn_ref, ssm_ref, scb_ref, scc_ref, sch_ref = (next(it) for _ in range(5))
    sccp_ref, sccn_ref, schp_ref, schn_ref = (next(it) for _ in range(4))
    g0_ref, g1_ref, g2_ref, x_ref, mod_ref = (next(it) for _ in range(5))
    wa_ref, ws_ref, wc_ref, wm_ref, scw_ref, ng_ref = (next(it) for _ in range(6))
    router_ref = next(it) if moe else None
    xo_ref, h2_ref = next(it), next(it)
    route_ref, cnt_ref = (next(it), next(it)) if moe else (None, None)

    i = pl.program_id(0)
    pos = (lax.broadcasted_iota(jnp.int32, (tm, 1), 0) + i * tm) & (seq - 1)
    u = scc_ref[...].astype(F32) * sch_ref[...].astype(F32)
    up = sccp_ref[HALO - 1:HALO, :].astype(F32) * schp_ref[HALO - 1:HALO, :].astype(F32)
    un = sccn_ref[0:1, :].astype(F32) * schn_ref[0:1, :].astype(F32)
    r = lax.broadcasted_iota(jnp.int32, (tm, 1), 0)
    um1 = jnp.where(pos == 0, 0.0, jnp.where(r == 0, up, pltpu.roll(u, 1, 0)))
    up1 = jnp.where(pos == seq - 1, 0.0, jnp.where(r == tm - 1, un, pltpu.roll(u, tm - 1, 0)))
    w = scw_ref[...]
    sc = scb_ref[...].astype(F32) * (w[0:1, :] * um1 + w[1:2, :] * u + w[2:3, :] * up1)

    gate = lambda g_ref: _sigmoid(g_ref[...].astype(F32))
    merged = gate(g0_ref) * jnp.dot(attn_ref[...], wa_ref[...], preferred_element_type=F32)
    merged = merged + gate(g1_ref) * jnp.dot(ssm_ref[...], ws_ref[...], preferred_element_type=F32)
    merged = merged + gate(g2_ref) * jnp.dot(sc.astype(BF16), wc_ref[...], preferred_element_type=F32)
    mod = mod_ref[0]
    xn = x_ref[...] + mod[2:3, :] * jnp.dot(merged.astype(BF16), wm_ref[...], preferred_element_type=F32)
    xo_ref[...] = xn
    h2 = _rms(xn) * ng_ref[...] * (1.0 + mod[4:5, :]) + mod[3:4, :]
    if moe:
        h2_ref[...] = _pack_bf16_pairs(h2)
        h_hi = h2.astype(BF16)
        h_lo = (h2 - h_hi.astype(F32)).astype(BF16)
        both = jnp.dot(h_hi, router_ref[...], preferred_element_type=F32)
        logits = both[:, :LANES] + both[:, LANES:] + jnp.dot(h_lo, router_ref[:, :LANES], preferred_element_type=F32)
        rec, counts = _top2_route(logits)
        route_ref[...] = rec
        cnt_ref[...] = jnp.broadcast_to(counts, cnt_ref.shape)
    else:
        h2_ref[...] = h2.astype(BF16)


def _merge(attn, ssm, proj, x, mod, lp, seq, tm):
    rows = x.shape[0]
    per_mod = rows // mod.shape[0]
    tm = min(tm, per_mod)
    moe = lp["router"] is not None
    hb = tm // HALO
    prev8 = lambda i: jnp.maximum(i * hb - 1, 0)
    next8 = lambda i: jnp.minimum((i + 1) * hb, rows // HALO - 1)
    tile = lambda col: pl.BlockSpec((tm, D_MODEL), lambda i: (i, col // D_MODEL))
    const = lambda shape: pl.BlockSpec(shape, lambda i: (0, 0))
    in_specs = [tile(0), tile(0), tile(COL_SCB), tile(COL_SCC), tile(COL_SCH),
                pl.BlockSpec((HALO, D_MODEL), lambda i: (prev8(i), COL_SCC // D_MODEL)),
                pl.BlockSpec((HALO, D_MODEL), lambda i: (next8(i), COL_SCC // D_MODEL)),
                pl.BlockSpec((HALO, D_MODEL), lambda i: (prev8(i), COL_SCH // D_MODEL)),
                pl.BlockSpec((HALO, D_MODEL), lambda i: (next8(i), COL_SCH // D_MODEL)),
                tile(COL_G), tile(COL_G + D_MODEL), tile(COL_G + 2 * D_MODEL), tile(0),
                pl.BlockSpec((1, SUBLANES, D_MODEL), lambda i: (i * tm // per_mod, 0, 0)),
                _resident((D_MODEL, D_MODEL)), _resident((D_MODEL, D_MODEL)), _resident((D_MODEL, D_MODEL)),
                _resident((D_MODEL, D_MODEL)), const((3, D_MODEL)), const((1, D_MODEL))]
    args = [attn, ssm, proj, proj, proj, proj, proj, proj, proj, proj, proj, proj, x, mod,
            lp["w_attn_out"], lp["w_ssm_out"], lp["w_sconv_out"], lp["w_merge"], lp["sconv_w"], lp["norm_ffn_g"]]
    out_shape = [jax.ShapeDtypeStruct((rows, D_MODEL), F32), jax.ShapeDtypeStruct((rows, D_MODEL), BF16)]
    out_specs = [tile(0), tile(0)]
    if moe:
        in_specs.append(_resident((D_MODEL, 2 * LANES)))
        args.append(lp["router"])
        out_shape[1] = jax.ShapeDtypeStruct((rows, D_MODEL // 2), jnp.int32)
        out_specs[1] = pl.BlockSpec((tm, D_MODEL // 2), lambda i: (i, 0))
        out_shape += [jax.ShapeDtypeStruct((rows, LANES), F32),
                      jax.ShapeDtypeStruct((rows // tm * SUBLANES, LANES), F32)]
        out_specs += [pl.BlockSpec((tm, LANES), lambda i: (i, 0)), pl.BlockSpec((SUBLANES, LANES), lambda i: (i, 0))]
    return pl.pallas_call(
        functools.partial(_merge_kernel, seq=seq, tm=tm, moe=moe),
        out_shape=tuple(out_shape),
        grid=(rows // tm,),
        in_specs=in_specs,
        out_specs=tuple(out_specs),
        compiler_params=_params(("parallel",), 56),
        name="merge",
    )(*args)


def _swiglu_acc(h, w1_ref, w3_ref, w2_ref, fs):
    a = _silu(jnp.dot(h, w1_ref[:, fs], preferred_element_type=F32)) * jnp.dot(h, w3_ref[:, fs],
                                                                                preferred_element_type=F32)
    return jnp.dot(a.astype(BF16), w2_ref[fs, :], preferred_element_type=F32)


def _swiglu(h, w1_ref, w3_ref, w2_ref):
    half = D_FF // 2
    return (_swiglu_acc(h, w1_ref, w3_ref, w2_ref, slice(0, half))
            + _swiglu_acc(h, w1_ref, w3_ref, w2_ref, slice(half, D_FF)))


def _ffn_kernel(h_ref, x_ref, mod_ref, w1_ref, w3_ref, w2_ref, o_ref):
    o_ref[...] = x_ref[...] + mod_ref[0, 5:6, :] * _swiglu(h_ref[...], w1_ref, w3_ref, w2_ref)


def _resident(shape):
    return pl.BlockSpec(shape, lambda *_: (0,) * len(shape), pipeline_mode=pl.Buffered(1))


def _ffn(h2, x, mod, w1, w3, w2, tm):
    rows = x.shape[0]
    per_mod = rows // mod.shape[0]
    tm = min(tm, per_mod)
    return pl.pallas_call(
        _ffn_kernel,
        out_shape=jax.ShapeDtypeStruct((rows, D_MODEL), F32),
        grid=(rows // tm,),
        in_specs=[pl.BlockSpec((tm, D_MODEL), lambda i: (i, 0)),
                  pl.BlockSpec((tm, D_MODEL), lambda i: (i, 0)),
                  pl.BlockSpec((1, SUBLANES, D_MODEL), lambda i: (i * tm // per_mod, 0, 0)),
                  _resident((D_MODEL, D_FF)), _resident((D_MODEL, D_FF)), _resident((D_FF, D_MODEL))],
        out_specs=pl.BlockSpec((tm, D_MODEL), lambda i: (i, 0)),
        compiler_params=_params(("parallel",), 56),
        name="ffn",
    )(h2, x, mod, w1, w3, w2)


def _cast_kernel(*refs):
    n = len(refs) // 2
    for src, dst in zip(refs[:n], refs[n:]):
        dst[...] = src[...].astype(dst.dtype)


def _cast_bf16(ws):
    e, r, c = ws[0].shape
    rb = max(d for d in range(16, r + 1, 16) if r % d == 0 and d * c * 4 <= 3 * MIB)
    spec = pl.BlockSpec((None, rb, c), lambda i, j: (i, j, 0))
    return pl.pallas_call(
        _cast_kernel,
        out_shape=tuple(jax.ShapeDtypeStruct(w.shape, BF16) for w in ws),
        grid=(e, r // rb),
        in_specs=[spec] * len(ws),
        out_specs=tuple([spec] * len(ws)),
        compiler_params=_params(("parallel", "parallel"), 48),
        name="cast_bf16",
    )(*ws)


SC_CORES = 2
SC_SUBCORES = 16
SC_CHUNK = 128


def _sc_gather(table, idx):
    nw = SC_CORES * SC_SUBCORES
    b, d = idx.shape[0], table.shape[1]
    per_w = b // nw
    assert per_w * nw == b and per_w % SC_CHUNK == 0
    mesh = plsc.VectorSubcoreMesh(core_axis_name="c", subcore_axis_name="s")

    def body(table_hbm, idx_hbm, out_hbm, idx_v, rows_v, sem):
        wid = lax.axis_index("s") * SC_CORES + lax.axis_index("c")

        @pl.loop(0, per_w // SC_CHUNK)
        def _(j):
            off = pl.multiple_of(wid * per_w + j * SC_CHUNK, SC_CHUNK)
            pltpu.sync_copy(idx_hbm.at[pl.ds(off, SC_CHUNK)], idx_v)
            pltpu.async_copy(table_hbm.at[idx_v], rows_v, sem).wait()
            pltpu.sync_copy(rows_v, out_hbm.at[pl.ds(off, SC_CHUNK)])

    return pl.kernel(
        body,
        out_type=jax.ShapeDtypeStruct((b, d), table.dtype),
        mesh=mesh,
        scratch_types=[pltpu.VMEM((SC_CHUNK,), jnp.int32), pltpu.VMEM((SC_CHUNK, d), table.dtype),
                       pltpu.SemaphoreType.DMA],
        name="sc_gather",
    )(table, idx)


def _route_tables(route, counts, tile):
    t = route.shape[0]
    n_mt = counts.shape[0] // SUBLANES
    cnt = counts.reshape(n_mt, SUBLANES, LANES)[:, 0, :N_EXPERTS].astype(jnp.int32)
    incl = jnp.cumsum(cnt, axis=0)
    padded = (incl[-1] + tile - 1) // tile * tile
    gend = jnp.cumsum(padded)
    base = jnp.repeat((gend - padded)[None, :] + incl - cnt, t // n_mt, axis=0)
    e = route[:, ROUTE_I1:ROUTE_I2 + 1].astype(jnp.int32)
    r = route[:, ROUTE_R1:ROUTE_R2 + 1].astype(jnp.int32)
    pick = e[:, :, None] == jnp.arange(N_EXPERTS)[None, None, :]
    slot = jnp.sum(jnp.where(pick, base[:, None, :], 0), axis=-1) + r
    n_slots = 2 * t + N_EXPERTS * tile
    tok_of_slot = jnp.zeros((n_slots,), jnp.int32).at[slot.reshape(-1)].set(jnp.repeat(jnp.arange(t), 2))
    n_valid = gend[-1] // tile
    n_tiles = n_slots // tile
    tile_expert = jnp.sum(jnp.arange(n_tiles)[:, None] * tile >= gend[None, :], axis=1)
    last = jnp.take(tile_expert, jnp.maximum(n_valid - 1, 0))
    tile_expert = jnp.where(jnp.arange(n_tiles) < n_valid, tile_expert, last).astype(jnp.int32)
    return slot.T.reshape(-1).astype(jnp.int32), tok_of_slot, tile_expert, n_valid.reshape(1).astype(jnp.int32)


def _moe_ffn_kernel(te_ref, nv_ref, h_ref, w1_ref, w3_ref, w2_ref, o_ref):
    i = pl.program_id(0)

    @pl.when(i < nv_ref[0])
    def _():
        h = _unpack_bf16_pairs(h_ref[...]).astype(BF16)
        o_ref[...] = _pack_bf16_pairs(_swiglu(h, w1_ref, w3_ref, w2_ref))

    @pl.when(i >= nv_ref[0])
    def _():
        o_ref[...] = jnp.zeros(o_ref.shape, jnp.int32)


def _moe_ffn(hs, tile_expert, n_valid, w1, w3, w2, tile):
    n_slots = hs.shape[0]
    grid_spec = pltpu.PrefetchScalarGridSpec(
        num_scalar_prefetch=2,
        grid=(n_slots // tile,),
        in_specs=[pl.BlockSpec((tile, D_MODEL // 2), lambda i, te, nv: (i, 0)),
                  pl.BlockSpec((None, D_MODEL, D_FF), lambda i, te, nv: (te[i], 0, 0)),
                  pl.BlockSpec((None, D_MODEL, D_FF), lambda i, te, nv: (te[i], 0, 0)),
                  pl.BlockSpec((None, D_FF, D_MODEL), lambda i, te, nv: (te[i], 0, 0))],
        out_specs=pl.BlockSpec((tile, D_MODEL // 2), lambda i, te, nv: (i, 0)),
    )
    return pl.pallas_call(
        _moe_ffn_kernel,
        out_shape=jax.ShapeDtypeStruct((n_slots, D_MODEL // 2), jnp.int32),
        grid_spec=grid_spec,
        compiler_params=_params(("arbitrary",), 56),
        name="moe_ffn",
    )(tile_expert, n_valid, hs, w1, w3, w2)


def _combine_kernel(x_ref, y1_ref, y2_ref, route_ref, mod_ref, o_ref):
    p1 = route_ref[:, ROUTE_P1:ROUTE_P1 + 1]
    p2 = route_ref[:, ROUTE_P2:ROUTE_P2 + 1]
    f = p1 * _unpack_bf16_pairs(y1_ref[...]) + p2 * _unpack_bf16_pairs(y2_ref[...])
    o_ref[...] = x_ref[...] + mod_ref[0, 5:6, :] * f


def _combine(x, yg, route, mod, tm):
    rows = x.shape[0]
    per_mod = rows // mod.shape[0]
    tm = min(tm, per_mod)
    return pl.pallas_call(
        _combine_kernel,
        out_shape=jax.ShapeDtypeStruct((rows, D_MODEL), F32),
        grid=(rows // tm,),
        in_specs=[pl.BlockSpec((tm, D_MODEL), lambda i: (i, 0)),
                  pl.BlockSpec((tm, D_MODEL // 2), lambda i: (i, 0)),
                  pl.BlockSpec((tm, D_MODEL // 2), lambda i: (i + rows // tm, 0)),
                  pl.BlockSpec((tm, LANES), lambda i: (i, 0)),
                  pl.BlockSpec((1, SUBLANES, D_MODEL), lambda i: (i * tm // per_mod, 0, 0))],
        out_specs=pl.BlockSpec((tm, D_MODEL), lambda i: (i, 0)),
        compiler_params=_params(("parallel",), 32),
        name="moe_combine",
    )(x, yg, yg, route, mod)


def _moe(h2p, x, mod, route, counts, w1, w3, w2, tile):
    slot, tok_of_slot, tile_expert, n_valid = _route_tables(route, counts, tile)
    hs = _sc_gather(h2p, tok_of_slot)
    ys = _moe_ffn(hs, tile_expert, n_valid, w1, w3, w2, tile)
    return _combine(x, _sc_gather(ys, slot), route, mod, tm=512)


def _layer_params(l, w_in, q_norm_g, k_norm_g, w_attn_out, ssm_conv_w, ssm_conv_b, ssm_dt_bias, ssm_a_log,
                  ssm_d, ssm_norm_g, w_ssm_out, sconv_w, w_sconv_out, w_merge, norm_mix_g, norm_ffn_g,
                  ffn_w1, ffn_w3, ffn_w2, moe_router, moe_w1, moe_w3, moe_w2):
    w = w_in[l]
    seg = lambda a, n: w[:, a:a + n]
    w_perm = jnp.concatenate(
        [seg(0, 1024), seg(1536, 1024), seg(4128, 1024), seg(5152, 1024), seg(6176, 1024), seg(7200, 3072),
         seg(2560, 1024), seg(3584, 512), seg(1024, 256), seg(1280, 256), seg(4096, 32),
         jnp.zeros((D_MODEL, N_PROJ - COL_DT - 2 * SSM_HEADS), w.dtype)], axis=1).astype(BF16)
    row = lambda v: v.reshape(1, -1)
    lane_row = lambda v, off: jnp.zeros((1, LANES), F32).at[0, off:off + SSM_HEADS].set(v)
    head_of_col = jnp.arange(D_MODEL) // SSM_HEADDIM
    expand = [jnp.tile((jnp.arange(LANES)[:, None] == head_of_col[None, :] + off).astype(BF16), (3, 1))
              for off in (0, SSM_HEADS)]
    lp = {
        "w_in": w_perm,
        "norm_mix_g": row(norm_mix_g[l]), "norm_ffn_g": row(norm_ffn_g[l]),
        "q_norm_g": row(q_norm_g[l]), "k_norm_g": row(k_norm_g[l]),
        "w_attn_out": w_attn_out[l].astype(BF16), "w_ssm_out": w_ssm_out[l].astype(BF16),
        "w_sconv_out": w_sconv_out[l].astype(BF16), "w_merge": w_merge[l].astype(BF16),
        "cw_x": ssm_conv_w[l][:, :D_MODEL], "cw_bc": ssm_conv_w[l][:, D_MODEL:],
        "cb_x": row(ssm_conv_b[l][:D_MODEL]), "cb_bc": row(ssm_conv_b[l][D_MODEL:]),
        "alog": [lane_row(ssm_a_log[l, d], d * SSM_HEADS) for d in range(2)],
        "dtb": [lane_row(ssm_dt_bias[l, d], d * SSM_HEADS) for d in range(2)],
        "expand": expand,
        "d_row": row(jnp.repeat(ssm_d[l], SSM_HEADDIM)),
        "ssm_norm_g": row(ssm_norm_g[l]),
        "sconv_w": sconv_w[l],
    }
    if l % 2 == 0:
        lp["router"] = None
        lp["ffn"] = tuple(t[l // 2].astype(BF16) for t in (ffn_w1, ffn_w3, ffn_w2))
    else:
        router = jnp.pad(moe_router[l // 2], ((0, 0), (0, LANES - N_EXPERTS)))
        r_hi = router.astype(BF16)
        lp["router"] = jnp.concatenate([r_hi, (router - r_hi.astype(F32)).astype(BF16)], axis=1)
        lp["ffn"] = _cast_bf16((moe_w1[l // 2], moe_w3[l // 2])) + _cast_bf16((moe_w2[l // 2],))
    return lp


def _rope_tables(seq):
    t = jnp.arange(seq)
    n = HEAD_DIM // 4
    inv = ROPE_THETA ** (-jnp.arange(n, dtype=F32) / n)
    ang_r = (t // GRID_W).astype(F32)[:, None] * inv[None, :]
    ang_c = (t % GRID_W).astype(F32)[:, None] * inv[None, :]
    zero = jnp.zeros_like(ang_r)
    cos = jnp.concatenate([jnp.cos(ang_r)] * 2 + [jnp.cos(ang_c)] * 2, axis=1)
    sa = jnp.concatenate([-jnp.sin(ang_r), zero, -jnp.sin(ang_c), zero], axis=1)
    sb = jnp.concatenate([zero, jnp.sin(ang_r), zero, jnp.sin(ang_c)], axis=1)
    return cos, sa, sb


def _trunk(x, mod, lp, nseq, seq, lat, layer=0, depth=1, carry=None):
    rows = nseq * seq
    proj = _inproj(x, mod, lp["norm_mix_g"], lp["w_in"], tm=min(2048, rows), tn=1152)
    if lat is None:
        attn, k_new, v_new = _attn_ctx(proj, lp["q_norm_g"], lp["k_norm_g"], nseq, seq, layer, depth,
                                       () if carry is None else carry[:2])
        yf, h_f = _ssd_pass(proj, lp, 0, nseq, seq, None, None, layer, depth, None if carry is None else carry[2])
        ssm, h_b = _ssd_pass(proj, lp, 1, nseq, seq, None, yf, layer, depth, None if carry is None else carry[3])
    else:
        cache_k, cache_v, h0_f, h0_b, tables = lat
        qr, kall, vall = _prep_lat(proj, cache_k, cache_v, *tables, lp["q_norm_g"], lp["k_norm_g"], nseq, seq)
        attn = _attn_lat(qr, kall, vall, tq=256).reshape(rows, D_MODEL)
        k_new = v_new = None
        yf, h_f = _ssd_pass(proj, lp, 0, nseq, seq, h0_f, None)
        ssm, h_b = _ssd_pass(proj, lp, 1, nseq, seq, h0_b, yf)
    outs = _merge(attn, ssm, proj, x, mod, lp, seq, tm=256)
    w1, w3, w2 = lp["ffn"]
    if lp["router"] is None:
        x = _ffn(outs[1], outs[0], mod, w1, w3, w2, tm=512)
    else:
        x = _moe(outs[1], outs[0], mod, outs[2], outs[3], w1, w3, w2, tile=512)
    return x, (k_new, v_new, h_f, h_b)


def kernel(x_prompt, x_sample, cache_k, cache_v, state_ssm_fwd, state_ssm_bwd, c, c_ctx, w_mod, b_mod, norm_mix_g, norm_ffn_g, w_in, q_norm_g, k_norm_g, w_attn_out, ssm_conv_w, ssm_conv_b, ssm_dt_bias, ssm_a_log, ssm_d, ssm_norm_g, w_ssm_out, sconv_w, w_sconv_out, w_merge, ffn_w1, ffn_w3, ffn_w2, moe_router, moe_w1, moe_w3, moe_w2):
    batch, seq, _ = x_prompt.shape
    nb, dseq, _ = x_sample.shape
    depth = w_in.shape[0]
    past = cache_k.shape[2]
    kvw = N_KV_HEADS * HEAD_DIM
    npair = SSM_HEADS // 2
    assert nb + 1 <= SUBLANES

    cond = jnp.zeros((SUBLANES, D_MODEL), F32).at[0].set(c_ctx).at[1:1 + nb].set(c)
    tables = _rope_tables(dseq)
    y_ctx = x_prompt.reshape(batch * seq, D_MODEL)
    y_lat = x_sample.reshape(nb * dseq, D_MODEL)
    mods = _modulation(cond, w_mod, b_mod.reshape(depth, 1, -1))
    carry = None
    for l in range(depth):
        lp = _layer_params(l, w_in, q_norm_g, k_norm_g, w_attn_out, ssm_conv_w, ssm_conv_b, ssm_dt_bias,
                           ssm_a_log, ssm_d, ssm_norm_g, w_ssm_out, sconv_w, w_sconv_out, w_merge,
                           norm_mix_g, norm_ffn_g, ffn_w1, ffn_w3, ffn_w2, moe_router, moe_w1, moe_w3, moe_w2)
        m = jnp.pad(mods[l].reshape(SUBLANES, 6, D_MODEL), ((0, 0), (0, SUBLANES - 6), (0, 0)))
        y_ctx, carry = _trunk(y_ctx, m[0:1], lp, batch, seq, None, l, depth, carry)
        lat = (cache_k[:, l].reshape(nb, past, kvw), cache_v[:, l].reshape(nb, past, kvw),
               state_ssm_fwd[:, l].reshape(nb, npair, SSM_CHUNK, SSM_STATE),
               state_ssm_bwd[:, l].reshape(nb, npair, SSM_CHUNK, SSM_STATE), tables)
        y_lat = _trunk(y_lat, m[1:1 + nb], lp, nb, dseq, lat)[0]
    k_all, v_all, hf_all, hb_all = carry
    state_shape = (batch, depth, SSM_HEADS, SSM_HEADDIM, SSM_STATE)
    return (y_ctx.reshape(batch, seq, D_MODEL), y_lat.reshape(nb, dseq, D_MODEL),
            k_all, v_all,
            hf_all.reshape(state_shape), hb_all.reshape(state_shape))
```

```python
import functools

import jax
import jax.numpy as jnp
from jax import lax
from jax.experimental import pallas as pl
from jax.experimental.pallas import tpu as pltpu
from jax.experimental.pallas import tpu_sc as plsc

F32 = jnp.float32
BF16 = jnp.bfloat16
HIGHEST = lax.Precision.HIGHEST

D_MODEL = 1024
HEAD_DIM = 128
N_HEADS = 8
N_KV_HEADS = 2
Q_PER_KV = N_HEADS // N_KV_HEADS
ROPE_THETA = 10000.0
GRID_W = 64
SSM_HEADS = 16
SSM_HEADDIM = 64
SSM_STATE = 128
SSM_CHUNK = 128
D_FF = 2816
N_EXPERTS = 8
EPS = 1e-6
Q_SCALE = 1.4426950408889634 * HEAD_DIM ** -0.5

COL_Q, COL_Z, COL_SCB, COL_SCC, COL_SCH, COL_G = 0, 1024, 2048, 3072, 4096, 5120
COL_XS, COL_BC, COL_K, COL_V, COL_DT = 8192, 9216, 9728, 9984, 10240
N_PROJ = 10368
LANES = 128
SUBLANES = 8
HALO = 16
MIB = 1024 * 1024


def _params(sem, vmem_mib):
    return pltpu.CompilerParams(dimension_semantics=sem, vmem_limit_bytes=vmem_mib * MIB)


def _sigmoid(t):
    return 1.0 / (1.0 + jnp.exp(-t))


def _silu(t):
    return t * _sigmoid(t)


def _rms(t):
    return t * lax.rsqrt(jnp.mean(t * t, axis=-1, keepdims=True) + EPS)


def _bdot(a, b):
    return jnp.dot(a.astype(BF16), b.astype(BF16), preferred_element_type=F32)


def _hdot(a, b):
    return jnp.dot(a, b, precision=HIGHEST, preferred_element_type=F32)


def _split3(t):
    a = t.astype(BF16)
    r = t - a.astype(F32)
    b = r.astype(BF16)
    return a, b, (r - b.astype(F32)).astype(BF16)


def _mod_kernel(c_ref, w_ref, b_ref, o_ref):
    o_ref[...] = _hdot(_silu(c_ref[...]), w_ref[...]) + b_ref[...]


def _modulation(cond8, w, b):
    depth, _, width = w.shape
    return pl.pallas_call(
        _mod_kernel,
        out_shape=jax.ShapeDtypeStruct((depth, SUBLANES, width), F32),
        grid=(depth, width // D_MODEL),
        in_specs=[pl.BlockSpec((SUBLANES, D_MODEL), lambda l, j: (0, 0)),
                  pl.BlockSpec((None, D_MODEL, D_MODEL), lambda l, j: (l, 0, j)),
                  pl.BlockSpec((None, 1, D_MODEL), lambda l, j: (l, 0, j))],
        out_specs=pl.BlockSpec((None, SUBLANES, D_MODEL), lambda l, j: (l, 0, j)),
        compiler_params=_params(("parallel", "parallel"), 32),
        name="modulation",
    )(cond8, w, b)


def _inproj_kernel(x_ref, mod_ref, g_ref, w_ref, o_ref, h_scr):
    @pl.when(pl.program_id(1) == 0)
    def _():
        h = _rms(x_ref[...]) * g_ref[...]
        h = h * (1.0 + mod_ref[0, 1:2, :]) + mod_ref[0, 0:1, :]
        h_scr[...] = h.astype(BF16)

    o_ref[...] = jnp.dot(h_scr[...], w_ref[...], preferred_element_type=F32).astype(o_ref.dtype)


def _inproj(x, mod, g, w, tm, tn):
    rows = x.shape[0]
    per_mod = rows // mod.shape[0]
    tm = min(tm, per_mod)
    return pl.pallas_call(
        _inproj_kernel,
        out_shape=jax.ShapeDtypeStruct((rows, N_PROJ), BF16),
        grid=(rows // tm, N_PROJ // tn),
        in_specs=[pl.BlockSpec((tm, D_MODEL), lambda i, j: (i, 0)),
                  pl.BlockSpec((1, SUBLANES, D_MODEL), lambda i, j: (i * tm // per_mod, 0, 0)),
                  pl.BlockSpec((1, D_MODEL), lambda i, j: (0, 0)),
                  pl.BlockSpec((D_MODEL, tn), lambda i, j: (0, j))],
        out_specs=pl.BlockSpec((tm, tn), lambda i, j: (i, j)),
        scratch_shapes=[pltpu.VMEM((tm, D_MODEL), BF16)],
        compiler_params=_params(("parallel", "arbitrary"), 48),
        name="inproj",
    )(x, mod, g, w)


def _softmax_pv(s, vb):
    m = jnp.max(s, axis=-1, keepdims=True)
    p = jnp.exp2(s - m)
    l = jnp.sum(p, axis=-1, keepdims=True)
    return jnp.dot(p.astype(BF16), vb, preferred_element_type=F32) / l


def _qk(qb, kb):
    return lax.dot_general(qb, kb, (((1,), (1,)), ((), ())), preferred_element_type=F32)


def _attn_ctx_kernel(q_ref, k_ref, v_ref, qg_ref, kg_ref, *rest):
    a_ref, ko_ref, vo_ref = rest[-3:]
    v = v_ref[...].astype(F32)
    for hk in range(N_KV_HEADS):
        ks = slice(hk * HEAD_DIM, (hk + 1) * HEAD_DIM)
        kn = _rms(k_ref[:, ks].astype(F32)) * kg_ref[...]
        ko_ref[:, hk, :] = kn
        vo_ref[:, hk, :] = v[:, ks]
        knb = kn.astype(BF16)
        vb = v[:, ks].astype(BF16)
        for g in range(Q_PER_KV):
            hs = slice((hk * Q_PER_KV + g) * HEAD_DIM, (hk * Q_PER_KV + g + 1) * HEAD_DIM)
            qn = _rms(q_ref[:, hs].astype(F32)) * (qg_ref[...] * Q_SCALE)
            s = _qk(qn.astype(BF16), knb)
            a_ref[:, hs] = _softmax_pv(s, vb).astype(BF16)


def _attn_ctx(proj, qg, kg, nseq, seq, layer, depth, earlier):
    kvw = N_KV_HEADS * HEAD_DIM
    cache = jax.ShapeDtypeStruct((nseq, depth, seq, N_KV_HEADS, HEAD_DIM), F32)
    cache_spec = pl.BlockSpec((None, None, seq, N_KV_HEADS, HEAD_DIM), lambda b: (b, layer, 0, 0, 0))
    n_in = 5
    return pl.pallas_call(
        _attn_ctx_kernel,
        out_shape=(jax.ShapeDtypeStruct((nseq * seq, D_MODEL), BF16), cache, cache),
        grid=(nseq,),
        in_specs=[pl.BlockSpec((seq, D_MODEL), lambda b: (b, COL_Q // D_MODEL)),
                  pl.BlockSpec((seq, kvw), lambda b: (b, COL_K // kvw)),
                  pl.BlockSpec((seq, kvw), lambda b: (b, COL_V // kvw)),
                  pl.BlockSpec((1, HEAD_DIM), lambda b: (0, 0)),
                  pl.BlockSpec((1, HEAD_DIM), lambda b: (0, 0))] + [pl.BlockSpec(memory_space=pl.ANY)] * len(earlier),
        out_specs=(pl.BlockSpec((seq, D_MODEL), lambda b: (b, 0)), cache_spec, cache_spec),
        input_output_aliases={n_in + i: 1 + i for i in range(len(earlier))},
        compiler_params=_params(("parallel",), 32),
        name="attn_ctx",
    )(proj, proj, proj, qg, kg, *earlier)


def _rope(t, cos, sa, sb):
    return t * cos + pltpu.roll(t, 96, 1) * sa + pltpu.roll(t, 32, 1) * sb


def _prep_lat_kernel(q_ref, k_ref, v_ref, ck_ref, cv_ref, cos_ref, sa_ref, sb_ref, qg_ref, kg_ref,
                     qo_ref, ko_ref, vo_ref):
    t = pl.program_id(1)

    @pl.when(t == 0)
    def _():
        ko_ref[...] = ck_ref[...].astype(BF16)
        vo_ref[...] = cv_ref[...].astype(BF16)

    @pl.when(t > 0)
    def _():
        cos, sa, sb = cos_ref[...], sa_ref[...], sb_ref[...]
        vo_ref[...] = v_ref[...]
        for h in range(N_KV_HEADS):
            hs = slice(h * HEAD_DIM, (h + 1) * HEAD_DIM)
            ko_ref[:, hs] = _rope(_rms(k_ref[:, hs].astype(F32)) * kg_ref[...], cos, sa, sb).astype(BF16)
        for h in range(N_HEADS):
            hs = slice(h * HEAD_DIM, (h + 1) * HEAD_DIM)
            qn = _rms(q_ref[:, hs].astype(F32)) * (qg_ref[...] * Q_SCALE)
            qo_ref[:, hs] = _rope(qn, cos, sa, sb).astype(BF16)


def _prep_lat(proj, cache_k, cache_v, cos, sa, sb, qg, kg, nb, seq):
    past = cache_k.shape[1]
    tr = past
    kvw = N_KV_HEADS * HEAD_DIM
    nt = seq // tr
    prev = lambda t: jnp.maximum(t - 1, 0)
    return pl.pallas_call(
        _prep_lat_kernel,
        out_shape=(jax.ShapeDtypeStruct((nb, seq, D_MODEL), BF16),
                   jax.ShapeDtypeStruct((nb, past + seq, kvw), BF16),
                   jax.ShapeDtypeStruct((nb, past + seq, kvw), BF16)),
        grid=(nb, nt + 1),
        in_specs=[pl.BlockSpec((tr, D_MODEL), lambda b, t: (b * nt + prev(t), COL_Q // D_MODEL)),
                  pl.BlockSpec((tr, kvw), lambda b, t: (b * nt + prev(t), COL_K // kvw)),
                  pl.BlockSpec((tr, kvw), lambda b, t: (b * nt + prev(t), COL_V // kvw)),
                  pl.BlockSpec((None, past, kvw), lambda b, t: (b, 0, 0)),
                  pl.BlockSpec((None, past, kvw), lambda b, t: (b, 0, 0)),
                  pl.BlockSpec((tr, HEAD_DIM), lambda b, t: (prev(t), 0)),
                  pl.BlockSpec((tr, HEAD_DIM), lambda b, t: (prev(t), 0)),
                  pl.BlockSpec((tr, HEAD_DIM), lambda b, t: (prev(t), 0)),
                  pl.BlockSpec((1, HEAD_DIM), lambda b, t: (0, 0)),
                  pl.BlockSpec((1, HEAD_DIM), lambda b, t: (0, 0))],
        out_specs=(pl.BlockSpec((None, tr, D_MODEL), lambda b, t: (b, prev(t), 0)),
                   pl.BlockSpec((None, tr, kvw), lambda b, t: (b, t, 0)),
                   pl.BlockSpec((None, tr, kvw), lambda b, t: (b, t, 0))),
        compiler_params=_params(("parallel", "arbitrary"), 32),
        name="prep_lat",
    )(proj, proj, proj, cache_k, cache_v, cos, sa, sb, qg, kg)


def _attn_lat_kernel(q_ref, k_ref, v_ref, o_ref):
    kb = k_ref[...]
    vb = v_ref[...]
    for g in range(Q_PER_KV):
        hs = slice(g * HEAD_DIM, (g + 1) * HEAD_DIM)
        s = _qk(q_ref[:, hs], kb)
        o_ref[:, hs] = _softmax_pv(s, vb).astype(BF16)


def _attn_lat(qr, kall, vall, tq):
    nb, seq, _ = qr.shape
    nk = kall.shape[1]
    gw = Q_PER_KV * HEAD_DIM
    return pl.pallas_call(
        _attn_lat_kernel,
        out_shape=jax.ShapeDtypeStruct((nb, seq, D_MODEL), BF16),
        grid=(nb, N_KV_HEADS, seq // tq),
        in_specs=[pl.BlockSpec((None, tq, gw), lambda b, h, i: (b, i, h)),
                  pl.BlockSpec((None, nk, HEAD_DIM), lambda b, h, i: (b, 0, h)),
                  pl.BlockSpec((None, nk, HEAD_DIM), lambda b, h, i: (b, 0, h))],
        out_specs=pl.BlockSpec((None, tq, gw), lambda b, h, i: (b, i, h)),
        compiler_params=_params(("parallel", "parallel", "arbitrary"), 48),
        name="attn_lat",
    )(qr, kall, vall)


def _ssd_kernel(*refs, reverse, has_h0, nc):
    it = iter(refs)
    xs_ref, bc_ref, dt_ref, xsp_ref, xsn_ref, bcp_ref, bcn_ref = (next(it) for _ in range(7))
    cwx_ref, cbx_ref, cwb_ref, cbb_ref, alog_ref, dtb_ref, e_ref, shift_ref = (next(it) for _ in range(8))
    h0_ref = next(it) if has_h0 else None
    if reverse:
        z_ref, yf_ref, d_ref, g_ref = (next(it) for _ in range(4))
    y_ref, hout_ref, ht_scr = refs[-3:]

    q = SSM_CHUNK
    npair = SSM_HEADS // 2
    c = pl.program_id(1)
    cc = nc - 1 - c if reverse else c
    first = cc == 0
    last = cc == nc - 1

    @pl.when(c == 0)
    def _():
        for p in range(npair):
            if has_h0:
                ht_scr[p] = h0_ref[p].T
            else:
                ht_scr[p] = jnp.zeros((q, q), F32)

    keep_prev = jnp.where(first, 0.0, 1.0).astype(BF16)
    keep_next = jnp.where(last, 0.0, 1.0).astype(BF16)

    def conv_silu(x_ref_, p_ref_, n_ref_, w_ref_, b_ref_):
        x = x_ref_[...]
        ext = jnp.concatenate([x, p_ref_[...] * keep_prev, n_ref_[...] * keep_next], axis=0)
        nb = jnp.dot(shift_ref[...], ext, preferred_element_type=F32)
        w = w_ref_[...]
        return _silu(w[0:1, :] * nb[:q, :] + w[1:2, :] * x.astype(F32) + w[2:3, :] * nb[q:, :] + b_ref_[...])

    xs = conv_silu(xs_ref, xsp_ref, xsn_ref, cwx_ref, cbx_ref)
    bc = conv_silu(bc_ref, bcp_ref, bcn_ref, cwb_ref, cbb_ref)

    off = SSM_HEADS if reverse else 0
    dt = jax.nn.softplus(dt_ref[...].astype(F32) + dtb_ref[...])
    da = dt * (-jnp.exp(alog_ref[...]))
    ii = lax.broadcasted_iota(jnp.int32, (q, q), 0)
    jj = lax.broadcasted_iota(jnp.int32, (q, q), 1)
    tri = (jj >= ii) if reverse else (ii >= jj)
    tri_b = jnp.where(tri, 1.0, 0.0).astype(BF16)
    tri_tb = jnp.where((ii >= jj) if reverse else (jj >= ii), 1.0, 0.0).astype(BF16)
    acs3 = jnp.dot(tri_b, jnp.concatenate(_split3(da), axis=1), preferred_element_type=F32)
    acs = acs3[:, :LANES] + acs3[:, LANES:2 * LANES] + acs3[:, 2 * LANES:]
    acs_t = jnp.dot(jnp.concatenate(_split3(da.T), axis=1), jnp.concatenate([tri_tb] * 3, axis=0),
                    preferred_element_type=F32)
    spread = jnp.dot(jnp.concatenate(_split3(jnp.concatenate([dt, acs], axis=0)), axis=1), e_ref[...],
                     preferred_element_type=F32)
    acs_x = spread[q:, :]
    tot_x = acs_x[0:1, :] if reverse else acs_x[q - 1:q, :]
    xdt = xs * spread[:q, :]
    xw = xdt * jnp.exp(tot_x - acs_x)
    eacs_x = jnp.exp(acs_x)
    etot_x = jnp.exp(tot_x)
    lo = lax.broadcasted_iota(jnp.int32, (q, LANES), 1) < SSM_HEADDIM

    parts = []
    for g in range(2):
        bg = bc[:, g * q:(g + 1) * q]
        cg = bc[:, (2 + g) * q:(3 + g) * q].astype(BF16)
        cb = _qk(cg, bg.astype(BF16))
        bgt = bg.T.astype(BF16)
        for pp in range(npair // 2):
            p = g * (npair // 2) + pp
            ps = slice(p * LANES, (p + 1) * LANES)
            ms = []
            for hh in range(2):
                ln = off + 2 * p + hh
                dec = jnp.where(tri, jnp.exp(acs[:, ln:ln + 1] - acs_t[ln:ln + 1, :]), 0.0)
                ms.append((cb * dec).astype(BF16))
            m = jnp.concatenate(ms, axis=1)
            xp = xdt[:, ps]
            bd = jnp.concatenate([jnp.where(lo, xp, 0.0), jnp.where(lo, 0.0, xp)], axis=0).astype(BF16)
            ht = ht_scr[p]
            yp = jnp.dot(m, bd, preferred_element_type=F32)
            yp = yp + jnp.dot(cg, ht.astype(BF16), preferred_element_type=F32) * eacs_x[:, ps]
            ht_scr[p] = etot_x[:, ps] * ht + jnp.dot(bgt, xw[:, ps].astype(BF16), preferred_element_type=F32)
            parts.append(yp)
    y = jnp.concatenate(parts, axis=1)

    if reverse:
        y = y + yf_ref[...] + d_ref[...] * xs
        y = y * _silu(z_ref[...].astype(F32))
        y_ref[...] = (_rms(y) * g_ref[...]).astype(y_ref.dtype)
    else:
        y_ref[...] = y

    @pl.when(c == nc - 1)
    def _():
        for p in range(npair):
            hout_ref[p] = ht_scr[p].T


def _ssd_pass(proj, lp, d, nseq, seq, h0, yf, layer=0, depth=1, earlier=None):
    reverse = yf is not None
    has_h0 = h0 is not None
    q = SSM_CHUNK
    nc = seq // q
    rows = nseq * seq
    npair = SSM_HEADS // 2
    hb = q // HALO
    bcw = 4 * SSM_STATE
    chunk = (lambda c: nc - 1 - c) if reverse else (lambda c: c)
    blk = lambda s, c: s * nc + chunk(c)
    prev8 = lambda s, c: jnp.maximum(blk(s, c) * hb - 1, 0)
    next8 = lambda s, c: jnp.minimum((blk(s, c) + 1) * hb, rows // HALO - 1)
    const = lambda s, c: (0, 0)
    in_specs = [
        pl.BlockSpec((q, D_MODEL), lambda s, c: (blk(s, c), COL_XS // D_MODEL)),
        pl.BlockSpec((q, bcw), lambda s, c: (blk(s, c), COL_BC // bcw)),
        pl.BlockSpec((q, LANES), lambda s, c: (blk(s, c), COL_DT // LANES)),
        pl.BlockSpec((HALO, D_MODEL), lambda s, c: (prev8(s, c), COL_XS // D_MODEL)),
        pl.BlockSpec((HALO, D_MODEL), lambda s, c: (next8(s, c), COL_XS // D_MODEL)),
        pl.BlockSpec((HALO, bcw), lambda s, c: (prev8(s, c), COL_BC // bcw)),
        pl.BlockSpec((HALO, bcw), lambda s, c: (next8(s, c), COL_BC // bcw)),
        pl.BlockSpec((3, D_MODEL), const), pl.BlockSpec((1, D_MODEL), const),
        pl.BlockSpec((3, bcw), const), pl.BlockSpec((1, bcw), const),
        pl.BlockSpec((1, LANES), const), pl.BlockSpec((1, LANES), const),
        pl.BlockSpec((3 * LANES, D_MODEL), const),
        pl.BlockSpec((2 * q, q + 2 * HALO), const),
    ]
    args = [proj] * 7 + [lp["cw_x"], lp["cb_x"], lp["cw_bc"], lp["cb_bc"], lp["alog"][d], lp["dtb"][d],
                         lp["expand"][d], lp["shift"]]
    if has_h0:
        in_specs.append(pl.BlockSpec((None, npair, q, q), lambda s, c: (s, 0, 0, 0)))
        args.append(h0)
    if reverse:
        in_specs += [pl.BlockSpec((q, D_MODEL), lambda s, c: (blk(s, c), COL_Z // D_MODEL)),
                     pl.BlockSpec((q, D_MODEL), lambda s, c: (blk(s, c), 0)),
                     pl.BlockSpec((1, D_MODEL), const), pl.BlockSpec((1, D_MODEL), const)]
        args += [proj, yf, lp["d_row"], lp["ssm_norm_g"]]
    aliases = {}
    if earlier is not None:
        aliases = {len(args): 1}
        in_specs.append(pl.BlockSpec(memory_space=pl.ANY))
        args.append(earlier)
    return pl.pallas_call(
        functools.partial(_ssd_kernel, reverse=reverse, has_h0=has_h0, nc=nc),
        out_shape=(jax.ShapeDtypeStruct((rows, D_MODEL), BF16 if reverse else F32),
                   jax.ShapeDtypeStruct((nseq, depth, npair, q, q), F32)),
        grid=(nseq, nc),
        in_specs=in_specs,
        out_specs=(pl.BlockSpec((q, D_MODEL), lambda s, c: (blk(s, c), 0)),
                   pl.BlockSpec((None, None, npair, q, q), lambda s, c: (s, layer, 0, 0, 0))),
        input_output_aliases=aliases,
        scratch_shapes=[pltpu.VMEM((npair, q, q), F32)],
        compiler_params=_params(("parallel", "arbitrary"), 32),
        name="ssd_bwd" if reverse else "ssd_fwd",
    )(*args)


ROUTE_I1, ROUTE_I2, ROUTE_P1, ROUTE_P2, ROUTE_R1, ROUTE_R2 = 0, 1, 2, 3, 4, 5


def _pack_bf16_pairs(v):
    n = v.shape[1] // 2
    bits = lax.bitcast_convert_type(v.astype(BF16).astype(F32), jnp.uint32)
    return lax.bitcast_convert_type((bits[:, :n] >> 16) | bits[:, n:], jnp.int32)


def _unpack_bf16_pairs(p):
    u = lax.bitcast_convert_type(p, jnp.uint32)
    lo = lax.bitcast_convert_type(u << 16, F32)
    hi = lax.bitcast_convert_type(u & jnp.uint32(0xFFFF0000), F32)
    return jnp.concatenate([lo, hi], axis=1)


def _top2_route(logits):
    tm = logits.shape[0]
    lane = lax.broadcasted_iota(jnp.int32, logits.shape, 1).astype(F32)
    lg = jnp.where(lane < N_EXPERTS, logits, -jnp.inf)
    m1 = jnp.max(lg, axis=-1, keepdims=True)
    i1 = jnp.min(jnp.where(lg == m1, lane, float(LANES)), axis=-1, keepdims=True)
    lg2 = jnp.where(lane == i1, -jnp.inf, lg)
    m2 = jnp.max(lg2, axis=-1, keepdims=True)
    i2 = jnp.min(jnp.where(lg2 == m2, lane, float(LANES)), axis=-1, keepdims=True)
    e = jnp.exp(m2 - m1)
    p1 = 1.0 / (1.0 + e)
    chosen = jnp.where(lane == i1, 1.0, jnp.where(lane == i2, 1.0, 0.0))
    earlier = (lax.broadcasted_iota(jnp.int32, (tm, tm), 0) > lax.broadcasted_iota(jnp.int32, (tm, tm), 1))
    ranks = _bdot(jnp.where(earlier, 1.0, 0.0), chosen)
    r1 = jnp.sum(jnp.where(lane == i1, ranks, 0.0), axis=-1, keepdims=True)
    r2 = jnp.sum(jnp.where(lane == i2, ranks, 0.0), axis=-1, keepdims=True)
    rec = jnp.where(lane == ROUTE_I1, i1, jnp.where(lane == ROUTE_I2, i2, 0.0))
    rec = jnp.where(lane == ROUTE_P1, p1, jnp.where(lane == ROUTE_P2, e * p1, rec))
    rec = jnp.where(lane == ROUTE_R1, r1, jnp.where(lane == ROUTE_R2, r2, rec))
    return rec, jnp.sum(chosen, axis=0, keepdims=True)


def _merge_kernel(*refs, seq, tm, moe):
    it = iter(refs)
    attn_ref, ssm_ref, scb_ref, scc_ref, sch_ref = (next(it) for _ in range(5))
    sccp_ref, sccn_ref, schp_ref, schn_ref = (next(it) for _ in range(4))
    g0_ref, g1_ref, g2_ref, x_ref, mod_ref = (next(it) for _ in range(5))
    wa_ref, ws_ref, wc_ref, wm_ref, scw_ref, ng_ref = (next(it) for _ in range(6))
    router_ref = next(it) if moe else None
    xo_ref, h2_ref = next(it), next(it)
    route_ref, cnt_ref = (next(it), next(it)) if moe else (None, None)

    i = pl.program_id(0)
    pos = (lax.broadcasted_iota(jnp.int32, (tm, 1), 0) + i * tm) & (seq - 1)
    u = scc_ref[...].astype(F32) * sch_ref[...].astype(F32)
    up = sccp_ref[HALO - 1:HALO, :].astype(F32) * schp_ref[HALO - 1:HALO, :].astype(F32)
    un = sccn_ref[0:1, :].astype(F32) * schn_ref[0:1, :].astype(F32)
    r = lax.broadcasted_iota(jnp.int32, (tm, 1), 0)
    um1 = jnp.where(pos == 0, 0.0, jnp.where(r == 0, up, pltpu.roll(u, 1, 0)))
    up1 = jnp.where(pos == seq - 1, 0.0, jnp.where(r == tm - 1, un, pltpu.roll(u, tm - 1, 0)))
    w = scw_ref[...]
    sc = scb_ref[...].astype(F32) * (w[0:1, :] * um1 + w[1:2, :] * u + w[2:3, :] * up1)

    gate = lambda g_ref: _sigmoid(g_ref[...].astype(F32))
    merged = gate(g0_ref) * jnp.dot(attn_ref[...], wa_ref[...], preferred_element_type=F32)
    merged = merged + gate(g1_ref) * jnp.dot(ssm_ref[...], ws_ref[...], preferred_element_type=F32)
    merged = merged + gate(g2_ref) * jnp.dot(sc.astype(BF16), wc_ref[...], preferred_element_type=F32)
    mod = mod_ref[0]
    xn = x_ref[...] + mod[2:3, :] * jnp.dot(merged.astype(BF16), wm_ref[...], preferred_element_type=F32)
    xo_ref[...] = xn
    h2 = _rms(xn) * ng_ref[...] * (1.0 + mod[4:5, :]) + mod[3:4, :]
    if moe:
        h2_ref[...] = _pack_bf16_pairs(h2)
        h_hi = h2.astype(BF16)
        h_lo = (h2 - h_hi.astype(F32)).astype(BF16)
        both = jnp.dot(h_hi, router_ref[...], preferred_element_type=F32)
        logits = both[:, :LANES] + both[:, LANES:] + jnp.dot(h_lo, router_ref[:, :LANES], preferred_element_type=F32)
        rec, counts = _top2_route(logits)
        route_ref[...] = rec
        cnt_ref[...] = jnp.broadcast_to(counts, cnt_ref.shape)
    else:
        h2_ref[...] = h2.astype(BF16)


def _merge(attn, ssm, proj, x, mod, lp, seq, tm):
    rows = x.shape[0]
    per_mod = rows // mod.shape[0]
    tm = min(tm, per_mod)
    moe = lp["router"] is not None
    hb = tm // HALO
    prev8 = lambda i: jnp.maximum(i * hb - 1, 0)
    next8 = lambda i: jnp.minimum((i + 1) * hb, rows // HALO - 1)
    tile = lambda col: pl.BlockSpec((tm, D_MODEL), lambda i: (i, col // D_MODEL))
    const = lambda shape: pl.BlockSpec(shape, lambda i: (0, 0))
    in_specs = [tile(0), tile(0), tile(COL_SCB), tile(COL_SCC), tile(COL_SCH),
                pl.BlockSpec((HALO, D_MODEL), lambda i: (prev8(i), COL_SCC // D_MODEL)),
                pl.BlockSpec((HALO, D_MODEL), lambda i: (next8(i), COL_SCC // D_MODEL)),
                pl.BlockSpec((HALO, D_MODEL), lambda i: (prev8(i), COL_SCH // D_MODEL)),
                pl.BlockSpec((HALO, D_MODEL), lambda i: (next8(i), COL_SCH // D_MODEL)),
                tile(COL_G), tile(COL_G + D_MODEL), tile(COL_G + 2 * D_MODEL), tile(0),
                pl.BlockSpec((1, SUBLANES, D_MODEL), lambda i: (i * tm // per_mod, 0, 0)),
                _resident((D_MODEL, D_MODEL)), _resident((D_MODEL, D_MODEL)), _resident((D_MODEL, D_MODEL)),
                _resident((D_MODEL, D_MODEL)), const((3, D_MODEL)), const((1, D_MODEL))]
    args = [attn, ssm, proj, proj, proj, proj, proj, proj, proj, proj, proj, proj, x, mod,
            lp["w_attn_out"], lp["w_ssm_out"], lp["w_sconv_out"], lp["w_merge"], lp["sconv_w"], lp["norm_ffn_g"]]
    out_shape = [jax.ShapeDtypeStruct((rows, D_MODEL), F32), jax.ShapeDtypeStruct((rows, D_MODEL), BF16)]
    out_specs = [tile(0), tile(0)]
    if moe:
        in_specs.append(_resident((D_MODEL, 2 * LANES)))
        args.append(lp["router"])
        out_shape[1] = jax.ShapeDtypeStruct((rows, D_MODEL // 2), jnp.int32)
        out_specs[1] = pl.BlockSpec((tm, D_MODEL // 2), lambda i: (i, 0))
        out_shape += [jax.ShapeDtypeStruct((rows, LANES), F32),
                      jax.ShapeDtypeStruct((rows // tm * SUBLANES, LANES), F32)]
        out_specs += [pl.BlockSpec((tm, LANES), lambda i: (i, 0)), pl.BlockSpec((SUBLANES, LANES), lambda i: (i, 0))]
    return pl.pallas_call(
        functools.partial(_merge_kernel, seq=seq, tm=tm, moe=moe),
        out_shape=tuple(out_shape),
        grid=(rows // tm,),
        in_specs=in_specs,
        out_specs=tuple(out_specs),
        compiler_params=_params(("parallel",), 56),
        name="merge",
    )(*args)


def _swiglu_acc(h, w1_ref, w3_ref, w2_ref, fs):
    a = _silu(jnp.dot(h, w1_ref[:, fs], preferred_element_type=F32)) * jnp.dot(h, w3_ref[:, fs],
                                                                                preferred_element_type=F32)
    return jnp.dot(a.astype(BF16), w2_ref[fs, :], preferred_element_type=F32)


def _swiglu(h, w1_ref, w3_ref, w2_ref):
    half = D_FF // 2
    return (_swiglu_acc(h, w1_ref, w3_ref, w2_ref, slice(0, half))
            + _swiglu_acc(h, w1_ref, w3_ref, w2_ref, slice(half, D_FF)))


def _ffn_kernel(h_ref, x_ref, mod_ref, w1_ref, w3_ref, w2_ref, o_ref):
    o_ref[...] = x_ref[...] + mod_ref[0, 5:6, :] * _swiglu(h_ref[...], w1_ref, w3_ref, w2_ref)


def _resident(shape):
    return pl.BlockSpec(shape, lambda *_: (0,) * len(shape), pipeline_mode=pl.Buffered(1))


def _ffn(h2, x, mod, w1, w3, w2, tm):
    rows = x.shape[0]
    per_mod = rows // mod.shape[0]
    tm = min(tm, per_mod)
    return pl.pallas_call(
        _ffn_kernel,
        out_shape=jax.ShapeDtypeStruct((rows, D_MODEL), F32),
        grid=(rows // tm,),
        in_specs=[pl.BlockSpec((tm, D_MODEL), lambda i: (i, 0)),
                  pl.BlockSpec((tm, D_MODEL), lambda i: (i, 0)),
                  pl.BlockSpec((1, SUBLANES, D_MODEL), lambda i: (i * tm // per_mod, 0, 0)),
                  _resident((D_MODEL, D_FF)), _resident((D_MODEL, D_FF)), _resident((D_FF, D_MODEL))],
        out_specs=pl.BlockSpec((tm, D_MODEL), lambda i: (i, 0)),
        compiler_params=_params(("parallel",), 56),
        name="ffn",
    )(h2, x, mod, w1, w3, w2)


def _cast_kernel(*refs):
    n = len(refs) // 2
    for src, dst in zip(refs[:n], refs[n:]):
        dst[...] = src[...].astype(dst.dtype)


def _cast_bf16(ws):
    e, r, c = ws[0].shape
    rb = max(d for d in range(16, r + 1, 16) if r % d == 0 and d * c * 4 <= 3 * MIB)
    spec = pl.BlockSpec((None, rb, c), lambda i, j: (i, j, 0))
    return pl.pallas_call(
        _cast_kernel,
        out_shape=tuple(jax.ShapeDtypeStruct(w.shape, BF16) for w in ws),
        grid=(e, r // rb),
        in_specs=[spec] * len(ws),
        out_specs=tuple([spec] * len(ws)),
        compiler_params=_params(("parallel", "parallel"), 48),
        name="cast_bf16",
    )(*ws)


SC_CORES = 2
SC_SUBCORES = 16
SC_CHUNK = 128


def _sc_gather(table, idx):
    nw = SC_CORES * SC_SUBCORES
    b, d = idx.shape[0], table.shape[1]
    per_w = b // nw
    assert per_w * nw == b and per_w % SC_CHUNK == 0
    mesh = plsc.VectorSubcoreMesh(core_axis_name="c", subcore_axis_name="s")

    def body(table_hbm, idx_hbm, out_hbm, idx_v, rows_v, sem):
        wid = lax.axis_index("s") * SC_CORES + lax.axis_index("c")

        @pl.loop(0, per_w // SC_CHUNK)
        def _(j):
            off = pl.multiple_of(wid * per_w + j * SC_CHUNK, SC_CHUNK)
            pltpu.sync_copy(idx_hbm.at[pl.ds(off, SC_CHUNK)], idx_v)
            pltpu.async_copy(table_hbm.at[idx_v], rows_v, sem).wait()
            pltpu.sync_copy(rows_v, out_hbm.at[pl.ds(off, SC_CHUNK)])

    return pl.kernel(
        body,
        out_type=jax.ShapeDtypeStruct((b, d), table.dtype),
        mesh=mesh,
        scratch_types=[pltpu.VMEM((SC_CHUNK,), jnp.int32), pltpu.VMEM((SC_CHUNK, d), table.dtype),
                       pltpu.SemaphoreType.DMA],
        name="sc_gather",
    )(table, idx)


def _route_tables(route, counts, tile):
    t = route.shape[0]
    n_mt = counts.shape[0] // SUBLANES
    cnt = counts.reshape(n_mt, SUBLANES, LANES)[:, 0, :N_EXPERTS].astype(jnp.int32)
    incl = jnp.cumsum(cnt, axis=0)
    padded = (incl[-1] + tile - 1) // tile * tile
    gend = jnp.cumsum(padded)
    base = jnp.repeat((gend - padded)[None, :] + incl - cnt, t // n_mt, axis=0)
    e = route[:, ROUTE_I1:ROUTE_I2 + 1].astype(jnp.int32)
    r = route[:, ROUTE_R1:ROUTE_R2 + 1].astype(jnp.int32)
    pick = e[:, :, None] == jnp.arange(N_EXPERTS)[None, None, :]
    slot = jnp.sum(jnp.where(pick, base[:, None, :], 0), axis=-1) + r
    n_slots = 2 * t + N_EXPERTS * tile
    tok_of_slot = (jnp.arange(n_slots, dtype=jnp.int32) % t).at[slot.reshape(-1)].set(
        jnp.repeat(jnp.arange(t, dtype=jnp.int32), 2))
    n_valid = gend[-1] // tile
    n_tiles = n_slots // tile
    tile_expert = jnp.sum(jnp.arange(n_tiles)[:, None] * tile >= gend[None, :], axis=1)
    last = jnp.take(tile_expert, jnp.maximum(n_valid - 1, 0))
    tile_expert = jnp.where(jnp.arange(n_tiles) < n_valid, tile_expert, last).astype(jnp.int32)
    return slot.T.reshape(-1).astype(jnp.int32), tok_of_slot, tile_expert, n_valid.reshape(1).astype(jnp.int32)


def _moe_ffn_kernel(te_ref, nv_ref, h_ref, w1_ref, w3_ref, w2_ref, o_ref):
    i = pl.program_id(0)

    @pl.when(i < nv_ref[0])
    def _():
        h = _unpack_bf16_pairs(h_ref[...]).astype(BF16)
        o_ref[...] = _pack_bf16_pairs(_swiglu(h, w1_ref, w3_ref, w2_ref))

    @pl.when(i >= nv_ref[0])
    def _():
        o_ref[...] = jnp.zeros(o_ref.shape, jnp.int32)


def _moe_ffn(hs, tile_expert, n_valid, w1, w3, w2, tile):
    n_slots = hs.shape[0]
    grid_spec = pltpu.PrefetchScalarGridSpec(
        num_scalar_prefetch=2,
        grid=(n_slots // tile,),
        in_specs=[pl.BlockSpec((tile, D_MODEL // 2), lambda i, te, nv: (i, 0)),
                  pl.BlockSpec((None, D_MODEL, D_FF), lambda i, te, nv: (te[i], 0, 0)),
                  pl.BlockSpec((None, D_MODEL, D_FF), lambda i, te, nv: (te[i], 0, 0)),
                  pl.BlockSpec((None, D_FF, D_MODEL), lambda i, te, nv: (te[i], 0, 0))],
        out_specs=pl.BlockSpec((tile, D_MODEL // 2), lambda i, te, nv: (i, 0)),
    )
    return pl.pallas_call(
        _moe_ffn_kernel,
        out_shape=jax.ShapeDtypeStruct((n_slots, D_MODEL // 2), jnp.int32),
        grid_spec=grid_spec,
        compiler_params=_params(("arbitrary",), 56),
        name="moe_ffn",
    )(tile_expert, n_valid, hs, w1, w3, w2)


def _combine_kernel(x_ref, y1_ref, y2_ref, route_ref, mod_ref, o_ref):
    p1 = route_ref[:, ROUTE_P1:ROUTE_P1 + 1]
    p2 = route_ref[:, ROUTE_P2:ROUTE_P2 + 1]
    f = p1 * _unpack_bf16_pairs(y1_ref[...]) + p2 * _unpack_bf16_pairs(y2_ref[...])
    o_ref[...] = x_ref[...] + mod_ref[0, 5:6, :] * f


def _combine(x, yg, route, mod, tm):
    rows = x.shape[0]
    per_mod = rows // mod.shape[0]
    tm = min(tm, per_mod)
    return pl.pallas_call(
        _combine_kernel,
        out_shape=jax.ShapeDtypeStruct((rows, D_MODEL), F32),
        grid=(rows // tm,),
        in_specs=[pl.BlockSpec((tm, D_MODEL), lambda i: (i, 0)),
                  pl.BlockSpec((tm, D_MODEL // 2), lambda i: (i, 0)),
                  pl.BlockSpec((tm, D_MODEL // 2), lambda i: (i + rows // tm, 0)),
                  pl.BlockSpec((tm, LANES), lambda i: (i, 0)),
                  pl.BlockSpec((1, SUBLANES, D_MODEL), lambda i: (i * tm // per_mod, 0, 0))],
        out_specs=pl.BlockSpec((tm, D_MODEL), lambda i: (i, 0)),
        compiler_params=_params(("parallel",), 32),
        name="moe_combine",
    )(x, yg, yg, route, mod)


def _moe(h2p, x, mod, route, counts, w1, w3, w2, tile):
    slot, tok_of_slot, tile_expert, n_valid = _route_tables(route, counts, tile)
    hs = _sc_gather(h2p, tok_of_slot)
    ys = _moe_ffn(hs, tile_expert, n_valid, w1, w3, w2, tile)
    return _combine(x, _sc_gather(ys, slot), route, mod, tm=512)


def _layer_params(l, w_in, q_norm_g, k_norm_g, w_attn_out, ssm_conv_w, ssm_conv_b, ssm_dt_bias, ssm_a_log,
                  ssm_d, ssm_norm_g, w_ssm_out, sconv_w, w_sconv_out, w_merge, norm_mix_g, norm_ffn_g,
                  ffn_w1, ffn_w3, ffn_w2, moe_router, moe_w1, moe_w3, moe_w2):
    w = w_in[l]
    seg = lambda a, n: w[:, a:a + n]
    w_perm = jnp.concatenate(
        [seg(0, 1024), seg(1536, 1024), seg(4128, 1024), seg(5152, 1024), seg(6176, 1024), seg(7200, 3072),
         seg(2560, 1024), seg(3584, 512), seg(1024, 256), seg(1280, 256), seg(4096, 32),
         jnp.zeros((D_MODEL, N_PROJ - COL_DT - 2 * SSM_HEADS), w.dtype)], axis=1).astype(BF16)
    row = lambda v: v.reshape(1, -1)
    lane_row = lambda v, off: jnp.zeros((1, LANES), F32).at[0, off:off + SSM_HEADS].set(v)
    head_of_col = jnp.arange(D_MODEL) // SSM_HEADDIM
    expand = [jnp.tile((jnp.arange(LANES)[:, None] == head_of_col[None, :] + off).astype(BF16), (3, 1))
              for off in (0, SSM_HEADS)]
    r = jnp.arange(2 * SSM_CHUNK)[:, None]
    cidx = jnp.arange(SSM_CHUNK + 2 * HALO)[None, :]
    up = (r < SSM_CHUNK) & (((cidx == r - 1) & (r >= 1)) | ((r == 0) & (cidx == SSM_CHUNK + HALO - 1)))
    dn = (r >= SSM_CHUNK) & (((cidx == r - SSM_CHUNK + 1) & (r < 2 * SSM_CHUNK - 1))
                             | ((r == 2 * SSM_CHUNK - 1) & (cidx == SSM_CHUNK + HALO)))
    shift = (up | dn).astype(BF16)
    lp = {
        "w_in": w_perm,
        "norm_mix_g": row(norm_mix_g[l]), "norm_ffn_g": row(norm_ffn_g[l]),
        "q_norm_g": row(q_norm_g[l]), "k_norm_g": row(k_norm_g[l]),
        "w_attn_out": w_attn_out[l].astype(BF16), "w_ssm_out": w_ssm_out[l].astype(BF16),
        "w_sconv_out": w_sconv_out[l].astype(BF16), "w_merge": w_merge[l].astype(BF16),
        "cw_x": ssm_conv_w[l][:, :D_MODEL], "cw_bc": ssm_conv_w[l][:, D_MODEL:],
        "cb_x": row(ssm_conv_b[l][:D_MODEL]), "cb_bc": row(ssm_conv_b[l][D_MODEL:]),
        "alog": [lane_row(ssm_a_log[l, d], d * SSM_HEADS) for d in range(2)],
        "dtb": [lane_row(ssm_dt_bias[l, d], d * SSM_HEADS) for d in range(2)],
        "expand": expand,
        "shift": shift,
        "d_row": row(jnp.repeat(ssm_d[l], SSM_HEADDIM)),
        "ssm_norm_g": row(ssm_norm_g[l]),
        "sconv_w": sconv_w[l],
    }
    if l % 2 == 0:
        lp["router"] = None
        lp["ffn"] = tuple(t[l // 2].astype(BF16) for t in (ffn_w1, ffn_w3, ffn_w2))
    else:
        router = jnp.pad(moe_router[l // 2], ((0, 0), (0, LANES - N_EXPERTS)))
        r_hi = router.astype(BF16)
        lp["router"] = jnp.concatenate([r_hi, (router - r_hi.astype(F32)).astype(BF16)], axis=1)
        lp["ffn"] = _cast_bf16((moe_w1[l // 2], moe_w3[l // 2])) + _cast_bf16((moe_w2[l // 2],))
    return lp


def _rope_tables(seq):
    t = jnp.arange(seq)
    n = HEAD_DIM // 4
    inv = ROPE_THETA ** (-jnp.arange(n, dtype=F32) / n)
    ang_r = (t // GRID_W).astype(F32)[:, None] * inv[None, :]
    ang_c = (t % GRID_W).astype(F32)[:, None] * inv[None, :]
    zero = jnp.zeros_like(ang_r)
    cos = jnp.concatenate([jnp.cos(ang_r)] * 2 + [jnp.cos(ang_c)] * 2, axis=1)
    sa = jnp.concatenate([-jnp.sin(ang_r), zero, -jnp.sin(ang_c), zero], axis=1)
    sb = jnp.concatenate([zero, jnp.sin(ang_r), zero, jnp.sin(ang_c)], axis=1)
    return cos, sa, sb


def _trunk(x, mod, lp, nseq, seq, lat, layer=0, depth=1, carry=None):
    rows = nseq * seq
    proj = _inproj(x, mod, lp["norm_mix_g"], lp["w_in"], tm=min(2048, rows), tn=1152)
    if lat is None:
        attn, k_new, v_new = _attn_ctx(proj, lp["q_norm_g"], lp["k_norm_g"], nseq, seq, layer, depth,
                                       () if carry is None else carry[:2])
        yf, h_f = _ssd_pass(proj, lp, 0, nseq, seq, None, None, layer, depth, None if carry is None else carry[2])
        ssm, h_b = _ssd_pass(proj, lp, 1, nseq, seq, None, yf, layer, depth, None if carry is None else carry[3])
    else:
        cache_k, cache_v, h0_f, h0_b, tables = lat
        qr, kall, vall = _prep_lat(proj, cache_k, cache_v, *tables, lp["q_norm_g"], lp["k_norm_g"], nseq, seq)
        attn = _attn_lat(qr, kall, vall, tq=256).reshape(rows, D_MODEL)
        k_new = v_new = None
        yf, h_f = _ssd_pass(proj, lp, 0, nseq, seq, h0_f, None)
        ssm, h_b = _ssd_pass(proj, lp, 1, nseq, seq, h0_b, yf)
    outs = _merge(attn, ssm, proj, x, mod, lp, seq, tm=256)
    w1, w3, w2 = lp["ffn"]
    if lp["router"] is None:
        x = _ffn(outs[1], outs[0], mod, w1, w3, w2, tm=512)
    else:
        x = _moe(outs[1], outs[0], mod, outs[2], outs[3], w1, w3, w2, tile=512)
    return x, (k_new, v_new, h_f, h_b)


def kernel(x_prompt, x_sample, cache_k, cache_v, state_ssm_fwd, state_ssm_bwd, c, c_ctx, w_mod, b_mod, norm_mix_g, norm_ffn_g, w_in, q_norm_g, k_norm_g, w_attn_out, ssm_conv_w, ssm_conv_b, ssm_dt_bias, ssm_a_log, ssm_d, ssm_norm_g, w_ssm_out, sconv_w, w_sconv_out, w_merge, ffn_w1, ffn_w3, ffn_w2, moe_router, moe_w1, moe_w3, moe_w2):
    batch, seq, _ = x_prompt.shape
    nb, dseq, _ = x_sample.shape
    depth = w_in.shape[0]
    past = cache_k.shape[2]
    kvw = N_KV_HEADS * HEAD_DIM
    npair = SSM_HEADS // 2
    assert nb + 1 <= SUBLANES

    cond = jnp.zeros((SUBLANES, D_MODEL), F32).at[0].set(c_ctx).at[1:1 + nb].set(c)
    tables = _rope_tables(dseq)
    y_ctx = x_prompt.reshape(batch * seq, D_MODEL)
    y_lat = x_sample.reshape(nb * dseq, D_MODEL)
    mods = _modulation(cond, w_mod, b_mod.reshape(depth, 1, -1))
    carry = None
    for l in range(depth):
        lp = _layer_params(l, w_in, q_norm_g, k_norm_g, w_attn_out, ssm_conv_w, ssm_conv_b, ssm_dt_bias,
                           ssm_a_log, ssm_d, ssm_norm_g, w_ssm_out, sconv_w, w_sconv_out, w_merge,
                           norm_mix_g, norm_ffn_g, ffn_w1, ffn_w3, ffn_w2, moe_router, moe_w1, moe_w3, moe_w2)
        m = jnp.pad(mods[l].reshape(SUBLANES, 6, D_MODEL), ((0, 0), (0, SUBLANES - 6), (0, 0)))
        y_ctx, carry = _trunk(y_ctx, m[0:1], lp, batch, seq, None, l, depth, carry)
        lat = (cache_k[:, l].reshape(nb, past, kvw), cache_v[:, l].reshape(nb, past, kvw),
               state_ssm_fwd[:, l].reshape(nb, npair, SSM_CHUNK, SSM_STATE),
               state_ssm_bwd[:, l].reshape(nb, npair, SSM_CHUNK, SSM_STATE), tables)
        y_lat = _trunk(y_lat, m[1:1 + nb], lp, nb, dseq, lat)[0]
    k_all, v_all, hf_all, hb_all = carry
    state_shape = (batch, depth, SSM_HEADS, SSM_HEADDIM, SSM_STATE)
    return (y_ctx.reshape(batch, seq, D_MODEL), y_lat.reshape(nb, dseq, D_MODEL),
            k_all, v_all,
            hf_all.reshape(state_shape), hb_all.reshape(state_shape))
```

```python
import functools

import jax
import jax.numpy as jnp
import numpy as np
from jax import lax
from jax.experimental import pallas as pl
from jax.experimental.pallas import tpu as pltpu
from jax.experimental.pallas import tpu_sc as plsc

F32 = jnp.float32
BF16 = jnp.bfloat16
HIGHEST = lax.Precision.HIGHEST

D_MODEL = 1024
HEAD_DIM = 128
N_HEADS = 8
N_KV_HEADS = 2
Q_PER_KV = N_HEADS // N_KV_HEADS
ROPE_THETA = 10000.0
GRID_W = 64
SSM_HEADS = 16
SSM_HEADDIM = 64
SSM_STATE = 128
SSM_CHUNK = 128
D_FF = 2816
N_EXPERTS = 8
EPS = 1e-6
Q_SCALE = 1.4426950408889634 * HEAD_DIM ** -0.5

COL_Q, COL_Z, COL_SCB, COL_SCC, COL_SCH, COL_G = 0, 1024, 2048, 3072, 4096, 5120
COL_XS, COL_BC, COL_K, COL_V, COL_DT = 8192, 9216, 9728, 9984, 10240
N_PROJ = 10368
LANES = 128
SUBLANES = 8
HALO = 16
MIB = 1024 * 1024


def _params(sem, vmem_mib):
    return pltpu.CompilerParams(dimension_semantics=sem, vmem_limit_bytes=vmem_mib * MIB)


def _sigmoid(t):
    return 1.0 / (1.0 + jnp.exp(-t))


def _silu(t):
    return t * _sigmoid(t)


def _rms(t):
    return t * lax.rsqrt(jnp.mean(t * t, axis=-1, keepdims=True) + EPS)


def _bdot(a, b):
    return jnp.dot(a.astype(BF16), b.astype(BF16), preferred_element_type=F32)


def _hdot(a, b):
    return jnp.dot(a, b, precision=HIGHEST, preferred_element_type=F32)


def _split3(t):
    a = t.astype(BF16)
    r = t - a.astype(F32)
    b = r.astype(BF16)
    return a, b, (r - b.astype(F32)).astype(BF16)


def _mod_kernel(c_ref, w_ref, b_ref, o_ref):
    o_ref[...] = _hdot(_silu(c_ref[...]), w_ref[...]) + b_ref[...]


def _modulation(cond8, w, b):
    depth, _, width = w.shape
    return pl.pallas_call(
        _mod_kernel,
        out_shape=jax.ShapeDtypeStruct((depth, SUBLANES, width), F32),
        grid=(depth, width // D_MODEL),
        in_specs=[pl.BlockSpec((SUBLANES, D_MODEL), lambda l, j: (0, 0)),
                  pl.BlockSpec((None, D_MODEL, D_MODEL), lambda l, j: (l, 0, j)),
                  pl.BlockSpec((None, 1, D_MODEL), lambda l, j: (l, 0, j))],
        out_specs=pl.BlockSpec((None, SUBLANES, D_MODEL), lambda l, j: (l, 0, j)),
        compiler_params=_params(("parallel", "parallel"), 32),
        name="modulation",
    )(cond8, w, b)


def _inproj_kernel(x_ref, mod_ref, g_ref, w_ref, o_ref, h_scr):
    @pl.when(pl.program_id(1) == 0)
    def _():
        h = _rms(x_ref[...]) * g_ref[...]
        h = h * (1.0 + mod_ref[0, 1:2, :]) + mod_ref[0, 0:1, :]
        h_scr[...] = h.astype(BF16)

    o_ref[...] = jnp.dot(h_scr[...], w_ref[...], preferred_element_type=F32).astype(o_ref.dtype)


def _inproj(x, mod, g, w, tm, tn):
    rows = x.shape[0]
    per_mod = rows // mod.shape[0]
    tm = min(tm, per_mod)
    return pl.pallas_call(
        _inproj_kernel,
        out_shape=jax.ShapeDtypeStruct((rows, N_PROJ), BF16),
        grid=(rows // tm, N_PROJ // tn),
        in_specs=[pl.BlockSpec((tm, D_MODEL), lambda i, j: (i, 0)),
                  pl.BlockSpec((1, SUBLANES, D_MODEL), lambda i, j: (i * tm // per_mod, 0, 0)),
                  pl.BlockSpec((1, D_MODEL), lambda i, j: (0, 0)),
                  pl.BlockSpec((D_MODEL, tn), lambda i, j: (0, j))],
        out_specs=pl.BlockSpec((tm, tn), lambda i, j: (i, j)),
        scratch_shapes=[pltpu.VMEM((tm, D_MODEL), BF16)],
        compiler_params=_params(("parallel", "arbitrary"), 48),
        name="inproj",
    )(x, mod, g, w)


def _softmax_pv(s, vb):
    m = jnp.max(s, axis=-1, keepdims=True)
    p = jnp.exp2(s - m)
    l = jnp.sum(p, axis=-1, keepdims=True)
    return jnp.dot(p.astype(BF16), vb, preferred_element_type=F32) / l


def _qk(qb, kb):
    return lax.dot_general(qb, kb, (((1,), (1,)), ((), ())), preferred_element_type=F32)


def _attn_ctx_kernel(q_ref, k_ref, v_ref, qg_ref, kg_ref, *rest):
    a_ref, ko_ref, vo_ref = rest[-3:]
    v = v_ref[...].astype(F32)
    for hk in range(N_KV_HEADS):
        ks = slice(hk * HEAD_DIM, (hk + 1) * HEAD_DIM)
        kn = _rms(k_ref[:, ks].astype(F32)) * kg_ref[...]
        ko_ref[:, hk, :] = kn
        vo_ref[:, hk, :] = v[:, ks]
        knb = kn.astype(BF16)
        vb = v[:, ks].astype(BF16)
        for g in range(Q_PER_KV):
            hs = slice((hk * Q_PER_KV + g) * HEAD_DIM, (hk * Q_PER_KV + g + 1) * HEAD_DIM)
            qn = _rms(q_ref[:, hs].astype(F32)) * (qg_ref[...] * Q_SCALE)
            s = _qk(qn.astype(BF16), knb)
            a_ref[:, hs] = _softmax_pv(s, vb).astype(BF16)


def _attn_ctx(proj, qg, kg, nseq, seq, layer, depth, earlier):
    kvw = N_KV_HEADS * HEAD_DIM
    cache = jax.ShapeDtypeStruct((nseq, depth, seq, N_KV_HEADS, HEAD_DIM), F32)
    cache_spec = pl.BlockSpec((None, None, seq, N_KV_HEADS, HEAD_DIM), lambda b: (b, layer, 0, 0, 0))
    n_in = 5
    return pl.pallas_call(
        _attn_ctx_kernel,
        out_shape=(jax.ShapeDtypeStruct((nseq * seq, D_MODEL), BF16), cache, cache),
        grid=(nseq,),
        in_specs=[pl.BlockSpec((seq, D_MODEL), lambda b: (b, COL_Q // D_MODEL)),
                  pl.BlockSpec((seq, kvw), lambda b: (b, COL_K // kvw)),
                  pl.BlockSpec((seq, kvw), lambda b: (b, COL_V // kvw)),
                  pl.BlockSpec((1, HEAD_DIM), lambda b: (0, 0)),
                  pl.BlockSpec((1, HEAD_DIM), lambda b: (0, 0))] + [pl.BlockSpec(memory_space=pl.ANY)] * len(earlier),
        out_specs=(pl.BlockSpec((seq, D_MODEL), lambda b: (b, 0)), cache_spec, cache_spec),
        input_output_aliases={n_in + i: 1 + i for i in range(len(earlier))},
        compiler_params=_params(("parallel",), 32),
        name="attn_ctx",
    )(proj, proj, proj, qg, kg, *earlier)


def _rope(t, cos, sa, sb):
    return t * cos + pltpu.roll(t, 96, 1) * sa + pltpu.roll(t, 32, 1) * sb


def _prep_lat_kernel(q_ref, k_ref, v_ref, ck_ref, cv_ref, cos_ref, sa_ref, sb_ref, qg_ref, kg_ref,
                     qo_ref, ko_ref, vo_ref):
    t = pl.program_id(1)

    @pl.when(t == 0)
    def _():
        ko_ref[...] = ck_ref[...].astype(BF16)
        vo_ref[...] = cv_ref[...].astype(BF16)

    @pl.when(t > 0)
    def _():
        cos, sa, sb = cos_ref[...], sa_ref[...], sb_ref[...]
        vo_ref[...] = v_ref[...]
        for h in range(N_KV_HEADS):
            hs = slice(h * HEAD_DIM, (h + 1) * HEAD_DIM)
            ko_ref[:, hs] = _rope(_rms(k_ref[:, hs].astype(F32)) * kg_ref[...], cos, sa, sb).astype(BF16)
        for h in range(N_HEADS):
            hs = slice(h * HEAD_DIM, (h + 1) * HEAD_DIM)
            qn = _rms(q_ref[:, hs].astype(F32)) * (qg_ref[...] * Q_SCALE)
            qo_ref[:, hs] = _rope(qn, cos, sa, sb).astype(BF16)


def _prep_lat(proj, cache_k, cache_v, cos, sa, sb, qg, kg, nb, seq):
    past = cache_k.shape[1]
    tr = past
    kvw = N_KV_HEADS * HEAD_DIM
    nt = seq // tr
    prev = lambda t: jnp.maximum(t - 1, 0)
    return pl.pallas_call(
        _prep_lat_kernel,
        out_shape=(jax.ShapeDtypeStruct((nb, seq, D_MODEL), BF16),
                   jax.ShapeDtypeStruct((nb, past + seq, kvw), BF16),
                   jax.ShapeDtypeStruct((nb, past + seq, kvw), BF16)),
        grid=(nb, nt + 1),
        in_specs=[pl.BlockSpec((tr, D_MODEL), lambda b, t: (b * nt + prev(t), COL_Q // D_MODEL)),
                  pl.BlockSpec((tr, kvw), lambda b, t: (b * nt + prev(t), COL_K // kvw)),
                  pl.BlockSpec((tr, kvw), lambda b, t: (b * nt + prev(t), COL_V // kvw)),
                  pl.BlockSpec((None, past, kvw), lambda b, t: (b, 0, 0)),
                  pl.BlockSpec((None, past, kvw), lambda b, t: (b, 0, 0)),
                  pl.BlockSpec((tr, HEAD_DIM), lambda b, t: (prev(t), 0)),
                  pl.BlockSpec((tr, HEAD_DIM), lambda b, t: (prev(t), 0)),
                  pl.BlockSpec((tr, HEAD_DIM), lambda b, t: (prev(t), 0)),
                  pl.BlockSpec((1, HEAD_DIM), lambda b, t: (0, 0)),
                  pl.BlockSpec((1, HEAD_DIM), lambda b, t: (0, 0))],
        out_specs=(pl.BlockSpec((None, tr, D_MODEL), lambda b, t: (b, prev(t), 0)),
                   pl.BlockSpec((None, tr, kvw), lambda b, t: (b, t, 0)),
                   pl.BlockSpec((None, tr, kvw), lambda b, t: (b, t, 0))),
        compiler_params=_params(("parallel", "arbitrary"), 32),
        name="prep_lat",
    )(proj, proj, proj, cache_k, cache_v, cos, sa, sb, qg, kg)


def _attn_lat_kernel(q_ref, k_ref, v_ref, o_ref):
    kb = k_ref[...]
    vb = v_ref[...]
    for g in range(Q_PER_KV):
        hs = slice(g * HEAD_DIM, (g + 1) * HEAD_DIM)
        s = _qk(q_ref[:, hs], kb)
        o_ref[:, hs] = _softmax_pv(s, vb).astype(BF16)


def _attn_lat(qr, kall, vall, tq):
    nb, seq, _ = qr.shape
    nk = kall.shape[1]
    gw = Q_PER_KV * HEAD_DIM
    return pl.pallas_call(
        _attn_lat_kernel,
        out_shape=jax.ShapeDtypeStruct((nb, seq, D_MODEL), BF16),
        grid=(nb, N_KV_HEADS, seq // tq),
        in_specs=[pl.BlockSpec((None, tq, gw), lambda b, h, i: (b, i, h)),
                  pl.BlockSpec((None, nk, HEAD_DIM), lambda b, h, i: (b, 0, h)),
                  pl.BlockSpec((None, nk, HEAD_DIM), lambda b, h, i: (b, 0, h))],
        out_specs=pl.BlockSpec((None, tq, gw), lambda b, h, i: (b, i, h)),
        compiler_params=_params(("parallel", "parallel", "arbitrary"), 48),
        name="attn_lat",
    )(qr, kall, vall)


def _ssd_kernel(*refs, reverse, has_h0, nc):
    it = iter(refs)
    if reverse:
        xsc_ref, bcc_ref, dt_ref, alog_ref, dtb_ref, e_ref = (next(it) for _ in range(6))
    else:
        xs_ref, bc_ref, dt_ref, xsp_ref, xsn_ref, bcp_ref, bcn_ref = (next(it) for _ in range(7))
        cwx_ref, cbx_ref, cwb_ref, cbb_ref, alog_ref, dtb_ref, e_ref, shift_ref = (next(it) for _ in range(8))
    h0_ref = next(it) if has_h0 else None
    if reverse:
        z_ref, yf_ref, d_ref, g_ref = (next(it) for _ in range(4))
        y_ref, hout_ref, ht_scr = refs[-3:]
    else:
        y_ref, xsc_ref, bcc_ref, hout_ref, ht_scr = refs[-5:]

    q = SSM_CHUNK
    npair = SSM_HEADS // 2
    c = pl.program_id(1)
    cc = nc - 1 - c if reverse else c

    @pl.when(c == 0)
    def _():
        for p in range(npair):
            if has_h0:
                ht_scr[p] = h0_ref[p].T
            else:
                ht_scr[p] = jnp.zeros((q, q), F32)

    if reverse:
        xs = xsc_ref[...].astype(F32)
        bc = bcc_ref[...].astype(F32)
    else:
        keep_prev = jnp.where(cc == 0, 0.0, 1.0).astype(BF16)
        keep_next = jnp.where(cc == nc - 1, 0.0, 1.0).astype(BF16)

        def conv_silu(x_ref_, p_ref_, n_ref_, w_ref_, b_ref_):
            x = x_ref_[...]
            ext = jnp.concatenate([x, p_ref_[...] * keep_prev, n_ref_[...] * keep_next], axis=0)
            nb = jnp.dot(shift_ref[...], ext, preferred_element_type=F32)
            w = w_ref_[...]
            return _silu(w[0:1, :] * nb[:q, :] + w[1:2, :] * x.astype(F32) + w[2:3, :] * nb[q:, :] + b_ref_[...])

        xs = conv_silu(xs_ref, xsp_ref, xsn_ref, cwx_ref, cbx_ref)
        bc = conv_silu(bc_ref, bcp_ref, bcn_ref, cwb_ref, cbb_ref)
        xsc_ref[...] = xs.astype(BF16)
        bcc_ref[...] = bc.astype(BF16)

    off = SSM_HEADS if reverse else 0
    dt = jax.nn.softplus(dt_ref[...].astype(F32) + dtb_ref[...])
    da = dt * (-jnp.exp(alog_ref[...]))
    ii = lax.broadcasted_iota(jnp.int32, (q, q), 0)
    jj = lax.broadcasted_iota(jnp.int32, (q, q), 1)
    tri = (jj >= ii) if reverse else (ii >= jj)
    tri_b = jnp.where(tri, 1.0, 0.0).astype(BF16)
    tri_tb = jnp.where((ii >= jj) if reverse else (jj >= ii), 1.0, 0.0).astype(BF16)
    acs3 = jnp.dot(tri_b, jnp.concatenate(_split3(da), axis=1), preferred_element_type=F32)
    acs = acs3[:, :LANES] + acs3[:, LANES:2 * LANES] + acs3[:, 2 * LANES:]
    acs_t = jnp.dot(jnp.concatenate(_split3(da.T), axis=1), jnp.concatenate([tri_tb] * 3, axis=0),
                    preferred_element_type=F32)
    spread = jnp.dot(jnp.concatenate(_split3(jnp.concatenate([dt, acs], axis=0)), axis=1), e_ref[...],
                     preferred_element_type=F32)
    acs_x = spread[q:, :]
    tot_x = acs_x[0:1, :] if reverse else acs_x[q - 1:q, :]
    xdt = xs * spread[:q, :]
    xw = xdt * jnp.exp(tot_x - acs_x)
    eacs_x = jnp.exp(acs_x)
    etot_x = jnp.exp(tot_x)
    lo = lax.broadcasted_iota(jnp.int32, (q, LANES), 1) < SSM_HEADDIM

    parts = []
    for g in range(2):
        bg = bc[:, g * q:(g + 1) * q]
        cg = bc[:, (2 + g) * q:(3 + g) * q].astype(BF16)
        cb = _qk(cg, bg.astype(BF16))
        bgt = bg.T.astype(BF16)
        for pp in range(npair // 2):
            p = g * (npair // 2) + pp
            ps = slice(p * LANES, (p + 1) * LANES)
            ms = []
            for hh in range(2):
                ln = off + 2 * p + hh
                dec = jnp.where(tri, jnp.exp(acs[:, ln:ln + 1] - acs_t[ln:ln + 1, :]), 0.0)
                ms.append((cb * dec).astype(BF16))
            m = jnp.concatenate(ms, axis=1)
            xp = xdt[:, ps]
            bd = jnp.concatenate([jnp.where(lo, xp, 0.0), jnp.where(lo, 0.0, xp)], axis=0).astype(BF16)
            ht = ht_scr[p]
            yp = jnp.dot(m, bd, preferred_element_type=F32)
            yp = yp + jnp.dot(cg, ht.astype(BF16), preferred_element_type=F32) * eacs_x[:, ps]
            ht_scr[p] = etot_x[:, ps] * ht + jnp.dot(bgt, xw[:, ps].astype(BF16), preferred_element_type=F32)
            parts.append(yp)
    y = jnp.concatenate(parts, axis=1)

    if reverse:
        y = y + yf_ref[...] + d_ref[...] * xs
        y = y * _silu(z_ref[...].astype(F32))
        y_ref[...] = (_rms(y) * g_ref[...]).astype(y_ref.dtype)
    else:
        y_ref[...] = y

    @pl.when(c == nc - 1)
    def _():
        for p in range(npair):
            hout_ref[p] = ht_scr[p].T


def _ssd_pass(proj, lp, d, nseq, seq, h0, fwd=None, layer=0, depth=1, earlier=None):
    reverse = fwd is not None
    has_h0 = h0 is not None
    q = SSM_CHUNK
    nc = seq // q
    rows = nseq * seq
    npair = SSM_HEADS // 2
    hb = q // HALO
    bcw = 4 * SSM_STATE
    chunk = (lambda c: nc - 1 - c) if reverse else (lambda c: c)
    blk = lambda s, c: s * nc + chunk(c)
    prev8 = lambda s, c: jnp.maximum(blk(s, c) * hb - 1, 0)
    next8 = lambda s, c: jnp.minimum((blk(s, c) + 1) * hb, rows // HALO - 1)
    const = lambda s, c: (0, 0)
    tile = lambda width, col=0: pl.BlockSpec((q, width), lambda s, c: (blk(s, c), col // width))
    halo = lambda width, col, where: pl.BlockSpec((HALO, width), lambda s, c: (where(s, c), col // width))
    scan_consts = [pl.BlockSpec((1, LANES), const), pl.BlockSpec((1, LANES), const),
                   pl.BlockSpec((3 * LANES, D_MODEL), const)]
    scan_args = [lp["alog"][d], lp["dtb"][d], lp["expand"][d]]
    if reverse:
        yf, xsc, bcc = fwd
        in_specs = [tile(D_MODEL), tile(bcw), tile(LANES, COL_DT)] + scan_consts
        args = [xsc, bcc, proj] + scan_args
    else:
        in_specs = [tile(D_MODEL, COL_XS), tile(bcw, COL_BC), tile(LANES, COL_DT),
                    halo(D_MODEL, COL_XS, prev8), halo(D_MODEL, COL_XS, next8),
                    halo(bcw, COL_BC, prev8), halo(bcw, COL_BC, next8),
                    pl.BlockSpec((3, D_MODEL), const), pl.BlockSpec((1, D_MODEL), const),
                    pl.BlockSpec((3, bcw), const), pl.BlockSpec((1, bcw), const)] + scan_consts + [
                    pl.BlockSpec((2 * q, q + 2 * HALO), const)]
        args = [proj] * 7 + [lp["cw_x"], lp["cb_x"], lp["cw_bc"], lp["cb_bc"]] + scan_args + [lp["shift"]]
    if has_h0:
        in_specs.append(pl.BlockSpec((None, npair, q, q), lambda s, c: (s, 0, 0, 0)))
        args.append(h0)
    if reverse:
        in_specs += [tile(D_MODEL, COL_Z), tile(D_MODEL), pl.BlockSpec((1, D_MODEL), const),
                     pl.BlockSpec((1, D_MODEL), const)]
        args += [proj, yf, lp["d_row"], lp["ssm_norm_g"]]
    state = jax.ShapeDtypeStruct((nseq, depth, npair, q, q), F32)
    state_spec = pl.BlockSpec((None, None, npair, q, q), lambda s, c: (s, layer, 0, 0, 0))
    if reverse:
        out_shape = (jax.ShapeDtypeStruct((rows, D_MODEL), BF16), state)
        out_specs = (tile(D_MODEL), state_spec)
    else:
        out_shape = (jax.ShapeDtypeStruct((rows, D_MODEL), F32), jax.ShapeDtypeStruct((rows, D_MODEL), BF16),
                     jax.ShapeDtypeStruct((rows, bcw), BF16), state)
        out_specs = (tile(D_MODEL), tile(D_MODEL), tile(bcw), state_spec)
    aliases = {}
    if earlier is not None:
        aliases = {len(args): len(out_shape) - 1}
        in_specs.append(pl.BlockSpec(memory_space=pl.ANY))
        args.append(earlier)
    return pl.pallas_call(
        functools.partial(_ssd_kernel, reverse=reverse, has_h0=has_h0, nc=nc),
        out_shape=out_shape,
        grid=(nseq, nc),
        in_specs=in_specs,
        out_specs=out_specs,
        input_output_aliases=aliases,
        scratch_shapes=[pltpu.VMEM((npair, q, q), F32)],
        compiler_params=_params(("parallel", "arbitrary"), 32),
        name="ssd_bwd" if reverse else "ssd_fwd",
    )(*args)


ROUTE_I1, ROUTE_I2, ROUTE_P1, ROUTE_P2, ROUTE_R1, ROUTE_R2 = 0, 1, 2, 3, 4, 5


def _pack_bf16_pairs(v):
    n = v.shape[1] // 2
    bits = lax.bitcast_convert_type(v.astype(BF16).astype(F32), jnp.uint32)
    return lax.bitcast_convert_type((bits[:, :n] >> 16) | bits[:, n:], jnp.int32)


def _unpack_bf16_pairs(p):
    u = lax.bitcast_convert_type(p, jnp.uint32)
    lo = lax.bitcast_convert_type(u << 16, F32)
    hi = lax.bitcast_convert_type(u & jnp.uint32(0xFFFF0000), F32)
    return jnp.concatenate([lo, hi], axis=1)


def _top2_route(logits):
    tm = logits.shape[0]
    lane = lax.broadcasted_iota(jnp.int32, logits.shape, 1).astype(F32)
    lg = jnp.where(lane < N_EXPERTS, logits, -jnp.inf)
    m1 = jnp.max(lg, axis=-1, keepdims=True)
    i1 = jnp.min(jnp.where(lg == m1, lane, float(LANES)), axis=-1, keepdims=True)
    lg2 = jnp.where(lane == i1, -jnp.inf, lg)
    m2 = jnp.max(lg2, axis=-1, keepdims=True)
    i2 = jnp.min(jnp.where(lg2 == m2, lane, float(LANES)), axis=-1, keepdims=True)
    e = jnp.exp(m2 - m1)
    p1 = 1.0 / (1.0 + e)
    chosen = jnp.where(lane == i1, 1.0, jnp.where(lane == i2, 1.0, 0.0))
    earlier = (lax.broadcasted_iota(jnp.int32, (tm, tm), 0) > lax.broadcasted_iota(jnp.int32, (tm, tm), 1))
    ranks = _bdot(jnp.where(earlier, 1.0, 0.0), chosen)
    r1 = jnp.sum(jnp.where(lane == i1, ranks, 0.0), axis=-1, keepdims=True)
    r2 = jnp.sum(jnp.where(lane == i2, ranks, 0.0), axis=-1, keepdims=True)
    rec = jnp.where(lane == ROUTE_I1, i1, jnp.where(lane == ROUTE_I2, i2, 0.0))
    rec = jnp.where(lane == ROUTE_P1, p1, jnp.where(lane == ROUTE_P2, e * p1, rec))
    rec = jnp.where(lane == ROUTE_R1, r1, jnp.where(lane == ROUTE_R2, r2, rec))
    return rec, jnp.sum(chosen, axis=0, keepdims=True)


def _merge_kernel(*refs, seq, tm, moe):
    it = iter(refs)
    attn_ref, ssm_ref, scb_ref, scc_ref, sch_ref = (next(it) for _ in range(5))
    sccp_ref, sccn_ref, schp_ref, schn_ref = (next(it) for _ in range(4))
    g0_ref, g1_ref, g2_ref, x_ref, mod_ref = (next(it) for _ in range(5))
    wa_ref, ws_ref, wc_ref, wm_ref, scw_ref, ng_ref = (next(it) for _ in range(6))
    router_ref = next(it) if moe else None
    xo_ref, h2_ref = next(it), next(it)
    route_ref, cnt_ref = (next(it), next(it)) if moe else (None, None)

    i = pl.program_id(0)
    pos = (lax.broadcasted_iota(jnp.int32, (tm, 1), 0) + i * tm) & (seq - 1)
    u = scc_ref[...].astype(F32) * sch_ref[...].astype(F32)
    up = sccp_ref[HALO - 1:HALO, :].astype(F32) * schp_ref[HALO - 1:HALO, :].astype(F32)
    un = sccn_ref[0:1, :].astype(F32) * schn_ref[0:1, :].astype(F32)
    r = lax.broadcasted_iota(jnp.int32, (tm, 1), 0)
    um1 = jnp.where(pos == 0, 0.0, jnp.where(r == 0, up, pltpu.roll(u, 1, 0)))
    up1 = jnp.where(pos == seq - 1, 0.0, jnp.where(r == tm - 1, un, pltpu.roll(u, tm - 1, 0)))
    w = scw_ref[...]
    sc = scb_ref[...].astype(F32) * (w[0:1, :] * um1 + w[1:2, :] * u + w[2:3, :] * up1)

    gate = lambda g_ref: _sigmoid(g_ref[...].astype(F32))
    merged = gate(g0_ref) * jnp.dot(attn_ref[...], wa_ref[...], preferred_element_type=F32)
    merged = merged + gate(g1_ref) * jnp.dot(ssm_ref[...], ws_ref[...], preferred_element_type=F32)
    merged = merged + gate(g2_ref) * jnp.dot(sc.astype(BF16), wc_ref[...], preferred_element_type=F32)
    mod = mod_ref[0]
    xn = x_ref[...] + mod[2:3, :] * jnp.dot(merged.astype(BF16), wm_ref[...], preferred_element_type=F32)
    xo_ref[...] = xn
    h2 = _rms(xn) * ng_ref[...] * (1.0 + mod[4:5, :]) + mod[3:4, :]
    if moe:
        h2_ref[...] = _pack_bf16_pairs(h2)
        h_hi = h2.astype(BF16)
        h_lo = (h2 - h_hi.astype(F32)).astype(BF16)
        both = jnp.dot(h_hi, router_ref[...], preferred_element_type=F32)
        logits = both[:, :LANES] + both[:, LANES:] + jnp.dot(h_lo, router_ref[:, :LANES], preferred_element_type=F32)
        rec, counts = _top2_route(logits)
        route_ref[...] = rec
        cnt_ref[...] = jnp.broadcast_to(counts, cnt_ref.shape)
    else:
        h2_ref[...] = h2.astype(BF16)


def _merge(attn, ssm, proj, x, mod, lp, seq, tm):
    rows = x.shape[0]
    per_mod = rows // mod.shape[0]
    tm = min(tm, per_mod)
    moe = lp["router"] is not None
    hb = tm // HALO
    prev8 = lambda i: jnp.maximum(i * hb - 1, 0)
    next8 = lambda i: jnp.minimum((i + 1) * hb, rows // HALO - 1)
    tile = lambda col: pl.BlockSpec((tm, D_MODEL), lambda i: (i, col // D_MODEL))
    const = lambda shape: pl.BlockSpec(shape, lambda i: (0, 0))
    in_specs = [tile(0), tile(0), tile(COL_SCB), tile(COL_SCC), tile(COL_SCH),
                pl.BlockSpec((HALO, D_MODEL), lambda i: (prev8(i), COL_SCC // D_MODEL)),
                pl.BlockSpec((HALO, D_MODEL), lambda i: (next8(i), COL_SCC // D_MODEL)),
                pl.BlockSpec((HALO, D_MODEL), lambda i: (prev8(i), COL_SCH // D_MODEL)),
                pl.BlockSpec((HALO, D_MODEL), lambda i: (next8(i), COL_SCH // D_MODEL)),
                tile(COL_G), tile(COL_G + D_MODEL), tile(COL_G + 2 * D_MODEL), tile(0),
                pl.BlockSpec((1, SUBLANES, D_MODEL), lambda i: (i * tm // per_mod, 0, 0)),
                _resident((D_MODEL, D_MODEL)), _resident((D_MODEL, D_MODEL)), _resident((D_MODEL, D_MODEL)),
                _resident((D_MODEL, D_MODEL)), const((3, D_MODEL)), const((1, D_MODEL))]
    args = [attn, ssm, proj, proj, proj, proj, proj, proj, proj, proj, proj, proj, x, mod,
            lp["w_attn_out"], lp["w_ssm_out"], lp["w_sconv_out"], lp["w_merge"], lp["sconv_w"], lp["norm_ffn_g"]]
    out_shape = [jax.ShapeDtypeStruct((rows, D_MODEL), F32), jax.ShapeDtypeStruct((rows, D_MODEL), BF16)]
    out_specs = [tile(0), tile(0)]
    if moe:
        in_specs.append(_resident((D_MODEL, 2 * LANES)))
        args.append(lp["router"])
        out_shape[1] = jax.ShapeDtypeStruct((rows, D_MODEL // 2), jnp.int32)
        out_specs[1] = pl.BlockSpec((tm, D_MODEL // 2), lambda i: (i, 0))
        out_shape += [jax.ShapeDtypeStruct((rows, LANES), F32),
                      jax.ShapeDtypeStruct((rows // tm * SUBLANES, LANES), F32)]
        out_specs += [pl.BlockSpec((tm, LANES), lambda i: (i, 0)), pl.BlockSpec((SUBLANES, LANES), lambda i: (i, 0))]
    return pl.pallas_call(
        functools.partial(_merge_kernel, seq=seq, tm=tm, moe=moe),
        out_shape=tuple(out_shape),
        grid=(rows // tm,),
        in_specs=in_specs,
        out_specs=tuple(out_specs),
        compiler_params=_params(("parallel",), 56),
        name="merge",
    )(*args)


def _swiglu_acc(h, w1_ref, w3_ref, w2_ref, fs):
    a = _silu(jnp.dot(h, w1_ref[:, fs], preferred_element_type=F32)) * jnp.dot(h, w3_ref[:, fs],
                                                                                preferred_element_type=F32)
    return jnp.dot(a.astype(BF16), w2_ref[fs, :], preferred_element_type=F32)


def _swiglu(h, w1_ref, w3_ref, w2_ref):
    half = D_FF // 2
    return (_swiglu_acc(h, w1_ref, w3_ref, w2_ref, slice(0, half))
            + _swiglu_acc(h, w1_ref, w3_ref, w2_ref, slice(half, D_FF)))


def _ffn_kernel(h_ref, x_ref, mod_ref, w1_ref, w3_ref, w2_ref, o_ref):
    o_ref[...] = x_ref[...] + mod_ref[0, 5:6, :] * _swiglu(h_ref[...], w1_ref, w3_ref, w2_ref)


def _resident(shape):
    return pl.BlockSpec(shape, lambda *_: (0,) * len(shape), pipeline_mode=pl.Buffered(1))


def _ffn(h2, x, mod, w1, w3, w2, tm):
    rows = x.shape[0]
    per_mod = rows // mod.shape[0]
    tm = min(tm, per_mod)
    return pl.pallas_call(
        _ffn_kernel,
        out_shape=jax.ShapeDtypeStruct((rows, D_MODEL), F32),
        grid=(rows // tm,),
        in_specs=[pl.BlockSpec((tm, D_MODEL), lambda i: (i, 0)),
                  pl.BlockSpec((tm, D_MODEL), lambda i: (i, 0)),
                  pl.BlockSpec((1, SUBLANES, D_MODEL), lambda i: (i * tm // per_mod, 0, 0)),
                  _resident((D_MODEL, D_FF)), _resident((D_MODEL, D_FF)), _resident((D_FF, D_MODEL))],
        out_specs=pl.BlockSpec((tm, D_MODEL), lambda i: (i, 0)),
        compiler_params=_params(("parallel",), 56),
        name="ffn",
    )(h2, x, mod, w1, w3, w2)


def _cast_kernel(*refs):
    n = len(refs) // 2
    for src, dst in zip(refs[:n], refs[n:]):
        dst[...] = src[...].astype(dst.dtype)


def _cast_bf16(ws):
    e, r, c = ws[0].shape
    rb = max(d for d in range(16, r + 1, 16) if r % d == 0 and d * c * 4 <= 3 * MIB)
    spec = pl.BlockSpec((None, rb, c), lambda i, j: (i, j, 0))
    return pl.pallas_call(
        _cast_kernel,
        out_shape=tuple(jax.ShapeDtypeStruct(w.shape, BF16) for w in ws),
        grid=(e, r // rb),
        in_specs=[spec] * len(ws),
        out_specs=tuple([spec] * len(ws)),
        compiler_params=_params(("parallel", "parallel"), 48),
        name="cast_bf16",
    )(*ws)


SC_CORES = 2
SC_SUBCORES = 16
SC_CHUNK = 128


def _sc_gather(table, idx):
    nw = SC_CORES * SC_SUBCORES
    b, d = idx.shape[0], table.shape[1]
    per_w = b // nw
    assert per_w * nw == b and per_w % SC_CHUNK == 0
    mesh = plsc.VectorSubcoreMesh(core_axis_name="c", subcore_axis_name="s")

    def body(table_hbm, idx_hbm, out_hbm, idx_v, rows_v, sem):
        wid = lax.axis_index("s") * SC_CORES + lax.axis_index("c")

        @pl.loop(0, per_w // SC_CHUNK)
        def _(j):
            off = pl.multiple_of(wid * per_w + j * SC_CHUNK, SC_CHUNK)
            pltpu.sync_copy(idx_hbm.at[pl.ds(off, SC_CHUNK)], idx_v)
            pltpu.async_copy(table_hbm.at[idx_v], rows_v, sem).wait()
            pltpu.sync_copy(rows_v, out_hbm.at[pl.ds(off, SC_CHUNK)])

    return pl.kernel(
        body,
        out_type=jax.ShapeDtypeStruct((b, d), table.dtype),
        mesh=mesh,
        scratch_types=[pltpu.VMEM((SC_CHUNK,), jnp.int32), pltpu.VMEM((SC_CHUNK, d), table.dtype),
                       pltpu.SemaphoreType.DMA],
        name="sc_gather",
    )(table, idx)


def _route_tables(route, counts, tile):
    t = route.shape[0]
    n_mt = counts.shape[0] // SUBLANES
    cnt = counts.reshape(n_mt, SUBLANES, LANES)[:, 0, :N_EXPERTS].astype(jnp.int32)
    incl = jnp.cumsum(cnt, axis=0)
    padded = (incl[-1] + tile - 1) // tile * tile
    gend = jnp.cumsum(padded)
    base = jnp.repeat((gend - padded)[None, :] + incl - cnt, t // n_mt, axis=0)
    e = route[:, ROUTE_I1:ROUTE_I2 + 1].astype(jnp.int32)
    r = route[:, ROUTE_R1:ROUTE_R2 + 1].astype(jnp.int32)
    pick = e[:, :, None] == jnp.arange(N_EXPERTS)[None, None, :]
    slot = jnp.sum(jnp.where(pick, base[:, None, :], 0), axis=-1) + r
    n_slots = 2 * t + N_EXPERTS * tile
    tok_of_slot = (jnp.arange(n_slots, dtype=jnp.int32) % t).at[slot.reshape(-1)].set(
        jnp.repeat(jnp.arange(t, dtype=jnp.int32), 2))
    n_valid = gend[-1] // tile
    n_tiles = n_slots // tile
    tile_expert = jnp.sum(jnp.arange(n_tiles)[:, None] * tile >= gend[None, :], axis=1)
    last = jnp.take(tile_expert, jnp.maximum(n_valid - 1, 0))
    tile_expert = jnp.where(jnp.arange(n_tiles) < n_valid, tile_expert, last).astype(jnp.int32)
    return slot.T.reshape(-1).astype(jnp.int32), tok_of_slot, tile_expert, n_valid.reshape(1).astype(jnp.int32)


def _moe_ffn_kernel(te_ref, nv_ref, h_ref, w1_ref, w3_ref, w2_ref, o_ref):
    i = pl.program_id(0)

    @pl.when(i < nv_ref[0])
    def _():
        h = _unpack_bf16_pairs(h_ref[...]).astype(BF16)
        o_ref[...] = _pack_bf16_pairs(_swiglu(h, w1_ref, w3_ref, w2_ref))

    @pl.when(i >= nv_ref[0])
    def _():
        o_ref[...] = jnp.zeros(o_ref.shape, jnp.int32)


def _moe_ffn(hs, tile_expert, n_valid, w1, w3, w2, tile):
    n_slots = hs.shape[0]
    grid_spec = pltpu.PrefetchScalarGridSpec(
        num_scalar_prefetch=2,
        grid=(n_slots // tile,),
        in_specs=[pl.BlockSpec((tile, D_MODEL // 2), lambda i, te, nv: (i, 0)),
                  pl.BlockSpec((None, D_MODEL, D_FF), lambda i, te, nv: (te[i], 0, 0)),
                  pl.BlockSpec((None, D_MODEL, D_FF), lambda i, te, nv: (te[i], 0, 0)),
                  pl.BlockSpec((None, D_FF, D_MODEL), lambda i, te, nv: (te[i], 0, 0))],
        out_specs=pl.BlockSpec((tile, D_MODEL // 2), lambda i, te, nv: (i, 0)),
    )
    return pl.pallas_call(
        _moe_ffn_kernel,
        out_shape=jax.ShapeDtypeStruct((n_slots, D_MODEL // 2), jnp.int32),
        grid_spec=grid_spec,
        compiler_params=_params(("arbitrary",), 56),
        name="moe_ffn",
    )(tile_expert, n_valid, hs, w1, w3, w2)


def _combine_kernel(x_ref, y1_ref, y2_ref, route_ref, mod_ref, o_ref):
    p1 = route_ref[:, ROUTE_P1:ROUTE_P1 + 1]
    p2 = route_ref[:, ROUTE_P2:ROUTE_P2 + 1]
    f = p1 * _unpack_bf16_pairs(y1_ref[...]) + p2 * _unpack_bf16_pairs(y2_ref[...])
    o_ref[...] = x_ref[...] + mod_ref[0, 5:6, :] * f


def _combine(x, yg, route, mod, tm):
    rows = x.shape[0]
    per_mod = rows // mod.shape[0]
    tm = min(tm, per_mod)
    return pl.pallas_call(
        _combine_kernel,
        out_shape=jax.ShapeDtypeStruct((rows, D_MODEL), F32),
        grid=(rows // tm,),
        in_specs=[pl.BlockSpec((tm, D_MODEL), lambda i: (i, 0)),
                  pl.BlockSpec((tm, D_MODEL // 2), lambda i: (i, 0)),
                  pl.BlockSpec((tm, D_MODEL // 2), lambda i: (i + rows // tm, 0)),
                  pl.BlockSpec((tm, LANES), lambda i: (i, 0)),
                  pl.BlockSpec((1, SUBLANES, D_MODEL), lambda i: (i * tm // per_mod, 0, 0))],
        out_specs=pl.BlockSpec((tm, D_MODEL), lambda i: (i, 0)),
        compiler_params=_params(("parallel",), 32),
        name="moe_combine",
    )(x, yg, yg, route, mod)


def _moe(h2p, x, mod, route, counts, w1, w3, w2, tile):
    slot, tok_of_slot, tile_expert, n_valid = _route_tables(route, counts, tile)
    hs = _sc_gather(h2p, tok_of_slot)
    ys = _moe_ffn(hs, tile_expert, n_valid, w1, w3, w2, tile)
    return _combine(x, _sc_gather(ys, slot), route, mod, tm=512)


def _layer_params(l, w_in, q_norm_g, k_norm_g, w_attn_out, ssm_conv_w, ssm_conv_b, ssm_dt_bias, ssm_a_log,
                  ssm_d, ssm_norm_g, w_ssm_out, sconv_w, w_sconv_out, w_merge, norm_mix_g, norm_ffn_g,
                  ffn_w1, ffn_w3, ffn_w2, moe_router, moe_w1, moe_w3, moe_w2):
    w = w_in[l]
    seg = lambda a, n: w[:, a:a + n]
    w_perm = jnp.concatenate(
        [seg(0, 1024), seg(1536, 1024), seg(4128, 1024), seg(5152, 1024), seg(6176, 1024), seg(7200, 3072),
         seg(2560, 1024), seg(3584, 512), seg(1024, 256), seg(1280, 256), seg(4096, 32),
         jnp.zeros((D_MODEL, N_PROJ - COL_DT - 2 * SSM_HEADS), w.dtype)], axis=1).astype(BF16)
    row = lambda v: v.reshape(1, -1)
    lane_row = lambda v, off: jnp.zeros((1, LANES), F32).at[0, off:off + SSM_HEADS].set(v)
    head_of_col = np.arange(D_MODEL) // SSM_HEADDIM
    expand = [jnp.asarray(np.tile(np.arange(LANES)[:, None] == head_of_col[None, :] + off, (3, 1)), BF16)
              for off in (0, SSM_HEADS)]
    r = np.arange(2 * SSM_CHUNK)[:, None]
    cidx = np.arange(SSM_CHUNK + 2 * HALO)[None, :]
    up = (r < SSM_CHUNK) & (((cidx == r - 1) & (r >= 1)) | ((r == 0) & (cidx == SSM_CHUNK + HALO - 1)))
    dn = (r >= SSM_CHUNK) & (((cidx == r - SSM_CHUNK + 1) & (r < 2 * SSM_CHUNK - 1))
                             | ((r == 2 * SSM_CHUNK - 1) & (cidx == SSM_CHUNK + HALO)))
    shift = jnp.asarray(up | dn, BF16)
    lp = {
        "w_in": w_perm,
        "norm_mix_g": row(norm_mix_g[l]), "norm_ffn_g": row(norm_ffn_g[l]),
        "q_norm_g": row(q_norm_g[l]), "k_norm_g": row(k_norm_g[l]),
        "w_attn_out": w_attn_out[l].astype(BF16), "w_ssm_out": w_ssm_out[l].astype(BF16),
        "w_sconv_out": w_sconv_out[l].astype(BF16), "w_merge": w_merge[l].astype(BF16),
        "cw_x": ssm_conv_w[l][:, :D_MODEL], "cw_bc": ssm_conv_w[l][:, D_MODEL:],
        "cb_x": row(ssm_conv_b[l][:D_MODEL]), "cb_bc": row(ssm_conv_b[l][D_MODEL:]),
        "alog": [lane_row(ssm_a_log[l, d], d * SSM_HEADS) for d in range(2)],
        "dtb": [lane_row(ssm_dt_bias[l, d], d * SSM_HEADS) for d in range(2)],
        "expand": expand,
        "shift": shift,
        "d_row": row(jnp.repeat(ssm_d[l], SSM_HEADDIM)),
        "ssm_norm_g": row(ssm_norm_g[l]),
        "sconv_w": sconv_w[l],
    }
    if l % 2 == 0:
        lp["router"] = None
        lp["ffn"] = tuple(t[l // 2].astype(BF16) for t in (ffn_w1, ffn_w3, ffn_w2))
    else:
        router = jnp.pad(moe_router[l // 2], ((0, 0), (0, LANES - N_EXPERTS)))
        r_hi = router.astype(BF16)
        lp["router"] = jnp.concatenate([r_hi, (router - r_hi.astype(F32)).astype(BF16)], axis=1)
        lp["ffn"] = _cast_bf16((moe_w1[l // 2], moe_w3[l // 2])) + _cast_bf16((moe_w2[l // 2],))
    return lp


def _rope_tables(seq):
    t = np.arange(seq)
    n = HEAD_DIM // 4
    inv = ROPE_THETA ** (-np.arange(n, dtype=np.float64) / n)
    ang_r = (t // GRID_W)[:, None] * inv[None, :]
    ang_c = (t % GRID_W)[:, None] * inv[None, :]
    zero = np.zeros_like(ang_r)
    cos = np.concatenate([np.cos(ang_r)] * 2 + [np.cos(ang_c)] * 2, axis=1)
    sa = np.concatenate([-np.sin(ang_r), zero, -np.sin(ang_c), zero], axis=1)
    sb = np.concatenate([zero, np.sin(ang_r), zero, np.sin(ang_c)], axis=1)
    return tuple(jnp.asarray(a, F32) for a in (cos, sa, sb))


def _trunk(x, mod, lp, nseq, seq, lat, layer=0, depth=1, carry=None):
    rows = nseq * seq
    proj = _inproj(x, mod, lp["norm_mix_g"], lp["w_in"], tm=min(2048, rows), tn=1152)
    if lat is None:
        attn, k_new, v_new = _attn_ctx(proj, lp["q_norm_g"], lp["k_norm_g"], nseq, seq, layer, depth,
                                       () if carry is None else carry[:2])
        *fwd, h_f = _ssd_pass(proj, lp, 0, nseq, seq, None, None, layer, depth, None if carry is None else carry[2])
        ssm, h_b = _ssd_pass(proj, lp, 1, nseq, seq, None, fwd, layer, depth, None if carry is None else carry[3])
    else:
        cache_k, cache_v, h0_f, h0_b, tables = lat
        qr, kall, vall = _prep_lat(proj, cache_k, cache_v, *tables, lp["q_norm_g"], lp["k_norm_g"], nseq, seq)
        attn = _attn_lat(qr, kall, vall, tq=256).reshape(rows, D_MODEL)
        k_new = v_new = None
        *fwd, h_f = _ssd_pass(proj, lp, 0, nseq, seq, h0_f)
        ssm, h_b = _ssd_pass(proj, lp, 1, nseq, seq, h0_b, fwd)
    outs = _merge(attn, ssm, proj, x, mod, lp, seq, tm=512)
    w1, w3, w2 = lp["ffn"]
    if lp["router"] is None:
        x = _ffn(outs[1], outs[0], mod, w1, w3, w2, tm=512)
    else:
        x = _moe(outs[1], outs[0], mod, outs[2], outs[3], w1, w3, w2, tile=512)
    return x, (k_new, v_new, h_f, h_b)


def kernel(x_prompt, x_sample, cache_k, cache_v, state_ssm_fwd, state_ssm_bwd, c, c_ctx, w_mod, b_mod, norm_mix_g, norm_ffn_g, w_in, q_norm_g, k_norm_g, w_attn_out, ssm_conv_w, ssm_conv_b, ssm_dt_bias, ssm_a_log, ssm_d, ssm_norm_g, w_ssm_out, sconv_w, w_sconv_out, w_merge, ffn_w1, ffn_w3, ffn_w2, moe_router, moe_w1, moe_w3, moe_w2):
    batch, seq, _ = x_prompt.shape
    nb, dseq, _ = x_sample.shape
    depth = w_in.shape[0]
    past = cache_k.shape[2]
    kvw = N_KV_HEADS * HEAD_DIM
    npair = SSM_HEADS // 2
    assert nb + 1 <= SUBLANES

    cond = jnp.zeros((SUBLANES, D_MODEL), F32).at[0].set(c_ctx).at[1:1 + nb].set(c)
    tables = _rope_tables(dseq)
    y_ctx = x_prompt.reshape(batch * seq, D_MODEL)
    y_lat = x_sample.reshape(nb * dseq, D_MODEL)
    mods = _modulation(cond, w_mod, b_mod.reshape(depth, 1, -1))
    carry = None
    for l in range(depth):
        lp = _layer_params(l, w_in, q_norm_g, k_norm_g, w_attn_out, ssm_conv_w, ssm_conv_b, ssm_dt_bias,
                           ssm_a_log, ssm_d, ssm_norm_g, w_ssm_out, sconv_w, w_sconv_out, w_merge,
                           norm_mix_g, norm_ffn_g, ffn_w1, ffn_w3, ffn_w2, moe_router, moe_w1, moe_w3, moe_w2)
        m = jnp.pad(mods[l].reshape(SUBLANES, 6, D_MODEL), ((0, 0), (0, SUBLANES - 6), (0, 0)))
        y_ctx, carry = _trunk(y_ctx, m[0:1], lp, batch, seq, None, l, depth, carry)
        lat = (cache_k[:, l].reshape(nb, past, kvw), cache_v[:, l].reshape(nb, past, kvw),
               state_ssm_fwd[:, l].reshape(nb, npair, SSM_CHUNK, SSM_STATE),
               state_ssm_bwd[:, l].reshape(nb, npair, SSM_CHUNK, SSM_STATE), tables)
        y_lat = _trunk(y_lat, m[1:1 + nb], lp, nb, dseq, lat)[0]
    k_all, v_all, hf_all, hb_all = carry
    state_shape = (batch, depth, SSM_HEADS, SSM_HEADDIM, SSM_STATE)
    return (y_ctx.reshape(batch, seq, D_MODEL), y_lat.reshape(nb, dseq, D_MODEL),
            k_all, v_all,
            hf_all.reshape(state_shape), hb_all.reshape(state_shape))
```

```python
import functools

import jax
import jax.numpy as jnp
import numpy as np
from jax import lax
from jax.experimental import pallas as pl
from jax.experimental.pallas import tpu as pltpu
from jax.experimental.pallas import tpu_sc as plsc

F32 = jnp.float32
BF16 = jnp.bfloat16
HIGHEST = lax.Precision.HIGHEST

D_MODEL = 1024
HEAD_DIM = 128
N_HEADS = 8
N_KV_HEADS = 2
Q_PER_KV = N_HEADS // N_KV_HEADS
ROPE_THETA = 10000.0
GRID_W = 64
SSM_HEADS = 16
SSM_HEADDIM = 64
SSM_STATE = 128
SSM_CHUNK = 128
D_FF = 2816
N_EXPERTS = 8
EPS = 1e-6
Q_SCALE = 1.4426950408889634 * HEAD_DIM ** -0.5

COL_Q, COL_Z, COL_SCB, COL_SCC, COL_SCH, COL_G = 0, 1024, 2048, 3072, 4096, 5120
COL_XS, COL_BC, COL_K, COL_V, COL_DT = 8192, 9216, 9728, 9984, 10240
N_PROJ = 10368
LANES = 128
SUBLANES = 8
HALO = 16
MIB = 1024 * 1024


def _params(sem, vmem_mib):
    return pltpu.CompilerParams(dimension_semantics=sem, vmem_limit_bytes=vmem_mib * MIB)


def _sigmoid(t):
    return 1.0 / (1.0 + jnp.exp(-t))


def _silu(t):
    return t * _sigmoid(t)


def _rms(t):
    return t * lax.rsqrt(jnp.mean(t * t, axis=-1, keepdims=True) + EPS)


def _bdot(a, b):
    return jnp.dot(a.astype(BF16), b.astype(BF16), preferred_element_type=F32)


def _hdot(a, b):
    return jnp.dot(a, b, precision=HIGHEST, preferred_element_type=F32)


def _split3(t):
    a = t.astype(BF16)
    r = t - a.astype(F32)
    b = r.astype(BF16)
    return a, b, (r - b.astype(F32)).astype(BF16)


def _mod_kernel(c_ref, w_ref, b_ref, o_ref):
    o_ref[...] = _hdot(_silu(c_ref[...]), w_ref[...]) + b_ref[...]


def _modulation(cond8, w, b):
    depth, _, width = w.shape
    return pl.pallas_call(
        _mod_kernel,
        out_shape=jax.ShapeDtypeStruct((depth, SUBLANES, width), F32),
        grid=(depth, width // D_MODEL),
        in_specs=[pl.BlockSpec((SUBLANES, D_MODEL), lambda l, j: (0, 0)),
                  pl.BlockSpec((None, D_MODEL, D_MODEL), lambda l, j: (l, 0, j)),
                  pl.BlockSpec((None, 1, D_MODEL), lambda l, j: (l, 0, j))],
        out_specs=pl.BlockSpec((None, SUBLANES, D_MODEL), lambda l, j: (l, 0, j)),
        compiler_params=_params(("parallel", "parallel"), 32),
        name="modulation",
    )(cond8, w, b)


def _inproj_kernel(x_ref, mod_ref, g_ref, w_ref, o_ref, h_scr):
    @pl.when(pl.program_id(1) == 0)
    def _():
        h = _rms(x_ref[...]) * g_ref[...]
        h = h * (1.0 + mod_ref[0, 1:2, :]) + mod_ref[0, 0:1, :]
        h_scr[...] = h.astype(BF16)

    o_ref[...] = jnp.dot(h_scr[...], w_ref[...], preferred_element_type=F32).astype(o_ref.dtype)


def _inproj(x, mod, g, w, tm, tn):
    rows = x.shape[0]
    per_mod = rows // mod.shape[0]
    tm = min(tm, per_mod)
    return pl.pallas_call(
        _inproj_kernel,
        out_shape=jax.ShapeDtypeStruct((rows, N_PROJ), BF16),
        grid=(rows // tm, N_PROJ // tn),
        in_specs=[pl.BlockSpec((tm, D_MODEL), lambda i, j: (i, 0)),
                  pl.BlockSpec((1, SUBLANES, D_MODEL), lambda i, j: (i * tm // per_mod, 0, 0)),
                  pl.BlockSpec((1, D_MODEL), lambda i, j: (0, 0)),
                  pl.BlockSpec((D_MODEL, tn), lambda i, j: (0, j))],
        out_specs=pl.BlockSpec((tm, tn), lambda i, j: (i, j)),
        scratch_shapes=[pltpu.VMEM((tm, D_MODEL), BF16)],
        compiler_params=_params(("parallel", "arbitrary"), 48),
        name="inproj",
    )(x, mod, g, w)


def _softmax_pv(s, vb):
    m = jnp.max(s, axis=-1, keepdims=True)
    p = jnp.exp2(s - m)
    l = jnp.sum(p, axis=-1, keepdims=True)
    return jnp.dot(p.astype(BF16), vb, preferred_element_type=F32) / l


def _qk(qb, kb):
    return lax.dot_general(qb, kb, (((1,), (1,)), ((), ())), preferred_element_type=F32)


def _attn_ctx_kernel(q_ref, k_ref, v_ref, qg_ref, kg_ref, *rest):
    a_ref, ko_ref, vo_ref = rest[-3:]
    v = v_ref[...].astype(F32)
    for hk in range(N_KV_HEADS):
        ks = slice(hk * HEAD_DIM, (hk + 1) * HEAD_DIM)
        kn = _rms(k_ref[:, ks].astype(F32)) * kg_ref[...]
        ko_ref[:, hk, :] = kn
        vo_ref[:, hk, :] = v[:, ks]
        knb = kn.astype(BF16)
        vb = v[:, ks].astype(BF16)
        for g in range(Q_PER_KV):
            hs = slice((hk * Q_PER_KV + g) * HEAD_DIM, (hk * Q_PER_KV + g + 1) * HEAD_DIM)
            qn = _rms(q_ref[:, hs].astype(F32)) * (qg_ref[...] * Q_SCALE)
            s = _qk(qn.astype(BF16), knb)
            a_ref[:, hs] = _softmax_pv(s, vb).astype(BF16)


def _attn_ctx(proj, qg, kg, nseq, seq, layer, depth, earlier):
    kvw = N_KV_HEADS * HEAD_DIM
    cache = jax.ShapeDtypeStruct((nseq, depth, seq, N_KV_HEADS, HEAD_DIM), F32)
    cache_spec = pl.BlockSpec((None, None, seq, N_KV_HEADS, HEAD_DIM), lambda b: (b, layer, 0, 0, 0))
    n_in = 5
    return pl.pallas_call(
        _attn_ctx_kernel,
        out_shape=(jax.ShapeDtypeStruct((nseq * seq, D_MODEL), BF16), cache, cache),
        grid=(nseq,),
        in_specs=[pl.BlockSpec((seq, D_MODEL), lambda b: (b, COL_Q // D_MODEL)),
                  pl.BlockSpec((seq, kvw), lambda b: (b, COL_K // kvw)),
                  pl.BlockSpec((seq, kvw), lambda b: (b, COL_V // kvw)),
                  pl.BlockSpec((1, HEAD_DIM), lambda b: (0, 0)),
                  pl.BlockSpec((1, HEAD_DIM), lambda b: (0, 0))] + [pl.BlockSpec(memory_space=pl.ANY)] * len(earlier),
        out_specs=(pl.BlockSpec((seq, D_MODEL), lambda b: (b, 0)), cache_spec, cache_spec),
        input_output_aliases={n_in + i: 1 + i for i in range(len(earlier))},
        compiler_params=_params(("parallel",), 32),
        name="attn_ctx",
    )(proj, proj, proj, qg, kg, *earlier)


def _rope(t, cos, sa, sb):
    return t * cos + pltpu.roll(t, 96, 1) * sa + pltpu.roll(t, 32, 1) * sb


def _prep_lat_kernel(q_ref, k_ref, v_ref, ck_ref, cv_ref, cos_ref, sa_ref, sb_ref, qg_ref, kg_ref,
                     qo_ref, ko_ref, vo_ref):
    t = pl.program_id(1)

    @pl.when(t == 0)
    def _():
        ko_ref[...] = ck_ref[...].astype(BF16)
        vo_ref[...] = cv_ref[...].astype(BF16)

    @pl.when(t > 0)
    def _():
        cos, sa, sb = cos_ref[...], sa_ref[...], sb_ref[...]
        vo_ref[...] = v_ref[...]
        for h in range(N_KV_HEADS):
            hs = slice(h * HEAD_DIM, (h + 1) * HEAD_DIM)
            ko_ref[:, hs] = _rope(_rms(k_ref[:, hs].astype(F32)) * kg_ref[...], cos, sa, sb).astype(BF16)
        for h in range(N_HEADS):
            hs = slice(h * HEAD_DIM, (h + 1) * HEAD_DIM)
            qn = _rms(q_ref[:, hs].astype(F32)) * (qg_ref[...] * Q_SCALE)
            qo_ref[:, hs] = _rope(qn, cos, sa, sb).astype(BF16)


def _prep_lat(proj, cache_k, cache_v, cos, sa, sb, qg, kg, nb, seq):
    past = cache_k.shape[1]
    tr = past
    kvw = N_KV_HEADS * HEAD_DIM
    nt = seq // tr
    prev = lambda t: jnp.maximum(t - 1, 0)
    return pl.pallas_call(
        _prep_lat_kernel,
        out_shape=(jax.ShapeDtypeStruct((nb, seq, D_MODEL), BF16),
                   jax.ShapeDtypeStruct((nb, past + seq, kvw), BF16),
                   jax.ShapeDtypeStruct((nb, past + seq, kvw), BF16)),
        grid=(nb, nt + 1),
        in_specs=[pl.BlockSpec((tr, D_MODEL), lambda b, t: (b * nt + prev(t), COL_Q // D_MODEL)),
                  pl.BlockSpec((tr, kvw), lambda b, t: (b * nt + prev(t), COL_K // kvw)),
                  pl.BlockSpec((tr, kvw), lambda b, t: (b * nt + prev(t), COL_V // kvw)),
                  pl.BlockSpec((None, past, kvw), lambda b, t: (b, 0, 0)),
                  pl.BlockSpec((None, past, kvw), lambda b, t: (b, 0, 0)),
                  pl.BlockSpec((tr, HEAD_DIM), lambda b, t: (prev(t), 0)),
                  pl.BlockSpec((tr, HEAD_DIM), lambda b, t: (prev(t), 0)),
                  pl.BlockSpec((tr, HEAD_DIM), lambda b, t: (prev(t), 0)),
                  pl.BlockSpec((1, HEAD_DIM), lambda b, t: (0, 0)),
                  pl.BlockSpec((1, HEAD_DIM), lambda b, t: (0, 0))],
        out_specs=(pl.BlockSpec((None, tr, D_MODEL), lambda b, t: (b, prev(t), 0)),
                   pl.BlockSpec((None, tr, kvw), lambda b, t: (b, t, 0)),
                   pl.BlockSpec((None, tr, kvw), lambda b, t: (b, t, 0))),
        compiler_params=_params(("parallel", "arbitrary"), 32),
        name="prep_lat",
    )(proj, proj, proj, cache_k, cache_v, cos, sa, sb, qg, kg)


def _attn_lat_kernel(q_ref, k_ref, v_ref, o_ref):
    kb = k_ref[...]
    vb = v_ref[...]
    for g in range(Q_PER_KV):
        hs = slice(g * HEAD_DIM, (g + 1) * HEAD_DIM)
        s = _qk(q_ref[:, hs], kb)
        o_ref[:, hs] = _softmax_pv(s, vb).astype(BF16)


def _attn_lat(qr, kall, vall, tq):
    nb, seq, _ = qr.shape
    nk = kall.shape[1]
    gw = Q_PER_KV * HEAD_DIM
    return pl.pallas_call(
        _attn_lat_kernel,
        out_shape=jax.ShapeDtypeStruct((nb, seq, D_MODEL), BF16),
        grid=(nb, N_KV_HEADS, seq // tq),
        in_specs=[pl.BlockSpec((None, tq, gw), lambda b, h, i: (b, i, h)),
                  pl.BlockSpec((None, nk, HEAD_DIM), lambda b, h, i: (b, 0, h)),
                  pl.BlockSpec((None, nk, HEAD_DIM), lambda b, h, i: (b, 0, h))],
        out_specs=pl.BlockSpec((None, tq, gw), lambda b, h, i: (b, i, h)),
        compiler_params=_params(("parallel", "parallel", "arbitrary"), 48),
        name="attn_lat",
    )(qr, kall, vall)


def _ssd_kernel(*refs, reverse, has_h0, nc):
    it = iter(refs)
    if reverse:
        xsc_ref, bcc_ref, dt_ref, alog_ref, dtb_ref, e_ref = (next(it) for _ in range(6))
    else:
        xs_ref, bc_ref, dt_ref, xsp_ref, xsn_ref, bcp_ref, bcn_ref = (next(it) for _ in range(7))
        cwx_ref, cbx_ref, cwb_ref, cbb_ref, alog_ref, dtb_ref, e_ref, shift_ref = (next(it) for _ in range(8))
    h0_ref = next(it) if has_h0 else None
    if reverse:
        z_ref, yf_ref, d_ref, g_ref = (next(it) for _ in range(4))
        y_ref, hout_ref, ht_scr = refs[-3:]
    else:
        y_ref, xsc_ref, bcc_ref, hout_ref, ht_scr = refs[-5:]

    q = SSM_CHUNK
    npair = SSM_HEADS // 2
    c = pl.program_id(1)
    cc = nc - 1 - c if reverse else c

    @pl.when(c == 0)
    def _():
        for p in range(npair):
            if has_h0:
                ht_scr[p] = h0_ref[p].T
            else:
                ht_scr[p] = jnp.zeros((q, q), F32)

    if reverse:
        xs = xsc_ref[...].astype(F32)
        bc = bcc_ref[...].astype(F32)
    else:
        keep_prev = jnp.where(cc == 0, 0.0, 1.0).astype(BF16)
        keep_next = jnp.where(cc == nc - 1, 0.0, 1.0).astype(BF16)

        def conv_silu(x_ref_, p_ref_, n_ref_, w_ref_, b_ref_):
            x = x_ref_[...]
            ext = jnp.concatenate([x, p_ref_[...] * keep_prev, n_ref_[...] * keep_next], axis=0)
            nb = jnp.dot(shift_ref[...], ext, preferred_element_type=F32)
            w = w_ref_[...]
            return _silu(w[0:1, :] * nb[:q, :] + w[1:2, :] * x.astype(F32) + w[2:3, :] * nb[q:, :] + b_ref_[...])

        xs = conv_silu(xs_ref, xsp_ref, xsn_ref, cwx_ref, cbx_ref)
        bc = conv_silu(bc_ref, bcp_ref, bcn_ref, cwb_ref, cbb_ref)
        xsc_ref[...] = xs.astype(BF16)
        bcc_ref[...] = bc.astype(BF16)

    off = SSM_HEADS if reverse else 0
    dt = jax.nn.softplus(dt_ref[...].astype(F32) + dtb_ref[...])
    da = dt * (-jnp.exp(alog_ref[...]))
    ii = lax.broadcasted_iota(jnp.int32, (q, q), 0)
    jj = lax.broadcasted_iota(jnp.int32, (q, q), 1)
    tri = (jj >= ii) if reverse else (ii >= jj)
    tri_b = jnp.where(tri, 1.0, 0.0).astype(BF16)
    tri_tb = jnp.where((ii >= jj) if reverse else (jj >= ii), 1.0, 0.0).astype(BF16)
    acs3 = jnp.dot(tri_b, jnp.concatenate(_split3(da), axis=1), preferred_element_type=F32)
    acs = acs3[:, :LANES] + acs3[:, LANES:2 * LANES] + acs3[:, 2 * LANES:]
    acs_t = jnp.dot(jnp.concatenate(_split3(da.T), axis=1), jnp.concatenate([tri_tb] * 3, axis=0),
                    preferred_element_type=F32)
    spread = jnp.dot(jnp.concatenate(_split3(jnp.concatenate([dt, acs], axis=0)), axis=1), e_ref[...],
                     preferred_element_type=F32)
    acs_x = spread[q:, :]
    tot_x = acs_x[0:1, :] if reverse else acs_x[q - 1:q, :]
    xdt = xs * spread[:q, :]
    xw = xdt * jnp.exp(tot_x - acs_x)
    eacs_x = jnp.exp(acs_x)
    etot_x = jnp.exp(tot_x)
    lo = lax.broadcasted_iota(jnp.int32, (q, LANES), 1) < SSM_HEADDIM

    parts = []
    for g in range(2):
        bg = bc[:, g * q:(g + 1) * q]
        cg = bc[:, (2 + g) * q:(3 + g) * q].astype(BF16)
        cb = _qk(cg, bg.astype(BF16))
        bgt = bg.T.astype(BF16)
        for pp in range(npair // 2):
            p = g * (npair // 2) + pp
            ps = slice(p * LANES, (p + 1) * LANES)
            ms = []
            for hh in range(2):
                ln = off + 2 * p + hh
                dec = jnp.where(tri, jnp.exp(acs[:, ln:ln + 1] - acs_t[ln:ln + 1, :]), 0.0)
                ms.append((cb * dec).astype(BF16))
            m = jnp.concatenate(ms, axis=1)
            xp = xdt[:, ps]
            bd = jnp.concatenate([jnp.where(lo, xp, 0.0), jnp.where(lo, 0.0, xp)], axis=0).astype(BF16)
            ht = ht_scr[p]
            yp = jnp.dot(m, bd, preferred_element_type=F32)
            yp = yp + jnp.dot(cg, ht.astype(BF16), preferred_element_type=F32) * eacs_x[:, ps]
            ht_scr[p] = etot_x[:, ps] * ht + jnp.dot(bgt, xw[:, ps].astype(BF16), preferred_element_type=F32)
            parts.append(yp)
    y = jnp.concatenate(parts, axis=1)

    if reverse:
        y = y + yf_ref[...] + d_ref[...] * xs
        y = y * _silu(z_ref[...].astype(F32))
        y_ref[...] = (_rms(y) * g_ref[...]).astype(y_ref.dtype)
    else:
        y_ref[...] = y

    @pl.when(c == nc - 1)
    def _():
        for p in range(npair):
            hout_ref[p] = ht_scr[p].T


def _ssd_pass(proj, lp, d, nseq, seq, h0, fwd=None, layer=0, depth=1, earlier=None):
    reverse = fwd is not None
    has_h0 = h0 is not None
    q = SSM_CHUNK
    nc = seq // q
    rows = nseq * seq
    npair = SSM_HEADS // 2
    hb = q // HALO
    bcw = 4 * SSM_STATE
    chunk = (lambda c: nc - 1 - c) if reverse else (lambda c: c)
    blk = lambda s, c: s * nc + chunk(c)
    prev8 = lambda s, c: jnp.maximum(blk(s, c) * hb - 1, 0)
    next8 = lambda s, c: jnp.minimum((blk(s, c) + 1) * hb, rows // HALO - 1)
    const = lambda s, c: (0, 0)
    tile = lambda width, col=0: pl.BlockSpec((q, width), lambda s, c: (blk(s, c), col // width))
    halo = lambda width, col, where: pl.BlockSpec((HALO, width), lambda s, c: (where(s, c), col // width))
    scan_consts = [pl.BlockSpec((1, LANES), const), pl.BlockSpec((1, LANES), const),
                   pl.BlockSpec((3 * LANES, D_MODEL), const)]
    scan_args = [lp["alog"][d], lp["dtb"][d], lp["expand"][d]]
    if reverse:
        yf, xsc, bcc = fwd
        in_specs = [tile(D_MODEL), tile(bcw), tile(LANES, COL_DT)] + scan_consts
        args = [xsc, bcc, proj] + scan_args
    else:
        in_specs = [tile(D_MODEL, COL_XS), tile(bcw, COL_BC), tile(LANES, COL_DT),
                    halo(D_MODEL, COL_XS, prev8), halo(D_MODEL, COL_XS, next8),
                    halo(bcw, COL_BC, prev8), halo(bcw, COL_BC, next8),
                    pl.BlockSpec((3, D_MODEL), const), pl.BlockSpec((1, D_MODEL), const),
                    pl.BlockSpec((3, bcw), const), pl.BlockSpec((1, bcw), const)] + scan_consts + [
                    pl.BlockSpec((2 * q, q + 2 * HALO), const)]
        args = [proj] * 7 + [lp["cw_x"], lp["cb_x"], lp["cw_bc"], lp["cb_bc"]] + scan_args + [lp["shift"]]
    if has_h0:
        in_specs.append(pl.BlockSpec((None, npair, q, q), lambda s, c: (s, 0, 0, 0)))
        args.append(h0)
    if reverse:
        in_specs += [tile(D_MODEL, COL_Z), tile(D_MODEL), pl.BlockSpec((1, D_MODEL), const),
                     pl.BlockSpec((1, D_MODEL), const)]
        args += [proj, yf, lp["d_row"], lp["ssm_norm_g"]]
    state = jax.ShapeDtypeStruct((nseq, depth, npair, q, q), F32)
    state_spec = pl.BlockSpec((None, None, npair, q, q), lambda s, c: (s, layer, 0, 0, 0))
    if reverse:
        out_shape = (jax.ShapeDtypeStruct((rows, D_MODEL), BF16), state)
        out_specs = (tile(D_MODEL), state_spec)
    else:
        out_shape = (jax.ShapeDtypeStruct((rows, D_MODEL), F32), jax.ShapeDtypeStruct((rows, D_MODEL), BF16),
                     jax.ShapeDtypeStruct((rows, bcw), BF16), state)
        out_specs = (tile(D_MODEL), tile(D_MODEL), tile(bcw), state_spec)
    aliases = {}
    if earlier is not None:
        aliases = {len(args): len(out_shape) - 1}
        in_specs.append(pl.BlockSpec(memory_space=pl.ANY))
        args.append(earlier)
    return pl.pallas_call(
        functools.partial(_ssd_kernel, reverse=reverse, has_h0=has_h0, nc=nc),
        out_shape=out_shape,
        grid=(nseq, nc),
        in_specs=in_specs,
        out_specs=out_specs,
        input_output_aliases=aliases,
        scratch_shapes=[pltpu.VMEM((npair, q, q), F32)],
        compiler_params=_params(("parallel", "arbitrary"), 32),
        name="ssd_bwd" if reverse else "ssd_fwd",
    )(*args)


ROUTE_I1, ROUTE_I2, ROUTE_P1, ROUTE_P2, ROUTE_R1, ROUTE_R2 = 0, 1, 2, 3, 4, 5


def _pack_bf16_pairs(v):
    n = v.shape[1] // 2
    bits = lax.bitcast_convert_type(v.astype(BF16).astype(F32), jnp.uint32)
    return lax.bitcast_convert_type((bits[:, :n] >> 16) | bits[:, n:], jnp.int32)


def _unpack_bf16_pairs(p):
    u = lax.bitcast_convert_type(p, jnp.uint32)
    lo = lax.bitcast_convert_type(u << 16, F32)
    hi = lax.bitcast_convert_type(u & jnp.uint32(0xFFFF0000), F32)
    return jnp.concatenate([lo, hi], axis=1)


def _top2_route(logits):
    tm = logits.shape[0]
    lane = lax.broadcasted_iota(jnp.int32, logits.shape, 1).astype(F32)
    lg = jnp.where(lane < N_EXPERTS, logits, -jnp.inf)
    m1 = jnp.max(lg, axis=-1, keepdims=True)
    i1 = jnp.min(jnp.where(lg == m1, lane, float(LANES)), axis=-1, keepdims=True)
    lg2 = jnp.where(lane == i1, -jnp.inf, lg)
    m2 = jnp.max(lg2, axis=-1, keepdims=True)
    i2 = jnp.min(jnp.where(lg2 == m2, lane, float(LANES)), axis=-1, keepdims=True)
    e = jnp.exp(m2 - m1)
    p1 = 1.0 / (1.0 + e)
    chosen = jnp.where(lane == i1, 1.0, jnp.where(lane == i2, 1.0, 0.0))
    earlier = (lax.broadcasted_iota(jnp.int32, (tm, tm), 0) > lax.broadcasted_iota(jnp.int32, (tm, tm), 1))
    ranks = _bdot(jnp.where(earlier, 1.0, 0.0), chosen)
    r1 = jnp.sum(jnp.where(lane == i1, ranks, 0.0), axis=-1, keepdims=True)
    r2 = jnp.sum(jnp.where(lane == i2, ranks, 0.0), axis=-1, keepdims=True)
    rec = jnp.where(lane == ROUTE_I1, i1, jnp.where(lane == ROUTE_I2, i2, 0.0))
    rec = jnp.where(lane == ROUTE_P1, p1, jnp.where(lane == ROUTE_P2, e * p1, rec))
    rec = jnp.where(lane == ROUTE_R1, r1, jnp.where(lane == ROUTE_R2, r2, rec))
    return rec, jnp.sum(chosen, axis=0, keepdims=True)


def _merge_kernel(*refs, seq, tm, moe):
    it = iter(refs)
    attn_ref, ssm_ref, scb_ref, scc_ref, sch_ref = (next(it) for _ in range(5))
    sccp_ref, sccn_ref, schp_ref, schn_ref = (next(it) for _ in range(4))
    g0_ref, g1_ref, g2_ref, x_ref, mod_ref = (next(it) for _ in range(5))
    wa_ref, ws_ref, wc_ref, wm_ref, scw_ref, ng_ref = (next(it) for _ in range(6))
    router_ref = next(it) if moe else None
    xo_ref, h2_ref = next(it), next(it)
    route_ref, cnt_ref = (next(it), next(it)) if moe else (None, None)

    i = pl.program_id(0)
    pos = (lax.broadcasted_iota(jnp.int32, (tm, 1), 0) + i * tm) & (seq - 1)
    u = scc_ref[...].astype(F32) * sch_ref[...].astype(F32)
    up = sccp_ref[HALO - 1:HALO, :].astype(F32) * schp_ref[HALO - 1:HALO, :].astype(F32)
    un = sccn_ref[0:1, :].astype(F32) * schn_ref[0:1, :].astype(F32)
    r = lax.broadcasted_iota(jnp.int32, (tm, 1), 0)
    um1 = jnp.where(pos == 0, 0.0, jnp.where(r == 0, up, pltpu.roll(u, 1, 0)))
    up1 = jnp.where(pos == seq - 1, 0.0, jnp.where(r == tm - 1, un, pltpu.roll(u, tm - 1, 0)))
    w = scw_ref[...]
    sc = scb_ref[...].astype(F32) * (w[0:1, :] * um1 + w[1:2, :] * u + w[2:3, :] * up1)

    gate = lambda g_ref: _sigmoid(g_ref[...].astype(F32))
    merged = gate(g0_ref) * jnp.dot(attn_ref[...], wa_ref[...], preferred_element_type=F32)
    merged = merged + gate(g1_ref) * jnp.dot(ssm_ref[...], ws_ref[...], preferred_element_type=F32)
    merged = merged + gate(g2_ref) * jnp.dot(sc.astype(BF16), wc_ref[...], preferred_element_type=F32)
    mod = mod_ref[0]
    xn = x_ref[...] + mod[2:3, :] * jnp.dot(merged.astype(BF16), wm_ref[...], preferred_element_type=F32)
    xo_ref[...] = xn
    h2 = _rms(xn) * ng_ref[...] * (1.0 + mod[4:5, :]) + mod[3:4, :]
    if moe:
        h2_ref[...] = _pack_bf16_pairs(h2)
        h_hi = h2.astype(BF16)
        h_lo = (h2 - h_hi.astype(F32)).astype(BF16)
        both = jnp.dot(h_hi, router_ref[...], preferred_element_type=F32)
        logits = both[:, :LANES] + both[:, LANES:] + jnp.dot(h_lo, router_ref[:, :LANES], preferred_element_type=F32)
        rec, counts = _top2_route(logits)
        route_ref[...] = rec
        cnt_ref[...] = jnp.broadcast_to(counts, cnt_ref.shape)
    else:
        h2_ref[...] = h2.astype(BF16)


def _merge(attn, ssm, proj, x, mod, lp, seq, tm):
    rows = x.shape[0]
    per_mod = rows // mod.shape[0]
    tm = min(tm, per_mod)
    moe = lp["router"] is not None
    hb = tm // HALO
    prev8 = lambda i: jnp.maximum(i * hb - 1, 0)
    next8 = lambda i: jnp.minimum((i + 1) * hb, rows // HALO - 1)
    tile = lambda col: pl.BlockSpec((tm, D_MODEL), lambda i: (i, col // D_MODEL))
    const = lambda shape: pl.BlockSpec(shape, lambda i: (0, 0))
    in_specs = [tile(0), tile(0), tile(COL_SCB), tile(COL_SCC), tile(COL_SCH),
                pl.BlockSpec((HALO, D_MODEL), lambda i: (prev8(i), COL_SCC // D_MODEL)),
                pl.BlockSpec((HALO, D_MODEL), lambda i: (next8(i), COL_SCC // D_MODEL)),
                pl.BlockSpec((HALO, D_MODEL), lambda i: (prev8(i), COL_SCH // D_MODEL)),
                pl.BlockSpec((HALO, D_MODEL), lambda i: (next8(i), COL_SCH // D_MODEL)),
                tile(COL_G), tile(COL_G + D_MODEL), tile(COL_G + 2 * D_MODEL), tile(0),
                pl.BlockSpec((1, SUBLANES, D_MODEL), lambda i: (i * tm // per_mod, 0, 0)),
                _resident((D_MODEL, D_MODEL)), _resident((D_MODEL, D_MODEL)), _resident((D_MODEL, D_MODEL)),
                _resident((D_MODEL, D_MODEL)), const((3, D_MODEL)), const((1, D_MODEL))]
    args = [attn, ssm, proj, proj, proj, proj, proj, proj, proj, proj, proj, proj, x, mod,
            lp["w_attn_out"], lp["w_ssm_out"], lp["w_sconv_out"], lp["w_merge"], lp["sconv_w"], lp["norm_ffn_g"]]
    out_shape = [jax.ShapeDtypeStruct((rows, D_MODEL), F32), jax.ShapeDtypeStruct((rows, D_MODEL), BF16)]
    out_specs = [tile(0), tile(0)]
    if moe:
        in_specs.append(_resident((D_MODEL, 2 * LANES)))
        args.append(lp["router"])
        out_shape[1] = jax.ShapeDtypeStruct((rows, D_MODEL // 2), jnp.int32)
        out_specs[1] = pl.BlockSpec((tm, D_MODEL // 2), lambda i: (i, 0))
        out_shape += [jax.ShapeDtypeStruct((rows, LANES), F32),
                      jax.ShapeDtypeStruct((rows // tm * SUBLANES, LANES), F32)]
        out_specs += [pl.BlockSpec((tm, LANES), lambda i: (i, 0)), pl.BlockSpec((SUBLANES, LANES), lambda i: (i, 0))]
    return pl.pallas_call(
        functools.partial(_merge_kernel, seq=seq, tm=tm, moe=moe),
        out_shape=tuple(out_shape),
        grid=(rows // tm,),
        in_specs=in_specs,
        out_specs=tuple(out_specs),
        compiler_params=_params(("parallel",), 56),
        name="merge",
    )(*args)


def _swiglu_acc(h, w1_ref, w3_ref, w2_ref, fs):
    a = _silu(jnp.dot(h, w1_ref[:, fs], preferred_element_type=F32)) * jnp.dot(h, w3_ref[:, fs],
                                                                                preferred_element_type=F32)
    return jnp.dot(a.astype(BF16), w2_ref[fs, :], preferred_element_type=F32)


def _swiglu(h, w1_ref, w3_ref, w2_ref):
    half = D_FF // 2
    return (_swiglu_acc(h, w1_ref, w3_ref, w2_ref, slice(0, half))
            + _swiglu_acc(h, w1_ref, w3_ref, w2_ref, slice(half, D_FF)))


def _ffn_kernel(h_ref, x_ref, mod_ref, w1_ref, w3_ref, w2_ref, o_ref):
    o_ref[...] = x_ref[...] + mod_ref[0, 5:6, :] * _swiglu(h_ref[...], w1_ref, w3_ref, w2_ref)


def _resident(shape):
    return pl.BlockSpec(shape, lambda *_: (0,) * len(shape), pipeline_mode=pl.Buffered(1))


def _ffn(h2, x, mod, w1, w3, w2, tm):
    rows = x.shape[0]
    per_mod = rows // mod.shape[0]
    tm = min(tm, per_mod)
    return pl.pallas_call(
        _ffn_kernel,
        out_shape=jax.ShapeDtypeStruct((rows, D_MODEL), F32),
        grid=(rows // tm,),
        in_specs=[pl.BlockSpec((tm, D_MODEL), lambda i: (i, 0)),
                  pl.BlockSpec((tm, D_MODEL), lambda i: (i, 0)),
                  pl.BlockSpec((1, SUBLANES, D_MODEL), lambda i: (i * tm // per_mod, 0, 0)),
                  _resident((D_MODEL, D_FF)), _resident((D_MODEL, D_FF)), _resident((D_FF, D_MODEL))],
        out_specs=pl.BlockSpec((tm, D_MODEL), lambda i: (i, 0)),
        compiler_params=_params(("parallel",), 56),
        name="ffn",
    )(h2, x, mod, w1, w3, w2)


def _cast_kernel(*refs):
    n = len(refs) // 2
    for src, dst in zip(refs[:n], refs[n:]):
        dst[...] = src[...].astype(dst.dtype)


def _cast_bf16(ws):
    e, r, c = ws[0].shape
    rb = max(d for d in range(16, r + 1, 16) if r % d == 0 and d * c * 4 <= 3 * MIB)
    spec = pl.BlockSpec((None, rb, c), lambda i, j: (i, j, 0))
    return pl.pallas_call(
        _cast_kernel,
        out_shape=tuple(jax.ShapeDtypeStruct(w.shape, BF16) for w in ws),
        grid=(e, r // rb),
        in_specs=[spec] * len(ws),
        out_specs=tuple([spec] * len(ws)),
        compiler_params=_params(("parallel", "parallel"), 48),
        name="cast_bf16",
    )(*ws)


SC_CORES = 2
SC_SUBCORES = 16
SC_CHUNK = 128


def _sc_gather(table, idx):
    nw = SC_CORES * SC_SUBCORES
    b, d = idx.shape[0], table.shape[1]
    per_w = b // nw
    assert per_w * nw == b and per_w % SC_CHUNK == 0
    mesh = plsc.VectorSubcoreMesh(core_axis_name="c", subcore_axis_name="s")

    def body(table_hbm, idx_hbm, out_hbm, idx_v, rows_v, sem):
        wid = lax.axis_index("s") * SC_CORES + lax.axis_index("c")

        @pl.loop(0, per_w // SC_CHUNK)
        def _(j):
            off = pl.multiple_of(wid * per_w + j * SC_CHUNK, SC_CHUNK)
            pltpu.sync_copy(idx_hbm.at[pl.ds(off, SC_CHUNK)], idx_v)
            pltpu.async_copy(table_hbm.at[idx_v], rows_v, sem).wait()
            pltpu.sync_copy(rows_v, out_hbm.at[pl.ds(off, SC_CHUNK)])

    return pl.kernel(
        body,
        out_type=jax.ShapeDtypeStruct((b, d), table.dtype),
        mesh=mesh,
        scratch_types=[pltpu.VMEM((SC_CHUNK,), jnp.int32), pltpu.VMEM((SC_CHUNK, d), table.dtype),
                       pltpu.SemaphoreType.DMA],
        name="sc_gather",
    )(table, idx)


def _sc_dispatch(table, slot, n_slots):
    nw = SC_CORES * SC_SUBCORES
    t, d = table.shape
    per_w = t // nw
    assert per_w * nw == t and per_w % SC_CHUNK == 0 and slot.shape[0] == 2 * t
    mesh = plsc.VectorSubcoreMesh(core_axis_name="c", subcore_axis_name="s")

    def body(table_hbm, slot_hbm, out_hbm, idx_v, rows_v, sem):
        wid = lax.axis_index("s") * SC_CORES + lax.axis_index("c")

        @pl.loop(0, per_w // SC_CHUNK)
        def _(j):
            off = pl.multiple_of(wid * per_w + j * SC_CHUNK, SC_CHUNK)
            pltpu.sync_copy(table_hbm.at[pl.ds(off, SC_CHUNK)], rows_v)
            for k in range(2):
                pltpu.sync_copy(slot_hbm.at[pl.ds(pl.multiple_of(k * t + off, SC_CHUNK), SC_CHUNK)], idx_v)
                pltpu.async_copy(rows_v, out_hbm.at[idx_v], sem).wait()

    return pl.kernel(
        body,
        out_type=jax.ShapeDtypeStruct((n_slots, d), table.dtype),
        mesh=mesh,
        scratch_types=[pltpu.VMEM((SC_CHUNK,), jnp.int32), pltpu.VMEM((SC_CHUNK, d), table.dtype),
                       pltpu.SemaphoreType.DMA],
        name="sc_dispatch",
    )(table, slot)


def _route_tables(route, counts, tile):
    t = route.shape[0]
    n_mt = counts.shape[0] // SUBLANES
    cnt = counts.reshape(n_mt, SUBLANES, LANES)[:, 0, :N_EXPERTS].astype(jnp.int32)
    incl = jnp.cumsum(cnt, axis=0)
    padded = (incl[-1] + tile - 1) // tile * tile
    gend = jnp.cumsum(padded)
    base = jnp.repeat((gend - padded)[None, :] + incl - cnt, t // n_mt, axis=0)
    e = route[:, ROUTE_I1:ROUTE_I2 + 1].astype(jnp.int32)
    r = route[:, ROUTE_R1:ROUTE_R2 + 1].astype(jnp.int32)
    pick = e[:, :, None] == jnp.arange(N_EXPERTS)[None, None, :]
    slot = jnp.sum(jnp.where(pick, base[:, None, :], 0), axis=-1) + r
    n_slots = 2 * t + N_EXPERTS * tile
    n_valid = gend[-1] // tile
    n_tiles = n_slots // tile
    tile_expert = jnp.sum(jnp.arange(n_tiles)[:, None] * tile >= gend[None, :], axis=1)
    last = jnp.take(tile_expert, jnp.maximum(n_valid - 1, 0))
    tile_expert = jnp.where(jnp.arange(n_tiles) < n_valid, tile_expert, last).astype(jnp.int32)
    return slot.T.reshape(-1).astype(jnp.int32), n_slots, tile_expert, n_valid.reshape(1).astype(jnp.int32)


def _moe_ffn_kernel(te_ref, nv_ref, h_ref, w1_ref, w3_ref, w2_ref, o_ref):
    i = pl.program_id(0)

    @pl.when(i < nv_ref[0])
    def _():
        h = _unpack_bf16_pairs(h_ref[...]).astype(BF16)
        o_ref[...] = _pack_bf16_pairs(_swiglu(h, w1_ref, w3_ref, w2_ref))

    @pl.when(i >= nv_ref[0])
    def _():
        o_ref[...] = jnp.zeros(o_ref.shape, jnp.int32)


def _moe_ffn(hs, tile_expert, n_valid, w1, w3, w2, tile):
    n_slots = hs.shape[0]
    grid_spec = pltpu.PrefetchScalarGridSpec(
        num_scalar_prefetch=2,
        grid=(n_slots // tile,),
        in_specs=[pl.BlockSpec((tile, D_MODEL // 2), lambda i, te, nv: (i, 0)),
                  pl.BlockSpec((None, D_MODEL, D_FF), lambda i, te, nv: (te[i], 0, 0)),
                  pl.BlockSpec((None, D_MODEL, D_FF), lambda i, te, nv: (te[i], 0, 0)),
                  pl.BlockSpec((None, D_FF, D_MODEL), lambda i, te, nv: (te[i], 0, 0))],
        out_specs=pl.BlockSpec((tile, D_MODEL // 2), lambda i, te, nv: (i, 0)),
    )
    return pl.pallas_call(
        _moe_ffn_kernel,
        out_shape=jax.ShapeDtypeStruct((n_slots, D_MODEL // 2), jnp.int32),
        grid_spec=grid_spec,
        compiler_params=_params(("arbitrary",), 56),
        name="moe_ffn",
    )(tile_expert, n_valid, hs, w1, w3, w2)


def _combine_kernel(x_ref, y1_ref, y2_ref, route_ref, mod_ref, o_ref):
    p1 = route_ref[:, ROUTE_P1:ROUTE_P1 + 1]
    p2 = route_ref[:, ROUTE_P2:ROUTE_P2 + 1]
    f = p1 * _unpack_bf16_pairs(y1_ref[...]) + p2 * _unpack_bf16_pairs(y2_ref[...])
    o_ref[...] = x_ref[...] + mod_ref[0, 5:6, :] * f


def _combine(x, yg, route, mod, tm):
    rows = x.shape[0]
    per_mod = rows // mod.shape[0]
    tm = min(tm, per_mod)
    return pl.pallas_call(
        _combine_kernel,
        out_shape=jax.ShapeDtypeStruct((rows, D_MODEL), F32),
        grid=(rows // tm,),
        in_specs=[pl.BlockSpec((tm, D_MODEL), lambda i: (i, 0)),
                  pl.BlockSpec((tm, D_MODEL // 2), lambda i: (i, 0)),
                  pl.BlockSpec((tm, D_MODEL // 2), lambda i: (i + rows // tm, 0)),
                  pl.BlockSpec((tm, LANES), lambda i: (i, 0)),
                  pl.BlockSpec((1, SUBLANES, D_MODEL), lambda i: (i * tm // per_mod, 0, 0))],
        out_specs=pl.BlockSpec((tm, D_MODEL), lambda i: (i, 0)),
        compiler_params=_params(("parallel",), 32),
        name="moe_combine",
    )(x, yg, yg, route, mod)


def _moe(h2p, x, mod, route, counts, w1, w3, w2, tile):
    slot, n_slots, tile_expert, n_valid = _route_tables(route, counts, tile)
    hs = _sc_dispatch(h2p, slot, n_slots)
    ys = _moe_ffn(hs, tile_expert, n_valid, w1, w3, w2, tile)
    return _combine(x, _sc_gather(ys, slot), route, mod, tm=512)


W_IN_SEGMENTS = ((0, 1024), (1536, 1024), (4128, 1024), (5152, 1024), (6176, 1024), (7200, 3072),
                 (2560, 1024), (3584, 512), (1024, 256), (1280, 256), (4096, 32))


def _permute_kernel(w_ref, o_ref):
    dst = 0
    for src, n in W_IN_SEGMENTS:
        o_ref[:, dst:dst + n] = w_ref[:, src:src + n].astype(BF16)
        dst += n
    o_ref[:, dst:] = jnp.zeros((o_ref.shape[0], o_ref.shape[1] - dst), BF16)


def _permute_w_in(w_in, l, rb=128):
    _, rows, cols = w_in.shape
    return pl.pallas_call(
        _permute_kernel,
        out_shape=jax.ShapeDtypeStruct((rows, N_PROJ), BF16),
        grid=(rows // rb,),
        in_specs=[pl.BlockSpec((None, rb, cols), lambda i: (l, i, 0))],
        out_specs=pl.BlockSpec((rb, N_PROJ), lambda i: (i, 0)),
        compiler_params=_params(("parallel",), 48),
        name="permute_w_in",
    )(w_in)


def _layer_params(l, w_in, q_norm_g, k_norm_g, w_attn_out, ssm_conv_w, ssm_conv_b, ssm_dt_bias, ssm_a_log,
                  ssm_d, ssm_norm_g, w_ssm_out, sconv_w, w_sconv_out, w_merge, norm_mix_g, norm_ffn_g,
                  ffn_w1, ffn_w3, ffn_w2, moe_router, moe_w1, moe_w3, moe_w2):
    w_perm = _permute_w_in(w_in, l)
    row = lambda v: v.reshape(1, -1)
    lane_row = lambda v, off: jnp.zeros((1, LANES), F32).at[0, off:off + SSM_HEADS].set(v)
    head_of_col = np.arange(D_MODEL) // SSM_HEADDIM
    expand = [jnp.asarray(np.tile(np.arange(LANES)[:, None] == head_of_col[None, :] + off, (3, 1)), BF16)
              for off in (0, SSM_HEADS)]
    r = np.arange(2 * SSM_CHUNK)[:, None]
    cidx = np.arange(SSM_CHUNK + 2 * HALO)[None, :]
    up = (r < SSM_CHUNK) & (((cidx == r - 1) & (r >= 1)) | ((r == 0) & (cidx == SSM_CHUNK + HALO - 1)))
    dn = (r >= SSM_CHUNK) & (((cidx == r - SSM_CHUNK + 1) & (r < 2 * SSM_CHUNK - 1))
                             | ((r == 2 * SSM_CHUNK - 1) & (cidx == SSM_CHUNK + HALO)))
    shift = jnp.asarray(up | dn, BF16)
    lp = {
        "w_in": w_perm,
        "norm_mix_g": row(norm_mix_g[l]), "norm_ffn_g": row(norm_ffn_g[l]),
        "q_norm_g": row(q_norm_g[l]), "k_norm_g": row(k_norm_g[l]),
        "w_attn_out": w_attn_out[l].astype(BF16), "w_ssm_out": w_ssm_out[l].astype(BF16),
        "w_sconv_out": w_sconv_out[l].astype(BF16), "w_merge": w_merge[l].astype(BF16),
        "cw_x": ssm_conv_w[l][:, :D_MODEL], "cw_bc": ssm_conv_w[l][:, D_MODEL:],
        "cb_x": row(ssm_conv_b[l][:D_MODEL]), "cb_bc": row(ssm_conv_b[l][D_MODEL:]),
        "alog": [lane_row(ssm_a_log[l, d], d * SSM_HEADS) for d in range(2)],
        "dtb": [lane_row(ssm_dt_bias[l, d], d * SSM_HEADS) for d in range(2)],
        "expand": expand,
        "shift": shift,
        "d_row": row(jnp.repeat(ssm_d[l], SSM_HEADDIM)),
        "ssm_norm_g": row(ssm_norm_g[l]),
        "sconv_w": sconv_w[l],
    }
    if l % 2 == 0:
        lp["router"] = None
        lp["ffn"] = tuple(t[l // 2].astype(BF16) for t in (ffn_w1, ffn_w3, ffn_w2))
    else:
        router = jnp.pad(moe_router[l // 2], ((0, 0), (0, LANES - N_EXPERTS)))
        r_hi = router.astype(BF16)
        lp["router"] = jnp.concatenate([r_hi, (router - r_hi.astype(F32)).astype(BF16)], axis=1)
        lp["ffn"] = _cast_bf16((moe_w1[l // 2], moe_w3[l // 2])) + _cast_bf16((moe_w2[l // 2],))
    return lp


def _rope_tables(seq):
    t = np.arange(seq)
    n = HEAD_DIM // 4
    inv = ROPE_THETA ** (-np.arange(n, dtype=np.float64) / n)
    ang_r = (t // GRID_W)[:, None] * inv[None, :]
    ang_c = (t % GRID_W)[:, None] * inv[None, :]
    zero = np.zeros_like(ang_r)
    cos = np.concatenate([np.cos(ang_r)] * 2 + [np.cos(ang_c)] * 2, axis=1)
    sa = np.concatenate([-np.sin(ang_r), zero, -np.sin(ang_c), zero], axis=1)
    sb = np.concatenate([zero, np.sin(ang_r), zero, np.sin(ang_c)], axis=1)
    return tuple(jnp.asarray(a, F32) for a in (cos, sa, sb))


def _trunk(x, mod, lp, nseq, seq, lat, layer=0, depth=1, carry=None):
    rows = nseq * seq
    proj = _inproj(x, mod, lp["norm_mix_g"], lp["w_in"], tm=min(2048, rows), tn=1152)
    if lat is None:
        attn, k_new, v_new = _attn_ctx(proj, lp["q_norm_g"], lp["k_norm_g"], nseq, seq, layer, depth,
                                       () if carry is None else carry[:2])
        *fwd, h_f = _ssd_pass(proj, lp, 0, nseq, seq, None, None, layer, depth, None if carry is None else carry[2])
        ssm, h_b = _ssd_pass(proj, lp, 1, nseq, seq, None, fwd, layer, depth, None if carry is None else carry[3])
    else:
        cache_k, cache_v, h0_f, h0_b, tables = lat
        qr, kall, vall = _prep_lat(proj, cache_k, cache_v, *tables, lp["q_norm_g"], lp["k_norm_g"], nseq, seq)
        attn = _attn_lat(qr, kall, vall, tq=256).reshape(rows, D_MODEL)
        k_new = v_new = None
        *fwd, h_f = _ssd_pass(proj, lp, 0, nseq, seq, h0_f)
        ssm, h_b = _ssd_pass(proj, lp, 1, nseq, seq, h0_b, fwd)
    outs = _merge(attn, ssm, proj, x, mod, lp, seq, tm=512)
    w1, w3, w2 = lp["ffn"]
    if lp["router"] is None:
        x = _ffn(outs[1], outs[0], mod, w1, w3, w2, tm=512)
    else:
        x = _moe(outs[1], outs[0], mod, outs[2], outs[3], w1, w3, w2, tile=512)
    return x, (k_new, v_new, h_f, h_b)


def kernel(x_prompt, x_sample, cache_k, cache_v, state_ssm_fwd, state_ssm_bwd, c, c_ctx, w_mod, b_mod, norm_mix_g, norm_ffn_g, w_in, q_norm_g, k_norm_g, w_attn_out, ssm_conv_w, ssm_conv_b, ssm_dt_bias, ssm_a_log, ssm_d, ssm_norm_g, w_ssm_out, sconv_w, w_sconv_out, w_merge, ffn_w1, ffn_w3, ffn_w2, moe_router, moe_w1, moe_w3, moe_w2):
    batch, seq, _ = x_prompt.shape
    nb, dseq, _ = x_sample.shape
    depth = w_in.shape[0]
    past = cache_k.shape[2]
    kvw = N_KV_HEADS * HEAD_DIM
    npair = SSM_HEADS // 2
    assert nb + 1 <= SUBLANES

    cond = jnp.zeros((SUBLANES, D_MODEL), F32).at[0].set(c_ctx).at[1:1 + nb].set(c)
    tables = _rope_tables(dseq)
    y_ctx = x_prompt.reshape(batch * seq, D_MODEL)
    y_lat = x_sample.reshape(nb * dseq, D_MODEL)
    mods = _modulation(cond, w_mod, b_mod.reshape(depth, 1, -1))
    carry = None
    for l in range(depth):
        lp = _layer_params(l, w_in, q_norm_g, k_norm_g, w_attn_out, ssm_conv_w, ssm_conv_b, ssm_dt_bias,
                           ssm_a_log, ssm_d, ssm_norm_g, w_ssm_out, sconv_w, w_sconv_out, w_merge,
                           norm_mix_g, norm_ffn_g, ffn_w1, ffn_w3, ffn_w2, moe_router, moe_w1, moe_w3, moe_w2)
        m = jnp.pad(mods[l].reshape(SUBLANES, 6, D_MODEL), ((0, 0), (0, SUBLANES - 6), (0, 0)))
        y_ctx, carry = _trunk(y_ctx, m[0:1], lp, batch, seq, None, l, depth, carry)
        lat = (cache_k[:, l].reshape(nb, past, kvw), cache_v[:, l].reshape(nb, past, kvw),
               state_ssm_fwd[:, l].reshape(nb, npair, SSM_CHUNK, SSM_STATE),
               state_ssm_bwd[:, l].reshape(nb, npair, SSM_CHUNK, SSM_STATE), tables)
        y_lat = _trunk(y_lat, m[1:1 + nb], lp, nb, dseq, lat)[0]
    k_all, v_all, hf_all, hb_all = carry
    state_shape = (batch, depth, SSM_HEADS, SSM_HEADDIM, SSM_STATE)
    return (y_ctx.reshape(batch, seq, D_MODEL), y_lat.reshape(nb, dseq, D_MODEL),
            k_all, v_all,
            hf_all.reshape(state_shape), hb_all.reshape(state_shape))
```

```python
import functools

import jax
import jax.numpy as jnp
import numpy as np
from jax import lax
from jax.experimental import pallas as pl
from jax.experimental.pallas import tpu as pltpu
from jax.experimental.pallas import tpu_sc as plsc

F32 = jnp.float32
BF16 = jnp.bfloat16
HIGHEST = lax.Precision.HIGHEST

D_MODEL = 1024
HEAD_DIM = 128
N_HEADS = 8
N_KV_HEADS = 2
Q_PER_KV = N_HEADS // N_KV_HEADS
ROPE_THETA = 10000.0
GRID_W = 64
SSM_HEADS = 16
SSM_HEADDIM = 64
SSM_STATE = 128
SSM_CHUNK = 128
D_FF = 2816
N_EXPERTS = 8
EPS = 1e-6
Q_SCALE = 1.4426950408889634 * HEAD_DIM ** -0.5

COL_Q, COL_Z, COL_SCB, COL_SCC, COL_SCH, COL_G = 0, 1024, 2048, 3072, 4096, 5120
COL_XS, COL_BC, COL_K, COL_V, COL_DT = 8192, 9216, 9728, 9984, 10240
N_PROJ = 10368
LANES = 128
SUBLANES = 8
HALO = 16
MIB = 1024 * 1024


def _params(sem, vmem_mib):
    return pltpu.CompilerParams(dimension_semantics=sem, vmem_limit_bytes=vmem_mib * MIB)


def _sigmoid(t):
    return 1.0 / (1.0 + jnp.exp(-t))


def _silu(t):
    return t * _sigmoid(t)


def _rms(t):
    return t * lax.rsqrt(jnp.mean(t * t, axis=-1, keepdims=True) + EPS)


def _bdot(a, b):
    return jnp.dot(a.astype(BF16), b.astype(BF16), preferred_element_type=F32)


def _hdot(a, b):
    return jnp.dot(a, b, precision=HIGHEST, preferred_element_type=F32)


def _split3(t):
    a = t.astype(BF16)
    r = t - a.astype(F32)
    b = r.astype(BF16)
    return a, b, (r - b.astype(F32)).astype(BF16)


def _mod_kernel(c_ref, w_ref, b_ref, o_ref):
    o_ref[...] = _hdot(_silu(c_ref[...]), w_ref[...]) + b_ref[...]


def _modulation(cond8, w, b):
    depth, _, width = w.shape
    return pl.pallas_call(
        _mod_kernel,
        out_shape=jax.ShapeDtypeStruct((depth, SUBLANES, width), F32),
        grid=(depth, width // D_MODEL),
        in_specs=[pl.BlockSpec((SUBLANES, D_MODEL), lambda l, j: (0, 0)),
                  pl.BlockSpec((None, D_MODEL, D_MODEL), lambda l, j: (l, 0, j)),
                  pl.BlockSpec((None, 1, D_MODEL), lambda l, j: (l, 0, j))],
        out_specs=pl.BlockSpec((None, SUBLANES, D_MODEL), lambda l, j: (l, 0, j)),
        compiler_params=_params(("parallel", "parallel"), 32),
        name="modulation",
    )(cond8, w, b)


def _inproj_kernel(x_ref, mod_ref, g_ref, w_ref, o_ref, h_scr):
    @pl.when(pl.program_id(1) == 0)
    def _():
        h = _rms(x_ref[...]) * g_ref[...]
        h = h * (1.0 + mod_ref[0, 1:2, :]) + mod_ref[0, 0:1, :]
        h_scr[...] = h.astype(BF16)

    o_ref[...] = _qk(h_scr[...], w_ref[...]).astype(o_ref.dtype)


def _inproj(x, mod, g, w, tm, tn):
    rows = x.shape[0]
    per_mod = rows // mod.shape[0]
    tm = min(tm, per_mod)
    return pl.pallas_call(
        _inproj_kernel,
        out_shape=jax.ShapeDtypeStruct((rows, N_PROJ), BF16),
        grid=(rows // tm, N_PROJ // tn),
        in_specs=[pl.BlockSpec((tm, D_MODEL), lambda i, j: (i, 0)),
                  pl.BlockSpec((1, SUBLANES, D_MODEL), lambda i, j: (i * tm // per_mod, 0, 0)),
                  pl.BlockSpec((1, D_MODEL), lambda i, j: (0, 0)),
                  pl.BlockSpec((tn, D_MODEL), lambda i, j: (j, 0))],
        out_specs=pl.BlockSpec((tm, tn), lambda i, j: (i, j)),
        scratch_shapes=[pltpu.VMEM((tm, D_MODEL), BF16)],
        compiler_params=_params(("parallel", "arbitrary"), 48),
        name="inproj",
    )(x, mod, g, w)


def _softmax_pv(s, vb):
    m = jnp.max(s, axis=-1, keepdims=True)
    p = jnp.exp2(s - m)
    l = jnp.sum(p, axis=-1, keepdims=True)
    return jnp.dot(p.astype(BF16), vb, preferred_element_type=F32) / l


def _qk(qb, kb):
    return lax.dot_general(qb, kb, (((1,), (1,)), ((), ())), preferred_element_type=F32)


def _attn_ctx_kernel(q_ref, k_ref, v_ref, qg_ref, kg_ref, *rest):
    a_ref, ko_ref, vo_ref = rest[-3:]
    v = v_ref[...].astype(F32)
    for hk in range(N_KV_HEADS):
        ks = slice(hk * HEAD_DIM, (hk + 1) * HEAD_DIM)
        kn = _rms(k_ref[:, ks].astype(F32)) * kg_ref[...]
        ko_ref[:, hk, :] = kn
        vo_ref[:, hk, :] = v[:, ks]
        knb = kn.astype(BF16)
        vb = v[:, ks].astype(BF16)
        for g in range(Q_PER_KV):
            hs = slice((hk * Q_PER_KV + g) * HEAD_DIM, (hk * Q_PER_KV + g + 1) * HEAD_DIM)
            qn = _rms(q_ref[:, hs].astype(F32)) * (qg_ref[...] * Q_SCALE)
            s = _qk(qn.astype(BF16), knb)
            a_ref[:, hs] = _softmax_pv(s, vb).astype(BF16)


def _attn_ctx(proj, qg, kg, nseq, seq, layer, depth, earlier):
    kvw = N_KV_HEADS * HEAD_DIM
    cache = jax.ShapeDtypeStruct((nseq, depth, seq, N_KV_HEADS, HEAD_DIM), F32)
    cache_spec = pl.BlockSpec((None, None, seq, N_KV_HEADS, HEAD_DIM), lambda b: (b, layer, 0, 0, 0))
    n_in = 5
    return pl.pallas_call(
        _attn_ctx_kernel,
        out_shape=(jax.ShapeDtypeStruct((nseq * seq, D_MODEL), BF16), cache, cache),
        grid=(nseq,),
        in_specs=[pl.BlockSpec((seq, D_MODEL), lambda b: (b, COL_Q // D_MODEL)),
                  pl.BlockSpec((seq, kvw), lambda b: (b, COL_K // kvw)),
                  pl.BlockSpec((seq, kvw), lambda b: (b, COL_V // kvw)),
                  pl.BlockSpec((1, HEAD_DIM), lambda b: (0, 0)),
                  pl.BlockSpec((1, HEAD_DIM), lambda b: (0, 0))] + [pl.BlockSpec(memory_space=pl.ANY)] * len(earlier),
        out_specs=(pl.BlockSpec((seq, D_MODEL), lambda b: (b, 0)), cache_spec, cache_spec),
        input_output_aliases={n_in + i: 1 + i for i in range(len(earlier))},
        compiler_params=_params(("parallel",), 32),
        name="attn_ctx",
    )(proj, proj, proj, qg, kg, *earlier)


def _rope(t, cos, sa, sb):
    return t * cos + pltpu.roll(t, 96, 1) * sa + pltpu.roll(t, 32, 1) * sb


def _prep_lat_kernel(q_ref, k_ref, v_ref, ck_ref, cv_ref, cos_ref, sa_ref, sb_ref, qg_ref, kg_ref,
                     qo_ref, ko_ref, vo_ref):
    t = pl.program_id(1)

    @pl.when(t == 0)
    def _():
        ko_ref[...] = ck_ref[...].astype(BF16)
        vo_ref[...] = cv_ref[...].astype(BF16)

    @pl.when(t > 0)
    def _():
        cos, sa, sb = cos_ref[...], sa_ref[...], sb_ref[...]
        vo_ref[...] = v_ref[...]
        for h in range(N_KV_HEADS):
            hs = slice(h * HEAD_DIM, (h + 1) * HEAD_DIM)
            ko_ref[:, hs] = _rope(_rms(k_ref[:, hs].astype(F32)) * kg_ref[...], cos, sa, sb).astype(BF16)
        for h in range(N_HEADS):
            hs = slice(h * HEAD_DIM, (h + 1) * HEAD_DIM)
            qn = _rms(q_ref[:, hs].astype(F32)) * (qg_ref[...] * Q_SCALE)
            qo_ref[:, hs] = _rope(qn, cos, sa, sb).astype(BF16)


def _prep_lat(proj, cache_k, cache_v, cos, sa, sb, qg, kg, nb, seq):
    past = cache_k.shape[1]
    tr = past
    kvw = N_KV_HEADS * HEAD_DIM
    nt = seq // tr
    prev = lambda t: jnp.maximum(t - 1, 0)
    return pl.pallas_call(
        _prep_lat_kernel,
        out_shape=(jax.ShapeDtypeStruct((nb, seq, D_MODEL), BF16),
                   jax.ShapeDtypeStruct((nb, past + seq, kvw), BF16),
                   jax.ShapeDtypeStruct((nb, past + seq, kvw), BF16)),
        grid=(nb, nt + 1),
        in_specs=[pl.BlockSpec((tr, D_MODEL), lambda b, t: (b * nt + prev(t), COL_Q // D_MODEL)),
                  pl.BlockSpec((tr, kvw), lambda b, t: (b * nt + prev(t), COL_K // kvw)),
                  pl.BlockSpec((tr, kvw), lambda b, t: (b * nt + prev(t), COL_V // kvw)),
                  pl.BlockSpec((None, past, kvw), lambda b, t: (b, 0, 0)),
                  pl.BlockSpec((None, past, kvw), lambda b, t: (b, 0, 0)),
                  pl.BlockSpec((tr, HEAD_DIM), lambda b, t: (prev(t), 0)),
                  pl.BlockSpec((tr, HEAD_DIM), lambda b, t: (prev(t), 0)),
                  pl.BlockSpec((tr, HEAD_DIM), lambda b, t: (prev(t), 0)),
                  pl.BlockSpec((1, HEAD_DIM), lambda b, t: (0, 0)),
                  pl.BlockSpec((1, HEAD_DIM), lambda b, t: (0, 0))],
        out_specs=(pl.BlockSpec((None, tr, D_MODEL), lambda b, t: (b, prev(t), 0)),
                   pl.BlockSpec((None, tr, kvw), lambda b, t: (b, t, 0)),
                   pl.BlockSpec((None, tr, kvw), lambda b, t: (b, t, 0))),
        compiler_params=_params(("parallel", "arbitrary"), 32),
        name="prep_lat",
    )(proj, proj, proj, cache_k, cache_v, cos, sa, sb, qg, kg)


def _attn_lat_kernel(q_ref, k_ref, v_ref, o_ref):
    kb = k_ref[...]
    vb = v_ref[...]
    for g in range(Q_PER_KV):
        hs = slice(g * HEAD_DIM, (g + 1) * HEAD_DIM)
        s = _qk(q_ref[:, hs], kb)
        o_ref[:, hs] = _softmax_pv(s, vb).astype(BF16)


def _attn_lat(qr, kall, vall, tq):
    nb, seq, _ = qr.shape
    nk = kall.shape[1]
    gw = Q_PER_KV * HEAD_DIM
    return pl.pallas_call(
        _attn_lat_kernel,
        out_shape=jax.ShapeDtypeStruct((nb, seq, D_MODEL), BF16),
        grid=(nb, N_KV_HEADS, seq // tq),
        in_specs=[pl.BlockSpec((None, tq, gw), lambda b, h, i: (b, i, h)),
                  pl.BlockSpec((None, nk, HEAD_DIM), lambda b, h, i: (b, 0, h)),
                  pl.BlockSpec((None, nk, HEAD_DIM), lambda b, h, i: (b, 0, h))],
        out_specs=pl.BlockSpec((None, tq, gw), lambda b, h, i: (b, i, h)),
        compiler_params=_params(("parallel", "parallel", "arbitrary"), 48),
        name="attn_lat",
    )(qr, kall, vall)


def _ssd_kernel(*refs, reverse, has_h0, nc):
    it = iter(refs)
    if reverse:
        xsc_ref, bcc_ref, dt_ref, alog_ref, dtb_ref, e_ref = (next(it) for _ in range(6))
    else:
        xs_ref, bc_ref, dt_ref, xsp_ref, xsn_ref, bcp_ref, bcn_ref = (next(it) for _ in range(7))
        cwx_ref, cbx_ref, cwb_ref, cbb_ref, alog_ref, dtb_ref, e_ref, shift_ref = (next(it) for _ in range(8))
    h0_ref = next(it) if has_h0 else None
    if reverse:
        z_ref, yf_ref, d_ref, g_ref = (next(it) for _ in range(4))
        y_ref, hout_ref, ht_scr = refs[-3:]
    else:
        y_ref, xsc_ref, bcc_ref, hout_ref, ht_scr = refs[-5:]

    q = SSM_CHUNK
    npair = SSM_HEADS // 2
    c = pl.program_id(1)
    cc = nc - 1 - c if reverse else c

    @pl.when(c == 0)
    def _():
        for p in range(npair):
            if has_h0:
                ht_scr[p] = h0_ref[p].T
            else:
                ht_scr[p] = jnp.zeros((q, q), F32)

    if reverse:
        xs = xsc_ref[...].astype(F32)
        bc = bcc_ref[...].astype(F32)
    else:
        keep_prev = jnp.where(cc == 0, 0.0, 1.0).astype(BF16)
        keep_next = jnp.where(cc == nc - 1, 0.0, 1.0).astype(BF16)

        def conv_silu(x_ref_, p_ref_, n_ref_, w_ref_, b_ref_):
            x = x_ref_[...]
            ext = jnp.concatenate([x, p_ref_[...] * keep_prev, n_ref_[...] * keep_next], axis=0)
            nb = jnp.dot(shift_ref[...], ext, preferred_element_type=F32)
            w = w_ref_[...]
            return _silu(w[0:1, :] * nb[:q, :] + w[1:2, :] * x.astype(F32) + w[2:3, :] * nb[q:, :] + b_ref_[...])

        xs = conv_silu(xs_ref, xsp_ref, xsn_ref, cwx_ref, cbx_ref)
        bc = conv_silu(bc_ref, bcp_ref, bcn_ref, cwb_ref, cbb_ref)
        xsc_ref[...] = xs.astype(BF16)
        bcc_ref[...] = bc.astype(BF16)

    off = SSM_HEADS if reverse else 0
    dt = jax.nn.softplus(dt_ref[...].astype(F32) + dtb_ref[...])
    da = dt * (-jnp.exp(alog_ref[...]))
    ii = lax.broadcasted_iota(jnp.int32, (q, q), 0)
    jj = lax.broadcasted_iota(jnp.int32, (q, q), 1)
    tri = (jj >= ii) if reverse else (ii >= jj)
    tri_b = jnp.where(tri, 1.0, 0.0).astype(BF16)
    tri_tb = jnp.where((ii >= jj) if reverse else (jj >= ii), 1.0, 0.0).astype(BF16)
    acs3 = jnp.dot(tri_b, jnp.concatenate(_split3(da), axis=1), preferred_element_type=F32)
    acs = acs3[:, :LANES] + acs3[:, LANES:2 * LANES] + acs3[:, 2 * LANES:]
    acs_t = jnp.dot(jnp.concatenate(_split3(da.T), axis=1), jnp.concatenate([tri_tb] * 3, axis=0),
                    preferred_element_type=F32)
    spread = jnp.dot(jnp.concatenate(_split3(jnp.concatenate([dt, acs], axis=0)), axis=1), e_ref[...],
                     preferred_element_type=F32)
    acs_x = spread[q:, :]
    tot_x = acs_x[0:1, :] if reverse else acs_x[q - 1:q, :]
    xdt = xs * spread[:q, :]
    xw = xdt * jnp.exp(tot_x - acs_x)
    eacs_x = jnp.exp(acs_x)
    etot_x = jnp.exp(tot_x)
    lo = lax.broadcasted_iota(jnp.int32, (q, LANES), 1) < SSM_HEADDIM

    parts = []
    for g in range(2):
        bg = bc[:, g * q:(g + 1) * q]
        cg = bc[:, (2 + g) * q:(3 + g) * q].astype(BF16)
        cb = _qk(cg, bg.astype(BF16))
        bgt = bg.T.astype(BF16)
        for pp in range(npair // 2):
            p = g * (npair // 2) + pp
            ps = slice(p * LANES, (p + 1) * LANES)
            ms = []
            for hh in range(2):
                ln = off + 2 * p + hh
                dec = jnp.where(tri, jnp.exp(acs[:, ln:ln + 1] - acs_t[ln:ln + 1, :]), 0.0)
                ms.append((cb * dec).astype(BF16))
            m = jnp.concatenate(ms, axis=1)
            xp = xdt[:, ps]
            bd = jnp.concatenate([jnp.where(lo, xp, 0.0), jnp.where(lo, 0.0, xp)], axis=0).astype(BF16)
            ht = ht_scr[p]
            yp = jnp.dot(m, bd, preferred_element_type=F32)
            yp = yp + jnp.dot(cg, ht.astype(BF16), preferred_element_type=F32) * eacs_x[:, ps]
            ht_scr[p] = etot_x[:, ps] * ht + jnp.dot(bgt, xw[:, ps].astype(BF16), preferred_element_type=F32)
            parts.append(yp)
    y = jnp.concatenate(parts, axis=1)

    if reverse:
        y = y + yf_ref[...] + d_ref[...] * xs
        y = y * _silu(z_ref[...].astype(F32))
        y_ref[...] = (_rms(y) * g_ref[...]).astype(y_ref.dtype)
    else:
        y_ref[...] = y

    @pl.when(c == nc - 1)
    def _():
        for p in range(npair):
            hout_ref[p] = ht_scr[p].T


def _ssd_pass(proj, lp, d, nseq, seq, h0, fwd=None, layer=0, depth=1, earlier=None):
    reverse = fwd is not None
    has_h0 = h0 is not None
    q = SSM_CHUNK
    nc = seq // q
    rows = nseq * seq
    npair = SSM_HEADS // 2
    hb = q // HALO
    bcw = 4 * SSM_STATE
    chunk = (lambda c: nc - 1 - c) if reverse else (lambda c: c)
    blk = lambda s, c: s * nc + chunk(c)
    prev8 = lambda s, c: jnp.maximum(blk(s, c) * hb - 1, 0)
    next8 = lambda s, c: jnp.minimum((blk(s, c) + 1) * hb, rows // HALO - 1)
    const = lambda s, c: (0, 0)
    tile = lambda width, col=0: pl.BlockSpec((q, width), lambda s, c: (blk(s, c), col // width))
    halo = lambda width, col, where: pl.BlockSpec((HALO, width), lambda s, c: (where(s, c), col // width))
    scan_consts = [pl.BlockSpec((1, LANES), const), pl.BlockSpec((1, LANES), const),
                   pl.BlockSpec((3 * LANES, D_MODEL), const)]
    scan_args = [lp["alog"][d], lp["dtb"][d], lp["expand"][d]]
    if reverse:
        yf, xsc, bcc = fwd
        in_specs = [tile(D_MODEL), tile(bcw), tile(LANES, COL_DT)] + scan_consts
        args = [xsc, bcc, proj] + scan_args
    else:
        in_specs = [tile(D_MODEL, COL_XS), tile(bcw, COL_BC), tile(LANES, COL_DT),
                    halo(D_MODEL, COL_XS, prev8), halo(D_MODEL, COL_XS, next8),
                    halo(bcw, COL_BC, prev8), halo(bcw, COL_BC, next8),
                    pl.BlockSpec((3, D_MODEL), const), pl.BlockSpec((1, D_MODEL), const),
                    pl.BlockSpec((3, bcw), const), pl.BlockSpec((1, bcw), const)] + scan_consts + [
                    pl.BlockSpec((2 * q, q + 2 * HALO), const)]
        args = [proj] * 7 + [lp["cw_x"], lp["cb_x"], lp["cw_bc"], lp["cb_bc"]] + scan_args + [lp["shift"]]
    if has_h0:
        in_specs.append(pl.BlockSpec((None, npair, q, q), lambda s, c: (s, 0, 0, 0)))
        args.append(h0)
    if reverse:
        in_specs += [tile(D_MODEL, COL_Z), tile(D_MODEL), pl.BlockSpec((1, D_MODEL), const),
                     pl.BlockSpec((1, D_MODEL), const)]
        args += [proj, yf, lp["d_row"], lp["ssm_norm_g"]]
    state = jax.ShapeDtypeStruct((nseq, depth, npair, q, q), F32)
    state_spec = pl.BlockSpec((None, None, npair, q, q), lambda s, c: (s, layer, 0, 0, 0))
    if reverse:
        out_shape = (jax.ShapeDtypeStruct((rows, D_MODEL), BF16), state)
        out_specs = (tile(D_MODEL), state_spec)
    else:
        out_shape = (jax.ShapeDtypeStruct((rows, D_MODEL), F32), jax.ShapeDtypeStruct((rows, D_MODEL), BF16),
                     jax.ShapeDtypeStruct((rows, bcw), BF16), state)
        out_specs = (tile(D_MODEL), tile(D_MODEL), tile(bcw), state_spec)
    aliases = {}
    if earlier is not None:
        aliases = {len(args): len(out_shape) - 1}
        in_specs.append(pl.BlockSpec(memory_space=pl.ANY))
        args.append(earlier)
    return pl.pallas_call(
        functools.partial(_ssd_kernel, reverse=reverse, has_h0=has_h0, nc=nc),
        out_shape=out_shape,
        grid=(nseq, nc),
        in_specs=in_specs,
        out_specs=out_specs,
        input_output_aliases=aliases,
        scratch_shapes=[pltpu.VMEM((npair, q, q), F32)],
        compiler_params=_params(("parallel", "arbitrary"), 32),
        name="ssd_bwd" if reverse else "ssd_fwd",
    )(*args)


ROUTE_I1, ROUTE_I2, ROUTE_P1, ROUTE_P2, ROUTE_R1, ROUTE_R2 = 0, 1, 2, 3, 4, 5


def _pack_bf16_pairs(v):
    n = v.shape[1] // 2
    bits = lax.bitcast_convert_type(v.astype(BF16).astype(F32), jnp.uint32)
    return lax.bitcast_convert_type((bits[:, :n] >> 16) | bits[:, n:], jnp.int32)


def _unpack_bf16_pairs(p):
    u = lax.bitcast_convert_type(p, jnp.uint32)
    lo = lax.bitcast_convert_type(u << 16, F32)
    hi = lax.bitcast_convert_type(u & jnp.uint32(0xFFFF0000), F32)
    return jnp.concatenate([lo, hi], axis=1)


def _top2_route(logits):
    tm = logits.shape[0]
    lane = lax.broadcasted_iota(jnp.int32, logits.shape, 1).astype(F32)
    lg = jnp.where(lane < N_EXPERTS, logits, -jnp.inf)
    m1 = jnp.max(lg, axis=-1, keepdims=True)
    i1 = jnp.min(jnp.where(lg == m1, lane, float(LANES)), axis=-1, keepdims=True)
    lg2 = jnp.where(lane == i1, -jnp.inf, lg)
    m2 = jnp.max(lg2, axis=-1, keepdims=True)
    i2 = jnp.min(jnp.where(lg2 == m2, lane, float(LANES)), axis=-1, keepdims=True)
    e = jnp.exp(m2 - m1)
    p1 = 1.0 / (1.0 + e)
    chosen = jnp.where(lane == i1, 1.0, jnp.where(lane == i2, 1.0, 0.0))
    earlier = (lax.broadcasted_iota(jnp.int32, (tm, tm), 0) > lax.broadcasted_iota(jnp.int32, (tm, tm), 1))
    ranks = _bdot(jnp.where(earlier, 1.0, 0.0), chosen)
    r1 = jnp.sum(jnp.where(lane == i1, ranks, 0.0), axis=-1, keepdims=True)
    r2 = jnp.sum(jnp.where(lane == i2, ranks, 0.0), axis=-1, keepdims=True)
    rec = jnp.where(lane == ROUTE_I1, i1, jnp.where(lane == ROUTE_I2, i2, 0.0))
    rec = jnp.where(lane == ROUTE_P1, p1, jnp.where(lane == ROUTE_P2, e * p1, rec))
    rec = jnp.where(lane == ROUTE_R1, r1, jnp.where(lane == ROUTE_R2, r2, rec))
    return rec, jnp.sum(chosen, axis=0, keepdims=True)


def _merge_kernel(*refs, seq, tm, moe):
    it = iter(refs)
    attn_ref, ssm_ref, scb_ref, scc_ref, sch_ref = (next(it) for _ in range(5))
    sccp_ref, sccn_ref, schp_ref, schn_ref = (next(it) for _ in range(4))
    g0_ref, g1_ref, g2_ref, x_ref, mod_ref = (next(it) for _ in range(5))
    wa_ref, ws_ref, wc_ref, wm_ref, scw_ref, ng_ref = (next(it) for _ in range(6))
    router_ref = next(it) if moe else None
    xo_ref, h2_ref = next(it), next(it)
    route_ref, cnt_ref = (next(it), next(it)) if moe else (None, None)

    i = pl.program_id(0)
    pos = (lax.broadcasted_iota(jnp.int32, (tm, 1), 0) + i * tm) & (seq - 1)
    u = scc_ref[...].astype(F32) * sch_ref[...].astype(F32)
    up = sccp_ref[HALO - 1:HALO, :].astype(F32) * schp_ref[HALO - 1:HALO, :].astype(F32)
    un = sccn_ref[0:1, :].astype(F32) * schn_ref[0:1, :].astype(F32)
    r = lax.broadcasted_iota(jnp.int32, (tm, 1), 0)
    um1 = jnp.where(pos == 0, 0.0, jnp.where(r == 0, up, pltpu.roll(u, 1, 0)))
    up1 = jnp.where(pos == seq - 1, 0.0, jnp.where(r == tm - 1, un, pltpu.roll(u, tm - 1, 0)))
    w = scw_ref[...]
    sc = scb_ref[...].astype(F32) * (w[0:1, :] * um1 + w[1:2, :] * u + w[2:3, :] * up1)

    gate = lambda g_ref: _sigmoid(g_ref[...].astype(F32))
    merged = gate(g0_ref) * jnp.dot(attn_ref[...], wa_ref[...], preferred_element_type=F32)
    merged = merged + gate(g1_ref) * jnp.dot(ssm_ref[...], ws_ref[...], preferred_element_type=F32)
    merged = merged + gate(g2_ref) * jnp.dot(sc.astype(BF16), wc_ref[...], preferred_element_type=F32)
    mod = mod_ref[0]
    xn = x_ref[...] + mod[2:3, :] * jnp.dot(merged.astype(BF16), wm_ref[...], preferred_element_type=F32)
    xo_ref[...] = xn
    h2 = _rms(xn) * ng_ref[...] * (1.0 + mod[4:5, :]) + mod[3:4, :]
    if moe:
        h2_ref[...] = _pack_bf16_pairs(h2)
        h_hi = h2.astype(BF16)
        h_lo = (h2 - h_hi.astype(F32)).astype(BF16)
        both = jnp.dot(h_hi, router_ref[...], preferred_element_type=F32)
        logits = both[:, :LANES] + both[:, LANES:] + jnp.dot(h_lo, router_ref[:, :LANES], preferred_element_type=F32)
        rec, counts = _top2_route(logits)
        route_ref[...] = rec
        cnt_ref[...] = jnp.broadcast_to(counts, cnt_ref.shape)
    else:
        h2_ref[...] = h2.astype(BF16)


def _merge(attn, ssm, proj, x, mod, lp, seq, tm):
    rows = x.shape[0]
    per_mod = rows // mod.shape[0]
    tm = min(tm, per_mod)
    moe = lp["router"] is not None
    hb = tm // HALO
    prev8 = lambda i: jnp.maximum(i * hb - 1, 0)
    next8 = lambda i: jnp.minimum((i + 1) * hb, rows // HALO - 1)
    tile = lambda col: pl.BlockSpec((tm, D_MODEL), lambda i: (i, col // D_MODEL))
    const = lambda shape: pl.BlockSpec(shape, lambda i: (0, 0))
    in_specs = [tile(0), tile(0), tile(COL_SCB), tile(COL_SCC), tile(COL_SCH),
                pl.BlockSpec((HALO, D_MODEL), lambda i: (prev8(i), COL_SCC // D_MODEL)),
                pl.BlockSpec((HALO, D_MODEL), lambda i: (next8(i), COL_SCC // D_MODEL)),
                pl.BlockSpec((HALO, D_MODEL), lambda i: (prev8(i), COL_SCH // D_MODEL)),
                pl.BlockSpec((HALO, D_MODEL), lambda i: (next8(i), COL_SCH // D_MODEL)),
                tile(COL_G), tile(COL_G + D_MODEL), tile(COL_G + 2 * D_MODEL), tile(0),
                pl.BlockSpec((1, SUBLANES, D_MODEL), lambda i: (i * tm // per_mod, 0, 0)),
                _resident((D_MODEL, D_MODEL)), _resident((D_MODEL, D_MODEL)), _resident((D_MODEL, D_MODEL)),
                _resident((D_MODEL, D_MODEL)), const((3, D_MODEL)), const((1, D_MODEL))]
    args = [attn, ssm, proj, proj, proj, proj, proj, proj, proj, proj, proj, proj, x, mod,
            lp["w_attn_out"], lp["w_ssm_out"], lp["w_sconv_out"], lp["w_merge"], lp["sconv_w"], lp["norm_ffn_g"]]
    out_shape = [jax.ShapeDtypeStruct((rows, D_MODEL), F32), jax.ShapeDtypeStruct((rows, D_MODEL), BF16)]
    out_specs = [tile(0), tile(0)]
    if moe:
        in_specs.append(_resident((D_MODEL, 2 * LANES)))
        args.append(lp["router"])
        out_shape[1] = jax.ShapeDtypeStruct((rows, D_MODEL // 2), jnp.int32)
        out_specs[1] = pl.BlockSpec((tm, D_MODEL // 2), lambda i: (i, 0))
        out_shape += [jax.ShapeDtypeStruct((rows, LANES), F32),
                      jax.ShapeDtypeStruct((rows // tm * SUBLANES, LANES), F32)]
        out_specs += [pl.BlockSpec((tm, LANES), lambda i: (i, 0)), pl.BlockSpec((SUBLANES, LANES), lambda i: (i, 0))]
    return pl.pallas_call(
        functools.partial(_merge_kernel, seq=seq, tm=tm, moe=moe),
        out_shape=tuple(out_shape),
        grid=(rows // tm,),
        in_specs=in_specs,
        out_specs=tuple(out_specs),
        compiler_params=_params(("parallel",), 56),
        name="merge",
    )(*args)


def _swiglu_acc(h, w1_ref, w3_ref, w2_ref, fs):
    a = _silu(jnp.dot(h, w1_ref[:, fs], preferred_element_type=F32)) * jnp.dot(h, w3_ref[:, fs],
                                                                                preferred_element_type=F32)
    return jnp.dot(a.astype(BF16), w2_ref[fs, :], preferred_element_type=F32)


def _swiglu(h, w1_ref, w3_ref, w2_ref):
    half = D_FF // 2
    return (_swiglu_acc(h, w1_ref, w3_ref, w2_ref, slice(0, half))
            + _swiglu_acc(h, w1_ref, w3_ref, w2_ref, slice(half, D_FF)))


def _ffn_kernel(h_ref, x_ref, mod_ref, w1_ref, w3_ref, w2_ref, o_ref):
    o_ref[...] = x_ref[...] + mod_ref[0, 5:6, :] * _swiglu(h_ref[...], w1_ref, w3_ref, w2_ref)


def _resident(shape):
    return pl.BlockSpec(shape, lambda *_: (0,) * len(shape), pipeline_mode=pl.Buffered(1))


def _ffn(h2, x, mod, w1, w3, w2, tm):
    rows = x.shape[0]
    per_mod = rows // mod.shape[0]
    tm = min(tm, per_mod)
    return pl.pallas_call(
        _ffn_kernel,
        out_shape=jax.ShapeDtypeStruct((rows, D_MODEL), F32),
        grid=(rows // tm,),
        in_specs=[pl.BlockSpec((tm, D_MODEL), lambda i: (i, 0)),
                  pl.BlockSpec((tm, D_MODEL), lambda i: (i, 0)),
                  pl.BlockSpec((1, SUBLANES, D_MODEL), lambda i: (i * tm // per_mod, 0, 0)),
                  _resident((D_MODEL, D_FF)), _resident((D_MODEL, D_FF)), _resident((D_FF, D_MODEL))],
        out_specs=pl.BlockSpec((tm, D_MODEL), lambda i: (i, 0)),
        compiler_params=_params(("parallel",), 56),
        name="ffn",
    )(h2, x, mod, w1, w3, w2)


def _cast_kernel(*refs):
    n = len(refs) // 2
    for src, dst in zip(refs[:n], refs[n:]):
        dst[...] = src[...].astype(dst.dtype)


def _cast_bf16(ws):
    e, r, c = ws[0].shape
    rb = max(d for d in range(16, r + 1, 16) if r % d == 0 and d * c * 4 <= 3 * MIB)
    spec = pl.BlockSpec((None, rb, c), lambda i, j: (i, j, 0))
    return pl.pallas_call(
        _cast_kernel,
        out_shape=tuple(jax.ShapeDtypeStruct(w.shape, BF16) for w in ws),
        grid=(e, r // rb),
        in_specs=[spec] * len(ws),
        out_specs=tuple([spec] * len(ws)),
        compiler_params=_params(("parallel", "parallel"), 48),
        name="cast_bf16",
    )(*ws)


SC_CORES = 2
SC_SUBCORES = 16
SC_CHUNK = 128


def _sc_gather(table, idx):
    nw = SC_CORES * SC_SUBCORES
    b, d = idx.shape[0], table.shape[1]
    per_w = b // nw
    assert per_w * nw == b and per_w % SC_CHUNK == 0
    mesh = plsc.VectorSubcoreMesh(core_axis_name="c", subcore_axis_name="s")

    def body(table_hbm, idx_hbm, out_hbm, idx_v, rows_v, sem):
        wid = lax.axis_index("s") * SC_CORES + lax.axis_index("c")

        @pl.loop(0, per_w // SC_CHUNK)
        def _(j):
            off = pl.multiple_of(wid * per_w + j * SC_CHUNK, SC_CHUNK)
            pltpu.sync_copy(idx_hbm.at[pl.ds(off, SC_CHUNK)], idx_v)
            pltpu.async_copy(table_hbm.at[idx_v], rows_v, sem).wait()
            pltpu.sync_copy(rows_v, out_hbm.at[pl.ds(off, SC_CHUNK)])

    return pl.kernel(
        body,
        out_type=jax.ShapeDtypeStruct((b, d), table.dtype),
        mesh=mesh,
        scratch_types=[pltpu.VMEM((SC_CHUNK,), jnp.int32), pltpu.VMEM((SC_CHUNK, d), table.dtype),
                       pltpu.SemaphoreType.DMA],
        name="sc_gather",
    )(table, idx)


def _sc_dispatch(table, slot, n_slots):
    nw = SC_CORES * SC_SUBCORES
    t, d = table.shape
    per_w = t // nw
    assert per_w * nw == t and per_w % SC_CHUNK == 0 and slot.shape[0] == 2 * t
    mesh = plsc.VectorSubcoreMesh(core_axis_name="c", subcore_axis_name="s")

    def body(table_hbm, slot_hbm, out_hbm, idx_v, rows_v, sem):
        wid = lax.axis_index("s") * SC_CORES + lax.axis_index("c")

        @pl.loop(0, per_w // SC_CHUNK)
        def _(j):
            off = pl.multiple_of(wid * per_w + j * SC_CHUNK, SC_CHUNK)
            pltpu.sync_copy(table_hbm.at[pl.ds(off, SC_CHUNK)], rows_v)
            for k in range(2):
                pltpu.sync_copy(slot_hbm.at[pl.ds(pl.multiple_of(k * t + off, SC_CHUNK), SC_CHUNK)], idx_v)
                pltpu.async_copy(rows_v, out_hbm.at[idx_v], sem).wait()

    return pl.kernel(
        body,
        out_type=jax.ShapeDtypeStruct((n_slots, d), table.dtype),
        mesh=mesh,
        scratch_types=[pltpu.VMEM((SC_CHUNK,), jnp.int32), pltpu.VMEM((SC_CHUNK, d), table.dtype),
                       pltpu.SemaphoreType.DMA],
        name="sc_dispatch",
    )(table, slot)


def _route_tables(route, counts, tile):
    t = route.shape[0]
    n_mt = counts.shape[0] // SUBLANES
    cnt = counts.reshape(n_mt, SUBLANES, LANES)[:, 0, :N_EXPERTS].astype(jnp.int32)
    incl = jnp.cumsum(cnt, axis=0)
    padded = (incl[-1] + tile - 1) // tile * tile
    gend = jnp.cumsum(padded)
    base = jnp.repeat((gend - padded)[None, :] + incl - cnt, t // n_mt, axis=0)
    e = route[:, ROUTE_I1:ROUTE_I2 + 1].astype(jnp.int32)
    r = route[:, ROUTE_R1:ROUTE_R2 + 1].astype(jnp.int32)
    pick = e[:, :, None] == jnp.arange(N_EXPERTS)[None, None, :]
    slot = jnp.sum(jnp.where(pick, base[:, None, :], 0), axis=-1) + r
    n_slots = 2 * t + N_EXPERTS * tile
    n_valid = gend[-1] // tile
    n_tiles = n_slots // tile
    tile_expert = jnp.sum(jnp.arange(n_tiles)[:, None] * tile >= gend[None, :], axis=1)
    last = jnp.take(tile_expert, jnp.maximum(n_valid - 1, 0))
    tile_expert = jnp.where(jnp.arange(n_tiles) < n_valid, tile_expert, last).astype(jnp.int32)
    return slot.T.reshape(-1).astype(jnp.int32), n_slots, tile_expert, n_valid.reshape(1).astype(jnp.int32)


def _moe_ffn_kernel(te_ref, nv_ref, h_ref, w1_ref, w3_ref, w2_ref, o_ref):
    i = pl.program_id(0)

    @pl.when(i < nv_ref[0])
    def _():
        h = _unpack_bf16_pairs(h_ref[...]).astype(BF16)
        o_ref[...] = _pack_bf16_pairs(_swiglu(h, w1_ref, w3_ref, w2_ref))

    @pl.when(i >= nv_ref[0])
    def _():
        o_ref[...] = jnp.zeros(o_ref.shape, jnp.int32)


def _moe_ffn(hs, tile_expert, n_valid, w1, w3, w2, tile):
    n_slots = hs.shape[0]
    grid_spec = pltpu.PrefetchScalarGridSpec(
        num_scalar_prefetch=2,
        grid=(n_slots // tile,),
        in_specs=[pl.BlockSpec((tile, D_MODEL // 2), lambda i, te, nv: (i, 0)),
                  pl.BlockSpec((None, D_MODEL, D_FF), lambda i, te, nv: (te[i], 0, 0)),
                  pl.BlockSpec((None, D_MODEL, D_FF), lambda i, te, nv: (te[i], 0, 0)),
                  pl.BlockSpec((None, D_FF, D_MODEL), lambda i, te, nv: (te[i], 0, 0))],
        out_specs=pl.BlockSpec((tile, D_MODEL // 2), lambda i, te, nv: (i, 0)),
    )
    return pl.pallas_call(
        _moe_ffn_kernel,
        out_shape=jax.ShapeDtypeStruct((n_slots, D_MODEL // 2), jnp.int32),
        grid_spec=grid_spec,
        compiler_params=_params(("arbitrary",), 56),
        name="moe_ffn",
    )(tile_expert, n_valid, hs, w1, w3, w2)


def _combine_kernel(x_ref, y1_ref, y2_ref, route_ref, mod_ref, o_ref):
    p1 = route_ref[:, ROUTE_P1:ROUTE_P1 + 1]
    p2 = route_ref[:, ROUTE_P2:ROUTE_P2 + 1]
    f = p1 * _unpack_bf16_pairs(y1_ref[...]) + p2 * _unpack_bf16_pairs(y2_ref[...])
    o_ref[...] = x_ref[...] + mod_ref[0, 5:6, :] * f


def _combine(x, yg, route, mod, tm):
    rows = x.shape[0]
    per_mod = rows // mod.shape[0]
    tm = min(tm, per_mod)
    return pl.pallas_call(
        _combine_kernel,
        out_shape=jax.ShapeDtypeStruct((rows, D_MODEL), F32),
        grid=(rows // tm,),
        in_specs=[pl.BlockSpec((tm, D_MODEL), lambda i: (i, 0)),
                  pl.BlockSpec((tm, D_MODEL // 2), lambda i: (i, 0)),
                  pl.BlockSpec((tm, D_MODEL // 2), lambda i: (i + rows // tm, 0)),
                  pl.BlockSpec((tm, LANES), lambda i: (i, 0)),
                  pl.BlockSpec((1, SUBLANES, D_MODEL), lambda i: (i * tm // per_mod, 0, 0))],
        out_specs=pl.BlockSpec((tm, D_MODEL), lambda i: (i, 0)),
        compiler_params=_params(("parallel",), 32),
        name="moe_combine",
    )(x, yg, yg, route, mod)


def _moe(h2p, x, mod, route, counts, w1, w3, w2, tile):
    slot, n_slots, tile_expert, n_valid = _route_tables(route, counts, tile)
    hs = _sc_dispatch(h2p, slot, n_slots)
    ys = _moe_ffn(hs, tile_expert, n_valid, w1, w3, w2, tile)
    return _combine(x, _sc_gather(ys, slot), route, mod, tm=512)


W_IN_SEGMENTS = ((0, 1024), (1536, 1024), (4128, 1024), (5152, 1024), (6176, 1024), (7200, 3072),
                 (2560, 1024), (3584, 512), (1024, 256), (1280, 256), (4096, 32))


def _permute_kernel(w_ref, o_ref):
    dst = 0
    for src, n in W_IN_SEGMENTS:
        o_ref[dst:dst + n, :] = w_ref[src:src + n, :].astype(BF16)
        dst += n
    o_ref[dst:, :] = jnp.zeros((o_ref.shape[0] - dst, o_ref.shape[1]), BF16)


def _permute_w_in(w_in, l, cb=128):
    w_t = jnp.swapaxes(w_in, 1, 2)
    _, feats, d = w_t.shape
    return pl.pallas_call(
        _permute_kernel,
        out_shape=jax.ShapeDtypeStruct((N_PROJ, d), BF16),
        grid=(d // cb,),
        in_specs=[pl.BlockSpec((None, feats, cb), lambda i: (l, 0, i))],
        out_specs=pl.BlockSpec((N_PROJ, cb), lambda i: (0, i)),
        compiler_params=_params(("parallel",), 48),
        name="permute_w_in",
    )(w_t)


def _layer_params(l, w_in, q_norm_g, k_norm_g, w_attn_out, ssm_conv_w, ssm_conv_b, ssm_dt_bias, ssm_a_log,
                  ssm_d, ssm_norm_g, w_ssm_out, sconv_w, w_sconv_out, w_merge, norm_mix_g, norm_ffn_g,
                  ffn_w1, ffn_w3, ffn_w2, moe_router, moe_w1, moe_w3, moe_w2):
    w_perm = _permute_w_in(w_in, l)
    row = lambda v: v.reshape(1, -1)
    lane_row = lambda v, off: jnp.zeros((1, LANES), F32).at[0, off:off + SSM_HEADS].set(v)
    head_of_col = np.arange(D_MODEL) // SSM_HEADDIM
    expand = [jnp.asarray(np.tile(np.arange(LANES)[:, None] == head_of_col[None, :] + off, (3, 1)), BF16)
              for off in (0, SSM_HEADS)]
    r = np.arange(2 * SSM_CHUNK)[:, None]
    cidx = np.arange(SSM_CHUNK + 2 * HALO)[None, :]
    up = (r < SSM_CHUNK) & (((cidx == r - 1) & (r >= 1)) | ((r == 0) & (cidx == SSM_CHUNK + HALO - 1)))
    dn = (r >= SSM_CHUNK) & (((cidx == r - SSM_CHUNK + 1) & (r < 2 * SSM_CHUNK - 1))
                             | ((r == 2 * SSM_CHUNK - 1) & (cidx == SSM_CHUNK + HALO)))
    shift = jnp.asarray(up | dn, BF16)
    lp = {
        "w_in": w_perm,
        "norm_mix_g": row(norm_mix_g[l]), "norm_ffn_g": row(norm_ffn_g[l]),
        "q_norm_g": row(q_norm_g[l]), "k_norm_g": row(k_norm_g[l]),
        "w_attn_out": w_attn_out[l].astype(BF16), "w_ssm_out": w_ssm_out[l].astype(BF16),
        "w_sconv_out": w_sconv_out[l].astype(BF16), "w_merge": w_merge[l].astype(BF16),
        "cw_x": ssm_conv_w[l][:, :D_MODEL], "cw_bc": ssm_conv_w[l][:, D_MODEL:],
        "cb_x": row(ssm_conv_b[l][:D_MODEL]), "cb_bc": row(ssm_conv_b[l][D_MODEL:]),
        "alog": [lane_row(ssm_a_log[l, d], d * SSM_HEADS) for d in range(2)],
        "dtb": [lane_row(ssm_dt_bias[l, d], d * SSM_HEADS) for d in range(2)],
        "expand": expand,
        "shift": shift,
        "d_row": row(jnp.repeat(ssm_d[l], SSM_HEADDIM)),
        "ssm_norm_g": row(ssm_norm_g[l]),
        "sconv_w": sconv_w[l],
    }
    if l % 2 == 0:
        lp["router"] = None
        lp["ffn"] = tuple(t[l // 2].astype(BF16) for t in (ffn_w1, ffn_w3, ffn_w2))
    else:
        router = jnp.pad(moe_router[l // 2], ((0, 0), (0, LANES - N_EXPERTS)))
        r_hi = router.astype(BF16)
        lp["router"] = jnp.concatenate([r_hi, (router - r_hi.astype(F32)).astype(BF16)], axis=1)
        lp["ffn"] = _cast_bf16((moe_w1[l // 2], moe_w3[l // 2])) + _cast_bf16((moe_w2[l // 2],))
    return lp


def _rope_tables(seq):
    t = np.arange(seq)
    n = HEAD_DIM // 4
    inv = ROPE_THETA ** (-np.arange(n, dtype=np.float64) / n)
    ang_r = (t // GRID_W)[:, None] * inv[None, :]
    ang_c = (t % GRID_W)[:, None] * inv[None, :]
    zero = np.zeros_like(ang_r)
    cos = np.concatenate([np.cos(ang_r)] * 2 + [np.cos(ang_c)] * 2, axis=1)
    sa = np.concatenate([-np.sin(ang_r), zero, -np.sin(ang_c), zero], axis=1)
    sb = np.concatenate([zero, np.sin(ang_r), zero, np.sin(ang_c)], axis=1)
    return tuple(jnp.asarray(a, F32) for a in (cos, sa, sb))


def _trunk(x, mod, lp, nseq, seq, lat, layer=0, depth=1, carry=None):
    rows = nseq * seq
    proj = _inproj(x, mod, lp["norm_mix_g"], lp["w_in"], tm=min(2048, rows), tn=1152)
    if lat is None:
        attn, k_new, v_new = _attn_ctx(proj, lp["q_norm_g"], lp["k_norm_g"], nseq, seq, layer, depth,
                                       () if carry is None else carry[:2])
        *fwd, h_f = _ssd_pass(proj, lp, 0, nseq, seq, None, None, layer, depth, None if carry is None else carry[2])
        ssm, h_b = _ssd_pass(proj, lp, 1, nseq, seq, None, fwd, layer, depth, None if carry is None else carry[3])
    else:
        cache_k, cache_v, h0_f, h0_b, tables = lat
        qr, kall, vall = _prep_lat(proj, cache_k, cache_v, *tables, lp["q_norm_g"], lp["k_norm_g"], nseq, seq)
        attn = _attn_lat(qr, kall, vall, tq=256).reshape(rows, D_MODEL)
        k_new = v_new = None
        *fwd, h_f = _ssd_pass(proj, lp, 0, nseq, seq, h0_f)
        ssm, h_b = _ssd_pass(proj, lp, 1, nseq, seq, h0_b, fwd)
    outs = _merge(attn, ssm, proj, x, mod, lp, seq, tm=512)
    w1, w3, w2 = lp["ffn"]
    if lp["router"] is None:
        x = _ffn(outs[1], outs[0], mod, w1, w3, w2, tm=512)
    else:
        x = _moe(outs[1], outs[0], mod, outs[2], outs[3], w1, w3, w2, tile=512)
    return x, (k_new, v_new, h_f, h_b)


def kernel(x_prompt, x_sample, cache_k, cache_v, state_ssm_fwd, state_ssm_bwd, c, c_ctx, w_mod, b_mod, norm_mix_g, norm_ffn_g, w_in, q_norm_g, k_norm_g, w_attn_out, ssm_conv_w, ssm_conv_b, ssm_dt_bias, ssm_a_log, ssm_d, ssm_norm_g, w_ssm_out, sconv_w, w_sconv_out, w_merge, ffn_w1, ffn_w3, ffn_w2, moe_router, moe_w1, moe_w3, moe_w2):
    batch, seq, _ = x_prompt.shape
    nb, dseq, _ = x_sample.shape
    depth = w_in.shape[0]
    past = cache_k.shape[2]
    kvw = N_KV_HEADS * HEAD_DIM
    npair = SSM_HEADS // 2
    assert nb + 1 <= SUBLANES

    cond = jnp.zeros((SUBLANES, D_MODEL), F32).at[0].set(c_ctx).at[1:1 + nb].set(c)
    tables = _rope_tables(dseq)
    y_ctx = x_prompt.reshape(batch * seq, D_MODEL)
    y_lat = x_sample.reshape(nb * dseq, D_MODEL)
    mods = _modulation(cond, w_mod, b_mod.reshape(depth, 1, -1))
    carry = None
    for l in range(depth):
        lp = _layer_params(l, w_in, q_norm_g, k_norm_g, w_attn_out, ssm_conv_w, ssm_conv_b, ssm_dt_bias,
                           ssm_a_log, ssm_d, ssm_norm_g, w_ssm_out, sconv_w, w_sconv_out, w_merge,
                           norm_mix_g, norm_ffn_g, ffn_w1, ffn_w3, ffn_w2, moe_router, moe_w1, moe_w3, moe_w2)
        m = jnp.pad(mods[l].reshape(SUBLANES, 6, D_MODEL), ((0, 0), (0, SUBLANES - 6), (0, 0)))
        y_ctx, carry = _trunk(y_ctx, m[0:1], lp, batch, seq, None, l, depth, carry)
        lat = (cache_k[:, l].reshape(nb, past, kvw), cache_v[:, l].reshape(nb, past, kvw),
               state_ssm_fwd[:, l].reshape(nb, npair, SSM_CHUNK, SSM_STATE),
               state_ssm_bwd[:, l].reshape(nb, npair, SSM_CHUNK, SSM_STATE), tables)
        y_lat = _trunk(y_lat, m[1:1 + nb], lp, nb, dseq, lat)[0]
    k_all, v_all, hf_all, hb_all = carry
    state_shape = (batch, depth, SSM_HEADS, SSM_HEADDIM, SSM_STATE)
    return (y_ctx.reshape(batch, seq, D_MODEL), y_lat.reshape(nb, dseq, D_MODEL),
            k_all, v_all,
            hf_all.reshape(state_shape), hb_all.reshape(state_shape))
```

```python
import functools

import jax
import jax.numpy as jnp
import numpy as np
from jax import lax
from jax.experimental import pallas as pl
from jax.experimental.pallas import tpu as pltpu
from jax.experimental.pallas import tpu_sc as plsc

F32 = jnp.float32
BF16 = jnp.bfloat16
HIGHEST = lax.Precision.HIGHEST

D_MODEL = 1024
HEAD_DIM = 128
N_HEADS = 8
N_KV_HEADS = 2
Q_PER_KV = N_HEADS // N_KV_HEADS
ROPE_THETA = 10000.0
GRID_W = 64
SSM_HEADS = 16
SSM_HEADDIM = 64
SSM_STATE = 128
SSM_CHUNK = 128
D_FF = 2816
N_EXPERTS = 8
EPS = 1e-6
Q_SCALE = 1.4426950408889634 * HEAD_DIM ** -0.5

COL_Q, COL_Z, COL_SCB, COL_SCC, COL_SCH, COL_G = 0, 1024, 2048, 3072, 4096, 5120
COL_XS, COL_BC, COL_K, COL_V, COL_DT = 8192, 9216, 9728, 9984, 10240
N_PROJ = 10368
LANES = 128
SUBLANES = 8
HALO = 16
MIB = 1024 * 1024


def _params(sem, vmem_mib):
    return pltpu.CompilerParams(dimension_semantics=sem, vmem_limit_bytes=vmem_mib * MIB)


def _sigmoid(t):
    return 1.0 / (1.0 + jnp.exp(-t))


def _silu(t):
    return t * _sigmoid(t)


def _rms(t):
    return t * lax.rsqrt(jnp.mean(t * t, axis=-1, keepdims=True) + EPS)


def _bdot(a, b):
    return jnp.dot(a.astype(BF16), b.astype(BF16), preferred_element_type=F32)


def _hdot(a, b):
    return jnp.dot(a, b, precision=HIGHEST, preferred_element_type=F32)


def _split3(t):
    a = t.astype(BF16)
    r = t - a.astype(F32)
    b = r.astype(BF16)
    return a, b, (r - b.astype(F32)).astype(BF16)


def _mod_kernel(c_ref, w_ref, b_ref, o_ref):
    o_ref[...] = _hdot(_silu(c_ref[...]), w_ref[...]) + b_ref[...]


def _modulation(cond8, w, b):
    depth, _, width = w.shape
    return pl.pallas_call(
        _mod_kernel,
        out_shape=jax.ShapeDtypeStruct((depth, SUBLANES, width), F32),
        grid=(depth, width // D_MODEL),
        in_specs=[pl.BlockSpec((SUBLANES, D_MODEL), lambda l, j: (0, 0)),
                  pl.BlockSpec((None, D_MODEL, D_MODEL), lambda l, j: (l, 0, j)),
                  pl.BlockSpec((None, 1, D_MODEL), lambda l, j: (l, 0, j))],
        out_specs=pl.BlockSpec((None, SUBLANES, D_MODEL), lambda l, j: (l, 0, j)),
        compiler_params=_params(("parallel", "parallel"), 32),
        name="modulation",
    )(cond8, w, b)


def _inproj_kernel(x_ref, mod_ref, g_ref, w_ref, o_ref, h_scr):
    @pl.when(pl.program_id(1) == 0)
    def _():
        h = _rms(x_ref[...]) * g_ref[...]
        h = h * (1.0 + mod_ref[0, 1:2, :]) + mod_ref[0, 0:1, :]
        h_scr[...] = h.astype(BF16)

    o_ref[...] = _qk(h_scr[...], w_ref[...]).astype(o_ref.dtype)


def _inproj(x, mod, g, w, tm, tn):
    rows = x.shape[0]
    per_mod = rows // mod.shape[0]
    tm = min(tm, per_mod)
    return pl.pallas_call(
        _inproj_kernel,
        out_shape=jax.ShapeDtypeStruct((rows, N_PROJ), BF16),
        grid=(rows // tm, N_PROJ // tn),
        in_specs=[pl.BlockSpec((tm, D_MODEL), lambda i, j: (i, 0)),
                  pl.BlockSpec((1, SUBLANES, D_MODEL), lambda i, j: (i * tm // per_mod, 0, 0)),
                  pl.BlockSpec((1, D_MODEL), lambda i, j: (0, 0)),
                  pl.BlockSpec((tn, D_MODEL), lambda i, j: (j, 0))],
        out_specs=pl.BlockSpec((tm, tn), lambda i, j: (i, j)),
        scratch_shapes=[pltpu.VMEM((tm, D_MODEL), BF16)],
        compiler_params=_params(("parallel", "arbitrary"), 48),
        name="inproj",
    )(x, mod, g, w)


def _softmax_pv(s, vb):
    m = jnp.max(s, axis=-1, keepdims=True)
    p = jnp.exp2(s - m)
    l = jnp.sum(p, axis=-1, keepdims=True)
    return jnp.dot(p.astype(BF16), vb, preferred_element_type=F32) / l


def _qk(qb, kb):
    return lax.dot_general(qb, kb, (((1,), (1,)), ((), ())), preferred_element_type=F32)


def _attn_ctx_kernel(q_ref, k_ref, v_ref, qg_ref, kg_ref, *rest):
    a_ref, ko_ref, vo_ref = rest[-3:]
    v = v_ref[...].astype(F32)
    for hk in range(N_KV_HEADS):
        ks = slice(hk * HEAD_DIM, (hk + 1) * HEAD_DIM)
        kn = _rms(k_ref[:, ks].astype(F32)) * kg_ref[...]
        ko_ref[:, hk, :] = kn
        vo_ref[:, hk, :] = v[:, ks]
        knb = kn.astype(BF16)
        vb = v[:, ks].astype(BF16)
        for g in range(Q_PER_KV):
            hs = slice((hk * Q_PER_KV + g) * HEAD_DIM, (hk * Q_PER_KV + g + 1) * HEAD_DIM)
            qn = _rms(q_ref[:, hs].astype(F32)) * (qg_ref[...] * Q_SCALE)
            s = _qk(qn.astype(BF16), knb)
            a_ref[:, hs] = _softmax_pv(s, vb).astype(BF16)


def _attn_ctx(proj, qg, kg, nseq, seq, layer, depth, earlier):
    kvw = N_KV_HEADS * HEAD_DIM
    cache = jax.ShapeDtypeStruct((nseq, depth, seq, N_KV_HEADS, HEAD_DIM), F32)
    cache_spec = pl.BlockSpec((None, None, seq, N_KV_HEADS, HEAD_DIM), lambda b: (b, layer, 0, 0, 0))
    n_in = 5
    return pl.pallas_call(
        _attn_ctx_kernel,
        out_shape=(jax.ShapeDtypeStruct((nseq * seq, D_MODEL), BF16), cache, cache),
        grid=(nseq,),
        in_specs=[pl.BlockSpec((seq, D_MODEL), lambda b: (b, COL_Q // D_MODEL)),
                  pl.BlockSpec((seq, kvw), lambda b: (b, COL_K // kvw)),
                  pl.BlockSpec((seq, kvw), lambda b: (b, COL_V // kvw)),
                  pl.BlockSpec((1, HEAD_DIM), lambda b: (0, 0)),
                  pl.BlockSpec((1, HEAD_DIM), lambda b: (0, 0))] + [pl.BlockSpec(memory_space=pl.ANY)] * len(earlier),
        out_specs=(pl.BlockSpec((seq, D_MODEL), lambda b: (b, 0)), cache_spec, cache_spec),
        input_output_aliases={n_in + i: 1 + i for i in range(len(earlier))},
        compiler_params=_params(("parallel",), 32),
        name="attn_ctx",
    )(proj, proj, proj, qg, kg, *earlier)


def _rope(t, cos, sa, sb):
    return t * cos + pltpu.roll(t, 96, 1) * sa + pltpu.roll(t, 32, 1) * sb


def _prep_lat_kernel(q_ref, k_ref, v_ref, ck_ref, cv_ref, cos_ref, sa_ref, sb_ref, qg_ref, kg_ref,
                     qo_ref, ko_ref, vo_ref):
    t = pl.program_id(1)

    @pl.when(t == 0)
    def _():
        ko_ref[...] = ck_ref[...].astype(BF16)
        vo_ref[...] = cv_ref[...].astype(BF16)

    @pl.when(t > 0)
    def _():
        cos, sa, sb = cos_ref[...], sa_ref[...], sb_ref[...]
        vo_ref[...] = v_ref[...]
        for h in range(N_KV_HEADS):
            hs = slice(h * HEAD_DIM, (h + 1) * HEAD_DIM)
            ko_ref[:, hs] = _rope(_rms(k_ref[:, hs].astype(F32)) * kg_ref[...], cos, sa, sb).astype(BF16)
        for h in range(N_HEADS):
            hs = slice(h * HEAD_DIM, (h + 1) * HEAD_DIM)
            qn = _rms(q_ref[:, hs].astype(F32)) * (qg_ref[...] * Q_SCALE)
            qo_ref[:, hs] = _rope(qn, cos, sa, sb).astype(BF16)


def _prep_lat(proj, cache_k, cache_v, cos, sa, sb, qg, kg, nb, seq):
    past = cache_k.shape[1]
    tr = past
    kvw = N_KV_HEADS * HEAD_DIM
    nt = seq // tr
    prev = lambda t: jnp.maximum(t - 1, 0)
    return pl.pallas_call(
        _prep_lat_kernel,
        out_shape=(jax.ShapeDtypeStruct((nb, seq, D_MODEL), BF16),
                   jax.ShapeDtypeStruct((nb, past + seq, kvw), BF16),
                   jax.ShapeDtypeStruct((nb, past + seq, kvw), BF16)),
        grid=(nb, nt + 1),
        in_specs=[pl.BlockSpec((tr, D_MODEL), lambda b, t: (b * nt + prev(t), COL_Q // D_MODEL)),
                  pl.BlockSpec((tr, kvw), lambda b, t: (b * nt + prev(t), COL_K // kvw)),
                  pl.BlockSpec((tr, kvw), lambda b, t: (b * nt + prev(t), COL_V // kvw)),
                  pl.BlockSpec((None, past, kvw), lambda b, t: (b, 0, 0)),
                  pl.BlockSpec((None, past, kvw), lambda b, t: (b, 0, 0)),
                  pl.BlockSpec((tr, HEAD_DIM), lambda b, t: (prev(t), 0)),
                  pl.BlockSpec((tr, HEAD_DIM), lambda b, t: (prev(t), 0)),
                  pl.BlockSpec((tr, HEAD_DIM), lambda b, t: (prev(t), 0)),
                  pl.BlockSpec((1, HEAD_DIM), lambda b, t: (0, 0)),
                  pl.BlockSpec((1, HEAD_DIM), lambda b, t: (0, 0))],
        out_specs=(pl.BlockSpec((None, tr, D_MODEL), lambda b, t: (b, prev(t), 0)),
                   pl.BlockSpec((None, tr, kvw), lambda b, t: (b, t, 0)),
                   pl.BlockSpec((None, tr, kvw), lambda b, t: (b, t, 0))),
        compiler_params=_params(("parallel", "arbitrary"), 32),
        name="prep_lat",
    )(proj, proj, proj, cache_k, cache_v, cos, sa, sb, qg, kg)


def _attn_lat_kernel(q_ref, k_ref, v_ref, o_ref):
    kb = k_ref[...]
    vb = v_ref[...]
    for g in range(Q_PER_KV):
        hs = slice(g * HEAD_DIM, (g + 1) * HEAD_DIM)
        s = _qk(q_ref[:, hs], kb)
        o_ref[:, hs] = _softmax_pv(s, vb).astype(BF16)


def _attn_lat(qr, kall, vall, tq):
    nb, seq, _ = qr.shape
    nk = kall.shape[1]
    gw = Q_PER_KV * HEAD_DIM
    return pl.pallas_call(
        _attn_lat_kernel,
        out_shape=jax.ShapeDtypeStruct((nb, seq, D_MODEL), BF16),
        grid=(nb, N_KV_HEADS, seq // tq),
        in_specs=[pl.BlockSpec((None, tq, gw), lambda b, h, i: (b, i, h)),
                  pl.BlockSpec((None, nk, HEAD_DIM), lambda b, h, i: (b, 0, h)),
                  pl.BlockSpec((None, nk, HEAD_DIM), lambda b, h, i: (b, 0, h))],
        out_specs=pl.BlockSpec((None, tq, gw), lambda b, h, i: (b, i, h)),
        compiler_params=_params(("parallel", "parallel", "arbitrary"), 48),
        name="attn_lat",
    )(qr, kall, vall)


SSD_GROUP = 2


def _ssd_kernel(*refs, reverse, has_h0, nc):
    it = iter(refs)
    if reverse:
        xsc_ref, bcc_ref, dt_ref, alog_ref, dtb_ref, e_ref = (next(it) for _ in range(6))
    else:
        xs_ref, bc_ref, dt_ref, xsp_ref, xsn_ref, bcp_ref, bcn_ref = (next(it) for _ in range(7))
        cwx_ref, cbx_ref, cwb_ref, cbb_ref, alog_ref, dtb_ref, e_ref, shift_ref = (next(it) for _ in range(8))
    h0_ref = next(it) if has_h0 else None
    if reverse:
        z_ref, yf_ref, d_ref, g_ref = (next(it) for _ in range(4))
        y_ref, hout_ref, ht_scr = refs[-3:]
    else:
        y_ref, xsc_ref, bcc_ref, hout_ref, ht_scr = refs[-5:]

    q = SSM_CHUNK
    npair = SSM_HEADS // 2
    c = pl.program_id(1)
    cc = nc - 1 - c if reverse else c

    @pl.when(c == 0)
    def _():
        for s in range(SSD_GROUP):
            for p in range(npair):
                if has_h0:
                    ht_scr[s, p] = h0_ref[s, p].T
                else:
                    ht_scr[s, p] = jnp.zeros((q, q), F32)

    off = SSM_HEADS if reverse else 0
    ii = lax.broadcasted_iota(jnp.int32, (q, q), 0)
    jj = lax.broadcasted_iota(jnp.int32, (q, q), 1)
    tri = (jj >= ii) if reverse else (ii >= jj)
    tri_b = jnp.where(tri, 1.0, 0.0).astype(BF16)
    tri_tb3 = jnp.concatenate([jnp.where((ii >= jj) if reverse else (jj >= ii), 1.0, 0.0).astype(BF16)] * 3, axis=0)
    lo = lax.broadcasted_iota(jnp.int32, (q, LANES), 1) < SSM_HEADDIM
    neg_a = -jnp.exp(alog_ref[...])
    if not reverse:
        keep_prev = jnp.where(cc == 0, 0.0, 1.0).astype(BF16)
        keep_next = jnp.where(cc == nc - 1, 0.0, 1.0).astype(BF16)

    def conv_silu(s, x_ref_, p_ref_, n_ref_, w_ref_, b_ref_):
        x = x_ref_[s]
        ext = jnp.concatenate([x, p_ref_[s] * keep_prev, n_ref_[s] * keep_next], axis=0)
        nb = jnp.dot(shift_ref[...], ext, preferred_element_type=F32)
        w = w_ref_[...]
        return _silu(w[0:1, :] * nb[:q, :] + w[1:2, :] * x.astype(F32) + w[2:3, :] * nb[q:, :] + b_ref_[...])

    for s in range(SSD_GROUP):
        if reverse:
            xs = xsc_ref[s].astype(F32)
            bc = bcc_ref[s].astype(F32)
        else:
            xs = conv_silu(s, xs_ref, xsp_ref, xsn_ref, cwx_ref, cbx_ref)
            bc = conv_silu(s, bc_ref, bcp_ref, bcn_ref, cwb_ref, cbb_ref)
            xsc_ref[s] = xs.astype(BF16)
            bcc_ref[s] = bc.astype(BF16)

        dt = jax.nn.softplus(dt_ref[s].astype(F32) + dtb_ref[...])
        da = dt * neg_a
        acs3 = jnp.dot(tri_b, jnp.concatenate(_split3(da), axis=1), preferred_element_type=F32)
        acs = acs3[:, :LANES] + acs3[:, LANES:2 * LANES] + acs3[:, 2 * LANES:]
        acs_t = jnp.dot(jnp.concatenate(_split3(da.T), axis=1), tri_tb3, preferred_element_type=F32)
        spread = jnp.dot(jnp.concatenate(_split3(jnp.concatenate([dt, acs], axis=0)), axis=1), e_ref[...],
                         preferred_element_type=F32)
        acs_x = spread[q:, :]
        tot_x = acs_x[0:1, :] if reverse else acs_x[q - 1:q, :]
        xdt = xs * spread[:q, :]
        xw = xdt * jnp.exp(tot_x - acs_x)
        eacs_x = jnp.exp(acs_x)
        etot_x = jnp.exp(tot_x)

        parts = []
        for g in range(2):
            bg = bc[:, g * q:(g + 1) * q]
            cg = bc[:, (2 + g) * q:(3 + g) * q].astype(BF16)
            cb = _qk(cg, bg.astype(BF16))
            bgt = bg.T.astype(BF16)
            for pp in range(npair // 2):
                p = g * (npair // 2) + pp
                ps = slice(p * LANES, (p + 1) * LANES)
                ms = []
                for hh in range(2):
                    ln = off + 2 * p + hh
                    dec = jnp.where(tri, jnp.exp(acs[:, ln:ln + 1] - acs_t[ln:ln + 1, :]), 0.0)
                    ms.append((cb * dec).astype(BF16))
                m = jnp.concatenate(ms, axis=1)
                xp = xdt[:, ps]
                bd = jnp.concatenate([jnp.where(lo, xp, 0.0), jnp.where(lo, 0.0, xp)], axis=0).astype(BF16)
                ht = ht_scr[s, p]
                yp = jnp.dot(m, bd, preferred_element_type=F32)
                yp = yp + jnp.dot(cg, ht.astype(BF16), preferred_element_type=F32) * eacs_x[:, ps]
                ht_scr[s, p] = etot_x[:, ps] * ht + jnp.dot(bgt, xw[:, ps].astype(BF16),
                                                            preferred_element_type=F32)
                parts.append(yp)
        y = jnp.concatenate(parts, axis=1)

        if reverse:
            y = y + yf_ref[s] + d_ref[...] * xs
            y = y * _silu(z_ref[s].astype(F32))
            y_ref[s] = (_rms(y) * g_ref[...]).astype(y_ref.dtype)
        else:
            y_ref[s] = y

    @pl.when(c == nc - 1)
    def _():
        for s in range(SSD_GROUP):
            for p in range(npair):
                hout_ref[s, p] = ht_scr[s, p].T


def _ssd_pass(proj, lp, d, nseq, seq, h0, fwd=None, layer=0, depth=1, earlier=None):
    reverse = fwd is not None
    has_h0 = h0 is not None
    q = SSM_CHUNK
    nc = seq // q
    g = SSD_GROUP
    assert nseq % g == 0
    npair = SSM_HEADS // 2
    hb = q // HALO
    bcw = 4 * SSM_STATE
    chunk = (lambda c: nc - 1 - c) if reverse else (lambda c: c)
    prev8 = lambda c: jnp.maximum(chunk(c) * hb - 1, 0)
    next8 = lambda c: jnp.minimum((chunk(c) + 1) * hb, seq // HALO - 1)
    const = lambda s, c: (0, 0)
    tile = lambda width, col=0: pl.BlockSpec((g, q, width), lambda s, c: (s, chunk(c), col // width))
    halo = lambda width, col, where: pl.BlockSpec((g, HALO, width), lambda s, c: (s, where(c), col // width))
    seqs = lambda t: t.reshape(nseq, seq, t.shape[-1])
    scan_consts = [pl.BlockSpec((1, LANES), const), pl.BlockSpec((1, LANES), const),
                   pl.BlockSpec((3 * LANES, D_MODEL), const)]
    scan_args = [lp["alog"][d], lp["dtb"][d], lp["expand"][d]]
    proj3 = seqs(proj)
    if reverse:
        yf, xsc, bcc = fwd
        in_specs = [tile(D_MODEL), tile(bcw), tile(LANES, COL_DT)] + scan_consts
        args = [xsc, bcc, proj3] + scan_args
    else:
        in_specs = [tile(D_MODEL, COL_XS), tile(bcw, COL_BC), tile(LANES, COL_DT),
                    halo(D_MODEL, COL_XS, prev8), halo(D_MODEL, COL_XS, next8),
                    halo(bcw, COL_BC, prev8), halo(bcw, COL_BC, next8),
                    pl.BlockSpec((3, D_MODEL), const), pl.BlockSpec((1, D_MODEL), const),
                    pl.BlockSpec((3, bcw), const), pl.BlockSpec((1, bcw), const)] + scan_consts + [
                    pl.BlockSpec((2 * q, q + 2 * HALO), const)]
        args = [proj3] * 7 + [lp["cw_x"], lp["cb_x"], lp["cw_bc"], lp["cb_bc"]] + scan_args + [lp["shift"]]
    if has_h0:
        in_specs.append(pl.BlockSpec((g, npair, q, q), lambda s, c: (s, 0, 0, 0)))
        args.append(h0)
    if reverse:
        in_specs += [tile(D_MODEL, COL_Z), tile(D_MODEL), pl.BlockSpec((1, D_MODEL), const),
                     pl.BlockSpec((1, D_MODEL), const)]
        args += [proj3, yf, lp["d_row"], lp["ssm_norm_g"]]
    state = jax.ShapeDtypeStruct((nseq, depth, npair, q, q), F32)
    state_spec = pl.BlockSpec((g, None, npair, q, q), lambda s, c: (s, layer, 0, 0, 0))
    if reverse:
        out_shape = (jax.ShapeDtypeStruct((nseq, seq, D_MODEL), BF16), state)
        out_specs = (tile(D_MODEL), state_spec)
    else:
        out_shape = (jax.ShapeDtypeStruct((nseq, seq, D_MODEL), F32),
                     jax.ShapeDtypeStruct((nseq, seq, D_MODEL), BF16),
                     jax.ShapeDtypeStruct((nseq, seq, bcw), BF16), state)
        out_specs = (tile(D_MODEL), tile(D_MODEL), tile(bcw), state_spec)
    aliases = {}
    if earlier is not None:
        aliases = {len(args): len(out_shape) - 1}
        in_specs.append(pl.BlockSpec(memory_space=pl.ANY))
        args.append(earlier)
    outs = pl.pallas_call(
        functools.partial(_ssd_kernel, reverse=reverse, has_h0=has_h0, nc=nc),
        out_shape=out_shape,
        grid=(nseq // g, nc),
        in_specs=in_specs,
        out_specs=out_specs,
        input_output_aliases=aliases,
        scratch_shapes=[pltpu.VMEM((g, npair, q, q), F32)],
        compiler_params=_params(("parallel", "arbitrary"), 48),
        name="ssd_bwd" if reverse else "ssd_fwd",
    )(*args)
    if reverse:
        return outs[0].reshape(nseq * seq, D_MODEL), outs[1]
    return outs


ROUTE_I1, ROUTE_I2, ROUTE_P1, ROUTE_P2, ROUTE_R1, ROUTE_R2 = 0, 1, 2, 3, 4, 5


def _pack_bf16_pairs(v):
    n = v.shape[1] // 2
    bits = lax.bitcast_convert_type(v.astype(BF16).astype(F32), jnp.uint32)
    return lax.bitcast_convert_type((bits[:, :n] >> 16) | bits[:, n:], jnp.int32)


def _unpack_bf16_pairs(p):
    u = lax.bitcast_convert_type(p, jnp.uint32)
    lo = lax.bitcast_convert_type(u << 16, F32)
    hi = lax.bitcast_convert_type(u & jnp.uint32(0xFFFF0000), F32)
    return jnp.concatenate([lo, hi], axis=1)


def _top2_route(logits):
    tm = logits.shape[0]
    lane = lax.broadcasted_iota(jnp.int32, logits.shape, 1).astype(F32)
    lg = jnp.where(lane < N_EXPERTS, logits, -jnp.inf)
    m1 = jnp.max(lg, axis=-1, keepdims=True)
    i1 = jnp.min(jnp.where(lg == m1, lane, float(LANES)), axis=-1, keepdims=True)
    lg2 = jnp.where(lane == i1, -jnp.inf, lg)
    m2 = jnp.max(lg2, axis=-1, keepdims=True)
    i2 = jnp.min(jnp.where(lg2 == m2, lane, float(LANES)), axis=-1, keepdims=True)
    e = jnp.exp(m2 - m1)
    p1 = 1.0 / (1.0 + e)
    chosen = jnp.where(lane == i1, 1.0, jnp.where(lane == i2, 1.0, 0.0))
    earlier = (lax.broadcasted_iota(jnp.int32, (tm, tm), 0) > lax.broadcasted_iota(jnp.int32, (tm, tm), 1))
    ranks = _bdot(jnp.where(earlier, 1.0, 0.0), chosen)
    r1 = jnp.sum(jnp.where(lane == i1, ranks, 0.0), axis=-1, keepdims=True)
    r2 = jnp.sum(jnp.where(lane == i2, ranks, 0.0), axis=-1, keepdims=True)
    rec = jnp.where(lane == ROUTE_I1, i1, jnp.where(lane == ROUTE_I2, i2, 0.0))
    rec = jnp.where(lane == ROUTE_P1, p1, jnp.where(lane == ROUTE_P2, e * p1, rec))
    rec = jnp.where(lane == ROUTE_R1, r1, jnp.where(lane == ROUTE_R2, r2, rec))
    return rec, jnp.sum(chosen, axis=0, keepdims=True)


def _merge_kernel(*refs, seq, tm, moe):
    it = iter(refs)
    attn_ref, ssm_ref, scb_ref, scc_ref, sch_ref = (next(it) for _ in range(5))
    sccp_ref, sccn_ref, schp_ref, schn_ref = (next(it) for _ in range(4))
    g0_ref, g1_ref, g2_ref, x_ref, mod_ref = (next(it) for _ in range(5))
    wa_ref, ws_ref, wc_ref, wm_ref, scw_ref, ng_ref = (next(it) for _ in range(6))
    router_ref = next(it) if moe else None
    xo_ref, h2_ref = next(it), next(it)
    route_ref, cnt_ref = (next(it), next(it)) if moe else (None, None)

    i = pl.program_id(0)
    pos = (lax.broadcasted_iota(jnp.int32, (tm, 1), 0) + i * tm) & (seq - 1)
    u = scc_ref[...].astype(F32) * sch_ref[...].astype(F32)
    up = sccp_ref[HALO - 1:HALO, :].astype(F32) * schp_ref[HALO - 1:HALO, :].astype(F32)
    un = sccn_ref[0:1, :].astype(F32) * schn_ref[0:1, :].astype(F32)
    r = lax.broadcasted_iota(jnp.int32, (tm, 1), 0)
    um1 = jnp.where(pos == 0, 0.0, jnp.where(r == 0, up, pltpu.roll(u, 1, 0)))
    up1 = jnp.where(pos == seq - 1, 0.0, jnp.where(r == tm - 1, un, pltpu.roll(u, tm - 1, 0)))
    w = scw_ref[...]
    sc = scb_ref[...].astype(F32) * (w[0:1, :] * um1 + w[1:2, :] * u + w[2:3, :] * up1)

    gate = lambda g_ref: _sigmoid(g_ref[...].astype(F32))
    merged = gate(g0_ref) * jnp.dot(attn_ref[...], wa_ref[...], preferred_element_type=F32)
    merged = merged + gate(g1_ref) * jnp.dot(ssm_ref[...], ws_ref[...], preferred_element_type=F32)
    merged = merged + gate(g2_ref) * jnp.dot(sc.astype(BF16), wc_ref[...], preferred_element_type=F32)
    mod = mod_ref[0]
    xn = x_ref[...] + mod[2:3, :] * jnp.dot(merged.astype(BF16), wm_ref[...], preferred_element_type=F32)
    xo_ref[...] = xn
    h2 = _rms(xn) * ng_ref[...] * (1.0 + mod[4:5, :]) + mod[3:4, :]
    if moe:
        h2_ref[...] = _pack_bf16_pairs(h2)
        h_hi = h2.astype(BF16)
        h_lo = (h2 - h_hi.astype(F32)).astype(BF16)
        both = jnp.dot(h_hi, router_ref[...], preferred_element_type=F32)
        logits = both[:, :LANES] + both[:, LANES:] + jnp.dot(h_lo, router_ref[:, :LANES], preferred_element_type=F32)
        rec, counts = _top2_route(logits)
        route_ref[...] = rec
        cnt_ref[...] = jnp.broadcast_to(counts, cnt_ref.shape)
    else:
        h2_ref[...] = h2.astype(BF16)


def _merge(attn, ssm, proj, x, mod, lp, seq, tm):
    rows = x.shape[0]
    per_mod = rows // mod.shape[0]
    tm = min(tm, per_mod)
    moe = lp["router"] is not None
    hb = tm // HALO
    prev8 = lambda i: jnp.maximum(i * hb - 1, 0)
    next8 = lambda i: jnp.minimum((i + 1) * hb, rows // HALO - 1)
    tile = lambda col: pl.BlockSpec((tm, D_MODEL), lambda i: (i, col // D_MODEL))
    const = lambda shape: pl.BlockSpec(shape, lambda i: (0, 0))
    in_specs = [tile(0), tile(0), tile(COL_SCB), tile(COL_SCC), tile(COL_SCH),
                pl.BlockSpec((HALO, D_MODEL), lambda i: (prev8(i), COL_SCC // D_MODEL)),
                pl.BlockSpec((HALO, D_MODEL), lambda i: (next8(i), COL_SCC // D_MODEL)),
                pl.BlockSpec((HALO, D_MODEL), lambda i: (prev8(i), COL_SCH // D_MODEL)),
                pl.BlockSpec((HALO, D_MODEL), lambda i: (next8(i), COL_SCH // D_MODEL)),
                tile(COL_G), tile(COL_G + D_MODEL), tile(COL_G + 2 * D_MODEL), tile(0),
                pl.BlockSpec((1, SUBLANES, D_MODEL), lambda i: (i * tm // per_mod, 0, 0)),
                _resident((D_MODEL, D_MODEL)), _resident((D_MODEL, D_MODEL)), _resident((D_MODEL, D_MODEL)),
                _resident((D_MODEL, D_MODEL)), const((3, D_MODEL)), const((1, D_MODEL))]
    args = [attn, ssm, proj, proj, proj, proj, proj, proj, proj, proj, proj, proj, x, mod,
            lp["w_attn_out"], lp["w_ssm_out"], lp["w_sconv_out"], lp["w_merge"], lp["sconv_w"], lp["norm_ffn_g"]]
    out_shape = [jax.ShapeDtypeStruct((rows, D_MODEL), F32), jax.ShapeDtypeStruct((rows, D_MODEL), BF16)]
    out_specs = [tile(0), tile(0)]
    if moe:
        in_specs.append(_resident((D_MODEL, 2 * LANES)))
        args.append(lp["router"])
        out_shape[1] = jax.ShapeDtypeStruct((rows, D_MODEL // 2), jnp.int32)
        out_specs[1] = pl.BlockSpec((tm, D_MODEL // 2), lambda i: (i, 0))
        out_shape += [jax.ShapeDtypeStruct((rows, LANES), F32),
                      jax.ShapeDtypeStruct((rows // tm * SUBLANES, LANES), F32)]
        out_specs += [pl.BlockSpec((tm, LANES), lambda i: (i, 0)), pl.BlockSpec((SUBLANES, LANES), lambda i: (i, 0))]
    return pl.pallas_call(
        functools.partial(_merge_kernel, seq=seq, tm=tm, moe=moe),
        out_shape=tuple(out_shape),
        grid=(rows // tm,),
        in_specs=in_specs,
        out_specs=tuple(out_specs),
        compiler_params=_params(("parallel",), 56),
        name="merge",
    )(*args)


def _swiglu_acc(h, w1_ref, w3_ref, w2_ref, fs):
    a = _silu(jnp.dot(h, w1_ref[:, fs], preferred_element_type=F32)) * jnp.dot(h, w3_ref[:, fs],
                                                                                preferred_element_type=F32)
    return jnp.dot(a.astype(BF16), w2_ref[fs, :], preferred_element_type=F32)


def _swiglu(h, w1_ref, w3_ref, w2_ref):
    half = D_FF // 2
    return (_swiglu_acc(h, w1_ref, w3_ref, w2_ref, slice(0, half))
            + _swiglu_acc(h, w1_ref, w3_ref, w2_ref, slice(half, D_FF)))


def _ffn_kernel(h_ref, x_ref, mod_ref, w1_ref, w3_ref, w2_ref, o_ref):
    o_ref[...] = x_ref[...] + mod_ref[0, 5:6, :] * _swiglu(h_ref[...], w1_ref, w3_ref, w2_ref)


def _resident(shape):
    return pl.BlockSpec(shape, lambda *_: (0,) * len(shape), pipeline_mode=pl.Buffered(1))


def _ffn(h2, x, mod, w1, w3, w2, tm):
    rows = x.shape[0]
    per_mod = rows // mod.shape[0]
    tm = min(tm, per_mod)
    return pl.pallas_call(
        _ffn_kernel,
        out_shape=jax.ShapeDtypeStruct((rows, D_MODEL), F32),
        grid=(rows // tm,),
        in_specs=[pl.BlockSpec((tm, D_MODEL), lambda i: (i, 0)),
                  pl.BlockSpec((tm, D_MODEL), lambda i: (i, 0)),
                  pl.BlockSpec((1, SUBLANES, D_MODEL), lambda i: (i * tm // per_mod, 0, 0)),
                  _resident((D_MODEL, D_FF)), _resident((D_MODEL, D_FF)), _resident((D_FF, D_MODEL))],
        out_specs=pl.BlockSpec((tm, D_MODEL), lambda i: (i, 0)),
        compiler_params=_params(("parallel",), 56),
        name="ffn",
    )(h2, x, mod, w1, w3, w2)


def _cast_kernel(*refs):
    n = len(refs) // 2
    for src, dst in zip(refs[:n], refs[n:]):
        dst[...] = src[...].astype(dst.dtype)


def _cast_bf16(ws):
    e, r, c = ws[0].shape
    rb = max(d for d in range(16, r + 1, 16) if r % d == 0 and d * c * 4 <= 3 * MIB)
    spec = pl.BlockSpec((None, rb, c), lambda i, j: (i, j, 0))
    return pl.pallas_call(
        _cast_kernel,
        out_shape=tuple(jax.ShapeDtypeStruct(w.shape, BF16) for w in ws),
        grid=(e, r // rb),
        in_specs=[spec] * len(ws),
        out_specs=tuple([spec] * len(ws)),
        compiler_params=_params(("parallel", "parallel"), 48),
        name="cast_bf16",
    )(*ws)


SC_CORES = 2
SC_SUBCORES = 16
SC_CHUNK = 128


def _sc_gather(table, idx):
    nw = SC_CORES * SC_SUBCORES
    b, d = idx.shape[0], table.shape[1]
    per_w = b // nw
    assert per_w * nw == b and per_w % SC_CHUNK == 0
    mesh = plsc.VectorSubcoreMesh(core_axis_name="c", subcore_axis_name="s")

    def body(table_hbm, idx_hbm, out_hbm, idx_v, rows_v, sem):
        wid = lax.axis_index("s") * SC_CORES + lax.axis_index("c")

        @pl.loop(0, per_w // SC_CHUNK)
        def _(j):
            off = pl.multiple_of(wid * per_w + j * SC_CHUNK, SC_CHUNK)
            pltpu.sync_copy(idx_hbm.at[pl.ds(off, SC_CHUNK)], idx_v)
            pltpu.async_copy(table_hbm.at[idx_v], rows_v, sem).wait()
            pltpu.sync_copy(rows_v, out_hbm.at[pl.ds(off, SC_CHUNK)])

    return pl.kernel(
        body,
        out_type=jax.ShapeDtypeStruct((b, d), table.dtype),
        mesh=mesh,
        scratch_types=[pltpu.VMEM((SC_CHUNK,), jnp.int32), pltpu.VMEM((SC_CHUNK, d), table.dtype),
                       pltpu.SemaphoreType.DMA],
        name="sc_gather",
    )(table, idx)


def _sc_dispatch(table, slot, n_slots):
    nw = SC_CORES * SC_SUBCORES
    t, d = table.shape
    per_w = t // nw
    assert per_w * nw == t and per_w % SC_CHUNK == 0 and slot.shape[0] == 2 * t
    mesh = plsc.VectorSubcoreMesh(core_axis_name="c", subcore_axis_name="s")

    def body(table_hbm, slot_hbm, out_hbm, idx_v, rows_v, sem):
        wid = lax.axis_index("s") * SC_CORES + lax.axis_index("c")

        @pl.loop(0, per_w // SC_CHUNK)
        def _(j):
            off = pl.multiple_of(wid * per_w + j * SC_CHUNK, SC_CHUNK)
            pltpu.sync_copy(table_hbm.at[pl.ds(off, SC_CHUNK)], rows_v)
            for k in range(2):
                pltpu.sync_copy(slot_hbm.at[pl.ds(pl.multiple_of(k * t + off, SC_CHUNK), SC_CHUNK)], idx_v)
                pltpu.async_copy(rows_v, out_hbm.at[idx_v], sem).wait()

    return pl.kernel(
        body,
        out_type=jax.ShapeDtypeStruct((n_slots, d), table.dtype),
        mesh=mesh,
        scratch_types=[pltpu.VMEM((SC_CHUNK,), jnp.int32), pltpu.VMEM((SC_CHUNK, d), table.dtype),
                       pltpu.SemaphoreType.DMA],
        name="sc_dispatch",
    )(table, slot)


def _route_tables(route, counts, tile):
    t = route.shape[0]
    n_mt = counts.shape[0] // SUBLANES
    cnt = counts.reshape(n_mt, SUBLANES, LANES)[:, 0, :N_EXPERTS].astype(jnp.int32)
    incl = jnp.cumsum(cnt, axis=0)
    padded = (incl[-1] + tile - 1) // tile * tile
    gend = jnp.cumsum(padded)
    base = jnp.repeat((gend - padded)[None, :] + incl - cnt, t // n_mt, axis=0)
    e = route[:, ROUTE_I1:ROUTE_I2 + 1].astype(jnp.int32)
    r = route[:, ROUTE_R1:ROUTE_R2 + 1].astype(jnp.int32)
    pick = e[:, :, None] == jnp.arange(N_EXPERTS)[None, None, :]
    slot = jnp.sum(jnp.where(pick, base[:, None, :], 0), axis=-1) + r
    n_slots = 2 * t + N_EXPERTS * tile
    n_valid = gend[-1] // tile
    n_tiles = n_slots // tile
    tile_expert = jnp.sum(jnp.arange(n_tiles)[:, None] * tile >= gend[None, :], axis=1)
    last = jnp.take(tile_expert, jnp.maximum(n_valid - 1, 0))
    tile_expert = jnp.where(jnp.arange(n_tiles) < n_valid, tile_expert, last).astype(jnp.int32)
    return slot.T.reshape(-1).astype(jnp.int32), n_slots, tile_expert, n_valid.reshape(1).astype(jnp.int32)


def _moe_ffn_kernel(te_ref, nv_ref, h_ref, w1_ref, w3_ref, w2_ref, o_ref):
    i = pl.program_id(0)

    @pl.when(i < nv_ref[0])
    def _():
        h = _unpack_bf16_pairs(h_ref[...]).astype(BF16)
        o_ref[...] = _pack_bf16_pairs(_swiglu(h, w1_ref, w3_ref, w2_ref))

    @pl.when(i >= nv_ref[0])
    def _():
        o_ref[...] = jnp.zeros(o_ref.shape, jnp.int32)


def _moe_ffn(hs, tile_expert, n_valid, w1, w3, w2, tile):
    n_slots = hs.shape[0]
    grid_spec = pltpu.PrefetchScalarGridSpec(
        num_scalar_prefetch=2,
        grid=(n_slots // tile,),
        in_specs=[pl.BlockSpec((tile, D_MODEL // 2), lambda i, te, nv: (i, 0)),
                  pl.BlockSpec((None, D_MODEL, D_FF), lambda i, te, nv: (te[i], 0, 0)),
                  pl.BlockSpec((None, D_MODEL, D_FF), lambda i, te, nv: (te[i], 0, 0)),
                  pl.BlockSpec((None, D_FF, D_MODEL), lambda i, te, nv: (te[i], 0, 0))],
        out_specs=pl.BlockSpec((tile, D_MODEL // 2), lambda i, te, nv: (i, 0)),
    )
    return pl.pallas_call(
        _moe_ffn_kernel,
        out_shape=jax.ShapeDtypeStruct((n_slots, D_MODEL // 2), jnp.int32),
        grid_spec=grid_spec,
        compiler_params=_params(("arbitrary",), 56),
        name="moe_ffn",
    )(tile_expert, n_valid, hs, w1, w3, w2)


def _combine_kernel(x_ref, y1_ref, y2_ref, route_ref, mod_ref, o_ref):
    p1 = route_ref[:, ROUTE_P1:ROUTE_P1 + 1]
    p2 = route_ref[:, ROUTE_P2:ROUTE_P2 + 1]
    f = p1 * _unpack_bf16_pairs(y1_ref[...]) + p2 * _unpack_bf16_pairs(y2_ref[...])
    o_ref[...] = x_ref[...] + mod_ref[0, 5:6, :] * f


def _combine(x, yg, route, mod, tm):
    rows = x.shape[0]
    per_mod = rows // mod.shape[0]
    tm = min(tm, per_mod)
    return pl.pallas_call(
        _combine_kernel,
        out_shape=jax.ShapeDtypeStruct((rows, D_MODEL), F32),
        grid=(rows // tm,),
        in_specs=[pl.BlockSpec((tm, D_MODEL), lambda i: (i, 0)),
                  pl.BlockSpec((tm, D_MODEL // 2), lambda i: (i, 0)),
                  pl.BlockSpec((tm, D_MODEL // 2), lambda i: (i + rows // tm, 0)),
                  pl.BlockSpec((tm, LANES), lambda i: (i, 0)),
                  pl.BlockSpec((1, SUBLANES, D_MODEL), lambda i: (i * tm // per_mod, 0, 0))],
        out_specs=pl.BlockSpec((tm, D_MODEL), lambda i: (i, 0)),
        compiler_params=_params(("parallel",), 32),
        name="moe_combine",
    )(x, yg, yg, route, mod)


def _moe(h2p, x, mod, route, counts, w1, w3, w2, tile):
    slot, n_slots, tile_expert, n_valid = _route_tables(route, counts, tile)
    hs = _sc_dispatch(h2p, slot, n_slots)
    ys = _moe_ffn(hs, tile_expert, n_valid, w1, w3, w2, tile)
    return _combine(x, _sc_gather(ys, slot), route, mod, tm=512)


W_IN_SEGMENTS = ((0, 1024), (1536, 1024), (4128, 1024), (5152, 1024), (6176, 1024), (7200, 3072),
                 (2560, 1024), (3584, 512), (1024, 256), (1280, 256), (4096, 32))


def _permute_kernel(w_ref, o_ref):
    dst = 0
    for src, n in W_IN_SEGMENTS:
        o_ref[dst:dst + n, :] = w_ref[src:src + n, :].astype(BF16)
        dst += n
    o_ref[dst:, :] = jnp.zeros((o_ref.shape[0] - dst, o_ref.shape[1]), BF16)


def _permute_w_in(w_in, l, cb=128):
    w_t = jnp.swapaxes(w_in, 1, 2)
    _, feats, d = w_t.shape
    return pl.pallas_call(
        _permute_kernel,
        out_shape=jax.ShapeDtypeStruct((N_PROJ, d), BF16),
        grid=(d // cb,),
        in_specs=[pl.BlockSpec((None, feats, cb), lambda i: (l, 0, i))],
        out_specs=pl.BlockSpec((N_PROJ, cb), lambda i: (0, i)),
        compiler_params=_params(("parallel",), 48),
        name="permute_w_in",
    )(w_t)


def _layer_params(l, w_in, q_norm_g, k_norm_g, w_attn_out, ssm_conv_w, ssm_conv_b, ssm_dt_bias, ssm_a_log,
                  ssm_d, ssm_norm_g, w_ssm_out, sconv_w, w_sconv_out, w_merge, norm_mix_g, norm_ffn_g,
                  ffn_w1, ffn_w3, ffn_w2, moe_router, moe_w1, moe_w3, moe_w2):
    w_perm = _permute_w_in(w_in, l)
    row = lambda v: v.reshape(1, -1)
    lane_row = lambda v, off: jnp.zeros((1, LANES), F32).at[0, off:off + SSM_HEADS].set(v)
    head_of_col = np.arange(D_MODEL) // SSM_HEADDIM
    expand = [jnp.asarray(np.tile(np.arange(LANES)[:, None] == head_of_col[None, :] + off, (3, 1)), BF16)
              for off in (0, SSM_HEADS)]
    r = np.arange(2 * SSM_CHUNK)[:, None]
    cidx = np.arange(SSM_CHUNK + 2 * HALO)[None, :]
    up = (r < SSM_CHUNK) & (((cidx == r - 1) & (r >= 1)) | ((r == 0) & (cidx == SSM_CHUNK + HALO - 1)))
    dn = (r >= SSM_CHUNK) & (((cidx == r - SSM_CHUNK + 1) & (r < 2 * SSM_CHUNK - 1))
                             | ((r == 2 * SSM_CHUNK - 1) & (cidx == SSM_CHUNK + HALO)))
    shift = jnp.asarray(up | dn, BF16)
    lp = {
        "w_in": w_perm,
        "norm_mix_g": row(norm_mix_g[l]), "norm_ffn_g": row(norm_ffn_g[l]),
        "q_norm_g": row(q_norm_g[l]), "k_norm_g": row(k_norm_g[l]),
        "w_attn_out": w_attn_out[l].astype(BF16), "w_ssm_out": w_ssm_out[l].astype(BF16),
        "w_sconv_out": w_sconv_out[l].astype(BF16), "w_merge": w_merge[l].astype(BF16),
        "cw_x": ssm_conv_w[l][:, :D_MODEL], "cw_bc": ssm_conv_w[l][:, D_MODEL:],
        "cb_x": row(ssm_conv_b[l][:D_MODEL]), "cb_bc": row(ssm_conv_b[l][D_MODEL:]),
        "alog": [lane_row(ssm_a_log[l, d], d * SSM_HEADS) for d in range(2)],
        "dtb": [lane_row(ssm_dt_bias[l, d], d * SSM_HEADS) for d in range(2)],
        "expand": expand,
        "shift": shift,
        "d_row": row(jnp.repeat(ssm_d[l], SSM_HEADDIM)),
        "ssm_norm_g": row(ssm_norm_g[l]),
        "sconv_w": sconv_w[l],
    }
    if l % 2 == 0:
        lp["router"] = None
        lp["ffn"] = tuple(t[l // 2].astype(BF16) for t in (ffn_w1, ffn_w3, ffn_w2))
    else:
        router = jnp.pad(moe_router[l // 2], ((0, 0), (0, LANES - N_EXPERTS)))
        r_hi = router.astype(BF16)
        lp["router"] = jnp.concatenate([r_hi, (router - r_hi.astype(F32)).astype(BF16)], axis=1)
        lp["ffn"] = _cast_bf16((moe_w1[l // 2], moe_w3[l // 2])) + _cast_bf16((moe_w2[l // 2],))
    return lp


def _rope_tables(seq):
    t = np.arange(seq)
    n = HEAD_DIM // 4
    inv = ROPE_THETA ** (-np.arange(n, dtype=np.float64) / n)
    ang_r = (t // GRID_W)[:, None] * inv[None, :]
    ang_c = (t % GRID_W)[:, None] * inv[None, :]
    zero = np.zeros_like(ang_r)
    cos = np.concatenate([np.cos(ang_r)] * 2 + [np.cos(ang_c)] * 2, axis=1)
    sa = np.concatenate([-np.sin(ang_r), zero, -np.sin(ang_c), zero], axis=1)
    sb = np.concatenate([zero, np.sin(ang_r), zero, np.sin(ang_c)], axis=1)
    return tuple(jnp.asarray(a, F32) for a in (cos, sa, sb))


def _trunk(x, mod, lp, nseq, seq, lat, layer=0, depth=1, carry=None):
    rows = nseq * seq
    proj = _inproj(x, mod, lp["norm_mix_g"], lp["w_in"], tm=min(2048, rows), tn=1152)
    if lat is None:
        attn, k_new, v_new = _attn_ctx(proj, lp["q_norm_g"], lp["k_norm_g"], nseq, seq, layer, depth,
                                       () if carry is None else carry[:2])
        *fwd, h_f = _ssd_pass(proj, lp, 0, nseq, seq, None, None, layer, depth, None if carry is None else carry[2])
        ssm, h_b = _ssd_pass(proj, lp, 1, nseq, seq, None, fwd, layer, depth, None if carry is None else carry[3])
    else:
        cache_k, cache_v, h0_f, h0_b, tables = lat
        qr, kall, vall = _prep_lat(proj, cache_k, cache_v, *tables, lp["q_norm_g"], lp["k_norm_g"], nseq, seq)
        attn = _attn_lat(qr, kall, vall, tq=256).reshape(rows, D_MODEL)
        k_new = v_new = None
        *fwd, h_f = _ssd_pass(proj, lp, 0, nseq, seq, h0_f)
        ssm, h_b = _ssd_pass(proj, lp, 1, nseq, seq, h0_b, fwd)
    outs = _merge(attn, ssm, proj, x, mod, lp, seq, tm=512)
    w1, w3, w2 = lp["ffn"]
    if lp["router"] is None:
        x = _ffn(outs[1], outs[0], mod, w1, w3, w2, tm=512)
    else:
        x = _moe(outs[1], outs[0], mod, outs[2], outs[3], w1, w3, w2, tile=256)
    return x, (k_new, v_new, h_f, h_b)


def kernel(x_prompt, x_sample, cache_k, cache_v, state_ssm_fwd, state_ssm_bwd, c, c_ctx, w_mod, b_mod, norm_mix_g, norm_ffn_g, w_in, q_norm_g, k_norm_g, w_attn_out, ssm_conv_w, ssm_conv_b, ssm_dt_bias, ssm_a_log, ssm_d, ssm_norm_g, w_ssm_out, sconv_w, w_sconv_out, w_merge, ffn_w1, ffn_w3, ffn_w2, moe_router, moe_w1, moe_w3, moe_w2):
    batch, seq, _ = x_prompt.shape
    nb, dseq, _ = x_sample.shape
    depth = w_in.shape[0]
    past = cache_k.shape[2]
    kvw = N_KV_HEADS * HEAD_DIM
    npair = SSM_HEADS // 2
    assert nb + 1 <= SUBLANES

    cond = jnp.zeros((SUBLANES, D_MODEL), F32).at[0].set(c_ctx).at[1:1 + nb].set(c)
    tables = _rope_tables(dseq)
    y_ctx = x_prompt.reshape(batch * seq, D_MODEL)
    y_lat = x_sample.reshape(nb * dseq, D_MODEL)
    mods = _modulation(cond, w_mod, b_mod.reshape(depth, 1, -1))
    carry = None
    for l in range(depth):
        lp = _layer_params(l, w_in, q_norm_g, k_norm_g, w_attn_out, ssm_conv_w, ssm_conv_b, ssm_dt_bias,
                           ssm_a_log, ssm_d, ssm_norm_g, w_ssm_out, sconv_w, w_sconv_out, w_merge,
                           norm_mix_g, norm_ffn_g, ffn_w1, ffn_w3, ffn_w2, moe_router, moe_w1, moe_w3, moe_w2)
        m = jnp.pad(mods[l].reshape(SUBLANES, 6, D_MODEL), ((0, 0), (0, SUBLANES - 6), (0, 0)))
        y_ctx, carry = _trunk(y_ctx, m[0:1], lp, batch, seq, None, l, depth, carry)
        lat = (cache_k[:, l].reshape(nb, past, kvw), cache_v[:, l].reshape(nb, past, kvw),
               state_ssm_fwd[:, l].reshape(nb, npair, SSM_CHUNK, SSM_STATE),
               state_ssm_bwd[:, l].reshape(nb, npair, SSM_CHUNK, SSM_STATE), tables)
        y_lat = _trunk(y_lat, m[1:1 + nb], lp, nb, dseq, lat)[0]
    k_all, v_all, hf_all, hb_all = carry
    state_shape = (batch, depth, SSM_HEADS, SSM_HEADDIM, SSM_STATE)
    return (y_ctx.reshape(batch, seq, D_MODEL), y_lat.reshape(nb, dseq, D_MODEL),
            k_all, v_all,
            hf_all.reshape(state_shape), hb_all.reshape(state_shape))
```

```python
import functools

import jax
import jax.numpy as jnp
import numpy as np
from jax import lax
from jax.experimental import pallas as pl
from jax.experimental.pallas import tpu as pltpu
from jax.experimental.pallas import tpu_sc as plsc

F32 = jnp.float32
BF16 = jnp.bfloat16
HIGHEST = lax.Precision.HIGHEST

D_MODEL = 1024
HEAD_DIM = 128
N_HEADS = 8
N_KV_HEADS = 2
Q_PER_KV = N_HEADS // N_KV_HEADS
ROPE_THETA = 10000.0
GRID_W = 64
SSM_HEADS = 16
SSM_HEADDIM = 64
SSM_STATE = 128
SSM_CHUNK = 128
D_FF = 2816
N_EXPERTS = 8
EPS = 1e-6
Q_SCALE = 1.4426950408889634 * HEAD_DIM ** -0.5

COL_Q, COL_Z, COL_SCB, COL_SCC, COL_SCH, COL_G = 0, 1024, 2048, 3072, 4096, 5120
COL_XS, COL_BC, COL_K, COL_V, COL_DT = 8192, 9216, 9728, 9984, 10240
N_PROJ = 10368
LANES = 128
SUBLANES = 8
HALO = 16
MIB = 1024 * 1024


def _params(sem, vmem_mib):
    return pltpu.CompilerParams(dimension_semantics=sem, vmem_limit_bytes=vmem_mib * MIB)


def _sigmoid(t):
    return 1.0 / (1.0 + jnp.exp(-t))


def _silu(t):
    return t * _sigmoid(t)


def _rms(t):
    return t * lax.rsqrt(jnp.mean(t * t, axis=-1, keepdims=True) + EPS)


def _bdot(a, b):
    return jnp.dot(a.astype(BF16), b.astype(BF16), preferred_element_type=F32)


def _hdot(a, b):
    return jnp.dot(a, b, precision=HIGHEST, preferred_element_type=F32)


def _split3(t):
    a = t.astype(BF16)
    r = t - a.astype(F32)
    b = r.astype(BF16)
    return a, b, (r - b.astype(F32)).astype(BF16)


def _mod_kernel(c_ref, w_ref, b_ref, o_ref):
    o_ref[...] = _hdot(_silu(c_ref[...]), w_ref[...]) + b_ref[...]


def _modulation(cond8, w, b):
    depth, _, width = w.shape
    return pl.pallas_call(
        _mod_kernel,
        out_shape=jax.ShapeDtypeStruct((depth, SUBLANES, width), F32),
        grid=(depth, width // D_MODEL),
        in_specs=[pl.BlockSpec((SUBLANES, D_MODEL), lambda l, j: (0, 0)),
                  pl.BlockSpec((None, D_MODEL, D_MODEL), lambda l, j: (l, 0, j)),
                  pl.BlockSpec((None, 1, D_MODEL), lambda l, j: (l, 0, j))],
        out_specs=pl.BlockSpec((None, SUBLANES, D_MODEL), lambda l, j: (l, 0, j)),
        compiler_params=_params(("parallel", "parallel"), 32),
        name="modulation",
    )(cond8, w, b)


def _inproj_kernel(x_ref, mod_ref, g_ref, w_ref, o_ref, h_scr):
    @pl.when(pl.program_id(1) == 0)
    def _():
        h = _rms(x_ref[...]) * g_ref[...]
        h = h * (1.0 + mod_ref[0, 1:2, :]) + mod_ref[0, 0:1, :]
        h_scr[...] = h.astype(BF16)

    o_ref[...] = _qk(h_scr[...], w_ref[...]).astype(o_ref.dtype)


def _inproj(x, mod, g, w, tm, tn):
    rows = x.shape[0]
    per_mod = rows // mod.shape[0]
    tm = min(tm, per_mod)
    return pl.pallas_call(
        _inproj_kernel,
        out_shape=jax.ShapeDtypeStruct((rows, N_PROJ), BF16),
        grid=(rows // tm, N_PROJ // tn),
        in_specs=[pl.BlockSpec((tm, D_MODEL), lambda i, j: (i, 0)),
                  pl.BlockSpec((1, SUBLANES, D_MODEL), lambda i, j: (i * tm // per_mod, 0, 0)),
                  pl.BlockSpec((1, D_MODEL), lambda i, j: (0, 0)),
                  pl.BlockSpec((tn, D_MODEL), lambda i, j: (j, 0))],
        out_specs=pl.BlockSpec((tm, tn), lambda i, j: (i, j)),
        scratch_shapes=[pltpu.VMEM((tm, D_MODEL), BF16)],
        compiler_params=_params(("parallel", "arbitrary"), 48),
        name="inproj",
    )(x, mod, g, w)


def _softmax_pv(s, vb):
    m = jnp.max(s, axis=-1, keepdims=True)
    p = jnp.exp2(s - m)
    l = jnp.sum(p, axis=-1, keepdims=True)
    return jnp.dot(p.astype(BF16), vb, preferred_element_type=F32) / l


def _qk(qb, kb):
    return lax.dot_general(qb, kb, (((1,), (1,)), ((), ())), preferred_element_type=F32)


def _attn_ctx_kernel(q_ref, k_ref, v_ref, qg_ref, kg_ref, *rest):
    a_ref, ko_ref, vo_ref = rest[-3:]
    v = v_ref[...].astype(F32)
    for hk in range(N_KV_HEADS):
        ks = slice(hk * HEAD_DIM, (hk + 1) * HEAD_DIM)
        kn = _rms(k_ref[:, ks].astype(F32)) * kg_ref[...]
        ko_ref[:, hk, :] = kn
        vo_ref[:, hk, :] = v[:, ks]
        knb = kn.astype(BF16)
        vb = v[:, ks].astype(BF16)
        for g in range(Q_PER_KV):
            hs = slice((hk * Q_PER_KV + g) * HEAD_DIM, (hk * Q_PER_KV + g + 1) * HEAD_DIM)
            qn = _rms(q_ref[:, hs].astype(F32)) * (qg_ref[...] * Q_SCALE)
            s = _qk(qn.astype(BF16), knb)
            a_ref[:, hs] = _softmax_pv(s, vb).astype(BF16)


def _attn_ctx(proj, qg, kg, nseq, seq, layer, depth, earlier):
    kvw = N_KV_HEADS * HEAD_DIM
    cache = jax.ShapeDtypeStruct((nseq, depth, seq, N_KV_HEADS, HEAD_DIM), F32)
    cache_spec = pl.BlockSpec((None, None, seq, N_KV_HEADS, HEAD_DIM), lambda b: (b, layer, 0, 0, 0))
    n_in = 5
    return pl.pallas_call(
        _attn_ctx_kernel,
        out_shape=(jax.ShapeDtypeStruct((nseq * seq, D_MODEL), BF16), cache, cache),
        grid=(nseq,),
        in_specs=[pl.BlockSpec((seq, D_MODEL), lambda b: (b, COL_Q // D_MODEL)),
                  pl.BlockSpec((seq, kvw), lambda b: (b, COL_K // kvw)),
                  pl.BlockSpec((seq, kvw), lambda b: (b, COL_V // kvw)),
                  pl.BlockSpec((1, HEAD_DIM), lambda b: (0, 0)),
                  pl.BlockSpec((1, HEAD_DIM), lambda b: (0, 0))] + [pl.BlockSpec(memory_space=pl.ANY)] * len(earlier),
        out_specs=(pl.BlockSpec((seq, D_MODEL), lambda b: (b, 0)), cache_spec, cache_spec),
        input_output_aliases={n_in + i: 1 + i for i in range(len(earlier))},
        compiler_params=_params(("parallel",), 32),
        name="attn_ctx",
    )(proj, proj, proj, qg, kg, *earlier)


def _rope(t, cos, sa, sb):
    return t * cos + pltpu.roll(t, 96, 1) * sa + pltpu.roll(t, 32, 1) * sb


def _prep_lat_kernel(k_ref, v_ref, ck_ref, cv_ref, cos_ref, sa_ref, sb_ref, kg_ref, ko_ref, vo_ref):
    t = pl.program_id(1)

    @pl.when(t == 0)
    def _():
        ko_ref[...] = ck_ref[...].astype(BF16)
        vo_ref[...] = cv_ref[...].astype(BF16)

    @pl.when(t > 0)
    def _():
        cos, sa, sb = cos_ref[...], sa_ref[...], sb_ref[...]
        vo_ref[...] = v_ref[...]
        for h in range(N_KV_HEADS):
            hs = slice(h * HEAD_DIM, (h + 1) * HEAD_DIM)
            ko_ref[:, hs] = _rope(_rms(k_ref[:, hs].astype(F32)) * kg_ref[...], cos, sa, sb).astype(BF16)


def _prep_lat(proj, cache_k, cache_v, cos, sa, sb, kg, nb, seq):
    past = cache_k.shape[1]
    tr = past
    kvw = N_KV_HEADS * HEAD_DIM
    nt = seq // tr
    prev = lambda t: jnp.maximum(t - 1, 0)
    return pl.pallas_call(
        _prep_lat_kernel,
        out_shape=(jax.ShapeDtypeStruct((nb, past + seq, kvw), BF16),
                   jax.ShapeDtypeStruct((nb, past + seq, kvw), BF16)),
        grid=(nb, nt + 1),
        in_specs=[pl.BlockSpec((tr, kvw), lambda b, t: (b * nt + prev(t), COL_K // kvw)),
                  pl.BlockSpec((tr, kvw), lambda b, t: (b * nt + prev(t), COL_V // kvw)),
                  pl.BlockSpec((None, past, kvw), lambda b, t: (b, 0, 0)),
                  pl.BlockSpec((None, past, kvw), lambda b, t: (b, 0, 0)),
                  pl.BlockSpec((tr, HEAD_DIM), lambda b, t: (prev(t), 0)),
                  pl.BlockSpec((tr, HEAD_DIM), lambda b, t: (prev(t), 0)),
                  pl.BlockSpec((tr, HEAD_DIM), lambda b, t: (prev(t), 0)),
                  pl.BlockSpec((1, HEAD_DIM), lambda b, t: (0, 0))],
        out_specs=(pl.BlockSpec((None, tr, kvw), lambda b, t: (b, t, 0)),
                   pl.BlockSpec((None, tr, kvw), lambda b, t: (b, t, 0))),
        compiler_params=_params(("parallel", "arbitrary"), 32),
        name="prep_lat",
    )(proj, proj, cache_k, cache_v, cos, sa, sb, kg)


def _attn_lat_kernel(q_ref, cos_ref, sa_ref, sb_ref, qg_ref, k_ref, v_ref, o_ref):
    cos, sa, sb = cos_ref[...], sa_ref[...], sb_ref[...]
    gain = qg_ref[...] * Q_SCALE
    for hk in range(N_KV_HEADS):
        kb = k_ref[:, hk * HEAD_DIM:(hk + 1) * HEAD_DIM]
        vb = v_ref[:, hk * HEAD_DIM:(hk + 1) * HEAD_DIM]
        for g in range(Q_PER_KV):
            hs = slice((hk * Q_PER_KV + g) * HEAD_DIM, (hk * Q_PER_KV + g + 1) * HEAD_DIM)
            qn = _rope(_rms(q_ref[:, hs].astype(F32)) * gain, cos, sa, sb)
            s = _qk(qn.astype(BF16), kb)
            o_ref[:, hs] = _softmax_pv(s, vb).astype(BF16)


def _attn_lat(proj, kall, vall, cos, sa, sb, qg, nb, seq, tq):
    nk, kvw = kall.shape[1:]
    nq = seq // tq
    table = pl.BlockSpec((tq, HEAD_DIM), lambda b, i: (i, 0))
    return pl.pallas_call(
        _attn_lat_kernel,
        out_shape=jax.ShapeDtypeStruct((nb * seq, D_MODEL), BF16),
        grid=(nb, nq),
        in_specs=[pl.BlockSpec((tq, D_MODEL), lambda b, i: (b * nq + i, COL_Q // D_MODEL)),
                  table, table, table, pl.BlockSpec((1, HEAD_DIM), lambda b, i: (0, 0)),
                  pl.BlockSpec((None, nk, kvw), lambda b, i: (b, 0, 0)),
                  pl.BlockSpec((None, nk, kvw), lambda b, i: (b, 0, 0))],
        out_specs=pl.BlockSpec((tq, D_MODEL), lambda b, i: (b * nq + i, 0)),
        compiler_params=_params(("parallel", "arbitrary"), 48),
        name="attn_lat",
    )(proj, cos, sa, sb, qg, kall, vall)


SSD_GROUP = 2


def _ssd_kernel(*refs, reverse, has_h0, nc):
    it = iter(refs)
    if reverse:
        xsc_ref, bcc_ref, dt_ref, alog_ref, dtb_ref, e_ref = (next(it) for _ in range(6))
    else:
        xs_ref, bc_ref, dt_ref, xsp_ref, xsn_ref, bcp_ref, bcn_ref = (next(it) for _ in range(7))
        cwx_ref, cbx_ref, cwb_ref, cbb_ref, alog_ref, dtb_ref, e_ref, shift_ref = (next(it) for _ in range(8))
    h0_ref = next(it) if has_h0 else None
    if reverse:
        z_ref, yf_ref, d_ref, g_ref = (next(it) for _ in range(4))
        y_ref, hout_ref, ht_scr = refs[-3:]
    else:
        y_ref, xsc_ref, bcc_ref, hout_ref, ht_scr = refs[-5:]

    q = SSM_CHUNK
    npair = SSM_HEADS // 2
    c = pl.program_id(1)
    cc = nc - 1 - c if reverse else c

    @pl.when(c == 0)
    def _():
        for s in range(SSD_GROUP):
            for p in range(npair):
                if has_h0:
                    ht_scr[s, p] = h0_ref[s, p].T
                else:
                    ht_scr[s, p] = jnp.zeros((q, q), F32)

    off = SSM_HEADS if reverse else 0
    ii = lax.broadcasted_iota(jnp.int32, (q, q), 0)
    jj = lax.broadcasted_iota(jnp.int32, (q, q), 1)
    tri = (jj >= ii) if reverse else (ii >= jj)
    tri_b = jnp.where(tri, 1.0, 0.0).astype(BF16)
    tri_tb3 = jnp.concatenate([jnp.where((ii >= jj) if reverse else (jj >= ii), 1.0, 0.0).astype(BF16)] * 3, axis=0)
    lo = lax.broadcasted_iota(jnp.int32, (q, LANES), 1) < SSM_HEADDIM
    neg_a = -jnp.exp(alog_ref[...])
    if not reverse:
        keep_prev = jnp.where(cc == 0, 0.0, 1.0).astype(BF16)
        keep_next = jnp.where(cc == nc - 1, 0.0, 1.0).astype(BF16)

    def conv_silu(s, x_ref_, p_ref_, n_ref_, w_ref_, b_ref_):
        x = x_ref_[s]
        ext = jnp.concatenate([x, p_ref_[s] * keep_prev, n_ref_[s] * keep_next], axis=0)
        nb = jnp.dot(shift_ref[...], ext, preferred_element_type=F32)
        w = w_ref_[...]
        return _silu(w[0:1, :] * nb[:q, :] + w[1:2, :] * x.astype(F32) + w[2:3, :] * nb[q:, :] + b_ref_[...])

    for s in range(SSD_GROUP):
        if reverse:
            xs = xsc_ref[s].astype(F32)
            bc = bcc_ref[s].astype(F32)
        else:
            xs = conv_silu(s, xs_ref, xsp_ref, xsn_ref, cwx_ref, cbx_ref)
            bc = conv_silu(s, bc_ref, bcp_ref, bcn_ref, cwb_ref, cbb_ref)
            xsc_ref[s] = xs.astype(BF16)
            bcc_ref[s] = bc.astype(BF16)

        dt = jax.nn.softplus(dt_ref[s].astype(F32) + dtb_ref[...])
        da = dt * neg_a
        acs3 = jnp.dot(tri_b, jnp.concatenate(_split3(da), axis=1), preferred_element_type=F32)
        acs = acs3[:, :LANES] + acs3[:, LANES:2 * LANES] + acs3[:, 2 * LANES:]
        acs_t = jnp.dot(jnp.concatenate(_split3(da.T), axis=1), tri_tb3, preferred_element_type=F32)
        spread = jnp.dot(jnp.concatenate(_split3(jnp.concatenate([dt, acs], axis=0)), axis=1), e_ref[...],
                         preferred_element_type=F32)
        acs_x = spread[q:, :]
        tot_x = acs_x[0:1, :] if reverse else acs_x[q - 1:q, :]
        xdt = xs * spread[:q, :]
        xw = xdt * jnp.exp(tot_x - acs_x)
        eacs_x = jnp.exp(acs_x)
        etot_x = jnp.exp(tot_x)

        parts = []
        for g in range(2):
            bg = bc[:, g * q:(g + 1) * q]
            cg = bc[:, (2 + g) * q:(3 + g) * q].astype(BF16)
            cb = _qk(cg, bg.astype(BF16))
            bgt = bg.T.astype(BF16)
            for pp in range(npair // 2):
                p = g * (npair // 2) + pp
                ps = slice(p * LANES, (p + 1) * LANES)
                ms = []
                for hh in range(2):
                    ln = off + 2 * p + hh
                    dec = jnp.where(tri, jnp.exp(acs[:, ln:ln + 1] - acs_t[ln:ln + 1, :]), 0.0)
                    ms.append((cb * dec).astype(BF16))
                m = jnp.concatenate(ms, axis=1)
                xp = xdt[:, ps]
                bd = jnp.concatenate([jnp.where(lo, xp, 0.0), jnp.where(lo, 0.0, xp)], axis=0).astype(BF16)
                ht = ht_scr[s, p]
                yp = jnp.dot(m, bd, preferred_element_type=F32)
                yp = yp + jnp.dot(cg, ht.astype(BF16), preferred_element_type=F32) * eacs_x[:, ps]
                ht_scr[s, p] = etot_x[:, ps] * ht + jnp.dot(bgt, xw[:, ps].astype(BF16),
                                                            preferred_element_type=F32)
                parts.append(yp)
        y = jnp.concatenate(parts, axis=1)

        if reverse:
            y = y + yf_ref[s] + d_ref[...] * xs
            y = y * _silu(z_ref[s].astype(F32))
            y_ref[s] = (_rms(y) * g_ref[...]).astype(y_ref.dtype)
        else:
            y_ref[s] = y

    @pl.when(c == nc - 1)
    def _():
        for s in range(SSD_GROUP):
            for p in range(npair):
                hout_ref[s, p] = ht_scr[s, p].T


def _ssd_pass(proj, lp, d, nseq, seq, h0, fwd=None, layer=0, depth=1, earlier=None):
    reverse = fwd is not None
    has_h0 = h0 is not None
    q = SSM_CHUNK
    nc = seq // q
    g = SSD_GROUP
    assert nseq % g == 0
    npair = SSM_HEADS // 2
    hb = q // HALO
    bcw = 4 * SSM_STATE
    chunk = (lambda c: nc - 1 - c) if reverse else (lambda c: c)
    prev8 = lambda c: jnp.maximum(chunk(c) * hb - 1, 0)
    next8 = lambda c: jnp.minimum((chunk(c) + 1) * hb, seq // HALO - 1)
    const = lambda s, c: (0, 0)
    tile = lambda width, col=0: pl.BlockSpec((g, q, width), lambda s, c: (s, chunk(c), col // width))
    halo = lambda width, col, where: pl.BlockSpec((g, HALO, width), lambda s, c: (s, where(c), col // width))
    seqs = lambda t: t.reshape(nseq, seq, t.shape[-1])
    scan_consts = [pl.BlockSpec((1, LANES), const), pl.BlockSpec((1, LANES), const),
                   pl.BlockSpec((3 * LANES, D_MODEL), const)]
    scan_args = [lp["alog"][d], lp["dtb"][d], lp["expand"][d]]
    proj3 = seqs(proj)
    if reverse:
        yf, xsc, bcc = fwd
        in_specs = [tile(D_MODEL), tile(bcw), tile(LANES, COL_DT)] + scan_consts
        args = [xsc, bcc, proj3] + scan_args
    else:
        in_specs = [tile(D_MODEL, COL_XS), tile(bcw, COL_BC), tile(LANES, COL_DT),
                    halo(D_MODEL, COL_XS, prev8), halo(D_MODEL, COL_XS, next8),
                    halo(bcw, COL_BC, prev8), halo(bcw, COL_BC, next8),
                    pl.BlockSpec((3, D_MODEL), const), pl.BlockSpec((1, D_MODEL), const),
                    pl.BlockSpec((3, bcw), const), pl.BlockSpec((1, bcw), const)] + scan_consts + [
                    pl.BlockSpec((2 * q, q + 2 * HALO), const)]
        args = [proj3] * 7 + [lp["cw_x"], lp["cb_x"], lp["cw_bc"], lp["cb_bc"]] + scan_args + [lp["shift"]]
    if has_h0:
        in_specs.append(pl.BlockSpec((g, npair, q, q), lambda s, c: (s, 0, 0, 0)))
        args.append(h0)
    if reverse:
        in_specs += [tile(D_MODEL, COL_Z), tile(D_MODEL), pl.BlockSpec((1, D_MODEL), const),
                     pl.BlockSpec((1, D_MODEL), const)]
        args += [proj3, yf, lp["d_row"], lp["ssm_norm_g"]]
    state = jax.ShapeDtypeStruct((nseq, depth, npair, q, q), F32)
    state_spec = pl.BlockSpec((g, None, npair, q, q), lambda s, c: (s, layer, 0, 0, 0))
    if reverse:
        out_shape = (jax.ShapeDtypeStruct((nseq, seq, D_MODEL), BF16), state)
        out_specs = (tile(D_MODEL), state_spec)
    else:
        out_shape = (jax.ShapeDtypeStruct((nseq, seq, D_MODEL), F32),
                     jax.ShapeDtypeStruct((nseq, seq, D_MODEL), BF16),
                     jax.ShapeDtypeStruct((nseq, seq, bcw), BF16), state)
        out_specs = (tile(D_MODEL), tile(D_MODEL), tile(bcw), state_spec)
    aliases = {}
    if earlier is not None:
        aliases = {len(args): len(out_shape) - 1}
        in_specs.append(pl.BlockSpec(memory_space=pl.ANY))
        args.append(earlier)
    outs = pl.pallas_call(
        functools.partial(_ssd_kernel, reverse=reverse, has_h0=has_h0, nc=nc),
        out_shape=out_shape,
        grid=(nseq // g, nc),
        in_specs=in_specs,
        out_specs=out_specs,
        input_output_aliases=aliases,
        scratch_shapes=[pltpu.VMEM((g, npair, q, q), F32)],
        compiler_params=_params(("parallel", "arbitrary"), 48),
        name="ssd_bwd" if reverse else "ssd_fwd",
    )(*args)
    if reverse:
        return outs[0].reshape(nseq * seq, D_MODEL), outs[1]
    return outs


ROUTE_I1, ROUTE_I2, ROUTE_P1, ROUTE_P2, ROUTE_R1, ROUTE_R2 = 0, 1, 2, 3, 4, 5


def _pack_bf16_pairs(v):
    n = v.shape[1] // 2
    bits = lax.bitcast_convert_type(v.astype(BF16).astype(F32), jnp.uint32)
    return lax.bitcast_convert_type((bits[:, :n] >> 16) | bits[:, n:], jnp.int32)


def _unpack_bf16_pairs(p):
    u = lax.bitcast_convert_type(p, jnp.uint32)
    lo = lax.bitcast_convert_type(u << 16, F32)
    hi = lax.bitcast_convert_type(u & jnp.uint32(0xFFFF0000), F32)
    return jnp.concatenate([lo, hi], axis=1)


def _top2_route(logits):
    tm = logits.shape[0]
    lane = lax.broadcasted_iota(jnp.int32, logits.shape, 1).astype(F32)
    lg = jnp.where(lane < N_EXPERTS, logits, -jnp.inf)
    m1 = jnp.max(lg, axis=-1, keepdims=True)
    i1 = jnp.min(jnp.where(lg == m1, lane, float(LANES)), axis=-1, keepdims=True)
    lg2 = jnp.where(lane == i1, -jnp.inf, lg)
    m2 = jnp.max(lg2, axis=-1, keepdims=True)
    i2 = jnp.min(jnp.where(lg2 == m2, lane, float(LANES)), axis=-1, keepdims=True)
    e = jnp.exp(m2 - m1)
    p1 = 1.0 / (1.0 + e)
    chosen = jnp.where(lane == i1, 1.0, jnp.where(lane == i2, 1.0, 0.0))
    earlier = (lax.broadcasted_iota(jnp.int32, (tm, tm), 0) > lax.broadcasted_iota(jnp.int32, (tm, tm), 1))
    ranks = _bdot(jnp.where(earlier, 1.0, 0.0), chosen)
    r1 = jnp.sum(jnp.where(lane == i1, ranks, 0.0), axis=-1, keepdims=True)
    r2 = jnp.sum(jnp.where(lane == i2, ranks, 0.0), axis=-1, keepdims=True)
    rec = jnp.where(lane == ROUTE_I1, i1, jnp.where(lane == ROUTE_I2, i2, 0.0))
    rec = jnp.where(lane == ROUTE_P1, p1, jnp.where(lane == ROUTE_P2, e * p1, rec))
    rec = jnp.where(lane == ROUTE_R1, r1, jnp.where(lane == ROUTE_R2, r2, rec))
    return rec, jnp.sum(chosen, axis=0, keepdims=True)


def _merge_kernel(*refs, seq, tm, moe):
    it = iter(refs)
    attn_ref, ssm_ref, scb_ref, scc_ref, sch_ref = (next(it) for _ in range(5))
    sccp_ref, sccn_ref, schp_ref, schn_ref = (next(it) for _ in range(4))
    g0_ref, g1_ref, g2_ref, x_ref, mod_ref = (next(it) for _ in range(5))
    wa_ref, ws_ref, wc_ref, wm_ref, scw_ref, ng_ref = (next(it) for _ in range(6))
    router_ref = next(it) if moe else None
    xo_ref, h2_ref = next(it), next(it)
    route_ref, cnt_ref = (next(it), next(it)) if moe else (None, None)

    i = pl.program_id(0)
    pos = (lax.broadcasted_iota(jnp.int32, (tm, 1), 0) + i * tm) & (seq - 1)
    u = scc_ref[...].astype(F32) * sch_ref[...].astype(F32)
    up = sccp_ref[HALO - 1:HALO, :].astype(F32) * schp_ref[HALO - 1:HALO, :].astype(F32)
    un = sccn_ref[0:1, :].astype(F32) * schn_ref[0:1, :].astype(F32)
    r = lax.broadcasted_iota(jnp.int32, (tm, 1), 0)
    um1 = jnp.where(pos == 0, 0.0, jnp.where(r == 0, up, pltpu.roll(u, 1, 0)))
    up1 = jnp.where(pos == seq - 1, 0.0, jnp.where(r == tm - 1, un, pltpu.roll(u, tm - 1, 0)))
    w = scw_ref[...]
    sc = scb_ref[...].astype(F32) * (w[0:1, :] * um1 + w[1:2, :] * u + w[2:3, :] * up1)

    gate = lambda g_ref: _sigmoid(g_ref[...].astype(F32))
    merged = gate(g0_ref) * jnp.dot(attn_ref[...], wa_ref[...], preferred_element_type=F32)
    merged = merged + gate(g1_ref) * jnp.dot(ssm_ref[...], ws_ref[...], preferred_element_type=F32)
    merged = merged + gate(g2_ref) * jnp.dot(sc.astype(BF16), wc_ref[...], preferred_element_type=F32)
    mod = mod_ref[0]
    xn = x_ref[...] + mod[2:3, :] * jnp.dot(merged.astype(BF16), wm_ref[...], preferred_element_type=F32)
    xo_ref[...] = xn
    h2 = _rms(xn) * ng_ref[...] * (1.0 + mod[4:5, :]) + mod[3:4, :]
    if moe:
        h2_ref[...] = _pack_bf16_pairs(h2)
        h_hi = h2.astype(BF16)
        h_lo = (h2 - h_hi.astype(F32)).astype(BF16)
        both = jnp.dot(h_hi, router_ref[...], preferred_element_type=F32)
        logits = both[:, :LANES] + both[:, LANES:] + jnp.dot(h_lo, router_ref[:, :LANES], preferred_element_type=F32)
        rec, counts = _top2_route(logits)
        route_ref[...] = rec
        cnt_ref[...] = jnp.broadcast_to(counts, cnt_ref.shape)
    else:
        h2_ref[...] = h2.astype(BF16)


def _merge(attn, ssm, proj, x, mod, lp, seq, tm):
    rows = x.shape[0]
    per_mod = rows // mod.shape[0]
    tm = min(tm, per_mod)
    moe = lp["router"] is not None
    hb = tm // HALO
    prev8 = lambda i: jnp.maximum(i * hb - 1, 0)
    next8 = lambda i: jnp.minimum((i + 1) * hb, rows // HALO - 1)
    tile = lambda col: pl.BlockSpec((tm, D_MODEL), lambda i: (i, col // D_MODEL))
    const = lambda shape: pl.BlockSpec(shape, lambda i: (0, 0))
    in_specs = [tile(0), tile(0), tile(COL_SCB), tile(COL_SCC), tile(COL_SCH),
                pl.BlockSpec((HALO, D_MODEL), lambda i: (prev8(i), COL_SCC // D_MODEL)),
                pl.BlockSpec((HALO, D_MODEL), lambda i: (next8(i), COL_SCC // D_MODEL)),
                pl.BlockSpec((HALO, D_MODEL), lambda i: (prev8(i), COL_SCH // D_MODEL)),
                pl.BlockSpec((HALO, D_MODEL), lambda i: (next8(i), COL_SCH // D_MODEL)),
                tile(COL_G), tile(COL_G + D_MODEL), tile(COL_G + 2 * D_MODEL), tile(0),
                pl.BlockSpec((1, SUBLANES, D_MODEL), lambda i: (i * tm // per_mod, 0, 0)),
                _resident((D_MODEL, D_MODEL)), _resident((D_MODEL, D_MODEL)), _resident((D_MODEL, D_MODEL)),
                _resident((D_MODEL, D_MODEL)), const((3, D_MODEL)), const((1, D_MODEL))]
    args = [attn, ssm, proj, proj, proj, proj, proj, proj, proj, proj, proj, proj, x, mod,
            lp["w_attn_out"], lp["w_ssm_out"], lp["w_sconv_out"], lp["w_merge"], lp["sconv_w"], lp["norm_ffn_g"]]
    out_shape = [jax.ShapeDtypeStruct((rows, D_MODEL), F32), jax.ShapeDtypeStruct((rows, D_MODEL), BF16)]
    out_specs = [tile(0), tile(0)]
    if moe:
        in_specs.append(_resident((D_MODEL, 2 * LANES)))
        args.append(lp["router"])
        out_shape[1] = jax.ShapeDtypeStruct((rows, D_MODEL // 2), jnp.int32)
        out_specs[1] = pl.BlockSpec((tm, D_MODEL // 2), lambda i: (i, 0))
        out_shape += [jax.ShapeDtypeStruct((rows, LANES), F32),
                      jax.ShapeDtypeStruct((rows // tm * SUBLANES, LANES), F32)]
        out_specs += [pl.BlockSpec((tm, LANES), lambda i: (i, 0)), pl.BlockSpec((SUBLANES, LANES), lambda i: (i, 0))]
    return pl.pallas_call(
        functools.partial(_merge_kernel, seq=seq, tm=tm, moe=moe),
        out_shape=tuple(out_shape),
        grid=(rows // tm,),
        in_specs=in_specs,
        out_specs=tuple(out_specs),
        compiler_params=_params(("parallel",), 56),
        name="merge",
    )(*args)


def _swiglu_acc(h, w1_ref, w3_ref, w2_ref, fs):
    a = _silu(jnp.dot(h, w1_ref[:, fs], preferred_element_type=F32)) * jnp.dot(h, w3_ref[:, fs],
                                                                                preferred_element_type=F32)
    return jnp.dot(a.astype(BF16), w2_ref[fs, :], preferred_element_type=F32)


def _swiglu(h, w1_ref, w3_ref, w2_ref):
    half = D_FF // 2
    return (_swiglu_acc(h, w1_ref, w3_ref, w2_ref, slice(0, half))
            + _swiglu_acc(h, w1_ref, w3_ref, w2_ref, slice(half, D_FF)))


def _ffn_kernel(h_ref, x_ref, mod_ref, w1_ref, w3_ref, w2_ref, o_ref):
    o_ref[...] = x_ref[...] + mod_ref[0, 5:6, :] * _swiglu(h_ref[...], w1_ref, w3_ref, w2_ref)


def _resident(shape):
    return pl.BlockSpec(shape, lambda *_: (0,) * len(shape), pipeline_mode=pl.Buffered(1))


def _ffn(h2, x, mod, w1, w3, w2, tm):
    rows = x.shape[0]
    per_mod = rows // mod.shape[0]
    tm = min(tm, per_mod)
    return pl.pallas_call(
        _ffn_kernel,
        out_shape=jax.ShapeDtypeStruct((rows, D_MODEL), F32),
        grid=(rows // tm,),
        in_specs=[pl.BlockSpec((tm, D_MODEL), lambda i: (i, 0)),
                  pl.BlockSpec((tm, D_MODEL), lambda i: (i, 0)),
                  pl.BlockSpec((1, SUBLANES, D_MODEL), lambda i: (i * tm // per_mod, 0, 0)),
                  _resident((D_MODEL, D_FF)), _resident((D_MODEL, D_FF)), _resident((D_FF, D_MODEL))],
        out_specs=pl.BlockSpec((tm, D_MODEL), lambda i: (i, 0)),
        compiler_params=_params(("parallel",), 56),
        name="ffn",
    )(h2, x, mod, w1, w3, w2)


def _cast_kernel(*refs):
    n = len(refs) // 2
    for src, dst in zip(refs[:n], refs[n:]):
        dst[...] = src[...].astype(dst.dtype)


def _cast_bf16(ws):
    e, r, c = ws[0].shape
    rb = max(d for d in range(16, r + 1, 16) if r % d == 0 and d * c * 4 <= 3 * MIB)
    spec = pl.BlockSpec((None, rb, c), lambda i, j: (i, j, 0))
    return pl.pallas_call(
        _cast_kernel,
        out_shape=tuple(jax.ShapeDtypeStruct(w.shape, BF16) for w in ws),
        grid=(e, r // rb),
        in_specs=[spec] * len(ws),
        out_specs=tuple([spec] * len(ws)),
        compiler_params=_params(("parallel", "parallel"), 48),
        name="cast_bf16",
    )(*ws)


SC_CORES = 2
SC_SUBCORES = 16
SC_CHUNK = 128


def _sc_gather(table, idx):
    nw = SC_CORES * SC_SUBCORES
    b, d = idx.shape[0], table.shape[1]
    per_w = b // nw
    assert per_w * nw == b and per_w % SC_CHUNK == 0
    mesh = plsc.VectorSubcoreMesh(core_axis_name="c", subcore_axis_name="s")

    def body(table_hbm, idx_hbm, out_hbm, idx_v, rows_v, sem):
        wid = lax.axis_index("s") * SC_CORES + lax.axis_index("c")

        @pl.loop(0, per_w // SC_CHUNK)
        def _(j):
            off = pl.multiple_of(wid * per_w + j * SC_CHUNK, SC_CHUNK)
            pltpu.sync_copy(idx_hbm.at[pl.ds(off, SC_CHUNK)], idx_v)
            pltpu.async_copy(table_hbm.at[idx_v], rows_v, sem).wait()
            pltpu.sync_copy(rows_v, out_hbm.at[pl.ds(off, SC_CHUNK)])

    return pl.kernel(
        body,
        out_type=jax.ShapeDtypeStruct((b, d), table.dtype),
        mesh=mesh,
        scratch_types=[pltpu.VMEM((SC_CHUNK,), jnp.int32), pltpu.VMEM((SC_CHUNK, d), table.dtype),
                       pltpu.SemaphoreType.DMA],
        name="sc_gather",
    )(table, idx)


def _sc_dispatch(table, slot, n_slots):
    nw = SC_CORES * SC_SUBCORES
    t, d = table.shape
    per_w = t // nw
    assert per_w * nw == t and per_w % SC_CHUNK == 0 and slot.shape[0] == 2 * t
    mesh = plsc.VectorSubcoreMesh(core_axis_name="c", subcore_axis_name="s")

    def body(table_hbm, slot_hbm, out_hbm, idx_v, rows_v, sem):
        wid = lax.axis_index("s") * SC_CORES + lax.axis_index("c")

        @pl.loop(0, per_w // SC_CHUNK)
        def _(j):
            off = pl.multiple_of(wid * per_w + j * SC_CHUNK, SC_CHUNK)
            pltpu.sync_copy(table_hbm.at[pl.ds(off, SC_CHUNK)], rows_v)
            for k in range(2):
                pltpu.sync_copy(slot_hbm.at[pl.ds(pl.multiple_of(k * t + off, SC_CHUNK), SC_CHUNK)], idx_v)
                pltpu.async_copy(rows_v, out_hbm.at[idx_v], sem).wait()

    return pl.kernel(
        body,
        out_type=jax.ShapeDtypeStruct((n_slots, d), table.dtype),
        mesh=mesh,
        scratch_types=[pltpu.VMEM((SC_CHUNK,), jnp.int32), pltpu.VMEM((SC_CHUNK, d), table.dtype),
                       pltpu.SemaphoreType.DMA],
        name="sc_dispatch",
    )(table, slot)


def _route_tables(route, counts, tile):
    t = route.shape[0]
    n_mt = counts.shape[0] // SUBLANES
    cnt = counts.reshape(n_mt, SUBLANES, LANES)[:, 0, :N_EXPERTS].astype(jnp.int32)
    incl = jnp.cumsum(cnt, axis=0)
    padded = (incl[-1] + tile - 1) // tile * tile
    gend = jnp.cumsum(padded)
    base = jnp.repeat((gend - padded)[None, :] + incl - cnt, t // n_mt, axis=0)
    e = route[:, ROUTE_I1:ROUTE_I2 + 1].astype(jnp.int32)
    r = route[:, ROUTE_R1:ROUTE_R2 + 1].astype(jnp.int32)
    pick = e[:, :, None] == jnp.arange(N_EXPERTS)[None, None, :]
    slot = jnp.sum(jnp.where(pick, base[:, None, :], 0), axis=-1) + r
    n_slots = 2 * t + N_EXPERTS * tile
    n_valid = gend[-1] // tile
    n_tiles = n_slots // tile
    tile_expert = jnp.sum(jnp.arange(n_tiles)[:, None] * tile >= gend[None, :], axis=1)
    last = jnp.take(tile_expert, jnp.maximum(n_valid - 1, 0))
    tile_expert = jnp.where(jnp.arange(n_tiles) < n_valid, tile_expert, last).astype(jnp.int32)
    return slot.T.reshape(-1).astype(jnp.int32), n_slots, tile_expert, n_valid.reshape(1).astype(jnp.int32)


def _moe_ffn_kernel(te_ref, nv_ref, h_ref, w1_ref, w3_ref, w2_ref, o_ref):
    i = pl.program_id(0)

    @pl.when(i < nv_ref[0])
    def _():
        h = _unpack_bf16_pairs(h_ref[...]).astype(BF16)
        o_ref[...] = _pack_bf16_pairs(_swiglu(h, w1_ref, w3_ref, w2_ref))

    @pl.when(i >= nv_ref[0])
    def _():
        o_ref[...] = jnp.zeros(o_ref.shape, jnp.int32)


def _moe_ffn(hs, tile_expert, n_valid, w1, w3, w2, tile):
    n_slots = hs.shape[0]
    grid_spec = pltpu.PrefetchScalarGridSpec(
        num_scalar_prefetch=2,
        grid=(n_slots // tile,),
        in_specs=[pl.BlockSpec((tile, D_MODEL // 2), lambda i, te, nv: (i, 0)),
                  pl.BlockSpec((None, D_MODEL, D_FF), lambda i, te, nv: (te[i], 0, 0)),
                  pl.BlockSpec((None, D_MODEL, D_FF), lambda i, te, nv: (te[i], 0, 0)),
                  pl.BlockSpec((None, D_FF, D_MODEL), lambda i, te, nv: (te[i], 0, 0))],
        out_specs=pl.BlockSpec((tile, D_MODEL // 2), lambda i, te, nv: (i, 0)),
    )
    return pl.pallas_call(
        _moe_ffn_kernel,
        out_shape=jax.ShapeDtypeStruct((n_slots, D_MODEL // 2), jnp.int32),
        grid_spec=grid_spec,
        compiler_params=_params(("arbitrary",), 56),
        name="moe_ffn",
    )(tile_expert, n_valid, hs, w1, w3, w2)


def _combine_kernel(x_ref, y1_ref, y2_ref, route_ref, mod_ref, o_ref):
    p1 = route_ref[:, ROUTE_P1:ROUTE_P1 + 1]
    p2 = route_ref[:, ROUTE_P2:ROUTE_P2 + 1]
    f = p1 * _unpack_bf16_pairs(y1_ref[...]) + p2 * _unpack_bf16_pairs(y2_ref[...])
    o_ref[...] = x_ref[...] + mod_ref[0, 5:6, :] * f


def _combine(x, yg, route, mod, tm):
    rows = x.shape[0]
    per_mod = rows // mod.shape[0]
    tm = min(tm, per_mod)
    return pl.pallas_call(
        _combine_kernel,
        out_shape=jax.ShapeDtypeStruct((rows, D_MODEL), F32),
        grid=(rows // tm,),
        in_specs=[pl.BlockSpec((tm, D_MODEL), lambda i: (i, 0)),
                  pl.BlockSpec((tm, D_MODEL // 2), lambda i: (i, 0)),
                  pl.BlockSpec((tm, D_MODEL // 2), lambda i: (i + rows // tm, 0)),
                  pl.BlockSpec((tm, LANES), lambda i: (i, 0)),
                  pl.BlockSpec((1, SUBLANES, D_MODEL), lambda i: (i * tm // per_mod, 0, 0))],
        out_specs=pl.BlockSpec((tm, D_MODEL), lambda i: (i, 0)),
        compiler_params=_params(("parallel",), 32),
        name="moe_combine",
    )(x, yg, yg, route, mod)


def _moe(h2p, x, mod, route, counts, w1, w3, w2, tile):
    slot, n_slots, tile_expert, n_valid = _route_tables(route, counts, tile)
    hs = _sc_dispatch(h2p, slot, n_slots)
    ys = _moe_ffn(hs, tile_expert, n_valid, w1, w3, w2, tile)
    return _combine(x, _sc_gather(ys, slot), route, mod, tm=512)


W_IN_SEGMENTS = ((0, 1024), (1536, 1024), (4128, 1024), (5152, 1024), (6176, 1024), (7200, 3072),
                 (2560, 1024), (3584, 512), (1024, 256), (1280, 256), (4096, 32))


def _permute_kernel(w_ref, o_ref):
    dst = 0
    for src, n in W_IN_SEGMENTS:
        o_ref[dst:dst + n, :] = w_ref[src:src + n, :].astype(BF16)
        dst += n
    o_ref[dst:, :] = jnp.zeros((o_ref.shape[0] - dst, o_ref.shape[1]), BF16)


def _permute_w_in(w_in, l, cb=128):
    w_t = jnp.swapaxes(w_in, 1, 2)
    _, feats, d = w_t.shape
    return pl.pallas_call(
        _permute_kernel,
        out_shape=jax.ShapeDtypeStruct((N_PROJ, d), BF16),
        grid=(d // cb,),
        in_specs=[pl.BlockSpec((None, feats, cb), lambda i: (l, 0, i))],
        out_specs=pl.BlockSpec((N_PROJ, cb), lambda i: (0, i)),
        compiler_params=_params(("parallel",), 48),
        name="permute_w_in",
    )(w_t)


def _layer_params(l, w_in, q_norm_g, k_norm_g, w_attn_out, ssm_conv_w, ssm_conv_b, ssm_dt_bias, ssm_a_log,
                  ssm_d, ssm_norm_g, w_ssm_out, sconv_w, w_sconv_out, w_merge, norm_mix_g, norm_ffn_g,
                  ffn_w1, ffn_w3, ffn_w2, moe_router, moe_w1, moe_w3, moe_w2):
    w_perm = _permute_w_in(w_in, l)
    row = lambda v: v.reshape(1, -1)
    lane_row = lambda v, off: jnp.zeros((1, LANES), F32).at[0, off:off + SSM_HEADS].set(v)
    head_of_col = np.arange(D_MODEL) // SSM_HEADDIM
    expand = [jnp.asarray(np.tile(np.arange(LANES)[:, None] == head_of_col[None, :] + off, (3, 1)), BF16)
              for off in (0, SSM_HEADS)]
    r = np.arange(2 * SSM_CHUNK)[:, None]
    cidx = np.arange(SSM_CHUNK + 2 * HALO)[None, :]
    up = (r < SSM_CHUNK) & (((cidx == r - 1) & (r >= 1)) | ((r == 0) & (cidx == SSM_CHUNK + HALO - 1)))
    dn = (r >= SSM_CHUNK) & (((cidx == r - SSM_CHUNK + 1) & (r < 2 * SSM_CHUNK - 1))
                             | ((r == 2 * SSM_CHUNK - 1) & (cidx == SSM_CHUNK + HALO)))
    shift = jnp.asarray(up | dn, BF16)
    lp = {
        "w_in": w_perm,
        "norm_mix_g": row(norm_mix_g[l]), "norm_ffn_g": row(norm_ffn_g[l]),
        "q_norm_g": row(q_norm_g[l]), "k_norm_g": row(k_norm_g[l]),
        "w_attn_out": w_attn_out[l].astype(BF16), "w_ssm_out": w_ssm_out[l].astype(BF16),
        "w_sconv_out": w_sconv_out[l].astype(BF16), "w_merge": w_merge[l].astype(BF16),
        "cw_x": ssm_conv_w[l][:, :D_MODEL], "cw_bc": ssm_conv_w[l][:, D_MODEL:],
        "cb_x": row(ssm_conv_b[l][:D_MODEL]), "cb_bc": row(ssm_conv_b[l][D_MODEL:]),
        "alog": [lane_row(ssm_a_log[l, d], d * SSM_HEADS) for d in range(2)],
        "dtb": [lane_row(ssm_dt_bias[l, d], d * SSM_HEADS) for d in range(2)],
        "expand": expand,
        "shift": shift,
        "d_row": row(jnp.repeat(ssm_d[l], SSM_HEADDIM)),
        "ssm_norm_g": row(ssm_norm_g[l]),
        "sconv_w": sconv_w[l],
    }
    if l % 2 == 0:
        lp["router"] = None
        lp["ffn"] = tuple(t[l // 2].astype(BF16) for t in (ffn_w1, ffn_w3, ffn_w2))
    else:
        router = jnp.pad(moe_router[l // 2], ((0, 0), (0, LANES - N_EXPERTS)))
        r_hi = router.astype(BF16)
        lp["router"] = jnp.concatenate([r_hi, (router - r_hi.astype(F32)).astype(BF16)], axis=1)
        lp["ffn"] = _cast_bf16((moe_w1[l // 2], moe_w3[l // 2])) + _cast_bf16((moe_w2[l // 2],))
    return lp


def _rope_tables(seq):
    t = np.arange(seq)
    n = HEAD_DIM // 4
    inv = ROPE_THETA ** (-np.arange(n, dtype=np.float64) / n)
    ang_r = (t // GRID_W)[:, None] * inv[None, :]
    ang_c = (t % GRID_W)[:, None] * inv[None, :]
    zero = np.zeros_like(ang_r)
    cos = np.concatenate([np.cos(ang_r)] * 2 + [np.cos(ang_c)] * 2, axis=1)
    sa = np.concatenate([-np.sin(ang_r), zero, -np.sin(ang_c), zero], axis=1)
    sb = np.concatenate([zero, np.sin(ang_r), zero, np.sin(ang_c)], axis=1)
    return tuple(jnp.asarray(a, F32) for a in (cos, sa, sb))


def _trunk(x, mod, lp, nseq, seq, lat, layer=0, depth=1, carry=None):
    rows = nseq * seq
    proj = _inproj(x, mod, lp["norm_mix_g"], lp["w_in"], tm=min(2048, rows), tn=1152)
    if lat is None:
        attn, k_new, v_new = _attn_ctx(proj, lp["q_norm_g"], lp["k_norm_g"], nseq, seq, layer, depth,
                                       () if carry is None else carry[:2])
        *fwd, h_f = _ssd_pass(proj, lp, 0, nseq, seq, None, None, layer, depth, None if carry is None else carry[2])
        ssm, h_b = _ssd_pass(proj, lp, 1, nseq, seq, None, fwd, layer, depth, None if carry is None else carry[3])
    else:
        cache_k, cache_v, h0_f, h0_b, tables = lat
        kall, vall = _prep_lat(proj, cache_k, cache_v, *tables, lp["k_norm_g"], nseq, seq)
        attn = _attn_lat(proj, kall, vall, *tables, lp["q_norm_g"], nseq, seq, tq=256)
        k_new = v_new = None
        *fwd, h_f = _ssd_pass(proj, lp, 0, nseq, seq, h0_f)
        ssm, h_b = _ssd_pass(proj, lp, 1, nseq, seq, h0_b, fwd)
    outs = _merge(attn, ssm, proj, x, mod, lp, seq, tm=512)
    w1, w3, w2 = lp["ffn"]
    if lp["router"] is None:
        x = _ffn(outs[1], outs[0], mod, w1, w3, w2, tm=512)
    else:
        x = _moe(outs[1], outs[0], mod, outs[2], outs[3], w1, w3, w2, tile=512)
    return x, (k_new, v_new, h_f, h_b)


def kernel(x_prompt, x_sample, cache_k, cache_v, state_ssm_fwd, state_ssm_bwd, c, c_ctx, w_mod, b_mod, norm_mix_g, norm_ffn_g, w_in, q_norm_g, k_norm_g, w_attn_out, ssm_conv_w, ssm_conv_b, ssm_dt_bias, ssm_a_log, ssm_d, ssm_norm_g, w_ssm_out, sconv_w, w_sconv_out, w_merge, ffn_w1, ffn_w3, ffn_w2, moe_router, moe_w1, moe_w3, moe_w2):
    batch, seq, _ = x_prompt.shape
    nb, dseq, _ = x_sample.shape
    depth = w_in.shape[0]
    past = cache_k.shape[2]
    kvw = N_KV_HEADS * HEAD_DIM
    npair = SSM_HEADS // 2
    assert nb + 1 <= SUBLANES

    cond = jnp.zeros((SUBLANES, D_MODEL), F32).at[0].set(c_ctx).at[1:1 + nb].set(c)
    tables = _rope_tables(dseq)
    y_ctx = x_prompt.reshape(batch * seq, D_MODEL)
    y_lat = x_sample.reshape(nb * dseq, D_MODEL)
    mods = _modulation(cond, w_mod, b_mod.reshape(depth, 1, -1))
    carry = None
    for l in range(depth):
        lp = _layer_params(l, w_in, q_norm_g, k_norm_g, w_attn_out, ssm_conv_w, ssm_conv_b, ssm_dt_bias,
                           ssm_a_log, ssm_d, ssm_norm_g, w_ssm_out, sconv_w, w_sconv_out, w_merge,
                           norm_mix_g, norm_ffn_g, ffn_w1, ffn_w3, ffn_w2, moe_router, moe_w1, moe_w3, moe_w2)
        m = jnp.pad(mods[l].reshape(SUBLANES, 6, D_MODEL), ((0, 0), (0, SUBLANES - 6), (0, 0)))
        y_ctx, carry = _trunk(y_ctx, m[0:1], lp, batch, seq, None, l, depth, carry)
        lat = (cache_k[:, l].reshape(nb, past, kvw), cache_v[:, l].reshape(nb, past, kvw),
               state_ssm_fwd[:, l].reshape(nb, npair, SSM_CHUNK, SSM_STATE),
               state_ssm_bwd[:, l].reshape(nb, npair, SSM_CHUNK, SSM_STATE), tables)
        y_lat = _trunk(y_lat, m[1:1 + nb], lp, nb, dseq, lat)[0]
    k_all, v_all, hf_all, hb_all = carry
    state_shape = (batch, depth, SSM_HEADS, SSM_HEADDIM, SSM_STATE)
    return (y_ctx.reshape(batch, seq, D_MODEL), y_lat.reshape(nb, dseq, D_MODEL),
            k_all, v_all,
            hf_all.reshape(state_shape), hb_all.reshape(state_shape))
```

```python
import functools

import jax
import jax.numpy as jnp
import numpy as np
from jax import lax
from jax.experimental import pallas as pl
from jax.experimental.pallas import tpu as pltpu
from jax.experimental.pallas import tpu_sc as plsc

F32 = jnp.float32
BF16 = jnp.bfloat16
HIGHEST = lax.Precision.HIGHEST

D_MODEL = 1024
HEAD_DIM = 128
N_HEADS = 8
N_KV_HEADS = 2
Q_PER_KV = N_HEADS // N_KV_HEADS
ROPE_THETA = 10000.0
GRID_W = 64
SSM_HEADS = 16
SSM_HEADDIM = 64
SSM_STATE = 128
SSM_CHUNK = 128
D_FF = 2816
N_EXPERTS = 8
EPS = 1e-6
Q_SCALE = 1.4426950408889634 * HEAD_DIM ** -0.5

COL_Q, COL_Z, COL_SCB, COL_SCC, COL_SCH, COL_G = 0, 1024, 2048, 3072, 4096, 5120
COL_XS, COL_BC, COL_K, COL_V, COL_DT = 8192, 9216, 9728, 9984, 10240
MXU_WIDTH = 256
N_PROJ = 10752
LANES = 128
SUBLANES = 8
HALO = 16
MIB = 1024 * 1024


def _params(sem, vmem_mib):
    return pltpu.CompilerParams(dimension_semantics=sem, vmem_limit_bytes=vmem_mib * MIB)


def _sigmoid(t):
    return 1.0 / (1.0 + jnp.exp(-t))


def _silu(t):
    return t * _sigmoid(t)


def _rms(t):
    return t * lax.rsqrt(jnp.mean(t * t, axis=-1, keepdims=True) + EPS)


def _bdot(a, b):
    return jnp.dot(a.astype(BF16), b.astype(BF16), preferred_element_type=F32)


def _hdot(a, b):
    return jnp.dot(a, b, precision=HIGHEST, preferred_element_type=F32)


def _split3(t):
    a = t.astype(BF16)
    r = t - a.astype(F32)
    b = r.astype(BF16)
    return a, b, (r - b.astype(F32)).astype(BF16)


def _mod_kernel(c_ref, w_ref, b_ref, o_ref):
    o_ref[...] = _hdot(_silu(c_ref[...]), w_ref[...]) + b_ref[...]


def _modulation(cond8, w, b):
    depth, _, width = w.shape
    return pl.pallas_call(
        _mod_kernel,
        out_shape=jax.ShapeDtypeStruct((depth, SUBLANES, width), F32),
        grid=(depth, width // D_MODEL),
        in_specs=[pl.BlockSpec((SUBLANES, D_MODEL), lambda l, j: (0, 0)),
                  pl.BlockSpec((None, D_MODEL, D_MODEL), lambda l, j: (l, 0, j)),
                  pl.BlockSpec((None, 1, D_MODEL), lambda l, j: (l, 0, j))],
        out_specs=pl.BlockSpec((None, SUBLANES, D_MODEL), lambda l, j: (l, 0, j)),
        compiler_params=_params(("parallel", "parallel"), 32),
        name="modulation",
    )(cond8, w, b)


def _inproj_kernel(x_ref, mod_ref, g_ref, w_ref, o_ref, h_scr):
    @pl.when(pl.program_id(1) == 0)
    def _():
        h = _rms(x_ref[...]) * g_ref[...]
        h = h * (1.0 + mod_ref[0, 1:2, :]) + mod_ref[0, 0:1, :]
        h_scr[...] = h.astype(BF16)

    o_ref[...] = _qk(h_scr[...], w_ref[...]).astype(o_ref.dtype)


def _inproj(x, mod, g, w, tm, tn):
    rows = x.shape[0]
    per_mod = rows // mod.shape[0]
    tm = min(tm, per_mod)
    return pl.pallas_call(
        _inproj_kernel,
        out_shape=jax.ShapeDtypeStruct((rows, N_PROJ), BF16),
        grid=(rows // tm, N_PROJ // tn),
        in_specs=[pl.BlockSpec((tm, D_MODEL), lambda i, j: (i, 0)),
                  pl.BlockSpec((1, SUBLANES, D_MODEL), lambda i, j: (i * tm // per_mod, 0, 0)),
                  pl.BlockSpec((1, D_MODEL), lambda i, j: (0, 0)),
                  pl.BlockSpec((tn, D_MODEL), lambda i, j: (j, 0))],
        out_specs=pl.BlockSpec((tm, tn), lambda i, j: (i, j)),
        scratch_shapes=[pltpu.VMEM((tm, D_MODEL), BF16)],
        compiler_params=_params(("parallel", "arbitrary"), 48),
        name="inproj",
    )(x, mod, g, w)


def _softmax_pv(s, vb):
    m = jnp.max(s, axis=-1, keepdims=True)
    p = jnp.exp2(s - m)
    l = jnp.sum(p, axis=-1, keepdims=True)
    return jnp.dot(p.astype(BF16), vb, preferred_element_type=F32) / l


def _qk(qb, kb):
    return lax.dot_general(qb, kb, (((1,), (1,)), ((), ())), preferred_element_type=F32)


def _attn_ctx_kernel(q_ref, k_ref, v_ref, qg_ref, kg_ref, *rest):
    a_ref, ko_ref, vo_ref = rest[-3:]
    v = v_ref[...].astype(F32)
    for hk in range(N_KV_HEADS):
        ks = slice(hk * HEAD_DIM, (hk + 1) * HEAD_DIM)
        kn = _rms(k_ref[:, ks].astype(F32)) * kg_ref[...]
        ko_ref[:, hk, :] = kn
        vo_ref[:, hk, :] = v[:, ks]
        knb = kn.astype(BF16)
        vb = v[:, ks].astype(BF16)
        for g in range(Q_PER_KV):
            hs = slice((hk * Q_PER_KV + g) * HEAD_DIM, (hk * Q_PER_KV + g + 1) * HEAD_DIM)
            qn = _rms(q_ref[:, hs].astype(F32)) * (qg_ref[...] * Q_SCALE)
            s = _qk(qn.astype(BF16), knb)
            a_ref[:, hs] = _softmax_pv(s, vb).astype(BF16)


def _attn_ctx(proj, qg, kg, nseq, seq, layer, depth, earlier):
    kvw = N_KV_HEADS * HEAD_DIM
    cache = jax.ShapeDtypeStruct((nseq, depth, seq, N_KV_HEADS, HEAD_DIM), F32)
    cache_spec = pl.BlockSpec((None, None, seq, N_KV_HEADS, HEAD_DIM), lambda b: (b, layer, 0, 0, 0))
    n_in = 5
    return pl.pallas_call(
        _attn_ctx_kernel,
        out_shape=(jax.ShapeDtypeStruct((nseq * seq, D_MODEL), BF16), cache, cache),
        grid=(nseq,),
        in_specs=[pl.BlockSpec((seq, D_MODEL), lambda b: (b, COL_Q // D_MODEL)),
                  pl.BlockSpec((seq, kvw), lambda b: (b, COL_K // kvw)),
                  pl.BlockSpec((seq, kvw), lambda b: (b, COL_V // kvw)),
                  pl.BlockSpec((1, HEAD_DIM), lambda b: (0, 0)),
                  pl.BlockSpec((1, HEAD_DIM), lambda b: (0, 0))] + [pl.BlockSpec(memory_space=pl.ANY)] * len(earlier),
        out_specs=(pl.BlockSpec((seq, D_MODEL), lambda b: (b, 0)), cache_spec, cache_spec),
        input_output_aliases={n_in + i: 1 + i for i in range(len(earlier))},
        compiler_params=_params(("parallel",), 32),
        name="attn_ctx",
    )(proj, proj, proj, qg, kg, *earlier)


def _rope(t, cos, sa, sb):
    return t * cos + pltpu.roll(t, 96, 1) * sa + pltpu.roll(t, 32, 1) * sb


def _prep_lat_kernel(k_ref, v_ref, ck_ref, cv_ref, cos_ref, sa_ref, sb_ref, kg_ref, ko_ref, vo_ref):
    t = pl.program_id(1)

    @pl.when(t == 0)
    def _():
        ko_ref[...] = ck_ref[...].astype(BF16)
        vo_ref[...] = cv_ref[...].astype(BF16)

    @pl.when(t > 0)
    def _():
        cos, sa, sb = cos_ref[...], sa_ref[...], sb_ref[...]
        vo_ref[...] = v_ref[...]
        for h in range(N_KV_HEADS):
            hs = slice(h * HEAD_DIM, (h + 1) * HEAD_DIM)
            ko_ref[:, hs] = _rope(_rms(k_ref[:, hs].astype(F32)) * kg_ref[...], cos, sa, sb).astype(BF16)


def _prep_lat(proj, cache_k, cache_v, cos, sa, sb, kg, nb, seq):
    past = cache_k.shape[1]
    tr = past
    kvw = N_KV_HEADS * HEAD_DIM
    nt = seq // tr
    prev = lambda t: jnp.maximum(t - 1, 0)
    return pl.pallas_call(
        _prep_lat_kernel,
        out_shape=(jax.ShapeDtypeStruct((nb, past + seq, kvw), BF16),
                   jax.ShapeDtypeStruct((nb, past + seq, kvw), BF16)),
        grid=(nb, nt + 1),
        in_specs=[pl.BlockSpec((tr, kvw), lambda b, t: (b * nt + prev(t), COL_K // kvw)),
                  pl.BlockSpec((tr, kvw), lambda b, t: (b * nt + prev(t), COL_V // kvw)),
                  pl.BlockSpec((None, past, kvw), lambda b, t: (b, 0, 0)),
                  pl.BlockSpec((None, past, kvw), lambda b, t: (b, 0, 0)),
                  pl.BlockSpec((tr, HEAD_DIM), lambda b, t: (prev(t), 0)),
                  pl.BlockSpec((tr, HEAD_DIM), lambda b, t: (prev(t), 0)),
                  pl.BlockSpec((tr, HEAD_DIM), lambda b, t: (prev(t), 0)),
                  pl.BlockSpec((1, HEAD_DIM), lambda b, t: (0, 0))],
        out_specs=(pl.BlockSpec((None, tr, kvw), lambda b, t: (b, t, 0)),
                   pl.BlockSpec((None, tr, kvw), lambda b, t: (b, t, 0))),
        compiler_params=_params(("parallel", "arbitrary"), 32),
        name="prep_lat",
    )(proj, proj, cache_k, cache_v, cos, sa, sb, kg)


def _attn_lat_kernel(q_ref, cos_ref, sa_ref, sb_ref, qg_ref, k_ref, v_ref, o_ref):
    cos, sa, sb = cos_ref[...], sa_ref[...], sb_ref[...]
    gain = qg_ref[...] * Q_SCALE
    for hk in range(N_KV_HEADS):
        kb = k_ref[:, hk * HEAD_DIM:(hk + 1) * HEAD_DIM]
        vb = v_ref[:, hk * HEAD_DIM:(hk + 1) * HEAD_DIM]
        for g in range(Q_PER_KV):
            hs = slice((hk * Q_PER_KV + g) * HEAD_DIM, (hk * Q_PER_KV + g + 1) * HEAD_DIM)
            qn = _rope(_rms(q_ref[:, hs].astype(F32)) * gain, cos, sa, sb)
            s = _qk(qn.astype(BF16), kb)
            o_ref[:, hs] = _softmax_pv(s, vb).astype(BF16)


def _attn_lat(proj, kall, vall, cos, sa, sb, qg, nb, seq, tq):
    nk, kvw = kall.shape[1:]
    nq = seq // tq
    table = pl.BlockSpec((tq, HEAD_DIM), lambda b, i: (i, 0))
    return pl.pallas_call(
        _attn_lat_kernel,
        out_shape=jax.ShapeDtypeStruct((nb * seq, D_MODEL), BF16),
        grid=(nb, nq),
        in_specs=[pl.BlockSpec((tq, D_MODEL), lambda b, i: (b * nq + i, COL_Q // D_MODEL)),
                  table, table, table, pl.BlockSpec((1, HEAD_DIM), lambda b, i: (0, 0)),
                  pl.BlockSpec((None, nk, kvw), lambda b, i: (b, 0, 0)),
                  pl.BlockSpec((None, nk, kvw), lambda b, i: (b, 0, 0))],
        out_specs=pl.BlockSpec((tq, D_MODEL), lambda b, i: (b * nq + i, 0)),
        compiler_params=_params(("parallel", "arbitrary"), 48),
        name="attn_lat",
    )(proj, cos, sa, sb, qg, kall, vall)


SSD_GROUP = 4


def _ssd_kernel(*refs, reverse, has_h0, nc):
    it = iter(refs)
    if reverse:
        xsc_ref, bcc_ref, dt_ref, alog_ref, dtb_ref, e_ref = (next(it) for _ in range(6))
    else:
        xs_ref, bc_ref, dt_ref, xsp_ref, xsn_ref, bcp_ref, bcn_ref = (next(it) for _ in range(7))
        cwx_ref, cbx_ref, cwb_ref, cbb_ref, alog_ref, dtb_ref, e_ref, shift_ref = (next(it) for _ in range(8))
    h0_ref = next(it) if has_h0 else None
    if reverse:
        z_ref, yf_ref, d_ref, g_ref = (next(it) for _ in range(4))
        y_ref, hout_ref, ht_scr = refs[-3:]
    else:
        y_ref, xsc_ref, bcc_ref, hout_ref, ht_scr = refs[-5:]

    q = SSM_CHUNK
    npair = SSM_HEADS // 2
    group = y_ref.shape[0]
    c = pl.program_id(1)
    cc = nc - 1 - c if reverse else c

    @pl.when(c == 0)
    def _():
        for s in range(group):
            for p in range(npair):
                if has_h0:
                    ht_scr[s, p] = h0_ref[s, p].T
                else:
                    ht_scr[s, p] = jnp.zeros((q, q), F32)

    off = SSM_HEADS if reverse else 0
    ii = lax.broadcasted_iota(jnp.int32, (q, q), 0)
    jj = lax.broadcasted_iota(jnp.int32, (q, q), 1)
    tri = (jj >= ii) if reverse else (ii >= jj)
    tri_b = jnp.where(tri, 1.0, 0.0).astype(BF16)
    tri_tb3 = jnp.concatenate([jnp.where((ii >= jj) if reverse else (jj >= ii), 1.0, 0.0).astype(BF16)] * 3, axis=0)
    lo = lax.broadcasted_iota(jnp.int32, (q, LANES), 1) < SSM_HEADDIM
    neg_a = -jnp.exp(alog_ref[...])
    if not reverse:
        keep_prev = jnp.where(cc == 0, 0.0, 1.0).astype(BF16)
        keep_next = jnp.where(cc == nc - 1, 0.0, 1.0).astype(BF16)

    def conv_silu(s, x_ref_, p_ref_, n_ref_, w_ref_, b_ref_):
        x = x_ref_[s]
        ext = jnp.concatenate([x, p_ref_[s] * keep_prev, n_ref_[s] * keep_next], axis=0)
        nb = jnp.dot(shift_ref[...], ext, preferred_element_type=F32)
        w = w_ref_[...]
        return _silu(w[0:1, :] * nb[:q, :] + w[1:2, :] * x.astype(F32) + w[2:3, :] * nb[q:, :] + b_ref_[...])

    for s in range(group):
        if reverse:
            xs = xsc_ref[s].astype(F32)
            bc = bcc_ref[s].astype(F32)
        else:
            xs = conv_silu(s, xs_ref, xsp_ref, xsn_ref, cwx_ref, cbx_ref)
            bc = conv_silu(s, bc_ref, bcp_ref, bcn_ref, cwb_ref, cbb_ref)
            xsc_ref[s] = xs.astype(BF16)
            bcc_ref[s] = bc.astype(BF16)

        dt = jax.nn.softplus(dt_ref[s].astype(F32) + dtb_ref[...])
        da = dt * neg_a
        acs3 = jnp.dot(tri_b, jnp.concatenate(_split3(da), axis=1), preferred_element_type=F32)
        acs = acs3[:, :LANES] + acs3[:, LANES:2 * LANES] + acs3[:, 2 * LANES:]
        acs_t = jnp.dot(jnp.concatenate(_split3(da.T), axis=1), tri_tb3, preferred_element_type=F32)
        spread = jnp.dot(jnp.concatenate(_split3(jnp.concatenate([dt, acs], axis=0)), axis=1), e_ref[...],
                         preferred_element_type=F32)
        acs_x = spread[q:, :]
        tot_x = acs_x[0:1, :] if reverse else acs_x[q - 1:q, :]
        xdt = xs * spread[:q, :]
        xw = xdt * jnp.exp(tot_x - acs_x)
        eacs_x = jnp.exp(acs_x)
        etot_x = jnp.exp(tot_x)

        parts = []
        for g in range(2):
            bg = bc[:, g * q:(g + 1) * q]
            cg = bc[:, (2 + g) * q:(3 + g) * q].astype(BF16)
            cb = _qk(cg, bg.astype(BF16))
            bgt = bg.T.astype(BF16)
            for pp in range(npair // 2):
                p = g * (npair // 2) + pp
                ps = slice(p * LANES, (p + 1) * LANES)
                ms = []
                for hh in range(2):
                    ln = off + 2 * p + hh
                    dec = jnp.where(tri, jnp.exp(acs[:, ln:ln + 1] - acs_t[ln:ln + 1, :]), 0.0)
                    ms.append((cb * dec).astype(BF16))
                m = jnp.concatenate(ms, axis=1)
                xp = xdt[:, ps]
                bd = jnp.concatenate([jnp.where(lo, xp, 0.0), jnp.where(lo, 0.0, xp)], axis=0).astype(BF16)
                ht = ht_scr[s, p]
                yp = jnp.dot(m, bd, preferred_element_type=F32)
                yp = yp + jnp.dot(cg, ht.astype(BF16), preferred_element_type=F32) * eacs_x[:, ps]
                ht_scr[s, p] = etot_x[:, ps] * ht + jnp.dot(bgt, xw[:, ps].astype(BF16),
                                                            preferred_element_type=F32)
                parts.append(yp)
        y = jnp.concatenate(parts, axis=1)

        if reverse:
            y = y + yf_ref[s] + d_ref[...] * xs
            y = y * _silu(z_ref[s].astype(F32))
            y_ref[s] = (_rms(y) * g_ref[...]).astype(y_ref.dtype)
        else:
            y_ref[s] = y

    @pl.when(c == nc - 1)
    def _():
        for s in range(group):
            for p in range(npair):
                hout_ref[s, p] = ht_scr[s, p].T


def _ssd_pass(proj, lp, d, nseq, seq, h0, fwd=None, layer=0, depth=1, earlier=None):
    reverse = fwd is not None
    has_h0 = h0 is not None
    q = SSM_CHUNK
    nc = seq // q
    g = min(SSD_GROUP, nseq)
    assert nseq % g == 0
    npair = SSM_HEADS // 2
    hb = q // HALO
    bcw = 4 * SSM_STATE
    chunk = (lambda c: nc - 1 - c) if reverse else (lambda c: c)
    prev8 = lambda c: jnp.maximum(chunk(c) * hb - 1, 0)
    next8 = lambda c: jnp.minimum((chunk(c) + 1) * hb, seq // HALO - 1)
    const = lambda s, c: (0, 0)
    tile = lambda width, col=0: pl.BlockSpec((g, q, width), lambda s, c: (s, chunk(c), col // width))
    halo = lambda width, col, where: pl.BlockSpec((g, HALO, width), lambda s, c: (s, where(c), col // width))
    seqs = lambda t: t.reshape(nseq, seq, t.shape[-1])
    scan_consts = [pl.BlockSpec((1, LANES), const), pl.BlockSpec((1, LANES), const),
                   pl.BlockSpec((3 * LANES, D_MODEL), const)]
    scan_args = [lp["alog"][d], lp["dtb"][d], lp["expand"][d]]
    proj3 = seqs(proj)
    if reverse:
        yf, xsc, bcc = fwd
        in_specs = [tile(D_MODEL), tile(bcw), tile(LANES, COL_DT)] + scan_consts
        args = [xsc, bcc, proj3] + scan_args
    else:
        in_specs = [tile(D_MODEL, COL_XS), tile(bcw, COL_BC), tile(LANES, COL_DT),
                    halo(D_MODEL, COL_XS, prev8), halo(D_MODEL, COL_XS, next8),
                    halo(bcw, COL_BC, prev8), halo(bcw, COL_BC, next8),
                    pl.BlockSpec((3, D_MODEL), const), pl.BlockSpec((1, D_MODEL), const),
                    pl.BlockSpec((3, bcw), const), pl.BlockSpec((1, bcw), const)] + scan_consts + [
                    pl.BlockSpec((2 * q, q + 2 * HALO), const)]
        args = [proj3] * 7 + [lp["cw_x"], lp["cb_x"], lp["cw_bc"], lp["cb_bc"]] + scan_args + [lp["shift"]]
    if has_h0:
        in_specs.append(pl.BlockSpec((g, npair, q, q), lambda s, c: (s, 0, 0, 0)))
        args.append(h0)
    if reverse:
        in_specs += [tile(D_MODEL, COL_Z), tile(D_MODEL), pl.BlockSpec((1, D_MODEL), const),
                     pl.BlockSpec((1, D_MODEL), const)]
        args += [proj3, yf, lp["d_row"], lp["ssm_norm_g"]]
    state = jax.ShapeDtypeStruct((nseq, depth, npair, q, q), F32)
    state_spec = pl.BlockSpec((g, None, npair, q, q), lambda s, c: (s, layer, 0, 0, 0))
    if reverse:
        out_shape = (jax.ShapeDtypeStruct((nseq, seq, D_MODEL), BF16), state)
        out_specs = (tile(D_MODEL), state_spec)
    else:
        out_shape = (jax.ShapeDtypeStruct((nseq, seq, D_MODEL), F32),
                     jax.ShapeDtypeStruct((nseq, seq, D_MODEL), BF16),
                     jax.ShapeDtypeStruct((nseq, seq, bcw), BF16), state)
        out_specs = (tile(D_MODEL), tile(D_MODEL), tile(bcw), state_spec)
    aliases = {}
    if earlier is not None:
        aliases = {len(args): len(out_shape) - 1}
        in_specs.append(pl.BlockSpec(memory_space=pl.ANY))
        args.append(earlier)
    outs = pl.pallas_call(
        functools.partial(_ssd_kernel, reverse=reverse, has_h0=has_h0, nc=nc),
        out_shape=out_shape,
        grid=(nseq // g, nc),
        in_specs=in_specs,
        out_specs=out_specs,
        input_output_aliases=aliases,
        scratch_shapes=[pltpu.VMEM((g, npair, q, q), F32)],
        compiler_params=_params(("parallel", "arbitrary"), 48),
        name="ssd_bwd" if reverse else "ssd_fwd",
    )(*args)
    if reverse:
        return outs[0].reshape(nseq * seq, D_MODEL), outs[1]
    return outs


ROUTE_I1, ROUTE_I2, ROUTE_P1, ROUTE_P2, ROUTE_R1, ROUTE_R2 = 0, 1, 2, 3, 4, 5


def _pack_bf16_pairs(v):
    n = v.shape[1] // 2
    bits = lax.bitcast_convert_type(v.astype(BF16).astype(F32), jnp.uint32)
    return lax.bitcast_convert_type((bits[:, :n] >> 16) | bits[:, n:], jnp.int32)


def _unpack_bf16_pairs(p):
    u = lax.bitcast_convert_type(p, jnp.uint32)
    lo = lax.bitcast_convert_type(u << 16, F32)
    hi = lax.bitcast_convert_type(u & jnp.uint32(0xFFFF0000), F32)
    return jnp.concatenate([lo, hi], axis=1)


def _top2_route(logits):
    tm = logits.shape[0]
    lane = lax.broadcasted_iota(jnp.int32, logits.shape, 1).astype(F32)
    lg = jnp.where(lane < N_EXPERTS, logits, -jnp.inf)
    m1 = jnp.max(lg, axis=-1, keepdims=True)
    i1 = jnp.min(jnp.where(lg == m1, lane, float(LANES)), axis=-1, keepdims=True)
    lg2 = jnp.where(lane == i1, -jnp.inf, lg)
    m2 = jnp.max(lg2, axis=-1, keepdims=True)
    i2 = jnp.min(jnp.where(lg2 == m2, lane, float(LANES)), axis=-1, keepdims=True)
    e = jnp.exp(m2 - m1)
    p1 = 1.0 / (1.0 + e)
    chosen = jnp.where(lane == i1, 1.0, jnp.where(lane == i2, 1.0, 0.0))
    earlier = (lax.broadcasted_iota(jnp.int32, (tm, tm), 0) > lax.broadcasted_iota(jnp.int32, (tm, tm), 1))
    ranks = _bdot(jnp.where(earlier, 1.0, 0.0), chosen)
    r1 = jnp.sum(jnp.where(lane == i1, ranks, 0.0), axis=-1, keepdims=True)
    r2 = jnp.sum(jnp.where(lane == i2, ranks, 0.0), axis=-1, keepdims=True)
    rec = jnp.where(lane == ROUTE_I1, i1, jnp.where(lane == ROUTE_I2, i2, 0.0))
    rec = jnp.where(lane == ROUTE_P1, p1, jnp.where(lane == ROUTE_P2, e * p1, rec))
    rec = jnp.where(lane == ROUTE_R1, r1, jnp.where(lane == ROUTE_R2, r2, rec))
    return rec, jnp.sum(chosen, axis=0, keepdims=True)


def _merge_kernel(*refs, seq, tm, moe):
    it = iter(refs)
    attn_ref, ssm_ref, scb_ref, scc_ref, sch_ref = (next(it) for _ in range(5))
    sccp_ref, sccn_ref, schp_ref, schn_ref = (next(it) for _ in range(4))
    g0_ref, g1_ref, g2_ref, x_ref, mod_ref = (next(it) for _ in range(5))
    wa_ref, ws_ref, wc_ref, wm_ref, scw_ref, ng_ref = (next(it) for _ in range(6))
    router_ref = next(it) if moe else None
    xo_ref, h2_ref = next(it), next(it)
    route_ref, cnt_ref = (next(it), next(it)) if moe else (None, None)

    i = pl.program_id(0)
    pos = (lax.broadcasted_iota(jnp.int32, (tm, 1), 0) + i * tm) & (seq - 1)
    u = scc_ref[...].astype(F32) * sch_ref[...].astype(F32)
    up = sccp_ref[HALO - 1:HALO, :].astype(F32) * schp_ref[HALO - 1:HALO, :].astype(F32)
    un = sccn_ref[0:1, :].astype(F32) * schn_ref[0:1, :].astype(F32)
    r = lax.broadcasted_iota(jnp.int32, (tm, 1), 0)
    um1 = jnp.where(pos == 0, 0.0, jnp.where(r == 0, up, pltpu.roll(u, 1, 0)))
    up1 = jnp.where(pos == seq - 1, 0.0, jnp.where(r == tm - 1, un, pltpu.roll(u, tm - 1, 0)))
    w = scw_ref[...]
    sc = scb_ref[...].astype(F32) * (w[0:1, :] * um1 + w[1:2, :] * u + w[2:3, :] * up1)

    gate = lambda g_ref: _sigmoid(g_ref[...].astype(F32))
    merged = gate(g0_ref) * jnp.dot(attn_ref[...], wa_ref[...], preferred_element_type=F32)
    merged = merged + gate(g1_ref) * jnp.dot(ssm_ref[...], ws_ref[...], preferred_element_type=F32)
    merged = merged + gate(g2_ref) * jnp.dot(sc.astype(BF16), wc_ref[...], preferred_element_type=F32)
    mod = mod_ref[0]
    xn = x_ref[...] + mod[2:3, :] * jnp.dot(merged.astype(BF16), wm_ref[...], preferred_element_type=F32)
    xo_ref[...] = xn
    h2 = _rms(xn) * ng_ref[...] * (1.0 + mod[4:5, :]) + mod[3:4, :]
    if moe:
        h2_ref[...] = _pack_bf16_pairs(h2)
        h_hi = h2.astype(BF16)
        h_lo = (h2 - h_hi.astype(F32)).astype(BF16)
        both = jnp.dot(h_hi, router_ref[...], preferred_element_type=F32)
        logits = both[:, :LANES] + both[:, LANES:] + jnp.dot(h_lo, router_ref[:, :LANES], preferred_element_type=F32)
        rec, counts = _top2_route(logits)
        route_ref[...] = rec
        cnt_ref[...] = jnp.broadcast_to(counts, cnt_ref.shape)
    else:
        h2_ref[...] = h2.astype(BF16)


def _merge(attn, ssm, proj, x, mod, lp, seq, tm):
    rows = x.shape[0]
    per_mod = rows // mod.shape[0]
    tm = min(tm, per_mod)
    moe = lp["router"] is not None
    hb = tm // HALO
    prev8 = lambda i: jnp.maximum(i * hb - 1, 0)
    next8 = lambda i: jnp.minimum((i + 1) * hb, rows // HALO - 1)
    tile = lambda col: pl.BlockSpec((tm, D_MODEL), lambda i: (i, col // D_MODEL))
    const = lambda shape: pl.BlockSpec(shape, lambda i: (0, 0))
    in_specs = [tile(0), tile(0), tile(COL_SCB), tile(COL_SCC), tile(COL_SCH),
                pl.BlockSpec((HALO, D_MODEL), lambda i: (prev8(i), COL_SCC // D_MODEL)),
                pl.BlockSpec((HALO, D_MODEL), lambda i: (next8(i), COL_SCC // D_MODEL)),
                pl.BlockSpec((HALO, D_MODEL), lambda i: (prev8(i), COL_SCH // D_MODEL)),
                pl.BlockSpec((HALO, D_MODEL), lambda i: (next8(i), COL_SCH // D_MODEL)),
                tile(COL_G), tile(COL_G + D_MODEL), tile(COL_G + 2 * D_MODEL), tile(0),
                pl.BlockSpec((1, SUBLANES, D_MODEL), lambda i: (i * tm // per_mod, 0, 0)),
                _resident((D_MODEL, D_MODEL)), _resident((D_MODEL, D_MODEL)), _resident((D_MODEL, D_MODEL)),
                _resident((D_MODEL, D_MODEL)), const((3, D_MODEL)), const((1, D_MODEL))]
    args = [attn, ssm, proj, proj, proj, proj, proj, proj, proj, proj, proj, proj, x, mod,
            lp["w_attn_out"], lp["w_ssm_out"], lp["w_sconv_out"], lp["w_merge"], lp["sconv_w"], lp["norm_ffn_g"]]
    out_shape = [jax.ShapeDtypeStruct((rows, D_MODEL), F32), jax.ShapeDtypeStruct((rows, D_MODEL), BF16)]
    out_specs = [tile(0), tile(0)]
    if moe:
        in_specs.append(_resident((D_MODEL, 2 * LANES)))
        args.append(lp["router"])
        out_shape[1] = jax.ShapeDtypeStruct((rows, D_MODEL // 2), jnp.int32)
        out_specs[1] = pl.BlockSpec((tm, D_MODEL // 2), lambda i: (i, 0))
        out_shape += [jax.ShapeDtypeStruct((rows, LANES), F32),
                      jax.ShapeDtypeStruct((rows // tm * SUBLANES, LANES), F32)]
        out_specs += [pl.BlockSpec((tm, LANES), lambda i: (i, 0)), pl.BlockSpec((SUBLANES, LANES), lambda i: (i, 0))]
    return pl.pallas_call(
        functools.partial(_merge_kernel, seq=seq, tm=tm, moe=moe),
        out_shape=tuple(out_shape),
        grid=(rows // tm,),
        in_specs=in_specs,
        out_specs=tuple(out_specs),
        compiler_params=_params(("parallel",), 56),
        name="merge",
    )(*args)


def _swiglu_acc(h, w1_ref, w3_ref, w2_ref, fs):
    a = _silu(jnp.dot(h, w1_ref[:, fs], preferred_element_type=F32)) * jnp.dot(h, w3_ref[:, fs],
                                                                                preferred_element_type=F32)
    return jnp.dot(a.astype(BF16), w2_ref[fs, :], preferred_element_type=F32)


def _swiglu(h, w1_ref, w3_ref, w2_ref):
    cut = (D_FF // 2 + MXU_WIDTH - 1) // MXU_WIDTH * MXU_WIDTH
    return (_swiglu_acc(h, w1_ref, w3_ref, w2_ref, slice(0, cut))
            + _swiglu_acc(h, w1_ref, w3_ref, w2_ref, slice(cut, D_FF)))


def _ffn_kernel(h_ref, x_ref, mod_ref, w1_ref, w3_ref, w2_ref, o_ref):
    o_ref[...] = x_ref[...] + mod_ref[0, 5:6, :] * _swiglu(h_ref[...], w1_ref, w3_ref, w2_ref)


def _resident(shape):
    return pl.BlockSpec(shape, lambda *_: (0,) * len(shape), pipeline_mode=pl.Buffered(1))


def _ffn(h2, x, mod, w1, w3, w2, tm):
    rows = x.shape[0]
    per_mod = rows // mod.shape[0]
    tm = min(tm, per_mod)
    return pl.pallas_call(
        _ffn_kernel,
        out_shape=jax.ShapeDtypeStruct((rows, D_MODEL), F32),
        grid=(rows // tm,),
        in_specs=[pl.BlockSpec((tm, D_MODEL), lambda i: (i, 0)),
                  pl.BlockSpec((tm, D_MODEL), lambda i: (i, 0)),
                  pl.BlockSpec((1, SUBLANES, D_MODEL), lambda i: (i * tm // per_mod, 0, 0)),
                  _resident((D_MODEL, D_FF)), _resident((D_MODEL, D_FF)), _resident((D_FF, D_MODEL))],
        out_specs=pl.BlockSpec((tm, D_MODEL), lambda i: (i, 0)),
        compiler_params=_params(("parallel",), 56),
        name="ffn",
    )(h2, x, mod, w1, w3, w2)


def _cast_kernel(*refs):
    n = len(refs) // 2
    for src, dst in zip(refs[:n], refs[n:]):
        dst[...] = src[...].astype(dst.dtype)


def _cast_bf16(ws):
    e, r, c = ws[0].shape
    rb = max(d for d in range(16, r + 1, 16) if r % d == 0 and d * c * 4 <= 3 * MIB)
    spec = pl.BlockSpec((None, rb, c), lambda i, j: (i, j, 0))
    return pl.pallas_call(
        _cast_kernel,
        out_shape=tuple(jax.ShapeDtypeStruct(w.shape, BF16) for w in ws),
        grid=(e, r // rb),
        in_specs=[spec] * len(ws),
        out_specs=tuple([spec] * len(ws)),
        compiler_params=_params(("parallel", "parallel"), 48),
        name="cast_bf16",
    )(*ws)


SC_CORES = 2
SC_SUBCORES = 16
SC_CHUNK = 128


def _sc_gather(table, idx):
    nw = SC_CORES * SC_SUBCORES
    b, d = idx.shape[0], table.shape[1]
    per_w = b // nw
    assert per_w * nw == b and per_w % SC_CHUNK == 0
    mesh = plsc.VectorSubcoreMesh(core_axis_name="c", subcore_axis_name="s")

    def body(table_hbm, idx_hbm, out_hbm, idx_v, rows_v, sem):
        wid = lax.axis_index("s") * SC_CORES + lax.axis_index("c")

        @pl.loop(0, per_w // SC_CHUNK)
        def _(j):
            off = pl.multiple_of(wid * per_w + j * SC_CHUNK, SC_CHUNK)
            pltpu.sync_copy(idx_hbm.at[pl.ds(off, SC_CHUNK)], idx_v)
            pltpu.async_copy(table_hbm.at[idx_v], rows_v, sem).wait()
            pltpu.sync_copy(rows_v, out_hbm.at[pl.ds(off, SC_CHUNK)])

    return pl.kernel(
        body,
        out_type=jax.ShapeDtypeStruct((b, d), table.dtype),
        mesh=mesh,
        scratch_types=[pltpu.VMEM((SC_CHUNK,), jnp.int32), pltpu.VMEM((SC_CHUNK, d), table.dtype),
                       pltpu.SemaphoreType.DMA],
        name="sc_gather",
    )(table, idx)


def _sc_dispatch(table, slot, n_slots):
    nw = SC_CORES * SC_SUBCORES
    t, d = table.shape
    per_w = t // nw
    assert per_w * nw == t and per_w % SC_CHUNK == 0 and slot.shape[0] == 2 * t
    mesh = plsc.VectorSubcoreMesh(core_axis_name="c", subcore_axis_name="s")

    def body(table_hbm, slot_hbm, out_hbm, idx_v, rows_v, sem):
        wid = lax.axis_index("s") * SC_CORES + lax.axis_index("c")

        @pl.loop(0, per_w // SC_CHUNK)
        def _(j):
            off = pl.multiple_of(wid * per_w + j * SC_CHUNK, SC_CHUNK)
            pltpu.sync_copy(table_hbm.at[pl.ds(off, SC_CHUNK)], rows_v)
            for k in range(2):
                pltpu.sync_copy(slot_hbm.at[pl.ds(pl.multiple_of(k * t + off, SC_CHUNK), SC_CHUNK)], idx_v)
                pltpu.async_copy(rows_v, out_hbm.at[idx_v], sem).wait()

    return pl.kernel(
        body,
        out_type=jax.ShapeDtypeStruct((n_slots, d), table.dtype),
        mesh=mesh,
        scratch_types=[pltpu.VMEM((SC_CHUNK,), jnp.int32), pltpu.VMEM((SC_CHUNK, d), table.dtype),
                       pltpu.SemaphoreType.DMA],
        name="sc_dispatch",
    )(table, slot)


def _route_tables(route, counts, tile):
    t = route.shape[0]
    n_mt = counts.shape[0] // SUBLANES
    cnt = counts.reshape(n_mt, SUBLANES, LANES)[:, 0, :N_EXPERTS].astype(jnp.int32)
    incl = jnp.cumsum(cnt, axis=0)
    padded = (incl[-1] + tile - 1) // tile * tile
    gend = jnp.cumsum(padded)
    base = jnp.repeat((gend - padded)[None, :] + incl - cnt, t // n_mt, axis=0)
    e = route[:, ROUTE_I1:ROUTE_I2 + 1].astype(jnp.int32)
    r = route[:, ROUTE_R1:ROUTE_R2 + 1].astype(jnp.int32)
    pick = e[:, :, None] == jnp.arange(N_EXPERTS)[None, None, :]
    slot = jnp.sum(jnp.where(pick, base[:, None, :], 0), axis=-1) + r
    n_slots = 2 * t + N_EXPERTS * tile
    n_valid = gend[-1] // tile
    n_tiles = n_slots // tile
    tile_expert = jnp.sum(jnp.arange(n_tiles)[:, None] * tile >= gend[None, :], axis=1)
    last = jnp.take(tile_expert, jnp.maximum(n_valid - 1, 0))
    tile_expert = jnp.where(jnp.arange(n_tiles) < n_valid, tile_expert, last).astype(jnp.int32)
    return slot.T.reshape(-1).astype(jnp.int32), n_slots, tile_expert, n_valid.reshape(1).astype(jnp.int32)


def _moe_ffn_kernel(te_ref, nv_ref, h_ref, w1_ref, w3_ref, w2_ref, o_ref):
    i = pl.program_id(0)

    @pl.when(i < nv_ref[0])
    def _():
        h = _unpack_bf16_pairs(h_ref[...]).astype(BF16)
        o_ref[...] = _pack_bf16_pairs(_swiglu(h, w1_ref, w3_ref, w2_ref))

    @pl.when(i >= nv_ref[0])
    def _():
        o_ref[...] = jnp.zeros(o_ref.shape, jnp.int32)


def _moe_ffn(hs, tile_expert, n_valid, w1, w3, w2, tile):
    n_slots = hs.shape[0]
    grid_spec = pltpu.PrefetchScalarGridSpec(
        num_scalar_prefetch=2,
        grid=(n_slots // tile,),
        in_specs=[pl.BlockSpec((tile, D_MODEL // 2), lambda i, te, nv: (i, 0)),
                  pl.BlockSpec((None, D_MODEL, D_FF), lambda i, te, nv: (te[i], 0, 0)),
                  pl.BlockSpec((None, D_MODEL, D_FF), lambda i, te, nv: (te[i], 0, 0)),
                  pl.BlockSpec((None, D_FF, D_MODEL), lambda i, te, nv: (te[i], 0, 0))],
        out_specs=pl.BlockSpec((tile, D_MODEL // 2), lambda i, te, nv: (i, 0)),
    )
    return pl.pallas_call(
        _moe_ffn_kernel,
        out_shape=jax.ShapeDtypeStruct((n_slots, D_MODEL // 2), jnp.int32),
        grid_spec=grid_spec,
        compiler_params=_params(("arbitrary",), 56),
        name="moe_ffn",
    )(tile_expert, n_valid, hs, w1, w3, w2)


def _combine_kernel(x_ref, y1_ref, y2_ref, route_ref, mod_ref, o_ref):
    p1 = route_ref[:, ROUTE_P1:ROUTE_P1 + 1]
    p2 = route_ref[:, ROUTE_P2:ROUTE_P2 + 1]
    f = p1 * _unpack_bf16_pairs(y1_ref[...]) + p2 * _unpack_bf16_pairs(y2_ref[...])
    o_ref[...] = x_ref[...] + mod_ref[0, 5:6, :] * f


def _combine(x, yg, route, mod, tm):
    rows = x.shape[0]
    per_mod = rows // mod.shape[0]
    tm = min(tm, per_mod)
    return pl.pallas_call(
        _combine_kernel,
        out_shape=jax.ShapeDtypeStruct((rows, D_MODEL), F32),
        grid=(rows // tm,),
        in_specs=[pl.BlockSpec((tm, D_MODEL), lambda i: (i, 0)),
                  pl.BlockSpec((tm, D_MODEL // 2), lambda i: (i, 0)),
                  pl.BlockSpec((tm, D_MODEL // 2), lambda i: (i + rows // tm, 0)),
                  pl.BlockSpec((tm, LANES), lambda i: (i, 0)),
                  pl.BlockSpec((1, SUBLANES, D_MODEL), lambda i: (i * tm // per_mod, 0, 0))],
        out_specs=pl.BlockSpec((tm, D_MODEL), lambda i: (i, 0)),
        compiler_params=_params(("parallel",), 32),
        name="moe_combine",
    )(x, yg, yg, route, mod)


def _moe(h2p, x, mod, route, counts, w1, w3, w2, tile):
    slot, n_slots, tile_expert, n_valid = _route_tables(route, counts, tile)
    hs = _sc_dispatch(h2p, slot, n_slots)
    ys = _moe_ffn(hs, tile_expert, n_valid, w1, w3, w2, tile)
    return _combine(x, _sc_gather(ys, slot), route, mod, tm=512)


W_IN_SEGMENTS = ((0, 1024), (1536, 1024), (4128, 1024), (5152, 1024), (6176, 1024), (7200, 3072),
                 (2560, 1024), (3584, 512), (1024, 256), (1280, 256), (4096, 32))


def _permute_kernel(w_ref, o_ref):
    dst = 0
    for src, n in W_IN_SEGMENTS:
        o_ref[dst:dst + n, :] = w_ref[src:src + n, :].astype(BF16)
        dst += n
    o_ref[dst:, :] = jnp.zeros((o_ref.shape[0] - dst, o_ref.shape[1]), BF16)


def _permute_w_in(w_in, l, cb=128):
    w_t = jnp.swapaxes(w_in, 1, 2)
    _, feats, d = w_t.shape
    return pl.pallas_call(
        _permute_kernel,
        out_shape=jax.ShapeDtypeStruct((N_PROJ, d), BF16),
        grid=(d // cb,),
        in_specs=[pl.BlockSpec((None, feats, cb), lambda i: (l, 0, i))],
        out_specs=pl.BlockSpec((N_PROJ, cb), lambda i: (0, i)),
        compiler_params=_params(("parallel",), 48),
        name="permute_w_in",
    )(w_t)


def _layer_params(l, w_in, q_norm_g, k_norm_g, w_attn_out, ssm_conv_w, ssm_conv_b, ssm_dt_bias, ssm_a_log,
                  ssm_d, ssm_norm_g, w_ssm_out, sconv_w, w_sconv_out, w_merge, norm_mix_g, norm_ffn_g,
                  ffn_w1, ffn_w3, ffn_w2, moe_router, moe_w1, moe_w3, moe_w2):
    w_perm = _permute_w_in(w_in, l)
    row = lambda v: v.reshape(1, -1)
    lane_row = lambda v, off: jnp.zeros((1, LANES), F32).at[0, off:off + SSM_HEADS].set(v)
    head_of_col = np.arange(D_MODEL) // SSM_HEADDIM
    expand = [jnp.asarray(np.tile(np.arange(LANES)[:, None] == head_of_col[None, :] + off, (3, 1)), BF16)
              for off in (0, SSM_HEADS)]
    r = np.arange(2 * SSM_CHUNK)[:, None]
    cidx = np.arange(SSM_CHUNK + 2 * HALO)[None, :]
    up = (r < SSM_CHUNK) & (((cidx == r - 1) & (r >= 1)) | ((r == 0) & (cidx == SSM_CHUNK + HALO - 1)))
    dn = (r >= SSM_CHUNK) & (((cidx == r - SSM_CHUNK + 1) & (r < 2 * SSM_CHUNK - 1))
                             | ((r == 2 * SSM_CHUNK - 1) & (cidx == SSM_CHUNK + HALO)))
    shift = jnp.asarray(up | dn, BF16)
    lp = {
        "w_in": w_perm,
        "norm_mix_g": row(norm_mix_g[l]), "norm_ffn_g": row(norm_ffn_g[l]),
        "q_norm_g": row(q_norm_g[l]), "k_norm_g": row(k_norm_g[l]),
        "w_attn_out": w_attn_out[l].astype(BF16), "w_ssm_out": w_ssm_out[l].astype(BF16),
        "w_sconv_out": w_sconv_out[l].astype(BF16), "w_merge": w_merge[l].astype(BF16),
        "cw_x": ssm_conv_w[l][:, :D_MODEL], "cw_bc": ssm_conv_w[l][:, D_MODEL:],
        "cb_x": row(ssm_conv_b[l][:D_MODEL]), "cb_bc": row(ssm_conv_b[l][D_MODEL:]),
        "alog": [lane_row(ssm_a_log[l, d], d * SSM_HEADS) for d in range(2)],
        "dtb": [lane_row(ssm_dt_bias[l, d], d * SSM_HEADS) for d in range(2)],
        "expand": expand,
        "shift": shift,
        "d_row": row(jnp.repeat(ssm_d[l], SSM_HEADDIM)),
        "ssm_norm_g": row(ssm_norm_g[l]),
        "sconv_w": sconv_w[l],
    }
    if l % 2 == 0:
        lp["router"] = None
        lp["ffn"] = tuple(t[l // 2].astype(BF16) for t in (ffn_w1, ffn_w3, ffn_w2))
    else:
        router = jnp.pad(moe_router[l // 2], ((0, 0), (0, LANES - N_EXPERTS)))
        r_hi = router.astype(BF16)
        lp["router"] = jnp.concatenate([r_hi, (router - r_hi.astype(F32)).astype(BF16)], axis=1)
        lp["ffn"] = _cast_bf16((moe_w1[l // 2], moe_w3[l // 2])) + _cast_bf16((moe_w2[l // 2],))
    return lp


def _rope_tables(seq):
    t = np.arange(seq)
    n = HEAD_DIM // 4
    inv = ROPE_THETA ** (-np.arange(n, dtype=np.float64) / n)
    ang_r = (t // GRID_W)[:, None] * inv[None, :]
    ang_c = (t % GRID_W)[:, None] * inv[None, :]
    zero = np.zeros_like(ang_r)
    cos = np.concatenate([np.cos(ang_r)] * 2 + [np.cos(ang_c)] * 2, axis=1)
    sa = np.concatenate([-np.sin(ang_r), zero, -np.sin(ang_c), zero], axis=1)
    sb = np.concatenate([zero, np.sin(ang_r), zero, np.sin(ang_c)], axis=1)
    return tuple(jnp.asarray(a, F32) for a in (cos, sa, sb))


def _trunk(x, mod, lp, nseq, seq, lat, layer=0, depth=1, carry=None):
    rows = nseq * seq
    proj = _inproj(x, mod, lp["norm_mix_g"], lp["w_in"], tm=min(2048, rows), tn=6 * MXU_WIDTH)
    if lat is None:
        attn, k_new, v_new = _attn_ctx(proj, lp["q_norm_g"], lp["k_norm_g"], nseq, seq, layer, depth,
                                       () if carry is None else carry[:2])
        *fwd, h_f = _ssd_pass(proj, lp, 0, nseq, seq, None, None, layer, depth, None if carry is None else carry[2])
        ssm, h_b = _ssd_pass(proj, lp, 1, nseq, seq, None, fwd, layer, depth, None if carry is None else carry[3])
    else:
        cache_k, cache_v, h0_f, h0_b, tables = lat
        kall, vall = _prep_lat(proj, cache_k, cache_v, *tables, lp["k_norm_g"], nseq, seq)
        attn = _attn_lat(proj, kall, vall, *tables, lp["q_norm_g"], nseq, seq, tq=256)
        k_new = v_new = None
        *fwd, h_f = _ssd_pass(proj, lp, 0, nseq, seq, h0_f)
        ssm, h_b = _ssd_pass(proj, lp, 1, nseq, seq, h0_b, fwd)
    outs = _merge(attn, ssm, proj, x, mod, lp, seq, tm=512)
    w1, w3, w2 = lp["ffn"]
    if lp["router"] is None:
        x = _ffn(outs[1], outs[0], mod, w1, w3, w2, tm=512)
    else:
        x = _moe(outs[1], outs[0], mod, outs[2], outs[3], w1, w3, w2, tile=512)
    return x, (k_new, v_new, h_f, h_b)


def kernel(x_prompt, x_sample, cache_k, cache_v, state_ssm_fwd, state_ssm_bwd, c, c_ctx, w_mod, b_mod, norm_mix_g, norm_ffn_g, w_in, q_norm_g, k_norm_g, w_attn_out, ssm_conv_w, ssm_conv_b, ssm_dt_bias, ssm_a_log, ssm_d, ssm_norm_g, w_ssm_out, sconv_w, w_sconv_out, w_merge, ffn_w1, ffn_w3, ffn_w2, moe_router, moe_w1, moe_w3, moe_w2):
    batch, seq, _ = x_prompt.shape
    nb, dseq, _ = x_sample.shape
    depth = w_in.shape[0]
    past = cache_k.shape[2]
    kvw = N_KV_HEADS * HEAD_DIM
    npair = SSM_HEADS // 2
    assert nb + 1 <= SUBLANES

    cond = jnp.zeros((SUBLANES, D_MODEL), F32).at[0].set(c_ctx).at[1:1 + nb].set(c)
    tables = _rope_tables(dseq)
    y_ctx = x_prompt.reshape(batch * seq, D_MODEL)
    y_lat = x_sample.reshape(nb * dseq, D_MODEL)
    mods = _modulation(cond, w_mod, b_mod.reshape(depth, 1, -1))
    carry = None
    for l in range(depth):
        lp = _layer_params(l, w_in, q_norm_g, k_norm_g, w_attn_out, ssm_conv_w, ssm_conv_b, ssm_dt_bias,
                           ssm_a_log, ssm_d, ssm_norm_g, w_ssm_out, sconv_w, w_sconv_out, w_merge,
                           norm_mix_g, norm_ffn_g, ffn_w1, ffn_w3, ffn_w2, moe_router, moe_w1, moe_w3, moe_w2)
        m = jnp.pad(mods[l].reshape(SUBLANES, 6, D_MODEL), ((0, 0), (0, SUBLANES - 6), (0, 0)))
        y_ctx, carry = _trunk(y_ctx, m[0:1], lp, batch, seq, None, l, depth, carry)
        lat = (cache_k[:, l].reshape(nb, past, kvw), cache_v[:, l].reshape(nb, past, kvw),
               state_ssm_fwd[:, l].reshape(nb, npair, SSM_CHUNK, SSM_STATE),
               state_ssm_bwd[:, l].reshape(nb, npair, SSM_CHUNK, SSM_STATE), tables)
        y_lat = _trunk(y_lat, m[1:1 + nb], lp, nb, dseq, lat)[0]
    k_all, v_all, hf_all, hb_all = carry
    state_shape = (batch, depth, SSM_HEADS, SSM_HEADDIM, SSM_STATE)
    return (y_ctx.reshape(batch, seq, D_MODEL), y_lat.reshape(nb, dseq, D_MODEL),
            k_all, v_all,
            hf_all.reshape(state_shape), hb_all.reshape(state_shape))
```

```python
import functools

import jax
import jax.numpy as jnp
import numpy as np
from jax import lax
from jax.experimental import pallas as pl
from jax.experimental.pallas import tpu as pltpu
from jax.experimental.pallas import tpu_sc as plsc

F32 = jnp.float32
BF16 = jnp.bfloat16
HIGHEST = lax.Precision.HIGHEST

D_MODEL = 1024
HEAD_DIM = 128
N_HEADS = 8
N_KV_HEADS = 2
Q_PER_KV = N_HEADS // N_KV_HEADS
ROPE_THETA = 10000.0
GRID_W = 64
SSM_HEADS = 16
SSM_HEADDIM = 64
SSM_STATE = 128
SSM_CHUNK = 128
D_FF = 2816
N_EXPERTS = 8
EPS = 1e-6
Q_SCALE = 1.4426950408889634 * HEAD_DIM ** -0.5

COL_Q, COL_Z, COL_SCB, COL_SCC, COL_SCH, COL_G = 0, 1024, 2048, 3072, 4096, 5120
COL_XS, COL_BC, COL_K, COL_V, COL_DT = 8192, 9216, 9728, 9984, 10240
MXU_WIDTH = 256
N_PROJ = 10752
LANES = 128
SUBLANES = 8
HALO = 16
MIB = 1024 * 1024


def _params(sem, vmem_mib):
    return pltpu.CompilerParams(dimension_semantics=sem, vmem_limit_bytes=vmem_mib * MIB)


def _sigmoid(t):
    return 1.0 / (1.0 + jnp.exp(-t))


def _silu(t):
    return t * _sigmoid(t)


def _rms(t):
    return t * lax.rsqrt(jnp.mean(t * t, axis=-1, keepdims=True) + EPS)


def _bdot(a, b):
    return jnp.dot(a.astype(BF16), b.astype(BF16), preferred_element_type=F32)


def _hdot(a, b):
    return jnp.dot(a, b, precision=HIGHEST, preferred_element_type=F32)


def _split3(t):
    a = t.astype(BF16)
    r = t - a.astype(F32)
    b = r.astype(BF16)
    return a, b, (r - b.astype(F32)).astype(BF16)


def _mod_kernel(c_ref, w_ref, b_ref, o_ref):
    o_ref[...] = _hdot(_silu(c_ref[...]), w_ref[...]) + b_ref[...]


def _modulation(cond8, w, b):
    depth, _, width = w.shape
    return pl.pallas_call(
        _mod_kernel,
        out_shape=jax.ShapeDtypeStruct((depth, SUBLANES, width), F32),
        grid=(depth, width // D_MODEL),
        in_specs=[pl.BlockSpec((SUBLANES, D_MODEL), lambda l, j: (0, 0)),
                  pl.BlockSpec((None, D_MODEL, D_MODEL), lambda l, j: (l, 0, j)),
                  pl.BlockSpec((None, 1, D_MODEL), lambda l, j: (l, 0, j))],
        out_specs=pl.BlockSpec((None, SUBLANES, D_MODEL), lambda l, j: (l, 0, j)),
        compiler_params=_params(("parallel", "parallel"), 32),
        name="modulation",
    )(cond8, w, b)


def _inproj_kernel(x_ref, mod_ref, g_ref, w_ref, o_ref, h_scr):
    @pl.when(pl.program_id(1) == 0)
    def _():
        h = _rms(x_ref[...]) * g_ref[...]
        h = h * (1.0 + mod_ref[0, 1:2, :]) + mod_ref[0, 0:1, :]
        h_scr[...] = h.astype(BF16)

    o_ref[...] = _qk(h_scr[...], w_ref[...]).astype(o_ref.dtype)


def _inproj(x, mod, g, w, tm, tn):
    rows = x.shape[0]
    per_mod = rows // mod.shape[0]
    tm = min(tm, per_mod)
    return pl.pallas_call(
        _inproj_kernel,
        out_shape=jax.ShapeDtypeStruct((rows, N_PROJ), BF16),
        grid=(rows // tm, N_PROJ // tn),
        in_specs=[pl.BlockSpec((tm, D_MODEL), lambda i, j: (i, 0)),
                  pl.BlockSpec((1, SUBLANES, D_MODEL), lambda i, j: (i * tm // per_mod, 0, 0)),
                  pl.BlockSpec((1, D_MODEL), lambda i, j: (0, 0)),
                  pl.BlockSpec((tn, D_MODEL), lambda i, j: (j, 0))],
        out_specs=pl.BlockSpec((tm, tn), lambda i, j: (i, j)),
        scratch_shapes=[pltpu.VMEM((tm, D_MODEL), BF16)],
        compiler_params=_params(("parallel", "arbitrary"), 48),
        name="inproj",
    )(x, mod, g, w)


def _softmax_pv(s, vb):
    m = jnp.max(s, axis=-1, keepdims=True)
    p = jnp.exp2(s - m)
    l = jnp.sum(p, axis=-1, keepdims=True)
    return jnp.dot(p.astype(BF16), vb, preferred_element_type=F32) / l


def _qk(qb, kb):
    return lax.dot_general(qb, kb, (((1,), (1,)), ((), ())), preferred_element_type=F32)


def _attn_ctx_kernel(q_ref, k_ref, v_ref, qg_ref, kg_ref, *rest):
    a_ref, ko_ref, vo_ref = rest[-3:]
    v = v_ref[...].astype(F32)
    for hk in range(N_KV_HEADS):
        ks = slice(hk * HEAD_DIM, (hk + 1) * HEAD_DIM)
        kn = _rms(k_ref[:, ks].astype(F32)) * kg_ref[...]
        ko_ref[:, hk, :] = kn
        vo_ref[:, hk, :] = v[:, ks]
        knb = kn.astype(BF16)
        vb = v[:, ks].astype(BF16)
        for g in range(Q_PER_KV):
            hs = slice((hk * Q_PER_KV + g) * HEAD_DIM, (hk * Q_PER_KV + g + 1) * HEAD_DIM)
            qn = _rms(q_ref[:, hs].astype(F32)) * (qg_ref[...] * Q_SCALE)
            s = _qk(qn.astype(BF16), knb)
            a_ref[:, hs] = _softmax_pv(s, vb).astype(BF16)


def _attn_ctx(proj, qg, kg, nseq, seq, layer, depth, earlier):
    kvw = N_KV_HEADS * HEAD_DIM
    cache = jax.ShapeDtypeStruct((nseq, depth, seq, N_KV_HEADS, HEAD_DIM), F32)
    cache_spec = pl.BlockSpec((None, None, seq, N_KV_HEADS, HEAD_DIM), lambda b: (b, layer, 0, 0, 0))
    n_in = 5
    return pl.pallas_call(
        _attn_ctx_kernel,
        out_shape=(jax.ShapeDtypeStruct((nseq * seq, D_MODEL), BF16), cache, cache),
        grid=(nseq,),
        in_specs=[pl.BlockSpec((seq, D_MODEL), lambda b: (b, COL_Q // D_MODEL)),
                  pl.BlockSpec((seq, kvw), lambda b: (b, COL_K // kvw)),
                  pl.BlockSpec((seq, kvw), lambda b: (b, COL_V // kvw)),
                  pl.BlockSpec((1, HEAD_DIM), lambda b: (0, 0)),
                  pl.BlockSpec((1, HEAD_DIM), lambda b: (0, 0))] + [pl.BlockSpec(memory_space=pl.ANY)] * len(earlier),
        out_specs=(pl.BlockSpec((seq, D_MODEL), lambda b: (b, 0)), cache_spec, cache_spec),
        input_output_aliases={n_in + i: 1 + i for i in range(len(earlier))},
        compiler_params=_params(("parallel",), 32),
        name="attn_ctx",
    )(proj, proj, proj, qg, kg, *earlier)


def _rope(t, cos, sa, sb):
    return t * cos + pltpu.roll(t, 96, 1) * sa + pltpu.roll(t, 32, 1) * sb


def _prep_lat_kernel(k_ref, v_ref, ck_ref, cv_ref, cos_ref, sa_ref, sb_ref, kg_ref, ko_ref, vo_ref):
    t = pl.program_id(1)

    @pl.when(t == 0)
    def _():
        ko_ref[...] = ck_ref[...].astype(BF16)
        vo_ref[...] = cv_ref[...].astype(BF16)

    @pl.when(t > 0)
    def _():
        cos, sa, sb = cos_ref[...], sa_ref[...], sb_ref[...]
        vo_ref[...] = v_ref[...]
        for h in range(N_KV_HEADS):
            hs = slice(h * HEAD_DIM, (h + 1) * HEAD_DIM)
            ko_ref[:, hs] = _rope(_rms(k_ref[:, hs].astype(F32)) * kg_ref[...], cos, sa, sb).astype(BF16)


def _prep_lat(proj, cache_k, cache_v, cos, sa, sb, kg, nb, seq):
    past = cache_k.shape[1]
    tr = past
    kvw = N_KV_HEADS * HEAD_DIM
    nt = seq // tr
    prev = lambda t: jnp.maximum(t - 1, 0)
    return pl.pallas_call(
        _prep_lat_kernel,
        out_shape=(jax.ShapeDtypeStruct((nb, past + seq, kvw), BF16),
                   jax.ShapeDtypeStruct((nb, past + seq, kvw), BF16)),
        grid=(nb, nt + 1),
        in_specs=[pl.BlockSpec((tr, kvw), lambda b, t: (b * nt + prev(t), COL_K // kvw)),
                  pl.BlockSpec((tr, kvw), lambda b, t: (b * nt + prev(t), COL_V // kvw)),
                  pl.BlockSpec((None, past, kvw), lambda b, t: (b, 0, 0)),
                  pl.BlockSpec((None, past, kvw), lambda b, t: (b, 0, 0)),
                  pl.BlockSpec((tr, HEAD_DIM), lambda b, t: (prev(t), 0)),
                  pl.BlockSpec((tr, HEAD_DIM), lambda b, t: (prev(t), 0)),
                  pl.BlockSpec((tr, HEAD_DIM), lambda b, t: (prev(t), 0)),
                  pl.BlockSpec((1, HEAD_DIM), lambda b, t: (0, 0))],
        out_specs=(pl.BlockSpec((None, tr, kvw), lambda b, t: (b, t, 0)),
                   pl.BlockSpec((None, tr, kvw), lambda b, t: (b, t, 0))),
        compiler_params=_params(("parallel", "arbitrary"), 32),
        name="prep_lat",
    )(proj, proj, cache_k, cache_v, cos, sa, sb, kg)


def _attn_lat_kernel(q_ref, cos_ref, sa_ref, sb_ref, qg_ref, k_ref, v_ref, o_ref):
    cos, sa, sb = cos_ref[...], sa_ref[...], sb_ref[...]
    gain = qg_ref[...] * Q_SCALE
    for hk in range(N_KV_HEADS):
        kb = k_ref[:, hk * HEAD_DIM:(hk + 1) * HEAD_DIM]
        vb = v_ref[:, hk * HEAD_DIM:(hk + 1) * HEAD_DIM]
        for g in range(Q_PER_KV):
            hs = slice((hk * Q_PER_KV + g) * HEAD_DIM, (hk * Q_PER_KV + g + 1) * HEAD_DIM)
            qn = _rope(_rms(q_ref[:, hs].astype(F32)) * gain, cos, sa, sb)
            s = _qk(qn.astype(BF16), kb)
            o_ref[:, hs] = _softmax_pv(s, vb).astype(BF16)


def _attn_lat(proj, kall, vall, cos, sa, sb, qg, nb, seq, tq):
    nk, kvw = kall.shape[1:]
    nq = seq // tq
    table = pl.BlockSpec((tq, HEAD_DIM), lambda b, i: (i, 0))
    return pl.pallas_call(
        _attn_lat_kernel,
        out_shape=jax.ShapeDtypeStruct((nb * seq, D_MODEL), BF16),
        grid=(nb, nq),
        in_specs=[pl.BlockSpec((tq, D_MODEL), lambda b, i: (b * nq + i, COL_Q // D_MODEL)),
                  table, table, table, pl.BlockSpec((1, HEAD_DIM), lambda b, i: (0, 0)),
                  pl.BlockSpec((None, nk, kvw), lambda b, i: (b, 0, 0)),
                  pl.BlockSpec((None, nk, kvw), lambda b, i: (b, 0, 0))],
        out_specs=pl.BlockSpec((tq, D_MODEL), lambda b, i: (b * nq + i, 0)),
        compiler_params=_params(("parallel", "arbitrary"), 48),
        name="attn_lat",
    )(proj, cos, sa, sb, qg, kall, vall)


SSD_GROUP = 4


def _ssd_kernel(*refs, reverse, has_h0, nc):
    it = iter(refs)
    if reverse:
        xsc_ref, bcc_ref, dt_ref, alog_ref, dtb_ref, e_ref = (next(it) for _ in range(6))
    else:
        xs_ref, bc_ref, dt_ref, xsp_ref, xsn_ref, bcp_ref, bcn_ref = (next(it) for _ in range(7))
        cwx_ref, cbx_ref, cwb_ref, cbb_ref, alog_ref, dtb_ref, e_ref, shift_ref = (next(it) for _ in range(8))
    h0_ref = next(it) if has_h0 else None
    if reverse:
        z_ref, yf_ref, d_ref, g_ref = (next(it) for _ in range(4))
        y_ref, hout_ref, ht_scr = refs[-3:]
    else:
        y_ref, xsc_ref, bcc_ref, hout_ref, ht_scr = refs[-5:]

    q = SSM_CHUNK
    npair = SSM_HEADS // 2
    group = y_ref.shape[0]
    c = pl.program_id(1)
    cc = nc - 1 - c if reverse else c

    @pl.when(c == 0)
    def _():
        for s in range(group):
            for p in range(npair):
                if has_h0:
                    ht_scr[s, p] = h0_ref[s, p].T
                else:
                    ht_scr[s, p] = jnp.zeros((q, q), F32)

    off = SSM_HEADS if reverse else 0
    ii = lax.broadcasted_iota(jnp.int32, (q, q), 0)
    jj = lax.broadcasted_iota(jnp.int32, (q, q), 1)
    tri = (jj >= ii) if reverse else (ii >= jj)
    tri_b = jnp.where(tri, 1.0, 0.0).astype(BF16)
    tri_tb3 = jnp.concatenate([jnp.where((ii >= jj) if reverse else (jj >= ii), 1.0, 0.0).astype(BF16)] * 3, axis=0)
    lo = lax.broadcasted_iota(jnp.int32, (q, LANES), 1) < SSM_HEADDIM
    neg_a = -jnp.exp(alog_ref[...])
    if not reverse:
        keep_prev = jnp.where(cc == 0, 0.0, 1.0).astype(BF16)
        keep_next = jnp.where(cc == nc - 1, 0.0, 1.0).astype(BF16)

    def conv_silu(s, x_ref_, p_ref_, n_ref_, w_ref_, b_ref_):
        x = x_ref_[s]
        ext = jnp.concatenate([x, p_ref_[s] * keep_prev, n_ref_[s] * keep_next], axis=0)
        nb = jnp.dot(shift_ref[...], ext, preferred_element_type=F32)
        w = w_ref_[...]
        return _silu(w[0:1, :] * nb[:q, :] + w[1:2, :] * x.astype(F32) + w[2:3, :] * nb[q:, :] + b_ref_[...])

    for s in range(group):
        if reverse:
            xs = xsc_ref[s].astype(F32)
            bc = bcc_ref[s].astype(F32)
        else:
            xs = conv_silu(s, xs_ref, xsp_ref, xsn_ref, cwx_ref, cbx_ref)
            bc = conv_silu(s, bc_ref, bcp_ref, bcn_ref, cwb_ref, cbb_ref)
            xsc_ref[s] = xs.astype(BF16)
            bcc_ref[s] = bc.astype(BF16)

        dt = jax.nn.softplus(dt_ref[s].astype(F32) + dtb_ref[...])
        da = dt * neg_a
        acs3 = jnp.dot(tri_b, jnp.concatenate(_split3(da), axis=1), preferred_element_type=F32)
        acs = acs3[:, :LANES] + acs3[:, LANES:2 * LANES] + acs3[:, 2 * LANES:]
        acs_t = jnp.dot(jnp.concatenate(_split3(da.T), axis=1), tri_tb3, preferred_element_type=F32)
        spread = jnp.dot(jnp.concatenate(_split3(jnp.concatenate([dt, acs], axis=0)), axis=1), e_ref[...],
                         preferred_element_type=F32)
        acs_x = spread[q:, :]
        tot_x = acs_x[0:1, :] if reverse else acs_x[q - 1:q, :]
        xdt = xs * spread[:q, :]
        xw = xdt * jnp.exp(tot_x - acs_x)
        eacs_x = jnp.exp(acs_x)
        etot_x = jnp.exp(tot_x)

        parts = []
        for g in range(2):
            bg = bc[:, g * q:(g + 1) * q]
            cg = bc[:, (2 + g) * q:(3 + g) * q].astype(BF16)
            cb = _qk(cg, bg.astype(BF16))
            bgt = bg.T.astype(BF16)
            for pp in range(npair // 2):
                p = g * (npair // 2) + pp
                ps = slice(p * LANES, (p + 1) * LANES)
                ms = []
                for hh in range(2):
                    ln = off + 2 * p + hh
                    dec = jnp.where(tri, jnp.exp(acs[:, ln:ln + 1] - acs_t[ln:ln + 1, :]), 0.0)
                    ms.append((cb * dec).astype(BF16))
                m = jnp.concatenate(ms, axis=1)
                xp = xdt[:, ps]
                bd = jnp.concatenate([jnp.where(lo, xp, 0.0), jnp.where(lo, 0.0, xp)], axis=0).astype(BF16)
                ht = ht_scr[s, p]
                yp = jnp.dot(m, bd, preferred_element_type=F32)
                yp = yp + jnp.dot(cg, ht.astype(BF16), preferred_element_type=F32) * eacs_x[:, ps]
                ht_scr[s, p] = etot_x[:, ps] * ht + jnp.dot(bgt, xw[:, ps].astype(BF16),
                                                            preferred_element_type=F32)
                parts.append(yp)
        y = jnp.concatenate(parts, axis=1)

        if reverse:
            y = y + yf_ref[s] + d_ref[...] * xs
            y = y * _silu(z_ref[s].astype(F32))
            y_ref[s] = (_rms(y) * g_ref[...]).astype(y_ref.dtype)
        else:
            y_ref[s] = y

    @pl.when(c == nc - 1)
    def _():
        for s in range(group):
            for p in range(npair):
                hout_ref[s, p] = ht_scr[s, p].T


def _ssd_pass(proj, lp, d, nseq, seq, h0, fwd=None, layer=0, depth=1, earlier=None):
    reverse = fwd is not None
    has_h0 = h0 is not None
    q = SSM_CHUNK
    nc = seq // q
    g = min(SSD_GROUP, nseq)
    assert nseq % g == 0
    npair = SSM_HEADS // 2
    hb = q // HALO
    bcw = 4 * SSM_STATE
    chunk = (lambda c: nc - 1 - c) if reverse else (lambda c: c)
    prev8 = lambda c: jnp.maximum(chunk(c) * hb - 1, 0)
    next8 = lambda c: jnp.minimum((chunk(c) + 1) * hb, seq // HALO - 1)
    const = lambda s, c: (0, 0)
    tile = lambda width, col=0: pl.BlockSpec((g, q, width), lambda s, c: (s, chunk(c), col // width))
    halo = lambda width, col, where: pl.BlockSpec((g, HALO, width), lambda s, c: (s, where(c), col // width))
    seqs = lambda t: t.reshape(nseq, seq, t.shape[-1])
    scan_consts = [pl.BlockSpec((1, LANES), const), pl.BlockSpec((1, LANES), const),
                   pl.BlockSpec((3 * LANES, D_MODEL), const)]
    scan_args = [lp["alog"][d], lp["dtb"][d], lp["expand"][d]]
    proj3 = seqs(proj)
    if reverse:
        yf, xsc, bcc = fwd
        in_specs = [tile(D_MODEL), tile(bcw), tile(LANES, COL_DT)] + scan_consts
        args = [xsc, bcc, proj3] + scan_args
    else:
        in_specs = [tile(D_MODEL, COL_XS), tile(bcw, COL_BC), tile(LANES, COL_DT),
                    halo(D_MODEL, COL_XS, prev8), halo(D_MODEL, COL_XS, next8),
                    halo(bcw, COL_BC, prev8), halo(bcw, COL_BC, next8),
                    pl.BlockSpec((3, D_MODEL), const), pl.BlockSpec((1, D_MODEL), const),
                    pl.BlockSpec((3, bcw), const), pl.BlockSpec((1, bcw), const)] + scan_consts + [
                    pl.BlockSpec((2 * q, q + 2 * HALO), const)]
        args = [proj3] * 7 + [lp["cw_x"], lp["cb_x"], lp["cw_bc"], lp["cb_bc"]] + scan_args + [lp["shift"]]
    if has_h0:
        in_specs.append(pl.BlockSpec((g, npair, q, q), lambda s, c: (s, 0, 0, 0)))
        args.append(h0)
    if reverse:
        in_specs += [tile(D_MODEL, COL_Z), tile(D_MODEL), pl.BlockSpec((1, D_MODEL), const),
                     pl.BlockSpec((1, D_MODEL), const)]
        args += [proj3, yf, lp["d_row"], lp["ssm_norm_g"]]
    state = jax.ShapeDtypeStruct((nseq, depth, npair, q, q), F32)
    state_spec = pl.BlockSpec((g, None, npair, q, q), lambda s, c: (s, layer, 0, 0, 0))
    if reverse:
        out_shape = (jax.ShapeDtypeStruct((nseq, seq, D_MODEL), BF16), state)
        out_specs = (tile(D_MODEL), state_spec)
    else:
        out_shape = (jax.ShapeDtypeStruct((nseq, seq, D_MODEL), F32),
                     jax.ShapeDtypeStruct((nseq, seq, D_MODEL), BF16),
                     jax.ShapeDtypeStruct((nseq, seq, bcw), BF16), state)
        out_specs = (tile(D_MODEL), tile(D_MODEL), tile(bcw), state_spec)
    aliases = {}
    if earlier is not None:
        aliases = {len(args): len(out_shape) - 1}
        in_specs.append(pl.BlockSpec(memory_space=pl.ANY))
        args.append(earlier)
    outs = pl.pallas_call(
        functools.partial(_ssd_kernel, reverse=reverse, has_h0=has_h0, nc=nc),
        out_shape=out_shape,
        grid=(nseq // g, nc),
        in_specs=in_specs,
        out_specs=out_specs,
        input_output_aliases=aliases,
        scratch_shapes=[pltpu.VMEM((g, npair, q, q), F32)],
        compiler_params=_params(("parallel", "arbitrary"), 48),
        name="ssd_bwd" if reverse else "ssd_fwd",
    )(*args)
    if reverse:
        return outs[0].reshape(nseq * seq, D_MODEL), outs[1]
    return outs


ROUTE_I1, ROUTE_I2, ROUTE_P1, ROUTE_P2, ROUTE_R1, ROUTE_R2 = 0, 1, 2, 3, 4, 5


def _pack_bf16_pairs(v):
    n = v.shape[1] // 2
    bits = lax.bitcast_convert_type(v.astype(BF16).astype(F32), jnp.uint32)
    return lax.bitcast_convert_type((bits[:, :n] >> 16) | bits[:, n:], jnp.int32)


def _unpack_bf16_pairs(p):
    u = lax.bitcast_convert_type(p, jnp.uint32)
    lo = lax.bitcast_convert_type(u << 16, F32)
    hi = lax.bitcast_convert_type(u & jnp.uint32(0xFFFF0000), F32)
    return jnp.concatenate([lo, hi], axis=1)


def _top2_route(logits):
    tm = logits.shape[0]
    lane = lax.broadcasted_iota(jnp.int32, logits.shape, 1).astype(F32)
    lg = jnp.where(lane < N_EXPERTS, logits, -jnp.inf)
    m1 = jnp.max(lg, axis=-1, keepdims=True)
    i1 = jnp.min(jnp.where(lg == m1, lane, float(LANES)), axis=-1, keepdims=True)
    lg2 = jnp.where(lane == i1, -jnp.inf, lg)
    m2 = jnp.max(lg2, axis=-1, keepdims=True)
    i2 = jnp.min(jnp.where(lg2 == m2, lane, float(LANES)), axis=-1, keepdims=True)
    e = jnp.exp(m2 - m1)
    p1 = 1.0 / (1.0 + e)
    chosen = jnp.where(lane == i1, 1.0, jnp.where(lane == i2, 1.0, 0.0))
    earlier = (lax.broadcasted_iota(jnp.int32, (tm, tm), 0) > lax.broadcasted_iota(jnp.int32, (tm, tm), 1))
    ranks = _bdot(jnp.where(earlier, 1.0, 0.0), chosen)
    r1 = jnp.sum(jnp.where(lane == i1, ranks, 0.0), axis=-1, keepdims=True)
    r2 = jnp.sum(jnp.where(lane == i2, ranks, 0.0), axis=-1, keepdims=True)
    rec = jnp.where(lane == ROUTE_I1, i1, jnp.where(lane == ROUTE_I2, i2, 0.0))
    rec = jnp.where(lane == ROUTE_P1, p1, jnp.where(lane == ROUTE_P2, e * p1, rec))
    rec = jnp.where(lane == ROUTE_R1, r1, jnp.where(lane == ROUTE_R2, r2, rec))
    return rec, jnp.sum(chosen, axis=0, keepdims=True)


def _merge_kernel(*refs, seq, tm, moe):
    it = iter(refs)
    attn_ref, ssm_ref, scb_ref, scc_ref, sch_ref = (next(it) for _ in range(5))
    sccp_ref, sccn_ref, schp_ref, schn_ref = (next(it) for _ in range(4))
    g0_ref, g1_ref, g2_ref, x_ref, mod_ref = (next(it) for _ in range(5))
    wa_ref, ws_ref, wc_ref, wm_ref, scw_ref, ng_ref = (next(it) for _ in range(6))
    router_ref = next(it) if moe else None
    xo_ref, h2_ref = next(it), next(it)
    route_ref, cnt_ref = (next(it), next(it)) if moe else (None, None)

    i = pl.program_id(0)
    pos = (lax.broadcasted_iota(jnp.int32, (tm, 1), 0) + i * tm) & (seq - 1)
    u = scc_ref[...].astype(F32) * sch_ref[...].astype(F32)
    up = sccp_ref[HALO - 1:HALO, :].astype(F32) * schp_ref[HALO - 1:HALO, :].astype(F32)
    un = sccn_ref[0:1, :].astype(F32) * schn_ref[0:1, :].astype(F32)
    r = lax.broadcasted_iota(jnp.int32, (tm, 1), 0)
    um1 = jnp.where(pos == 0, 0.0, jnp.where(r == 0, up, pltpu.roll(u, 1, 0)))
    up1 = jnp.where(pos == seq - 1, 0.0, jnp.where(r == tm - 1, un, pltpu.roll(u, tm - 1, 0)))
    w = scw_ref[...]
    sc = scb_ref[...].astype(F32) * (w[0:1, :] * um1 + w[1:2, :] * u + w[2:3, :] * up1)

    gate = lambda g_ref: _sigmoid(g_ref[...].astype(F32))
    merged = gate(g0_ref) * jnp.dot(attn_ref[...], wa_ref[...], preferred_element_type=F32)
    merged = merged + gate(g1_ref) * jnp.dot(ssm_ref[...], ws_ref[...], preferred_element_type=F32)
    merged = merged + gate(g2_ref) * jnp.dot(sc.astype(BF16), wc_ref[...], preferred_element_type=F32)
    mod = mod_ref[0]
    xn = x_ref[...] + mod[2:3, :] * jnp.dot(merged.astype(BF16), wm_ref[...], preferred_element_type=F32)
    xo_ref[...] = xn
    h2 = _rms(xn) * ng_ref[...] * (1.0 + mod[4:5, :]) + mod[3:4, :]
    if moe:
        h2_ref[...] = _pack_bf16_pairs(h2)
        h_hi = h2.astype(BF16)
        h_lo = (h2 - h_hi.astype(F32)).astype(BF16)
        both = jnp.dot(h_hi, router_ref[...], preferred_element_type=F32)
        logits = both[:, :LANES] + both[:, LANES:] + jnp.dot(h_lo, router_ref[:, :LANES], preferred_element_type=F32)
        rec, counts = _top2_route(logits)
        route_ref[...] = rec
        cnt_ref[...] = jnp.broadcast_to(counts, cnt_ref.shape)
    else:
        h2_ref[...] = h2.astype(BF16)


def _merge(attn, ssm, proj, x, mod, lp, seq, tm):
    rows = x.shape[0]
    per_mod = rows // mod.shape[0]
    tm = min(tm, per_mod)
    moe = lp["router"] is not None
    hb = tm // HALO
    prev8 = lambda i: jnp.maximum(i * hb - 1, 0)
    next8 = lambda i: jnp.minimum((i + 1) * hb, rows // HALO - 1)
    tile = lambda col: pl.BlockSpec((tm, D_MODEL), lambda i: (i, col // D_MODEL))
    const = lambda shape: pl.BlockSpec(shape, lambda i: (0, 0))
    in_specs = [tile(0), tile(0), tile(COL_SCB), tile(COL_SCC), tile(COL_SCH),
                pl.BlockSpec((HALO, D_MODEL), lambda i: (prev8(i), COL_SCC // D_MODEL)),
                pl.BlockSpec((HALO, D_MODEL), lambda i: (next8(i), COL_SCC // D_MODEL)),
                pl.BlockSpec((HALO, D_MODEL), lambda i: (prev8(i), COL_SCH // D_MODEL)),
                pl.BlockSpec((HALO, D_MODEL), lambda i: (next8(i), COL_SCH // D_MODEL)),
                tile(COL_G), tile(COL_G + D_MODEL), tile(COL_G + 2 * D_MODEL), tile(0),
                pl.BlockSpec((1, SUBLANES, D_MODEL), lambda i: (i * tm // per_mod, 0, 0)),
                _resident((D_MODEL, D_MODEL)), _resident((D_MODEL, D_MODEL)), _resident((D_MODEL, D_MODEL)),
                _resident((D_MODEL, D_MODEL)), const((3, D_MODEL)), const((1, D_MODEL))]
    args = [attn, ssm, proj, proj, proj, proj, proj, proj, proj, proj, proj, proj, x, mod,
            lp["w_attn_out"], lp["w_ssm_out"], lp["w_sconv_out"], lp["w_merge"], lp["sconv_w"], lp["norm_ffn_g"]]
    out_shape = [jax.ShapeDtypeStruct((rows, D_MODEL), F32), jax.ShapeDtypeStruct((rows, D_MODEL), BF16)]
    out_specs = [tile(0), tile(0)]
    if moe:
        in_specs.append(_resident((D_MODEL, 2 * LANES)))
        args.append(lp["router"])
        out_shape[1] = jax.ShapeDtypeStruct((rows, D_MODEL // 2), jnp.int32)
        out_specs[1] = pl.BlockSpec((tm, D_MODEL // 2), lambda i: (i, 0))
        out_shape += [jax.ShapeDtypeStruct((rows, LANES), F32),
                      jax.ShapeDtypeStruct((rows // tm * SUBLANES, LANES), F32)]
        out_specs += [pl.BlockSpec((tm, LANES), lambda i: (i, 0)), pl.BlockSpec((SUBLANES, LANES), lambda i: (i, 0))]
    return pl.pallas_call(
        functools.partial(_merge_kernel, seq=seq, tm=tm, moe=moe),
        out_shape=tuple(out_shape),
        grid=(rows // tm,),
        in_specs=in_specs,
        out_specs=tuple(out_specs),
        compiler_params=_params(("parallel",), 56),
        name="merge",
    )(*args)


def _swiglu_acc(h, w1_ref, w3_ref, w2_ref, fs):
    a = _silu(jnp.dot(h, w1_ref[:, fs], preferred_element_type=F32)) * jnp.dot(h, w3_ref[:, fs],
                                                                                preferred_element_type=F32)
    return jnp.dot(a.astype(BF16), w2_ref[fs, :], preferred_element_type=F32)


def _swiglu(h, w1_ref, w3_ref, w2_ref):
    cut = (D_FF // 2 + MXU_WIDTH - 1) // MXU_WIDTH * MXU_WIDTH
    return (_swiglu_acc(h, w1_ref, w3_ref, w2_ref, slice(0, cut))
            + _swiglu_acc(h, w1_ref, w3_ref, w2_ref, slice(cut, D_FF)))


def _ffn_kernel(h_ref, x_ref, mod_ref, w1_ref, w3_ref, w2_ref, o_ref):
    o_ref[...] = x_ref[...] + mod_ref[0, 5:6, :] * _swiglu(h_ref[...], w1_ref, w3_ref, w2_ref)


def _resident(shape):
    return pl.BlockSpec(shape, lambda *_: (0,) * len(shape), pipeline_mode=pl.Buffered(1))


def _ffn(h2, x, mod, w1, w3, w2, tm):
    rows = x.shape[0]
    per_mod = rows // mod.shape[0]
    tm = min(tm, per_mod)
    return pl.pallas_call(
        _ffn_kernel,
        out_shape=jax.ShapeDtypeStruct((rows, D_MODEL), F32),
        grid=(rows // tm,),
        in_specs=[pl.BlockSpec((tm, D_MODEL), lambda i: (i, 0)),
                  pl.BlockSpec((tm, D_MODEL), lambda i: (i, 0)),
                  pl.BlockSpec((1, SUBLANES, D_MODEL), lambda i: (i * tm // per_mod, 0, 0)),
                  _resident((D_MODEL, D_FF)), _resident((D_MODEL, D_FF)), _resident((D_FF, D_MODEL))],
        out_specs=pl.BlockSpec((tm, D_MODEL), lambda i: (i, 0)),
        compiler_params=_params(("parallel",), 56),
        name="ffn",
    )(h2, x, mod, w1, w3, w2)


def _cast_kernel(*refs):
    n = len(refs) // 2
    for src, dst in zip(refs[:n], refs[n:]):
        dst[...] = src[...].astype(dst.dtype)


def _cast_bf16(ws):
    e, r, c = ws[0].shape
    rb = max(d for d in range(16, r + 1, 16) if r % d == 0 and d * c * 4 <= 3 * MIB)
    spec = pl.BlockSpec((None, rb, c), lambda i, j: (i, j, 0))
    return pl.pallas_call(
        _cast_kernel,
        out_shape=tuple(jax.ShapeDtypeStruct(w.shape, BF16) for w in ws),
        grid=(e, r // rb),
        in_specs=[spec] * len(ws),
        out_specs=tuple([spec] * len(ws)),
        compiler_params=_params(("parallel", "parallel"), 48),
        name="cast_bf16",
    )(*ws)


SC_CORES = 2
SC_SUBCORES = 16
SC_CHUNK = 128


def _sc_gather(table, idx):
    nw = SC_CORES * SC_SUBCORES
    b, d = idx.shape[0], table.shape[1]
    per_w = b // nw
    assert per_w * nw == b and per_w % SC_CHUNK == 0
    mesh = plsc.VectorSubcoreMesh(core_axis_name="c", subcore_axis_name="s")

    def body(table_hbm, idx_hbm, out_hbm, idx_v, rows_v, sem):
        wid = lax.axis_index("s") * SC_CORES + lax.axis_index("c")

        @pl.loop(0, per_w // SC_CHUNK)
        def _(j):
            off = pl.multiple_of(wid * per_w + j * SC_CHUNK, SC_CHUNK)
            pltpu.sync_copy(idx_hbm.at[pl.ds(off, SC_CHUNK)], idx_v)
            pltpu.async_copy(table_hbm.at[idx_v], rows_v, sem).wait()
            pltpu.sync_copy(rows_v, out_hbm.at[pl.ds(off, SC_CHUNK)])

    return pl.kernel(
        body,
        out_type=jax.ShapeDtypeStruct((b, d), table.dtype),
        mesh=mesh,
        scratch_types=[pltpu.VMEM((SC_CHUNK,), jnp.int32), pltpu.VMEM((SC_CHUNK, d), table.dtype),
                       pltpu.SemaphoreType.DMA],
        name="sc_gather",
    )(table, idx)


def _sc_dispatch(table, slot, n_slots):
    nw = SC_CORES * SC_SUBCORES
    t, d = table.shape
    per_w = t // nw
    assert per_w * nw == t and per_w % SC_CHUNK == 0 and slot.shape[0] == 2 * t
    mesh = plsc.VectorSubcoreMesh(core_axis_name="c", subcore_axis_name="s")

    def body(table_hbm, slot_hbm, out_hbm, idx_v, rows_v, sem):
        wid = lax.axis_index("s") * SC_CORES + lax.axis_index("c")

        @pl.loop(0, per_w // SC_CHUNK)
        def _(j):
            off = pl.multiple_of(wid * per_w + j * SC_CHUNK, SC_CHUNK)
            pltpu.sync_copy(table_hbm.at[pl.ds(off, SC_CHUNK)], rows_v)
            for k in range(2):
                pltpu.sync_copy(slot_hbm.at[pl.ds(pl.multiple_of(k * t + off, SC_CHUNK), SC_CHUNK)], idx_v)
                pltpu.async_copy(rows_v, out_hbm.at[idx_v], sem).wait()

    return pl.kernel(
        body,
        out_type=jax.ShapeDtypeStruct((n_slots, d), table.dtype),
        mesh=mesh,
        scratch_types=[pltpu.VMEM((SC_CHUNK,), jnp.int32), pltpu.VMEM((SC_CHUNK, d), table.dtype),
                       pltpu.SemaphoreType.DMA],
        name="sc_dispatch",
    )(table, slot)


def _route_tables(route, counts, tile):
    t = route.shape[0]
    n_mt = counts.shape[0] // SUBLANES
    cnt = counts.reshape(n_mt, SUBLANES, LANES)[:, 0, :N_EXPERTS].astype(jnp.int32)
    incl = jnp.cumsum(cnt, axis=0)
    padded = (incl[-1] + tile - 1) // tile * tile
    gend = jnp.cumsum(padded)
    base = jnp.repeat((gend - padded)[None, :] + incl - cnt, t // n_mt, axis=0)
    e = route[:, ROUTE_I1:ROUTE_I2 + 1].astype(jnp.int32)
    r = route[:, ROUTE_R1:ROUTE_R2 + 1].astype(jnp.int32)
    pick = e[:, :, None] == jnp.arange(N_EXPERTS)[None, None, :]
    slot = jnp.sum(jnp.where(pick, base[:, None, :], 0), axis=-1) + r
    n_slots = 2 * t + N_EXPERTS * tile
    n_valid = gend[-1] // tile
    n_tiles = n_slots // tile
    tile_start = jnp.arange(n_tiles) * tile
    tile_expert = jnp.sum(tile_start[:, None] >= gend[None, :], axis=1)
    last = jnp.take(tile_expert, jnp.maximum(n_valid - 1, 0))
    tile_expert = jnp.where(jnp.arange(n_tiles) < n_valid, tile_expert, last)
    chosen = tile_expert[:, None] == jnp.arange(N_EXPERTS)[None, :]
    first_row = tile_start - jnp.sum(jnp.where(chosen, (gend - padded)[None, :], 0), axis=1)
    tile_rows = jnp.clip(jnp.sum(jnp.where(chosen, incl[-1][None, :], 0), axis=1) - first_row, 0, tile)
    return (slot.T.reshape(-1).astype(jnp.int32), n_slots, tile_expert.astype(jnp.int32),
            tile_rows.astype(jnp.int32))


def _moe_ffn_kernel(te_ref, rows_ref, h_ref, w1_ref, w3_ref, w2_ref, o_ref):
    rows = rows_ref[pl.program_id(0)]
    half = o_ref.shape[0] // 2

    def expert_ffn(r):
        h = _unpack_bf16_pairs(h_ref[r, :]).astype(BF16)
        o_ref[r, :] = _pack_bf16_pairs(_swiglu(h, w1_ref, w3_ref, w2_ref))

    @pl.when(rows > half)
    def _():
        expert_ffn(slice(None))

    @pl.when((rows > 0) & (rows <= half))
    def _():
        expert_ffn(slice(0, half))
        o_ref[half:, :] = jnp.zeros((half, o_ref.shape[1]), jnp.int32)

    @pl.when(rows == 0)
    def _():
        o_ref[...] = jnp.zeros(o_ref.shape, jnp.int32)


def _moe_ffn(hs, tile_expert, tile_rows, w1, w3, w2, tile):
    n_slots = hs.shape[0]
    grid_spec = pltpu.PrefetchScalarGridSpec(
        num_scalar_prefetch=2,
        grid=(n_slots // tile,),
        in_specs=[pl.BlockSpec((tile, D_MODEL // 2), lambda i, te, nv: (i, 0)),
                  pl.BlockSpec((None, D_MODEL, D_FF), lambda i, te, nv: (te[i], 0, 0)),
                  pl.BlockSpec((None, D_MODEL, D_FF), lambda i, te, nv: (te[i], 0, 0)),
                  pl.BlockSpec((None, D_FF, D_MODEL), lambda i, te, nv: (te[i], 0, 0))],
        out_specs=pl.BlockSpec((tile, D_MODEL // 2), lambda i, te, nv: (i, 0)),
    )
    return pl.pallas_call(
        _moe_ffn_kernel,
        out_shape=jax.ShapeDtypeStruct((n_slots, D_MODEL // 2), jnp.int32),
        grid_spec=grid_spec,
        compiler_params=_params(("arbitrary",), 56),
        name="moe_ffn",
    )(tile_expert, tile_rows, hs, w1, w3, w2)


def _combine_kernel(x_ref, y1_ref, y2_ref, route_ref, mod_ref, o_ref):
    p1 = route_ref[:, ROUTE_P1:ROUTE_P1 + 1]
    p2 = route_ref[:, ROUTE_P2:ROUTE_P2 + 1]
    f = p1 * _unpack_bf16_pairs(y1_ref[...]) + p2 * _unpack_bf16_pairs(y2_ref[...])
    o_ref[...] = x_ref[...] + mod_ref[0, 5:6, :] * f


def _combine(x, yg, route, mod, tm):
    rows = x.shape[0]
    per_mod = rows // mod.shape[0]
    tm = min(tm, per_mod)
    return pl.pallas_call(
        _combine_kernel,
        out_shape=jax.ShapeDtypeStruct((rows, D_MODEL), F32),
        grid=(rows // tm,),
        in_specs=[pl.BlockSpec((tm, D_MODEL), lambda i: (i, 0)),
                  pl.BlockSpec((tm, D_MODEL // 2), lambda i: (i, 0)),
                  pl.BlockSpec((tm, D_MODEL // 2), lambda i: (i + rows // tm, 0)),
                  pl.BlockSpec((tm, LANES), lambda i: (i, 0)),
                  pl.BlockSpec((1, SUBLANES, D_MODEL), lambda i: (i * tm // per_mod, 0, 0))],
        out_specs=pl.BlockSpec((tm, D_MODEL), lambda i: (i, 0)),
        compiler_params=_params(("parallel",), 32),
        name="moe_combine",
    )(x, yg, yg, route, mod)


def _moe(h2p, x, mod, route, counts, w1, w3, w2, tile):
    slot, n_slots, tile_expert, tile_rows = _route_tables(route, counts, tile)
    hs = _sc_dispatch(h2p, slot, n_slots)
    ys = _moe_ffn(hs, tile_expert, tile_rows, w1, w3, w2, tile)
    return _combine(x, _sc_gather(ys, slot), route, mod, tm=512)


W_IN_SEGMENTS = ((0, 1024), (1536, 1024), (4128, 1024), (5152, 1024), (6176, 1024), (7200, 3072),
                 (2560, 1024), (3584, 512), (1024, 256), (1280, 256), (4096, 32))


def _permute_kernel(w_ref, o_ref):
    dst = 0
    for src, n in W_IN_SEGMENTS:
        o_ref[dst:dst + n, :] = w_ref[src:src + n, :].astype(BF16)
        dst += n
    o_ref[dst:, :] = jnp.zeros((o_ref.shape[0] - dst, o_ref.shape[1]), BF16)


def _permute_w_in(w_in, l, cb=128):
    w_t = jnp.swapaxes(w_in, 1, 2)
    _, feats, d = w_t.shape
    return pl.pallas_call(
        _permute_kernel,
        out_shape=jax.ShapeDtypeStruct((N_PROJ, d), BF16),
        grid=(d // cb,),
        in_specs=[pl.BlockSpec((None, feats, cb), lambda i: (l, 0, i))],
        out_specs=pl.BlockSpec((N_PROJ, cb), lambda i: (0, i)),
        compiler_params=_params(("parallel",), 48),
        name="permute_w_in",
    )(w_t)


def _layer_params(l, w_in, q_norm_g, k_norm_g, w_attn_out, ssm_conv_w, ssm_conv_b, ssm_dt_bias, ssm_a_log,
                  ssm_d, ssm_norm_g, w_ssm_out, sconv_w, w_sconv_out, w_merge, norm_mix_g, norm_ffn_g,
                  ffn_w1, ffn_w3, ffn_w2, moe_router, moe_w1, moe_w3, moe_w2):
    w_perm = _permute_w_in(w_in, l)
    row = lambda v: v.reshape(1, -1)
    lane_row = lambda v, off: jnp.zeros((1, LANES), F32).at[0, off:off + SSM_HEADS].set(v)
    head_of_col = np.arange(D_MODEL) // SSM_HEADDIM
    expand = [jnp.asarray(np.tile(np.arange(LANES)[:, None] == head_of_col[None, :] + off, (3, 1)), BF16)
              for off in (0, SSM_HEADS)]
    r = np.arange(2 * SSM_CHUNK)[:, None]
    cidx = np.arange(SSM_CHUNK + 2 * HALO)[None, :]
    up = (r < SSM_CHUNK) & (((cidx == r - 1) & (r >= 1)) | ((r == 0) & (cidx == SSM_CHUNK + HALO - 1)))
    dn = (r >= SSM_CHUNK) & (((cidx == r - SSM_CHUNK + 1) & (r < 2 * SSM_CHUNK - 1))
                             | ((r == 2 * SSM_CHUNK - 1) & (cidx == SSM_CHUNK + HALO)))
    shift = jnp.asarray(up | dn, BF16)
    lp = {
        "w_in": w_perm,
        "norm_mix_g": row(norm_mix_g[l]), "norm_ffn_g": row(norm_ffn_g[l]),
        "q_norm_g": row(q_norm_g[l]), "k_norm_g": row(k_norm_g[l]),
        "w_attn_out": w_attn_out[l].astype(BF16), "w_ssm_out": w_ssm_out[l].astype(BF16),
        "w_sconv_out": w_sconv_out[l].astype(BF16), "w_merge": w_merge[l].astype(BF16),
        "cw_x": ssm_conv_w[l][:, :D_MODEL], "cw_bc": ssm_conv_w[l][:, D_MODEL:],
        "cb_x": row(ssm_conv_b[l][:D_MODEL]), "cb_bc": row(ssm_conv_b[l][D_MODEL:]),
        "alog": [lane_row(ssm_a_log[l, d], d * SSM_HEADS) for d in range(2)],
        "dtb": [lane_row(ssm_dt_bias[l, d], d * SSM_HEADS) for d in range(2)],
        "expand": expand,
        "shift": shift,
        "d_row": row(jnp.repeat(ssm_d[l], SSM_HEADDIM)),
        "ssm_norm_g": row(ssm_norm_g[l]),
        "sconv_w": sconv_w[l],
    }
    if l % 2 == 0:
        lp["router"] = None
        lp["ffn"] = tuple(t[l // 2].astype(BF16) for t in (ffn_w1, ffn_w3, ffn_w2))
    else:
        router = jnp.pad(moe_router[l // 2], ((0, 0), (0, LANES - N_EXPERTS)))
        r_hi = router.astype(BF16)
        lp["router"] = jnp.concatenate([r_hi, (router - r_hi.astype(F32)).astype(BF16)], axis=1)
        lp["ffn"] = _cast_bf16((moe_w1[l // 2], moe_w3[l // 2])) + _cast_bf16((moe_w2[l // 2],))
    return lp


def _rope_tables(seq):
    t = np.arange(seq)
    n = HEAD_DIM // 4
    inv = ROPE_THETA ** (-np.arange(n, dtype=np.float64) / n)
    ang_r = (t // GRID_W)[:, None] * inv[None, :]
    ang_c = (t % GRID_W)[:, None] * inv[None, :]
    zero = np.zeros_like(ang_r)
    cos = np.concatenate([np.cos(ang_r)] * 2 + [np.cos(ang_c)] * 2, axis=1)
    sa = np.concatenate([-np.sin(ang_r), zero, -np.sin(ang_c), zero], axis=1)
    sb = np.concatenate([zero, np.sin(ang_r), zero, np.sin(ang_c)], axis=1)
    return tuple(jnp.asarray(a, F32) for a in (cos, sa, sb))


def _trunk(x, mod, lp, nseq, seq, lat, layer=0, depth=1, carry=None):
    rows = nseq * seq
    proj = _inproj(x, mod, lp["norm_mix_g"], lp["w_in"], tm=min(2048, rows), tn=6 * MXU_WIDTH)
    if lat is None:
        attn, k_new, v_new = _attn_ctx(proj, lp["q_norm_g"], lp["k_norm_g"], nseq, seq, layer, depth,
                                       () if carry is None else carry[:2])
        *fwd, h_f = _ssd_pass(proj, lp, 0, nseq, seq, None, None, layer, depth, None if carry is None else carry[2])
        ssm, h_b = _ssd_pass(proj, lp, 1, nseq, seq, None, fwd, layer, depth, None if carry is None else carry[3])
    else:
        cache_k, cache_v, h0_f, h0_b, tables = lat
        kall, vall = _prep_lat(proj, cache_k, cache_v, *tables, lp["k_norm_g"], nseq, seq)
        attn = _attn_lat(proj, kall, vall, *tables, lp["q_norm_g"], nseq, seq, tq=256)
        k_new = v_new = None
        *fwd, h_f = _ssd_pass(proj, lp, 0, nseq, seq, h0_f)
        ssm, h_b = _ssd_pass(proj, lp, 1, nseq, seq, h0_b, fwd)
    outs = _merge(attn, ssm, proj, x, mod, lp, seq, tm=512)
    w1, w3, w2 = lp["ffn"]
    if lp["router"] is None:
        x = _ffn(outs[1], outs[0], mod, w1, w3, w2, tm=512)
    else:
        x = _moe(outs[1], outs[0], mod, outs[2], outs[3], w1, w3, w2, tile=512)
    return x, (k_new, v_new, h_f, h_b)


def kernel(x_prompt, x_sample, cache_k, cache_v, state_ssm_fwd, state_ssm_bwd, c, c_ctx, w_mod, b_mod, norm_mix_g, norm_ffn_g, w_in, q_norm_g, k_norm_g, w_attn_out, ssm_conv_w, ssm_conv_b, ssm_dt_bias, ssm_a_log, ssm_d, ssm_norm_g, w_ssm_out, sconv_w, w_sconv_out, w_merge, ffn_w1, ffn_w3, ffn_w2, moe_router, moe_w1, moe_w3, moe_w2):
    batch, seq, _ = x_prompt.shape
    nb, dseq, _ = x_sample.shape
    depth = w_in.shape[0]
    past = cache_k.shape[2]
    kvw = N_KV_HEADS * HEAD_DIM
    npair = SSM_HEADS // 2
    assert nb + 1 <= SUBLANES

    cond = jnp.zeros((SUBLANES, D_MODEL), F32).at[0].set(c_ctx).at[1:1 + nb].set(c)
    tables = _rope_tables(dseq)
    y_ctx = x_prompt.reshape(batch * seq, D_MODEL)
    y_lat = x_sample.reshape(nb * dseq, D_MODEL)
    mods = _modulation(cond, w_mod, b_mod.reshape(depth, 1, -1))
    carry = None
    for l in range(depth):
        lp = _layer_params(l, w_in, q_norm_g, k_norm_g, w_attn_out, ssm_conv_w, ssm_conv_b, ssm_dt_bias,
                           ssm_a_log, ssm_d, ssm_norm_g, w_ssm_out, sconv_w, w_sconv_out, w_merge,
                           norm_mix_g, norm_ffn_g, ffn_w1, ffn_w3, ffn_w2, moe_router, moe_w1, moe_w3, moe_w2)
        m = jnp.pad(mods[l].reshape(SUBLANES, 6, D_MODEL), ((0, 0), (0, SUBLANES - 6), (0, 0)))
        y_ctx, carry = _trunk(y_ctx, m[0:1], lp, batch, seq, None, l, depth, carry)
        lat = (cache_k[:, l].reshape(nb, past, kvw), cache_v[:, l].reshape(nb, past, kvw),
               state_ssm_fwd[:, l].reshape(nb, npair, SSM_CHUNK, SSM_STATE),
               state_ssm_bwd[:, l].reshape(nb, npair, SSM_CHUNK, SSM_STATE), tables)
        y_lat = _trunk(y_lat, m[1:1 + nb], lp, nb, dseq, lat)[0]
    k_all, v_all, hf_all, hb_all = carry
    state_shape = (batch, depth, SSM_HEADS, SSM_HEADDIM, SSM_STATE)
    return (y_ctx.reshape(batch, seq, D_MODEL), y_lat.reshape(nb, dseq, D_MODEL),
            k_all, v_all,
            hf_all.reshape(state_shape), hb_all.reshape(state_shape))
```

```python
import functools

import jax
import jax.numpy as jnp
import numpy as np
from jax import lax
from jax.experimental import pallas as pl
from jax.experimental.pallas import tpu as pltpu
from jax.experimental.pallas import tpu_sc as plsc

F32 = jnp.float32
BF16 = jnp.bfloat16
HIGHEST = lax.Precision.HIGHEST

D_MODEL = 1024
HEAD_DIM = 128
N_HEADS = 8
N_KV_HEADS = 2
Q_PER_KV = N_HEADS // N_KV_HEADS
ROPE_THETA = 10000.0
GRID_W = 64
SSM_HEADS = 16
SSM_HEADDIM = 64
SSM_STATE = 128
SSM_CHUNK = 128
D_FF = 2816
N_EXPERTS = 8
EPS = 1e-6
Q_SCALE = 1.4426950408889634 * HEAD_DIM ** -0.5

COL_Q, COL_Z, COL_SCB, COL_SCC, COL_SCH, COL_G = 0, 1024, 2048, 3072, 4096, 5120
COL_XS, COL_BC, COL_K, COL_V, COL_DT = 8192, 9216, 9728, 9984, 10240
MXU_WIDTH = 256
N_PROJ = 10752
LANES = 128
SUBLANES = 8
HALO = 16
MIB = 1024 * 1024


def _params(sem, vmem_mib):
    return pltpu.CompilerParams(dimension_semantics=sem, vmem_limit_bytes=vmem_mib * MIB)


def _sigmoid(t):
    return 1.0 / (1.0 + jnp.exp(-t))


def _silu(t):
    return t * _sigmoid(t)


def _rms(t):
    return t * lax.rsqrt(jnp.mean(t * t, axis=-1, keepdims=True) + EPS)


def _bdot(a, b):
    return jnp.dot(a.astype(BF16), b.astype(BF16), preferred_element_type=F32)


def _hdot(a, b):
    return jnp.dot(a, b, precision=HIGHEST, preferred_element_type=F32)


def _split3(t):
    a = t.astype(BF16)
    r = t - a.astype(F32)
    b = r.astype(BF16)
    return a, b, (r - b.astype(F32)).astype(BF16)


def _mod_kernel(c_ref, w_ref, b_ref, o_ref):
    o_ref[...] = _hdot(_silu(c_ref[...]), w_ref[...]) + b_ref[...]


def _modulation(cond8, w, b):
    depth, _, width = w.shape
    return pl.pallas_call(
        _mod_kernel,
        out_shape=jax.ShapeDtypeStruct((depth, SUBLANES, width), F32),
        grid=(depth, width // D_MODEL),
        in_specs=[pl.BlockSpec((SUBLANES, D_MODEL), lambda l, j: (0, 0)),
                  pl.BlockSpec((None, D_MODEL, D_MODEL), lambda l, j: (l, 0, j)),
                  pl.BlockSpec((None, 1, D_MODEL), lambda l, j: (l, 0, j))],
        out_specs=pl.BlockSpec((None, SUBLANES, D_MODEL), lambda l, j: (l, 0, j)),
        compiler_params=_params(("parallel", "parallel"), 32),
        name="modulation",
    )(cond8, w, b)


def _inproj_kernel(x_ref, mod_ref, g_ref, w_ref, o_ref, h_scr):
    @pl.when(pl.program_id(1) == 0)
    def _():
        h = _rms(x_ref[...]) * g_ref[...]
        h = h * (1.0 + mod_ref[0, 1:2, :]) + mod_ref[0, 0:1, :]
        h_scr[...] = h.astype(BF16)

    o_ref[...] = _qk(h_scr[...], w_ref[...]).astype(o_ref.dtype)


def _inproj(x, mod, g, w, tm, tn):
    rows = x.shape[0]
    per_mod = rows // mod.shape[0]
    tm = min(tm, per_mod)
    return pl.pallas_call(
        _inproj_kernel,
        out_shape=jax.ShapeDtypeStruct((rows, N_PROJ), BF16),
        grid=(rows // tm, N_PROJ // tn),
        in_specs=[pl.BlockSpec((tm, D_MODEL), lambda i, j: (i, 0)),
                  pl.BlockSpec((1, SUBLANES, D_MODEL), lambda i, j: (i * tm // per_mod, 0, 0)),
                  pl.BlockSpec((1, D_MODEL), lambda i, j: (0, 0)),
                  pl.BlockSpec((tn, D_MODEL), lambda i, j: (j, 0))],
        out_specs=pl.BlockSpec((tm, tn), lambda i, j: (i, j)),
        scratch_shapes=[pltpu.VMEM((tm, D_MODEL), BF16)],
        compiler_params=_params(("parallel", "arbitrary"), 48),
        name="inproj",
    )(x, mod, g, w)


def _softmax_pv(s, vb):
    m = jnp.max(s, axis=-1, keepdims=True)
    p = jnp.exp2(s - m)
    l = jnp.sum(p, axis=-1, keepdims=True)
    return jnp.dot(p.astype(BF16), vb, preferred_element_type=F32) / l


def _qk(qb, kb):
    return lax.dot_general(qb, kb, (((1,), (1,)), ((), ())), preferred_element_type=F32)


def _attn_ctx_kernel(q_ref, k_ref, v_ref, qg_ref, kg_ref, *rest):
    a_ref, ko_ref, vo_ref = rest[-3:]
    group, seq = ko_ref.shape[:2]
    for b in range(group):
        rows = slice(b * seq, (b + 1) * seq)
        v = v_ref[rows, :].astype(F32)
        for hk in range(N_KV_HEADS):
            ks = slice(hk * HEAD_DIM, (hk + 1) * HEAD_DIM)
            kn = _rms(k_ref[rows, ks].astype(F32)) * kg_ref[...]
            ko_ref[b, :, hk, :] = kn
            vo_ref[b, :, hk, :] = v[:, ks]
            knb = kn.astype(BF16)
            vb = v[:, ks].astype(BF16)
            for g in range(Q_PER_KV):
                hs = slice((hk * Q_PER_KV + g) * HEAD_DIM, (hk * Q_PER_KV + g + 1) * HEAD_DIM)
                qn = _rms(q_ref[rows, hs].astype(F32)) * (qg_ref[...] * Q_SCALE)
                s = _qk(qn.astype(BF16), knb)
                a_ref[rows, hs] = _softmax_pv(s, vb).astype(BF16)


def _attn_ctx(proj, qg, kg, nseq, seq, layer, depth, earlier):
    kvw = N_KV_HEADS * HEAD_DIM
    g = 2 if nseq % 2 == 0 else 1
    cache = jax.ShapeDtypeStruct((nseq, depth, seq, N_KV_HEADS, HEAD_DIM), F32)
    cache_spec = pl.BlockSpec((g, None, seq, N_KV_HEADS, HEAD_DIM), lambda b: (b, layer, 0, 0, 0))
    n_in = 5
    return pl.pallas_call(
        _attn_ctx_kernel,
        out_shape=(jax.ShapeDtypeStruct((nseq * seq, D_MODEL), BF16), cache, cache),
        grid=(nseq // g,),
        in_specs=[pl.BlockSpec((g * seq, D_MODEL), lambda b: (b, COL_Q // D_MODEL)),
                  pl.BlockSpec((g * seq, kvw), lambda b: (b, COL_K // kvw)),
                  pl.BlockSpec((g * seq, kvw), lambda b: (b, COL_V // kvw)),
                  pl.BlockSpec((1, HEAD_DIM), lambda b: (0, 0)),
                  pl.BlockSpec((1, HEAD_DIM), lambda b: (0, 0))] + [pl.BlockSpec(memory_space=pl.ANY)] * len(earlier),
        out_specs=(pl.BlockSpec((g * seq, D_MODEL), lambda b: (b, 0)), cache_spec, cache_spec),
        input_output_aliases={n_in + i: 1 + i for i in range(len(earlier))},
        compiler_params=_params(("parallel",), 32),
        name="attn_ctx",
    )(proj, proj, proj, qg, kg, *earlier)


def _rope(t, cos, sa, sb):
    return t * cos + pltpu.roll(t, 96, 1) * sa + pltpu.roll(t, 32, 1) * sb


def _prep_lat_kernel(k_ref, v_ref, ck_ref, cv_ref, cos_ref, sa_ref, sb_ref, kg_ref, ko_ref, vo_ref):
    t = pl.program_id(1)

    @pl.when(t == 0)
    def _():
        ko_ref[...] = ck_ref[...].astype(BF16)
        vo_ref[...] = cv_ref[...].astype(BF16)

    @pl.when(t > 0)
    def _():
        cos, sa, sb = cos_ref[...], sa_ref[...], sb_ref[...]
        vo_ref[...] = v_ref[...]
        for h in range(N_KV_HEADS):
            hs = slice(h * HEAD_DIM, (h + 1) * HEAD_DIM)
            ko_ref[:, hs] = _rope(_rms(k_ref[:, hs].astype(F32)) * kg_ref[...], cos, sa, sb).astype(BF16)


def _prep_lat(proj, cache_k, cache_v, cos, sa, sb, kg, nb, seq):
    past = cache_k.shape[1]
    tr = past
    kvw = N_KV_HEADS * HEAD_DIM
    nt = seq // tr
    prev = lambda t: jnp.maximum(t - 1, 0)
    return pl.pallas_call(
        _prep_lat_kernel,
        out_shape=(jax.ShapeDtypeStruct((nb, past + seq, kvw), BF16),
                   jax.ShapeDtypeStruct((nb, past + seq, kvw), BF16)),
        grid=(nb, nt + 1),
        in_specs=[pl.BlockSpec((tr, kvw), lambda b, t: (b * nt + prev(t), COL_K // kvw)),
                  pl.BlockSpec((tr, kvw), lambda b, t: (b * nt + prev(t), COL_V // kvw)),
                  pl.BlockSpec((None, past, kvw), lambda b, t: (b, 0, 0)),
                  pl.BlockSpec((None, past, kvw), lambda b, t: (b, 0, 0)),
                  pl.BlockSpec((tr, HEAD_DIM), lambda b, t: (prev(t), 0)),
                  pl.BlockSpec((tr, HEAD_DIM), lambda b, t: (prev(t), 0)),
                  pl.BlockSpec((tr, HEAD_DIM), lambda b, t: (prev(t), 0)),
                  pl.BlockSpec((1, HEAD_DIM), lambda b, t: (0, 0))],
        out_specs=(pl.BlockSpec((None, tr, kvw), lambda b, t: (b, t, 0)),
                   pl.BlockSpec((None, tr, kvw), lambda b, t: (b, t, 0))),
        compiler_params=_params(("parallel", "arbitrary"), 32),
        name="prep_lat",
    )(proj, proj, cache_k, cache_v, cos, sa, sb, kg)


def _attn_lat_kernel(q_ref, cos_ref, sa_ref, sb_ref, qg_ref, k_ref, v_ref, o_ref):
    cos, sa, sb = cos_ref[...], sa_ref[...], sb_ref[...]
    gain = qg_ref[...] * Q_SCALE
    for hk in range(N_KV_HEADS):
        kb = k_ref[:, hk * HEAD_DIM:(hk + 1) * HEAD_DIM]
        vb = v_ref[:, hk * HEAD_DIM:(hk + 1) * HEAD_DIM]
        for g in range(Q_PER_KV):
            hs = slice((hk * Q_PER_KV + g) * HEAD_DIM, (hk * Q_PER_KV + g + 1) * HEAD_DIM)
            qn = _rope(_rms(q_ref[:, hs].astype(F32)) * gain, cos, sa, sb)
            s = _qk(qn.astype(BF16), kb)
            o_ref[:, hs] = _softmax_pv(s, vb).astype(BF16)


def _attn_lat(proj, kall, vall, cos, sa, sb, qg, nb, seq, tq):
    nk, kvw = kall.shape[1:]
    nq = seq // tq
    table = pl.BlockSpec((tq, HEAD_DIM), lambda b, i: (i, 0))
    return pl.pallas_call(
        _attn_lat_kernel,
        out_shape=jax.ShapeDtypeStruct((nb * seq, D_MODEL), BF16),
        grid=(nb, nq),
        in_specs=[pl.BlockSpec((tq, D_MODEL), lambda b, i: (b * nq + i, COL_Q // D_MODEL)),
                  table, table, table, pl.BlockSpec((1, HEAD_DIM), lambda b, i: (0, 0)),
                  pl.BlockSpec((None, nk, kvw), lambda b, i: (b, 0, 0)),
                  pl.BlockSpec((None, nk, kvw), lambda b, i: (b, 0, 0))],
        out_specs=pl.BlockSpec((tq, D_MODEL), lambda b, i: (b * nq + i, 0)),
        compiler_params=_params(("parallel", "arbitrary"), 48),
        name="attn_lat",
    )(proj, cos, sa, sb, qg, kall, vall)


SSD_GROUP = 4


def _ssd_kernel(*refs, reverse, has_h0, nc):
    it = iter(refs)
    if reverse:
        xsc_ref, bcc_ref, dt_ref, alog_ref, dtb_ref, e_ref = (next(it) for _ in range(6))
    else:
        xs_ref, bc_ref, dt_ref, xsp_ref, xsn_ref, bcp_ref, bcn_ref = (next(it) for _ in range(7))
        cwx_ref, cbx_ref, cwb_ref, cbb_ref, alog_ref, dtb_ref, e_ref, shift_ref = (next(it) for _ in range(8))
    h0_ref = next(it) if has_h0 else None
    if reverse:
        z_ref, yf_ref, d_ref, g_ref = (next(it) for _ in range(4))
        y_ref, hout_ref, ht_scr = refs[-3:]
    else:
        y_ref, xsc_ref, bcc_ref, hout_ref, ht_scr = refs[-5:]

    q = SSM_CHUNK
    npair = SSM_HEADS // 2
    group = y_ref.shape[0]
    c = pl.program_id(1)
    cc = nc - 1 - c if reverse else c

    @pl.when(c == 0)
    def _():
        for s in range(group):
            for p in range(npair):
                if has_h0:
                    ht_scr[s, p] = h0_ref[s, p].T
                else:
                    ht_scr[s, p] = jnp.zeros((q, q), F32)

    off = SSM_HEADS if reverse else 0
    ii = lax.broadcasted_iota(jnp.int32, (q, q), 0)
    jj = lax.broadcasted_iota(jnp.int32, (q, q), 1)
    tri = (jj >= ii) if reverse else (ii >= jj)
    tri_b = jnp.where(tri, 1.0, 0.0).astype(BF16)
    tri_tb3 = jnp.concatenate([jnp.where((ii >= jj) if reverse else (jj >= ii), 1.0, 0.0).astype(BF16)] * 3, axis=0)
    lo = lax.broadcasted_iota(jnp.int32, (q, LANES), 1) < SSM_HEADDIM
    neg_a = -jnp.exp(alog_ref[...])
    if not reverse:
        keep_prev = jnp.where(cc == 0, 0.0, 1.0).astype(BF16)
        keep_next = jnp.where(cc == nc - 1, 0.0, 1.0).astype(BF16)

    def conv_silu(s, x_ref_, p_ref_, n_ref_, w_ref_, b_ref_):
        x = x_ref_[s]
        ext = jnp.concatenate([x, p_ref_[s] * keep_prev, n_ref_[s] * keep_next], axis=0)
        nb = jnp.dot(shift_ref[...], ext, preferred_element_type=F32)
        w = w_ref_[...]
        return _silu(w[0:1, :] * nb[:q, :] + w[1:2, :] * x.astype(F32) + w[2:3, :] * nb[q:, :] + b_ref_[...])

    for s in range(group):
        if reverse:
            xs = xsc_ref[s].astype(F32)
            bc = bcc_ref[s].astype(F32)
        else:
            xs = conv_silu(s, xs_ref, xsp_ref, xsn_ref, cwx_ref, cbx_ref)
            bc = conv_silu(s, bc_ref, bcp_ref, bcn_ref, cwb_ref, cbb_ref)
            xsc_ref[s] = xs.astype(BF16)
            bcc_ref[s] = bc.astype(BF16)

        dt = jax.nn.softplus(dt_ref[s].astype(F32) + dtb_ref[...])
        da = dt * neg_a
        acs3 = jnp.dot(tri_b, jnp.concatenate(_split3(da), axis=1), preferred_element_type=F32)
        acs = acs3[:, :LANES] + acs3[:, LANES:2 * LANES] + acs3[:, 2 * LANES:]
        acs_t = jnp.dot(jnp.concatenate(_split3(da.T), axis=1), tri_tb3, preferred_element_type=F32)
        spread = jnp.dot(jnp.concatenate(_split3(jnp.concatenate([dt, acs], axis=0)), axis=1), e_ref[...],
                         preferred_element_type=F32)
        acs_x = spread[q:, :]
        tot_x = acs_x[0:1, :] if reverse else acs_x[q - 1:q, :]
        xdt = xs * spread[:q, :]
        xw = xdt * jnp.exp(tot_x - acs_x)
        eacs_x = jnp.exp(acs_x)
        etot_x = jnp.exp(tot_x)

        parts = []
        for g in range(2):
            bg = bc[:, g * q:(g + 1) * q]
            cg = bc[:, (2 + g) * q:(3 + g) * q].astype(BF16)
            cb = _qk(cg, bg.astype(BF16))
            bgt = bg.T.astype(BF16)
            for pp in range(npair // 2):
                p = g * (npair // 2) + pp
                ps = slice(p * LANES, (p + 1) * LANES)
                ms = []
                for hh in range(2):
                    ln = off + 2 * p + hh
                    dec = jnp.where(tri, jnp.exp(acs[:, ln:ln + 1] - acs_t[ln:ln + 1, :]), 0.0)
                    ms.append((cb * dec).astype(BF16))
                m = jnp.concatenate(ms, axis=1)
                xp = xdt[:, ps]
                bd = jnp.concatenate([jnp.where(lo, xp, 0.0), jnp.where(lo, 0.0, xp)], axis=0).astype(BF16)
                ht = ht_scr[s, p]
                yp = jnp.dot(m, bd, preferred_element_type=F32)
                yp = yp + jnp.dot(cg, ht.astype(BF16), preferred_element_type=F32) * eacs_x[:, ps]
                ht_scr[s, p] = etot_x[:, ps] * ht + jnp.dot(bgt, xw[:, ps].astype(BF16),
                                                            preferred_element_type=F32)
                parts.append(yp)
        y = jnp.concatenate(parts, axis=1)

        if reverse:
            y = y + yf_ref[s] + d_ref[...] * xs
            y = y * _silu(z_ref[s].astype(F32))
            y_ref[s] = (_rms(y) * g_ref[...]).astype(y_ref.dtype)
        else:
            y_ref[s] = y

    @pl.when(c == nc - 1)
    def _():
        for s in range(group):
            for p in range(npair):
                hout_ref[s, p] = ht_scr[s, p].T


def _ssd_pass(proj, lp, d, nseq, seq, h0, fwd=None, layer=0, depth=1, earlier=None):
    reverse = fwd is not None
    has_h0 = h0 is not None
    q = SSM_CHUNK
    nc = seq // q
    g = min(SSD_GROUP, nseq)
    assert nseq % g == 0
    npair = SSM_HEADS // 2
    hb = q // HALO
    bcw = 4 * SSM_STATE
    chunk = (lambda c: nc - 1 - c) if reverse else (lambda c: c)
    prev8 = lambda c: jnp.maximum(chunk(c) * hb - 1, 0)
    next8 = lambda c: jnp.minimum((chunk(c) + 1) * hb, seq // HALO - 1)
    const = lambda s, c: (0, 0)
    tile = lambda width, col=0: pl.BlockSpec((g, q, width), lambda s, c: (s, chunk(c), col // width))
    halo = lambda width, col, where: pl.BlockSpec((g, HALO, width), lambda s, c: (s, where(c), col // width))
    seqs = lambda t: t.reshape(nseq, seq, t.shape[-1])
    scan_consts = [pl.BlockSpec((1, LANES), const), pl.BlockSpec((1, LANES), const),
                   pl.BlockSpec((3 * LANES, D_MODEL), const)]
    scan_args = [lp["alog"][d], lp["dtb"][d], lp["expand"][d]]
    proj3 = seqs(proj)
    if reverse:
        yf, xsc, bcc = fwd
        in_specs = [tile(D_MODEL), tile(bcw), tile(LANES, COL_DT)] + scan_consts
        args = [xsc, bcc, proj3] + scan_args
    else:
        in_specs = [tile(D_MODEL, COL_XS), tile(bcw, COL_BC), tile(LANES, COL_DT),
                    halo(D_MODEL, COL_XS, prev8), halo(D_MODEL, COL_XS, next8),
                    halo(bcw, COL_BC, prev8), halo(bcw, COL_BC, next8),
                    pl.BlockSpec((3, D_MODEL), const), pl.BlockSpec((1, D_MODEL), const),
                    pl.BlockSpec((3, bcw), const), pl.BlockSpec((1, bcw), const)] + scan_consts + [
                    pl.BlockSpec((2 * q, q + 2 * HALO), const)]
        args = [proj3] * 7 + [lp["cw_x"], lp["cb_x"], lp["cw_bc"], lp["cb_bc"]] + scan_args + [lp["shift"]]
    if has_h0:
        in_specs.append(pl.BlockSpec((g, npair, q, q), lambda s, c: (s, 0, 0, 0)))
        args.append(h0)
    if reverse:
        in_specs += [tile(D_MODEL, COL_Z), tile(D_MODEL), pl.BlockSpec((1, D_MODEL), const),
                     pl.BlockSpec((1, D_MODEL), const)]
        args += [proj3, yf, lp["d_row"], lp["ssm_norm_g"]]
    state = jax.ShapeDtypeStruct((nseq, depth, npair, q, q), F32)
    state_spec = pl.BlockSpec((g, None, npair, q, q), lambda s, c: (s, layer, 0, 0, 0))
    if reverse:
        out_shape = (jax.ShapeDtypeStruct((nseq, seq, D_MODEL), BF16), state)
        out_specs = (tile(D_MODEL), state_spec)
    else:
        out_shape = (jax.ShapeDtypeStruct((nseq, seq, D_MODEL), F32),
                     jax.ShapeDtypeStruct((nseq, seq, D_MODEL), BF16),
                     jax.ShapeDtypeStruct((nseq, seq, bcw), BF16), state)
        out_specs = (tile(D_MODEL), tile(D_MODEL), tile(bcw), state_spec)
    aliases = {}
    if earlier is not None:
        aliases = {len(args): len(out_shape) - 1}
        in_specs.append(pl.BlockSpec(memory_space=pl.ANY))
        args.append(earlier)
    outs = pl.pallas_call(
        functools.partial(_ssd_kernel, reverse=reverse, has_h0=has_h0, nc=nc),
        out_shape=out_shape,
        grid=(nseq // g, nc),
        in_specs=in_specs,
        out_specs=out_specs,
        input_output_aliases=aliases,
        scratch_shapes=[pltpu.VMEM((g, npair, q, q), F32)],
        compiler_params=_params(("parallel", "arbitrary"), 48),
        name="ssd_bwd" if reverse else "ssd_fwd",
    )(*args)
    if reverse:
        return outs[0].reshape(nseq * seq, D_MODEL), outs[1]
    return outs


ROUTE_I1, ROUTE_I2, ROUTE_P1, ROUTE_P2, ROUTE_R1, ROUTE_R2 = 0, 1, 2, 3, 4, 5


def _pack_bf16_pairs(v):
    n = v.shape[1] // 2
    bits = lax.bitcast_convert_type(v.astype(BF16).astype(F32), jnp.uint32)
    return lax.bitcast_convert_type((bits[:, :n] >> 16) | bits[:, n:], jnp.int32)


def _unpack_bf16_pairs(p):
    u = lax.bitcast_convert_type(p, jnp.uint32)
    lo = lax.bitcast_convert_type(u << 16, F32)
    hi = lax.bitcast_convert_type(u & jnp.uint32(0xFFFF0000), F32)
    return jnp.concatenate([lo, hi], axis=1)


def _top2_route(logits):
    tm = logits.shape[0]
    lane = lax.broadcasted_iota(jnp.int32, logits.shape, 1).astype(F32)
    lg = jnp.where(lane < N_EXPERTS, logits, -jnp.inf)
    m1 = jnp.max(lg, axis=-1, keepdims=True)
    i1 = jnp.min(jnp.where(lg == m1, lane, float(LANES)), axis=-1, keepdims=True)
    lg2 = jnp.where(lane == i1, -jnp.inf, lg)
    m2 = jnp.max(lg2, axis=-1, keepdims=True)
    i2 = jnp.min(jnp.where(lg2 == m2, lane, float(LANES)), axis=-1, keepdims=True)
    e = jnp.exp(m2 - m1)
    p1 = 1.0 / (1.0 + e)
    chosen = jnp.where(lane == i1, 1.0, jnp.where(lane == i2, 1.0, 0.0))
    earlier = (lax.broadcasted_iota(jnp.int32, (tm, tm), 0) > lax.broadcasted_iota(jnp.int32, (tm, tm), 1))
    ranks = _bdot(jnp.where(earlier, 1.0, 0.0), chosen)
    r1 = jnp.sum(jnp.where(lane == i1, ranks, 0.0), axis=-1, keepdims=True)
    r2 = jnp.sum(jnp.where(lane == i2, ranks, 0.0), axis=-1, keepdims=True)
    rec = jnp.where(lane == ROUTE_I1, i1, jnp.where(lane == ROUTE_I2, i2, 0.0))
    rec = jnp.where(lane == ROUTE_P1, p1, jnp.where(lane == ROUTE_P2, e * p1, rec))
    rec = jnp.where(lane == ROUTE_R1, r1, jnp.where(lane == ROUTE_R2, r2, rec))
    return rec, jnp.sum(chosen, axis=0, keepdims=True)


def _merge_kernel(*refs, seq, tm, moe):
    it = iter(refs)
    attn_ref, ssm_ref, scb_ref, scc_ref, sch_ref = (next(it) for _ in range(5))
    sccp_ref, sccn_ref, schp_ref, schn_ref = (next(it) for _ in range(4))
    g0_ref, g1_ref, g2_ref, x_ref, mod_ref = (next(it) for _ in range(5))
    wa_ref, ws_ref, wc_ref, wm_ref, scw_ref, ng_ref = (next(it) for _ in range(6))
    router_ref = next(it) if moe else None
    xo_ref, h2_ref = next(it), next(it)
    route_ref, cnt_ref = (next(it), next(it)) if moe else (None, None)

    i = pl.program_id(0)
    pos = (lax.broadcasted_iota(jnp.int32, (tm, 1), 0) + i * tm) & (seq - 1)
    u = scc_ref[...].astype(F32) * sch_ref[...].astype(F32)
    up = sccp_ref[HALO - 1:HALO, :].astype(F32) * schp_ref[HALO - 1:HALO, :].astype(F32)
    un = sccn_ref[0:1, :].astype(F32) * schn_ref[0:1, :].astype(F32)
    r = lax.broadcasted_iota(jnp.int32, (tm, 1), 0)
    um1 = jnp.where(pos == 0, 0.0, jnp.where(r == 0, up, pltpu.roll(u, 1, 0)))
    up1 = jnp.where(pos == seq - 1, 0.0, jnp.where(r == tm - 1, un, pltpu.roll(u, tm - 1, 0)))
    w = scw_ref[...]
    sc = scb_ref[...].astype(F32) * (w[0:1, :] * um1 + w[1:2, :] * u + w[2:3, :] * up1)

    gate = lambda g_ref: _sigmoid(g_ref[...].astype(F32))
    merged = gate(g0_ref) * jnp.dot(attn_ref[...], wa_ref[...], preferred_element_type=F32)
    merged = merged + gate(g1_ref) * jnp.dot(ssm_ref[...], ws_ref[...], preferred_element_type=F32)
    merged = merged + gate(g2_ref) * jnp.dot(sc.astype(BF16), wc_ref[...], preferred_element_type=F32)
    mod = mod_ref[0]
    xn = x_ref[...] + mod[2:3, :] * jnp.dot(merged.astype(BF16), wm_ref[...], preferred_element_type=F32)
    xo_ref[...] = xn
    h2 = _rms(xn) * ng_ref[...] * (1.0 + mod[4:5, :]) + mod[3:4, :]
    if moe:
        h2_ref[...] = _pack_bf16_pairs(h2)
        h_hi = h2.astype(BF16)
        h_lo = (h2 - h_hi.astype(F32)).astype(BF16)
        both = jnp.dot(h_hi, router_ref[...], preferred_element_type=F32)
        logits = both[:, :LANES] + both[:, LANES:] + jnp.dot(h_lo, router_ref[:, :LANES], preferred_element_type=F32)
        rec, counts = _top2_route(logits)
        route_ref[...] = rec
        cnt_ref[...] = jnp.broadcast_to(counts, cnt_ref.shape)
    else:
        h2_ref[...] = h2.astype(BF16)


def _merge(attn, ssm, proj, x, mod, lp, seq, tm):
    rows = x.shape[0]
    per_mod = rows // mod.shape[0]
    tm = min(tm, per_mod)
    moe = lp["router"] is not None
    hb = tm // HALO
    prev8 = lambda i: jnp.maximum(i * hb - 1, 0)
    next8 = lambda i: jnp.minimum((i + 1) * hb, rows // HALO - 1)
    tile = lambda col: pl.BlockSpec((tm, D_MODEL), lambda i: (i, col // D_MODEL))
    const = lambda shape: pl.BlockSpec(shape, lambda i: (0, 0))
    in_specs = [tile(0), tile(0), tile(COL_SCB), tile(COL_SCC), tile(COL_SCH),
                pl.BlockSpec((HALO, D_MODEL), lambda i: (prev8(i), COL_SCC // D_MODEL)),
                pl.BlockSpec((HALO, D_MODEL), lambda i: (next8(i), COL_SCC // D_MODEL)),
                pl.BlockSpec((HALO, D_MODEL), lambda i: (prev8(i), COL_SCH // D_MODEL)),
                pl.BlockSpec((HALO, D_MODEL), lambda i: (next8(i), COL_SCH // D_MODEL)),
                tile(COL_G), tile(COL_G + D_MODEL), tile(COL_G + 2 * D_MODEL), tile(0),
                pl.BlockSpec((1, SUBLANES, D_MODEL), lambda i: (i * tm // per_mod, 0, 0)),
                _resident((D_MODEL, D_MODEL)), _resident((D_MODEL, D_MODEL)), _resident((D_MODEL, D_MODEL)),
                _resident((D_MODEL, D_MODEL)), const((3, D_MODEL)), const((1, D_MODEL))]
    args = [attn, ssm, proj, proj, proj, proj, proj, proj, proj, proj, proj, proj, x, mod,
            lp["w_attn_out"], lp["w_ssm_out"], lp["w_sconv_out"], lp["w_merge"], lp["sconv_w"], lp["norm_ffn_g"]]
    out_shape = [jax.ShapeDtypeStruct((rows, D_MODEL), F32), jax.ShapeDtypeStruct((rows, D_MODEL), BF16)]
    out_specs = [tile(0), tile(0)]
    if moe:
        in_specs.append(_resident((D_MODEL, 2 * LANES)))
        args.append(lp["router"])
        out_shape[1] = jax.ShapeDtypeStruct((rows, D_MODEL // 2), jnp.int32)
        out_specs[1] = pl.BlockSpec((tm, D_MODEL // 2), lambda i: (i, 0))
        out_shape += [jax.ShapeDtypeStruct((rows, LANES), F32),
                      jax.ShapeDtypeStruct((rows // tm * SUBLANES, LANES), F32)]
        out_specs += [pl.BlockSpec((tm, LANES), lambda i: (i, 0)), pl.BlockSpec((SUBLANES, LANES), lambda i: (i, 0))]
    return pl.pallas_call(
        functools.partial(_merge_kernel, seq=seq, tm=tm, moe=moe),
        out_shape=tuple(out_shape),
        grid=(rows // tm,),
        in_specs=in_specs,
        out_specs=tuple(out_specs),
        compiler_params=_params(("parallel",), 56),
        name="merge",
    )(*args)


def _swiglu_acc(h, w1_ref, w3_ref, w2_ref, fs):
    a = _silu(jnp.dot(h, w1_ref[:, fs], preferred_element_type=F32)) * jnp.dot(h, w3_ref[:, fs],
                                                                                preferred_element_type=F32)
    return jnp.dot(a.astype(BF16), w2_ref[fs, :], preferred_element_type=F32)


def _swiglu(h, w1_ref, w3_ref, w2_ref):
    cut = (D_FF // 2 + MXU_WIDTH - 1) // MXU_WIDTH * MXU_WIDTH
    return (_swiglu_acc(h, w1_ref, w3_ref, w2_ref, slice(0, cut))
            + _swiglu_acc(h, w1_ref, w3_ref, w2_ref, slice(cut, D_FF)))


def _ffn_kernel(h_ref, x_ref, mod_ref, w1_ref, w3_ref, w2_ref, o_ref):
    o_ref[...] = x_ref[...] + mod_ref[0, 5:6, :] * _swiglu(h_ref[...], w1_ref, w3_ref, w2_ref)


def _resident(shape):
    return pl.BlockSpec(shape, lambda *_: (0,) * len(shape), pipeline_mode=pl.Buffered(1))


def _ffn(h2, x, mod, w1, w3, w2, tm):
    rows = x.shape[0]
    per_mod = rows // mod.shape[0]
    tm = min(tm, per_mod)
    return pl.pallas_call(
        _ffn_kernel,
        out_shape=jax.ShapeDtypeStruct((rows, D_MODEL), F32),
        grid=(rows // tm,),
        in_specs=[pl.BlockSpec((tm, D_MODEL), lambda i: (i, 0)),
                  pl.BlockSpec((tm, D_MODEL), lambda i: (i, 0)),
                  pl.BlockSpec((1, SUBLANES, D_MODEL), lambda i: (i * tm // per_mod, 0, 0)),
                  _resident((D_MODEL, D_FF)), _resident((D_MODEL, D_FF)), _resident((D_FF, D_MODEL))],
        out_specs=pl.BlockSpec((tm, D_MODEL), lambda i: (i, 0)),
        compiler_params=_params(("parallel",), 56),
        name="ffn",
    )(h2, x, mod, w1, w3, w2)


def _cast_kernel(*refs):
    n = len(refs) // 2
    for src, dst in zip(refs[:n], refs[n:]):
        dst[...] = src[...].astype(dst.dtype)


def _cast_bf16(ws):
    e, r, c = ws[0].shape
    rb = max(d for d in range(16, r + 1, 16) if r % d == 0 and d * c * 4 <= 3 * MIB)
    spec = pl.BlockSpec((None, rb, c), lambda i, j: (i, j, 0))
    return pl.pallas_call(
        _cast_kernel,
        out_shape=tuple(jax.ShapeDtypeStruct(w.shape, BF16) for w in ws),
        grid=(e, r // rb),
        in_specs=[spec] * len(ws),
        out_specs=tuple([spec] * len(ws)),
        compiler_params=_params(("parallel", "parallel"), 48),
        name="cast_bf16",
    )(*ws)


SC_CORES = 2
SC_SUBCORES = 16
SC_CHUNK = 128


def _sc_gather(table, idx):
    nw = SC_CORES * SC_SUBCORES
    b, d = idx.shape[0], table.shape[1]
    per_w = b // nw
    assert per_w * nw == b and per_w % SC_CHUNK == 0
    mesh = plsc.VectorSubcoreMesh(core_axis_name="c", subcore_axis_name="s")

    def body(table_hbm, idx_hbm, out_hbm, idx_v, rows_v, sem):
        wid = lax.axis_index("s") * SC_CORES + lax.axis_index("c")

        @pl.loop(0, per_w // SC_CHUNK)
        def _(j):
            off = pl.multiple_of(wid * per_w + j * SC_CHUNK, SC_CHUNK)
            pltpu.sync_copy(idx_hbm.at[pl.ds(off, SC_CHUNK)], idx_v)
            pltpu.async_copy(table_hbm.at[idx_v], rows_v, sem).wait()
            pltpu.sync_copy(rows_v, out_hbm.at[pl.ds(off, SC_CHUNK)])

    return pl.kernel(
        body,
        out_type=jax.ShapeDtypeStruct((b, d), table.dtype),
        mesh=mesh,
        scratch_types=[pltpu.VMEM((SC_CHUNK,), jnp.int32), pltpu.VMEM((SC_CHUNK, d), table.dtype),
                       pltpu.SemaphoreType.DMA],
        name="sc_gather",
    )(table, idx)


def _sc_dispatch(table, slot, n_slots):
    nw = SC_CORES * SC_SUBCORES
    t, d = table.shape
    per_w = t // nw
    assert per_w * nw == t and per_w % SC_CHUNK == 0 and slot.shape[0] == 2 * t
    mesh = plsc.VectorSubcoreMesh(core_axis_name="c", subcore_axis_name="s")

    def body(table_hbm, slot_hbm, out_hbm, idx_v, rows_v, sem):
        wid = lax.axis_index("s") * SC_CORES + lax.axis_index("c")

        @pl.loop(0, per_w // SC_CHUNK)
        def _(j):
            off = pl.multiple_of(wid * per_w + j * SC_CHUNK, SC_CHUNK)
            pltpu.sync_copy(table_hbm.at[pl.ds(off, SC_CHUNK)], rows_v)
            for k in range(2):
                pltpu.sync_copy(slot_hbm.at[pl.ds(pl.multiple_of(k * t + off, SC_CHUNK), SC_CHUNK)], idx_v)
                pltpu.async_copy(rows_v, out_hbm.at[idx_v], sem).wait()

    return pl.kernel(
        body,
        out_type=jax.ShapeDtypeStruct((n_slots, d), table.dtype),
        mesh=mesh,
        scratch_types=[pltpu.VMEM((SC_CHUNK,), jnp.int32), pltpu.VMEM((SC_CHUNK, d), table.dtype),
                       pltpu.SemaphoreType.DMA],
        name="sc_dispatch",
    )(table, slot)


def _route_tables(route, counts, tile):
    t = route.shape[0]
    n_mt = counts.shape[0] // SUBLANES
    cnt = counts.reshape(n_mt, SUBLANES, LANES)[:, 0, :N_EXPERTS].astype(jnp.int32)
    incl = jnp.cumsum(cnt, axis=0)
    padded = (incl[-1] + tile - 1) // tile * tile
    gend = jnp.cumsum(padded)
    base = jnp.repeat((gend - padded)[None, :] + incl - cnt, t // n_mt, axis=0)
    e = route[:, ROUTE_I1:ROUTE_I2 + 1].astype(jnp.int32)
    r = route[:, ROUTE_R1:ROUTE_R2 + 1].astype(jnp.int32)
    pick = e[:, :, None] == jnp.arange(N_EXPERTS)[None, None, :]
    slot = jnp.sum(jnp.where(pick, base[:, None, :], 0), axis=-1) + r
    n_slots = (2 * t + N_EXPERTS * (tile - 1)) // tile * tile
    n_valid = gend[-1] // tile
    n_tiles = n_slots // tile
    tile_expert = jnp.sum(jnp.arange(n_tiles)[:, None] * tile >= gend[None, :], axis=1)
    last = jnp.take(tile_expert, jnp.maximum(n_valid - 1, 0))
    tile_expert = jnp.where(jnp.arange(n_tiles) < n_valid, tile_expert, last).astype(jnp.int32)
    return slot.T.reshape(-1).astype(jnp.int32), n_slots, tile_expert, n_valid.reshape(1).astype(jnp.int32)


def _moe_ffn_kernel(te_ref, nv_ref, h_ref, w1_ref, w3_ref, w2_ref, o_ref):
    i = pl.program_id(0)

    @pl.when(i < nv_ref[0])
    def _():
        h = _unpack_bf16_pairs(h_ref[...]).astype(BF16)
        o_ref[...] = _pack_bf16_pairs(_swiglu(h, w1_ref, w3_ref, w2_ref))

    @pl.when(i >= nv_ref[0])
    def _():
        o_ref[...] = jnp.zeros(o_ref.shape, jnp.int32)


def _moe_ffn(hs, tile_expert, n_valid, w1, w3, w2, tile):
    n_slots = hs.shape[0]
    grid_spec = pltpu.PrefetchScalarGridSpec(
        num_scalar_prefetch=2,
        grid=(n_slots // tile,),
        in_specs=[pl.BlockSpec((tile, D_MODEL // 2), lambda i, te, nv: (i, 0)),
                  pl.BlockSpec((None, D_MODEL, D_FF), lambda i, te, nv: (te[i], 0, 0)),
                  pl.BlockSpec((None, D_MODEL, D_FF), lambda i, te, nv: (te[i], 0, 0)),
                  pl.BlockSpec((None, D_FF, D_MODEL), lambda i, te, nv: (te[i], 0, 0))],
        out_specs=pl.BlockSpec((tile, D_MODEL // 2), lambda i, te, nv: (i, 0)),
    )
    return pl.pallas_call(
        _moe_ffn_kernel,
        out_shape=jax.ShapeDtypeStruct((n_slots, D_MODEL // 2), jnp.int32),
        grid_spec=grid_spec,
        compiler_params=_params(("arbitrary",), 56),
        name="moe_ffn",
    )(tile_expert, n_valid, hs, w1, w3, w2)


def _combine_kernel(x_ref, y1_ref, y2_ref, route_ref, mod_ref, o_ref):
    p1 = route_ref[:, ROUTE_P1:ROUTE_P1 + 1]
    p2 = route_ref[:, ROUTE_P2:ROUTE_P2 + 1]
    f = p1 * _unpack_bf16_pairs(y1_ref[...]) + p2 * _unpack_bf16_pairs(y2_ref[...])
    o_ref[...] = x_ref[...] + mod_ref[0, 5:6, :] * f


def _combine(x, yg, route, mod, tm):
    rows = x.shape[0]
    per_mod = rows // mod.shape[0]
    tm = min(tm, per_mod)
    return pl.pallas_call(
        _combine_kernel,
        out_shape=jax.ShapeDtypeStruct((rows, D_MODEL), F32),
        grid=(rows // tm,),
        in_specs=[pl.BlockSpec((tm, D_MODEL), lambda i: (i, 0)),
                  pl.BlockSpec((tm, D_MODEL // 2), lambda i: (i, 0)),
                  pl.BlockSpec((tm, D_MODEL // 2), lambda i: (i + rows // tm, 0)),
                  pl.BlockSpec((tm, LANES), lambda i: (i, 0)),
                  pl.BlockSpec((1, SUBLANES, D_MODEL), lambda i: (i * tm // per_mod, 0, 0))],
        out_specs=pl.BlockSpec((tm, D_MODEL), lambda i: (i, 0)),
        compiler_params=_params(("parallel",), 32),
        name="moe_combine",
    )(x, yg, yg, route, mod)


def _moe(h2p, x, mod, route, counts, w1, w3, w2, tile):
    slot, n_slots, tile_expert, n_valid = _route_tables(route, counts, tile)
    hs = _sc_dispatch(h2p, slot, n_slots)
    ys = _moe_ffn(hs, tile_expert, n_valid, w1, w3, w2, tile)
    return _combine(x, _sc_gather(ys, slot), route, mod, tm=512)


W_IN_SEGMENTS = ((0, 1024), (1536, 1024), (4128, 1024), (5152, 1024), (6176, 1024), (7200, 3072),
                 (2560, 1024), (3584, 512), (1024, 256), (1280, 256), (4096, 32))


def _permute_kernel(w_ref, o_ref):
    dst = 0
    for src, n in W_IN_SEGMENTS:
        o_ref[dst:dst + n, :] = w_ref[src:src + n, :].astype(BF16)
        dst += n
    o_ref[dst:, :] = jnp.zeros((o_ref.shape[0] - dst, o_ref.shape[1]), BF16)


def _permute_w_in(w_in, l, cb=128):
    w_t = jnp.swapaxes(w_in, 1, 2)
    _, feats, d = w_t.shape
    return pl.pallas_call(
        _permute_kernel,
        out_shape=jax.ShapeDtypeStruct((N_PROJ, d), BF16),
        grid=(d // cb,),
        in_specs=[pl.BlockSpec((None, feats, cb), lambda i: (l, 0, i))],
        out_specs=pl.BlockSpec((N_PROJ, cb), lambda i: (0, i)),
        compiler_params=_params(("parallel",), 48),
        name="permute_w_in",
    )(w_t)


def _layer_params(l, w_in, q_norm_g, k_norm_g, w_attn_out, ssm_conv_w, ssm_conv_b, ssm_dt_bias, ssm_a_log,
                  ssm_d, ssm_norm_g, w_ssm_out, sconv_w, w_sconv_out, w_merge, norm_mix_g, norm_ffn_g,
                  ffn_w1, ffn_w3, ffn_w2, moe_router, moe_w1, moe_w3, moe_w2):
    w_perm = _permute_w_in(w_in, l)
    row = lambda v: v.reshape(1, -1)
    lane_row = lambda v, off: jnp.zeros((1, LANES), F32).at[0, off:off + SSM_HEADS].set(v)
    head_of_col = np.arange(D_MODEL) // SSM_HEADDIM
    expand = [jnp.asarray(np.tile(np.arange(LANES)[:, None] == head_of_col[None, :] + off, (3, 1)), BF16)
              for off in (0, SSM_HEADS)]
    r = np.arange(2 * SSM_CHUNK)[:, None]
    cidx = np.arange(SSM_CHUNK + 2 * HALO)[None, :]
    up = (r < SSM_CHUNK) & (((cidx == r - 1) & (r >= 1)) | ((r == 0) & (cidx == SSM_CHUNK + HALO - 1)))
    dn = (r >= SSM_CHUNK) & (((cidx == r - SSM_CHUNK + 1) & (r < 2 * SSM_CHUNK - 1))
                             | ((r == 2 * SSM_CHUNK - 1) & (cidx == SSM_CHUNK + HALO)))
    shift = jnp.asarray(up | dn, BF16)
    lp = {
        "w_in": w_perm,
        "norm_mix_g": row(norm_mix_g[l]), "norm_ffn_g": row(norm_ffn_g[l]),
        "q_norm_g": row(q_norm_g[l]), "k_norm_g": row(k_norm_g[l]),
        "w_attn_out": w_attn_out[l].astype(BF16), "w_ssm_out": w_ssm_out[l].astype(BF16),
        "w_sconv_out": w_sconv_out[l].astype(BF16), "w_merge": w_merge[l].astype(BF16),
        "cw_x": ssm_conv_w[l][:, :D_MODEL], "cw_bc": ssm_conv_w[l][:, D_MODEL:],
        "cb_x": row(ssm_conv_b[l][:D_MODEL]), "cb_bc": row(ssm_conv_b[l][D_MODEL:]),
        "alog": [lane_row(ssm_a_log[l, d], d * SSM_HEADS) for d in range(2)],
        "dtb": [lane_row(ssm_dt_bias[l, d], d * SSM_HEADS) for d in range(2)],
        "expand": expand,
        "shift": shift,
        "d_row": row(jnp.repeat(ssm_d[l], SSM_HEADDIM)),
        "ssm_norm_g": row(ssm_norm_g[l]),
        "sconv_w": sconv_w[l],
    }
    if l % 2 == 0:
        lp["router"] = None
        lp["ffn"] = tuple(t[l // 2].astype(BF16) for t in (ffn_w1, ffn_w3, ffn_w2))
    else:
        router = jnp.pad(moe_router[l // 2], ((0, 0), (0, LANES - N_EXPERTS)))
        r_hi = router.astype(BF16)
        lp["router"] = jnp.concatenate([r_hi, (router - r_hi.astype(F32)).astype(BF16)], axis=1)
        lp["ffn"] = _cast_bf16((moe_w1[l // 2], moe_w3[l // 2])) + _cast_bf16((moe_w2[l // 2],))
    return lp


def _rope_tables(seq):
    t = np.arange(seq)
    n = HEAD_DIM // 4
    inv = ROPE_THETA ** (-np.arange(n, dtype=np.float64) / n)
    ang_r = (t // GRID_W)[:, None] * inv[None, :]
    ang_c = (t % GRID_W)[:, None] * inv[None, :]
    zero = np.zeros_like(ang_r)
    cos = np.concatenate([np.cos(ang_r)] * 2 + [np.cos(ang_c)] * 2, axis=1)
    sa = np.concatenate([-np.sin(ang_r), zero, -np.sin(ang_c), zero], axis=1)
    sb = np.concatenate([zero, np.sin(ang_r), zero, np.sin(ang_c)], axis=1)
    return tuple(jnp.asarray(a, F32) for a in (cos, sa, sb))


def _trunk(x, mod, lp, nseq, seq, lat, layer=0, depth=1, carry=None):
    rows = nseq * seq
    proj = _inproj(x, mod, lp["norm_mix_g"], lp["w_in"], tm=min(2048, rows), tn=6 * MXU_WIDTH)
    if lat is None:
        attn, k_new, v_new = _attn_ctx(proj, lp["q_norm_g"], lp["k_norm_g"], nseq, seq, layer, depth,
                                       () if carry is None else carry[:2])
        *fwd, h_f = _ssd_pass(proj, lp, 0, nseq, seq, None, None, layer, depth, None if carry is None else carry[2])
        ssm, h_b = _ssd_pass(proj, lp, 1, nseq, seq, None, fwd, layer, depth, None if carry is None else carry[3])
    else:
        cache_k, cache_v, h0_f, h0_b, tables = lat
        kall, vall = _prep_lat(proj, cache_k, cache_v, *tables, lp["k_norm_g"], nseq, seq)
        attn = _attn_lat(proj, kall, vall, *tables, lp["q_norm_g"], nseq, seq, tq=256)
        k_new = v_new = None
        *fwd, h_f = _ssd_pass(proj, lp, 0, nseq, seq, h0_f)
        ssm, h_b = _ssd_pass(proj, lp, 1, nseq, seq, h0_b, fwd)
    outs = _merge(attn, ssm, proj, x, mod, lp, seq, tm=512)
    w1, w3, w2 = lp["ffn"]
    if lp["router"] is None:
        x = _ffn(outs[1], outs[0], mod, w1, w3, w2, tm=512)
    else:
        x = _moe(outs[1], outs[0], mod, outs[2], outs[3], w1, w3, w2, tile=384)
    return x, (k_new, v_new, h_f, h_b)


def kernel(x_prompt, x_sample, cache_k, cache_v, state_ssm_fwd, state_ssm_bwd, c, c_ctx, w_mod, b_mod, norm_mix_g, norm_ffn_g, w_in, q_norm_g, k_norm_g, w_attn_out, ssm_conv_w, ssm_conv_b, ssm_dt_bias, ssm_a_log, ssm_d, ssm_norm_g, w_ssm_out, sconv_w, w_sconv_out, w_merge, ffn_w1, ffn_w3, ffn_w2, moe_router, moe_w1, moe_w3, moe_w2):
    batch, seq, _ = x_prompt.shape
    nb, dseq, _ = x_sample.shape
    depth = w_in.shape[0]
    past = cache_k.shape[2]
    kvw = N_KV_HEADS * HEAD_DIM
    npair = SSM_HEADS // 2
    assert nb + 1 <= SUBLANES

    cond = jnp.zeros((SUBLANES, D_MODEL), F32).at[0].set(c_ctx).at[1:1 + nb].set(c)
    tables = _rope_tables(dseq)
    y_ctx = x_prompt.reshape(batch * seq, D_MODEL)
    y_lat = x_sample.reshape(nb * dseq, D_MODEL)
    mods = _modulation(cond, w_mod, b_mod.reshape(depth, 1, -1))
    carry = None
    for l in range(depth):
        lp = _layer_params(l, w_in, q_norm_g, k_norm_g, w_attn_out, ssm_conv_w, ssm_conv_b, ssm_dt_bias,
                           ssm_a_log, ssm_d, ssm_norm_g, w_ssm_out, sconv_w, w_sconv_out, w_merge,
                           norm_mix_g, norm_ffn_g, ffn_w1, ffn_w3, ffn_w2, moe_router, moe_w1, moe_w3, moe_w2)
        m = jnp.pad(mods[l].reshape(SUBLANES, 6, D_MODEL), ((0, 0), (0, SUBLANES - 6), (0, 0)))
        y_ctx, carry = _trunk(y_ctx, m[0:1], lp, batch, seq, None, l, depth, carry)
        lat = (cache_k[:, l].reshape(nb, past, kvw), cache_v[:, l].reshape(nb, past, kvw),
               state_ssm_fwd[:, l].reshape(nb, npair, SSM_CHUNK, SSM_STATE),
               state_ssm_bwd[:, l].reshape(nb, npair, SSM_CHUNK, SSM_STATE), tables)
        y_lat = _trunk(y_lat, m[1:1 + nb], lp, nb, dseq, lat)[0]
    k_all, v_all, hf_all, hb_all = carry
    state_shape = (batch, depth, SSM_HEADS, SSM_HEADDIM, SSM_STATE)
    return (y_ctx.reshape(batch, seq, D_MODEL), y_lat.reshape(nb, dseq, D_MODEL),
            k_all, v_all,
            hf_all.reshape(state_shape), hb_all.reshape(state_shape))
```

```python
import functools

import jax
import jax.numpy as jnp
import numpy as np
from jax import lax
from jax.experimental import pallas as pl
from jax.experimental.pallas import tpu as pltpu
from jax.experimental.pallas import tpu_sc as plsc

F32 = jnp.float32
BF16 = jnp.bfloat16
HIGHEST = lax.Precision.HIGHEST

D_MODEL = 1024
HEAD_DIM = 128
N_HEADS = 8
N_KV_HEADS = 2
Q_PER_KV = N_HEADS // N_KV_HEADS
ROPE_THETA = 10000.0
GRID_W = 64
SSM_HEADS = 16
SSM_HEADDIM = 64
SSM_STATE = 128
SSM_CHUNK = 128
D_FF = 2816
N_EXPERTS = 8
EPS = 1e-6
Q_SCALE = 1.4426950408889634 * HEAD_DIM ** -0.5

COL_Q, COL_Z, COL_SCB, COL_SCC, COL_SCH, COL_G = 0, 1024, 2048, 3072, 4096, 5120
COL_XS, COL_BC, COL_K, COL_V, COL_DT = 8192, 9216, 9728, 9984, 10240
MXU_WIDTH = 256
N_PROJ = 10752
LANES = 128
SUBLANES = 8
HALO = 16
MIB = 1024 * 1024


def _params(sem, vmem_mib):
    return pltpu.CompilerParams(dimension_semantics=sem, vmem_limit_bytes=vmem_mib * MIB)


def _sigmoid(t):
    return 1.0 / (1.0 + jnp.exp(-t))


def _silu(t):
    return t * _sigmoid(t)


def _rms(t):
    return t * lax.rsqrt(jnp.mean(t * t, axis=-1, keepdims=True) + EPS)


def _bdot(a, b):
    return jnp.dot(a.astype(BF16), b.astype(BF16), preferred_element_type=F32)


def _hdot(a, b):
    return jnp.dot(a, b, precision=HIGHEST, preferred_element_type=F32)


def _split3(t):
    a = t.astype(BF16)
    r = t - a.astype(F32)
    b = r.astype(BF16)
    return a, b, (r - b.astype(F32)).astype(BF16)


def _mod_kernel(c_ref, w_ref, b_ref, o_ref):
    o_ref[...] = _hdot(_silu(c_ref[...]), w_ref[...]) + b_ref[...]


def _modulation(cond8, w, b):
    depth, _, width = w.shape
    return pl.pallas_call(
        _mod_kernel,
        out_shape=jax.ShapeDtypeStruct((depth, SUBLANES, width), F32),
        grid=(depth, width // D_MODEL),
        in_specs=[pl.BlockSpec((SUBLANES, D_MODEL), lambda l, j: (0, 0)),
                  pl.BlockSpec((None, D_MODEL, D_MODEL), lambda l, j: (l, 0, j)),
                  pl.BlockSpec((None, 1, D_MODEL), lambda l, j: (l, 0, j))],
        out_specs=pl.BlockSpec((None, SUBLANES, D_MODEL), lambda l, j: (l, 0, j)),
        compiler_params=_params(("parallel", "parallel"), 32),
        name="modulation",
    )(cond8, w, b)


def _inproj_kernel(x_ref, mod_ref, g_ref, w_ref, o_ref, h_scr):
    @pl.when(pl.program_id(1) == 0)
    def _():
        h = _rms(x_ref[...]) * g_ref[...]
        h = h * (1.0 + mod_ref[0, 1:2, :]) + mod_ref[0, 0:1, :]
        h_scr[...] = h.astype(BF16)

    o_ref[...] = _qk(h_scr[...], w_ref[...]).astype(o_ref.dtype)


def _inproj(x, mod, g, w, tm, tn):
    rows = x.shape[0]
    per_mod = rows // mod.shape[0]
    tm = min(tm, per_mod)
    return pl.pallas_call(
        _inproj_kernel,
        out_shape=jax.ShapeDtypeStruct((rows, N_PROJ), BF16),
        grid=(rows // tm, N_PROJ // tn),
        in_specs=[pl.BlockSpec((tm, D_MODEL), lambda i, j: (i, 0)),
                  pl.BlockSpec((1, SUBLANES, D_MODEL), lambda i, j: (i * tm // per_mod, 0, 0)),
                  pl.BlockSpec((1, D_MODEL), lambda i, j: (0, 0)),
                  pl.BlockSpec((tn, D_MODEL), lambda i, j: (j, 0))],
        out_specs=pl.BlockSpec((tm, tn), lambda i, j: (i, j)),
        scratch_shapes=[pltpu.VMEM((tm, D_MODEL), BF16)],
        compiler_params=_params(("parallel", "arbitrary"), 48),
        name="inproj",
    )(x, mod, g, w)


def _softmax_pv(s, vb):
    m = jnp.max(s, axis=-1, keepdims=True)
    p = jnp.exp2(s - m)
    l = jnp.sum(p, axis=-1, keepdims=True)
    return jnp.dot(p.astype(BF16), vb, preferred_element_type=F32) / l


def _qk(qb, kb):
    return lax.dot_general(qb, kb, (((1,), (1,)), ((), ())), preferred_element_type=F32)


def _attn_ctx_kernel(q_ref, k_ref, v_ref, qg_ref, kg_ref, *rest):
    a_ref, ko_ref, vo_ref = rest[-3:]
    v = v_ref[...].astype(F32)
    for hk in range(N_KV_HEADS):
        ks = slice(hk * HEAD_DIM, (hk + 1) * HEAD_DIM)
        kn = _rms(k_ref[:, ks].astype(F32)) * kg_ref[...]
        ko_ref[:, hk, :] = kn
        vo_ref[:, hk, :] = v[:, ks]
        knb = kn.astype(BF16)
        vb = v[:, ks].astype(BF16)
        for g in range(Q_PER_KV):
            hs = slice((hk * Q_PER_KV + g) * HEAD_DIM, (hk * Q_PER_KV + g + 1) * HEAD_DIM)
            qn = _rms(q_ref[:, hs].astype(F32)) * (qg_ref[...] * Q_SCALE)
            s = _qk(qn.astype(BF16), knb)
            a_ref[:, hs] = _softmax_pv(s, vb).astype(BF16)


def _attn_ctx(proj, qg, kg, nseq, seq, layer, depth, earlier):
    kvw = N_KV_HEADS * HEAD_DIM
    cache = jax.ShapeDtypeStruct((nseq, depth, seq, N_KV_HEADS, HEAD_DIM), F32)
    cache_spec = pl.BlockSpec((None, None, seq, N_KV_HEADS, HEAD_DIM), lambda b: (b, layer, 0, 0, 0))
    n_in = 5
    return pl.pallas_call(
        _attn_ctx_kernel,
        out_shape=(jax.ShapeDtypeStruct((nseq * seq, D_MODEL), BF16), cache, cache),
        grid=(nseq,),
        in_specs=[pl.BlockSpec((seq, D_MODEL), lambda b: (b, COL_Q // D_MODEL)),
                  pl.BlockSpec((seq, kvw), lambda b: (b, COL_K // kvw)),
                  pl.BlockSpec((seq, kvw), lambda b: (b, COL_V // kvw)),
                  pl.BlockSpec((1, HEAD_DIM), lambda b: (0, 0)),
                  pl.BlockSpec((1, HEAD_DIM), lambda b: (0, 0))] + [pl.BlockSpec(memory_space=pl.ANY)] * len(earlier),
        out_specs=(pl.BlockSpec((seq, D_MODEL), lambda b: (b, 0)), cache_spec, cache_spec),
        input_output_aliases={n_in + i: 1 + i for i in range(len(earlier))},
        compiler_params=_params(("parallel",), 32),
        name="attn_ctx",
    )(proj, proj, proj, qg, kg, *earlier)


def _rope(t, cos, sa, sb):
    return t * cos + pltpu.roll(t, 96, 1) * sa + pltpu.roll(t, 32, 1) * sb


def _prep_lat_kernel(k_ref, v_ref, ck_ref, cv_ref, cos_ref, sa_ref, sb_ref, kg_ref, ko_ref, vo_ref):
    t = pl.program_id(1)

    @pl.when(t == 0)
    def _():
        ko_ref[...] = ck_ref[...].astype(BF16)
        vo_ref[...] = cv_ref[...].astype(BF16)

    @pl.when(t > 0)
    def _():
        cos, sa, sb = cos_ref[...], sa_ref[...], sb_ref[...]
        vo_ref[...] = v_ref[...]
        for h in range(N_KV_HEADS):
            hs = slice(h * HEAD_DIM, (h + 1) * HEAD_DIM)
            ko_ref[:, hs] = _rope(_rms(k_ref[:, hs].astype(F32)) * kg_ref[...], cos, sa, sb).astype(BF16)


def _prep_lat(proj, cache_k, cache_v, cos, sa, sb, kg, nb, seq):
    past = cache_k.shape[1]
    tr = past
    kvw = N_KV_HEADS * HEAD_DIM
    nt = seq // tr
    prev = lambda t: jnp.maximum(t - 1, 0)
    return pl.pallas_call(
        _prep_lat_kernel,
        out_shape=(jax.ShapeDtypeStruct((nb, past + seq, kvw), BF16),
                   jax.ShapeDtypeStruct((nb, past + seq, kvw), BF16)),
        grid=(nb, nt + 1),
        in_specs=[pl.BlockSpec((tr, kvw), lambda b, t: (b * nt + prev(t), COL_K // kvw)),
                  pl.BlockSpec((tr, kvw), lambda b, t: (b * nt + prev(t), COL_V // kvw)),
                  pl.BlockSpec((None, past, kvw), lambda b, t: (b, 0, 0)),
                  pl.BlockSpec((None, past, kvw), lambda b, t: (b, 0, 0)),
                  pl.BlockSpec((tr, HEAD_DIM), lambda b, t: (prev(t), 0)),
                  pl.BlockSpec((tr, HEAD_DIM), lambda b, t: (prev(t), 0)),
                  pl.BlockSpec((tr, HEAD_DIM), lambda b, t: (prev(t), 0)),
                  pl.BlockSpec((1, HEAD_DIM), lambda b, t: (0, 0))],
        out_specs=(pl.BlockSpec((None, tr, kvw), lambda b, t: (b, t, 0)),
                   pl.BlockSpec((None, tr, kvw), lambda b, t: (b, t, 0))),
        compiler_params=_params(("parallel", "arbitrary"), 32),
        name="prep_lat",
    )(proj, proj, cache_k, cache_v, cos, sa, sb, kg)


def _attn_lat_kernel(q_ref, cos_ref, sa_ref, sb_ref, qg_ref, k_ref, v_ref, o_ref):
    cos, sa, sb = cos_ref[...], sa_ref[...], sb_ref[...]
    gain = qg_ref[...] * Q_SCALE
    for hk in range(N_KV_HEADS):
        kb = k_ref[:, hk * HEAD_DIM:(hk + 1) * HEAD_DIM]
        vb = v_ref[:, hk * HEAD_DIM:(hk + 1) * HEAD_DIM]
        for g in range(Q_PER_KV):
            hs = slice((hk * Q_PER_KV + g) * HEAD_DIM, (hk * Q_PER_KV + g + 1) * HEAD_DIM)
            qn = _rope(_rms(q_ref[:, hs].astype(F32)) * gain, cos, sa, sb)
            s = _qk(qn.astype(BF16), kb)
            o_ref[:, hs] = _softmax_pv(s, vb).astype(BF16)


def _attn_lat(proj, kall, vall, cos, sa, sb, qg, nb, seq, tq):
    nk, kvw = kall.shape[1:]
    nq = seq // tq
    table = pl.BlockSpec((tq, HEAD_DIM), lambda b, i: (i, 0))
    return pl.pallas_call(
        _attn_lat_kernel,
        out_shape=jax.ShapeDtypeStruct((nb * seq, D_MODEL), BF16),
        grid=(nb, nq),
        in_specs=[pl.BlockSpec((tq, D_MODEL), lambda b, i: (b * nq + i, COL_Q // D_MODEL)),
                  table, table, table, pl.BlockSpec((1, HEAD_DIM), lambda b, i: (0, 0)),
                  pl.BlockSpec((None, nk, kvw), lambda b, i: (b, 0, 0)),
                  pl.BlockSpec((None, nk, kvw), lambda b, i: (b, 0, 0))],
        out_specs=pl.BlockSpec((tq, D_MODEL), lambda b, i: (b * nq + i, 0)),
        compiler_params=_params(("parallel", "arbitrary"), 48),
        name="attn_lat",
    )(proj, cos, sa, sb, qg, kall, vall)


SSD_GROUP = 4


def _ssd_kernel(*refs, reverse, has_h0, nc):
    it = iter(refs)
    if reverse:
        xsc_ref, bcc_ref, dt_ref, alog_ref, dtb_ref, e_ref = (next(it) for _ in range(6))
    else:
        xs_ref, bc_ref, dt_ref, xsp_ref, xsn_ref, bcp_ref, bcn_ref = (next(it) for _ in range(7))
        cwx_ref, cbx_ref, cwb_ref, cbb_ref, alog_ref, dtb_ref, e_ref, shift_ref = (next(it) for _ in range(8))
    h0_ref = next(it) if has_h0 else None
    if reverse:
        z_ref, yf_ref, d_ref, g_ref = (next(it) for _ in range(4))
        y_ref, hout_ref, ht_scr = refs[-3:]
    else:
        y_ref, xsc_ref, bcc_ref, hout_ref, ht_scr = refs[-5:]

    q = SSM_CHUNK
    npair = SSM_HEADS // 2
    group = y_ref.shape[0]
    c = pl.program_id(1)
    cc = nc - 1 - c if reverse else c

    @pl.when(c == 0)
    def _():
        for s in range(group):
            for p in range(npair):
                if has_h0:
                    ht_scr[s, p] = h0_ref[s, p].T
                else:
                    ht_scr[s, p] = jnp.zeros((q, q), F32)

    off = SSM_HEADS if reverse else 0
    ii = lax.broadcasted_iota(jnp.int32, (q, q), 0)
    jj = lax.broadcasted_iota(jnp.int32, (q, q), 1)
    tri = (jj >= ii) if reverse else (ii >= jj)
    tri_b = jnp.where(tri, 1.0, 0.0).astype(BF16)
    tri_tb3 = jnp.concatenate([jnp.where((ii >= jj) if reverse else (jj >= ii), 1.0, 0.0).astype(BF16)] * 3, axis=0)
    lo = lax.broadcasted_iota(jnp.int32, (q, LANES), 1) < SSM_HEADDIM
    neg_a = -jnp.exp(alog_ref[...])
    if not reverse:
        keep_prev = jnp.where(cc == 0, 0.0, 1.0).astype(BF16)
        keep_next = jnp.where(cc == nc - 1, 0.0, 1.0).astype(BF16)

    def conv_silu(s, x_ref_, p_ref_, n_ref_, w_ref_, b_ref_):
        x = x_ref_[s]
        ext = jnp.concatenate([x, p_ref_[s] * keep_prev, n_ref_[s] * keep_next], axis=0)
        nb = jnp.dot(shift_ref[...], ext, preferred_element_type=F32)
        w = w_ref_[...]
        return _silu(w[0:1, :] * nb[:q, :] + w[1:2, :] * x.astype(F32) + w[2:3, :] * nb[q:, :] + b_ref_[...])

    for s in range(group):
        if reverse:
            xs = xsc_ref[s].astype(F32)
            bc = bcc_ref[s].astype(F32)
        else:
            xs = conv_silu(s, xs_ref, xsp_ref, xsn_ref, cwx_ref, cbx_ref)
            bc = conv_silu(s, bc_ref, bcp_ref, bcn_ref, cwb_ref, cbb_ref)
            xsc_ref[s] = xs.astype(BF16)
            bcc_ref[s] = bc.astype(BF16)

        dt = jax.nn.softplus(dt_ref[s].astype(F32) + dtb_ref[...])
        da = dt * neg_a
        acs3 = jnp.dot(tri_b, jnp.concatenate(_split3(da), axis=1), preferred_element_type=F32)
        acs = acs3[:, :LANES] + acs3[:, LANES:2 * LANES] + acs3[:, 2 * LANES:]
        acs_t = jnp.dot(jnp.concatenate(_split3(da.T), axis=1), tri_tb3, preferred_element_type=F32)
        spread = jnp.dot(jnp.concatenate(_split3(jnp.concatenate([dt, acs], axis=0)), axis=1), e_ref[...],
                         preferred_element_type=F32)
        acs_x = spread[q:, :]
        tot_x = acs_x[0:1, :] if reverse else acs_x[q - 1:q, :]
        xdt = xs * spread[:q, :]
        xw = xdt * jnp.exp(tot_x - acs_x)
        eacs_x = jnp.exp(acs_x)
        etot_x = jnp.exp(tot_x)

        parts = []
        for g in range(2):
            bg = bc[:, g * q:(g + 1) * q]
            cg = bc[:, (2 + g) * q:(3 + g) * q].astype(BF16)
            cb = _qk(cg, bg.astype(BF16))
            bgt = bg.T.astype(BF16)
            for pp in range(npair // 2):
                p = g * (npair // 2) + pp
                ps = slice(p * LANES, (p + 1) * LANES)
                ms = []
                for hh in range(2):
                    ln = off + 2 * p + hh
                    dec = jnp.where(tri, jnp.exp(acs[:, ln:ln + 1] - acs_t[ln:ln + 1, :]), 0.0)
                    ms.append((cb * dec).astype(BF16))
                m = jnp.concatenate(ms, axis=1)
                xp = xdt[:, ps]
                bd = jnp.concatenate([jnp.where(lo, xp, 0.0), jnp.where(lo, 0.0, xp)], axis=0).astype(BF16)
                ht = ht_scr[s, p]
                yp = jnp.dot(m, bd, preferred_element_type=F32)
                yp = yp + jnp.dot(cg, ht.astype(BF16), preferred_element_type=F32) * eacs_x[:, ps]
                ht_scr[s, p] = etot_x[:, ps] * ht + jnp.dot(bgt, xw[:, ps].astype(BF16),
                                                            preferred_element_type=F32)
                parts.append(yp)
        y = jnp.concatenate(parts, axis=1)

        if reverse:
            y = y + yf_ref[s] + d_ref[...] * xs
            y = y * _silu(z_ref[s].astype(F32))
            y_ref[s] = (_rms(y) * g_ref[...]).astype(y_ref.dtype)
        else:
            y_ref[s] = y

    @pl.when(c == nc - 1)
    def _():
        for s in range(group):
            for p in range(npair):
                hout_ref[s, p] = ht_scr[s, p].T


def _ssd_pass(proj, lp, d, nseq, seq, h0, fwd=None, layer=0, depth=1, earlier=None):
    reverse = fwd is not None
    has_h0 = h0 is not None
    q = SSM_CHUNK
    nc = seq // q
    g = min(SSD_GROUP, nseq)
    assert nseq % g == 0
    npair = SSM_HEADS // 2
    hb = q // HALO
    bcw = 4 * SSM_STATE
    chunk = (lambda c: nc - 1 - c) if reverse else (lambda c: c)
    prev8 = lambda c: jnp.maximum(chunk(c) * hb - 1, 0)
    next8 = lambda c: jnp.minimum((chunk(c) + 1) * hb, seq // HALO - 1)
    const = lambda s, c: (0, 0)
    tile = lambda width, col=0: pl.BlockSpec((g, q, width), lambda s, c: (s, chunk(c), col // width))
    halo = lambda width, col, where: pl.BlockSpec((g, HALO, width), lambda s, c: (s, where(c), col // width))
    seqs = lambda t: t.reshape(nseq, seq, t.shape[-1])
    scan_consts = [pl.BlockSpec((1, LANES), const), pl.BlockSpec((1, LANES), const),
                   pl.BlockSpec((3 * LANES, D_MODEL), const)]
    scan_args = [lp["alog"][d], lp["dtb"][d], lp["expand"][d]]
    proj3 = seqs(proj)
    if reverse:
        yf, xsc, bcc = fwd
        in_specs = [tile(D_MODEL), tile(bcw), tile(LANES, COL_DT)] + scan_consts
        args = [xsc, bcc, proj3] + scan_args
    else:
        in_specs = [tile(D_MODEL, COL_XS), tile(bcw, COL_BC), tile(LANES, COL_DT),
                    halo(D_MODEL, COL_XS, prev8), halo(D_MODEL, COL_XS, next8),
                    halo(bcw, COL_BC, prev8), halo(bcw, COL_BC, next8),
                    pl.BlockSpec((3, D_MODEL), const), pl.BlockSpec((1, D_MODEL), const),
                    pl.BlockSpec((3, bcw), const), pl.BlockSpec((1, bcw), const)] + scan_consts + [
                    pl.BlockSpec((2 * q, q + 2 * HALO), const)]
        args = [proj3] * 7 + [lp["cw_x"], lp["cb_x"], lp["cw_bc"], lp["cb_bc"]] + scan_args + [lp["shift"]]
    if has_h0:
        in_specs.append(pl.BlockSpec((g, npair, q, q), lambda s, c: (s, 0, 0, 0)))
        args.append(h0)
    if reverse:
        in_specs += [tile(D_MODEL, COL_Z), tile(D_MODEL), pl.BlockSpec((1, D_MODEL), const),
                     pl.BlockSpec((1, D_MODEL), const)]
        args += [proj3, yf, lp["d_row"], lp["ssm_norm_g"]]
    state = jax.ShapeDtypeStruct((nseq, depth, npair, q, q), F32)
    state_spec = pl.BlockSpec((g, None, npair, q, q), lambda s, c: (s, layer, 0, 0, 0))
    if reverse:
        out_shape = (jax.ShapeDtypeStruct((nseq, seq, D_MODEL), BF16), state)
        out_specs = (tile(D_MODEL), state_spec)
    else:
        out_shape = (jax.ShapeDtypeStruct((nseq, seq, D_MODEL), F32),
                     jax.ShapeDtypeStruct((nseq, seq, D_MODEL), BF16),
                     jax.ShapeDtypeStruct((nseq, seq, bcw), BF16), state)
        out_specs = (tile(D_MODEL), tile(D_MODEL), tile(bcw), state_spec)
    aliases = {}
    if earlier is not None:
        aliases = {len(args): len(out_shape) - 1}
        in_specs.append(pl.BlockSpec(memory_space=pl.ANY))
        args.append(earlier)
    outs = pl.pallas_call(
        functools.partial(_ssd_kernel, reverse=reverse, has_h0=has_h0, nc=nc),
        out_shape=out_shape,
        grid=(nseq // g, nc),
        in_specs=in_specs,
        out_specs=out_specs,
        input_output_aliases=aliases,
        scratch_shapes=[pltpu.VMEM((g, npair, q, q), F32)],
        compiler_params=_params(("parallel", "arbitrary"), 48),
        name="ssd_bwd" if reverse else "ssd_fwd",
    )(*args)
    if reverse:
        return outs[0].reshape(nseq * seq, D_MODEL), outs[1]
    return outs


ROUTE_I1, ROUTE_I2, ROUTE_P1, ROUTE_P2, ROUTE_R1, ROUTE_R2 = 0, 1, 2, 3, 4, 5


def _pack_bf16_pairs(v):
    n = v.shape[1] // 2
    bits = lax.bitcast_convert_type(v.astype(BF16).astype(F32), jnp.uint32)
    return lax.bitcast_convert_type((bits[:, :n] >> 16) | bits[:, n:], jnp.int32)


def _unpack_bf16_pairs(p):
    u = lax.bitcast_convert_type(p, jnp.uint32)
    lo = lax.bitcast_convert_type(u << 16, F32)
    hi = lax.bitcast_convert_type(u & jnp.uint32(0xFFFF0000), F32)
    return jnp.concatenate([lo, hi], axis=1)


def _top2_route(logits):
    tm = logits.shape[0]
    lane = lax.broadcasted_iota(jnp.int32, logits.shape, 1).astype(F32)
    lg = jnp.where(lane < N_EXPERTS, logits, -jnp.inf)
    m1 = jnp.max(lg, axis=-1, keepdims=True)
    i1 = jnp.min(jnp.where(lg == m1, lane, float(LANES)), axis=-1, keepdims=True)
    lg2 = jnp.where(lane == i1, -jnp.inf, lg)
    m2 = jnp.max(lg2, axis=-1, keepdims=True)
    i2 = jnp.min(jnp.where(lg2 == m2, lane, float(LANES)), axis=-1, keepdims=True)
    e = jnp.exp(m2 - m1)
    p1 = 1.0 / (1.0 + e)
    chosen = jnp.where(lane == i1, 1.0, jnp.where(lane == i2, 1.0, 0.0))
    earlier = (lax.broadcasted_iota(jnp.int32, (tm, tm), 0) > lax.broadcasted_iota(jnp.int32, (tm, tm), 1))
    ranks = _bdot(jnp.where(earlier, 1.0, 0.0), chosen)
    r1 = jnp.sum(jnp.where(lane == i1, ranks, 0.0), axis=-1, keepdims=True)
    r2 = jnp.sum(jnp.where(lane == i2, ranks, 0.0), axis=-1, keepdims=True)
    rec = jnp.where(lane == ROUTE_I1, i1, jnp.where(lane == ROUTE_I2, i2, 0.0))
    rec = jnp.where(lane == ROUTE_P1, p1, jnp.where(lane == ROUTE_P2, e * p1, rec))
    rec = jnp.where(lane == ROUTE_R1, r1, jnp.where(lane == ROUTE_R2, r2, rec))
    return rec, jnp.sum(chosen, axis=0, keepdims=True)


def _merge_kernel(*refs, seq, tm, moe):
    it = iter(refs)
    attn_ref, ssm_ref, scb_ref, scc_ref, sch_ref = (next(it) for _ in range(5))
    sccp_ref, sccn_ref, schp_ref, schn_ref = (next(it) for _ in range(4))
    g0_ref, g1_ref, g2_ref, x_ref, mod_ref = (next(it) for _ in range(5))
    wa_ref, ws_ref, wc_ref, wm_ref, scw_ref, ng_ref = (next(it) for _ in range(6))
    router_ref = next(it) if moe else None
    xo_ref, h2_ref = next(it), next(it)
    route_ref, cnt_ref = (next(it), next(it)) if moe else (None, None)

    i = pl.program_id(0)
    pos = (lax.broadcasted_iota(jnp.int32, (tm, 1), 0) + i * tm) & (seq - 1)
    u = scc_ref[...].astype(F32) * sch_ref[...].astype(F32)
    up = sccp_ref[HALO - 1:HALO, :].astype(F32) * schp_ref[HALO - 1:HALO, :].astype(F32)
    un = sccn_ref[0:1, :].astype(F32) * schn_ref[0:1, :].astype(F32)
    r = lax.broadcasted_iota(jnp.int32, (tm, 1), 0)
    um1 = jnp.where(pos == 0, 0.0, jnp.where(r == 0, up, pltpu.roll(u, 1, 0)))
    up1 = jnp.where(pos == seq - 1, 0.0, jnp.where(r == tm - 1, un, pltpu.roll(u, tm - 1, 0)))
    w = scw_ref[...]
    sc = scb_ref[...].astype(F32) * (w[0:1, :] * um1 + w[1:2, :] * u + w[2:3, :] * up1)

    gate = lambda g_ref: _sigmoid(g_ref[...].astype(F32))
    merged = gate(g0_ref) * jnp.dot(attn_ref[...], wa_ref[...], preferred_element_type=F32)
    merged = merged + gate(g1_ref) * jnp.dot(ssm_ref[...], ws_ref[...], preferred_element_type=F32)
    merged = merged + gate(g2_ref) * jnp.dot(sc.astype(BF16), wc_ref[...], preferred_element_type=F32)
    mod = mod_ref[0]
    xn = x_ref[...] + mod[2:3, :] * jnp.dot(merged.astype(BF16), wm_ref[...], preferred_element_type=F32)
    xo_ref[...] = xn
    h2 = _rms(xn) * ng_ref[...] * (1.0 + mod[4:5, :]) + mod[3:4, :]
    if moe:
        h2_ref[...] = _pack_bf16_pairs(h2)
        h_hi = h2.astype(BF16)
        h_lo = (h2 - h_hi.astype(F32)).astype(BF16)
        both = jnp.dot(h_hi, router_ref[...], preferred_element_type=F32)
        logits = both[:, :LANES] + both[:, LANES:] + jnp.dot(h_lo, router_ref[:, :LANES], preferred_element_type=F32)
        rec, counts = _top2_route(logits)
        route_ref[...] = rec
        cnt_ref[...] = jnp.broadcast_to(counts, cnt_ref.shape)
    else:
        h2_ref[...] = h2.astype(BF16)


def _merge(attn, ssm, proj, x, mod, lp, seq, tm):
    rows = x.shape[0]
    per_mod = rows // mod.shape[0]
    tm = min(tm, per_mod)
    moe = lp["router"] is not None
    hb = tm // HALO
    prev8 = lambda i: jnp.maximum(i * hb - 1, 0)
    next8 = lambda i: jnp.minimum((i + 1) * hb, rows // HALO - 1)
    tile = lambda col: pl.BlockSpec((tm, D_MODEL), lambda i: (i, col // D_MODEL))
    const = lambda shape: pl.BlockSpec(shape, lambda i: (0, 0))
    in_specs = [tile(0), tile(0), tile(COL_SCB), tile(COL_SCC), tile(COL_SCH),
                pl.BlockSpec((HALO, D_MODEL), lambda i: (prev8(i), COL_SCC // D_MODEL)),
                pl.BlockSpec((HALO, D_MODEL), lambda i: (next8(i), COL_SCC // D_MODEL)),
                pl.BlockSpec((HALO, D_MODEL), lambda i: (prev8(i), COL_SCH // D_MODEL)),
                pl.BlockSpec((HALO, D_MODEL), lambda i: (next8(i), COL_SCH // D_MODEL)),
                tile(COL_G), tile(COL_G + D_MODEL), tile(COL_G + 2 * D_MODEL), tile(0),
                pl.BlockSpec((1, SUBLANES, D_MODEL), lambda i: (i * tm // per_mod, 0, 0)),
                _resident((D_MODEL, D_MODEL)), _resident((D_MODEL, D_MODEL)), _resident((D_MODEL, D_MODEL)),
                _resident((D_MODEL, D_MODEL)), const((3, D_MODEL)), const((1, D_MODEL))]
    args = [attn, ssm, proj, proj, proj, proj, proj, proj, proj, proj, proj, proj, x, mod,
            lp["w_attn_out"], lp["w_ssm_out"], lp["w_sconv_out"], lp["w_merge"], lp["sconv_w"], lp["norm_ffn_g"]]
    out_shape = [jax.ShapeDtypeStruct((rows, D_MODEL), F32), jax.ShapeDtypeStruct((rows, D_MODEL), BF16)]
    out_specs = [tile(0), tile(0)]
    if moe:
        in_specs.append(_resident((D_MODEL, 2 * LANES)))
        args.append(lp["router"])
        out_shape[1] = jax.ShapeDtypeStruct((rows, D_MODEL // 2), jnp.int32)
        out_specs[1] = pl.BlockSpec((tm, D_MODEL // 2), lambda i: (i, 0))
        out_shape += [jax.ShapeDtypeStruct((rows, LANES), F32),
                      jax.ShapeDtypeStruct((rows // tm * SUBLANES, LANES), F32)]
        out_specs += [pl.BlockSpec((tm, LANES), lambda i: (i, 0)), pl.BlockSpec((SUBLANES, LANES), lambda i: (i, 0))]
    return pl.pallas_call(
        functools.partial(_merge_kernel, seq=seq, tm=tm, moe=moe),
        out_shape=tuple(out_shape),
        grid=(rows // tm,),
        in_specs=in_specs,
        out_specs=tuple(out_specs),
        compiler_params=_params(("parallel",), 56),
        name="merge",
    )(*args)


def _swiglu_acc(h, w1_ref, w3_ref, w2_ref, fs):
    a = _silu(jnp.dot(h, w1_ref[:, fs], preferred_element_type=F32)) * jnp.dot(h, w3_ref[:, fs],
                                                                                preferred_element_type=F32)
    return jnp.dot(a.astype(BF16), w2_ref[fs, :], preferred_element_type=F32)


def _swiglu(h, w1_ref, w3_ref, w2_ref):
    cut = (D_FF // 2 + MXU_WIDTH - 1) // MXU_WIDTH * MXU_WIDTH
    return (_swiglu_acc(h, w1_ref, w3_ref, w2_ref, slice(0, cut))
            + _swiglu_acc(h, w1_ref, w3_ref, w2_ref, slice(cut, D_FF)))


def _ffn_kernel(h_ref, x_ref, mod_ref, w1_ref, w3_ref, w2_ref, o_ref):
    o_ref[...] = x_ref[...] + mod_ref[0, 5:6, :] * _swiglu(h_ref[...], w1_ref, w3_ref, w2_ref)


def _resident(shape):
    return pl.BlockSpec(shape, lambda *_: (0,) * len(shape), pipeline_mode=pl.Buffered(1))


def _ffn(h2, x, mod, w1, w3, w2, tm):
    rows = x.shape[0]
    per_mod = rows // mod.shape[0]
    tm = min(tm, per_mod)
    return pl.pallas_call(
        _ffn_kernel,
        out_shape=jax.ShapeDtypeStruct((rows, D_MODEL), F32),
        grid=(rows // tm,),
        in_specs=[pl.BlockSpec((tm, D_MODEL), lambda i: (i, 0)),
                  pl.BlockSpec((tm, D_MODEL), lambda i: (i, 0)),
                  pl.BlockSpec((1, SUBLANES, D_MODEL), lambda i: (i * tm // per_mod, 0, 0)),
                  _resident((D_MODEL, D_FF)), _resident((D_MODEL, D_FF)), _resident((D_FF, D_MODEL))],
        out_specs=pl.BlockSpec((tm, D_MODEL), lambda i: (i, 0)),
        compiler_params=_params(("parallel",), 56),
        name="ffn",
    )(h2, x, mod, w1, w3, w2)


def _cast_kernel(*refs):
    n = len(refs) // 2
    for src, dst in zip(refs[:n], refs[n:]):
        dst[...] = src[...].astype(dst.dtype)


def _cast_bf16(ws):
    e, r, c = ws[0].shape
    rb = max(d for d in range(16, r + 1, 16) if r % d == 0 and d * c * 4 <= 3 * MIB)
    spec = pl.BlockSpec((None, rb, c), lambda i, j: (i, j, 0))
    return pl.pallas_call(
        _cast_kernel,
        out_shape=tuple(jax.ShapeDtypeStruct(w.shape, BF16) for w in ws),
        grid=(e, r // rb),
        in_specs=[spec] * len(ws),
        out_specs=tuple([spec] * len(ws)),
        compiler_params=_params(("parallel", "parallel"), 48),
        name="cast_bf16",
    )(*ws)


SC_CORES = 2
SC_SUBCORES = 16
SC_CHUNK = 128


def _sc_gather(table, idx):
    nw = SC_CORES * SC_SUBCORES
    b, d = idx.shape[0], table.shape[1]
    per_w = b // nw
    assert per_w * nw == b and per_w % SC_CHUNK == 0
    mesh = plsc.VectorSubcoreMesh(core_axis_name="c", subcore_axis_name="s")

    def body(table_hbm, idx_hbm, out_hbm, idx_v, rows_v, sem):
        wid = lax.axis_index("s") * SC_CORES + lax.axis_index("c")

        @pl.loop(0, per_w // SC_CHUNK)
        def _(j):
            off = pl.multiple_of(wid * per_w + j * SC_CHUNK, SC_CHUNK)
            pltpu.sync_copy(idx_hbm.at[pl.ds(off, SC_CHUNK)], idx_v)
            pltpu.async_copy(table_hbm.at[idx_v], rows_v, sem).wait()
            pltpu.sync_copy(rows_v, out_hbm.at[pl.ds(off, SC_CHUNK)])

    return pl.kernel(
        body,
        out_type=jax.ShapeDtypeStruct((b, d), table.dtype),
        mesh=mesh,
        scratch_types=[pltpu.VMEM((SC_CHUNK,), jnp.int32), pltpu.VMEM((SC_CHUNK, d), table.dtype),
                       pltpu.SemaphoreType.DMA],
        name="sc_gather",
    )(table, idx)


def _sc_dispatch(tables, slot, n_slots):
    nw = SC_CORES * SC_SUBCORES
    n_tab = len(tables)
    rows, d = tables[0].shape
    t = n_tab * rows
    per_w = t // nw
    w_per_tab = nw // n_tab
    assert all(tb.shape == (rows, d) for tb in tables) and w_per_tab * n_tab == nw
    assert per_w * nw == t and per_w % SC_CHUNK == 0 and slot.shape[0] == 2 * t
    mesh = plsc.VectorSubcoreMesh(core_axis_name="c", subcore_axis_name="s")

    def body(*refs):
        tabs = refs[:n_tab]
        slot_hbm, out_hbm, idx_v, rows_v, sem = refs[n_tab:]
        wid = lax.axis_index("s") * SC_CORES + lax.axis_index("c")

        @pl.loop(0, per_w // SC_CHUNK)
        def _(j):
            off = pl.multiple_of(wid * per_w + j * SC_CHUNK, SC_CHUNK)
            for i, tab in enumerate(tabs):
                @pl.when(wid // w_per_tab == i)
                def _():
                    pltpu.sync_copy(tab.at[pl.ds(pl.multiple_of(off - i * rows, SC_CHUNK), SC_CHUNK)], rows_v)
            for k in range(2):
                pltpu.sync_copy(slot_hbm.at[pl.ds(pl.multiple_of(k * t + off, SC_CHUNK), SC_CHUNK)], idx_v)
                pltpu.async_copy(rows_v, out_hbm.at[idx_v], sem).wait()

    return pl.kernel(
        body,
        out_type=jax.ShapeDtypeStruct((n_slots, d), tables[0].dtype),
        mesh=mesh,
        scratch_types=[pltpu.VMEM((SC_CHUNK,), jnp.int32), pltpu.VMEM((SC_CHUNK, d), tables[0].dtype),
                       pltpu.SemaphoreType.DMA],
        name="sc_dispatch",
    )(*tables, slot)


def _route_tables(route, counts, tile):
    t = route.shape[0]
    n_mt = counts.shape[0] // SUBLANES
    cnt = counts.reshape(n_mt, SUBLANES, LANES)[:, 0, :N_EXPERTS].astype(jnp.int32)
    incl = jnp.cumsum(cnt, axis=0)
    padded = (incl[-1] + tile - 1) // tile * tile
    gend = jnp.cumsum(padded)
    base = jnp.repeat((gend - padded)[None, :] + incl - cnt, t // n_mt, axis=0)
    e = route[:, ROUTE_I1:ROUTE_I2 + 1].astype(jnp.int32)
    r = route[:, ROUTE_R1:ROUTE_R2 + 1].astype(jnp.int32)
    pick = e[:, :, None] == jnp.arange(N_EXPERTS)[None, None, :]
    slot = jnp.sum(jnp.where(pick, base[:, None, :], 0), axis=-1) + r
    n_slots = 2 * t + N_EXPERTS * tile
    n_valid = gend[-1] // tile
    n_tiles = n_slots // tile
    tile_expert = jnp.sum(jnp.arange(n_tiles)[:, None] * tile >= gend[None, :], axis=1)
    last = jnp.take(tile_expert, jnp.maximum(n_valid - 1, 0))
    tile_expert = jnp.where(jnp.arange(n_tiles) < n_valid, tile_expert, last).astype(jnp.int32)
    return slot.T.reshape(-1).astype(jnp.int32), n_slots, tile_expert, n_valid.reshape(1).astype(jnp.int32)


def _moe_ffn_kernel(te_ref, nv_ref, h_ref, w1_ref, w3_ref, w2_ref, o_ref):
    i = pl.program_id(0)

    @pl.when(i < nv_ref[0])
    def _():
        h = _unpack_bf16_pairs(h_ref[...]).astype(BF16)
        o_ref[...] = _pack_bf16_pairs(_swiglu(h, w1_ref, w3_ref, w2_ref))

    @pl.when(i >= nv_ref[0])
    def _():
        o_ref[...] = jnp.zeros(o_ref.shape, jnp.int32)


def _moe_ffn(hs, tile_expert, n_valid, w1, w3, w2, tile):
    n_slots = hs.shape[0]
    grid_spec = pltpu.PrefetchScalarGridSpec(
        num_scalar_prefetch=2,
        grid=(n_slots // tile,),
        in_specs=[pl.BlockSpec((tile, D_MODEL // 2), lambda i, te, nv: (i, 0)),
                  pl.BlockSpec((None, D_MODEL, D_FF), lambda i, te, nv: (te[i], 0, 0)),
                  pl.BlockSpec((None, D_MODEL, D_FF), lambda i, te, nv: (te[i], 0, 0)),
                  pl.BlockSpec((None, D_FF, D_MODEL), lambda i, te, nv: (te[i], 0, 0))],
        out_specs=pl.BlockSpec((tile, D_MODEL // 2), lambda i, te, nv: (i, 0)),
    )
    return pl.pallas_call(
        _moe_ffn_kernel,
        out_shape=jax.ShapeDtypeStruct((n_slots, D_MODEL // 2), jnp.int32),
        grid_spec=grid_spec,
        compiler_params=_params(("arbitrary",), 56),
        name="moe_ffn",
    )(tile_expert, n_valid, hs, w1, w3, w2)


def _combine_kernel(*refs, n_pass, tiles):
    x_refs = refs[:n_pass]
    y1_ref, y2_ref, route_ref, mod_ref = refs[n_pass:n_pass + 4]
    o_refs = refs[n_pass + 4:]
    i = pl.program_id(0)
    p1 = route_ref[:, ROUTE_P1:ROUTE_P1 + 1]
    p2 = route_ref[:, ROUTE_P2:ROUTE_P2 + 1]
    f = mod_ref[0, 5:6, :] * (p1 * _unpack_bf16_pairs(y1_ref[...]) + p2 * _unpack_bf16_pairs(y2_ref[...]))
    for p in range(n_pass):
        @pl.when((i >= p * tiles) & (i < (p + 1) * tiles))
        def _():
            o_refs[p][...] = x_refs[p][...] + f


def _combine(xs, yg, route, mods, tm):
    n_pass = len(xs)
    rows = xs[0].shape[0]
    tiles = rows // tm
    assert all(x.shape[0] == rows for x in xs) and all(rows // m.shape[0] >= tm for m in mods)
    mod_all = jnp.concatenate(mods, axis=0)
    mod_base = np.cumsum([0] + [m.shape[0] for m in mods[:-1]])

    def mod_index(i):
        idx = 0
        for p, m in enumerate(mods):
            here = int(mod_base[p]) + (i - p * tiles) * tm // (rows // m.shape[0])
            idx = jnp.where(i >= p * tiles, here, idx)
        return idx

    own_tile = lambda p: (lambda i: (jnp.clip(i - p * tiles, 0, tiles - 1), 0))
    return pl.pallas_call(
        functools.partial(_combine_kernel, n_pass=n_pass, tiles=tiles),
        out_shape=tuple(jax.ShapeDtypeStruct((rows, D_MODEL), F32) for _ in xs),
        grid=(n_pass * tiles,),
        in_specs=[pl.BlockSpec((tm, D_MODEL), own_tile(p)) for p in range(n_pass)] + [
            pl.BlockSpec((tm, D_MODEL // 2), lambda i: (i, 0)),
            pl.BlockSpec((tm, D_MODEL // 2), lambda i: (i + n_pass * tiles, 0)),
            pl.BlockSpec((tm, LANES), lambda i: (i, 0)),
            pl.BlockSpec((1, SUBLANES, D_MODEL), lambda i: (mod_index(i), 0, 0))],
        out_specs=tuple(pl.BlockSpec((tm, D_MODEL), own_tile(p)) for p in range(n_pass)),
        compiler_params=_params(("arbitrary",), 32),
        name="moe_combine",
    )(*xs, yg, yg, route, mod_all)


def _moe(passes, w1, w3, w2, tile):
    route = jnp.concatenate([p[3] for p in passes], axis=0)
    counts = jnp.concatenate([p[4] for p in passes], axis=0)
    slot, n_slots, tile_expert, n_valid = _route_tables(route, counts, tile)
    hs = _sc_dispatch([p[0] for p in passes], slot, n_slots)
    ys = _moe_ffn(hs, tile_expert, n_valid, w1, w3, w2, tile)
    return _combine([p[1] for p in passes], _sc_gather(ys, slot), route, [p[2] for p in passes], tm=512)


W_IN_SEGMENTS = ((0, 1024), (1536, 1024), (4128, 1024), (5152, 1024), (6176, 1024), (7200, 3072),
                 (2560, 1024), (3584, 512), (1024, 256), (1280, 256), (4096, 32))


def _permute_kernel(w_ref, o_ref):
    dst = 0
    for src, n in W_IN_SEGMENTS:
        o_ref[dst:dst + n, :] = w_ref[src:src + n, :].astype(BF16)
        dst += n
    o_ref[dst:, :] = jnp.zeros((o_ref.shape[0] - dst, o_ref.shape[1]), BF16)


def _permute_w_in(w_in, l, cb=128):
    w_t = jnp.swapaxes(w_in, 1, 2)
    _, feats, d = w_t.shape
    return pl.pallas_call(
        _permute_kernel,
        out_shape=jax.ShapeDtypeStruct((N_PROJ, d), BF16),
        grid=(d // cb,),
        in_specs=[pl.BlockSpec((None, feats, cb), lambda i: (l, 0, i))],
        out_specs=pl.BlockSpec((N_PROJ, cb), lambda i: (0, i)),
        compiler_params=_params(("parallel",), 48),
        name="permute_w_in",
    )(w_t)


def _layer_params(l, w_in, q_norm_g, k_norm_g, w_attn_out, ssm_conv_w, ssm_conv_b, ssm_dt_bias, ssm_a_log,
                  ssm_d, ssm_norm_g, w_ssm_out, sconv_w, w_sconv_out, w_merge, norm_mix_g, norm_ffn_g,
                  ffn_w1, ffn_w3, ffn_w2, moe_router, moe_w1, moe_w3, moe_w2):
    w_perm = _permute_w_in(w_in, l)
    row = lambda v: v.reshape(1, -1)
    lane_row = lambda v, off: jnp.zeros((1, LANES), F32).at[0, off:off + SSM_HEADS].set(v)
    head_of_col = np.arange(D_MODEL) // SSM_HEADDIM
    expand = [jnp.asarray(np.tile(np.arange(LANES)[:, None] == head_of_col[None, :] + off, (3, 1)), BF16)
              for off in (0, SSM_HEADS)]
    r = np.arange(2 * SSM_CHUNK)[:, None]
    cidx = np.arange(SSM_CHUNK + 2 * HALO)[None, :]
    up = (r < SSM_CHUNK) & (((cidx == r - 1) & (r >= 1)) | ((r == 0) & (cidx == SSM_CHUNK + HALO - 1)))
    dn = (r >= SSM_CHUNK) & (((cidx == r - SSM_CHUNK + 1) & (r < 2 * SSM_CHUNK - 1))
                             | ((r == 2 * SSM_CHUNK - 1) & (cidx == SSM_CHUNK + HALO)))
    shift = jnp.asarray(up | dn, BF16)
    lp = {
        "w_in": w_perm,
        "norm_mix_g": row(norm_mix_g[l]), "norm_ffn_g": row(norm_ffn_g[l]),
        "q_norm_g": row(q_norm_g[l]), "k_norm_g": row(k_norm_g[l]),
        "w_attn_out": w_attn_out[l].astype(BF16), "w_ssm_out": w_ssm_out[l].astype(BF16),
        "w_sconv_out": w_sconv_out[l].astype(BF16), "w_merge": w_merge[l].astype(BF16),
        "cw_x": ssm_conv_w[l][:, :D_MODEL], "cw_bc": ssm_conv_w[l][:, D_MODEL:],
        "cb_x": row(ssm_conv_b[l][:D_MODEL]), "cb_bc": row(ssm_conv_b[l][D_MODEL:]),
        "alog": [lane_row(ssm_a_log[l, d], d * SSM_HEADS) for d in range(2)],
        "dtb": [lane_row(ssm_dt_bias[l, d], d * SSM_HEADS) for d in range(2)],
        "expand": expand,
        "shift": shift,
        "d_row": row(jnp.repeat(ssm_d[l], SSM_HEADDIM)),
        "ssm_norm_g": row(ssm_norm_g[l]),
        "sconv_w": sconv_w[l],
    }
    if l % 2 == 0:
        lp["router"] = None
        lp["ffn"] = tuple(t[l // 2].astype(BF16) for t in (ffn_w1, ffn_w3, ffn_w2))
    else:
        router = jnp.pad(moe_router[l // 2], ((0, 0), (0, LANES - N_EXPERTS)))
        r_hi = router.astype(BF16)
        lp["router"] = jnp.concatenate([r_hi, (router - r_hi.astype(F32)).astype(BF16)], axis=1)
        lp["ffn"] = _cast_bf16((moe_w1[l // 2], moe_w3[l // 2])) + _cast_bf16((moe_w2[l // 2],))
    return lp


def _rope_tables(seq):
    t = np.arange(seq)
    n = HEAD_DIM // 4
    inv = ROPE_THETA ** (-np.arange(n, dtype=np.float64) / n)
    ang_r = (t // GRID_W)[:, None] * inv[None, :]
    ang_c = (t % GRID_W)[:, None] * inv[None, :]
    zero = np.zeros_like(ang_r)
    cos = np.concatenate([np.cos(ang_r)] * 2 + [np.cos(ang_c)] * 2, axis=1)
    sa = np.concatenate([-np.sin(ang_r), zero, -np.sin(ang_c), zero], axis=1)
    sb = np.concatenate([zero, np.sin(ang_r), zero, np.sin(ang_c)], axis=1)
    return tuple(jnp.asarray(a, F32) for a in (cos, sa, sb))


def _trunk(x, mod, lp, nseq, seq, lat, layer=0, depth=1, carry=None):
    rows = nseq * seq
    proj = _inproj(x, mod, lp["norm_mix_g"], lp["w_in"], tm=min(2048, rows), tn=6 * MXU_WIDTH)
    if lat is None:
        attn, k_new, v_new = _attn_ctx(proj, lp["q_norm_g"], lp["k_norm_g"], nseq, seq, layer, depth,
                                       () if carry is None else carry[:2])
        *fwd, h_f = _ssd_pass(proj, lp, 0, nseq, seq, None, None, layer, depth, None if carry is None else carry[2])
        ssm, h_b = _ssd_pass(proj, lp, 1, nseq, seq, None, fwd, layer, depth, None if carry is None else carry[3])
    else:
        cache_k, cache_v, h0_f, h0_b, tables = lat
        kall, vall = _prep_lat(proj, cache_k, cache_v, *tables, lp["k_norm_g"], nseq, seq)
        attn = _attn_lat(proj, kall, vall, *tables, lp["q_norm_g"], nseq, seq, tq=256)
        k_new = v_new = None
        *fwd, h_f = _ssd_pass(proj, lp, 0, nseq, seq, h0_f)
        ssm, h_b = _ssd_pass(proj, lp, 1, nseq, seq, h0_b, fwd)
    outs = _merge(attn, ssm, proj, x, mod, lp, seq, tm=512)
    if lp["router"] is None:
        x = _ffn(outs[1], outs[0], mod, *lp["ffn"], tm=512)
    else:
        x = (outs[1], outs[0], mod, outs[2], outs[3])
    return x, (k_new, v_new, h_f, h_b)


def kernel(x_prompt, x_sample, cache_k, cache_v, state_ssm_fwd, state_ssm_bwd, c, c_ctx, w_mod, b_mod, norm_mix_g, norm_ffn_g, w_in, q_norm_g, k_norm_g, w_attn_out, ssm_conv_w, ssm_conv_b, ssm_dt_bias, ssm_a_log, ssm_d, ssm_norm_g, w_ssm_out, sconv_w, w_sconv_out, w_merge, ffn_w1, ffn_w3, ffn_w2, moe_router, moe_w1, moe_w3, moe_w2):
    batch, seq, _ = x_prompt.shape
    nb, dseq, _ = x_sample.shape
    depth = w_in.shape[0]
    past = cache_k.shape[2]
    kvw = N_KV_HEADS * HEAD_DIM
    npair = SSM_HEADS // 2
    assert nb + 1 <= SUBLANES

    cond = jnp.zeros((SUBLANES, D_MODEL), F32).at[0].set(c_ctx).at[1:1 + nb].set(c)
    tables = _rope_tables(dseq)
    y_ctx = x_prompt.reshape(batch * seq, D_MODEL)
    y_lat = x_sample.reshape(nb * dseq, D_MODEL)
    mods = _modulation(cond, w_mod, b_mod.reshape(depth, 1, -1))
    carry = None
    for l in range(depth):
        lp = _layer_params(l, w_in, q_norm_g, k_norm_g, w_attn_out, ssm_conv_w, ssm_conv_b, ssm_dt_bias,
                           ssm_a_log, ssm_d, ssm_norm_g, w_ssm_out, sconv_w, w_sconv_out, w_merge,
                           norm_mix_g, norm_ffn_g, ffn_w1, ffn_w3, ffn_w2, moe_router, moe_w1, moe_w3, moe_w2)
        m = jnp.pad(mods[l].reshape(SUBLANES, 6, D_MODEL), ((0, 0), (0, SUBLANES - 6), (0, 0)))
        y_ctx, carry = _trunk(y_ctx, m[0:1], lp, batch, seq, None, l, depth, carry)
        lat = (cache_k[:, l].reshape(nb, past, kvw), cache_v[:, l].reshape(nb, past, kvw),
               state_ssm_fwd[:, l].reshape(nb, npair, SSM_CHUNK, SSM_STATE),
               state_ssm_bwd[:, l].reshape(nb, npair, SSM_CHUNK, SSM_STATE), tables)
        y_lat = _trunk(y_lat, m[1:1 + nb], lp, nb, dseq, lat)[0]
        if lp["router"] is not None:
            y_ctx, y_lat = _moe([y_ctx, y_lat], *lp["ffn"], tile=512)
    k_all, v_all, hf_all, hb_all = carry
    state_shape = (batch, depth, SSM_HEADS, SSM_HEADDIM, SSM_STATE)
    return (y_ctx.reshape(batch, seq, D_MODEL), y_lat.reshape(nb, dseq, D_MODEL),
            k_all, v_all,
            hf_all.reshape(state_shape), hb_all.reshape(state_shape))
```

```python
import functools

import jax
import jax.numpy as jnp
import numpy as np
from jax import lax
from jax.experimental import pallas as pl
from jax.experimental.pallas import tpu as pltpu
from jax.experimental.pallas import tpu_sc as plsc

F32 = jnp.float32
BF16 = jnp.bfloat16
HIGHEST = lax.Precision.HIGHEST

D_MODEL = 1024
HEAD_DIM = 128
N_HEADS = 8
N_KV_HEADS = 2
Q_PER_KV = N_HEADS // N_KV_HEADS
ROPE_THETA = 10000.0
GRID_W = 64
SSM_HEADS = 16
SSM_HEADDIM = 64
SSM_STATE = 128
SSM_CHUNK = 128
D_FF = 2816
N_EXPERTS = 8
EPS = 1e-6
Q_SCALE = 1.4426950408889634 * HEAD_DIM ** -0.5

COL_Q, COL_Z, COL_SCB, COL_SCC, COL_SCH, COL_G = 0, 1024, 2048, 3072, 4096, 5120
COL_XS, COL_BC, COL_K, COL_V, COL_DT = 8192, 9216, 9728, 9984, 10240
MXU_WIDTH = 256
N_PROJ = 10752
LANES = 128
SUBLANES = 8
HALO = 16
MIB = 1024 * 1024


def _params(sem, vmem_mib):
    return pltpu.CompilerParams(dimension_semantics=sem, vmem_limit_bytes=vmem_mib * MIB)


def _sigmoid(t):
    return 1.0 / (1.0 + jnp.exp(-t))


def _silu(t):
    return t * _sigmoid(t)


def _rms(t):
    return t * lax.rsqrt(jnp.mean(t * t, axis=-1, keepdims=True) + EPS)


def _bdot(a, b):
    return jnp.dot(a.astype(BF16), b.astype(BF16), preferred_element_type=F32)


def _hdot(a, b):
    return jnp.dot(a, b, precision=HIGHEST, preferred_element_type=F32)


def _split3(t):
    a = t.astype(BF16)
    r = t - a.astype(F32)
    b = r.astype(BF16)
    return a, b, (r - b.astype(F32)).astype(BF16)


def _mod_kernel(c_ref, w_ref, b_ref, o_ref):
    o_ref[...] = _hdot(_silu(c_ref[...]), w_ref[...]) + b_ref[...]


def _modulation(cond8, w, b):
    depth, _, width = w.shape
    return pl.pallas_call(
        _mod_kernel,
        out_shape=jax.ShapeDtypeStruct((depth, SUBLANES, width), F32),
        grid=(depth, width // D_MODEL),
        in_specs=[pl.BlockSpec((SUBLANES, D_MODEL), lambda l, j: (0, 0)),
                  pl.BlockSpec((None, D_MODEL, D_MODEL), lambda l, j: (l, 0, j)),
                  pl.BlockSpec((None, 1, D_MODEL), lambda l, j: (l, 0, j))],
        out_specs=pl.BlockSpec((None, SUBLANES, D_MODEL), lambda l, j: (l, 0, j)),
        compiler_params=_params(("parallel", "parallel"), 32),
        name="modulation",
    )(cond8, w, b)


def _inproj_kernel(x_ref, mod_ref, g_ref, w_ref, o_ref, h_scr):
    @pl.when(pl.program_id(1) == 0)
    def _():
        h = _rms(x_ref[...]) * g_ref[...]
        h = h * (1.0 + mod_ref[0, 1:2, :]) + mod_ref[0, 0:1, :]
        h_scr[...] = h.astype(BF16)

    o_ref[...] = _qk(h_scr[...], w_ref[...]).astype(o_ref.dtype)


def _inproj(x, mod, g, w, tm, tn):
    rows = x.shape[0]
    per_mod = rows // mod.shape[0]
    tm = min(tm, per_mod)
    return pl.pallas_call(
        _inproj_kernel,
        out_shape=jax.ShapeDtypeStruct((rows, N_PROJ), BF16),
        grid=(rows // tm, N_PROJ // tn),
        in_specs=[pl.BlockSpec((tm, D_MODEL), lambda i, j: (i, 0)),
                  pl.BlockSpec((1, SUBLANES, D_MODEL), lambda i, j: (i * tm // per_mod, 0, 0)),
                  pl.BlockSpec((1, D_MODEL), lambda i, j: (0, 0)),
                  pl.BlockSpec((tn, D_MODEL), lambda i, j: (j, 0))],
        out_specs=pl.BlockSpec((tm, tn), lambda i, j: (i, j)),
        scratch_shapes=[pltpu.VMEM((tm, D_MODEL), BF16)],
        compiler_params=_params(("parallel", "arbitrary"), 48),
        name="inproj",
    )(x, mod, g, w)


def _softmax_pv(s, vb):
    m = jnp.max(s, axis=-1, keepdims=True)
    p = jnp.exp2(s - m)
    l = jnp.sum(p, axis=-1, keepdims=True)
    return jnp.dot(p.astype(BF16), vb, preferred_element_type=F32) / l


def _qk(qb, kb):
    return lax.dot_general(qb, kb, (((1,), (1,)), ((), ())), preferred_element_type=F32)


def _attn_ctx_kernel(q_ref, k_ref, v_ref, qg_ref, kg_ref, *rest):
    a_ref, ko_ref, vo_ref = rest[-3:]
    v = v_ref[...].astype(F32)
    for hk in range(N_KV_HEADS):
        ks = slice(hk * HEAD_DIM, (hk + 1) * HEAD_DIM)
        kn = _rms(k_ref[:, ks].astype(F32)) * kg_ref[...]
        ko_ref[:, hk, :] = kn
        vo_ref[:, hk, :] = v[:, ks]
        knb = kn.astype(BF16)
        vb = v[:, ks].astype(BF16)
        for g in range(Q_PER_KV):
            hs = slice((hk * Q_PER_KV + g) * HEAD_DIM, (hk * Q_PER_KV + g + 1) * HEAD_DIM)
            qn = _rms(q_ref[:, hs].astype(F32)) * (qg_ref[...] * Q_SCALE)
            s = _qk(qn.astype(BF16), knb)
            a_ref[:, hs] = _softmax_pv(s, vb).astype(BF16)


def _attn_ctx(proj, qg, kg, nseq, seq, layer, depth, earlier):
    kvw = N_KV_HEADS * HEAD_DIM
    cache = jax.ShapeDtypeStruct((nseq, depth, seq, N_KV_HEADS, HEAD_DIM), F32)
    cache_spec = pl.BlockSpec((None, None, seq, N_KV_HEADS, HEAD_DIM), lambda b: (b, layer, 0, 0, 0))
    n_in = 5
    return pl.pallas_call(
        _attn_ctx_kernel,
        out_shape=(jax.ShapeDtypeStruct((nseq * seq, D_MODEL), BF16), cache, cache),
        grid=(nseq,),
        in_specs=[pl.BlockSpec((seq, D_MODEL), lambda b: (b, COL_Q // D_MODEL)),
                  pl.BlockSpec((seq, kvw), lambda b: (b, COL_K // kvw)),
                  pl.BlockSpec((seq, kvw), lambda b: (b, COL_V // kvw)),
                  pl.BlockSpec((1, HEAD_DIM), lambda b: (0, 0)),
                  pl.BlockSpec((1, HEAD_DIM), lambda b: (0, 0))] + [pl.BlockSpec(memory_space=pl.ANY)] * len(earlier),
        out_specs=(pl.BlockSpec((seq, D_MODEL), lambda b: (b, 0)), cache_spec, cache_spec),
        input_output_aliases={n_in + i: 1 + i for i in range(len(earlier))},
        compiler_params=_params(("parallel",), 32),
        name="attn_ctx",
    )(proj, proj, proj, qg, kg, *earlier)


def _rope(t, cos, sa, sb):
    return t * cos + pltpu.roll(t, 96, 1) * sa + pltpu.roll(t, 32, 1) * sb


def _prep_lat_kernel(k_ref, v_ref, ck_ref, cv_ref, cos_ref, sa_ref, sb_ref, kg_ref, ko_ref, vo_ref):
    t = pl.program_id(1)

    @pl.when(t == 0)
    def _():
        ko_ref[...] = ck_ref[...].astype(BF16)
        vo_ref[...] = cv_ref[...].astype(BF16)

    @pl.when(t > 0)
    def _():
        cos, sa, sb = cos_ref[...], sa_ref[...], sb_ref[...]
        vo_ref[...] = v_ref[...]
        for h in range(N_KV_HEADS):
            hs = slice(h * HEAD_DIM, (h + 1) * HEAD_DIM)
            ko_ref[:, hs] = _rope(_rms(k_ref[:, hs].astype(F32)) * kg_ref[...], cos, sa, sb).astype(BF16)


def _prep_lat(proj, cache_k, cache_v, cos, sa, sb, kg, nb, seq):
    past = cache_k.shape[1]
    tr = past
    kvw = N_KV_HEADS * HEAD_DIM
    nt = seq // tr
    prev = lambda t: jnp.maximum(t - 1, 0)
    return pl.pallas_call(
        _prep_lat_kernel,
        out_shape=(jax.ShapeDtypeStruct((nb, past + seq, kvw), BF16),
                   jax.ShapeDtypeStruct((nb, past + seq, kvw), BF16)),
        grid=(nb, nt + 1),
        in_specs=[pl.BlockSpec((tr, kvw), lambda b, t: (b * nt + prev(t), COL_K // kvw)),
                  pl.BlockSpec((tr, kvw), lambda b, t: (b * nt + prev(t), COL_V // kvw)),
                  pl.BlockSpec((None, past, kvw), lambda b, t: (b, 0, 0)),
                  pl.BlockSpec((None, past, kvw), lambda b, t: (b, 0, 0)),
                  pl.BlockSpec((tr, HEAD_DIM), lambda b, t: (prev(t), 0)),
                  pl.BlockSpec((tr, HEAD_DIM), lambda b, t: (prev(t), 0)),
                  pl.BlockSpec((tr, HEAD_DIM), lambda b, t: (prev(t), 0)),
                  pl.BlockSpec((1, HEAD_DIM), lambda b, t: (0, 0))],
        out_specs=(pl.BlockSpec((None, tr, kvw), lambda b, t: (b, t, 0)),
                   pl.BlockSpec((None, tr, kvw), lambda b, t: (b, t, 0))),
        compiler_params=_params(("parallel", "arbitrary"), 32),
        name="prep_lat",
    )(proj, proj, cache_k, cache_v, cos, sa, sb, kg)


def _attn_lat_kernel(q_ref, cos_ref, sa_ref, sb_ref, qg_ref, k_ref, v_ref, o_ref):
    cos, sa, sb = cos_ref[...], sa_ref[...], sb_ref[...]
    gain = qg_ref[...] * Q_SCALE
    for hk in range(N_KV_HEADS):
        kb = k_ref[:, hk * HEAD_DIM:(hk + 1) * HEAD_DIM]
        vb = v_ref[:, hk * HEAD_DIM:(hk + 1) * HEAD_DIM]
        for g in range(Q_PER_KV):
            hs = slice((hk * Q_PER_KV + g) * HEAD_DIM, (hk * Q_PER_KV + g + 1) * HEAD_DIM)
            qn = _rope(_rms(q_ref[:, hs].astype(F32)) * gain, cos, sa, sb)
            s = _qk(qn.astype(BF16), kb)
            o_ref[:, hs] = _softmax_pv(s, vb).astype(BF16)


def _attn_lat(proj, kall, vall, cos, sa, sb, qg, nb, seq, tq):
    nk, kvw = kall.shape[1:]
    nq = seq // tq
    table = pl.BlockSpec((tq, HEAD_DIM), lambda b, i: (i, 0))
    return pl.pallas_call(
        _attn_lat_kernel,
        out_shape=jax.ShapeDtypeStruct((nb * seq, D_MODEL), BF16),
        grid=(nb, nq),
        in_specs=[pl.BlockSpec((tq, D_MODEL), lambda b, i: (b * nq + i, COL_Q // D_MODEL)),
                  table, table, table, pl.BlockSpec((1, HEAD_DIM), lambda b, i: (0, 0)),
                  pl.BlockSpec((None, nk, kvw), lambda b, i: (b, 0, 0)),
                  pl.BlockSpec((None, nk, kvw), lambda b, i: (b, 0, 0))],
        out_specs=pl.BlockSpec((tq, D_MODEL), lambda b, i: (b * nq + i, 0)),
        compiler_params=_params(("parallel", "arbitrary"), 48),
        name="attn_lat",
    )(proj, cos, sa, sb, qg, kall, vall)


SSD_GROUP = 4


def _ssd_kernel(*refs, reverse, has_h0, nc):
    it = iter(refs)
    if reverse:
        xsc_ref, bcc_ref, dt_ref, alog_ref, dtb_ref, e_ref = (next(it) for _ in range(6))
    else:
        xs_ref, bc_ref, dt_ref, xsp_ref, xsn_ref, bcp_ref, bcn_ref = (next(it) for _ in range(7))
        cwx_ref, cbx_ref, cwb_ref, cbb_ref, alog_ref, dtb_ref, e_ref, shift_ref = (next(it) for _ in range(8))
    h0_ref = next(it) if has_h0 else None
    if reverse:
        z_ref, yf_ref, d_ref, g_ref = (next(it) for _ in range(4))
        y_ref, hout_ref, ht_scr = refs[-3:]
    else:
        y_ref, xsc_ref, bcc_ref, hout_ref, ht_scr = refs[-5:]

    q = SSM_CHUNK
    npair = SSM_HEADS // 2
    group = y_ref.shape[0]
    c = pl.program_id(1)
    cc = nc - 1 - c if reverse else c

    @pl.when(c == 0)
    def _():
        for s in range(group):
            for p in range(npair):
                if has_h0:
                    ht_scr[s, p] = h0_ref[s, p].T
                else:
                    ht_scr[s, p] = jnp.zeros((q, q), F32)

    off = SSM_HEADS if reverse else 0
    ii = lax.broadcasted_iota(jnp.int32, (q, q), 0)
    jj = lax.broadcasted_iota(jnp.int32, (q, q), 1)
    tri = (jj >= ii) if reverse else (ii >= jj)
    tri_b = jnp.where(tri, 1.0, 0.0).astype(BF16)
    tri_tb3 = jnp.concatenate([jnp.where((ii >= jj) if reverse else (jj >= ii), 1.0, 0.0).astype(BF16)] * 3, axis=0)
    lo = lax.broadcasted_iota(jnp.int32, (q, LANES), 1) < SSM_HEADDIM
    neg_a = -jnp.exp(alog_ref[...])
    if not reverse:
        keep_prev = jnp.where(cc == 0, 0.0, 1.0).astype(BF16)
        keep_next = jnp.where(cc == nc - 1, 0.0, 1.0).astype(BF16)

    def conv_silu(s, x_ref_, p_ref_, n_ref_, w_ref_, b_ref_):
        x = x_ref_[s]
        ext = jnp.concatenate([x, p_ref_[s] * keep_prev, n_ref_[s] * keep_next], axis=0)
        nb = jnp.dot(shift_ref[...], ext, preferred_element_type=F32)
        w = w_ref_[...]
        return _silu(w[0:1, :] * nb[:q, :] + w[1:2, :] * x.astype(F32) + w[2:3, :] * nb[q:, :] + b_ref_[...])

    for s in range(group):
        if reverse:
            xs = xsc_ref[s].astype(F32)
            bc = bcc_ref[s].astype(F32)
        else:
            xs = conv_silu(s, xs_ref, xsp_ref, xsn_ref, cwx_ref, cbx_ref)
            bc = conv_silu(s, bc_ref, bcp_ref, bcn_ref, cwb_ref, cbb_ref)
            xsc_ref[s] = xs.astype(BF16)
            bcc_ref[s] = bc.astype(BF16)

        dt = jax.nn.softplus(dt_ref[s].astype(F32) + dtb_ref[...])
        da = dt * neg_a
        acs3 = jnp.dot(tri_b, jnp.concatenate(_split3(da), axis=1), preferred_element_type=F32)
        acs = acs3[:, :LANES] + acs3[:, LANES:2 * LANES] + acs3[:, 2 * LANES:]
        acs_t = jnp.dot(jnp.concatenate(_split3(da.T), axis=1), tri_tb3, preferred_element_type=F32)
        spread = jnp.dot(jnp.concatenate(_split3(jnp.concatenate([dt, acs], axis=0)), axis=1), e_ref[...],
                         preferred_element_type=F32)
        acs_x = spread[q:, :]
        tot_x = acs_x[0:1, :] if reverse else acs_x[q - 1:q, :]
        xdt = xs * spread[:q, :]
        xw = xdt * jnp.exp(tot_x - acs_x)
        eacs_x = jnp.exp(acs_x)
        etot_x = jnp.exp(tot_x)

        parts = []
        for g in range(2):
            bg = bc[:, g * q:(g + 1) * q]
            cg = bc[:, (2 + g) * q:(3 + g) * q].astype(BF16)
            cb = _qk(cg, bg.astype(BF16))
            bgt = bg.T.astype(BF16)
            for pp in range(npair // 2):
                p = g * (npair // 2) + pp
                ps = slice(p * LANES, (p + 1) * LANES)
                ms = []
                for hh in range(2):
                    ln = off + 2 * p + hh
                    dec = jnp.where(tri, jnp.exp(acs[:, ln:ln + 1] - acs_t[ln:ln + 1, :]), 0.0)
                    ms.append((cb * dec).astype(BF16))
                m = jnp.concatenate(ms, axis=1)
                xp = xdt[:, ps]
                bd = jnp.concatenate([jnp.where(lo, xp, 0.0), jnp.where(lo, 0.0, xp)], axis=0).astype(BF16)
                ht = ht_scr[s, p]
                yp = jnp.dot(m, bd, preferred_element_type=F32)
                yp = yp + jnp.dot(cg, ht.astype(BF16), preferred_element_type=F32) * eacs_x[:, ps]
                ht_scr[s, p] = etot_x[:, ps] * ht + jnp.dot(bgt, xw[:, ps].astype(BF16),
                                                            preferred_element_type=F32)
                parts.append(yp)
        y = jnp.concatenate(parts, axis=1)

        if reverse:
            y = y + yf_ref[s] + d_ref[...] * xs
            y = y * _silu(z_ref[s].astype(F32))
            y_ref[s] = (_rms(y) * g_ref[...]).astype(y_ref.dtype)
        else:
            y_ref[s] = y

    @pl.when(c == nc - 1)
    def _():
        for s in range(group):
            for p in range(npair):
                hout_ref[s, p] = ht_scr[s, p].T


def _ssd_pass(proj, lp, d, nseq, seq, h0, fwd=None, layer=0, depth=1, earlier=None):
    reverse = fwd is not None
    has_h0 = h0 is not None
    q = SSM_CHUNK
    nc = seq // q
    g = min(SSD_GROUP, nseq)
    assert nseq % g == 0
    npair = SSM_HEADS // 2
    hb = q // HALO
    bcw = 4 * SSM_STATE
    chunk = (lambda c: nc - 1 - c) if reverse else (lambda c: c)
    prev8 = lambda c: jnp.maximum(chunk(c) * hb - 1, 0)
    next8 = lambda c: jnp.minimum((chunk(c) + 1) * hb, seq // HALO - 1)
    const = lambda s, c: (0, 0)
    tile = lambda width, col=0: pl.BlockSpec((g, q, width), lambda s, c: (s, chunk(c), col // width))
    halo = lambda width, col, where: pl.BlockSpec((g, HALO, width), lambda s, c: (s, where(c), col // width))
    seqs = lambda t: t.reshape(nseq, seq, t.shape[-1])
    scan_consts = [pl.BlockSpec((1, LANES), const), pl.BlockSpec((1, LANES), const),
                   pl.BlockSpec((3 * LANES, D_MODEL), const)]
    scan_args = [lp["alog"][d], lp["dtb"][d], lp["expand"][d]]
    proj3 = seqs(proj)
    if reverse:
        yf, xsc, bcc = fwd
        in_specs = [tile(D_MODEL), tile(bcw), tile(LANES, COL_DT)] + scan_consts
        args = [xsc, bcc, proj3] + scan_args
    else:
        in_specs = [tile(D_MODEL, COL_XS), tile(bcw, COL_BC), tile(LANES, COL_DT),
                    halo(D_MODEL, COL_XS, prev8), halo(D_MODEL, COL_XS, next8),
                    halo(bcw, COL_BC, prev8), halo(bcw, COL_BC, next8),
                    pl.BlockSpec((3, D_MODEL), const), pl.BlockSpec((1, D_MODEL), const),
                    pl.BlockSpec((3, bcw), const), pl.BlockSpec((1, bcw), const)] + scan_consts + [
                    pl.BlockSpec((2 * q, q + 2 * HALO), const)]
        args = [proj3] * 7 + [lp["cw_x"], lp["cb_x"], lp["cw_bc"], lp["cb_bc"]] + scan_args + [lp["shift"]]
    if has_h0:
        in_specs.append(pl.BlockSpec((g, npair, q, q), lambda s, c: (s, 0, 0, 0)))
        args.append(h0)
    if reverse:
        in_specs += [tile(D_MODEL, COL_Z), tile(D_MODEL), pl.BlockSpec((1, D_MODEL), const),
                     pl.BlockSpec((1, D_MODEL), const)]
        args += [proj3, yf, lp["d_row"], lp["ssm_norm_g"]]
    state = jax.ShapeDtypeStruct((nseq, depth, npair, q, q), F32)
    state_spec = pl.BlockSpec((g, None, npair, q, q), lambda s, c: (s, layer, 0, 0, 0))
    if reverse:
        out_shape = (jax.ShapeDtypeStruct((nseq, seq, D_MODEL), BF16), state)
        out_specs = (tile(D_MODEL), state_spec)
    else:
        out_shape = (jax.ShapeDtypeStruct((nseq, seq, D_MODEL), F32),
                     jax.ShapeDtypeStruct((nseq, seq, D_MODEL), BF16),
                     jax.ShapeDtypeStruct((nseq, seq, bcw), BF16), state)
        out_specs = (tile(D_MODEL), tile(D_MODEL), tile(bcw), state_spec)
    aliases = {}
    if earlier is not None:
        aliases = {len(args): len(out_shape) - 1}
        in_specs.append(pl.BlockSpec(memory_space=pl.ANY))
        args.append(earlier)
    outs = pl.pallas_call(
        functools.partial(_ssd_kernel, reverse=reverse, has_h0=has_h0, nc=nc),
        out_shape=out_shape,
        grid=(nseq // g, nc),
        in_specs=in_specs,
        out_specs=out_specs,
        input_output_aliases=aliases,
        scratch_shapes=[pltpu.VMEM((g, npair, q, q), F32)],
        compiler_params=_params(("parallel", "arbitrary"), 48),
        name="ssd_bwd" if reverse else "ssd_fwd",
    )(*args)
    if reverse:
        return outs[0].reshape(nseq * seq, D_MODEL), outs[1]
    return outs


ROUTE_I1, ROUTE_I2, ROUTE_P1, ROUTE_P2, ROUTE_R1, ROUTE_R2 = 0, 1, 2, 3, 4, 5


def _pack_bf16_pairs(v):
    n = v.shape[1] // 2
    bits = lax.bitcast_convert_type(v.astype(BF16).astype(F32), jnp.uint32)
    return lax.bitcast_convert_type((bits[:, :n] >> 16) | bits[:, n:], jnp.int32)


def _unpack_bf16_pairs(p):
    u = lax.bitcast_convert_type(p, jnp.uint32)
    lo = lax.bitcast_convert_type(u << 16, F32)
    hi = lax.bitcast_convert_type(u & jnp.uint32(0xFFFF0000), F32)
    return jnp.concatenate([lo, hi], axis=1)


def _top2_route(logits):
    tm = logits.shape[0]
    lane = lax.broadcasted_iota(jnp.int32, logits.shape, 1).astype(F32)
    lg = jnp.where(lane < N_EXPERTS, logits, -jnp.inf)
    m1 = jnp.max(lg, axis=-1, keepdims=True)
    i1 = jnp.min(jnp.where(lg == m1, lane, float(LANES)), axis=-1, keepdims=True)
    lg2 = jnp.where(lane == i1, -jnp.inf, lg)
    m2 = jnp.max(lg2, axis=-1, keepdims=True)
    i2 = jnp.min(jnp.where(lg2 == m2, lane, float(LANES)), axis=-1, keepdims=True)
    e = jnp.exp(m2 - m1)
    p1 = 1.0 / (1.0 + e)
    chosen = jnp.where(lane == i1, 1.0, jnp.where(lane == i2, 1.0, 0.0))
    earlier = (lax.broadcasted_iota(jnp.int32, (tm, tm), 0) > lax.broadcasted_iota(jnp.int32, (tm, tm), 1))
    ranks = _bdot(jnp.where(earlier, 1.0, 0.0), chosen)
    r1 = jnp.sum(jnp.where(lane == i1, ranks, 0.0), axis=-1, keepdims=True)
    r2 = jnp.sum(jnp.where(lane == i2, ranks, 0.0), axis=-1, keepdims=True)
    rec = jnp.where(lane == ROUTE_I1, i1, jnp.where(lane == ROUTE_I2, i2, 0.0))
    rec = jnp.where(lane == ROUTE_P1, p1, jnp.where(lane == ROUTE_P2, e * p1, rec))
    rec = jnp.where(lane == ROUTE_R1, r1, jnp.where(lane == ROUTE_R2, r2, rec))
    return rec, jnp.sum(chosen, axis=0, keepdims=True)


def _merge_kernel(*refs, seq, tm, moe):
    it = iter(refs)
    attn_ref, ssm_ref, scb_ref, scc_ref, sch_ref = (next(it) for _ in range(5))
    sccp_ref, sccn_ref, schp_ref, schn_ref = (next(it) for _ in range(4))
    g0_ref, g1_ref, g2_ref, x_ref, mod_ref = (next(it) for _ in range(5))
    wa_ref, ws_ref, wc_ref, wm_ref, scw_ref, ng_ref = (next(it) for _ in range(6))
    router_ref = next(it) if moe else None
    xo_ref, h2_ref = next(it), next(it)
    route_ref, cnt_ref = (next(it), next(it)) if moe else (None, None)

    i = pl.program_id(0)
    pos = (lax.broadcasted_iota(jnp.int32, (tm, 1), 0) + i * tm) & (seq - 1)
    u = scc_ref[...].astype(F32) * sch_ref[...].astype(F32)
    up = sccp_ref[HALO - 1:HALO, :].astype(F32) * schp_ref[HALO - 1:HALO, :].astype(F32)
    un = sccn_ref[0:1, :].astype(F32) * schn_ref[0:1, :].astype(F32)
    r = lax.broadcasted_iota(jnp.int32, (tm, 1), 0)
    um1 = jnp.where(pos == 0, 0.0, jnp.where(r == 0, up, pltpu.roll(u, 1, 0)))
    up1 = jnp.where(pos == seq - 1, 0.0, jnp.where(r == tm - 1, un, pltpu.roll(u, tm - 1, 0)))
    w = scw_ref[...]
    sc = scb_ref[...].astype(F32) * (w[0:1, :] * um1 + w[1:2, :] * u + w[2:3, :] * up1)

    gate = lambda g_ref: _sigmoid(g_ref[...].astype(F32))
    merged = gate(g0_ref) * jnp.dot(attn_ref[...], wa_ref[...], preferred_element_type=F32)
    merged = merged + gate(g1_ref) * jnp.dot(ssm_ref[...], ws_ref[...], preferred_element_type=F32)
    merged = merged + gate(g2_ref) * jnp.dot(sc.astype(BF16), wc_ref[...], preferred_element_type=F32)
    mod = mod_ref[0]
    xn = x_ref[...] + mod[2:3, :] * jnp.dot(merged.astype(BF16), wm_ref[...], preferred_element_type=F32)
    xo_ref[...] = xn
    h2 = _rms(xn) * ng_ref[...] * (1.0 + mod[4:5, :]) + mod[3:4, :]
    if moe:
        h2_ref[...] = _pack_bf16_pairs(h2)
        h_hi = h2.astype(BF16)
        h_lo = (h2 - h_hi.astype(F32)).astype(BF16)
        both = jnp.dot(h_hi, router_ref[...], preferred_element_type=F32)
        logits = both[:, :LANES] + both[:, LANES:] + jnp.dot(h_lo, router_ref[:, :LANES], preferred_element_type=F32)
        rec, counts = _top2_route(logits)
        route_ref[...] = rec
        cnt_ref[...] = jnp.broadcast_to(counts, cnt_ref.shape)
    else:
        h2_ref[...] = h2.astype(BF16)


def _merge(attn, ssm, proj, x, mod, lp, seq, tm):
    rows = x.shape[0]
    per_mod = rows // mod.shape[0]
    tm = min(tm, per_mod)
    moe = lp["router"] is not None
    hb = tm // HALO
    prev8 = lambda i: jnp.maximum(i * hb - 1, 0)
    next8 = lambda i: jnp.minimum((i + 1) * hb, rows // HALO - 1)
    tile = lambda col: pl.BlockSpec((tm, D_MODEL), lambda i: (i, col // D_MODEL))
    const = lambda shape: pl.BlockSpec(shape, lambda i: (0, 0))
    in_specs = [tile(0), tile(0), tile(COL_SCB), tile(COL_SCC), tile(COL_SCH),
                pl.BlockSpec((HALO, D_MODEL), lambda i: (prev8(i), COL_SCC // D_MODEL)),
                pl.BlockSpec((HALO, D_MODEL), lambda i: (next8(i), COL_SCC // D_MODEL)),
                pl.BlockSpec((HALO, D_MODEL), lambda i: (prev8(i), COL_SCH // D_MODEL)),
                pl.BlockSpec((HALO, D_MODEL), lambda i: (next8(i), COL_SCH // D_MODEL)),
                tile(COL_G), tile(COL_G + D_MODEL), tile(COL_G + 2 * D_MODEL), tile(0),
                pl.BlockSpec((1, SUBLANES, D_MODEL), lambda i: (i * tm // per_mod, 0, 0)),
                _resident((D_MODEL, D_MODEL)), _resident((D_MODEL, D_MODEL)), _resident((D_MODEL, D_MODEL)),
                _resident((D_MODEL, D_MODEL)), const((3, D_MODEL)), const((1, D_MODEL))]
    args = [attn, ssm, proj, proj, proj, proj, proj, proj, proj, proj, proj, proj, x, mod,
            lp["w_attn_out"], lp["w_ssm_out"], lp["w_sconv_out"], lp["w_merge"], lp["sconv_w"], lp["norm_ffn_g"]]
    out_shape = [jax.ShapeDtypeStruct((rows, D_MODEL), F32), jax.ShapeDtypeStruct((rows, D_MODEL), BF16)]
    out_specs = [tile(0), tile(0)]
    if moe:
        in_specs.append(_resident((D_MODEL, 2 * LANES)))
        args.append(lp["router"])
        out_shape[1] = jax.ShapeDtypeStruct((rows, D_MODEL // 2), jnp.int32)
        out_specs[1] = pl.BlockSpec((tm, D_MODEL // 2), lambda i: (i, 0))
        out_shape += [jax.ShapeDtypeStruct((rows, LANES), F32),
                      jax.ShapeDtypeStruct((rows // tm * SUBLANES, LANES), F32)]
        out_specs += [pl.BlockSpec((tm, LANES), lambda i: (i, 0)), pl.BlockSpec((SUBLANES, LANES), lambda i: (i, 0))]
    return pl.pallas_call(
        functools.partial(_merge_kernel, seq=seq, tm=tm, moe=moe),
        out_shape=tuple(out_shape),
        grid=(rows // tm,),
        in_specs=in_specs,
        out_specs=tuple(out_specs),
        compiler_params=_params(("parallel",), 56),
        name="merge",
    )(*args)


def _swiglu_acc(h, w1_ref, w3_ref, w2_ref, fs):
    a = _silu(jnp.dot(h, w1_ref[:, fs], preferred_element_type=F32)) * jnp.dot(h, w3_ref[:, fs],
                                                                                preferred_element_type=F32)
    return jnp.dot(a.astype(BF16), w2_ref[fs, :], preferred_element_type=F32)


def _swiglu(h, w1_ref, w3_ref, w2_ref):
    cut = (D_FF // 2 + MXU_WIDTH - 1) // MXU_WIDTH * MXU_WIDTH
    return (_swiglu_acc(h, w1_ref, w3_ref, w2_ref, slice(0, cut))
            + _swiglu_acc(h, w1_ref, w3_ref, w2_ref, slice(cut, D_FF)))


def _ffn_kernel(h_ref, x_ref, mod_ref, w1_ref, w3_ref, w2_ref, o_ref):
    o_ref[...] = x_ref[...] + mod_ref[0, 5:6, :] * _swiglu(h_ref[...], w1_ref, w3_ref, w2_ref)


def _resident(shape):
    return pl.BlockSpec(shape, lambda *_: (0,) * len(shape), pipeline_mode=pl.Buffered(1))


def _ffn(h2, x, mod, w1, w3, w2, tm):
    rows = x.shape[0]
    per_mod = rows // mod.shape[0]
    tm = min(tm, per_mod)
    return pl.pallas_call(
        _ffn_kernel,
        out_shape=jax.ShapeDtypeStruct((rows, D_MODEL), F32),
        grid=(rows // tm,),
        in_specs=[pl.BlockSpec((tm, D_MODEL), lambda i: (i, 0)),
                  pl.BlockSpec((tm, D_MODEL), lambda i: (i, 0)),
                  pl.BlockSpec((1, SUBLANES, D_MODEL), lambda i: (i * tm // per_mod, 0, 0)),
                  _resident((D_MODEL, D_FF)), _resident((D_MODEL, D_FF)), _resident((D_FF, D_MODEL))],
        out_specs=pl.BlockSpec((tm, D_MODEL), lambda i: (i, 0)),
        compiler_params=_params(("parallel",), 56),
        name="ffn",
    )(h2, x, mod, w1, w3, w2)


def _cast_kernel(*refs):
    n = len(refs) // 2
    for src, dst in zip(refs[:n], refs[n:]):
        dst[...] = src[...].astype(dst.dtype)


def _cast_bf16(ws):
    e, r, c = ws[0].shape
    rb = max(d for d in range(16, r + 1, 16) if r % d == 0 and d * c * 4 <= 3 * MIB)
    spec = pl.BlockSpec((None, rb, c), lambda i, j: (i, j, 0))
    return pl.pallas_call(
        _cast_kernel,
        out_shape=tuple(jax.ShapeDtypeStruct(w.shape, BF16) for w in ws),
        grid=(e, r // rb),
        in_specs=[spec] * len(ws),
        out_specs=tuple([spec] * len(ws)),
        compiler_params=_params(("parallel", "parallel"), 48),
        name="cast_bf16",
    )(*ws)


SC_CORES = 2
SC_SUBCORES = 16
SC_CHUNK = 128


def _sc_gather(table, idx):
    nw = SC_CORES * SC_SUBCORES
    b, d = idx.shape[0], table.shape[1]
    per_w = b // nw
    ch = SC_CHUNK // 2
    n = per_w // ch
    assert per_w * nw == b and n * ch == per_w
    mesh = plsc.VectorSubcoreMesh(core_axis_name="c", subcore_axis_name="s")

    def body(table_hbm, idx_hbm, out_hbm, idx_a, idx_b, rows_a, rows_b, sem_a, sem_b):
        wid = lax.axis_index("s") * SC_CORES + lax.axis_index("c")
        bufs = ((idx_a, rows_a, sem_a), (idx_b, rows_b, sem_b))
        offset = lambda j: pl.multiple_of(wid * per_w + j * ch, ch)

        def fetch(j):
            idx_v, rows_v, sem = bufs[j % 2]
            pltpu.sync_copy(idx_hbm.at[pl.ds(offset(j), ch)], idx_v)
            return pltpu.async_copy(table_hbm.at[idx_v], rows_v, sem)

        pending = fetch(0)
        for j in range(n):
            following = fetch(j + 1) if j + 1 < n else None
            pending.wait()
            pltpu.sync_copy(bufs[j % 2][1], out_hbm.at[pl.ds(offset(j), ch)])
            pending = following

    return pl.kernel(
        body,
        out_type=jax.ShapeDtypeStruct((b, d), table.dtype),
        mesh=mesh,
        scratch_types=[pltpu.VMEM((ch,), jnp.int32)] * 2 + [pltpu.VMEM((ch, d), table.dtype)] * 2
        + [pltpu.SemaphoreType.DMA] * 2,
        name="sc_gather",
    )(table, idx)


def _sc_dispatch(tables, slot, n_slots):
    nw = SC_CORES * SC_SUBCORES
    n_tab = len(tables)
    rows, d = tables[0].shape
    t = n_tab * rows
    per_w = t // nw
    w_per_tab = nw // n_tab
    assert all(tb.shape == (rows, d) for tb in tables) and w_per_tab * n_tab == nw
    assert per_w * nw == t and per_w % SC_CHUNK == 0 and slot.shape[0] == 2 * t
    mesh = plsc.VectorSubcoreMesh(core_axis_name="c", subcore_axis_name="s")

    def body(*refs):
        tabs = refs[:n_tab]
        slot_hbm, out_hbm, idx_v, rows_v, sem = refs[n_tab:]
        wid = lax.axis_index("s") * SC_CORES + lax.axis_index("c")

        @pl.loop(0, per_w // SC_CHUNK)
        def _(j):
            off = pl.multiple_of(wid * per_w + j * SC_CHUNK, SC_CHUNK)
            for i, tab in enumerate(tabs):
                @pl.when(wid // w_per_tab == i)
                def _():
                    pltpu.sync_copy(tab.at[pl.ds(pl.multiple_of(off - i * rows, SC_CHUNK), SC_CHUNK)], rows_v)
            for k in range(2):
                pltpu.sync_copy(slot_hbm.at[pl.ds(pl.multiple_of(k * t + off, SC_CHUNK), SC_CHUNK)], idx_v)
                pltpu.async_copy(rows_v, out_hbm.at[idx_v], sem).wait()

    return pl.kernel(
        body,
        out_type=jax.ShapeDtypeStruct((n_slots, d), tables[0].dtype),
        mesh=mesh,
        scratch_types=[pltpu.VMEM((SC_CHUNK,), jnp.int32), pltpu.VMEM((SC_CHUNK, d), tables[0].dtype),
                       pltpu.SemaphoreType.DMA],
        name="sc_dispatch",
    )(*tables, slot)


def _route_tables(route, counts, tile):
    t = route.shape[0]
    n_mt = counts.shape[0] // SUBLANES
    cnt = counts.reshape(n_mt, SUBLANES, LANES)[:, 0, :N_EXPERTS].astype(jnp.int32)
    incl = jnp.cumsum(cnt, axis=0)
    padded = (incl[-1] + tile - 1) // tile * tile
    gend = jnp.cumsum(padded)
    base = jnp.repeat((gend - padded)[None, :] + incl - cnt, t // n_mt, axis=0)
    e = route[:, ROUTE_I1:ROUTE_I2 + 1].astype(jnp.int32)
    r = route[:, ROUTE_R1:ROUTE_R2 + 1].astype(jnp.int32)
    pick = e[:, :, None] == jnp.arange(N_EXPERTS)[None, None, :]
    slot = jnp.sum(jnp.where(pick, base[:, None, :], 0), axis=-1) + r
    n_slots = 2 * t + N_EXPERTS * tile
    n_valid = gend[-1] // tile
    n_tiles = n_slots // tile
    tile_expert = jnp.sum(jnp.arange(n_tiles)[:, None] * tile >= gend[None, :], axis=1)
    last = jnp.take(tile_expert, jnp.maximum(n_valid - 1, 0))
    tile_expert = jnp.where(jnp.arange(n_tiles) < n_valid, tile_expert, last).astype(jnp.int32)
    return slot.astype(jnp.int32), n_slots, tile_expert, n_valid.reshape(1).astype(jnp.int32)


def _moe_ffn_kernel(te_ref, nv_ref, h_ref, w1_ref, w3_ref, w2_ref, o_ref):
    i = pl.program_id(0)

    @pl.when(i < nv_ref[0])
    def _():
        h = _unpack_bf16_pairs(h_ref[...]).astype(BF16)
        o_ref[...] = _pack_bf16_pairs(_swiglu(h, w1_ref, w3_ref, w2_ref))

    @pl.when(i >= nv_ref[0])
    def _():
        o_ref[...] = jnp.zeros(o_ref.shape, jnp.int32)


def _moe_ffn(hs, tile_expert, n_valid, w1, w3, w2, tile):
    n_slots = hs.shape[0]
    grid_spec = pltpu.PrefetchScalarGridSpec(
        num_scalar_prefetch=2,
        grid=(n_slots // tile,),
        in_specs=[pl.BlockSpec((tile, D_MODEL // 2), lambda i, te, nv: (i, 0)),
                  pl.BlockSpec((None, D_MODEL, D_FF), lambda i, te, nv: (te[i], 0, 0)),
                  pl.BlockSpec((None, D_MODEL, D_FF), lambda i, te, nv: (te[i], 0, 0)),
                  pl.BlockSpec((None, D_FF, D_MODEL), lambda i, te, nv: (te[i], 0, 0))],
        out_specs=pl.BlockSpec((tile, D_MODEL // 2), lambda i, te, nv: (i, 0)),
    )
    return pl.pallas_call(
        _moe_ffn_kernel,
        out_shape=jax.ShapeDtypeStruct((n_slots, D_MODEL // 2), jnp.int32),
        grid_spec=grid_spec,
        compiler_params=_params(("arbitrary",), 56),
        name="moe_ffn",
    )(tile_expert, n_valid, hs, w1, w3, w2)


def _combine_kernel(x_ref, y1_ref, y2_ref, route_ref, mod_ref, o_ref):
    p1 = route_ref[:, ROUTE_P1:ROUTE_P1 + 1]
    p2 = route_ref[:, ROUTE_P2:ROUTE_P2 + 1]
    f = p1 * _unpack_bf16_pairs(y1_ref[...]) + p2 * _unpack_bf16_pairs(y2_ref[...])
    o_ref[...] = x_ref[...] + mod_ref[0, 5:6, :] * f


def _combine(x, yg, route, mod, tm):
    rows = x.shape[0]
    per_mod = rows // mod.shape[0]
    tm = min(tm, per_mod)
    return pl.pallas_call(
        _combine_kernel,
        out_shape=jax.ShapeDtypeStruct((rows, D_MODEL), F32),
        grid=(rows // tm,),
        in_specs=[pl.BlockSpec((tm, D_MODEL), lambda i: (i, 0)),
                  pl.BlockSpec((tm, D_MODEL // 2), lambda i: (i, 0)),
                  pl.BlockSpec((tm, D_MODEL // 2), lambda i: (i + rows // tm, 0)),
                  pl.BlockSpec((tm, LANES), lambda i: (i, 0)),
                  pl.BlockSpec((1, SUBLANES, D_MODEL), lambda i: (i * tm // per_mod, 0, 0))],
        out_specs=pl.BlockSpec((tm, D_MODEL), lambda i: (i, 0)),
        compiler_params=_params(("parallel",), 32),
        name="moe_combine",
    )(x, yg, yg, route, mod)


def _moe(passes, w1, w3, w2, tile):
    route = jnp.concatenate([p[3] for p in passes], axis=0)
    counts = jnp.concatenate([p[4] for p in passes], axis=0)
    slot, n_slots, tile_expert, n_valid = _route_tables(route, counts, tile)
    hs = _sc_dispatch([p[0] for p in passes], slot.T.reshape(-1), n_slots)
    ys = _moe_ffn(hs, tile_expert, n_valid, w1, w3, w2, tile)
    outs, first = [], 0
    for _, x, mod, rt, _ in passes:
        own = slot[first:first + x.shape[0]].T.reshape(-1)
        outs.append(_combine(x, _sc_gather(ys, own), rt, mod, tm=512))
        first += x.shape[0]
    return outs


W_IN_SEGMENTS = ((0, 1024), (1536, 1024), (4128, 1024), (5152, 1024), (6176, 1024), (7200, 3072),
                 (2560, 1024), (3584, 512), (1024, 256), (1280, 256), (4096, 32))


def _permute_kernel(w_ref, o_ref):
    dst = 0
    for src, n in W_IN_SEGMENTS:
        o_ref[dst:dst + n, :] = w_ref[src:src + n, :].astype(BF16)
        dst += n
    o_ref[dst:, :] = jnp.zeros((o_ref.shape[0] - dst, o_ref.shape[1]), BF16)


def _permute_w_in(w_in, l, cb=128):
    w_t = jnp.swapaxes(w_in, 1, 2)
    _, feats, d = w_t.shape
    return pl.pallas_call(
        _permute_kernel,
        out_shape=jax.ShapeDtypeStruct((N_PROJ, d), BF16),
        grid=(d // cb,),
        in_specs=[pl.BlockSpec((None, feats, cb), lambda i: (l, 0, i))],
        out_specs=pl.BlockSpec((N_PROJ, cb), lambda i: (0, i)),
        compiler_params=_params(("parallel",), 48),
        name="permute_w_in",
    )(w_t)


def _layer_params(l, w_in, q_norm_g, k_norm_g, w_attn_out, ssm_conv_w, ssm_conv_b, ssm_dt_bias, ssm_a_log,
                  ssm_d, ssm_norm_g, w_ssm_out, sconv_w, w_sconv_out, w_merge, norm_mix_g, norm_ffn_g,
                  ffn_w1, ffn_w3, ffn_w2, moe_router, moe_w1, moe_w3, moe_w2):
    w_perm = _permute_w_in(w_in, l)
    row = lambda v: v.reshape(1, -1)
    lane_row = lambda v, off: jnp.zeros((1, LANES), F32).at[0, off:off + SSM_HEADS].set(v)
    head_of_col = np.arange(D_MODEL) // SSM_HEADDIM
    expand = [jnp.asarray(np.tile(np.arange(LANES)[:, None] == head_of_col[None, :] + off, (3, 1)), BF16)
              for off in (0, SSM_HEADS)]
    r = np.arange(2 * SSM_CHUNK)[:, None]
    cidx = np.arange(SSM_CHUNK + 2 * HALO)[None, :]
    up = (r < SSM_CHUNK) & (((cidx == r - 1) & (r >= 1)) | ((r == 0) & (cidx == SSM_CHUNK + HALO - 1)))
    dn = (r >= SSM_CHUNK) & (((cidx == r - SSM_CHUNK + 1) & (r < 2 * SSM_CHUNK - 1))
                             | ((r == 2 * SSM_CHUNK - 1) & (cidx == SSM_CHUNK + HALO)))
    shift = jnp.asarray(up | dn, BF16)
    lp = {
        "w_in": w_perm,
        "norm_mix_g": row(norm_mix_g[l]), "norm_ffn_g": row(norm_ffn_g[l]),
        "q_norm_g": row(q_norm_g[l]), "k_norm_g": row(k_norm_g[l]),
        "w_attn_out": w_attn_out[l].astype(BF16), "w_ssm_out": w_ssm_out[l].astype(BF16),
        "w_sconv_out": w_sconv_out[l].astype(BF16), "w_merge": w_merge[l].astype(BF16),
        "cw_x": ssm_conv_w[l][:, :D_MODEL], "cw_bc": ssm_conv_w[l][:, D_MODEL:],
        "cb_x": row(ssm_conv_b[l][:D_MODEL]), "cb_bc": row(ssm_conv_b[l][D_MODEL:]),
        "alog": [lane_row(ssm_a_log[l, d], d * SSM_HEADS) for d in range(2)],
        "dtb": [lane_row(ssm_dt_bias[l, d], d * SSM_HEADS) for d in range(2)],
        "expand": expand,
        "shift": shift,
        "d_row": row(jnp.repeat(ssm_d[l], SSM_HEADDIM)),
        "ssm_norm_g": row(ssm_norm_g[l]),
        "sconv_w": sconv_w[l],
    }
    if l % 2 == 0:
        lp["router"] = None
        lp["ffn"] = tuple(t[l // 2].astype(BF16) for t in (ffn_w1, ffn_w3, ffn_w2))
    else:
        router = jnp.pad(moe_router[l // 2], ((0, 0), (0, LANES - N_EXPERTS)))
        r_hi = router.astype(BF16)
        lp["router"] = jnp.concatenate([r_hi, (router - r_hi.astype(F32)).astype(BF16)], axis=1)
        lp["ffn"] = _cast_bf16((moe_w1[l // 2], moe_w3[l // 2])) + _cast_bf16((moe_w2[l // 2],))
    return lp


def _rope_tables(seq):
    t = np.arange(seq)
    n = HEAD_DIM // 4
    inv = ROPE_THETA ** (-np.arange(n, dtype=np.float64) / n)
    ang_r = (t // GRID_W)[:, None] * inv[None, :]
    ang_c = (t % GRID_W)[:, None] * inv[None, :]
    zero = np.zeros_like(ang_r)
    cos = np.concatenate([np.cos(ang_r)] * 2 + [np.cos(ang_c)] * 2, axis=1)
    sa = np.concatenate([-np.sin(ang_r), zero, -np.sin(ang_c), zero], axis=1)
    sb = np.concatenate([zero, np.sin(ang_r), zero, np.sin(ang_c)], axis=1)
    return tuple(jnp.asarray(a, F32) for a in (cos, sa, sb))


def _trunk(x, mod, lp, nseq, seq, lat, layer=0, depth=1, carry=None):
    rows = nseq * seq
    proj = _inproj(x, mod, lp["norm_mix_g"], lp["w_in"], tm=min(2048, rows), tn=6 * MXU_WIDTH)
    if lat is None:
        attn, k_new, v_new = _attn_ctx(proj, lp["q_norm_g"], lp["k_norm_g"], nseq, seq, layer, depth,
                                       () if carry is None else carry[:2])
        *fwd, h_f = _ssd_pass(proj, lp, 0, nseq, seq, None, None, layer, depth, None if carry is None else carry[2])
        ssm, h_b = _ssd_pass(proj, lp, 1, nseq, seq, None, fwd, layer, depth, None if carry is None else carry[3])
    else:
        cache_k, cache_v, h0_f, h0_b, tables = lat
        kall, vall = _prep_lat(proj, cache_k, cache_v, *tables, lp["k_norm_g"], nseq, seq)
        attn = _attn_lat(proj, kall, vall, *tables, lp["q_norm_g"], nseq, seq, tq=256)
        k_new = v_new = None
        *fwd, h_f = _ssd_pass(proj, lp, 0, nseq, seq, h0_f)
        ssm, h_b = _ssd_pass(proj, lp, 1, nseq, seq, h0_b, fwd)
    outs = _merge(attn, ssm, proj, x, mod, lp, seq, tm=512)
    if lp["router"] is None:
        x = _ffn(outs[1], outs[0], mod, *lp["ffn"], tm=512)
    else:
        x = (outs[1], outs[0], mod, outs[2], outs[3])
    return x, (k_new, v_new, h_f, h_b)


def kernel(x_prompt, x_sample, cache_k, cache_v, state_ssm_fwd, state_ssm_bwd, c, c_ctx, w_mod, b_mod, norm_mix_g, norm_ffn_g, w_in, q_norm_g, k_norm_g, w_attn_out, ssm_conv_w, ssm_conv_b, ssm_dt_bias, ssm_a_log, ssm_d, ssm_norm_g, w_ssm_out, sconv_w, w_sconv_out, w_merge, ffn_w1, ffn_w3, ffn_w2, moe_router, moe_w1, moe_w3, moe_w2):
    batch, seq, _ = x_prompt.shape
    nb, dseq, _ = x_sample.shape
    depth = w_in.shape[0]
    past = cache_k.shape[2]
    kvw = N_KV_HEADS * HEAD_DIM
    npair = SSM_HEADS // 2
    assert nb + 1 <= SUBLANES

    cond = jnp.zeros((SUBLANES, D_MODEL), F32).at[0].set(c_ctx).at[1:1 + nb].set(c)
    tables = _rope_tables(dseq)
    y_ctx = x_prompt.reshape(batch * seq, D_MODEL)
    y_lat = x_sample.reshape(nb * dseq, D_MODEL)
    mods = _modulation(cond, w_mod, b_mod.reshape(depth, 1, -1))
    carry = None
    for l in range(depth):
        lp = _layer_params(l, w_in, q_norm_g, k_norm_g, w_attn_out, ssm_conv_w, ssm_conv_b, ssm_dt_bias,
                           ssm_a_log, ssm_d, ssm_norm_g, w_ssm_out, sconv_w, w_sconv_out, w_merge,
                           norm_mix_g, norm_ffn_g, ffn_w1, ffn_w3, ffn_w2, moe_router, moe_w1, moe_w3, moe_w2)
        m = jnp.pad(mods[l].reshape(SUBLANES, 6, D_MODEL), ((0, 0), (0, SUBLANES - 6), (0, 0)))
        y_ctx, carry = _trunk(y_ctx, m[0:1], lp, batch, seq, None, l, depth, carry)
        lat = (cache_k[:, l].reshape(nb, past, kvw), cache_v[:, l].reshape(nb, past, kvw),
               state_ssm_fwd[:, l].reshape(nb, npair, SSM_CHUNK, SSM_STATE),
               state_ssm_bwd[:, l].reshape(nb, npair, SSM_CHUNK, SSM_STATE), tables)
        y_lat = _trunk(y_lat, m[1:1 + nb], lp, nb, dseq, lat)[0]
        if lp["router"] is not None:
            y_ctx, y_lat = _moe([y_ctx, y_lat], *lp["ffn"], tile=512)
    k_all, v_all, hf_all, hb_all = carry
    state_shape = (batch, depth, SSM_HEADS, SSM_HEADDIM, SSM_STATE)
    return (y_ctx.reshape(batch, seq, D_MODEL), y_lat.reshape(nb, dseq, D_MODEL),
            k_all, v_all,
            hf_all.reshape(state_shape), hb_all.reshape(state_shape))
```

```python
import functools

import jax
import jax.numpy as jnp
import numpy as np
from jax import lax
from jax.experimental import pallas as pl
from jax.experimental.pallas import tpu as pltpu
from jax.experimental.pallas import tpu_sc as plsc

F32 = jnp.float32
BF16 = jnp.bfloat16
HIGHEST = lax.Precision.HIGHEST

D_MODEL = 1024
HEAD_DIM = 128
N_HEADS = 8
N_KV_HEADS = 2
Q_PER_KV = N_HEADS // N_KV_HEADS
ROPE_THETA = 10000.0
GRID_W = 64
SSM_HEADS = 16
SSM_HEADDIM = 64
SSM_STATE = 128
SSM_CHUNK = 128
D_FF = 2816
N_EXPERTS = 8
EPS = 1e-6
Q_SCALE = 1.4426950408889634 * HEAD_DIM ** -0.5

COL_Q, COL_Z, COL_SCB, COL_SCC, COL_SCH, COL_G = 0, 1024, 2048, 3072, 4096, 5120
COL_XS, COL_BC, COL_K, COL_V, COL_DT = 8192, 9216, 9728, 9984, 10240
MXU_WIDTH = 256
N_PROJ = 10752
LANES = 128
SUBLANES = 8
HALO = 16
MIB = 1024 * 1024


def _params(sem, vmem_mib):
    return pltpu.CompilerParams(dimension_semantics=sem, vmem_limit_bytes=vmem_mib * MIB)


def _sigmoid(t):
    return 1.0 / (1.0 + jnp.exp(-t))


def _silu(t):
    return t * _sigmoid(t)


def _rms(t):
    return t * lax.rsqrt(jnp.mean(t * t, axis=-1, keepdims=True) + EPS)


def _bdot(a, b):
    return jnp.dot(a.astype(BF16), b.astype(BF16), preferred_element_type=F32)


def _hdot(a, b):
    return jnp.dot(a, b, precision=HIGHEST, preferred_element_type=F32)


def _split3(t):
    a = t.astype(BF16)
    r = t - a.astype(F32)
    b = r.astype(BF16)
    return a, b, (r - b.astype(F32)).astype(BF16)


def _mod_kernel(c_ref, w_ref, b_ref, o_ref):
    o_ref[...] = _hdot(_silu(c_ref[...]), w_ref[...]) + b_ref[...]


def _modulation(cond8, w, b):
    depth, _, width = w.shape
    return pl.pallas_call(
        _mod_kernel,
        out_shape=jax.ShapeDtypeStruct((depth, SUBLANES, width), F32),
        grid=(depth, width // D_MODEL),
        in_specs=[pl.BlockSpec((SUBLANES, D_MODEL), lambda l, j: (0, 0)),
                  pl.BlockSpec((None, D_MODEL, D_MODEL), lambda l, j: (l, 0, j)),
                  pl.BlockSpec((None, 1, D_MODEL), lambda l, j: (l, 0, j))],
        out_specs=pl.BlockSpec((None, SUBLANES, D_MODEL), lambda l, j: (l, 0, j)),
        compiler_params=_params(("parallel", "parallel"), 32),
        name="modulation",
    )(cond8, w, b)


def _inproj_kernel(x_ref, mod_ref, g_ref, w_ref, o_ref, h_scr):
    @pl.when(pl.program_id(1) == 0)
    def _():
        h = _rms(x_ref[...]) * g_ref[...]
        h = h * (1.0 + mod_ref[0, 1:2, :]) + mod_ref[0, 0:1, :]
        h_scr[...] = h.astype(BF16)

    o_ref[...] = _qk(h_scr[...], w_ref[...]).astype(o_ref.dtype)


def _inproj(x, mod, g, w, tm, tn):
    rows = x.shape[0]
    per_mod = rows // mod.shape[0]
    tm = min(tm, per_mod)
    return pl.pallas_call(
        _inproj_kernel,
        out_shape=jax.ShapeDtypeStruct((rows, N_PROJ), BF16),
        grid=(rows // tm, N_PROJ // tn),
        in_specs=[pl.BlockSpec((tm, D_MODEL), lambda i, j: (i, 0)),
                  pl.BlockSpec((1, SUBLANES, D_MODEL), lambda i, j: (i * tm // per_mod, 0, 0)),
                  pl.BlockSpec((1, D_MODEL), lambda i, j: (0, 0)),
                  pl.BlockSpec((tn, D_MODEL), lambda i, j: (j, 0))],
        out_specs=pl.BlockSpec((tm, tn), lambda i, j: (i, j)),
        scratch_shapes=[pltpu.VMEM((tm, D_MODEL), BF16)],
        compiler_params=_params(("parallel", "arbitrary"), 48),
        name="inproj",
    )(x, mod, g, w)


def _softmax_pv(s, vb):
    m = jnp.max(s, axis=-1, keepdims=True)
    p = jnp.exp2(s - m)
    l = jnp.sum(p, axis=-1, keepdims=True)
    return jnp.dot(p.astype(BF16), vb, preferred_element_type=F32) / l


def _qk(qb, kb):
    return lax.dot_general(qb, kb, (((1,), (1,)), ((), ())), preferred_element_type=F32)


def _attn_ctx_kernel(q_ref, k_ref, v_ref, qg_ref, kg_ref, *rest):
    a_ref, ko_ref, vo_ref = rest[-3:]
    group, seq = ko_ref.shape[:2]
    for b in range(group):
        rows = slice(b * seq, (b + 1) * seq)
        v = v_ref[rows, :].astype(F32)
        for hk in range(N_KV_HEADS):
            ks = slice(hk * HEAD_DIM, (hk + 1) * HEAD_DIM)
            kn = _rms(k_ref[rows, ks].astype(F32)) * kg_ref[...]
            ko_ref[b, :, hk, :] = kn
            vo_ref[b, :, hk, :] = v[:, ks]
            knb = kn.astype(BF16)
            vb = v[:, ks].astype(BF16)
            for g in range(Q_PER_KV):
                hs = slice((hk * Q_PER_KV + g) * HEAD_DIM, (hk * Q_PER_KV + g + 1) * HEAD_DIM)
                qn = _rms(q_ref[rows, hs].astype(F32)) * (qg_ref[...] * Q_SCALE)
                s = _qk(qn.astype(BF16), knb)
                a_ref[rows, hs] = _softmax_pv(s, vb).astype(BF16)


def _attn_ctx(proj, qg, kg, nseq, seq, layer, depth, earlier):
    kvw = N_KV_HEADS * HEAD_DIM
    g = 2 if nseq % 2 == 0 else 1
    cache = jax.ShapeDtypeStruct((nseq, depth, seq, N_KV_HEADS, HEAD_DIM), F32)
    cache_spec = pl.BlockSpec((g, None, seq, N_KV_HEADS, HEAD_DIM), lambda b: (b, layer, 0, 0, 0))
    n_in = 5
    return pl.pallas_call(
        _attn_ctx_kernel,
        out_shape=(jax.ShapeDtypeStruct((nseq * seq, D_MODEL), BF16), cache, cache),
        grid=(nseq // g,),
        in_specs=[pl.BlockSpec((g * seq, D_MODEL), lambda b: (b, COL_Q // D_MODEL)),
                  pl.BlockSpec((g * seq, kvw), lambda b: (b, COL_K // kvw)),
                  pl.BlockSpec((g * seq, kvw), lambda b: (b, COL_V // kvw)),
                  pl.BlockSpec((1, HEAD_DIM), lambda b: (0, 0)),
                  pl.BlockSpec((1, HEAD_DIM), lambda b: (0, 0))] + [pl.BlockSpec(memory_space=pl.ANY)] * len(earlier),
        out_specs=(pl.BlockSpec((g * seq, D_MODEL), lambda b: (b, 0)), cache_spec, cache_spec),
        input_output_aliases={n_in + i: 1 + i for i in range(len(earlier))},
        compiler_params=_params(("parallel",), 32),
        name="attn_ctx",
    )(proj, proj, proj, qg, kg, *earlier)


def _rope(t, cos, sa, sb):
    return t * cos + pltpu.roll(t, 96, 1) * sa + pltpu.roll(t, 32, 1) * sb


def _prep_lat_kernel(k_ref, v_ref, ck_ref, cv_ref, cos_ref, sa_ref, sb_ref, kg_ref, ko_ref, vo_ref):
    t = pl.program_id(1)

    @pl.when(t == 0)
    def _():
        ko_ref[...] = ck_ref[...].astype(BF16)
        vo_ref[...] = cv_ref[...].astype(BF16)

    @pl.when(t > 0)
    def _():
        cos, sa, sb = cos_ref[...], sa_ref[...], sb_ref[...]
        vo_ref[...] = v_ref[...]
        for h in range(N_KV_HEADS):
            hs = slice(h * HEAD_DIM, (h + 1) * HEAD_DIM)
            ko_ref[:, hs] = _rope(_rms(k_ref[:, hs].astype(F32)) * kg_ref[...], cos, sa, sb).astype(BF16)


def _prep_lat(proj, cache_k, cache_v, cos, sa, sb, kg, nb, seq):
    past = cache_k.shape[1]
    tr = past
    kvw = N_KV_HEADS * HEAD_DIM
    nt = seq // tr
    prev = lambda t: jnp.maximum(t - 1, 0)
    return pl.pallas_call(
        _prep_lat_kernel,
        out_shape=(jax.ShapeDtypeStruct((nb, past + seq, kvw), BF16),
                   jax.ShapeDtypeStruct((nb, past + seq, kvw), BF16)),
        grid=(nb, nt + 1),
        in_specs=[pl.BlockSpec((tr, kvw), lambda b, t: (b * nt + prev(t), COL_K // kvw)),
                  pl.BlockSpec((tr, kvw), lambda b, t: (b * nt + prev(t), COL_V // kvw)),
                  pl.BlockSpec((None, past, kvw), lambda b, t: (b, 0, 0)),
                  pl.BlockSpec((None, past, kvw), lambda b, t: (b, 0, 0)),
                  pl.BlockSpec((tr, HEAD_DIM), lambda b, t: (prev(t), 0)),
                  pl.BlockSpec((tr, HEAD_DIM), lambda b, t: (prev(t), 0)),
                  pl.BlockSpec((tr, HEAD_DIM), lambda b, t: (prev(t), 0)),
                  pl.BlockSpec((1, HEAD_DIM), lambda b, t: (0, 0))],
        out_specs=(pl.BlockSpec((None, tr, kvw), lambda b, t: (b, t, 0)),
                   pl.BlockSpec((None, tr, kvw), lambda b, t: (b, t, 0))),
        compiler_params=_params(("parallel", "arbitrary"), 32),
        name="prep_lat",
    )(proj, proj, cache_k, cache_v, cos, sa, sb, kg)


def _attn_lat_kernel(q_ref, cos_ref, sa_ref, sb_ref, qg_ref, k_ref, v_ref, o_ref):
    cos, sa, sb = cos_ref[...], sa_ref[...], sb_ref[...]
    gain = qg_ref[...] * Q_SCALE
    for hk in range(N_KV_HEADS):
        kb = k_ref[:, hk * HEAD_DIM:(hk + 1) * HEAD_DIM]
        vb = v_ref[:, hk * HEAD_DIM:(hk + 1) * HEAD_DIM]
        for g in range(Q_PER_KV):
            hs = slice((hk * Q_PER_KV + g) * HEAD_DIM, (hk * Q_PER_KV + g + 1) * HEAD_DIM)
            qn = _rope(_rms(q_ref[:, hs].astype(F32)) * gain, cos, sa, sb)
            s = _qk(qn.astype(BF16), kb)
            o_ref[:, hs] = _softmax_pv(s, vb).astype(BF16)


def _attn_lat(proj, kall, vall, cos, sa, sb, qg, nb, seq, tq):
    nk, kvw = kall.shape[1:]
    nq = seq // tq
    table = pl.BlockSpec((tq, HEAD_DIM), lambda b, i: (i, 0))
    return pl.pallas_call(
        _attn_lat_kernel,
        out_shape=jax.ShapeDtypeStruct((nb * seq, D_MODEL), BF16),
        grid=(nb, nq),
        in_specs=[pl.BlockSpec((tq, D_MODEL), lambda b, i: (b * nq + i, COL_Q // D_MODEL)),
                  table, table, table, pl.BlockSpec((1, HEAD_DIM), lambda b, i: (0, 0)),
                  pl.BlockSpec((None, nk, kvw), lambda b, i: (b, 0, 0)),
                  pl.BlockSpec((None, nk, kvw), lambda b, i: (b, 0, 0))],
        out_specs=pl.BlockSpec((tq, D_MODEL), lambda b, i: (b * nq + i, 0)),
        compiler_params=_params(("parallel", "arbitrary"), 48),
        name="attn_lat",
    )(proj, cos, sa, sb, qg, kall, vall)


SSD_GROUP = 8


def _ssd_kernel(*refs, reverse, has_h0, nc):
    it = iter(refs)
    if reverse:
        xsc_ref, bcc_ref, dt_ref, alog_ref, dtb_ref, e_ref = (next(it) for _ in range(6))
    else:
        xs_ref, bc_ref, dt_ref, xsp_ref, xsn_ref, bcp_ref, bcn_ref = (next(it) for _ in range(7))
        cwx_ref, cbx_ref, cwb_ref, cbb_ref, alog_ref, dtb_ref, e_ref, shift_ref = (next(it) for _ in range(8))
    h0_ref = next(it) if has_h0 else None
    if reverse:
        z_ref, yf_ref, d_ref, g_ref = (next(it) for _ in range(4))
        y_ref, hout_ref, ht_scr = refs[-3:]
    else:
        y_ref, xsc_ref, bcc_ref, hout_ref, ht_scr = refs[-5:]

    q = SSM_CHUNK
    npair = SSM_HEADS // 2
    group = y_ref.shape[0]
    c = pl.program_id(1)
    cc = nc - 1 - c if reverse else c

    @pl.when(c == 0)
    def _():
        for s in range(group):
            for p in range(npair):
                if has_h0:
                    ht_scr[s, p] = h0_ref[s, p].T
                else:
                    ht_scr[s, p] = jnp.zeros((q, q), F32)

    off = SSM_HEADS if reverse else 0
    ii = lax.broadcasted_iota(jnp.int32, (q, q), 0)
    jj = lax.broadcasted_iota(jnp.int32, (q, q), 1)
    tri = (jj >= ii) if reverse else (ii >= jj)
    tri_b = jnp.where(tri, 1.0, 0.0).astype(BF16)
    tri_tb3 = jnp.concatenate([jnp.where((ii >= jj) if reverse else (jj >= ii), 1.0, 0.0).astype(BF16)] * 3, axis=0)
    lo = lax.broadcasted_iota(jnp.int32, (q, LANES), 1) < SSM_HEADDIM
    neg_a = -jnp.exp(alog_ref[...])
    if not reverse:
        keep_prev = jnp.where(cc == 0, 0.0, 1.0).astype(BF16)
        keep_next = jnp.where(cc == nc - 1, 0.0, 1.0).astype(BF16)

    def conv_silu(s, x_ref_, p_ref_, n_ref_, w_ref_, b_ref_):
        x = x_ref_[s]
        ext = jnp.concatenate([x, p_ref_[s] * keep_prev, n_ref_[s] * keep_next], axis=0)
        nb = jnp.dot(shift_ref[...], ext, preferred_element_type=F32)
        w = w_ref_[...]
        return _silu(w[0:1, :] * nb[:q, :] + w[1:2, :] * x.astype(F32) + w[2:3, :] * nb[q:, :] + b_ref_[...])

    for s in range(group):
        if reverse:
            xs = xsc_ref[s].astype(F32)
            bc = bcc_ref[s].astype(F32)
        else:
            xs = conv_silu(s, xs_ref, xsp_ref, xsn_ref, cwx_ref, cbx_ref)
            bc = conv_silu(s, bc_ref, bcp_ref, bcn_ref, cwb_ref, cbb_ref)
            xsc_ref[s] = xs.astype(BF16)
            bcc_ref[s] = bc.astype(BF16)

        dt = jax.nn.softplus(dt_ref[s].astype(F32) + dtb_ref[...])
        da = dt * neg_a
        acs3 = jnp.dot(tri_b, jnp.concatenate(_split3(da), axis=1), preferred_element_type=F32)
        acs = acs3[:, :LANES] + acs3[:, LANES:2 * LANES] + acs3[:, 2 * LANES:]
        acs_t = jnp.dot(jnp.concatenate(_split3(da.T), axis=1), tri_tb3, preferred_element_type=F32)
        spread = jnp.dot(jnp.concatenate(_split3(jnp.concatenate([dt, acs], axis=0)), axis=1), e_ref[...],
                         preferred_element_type=F32)
        acs_x = spread[q:, :]
        tot_x = acs_x[0:1, :] if reverse else acs_x[q - 1:q, :]
        xdt = xs * spread[:q, :]
        xw = xdt * jnp.exp(tot_x - acs_x)
        eacs_x = jnp.exp(acs_x)
        etot_x = jnp.exp(tot_x)

        parts = []
        for g in range(2):
            bg = bc[:, g * q:(g + 1) * q]
            cg = bc[:, (2 + g) * q:(3 + g) * q].astype(BF16)
            cb = _qk(cg, bg.astype(BF16))
            bgt = bg.T.astype(BF16)
            for pp in range(npair // 2):
                p = g * (npair // 2) + pp
                ps = slice(p * LANES, (p + 1) * LANES)
                ms = []
                for hh in range(2):
                    ln = off + 2 * p + hh
                    dec = jnp.where(tri, jnp.exp(acs[:, ln:ln + 1] - acs_t[ln:ln + 1, :]), 0.0)
                    ms.append((cb * dec).astype(BF16))
                m = jnp.concatenate(ms, axis=1)
                xp = xdt[:, ps]
                bd = jnp.concatenate([jnp.where(lo, xp, 0.0), jnp.where(lo, 0.0, xp)], axis=0).astype(BF16)
                ht = ht_scr[s, p]
                yp = jnp.dot(m, bd, preferred_element_type=F32)
                yp = yp + jnp.dot(cg, ht.astype(BF16), preferred_element_type=F32) * eacs_x[:, ps]
                ht_scr[s, p] = etot_x[:, ps] * ht + jnp.dot(bgt, xw[:, ps].astype(BF16),
                                                            preferred_element_type=F32)
                parts.append(yp)
        y = jnp.concatenate(parts, axis=1)

        if reverse:
            y = y + yf_ref[s] + d_ref[...] * xs
            y = y * _silu(z_ref[s].astype(F32))
            y_ref[s] = (_rms(y) * g_ref[...]).astype(y_ref.dtype)
        else:
            y_ref[s] = y

    @pl.when(c == nc - 1)
    def _():
        for s in range(group):
            for p in range(npair):
                hout_ref[s, p] = ht_scr[s, p].T


def _ssd_pass(proj, lp, d, nseq, seq, h0, fwd=None, layer=0, depth=1, earlier=None):
    reverse = fwd is not None
    has_h0 = h0 is not None
    q = SSM_CHUNK
    nc = seq // q
    g = min(SSD_GROUP, nseq)
    assert nseq % g == 0
    npair = SSM_HEADS // 2
    hb = q // HALO
    bcw = 4 * SSM_STATE
    chunk = (lambda c: nc - 1 - c) if reverse else (lambda c: c)
    prev8 = lambda c: jnp.maximum(chunk(c) * hb - 1, 0)
    next8 = lambda c: jnp.minimum((chunk(c) + 1) * hb, seq // HALO - 1)
    const = lambda s, c: (0, 0)
    tile = lambda width, col=0: pl.BlockSpec((g, q, width), lambda s, c: (s, chunk(c), col // width))
    halo = lambda width, col, where: pl.BlockSpec((g, HALO, width), lambda s, c: (s, where(c), col // width))
    seqs = lambda t: t.reshape(nseq, seq, t.shape[-1])
    scan_consts = [pl.BlockSpec((1, LANES), const), pl.BlockSpec((1, LANES), const),
                   pl.BlockSpec((3 * LANES, D_MODEL), const)]
    scan_args = [lp["alog"][d], lp["dtb"][d], lp["expand"][d]]
    proj3 = seqs(proj)
    if reverse:
        yf, xsc, bcc = fwd
        in_specs = [tile(D_MODEL), tile(bcw), tile(LANES, COL_DT)] + scan_consts
        args = [xsc, bcc, proj3] + scan_args
    else:
        in_specs = [tile(D_MODEL, COL_XS), tile(bcw, COL_BC), tile(LANES, COL_DT),
                    halo(D_MODEL, COL_XS, prev8), halo(D_MODEL, COL_XS, next8),
                    halo(bcw, COL_BC, prev8), halo(bcw, COL_BC, next8),
                    pl.BlockSpec((3, D_MODEL), const), pl.BlockSpec((1, D_MODEL), const),
                    pl.BlockSpec((3, bcw), const), pl.BlockSpec((1, bcw), const)] + scan_consts + [
                    pl.BlockSpec((2 * q, q + 2 * HALO), const)]
        args = [proj3] * 7 + [lp["cw_x"], lp["cb_x"], lp["cw_bc"], lp["cb_bc"]] + scan_args + [lp["shift"]]
    if has_h0:
        in_specs.append(pl.BlockSpec((g, npair, q, q), lambda s, c: (s, 0, 0, 0)))
        args.append(h0)
    if reverse:
        in_specs += [tile(D_MODEL, COL_Z), tile(D_MODEL), pl.BlockSpec((1, D_MODEL), const),
                     pl.BlockSpec((1, D_MODEL), const)]
        args += [proj3, yf, lp["d_row"], lp["ssm_norm_g"]]
    state = jax.ShapeDtypeStruct((nseq, depth, npair, q, q), F32)
    state_spec = pl.BlockSpec((g, None, npair, q, q), lambda s, c: (s, layer, 0, 0, 0))
    if reverse:
        out_shape = (jax.ShapeDtypeStruct((nseq, seq, D_MODEL), BF16), state)
        out_specs = (tile(D_MODEL), state_spec)
    else:
        out_shape = (jax.ShapeDtypeStruct((nseq, seq, D_MODEL), F32),
                     jax.ShapeDtypeStruct((nseq, seq, D_MODEL), BF16),
                     jax.ShapeDtypeStruct((nseq, seq, bcw), BF16), state)
        out_specs = (tile(D_MODEL), tile(D_MODEL), tile(bcw), state_spec)
    aliases = {}
    if earlier is not None:
        aliases = {len(args): len(out_shape) - 1}
        in_specs.append(pl.BlockSpec(memory_space=pl.ANY))
        args.append(earlier)
    outs = pl.pallas_call(
        functools.partial(_ssd_kernel, reverse=reverse, has_h0=has_h0, nc=nc),
        out_shape=out_shape,
        grid=(nseq // g, nc),
        in_specs=in_specs,
        out_specs=out_specs,
        input_output_aliases=aliases,
        scratch_shapes=[pltpu.VMEM((g, npair, q, q), F32)],
        compiler_params=_params(("parallel", "arbitrary"), 48),
        name="ssd_bwd" if reverse else "ssd_fwd",
    )(*args)
    if reverse:
        return outs[0].reshape(nseq * seq, D_MODEL), outs[1]
    return outs


ROUTE_I1, ROUTE_I2, ROUTE_P1, ROUTE_P2, ROUTE_R1, ROUTE_R2 = 0, 1, 2, 3, 4, 5


def _pack_bf16_pairs(v):
    n = v.shape[1] // 2
    bits = lax.bitcast_convert_type(v.astype(BF16).astype(F32), jnp.uint32)
    return lax.bitcast_convert_type((bits[:, :n] >> 16) | bits[:, n:], jnp.int32)


def _unpack_bf16_pairs(p):
    u = lax.bitcast_convert_type(p, jnp.uint32)
    lo = lax.bitcast_convert_type(u << 16, F32)
    hi = lax.bitcast_convert_type(u & jnp.uint32(0xFFFF0000), F32)
    return jnp.concatenate([lo, hi], axis=1)


def _top2_route(logits):
    tm = logits.shape[0]
    lane = lax.broadcasted_iota(jnp.int32, logits.shape, 1).astype(F32)
    lg = jnp.where(lane < N_EXPERTS, logits, -jnp.inf)
    m1 = jnp.max(lg, axis=-1, keepdims=True)
    i1 = jnp.min(jnp.where(lg == m1, lane, float(LANES)), axis=-1, keepdims=True)
    lg2 = jnp.where(lane == i1, -jnp.inf, lg)
    m2 = jnp.max(lg2, axis=-1, keepdims=True)
    i2 = jnp.min(jnp.where(lg2 == m2, lane, float(LANES)), axis=-1, keepdims=True)
    e = jnp.exp(m2 - m1)
    p1 = 1.0 / (1.0 + e)
    chosen = jnp.where(lane == i1, 1.0, jnp.where(lane == i2, 1.0, 0.0))
    earlier = (lax.broadcasted_iota(jnp.int32, (tm, tm), 0) > lax.broadcasted_iota(jnp.int32, (tm, tm), 1))
    ranks = _bdot(jnp.where(earlier, 1.0, 0.0), chosen)
    r1 = jnp.sum(jnp.where(lane == i1, ranks, 0.0), axis=-1, keepdims=True)
    r2 = jnp.sum(jnp.where(lane == i2, ranks, 0.0), axis=-1, keepdims=True)
    rec = jnp.where(lane == ROUTE_I1, i1, jnp.where(lane == ROUTE_I2, i2, 0.0))
    rec = jnp.where(lane == ROUTE_P1, p1, jnp.where(lane == ROUTE_P2, e * p1, rec))
    rec = jnp.where(lane == ROUTE_R1, r1, jnp.where(lane == ROUTE_R2, r2, rec))
    return rec, jnp.sum(chosen, axis=0, keepdims=True)


def _merge_kernel(*refs, seq, tm, moe):
    it = iter(refs)
    attn_ref, ssm_ref, scb_ref, scc_ref, sch_ref = (next(it) for _ in range(5))
    sccp_ref, sccn_ref, schp_ref, schn_ref = (next(it) for _ in range(4))
    g0_ref, g1_ref, g2_ref, x_ref, mod_ref = (next(it) for _ in range(5))
    wa_ref, ws_ref, wc_ref, wm_ref, scw_ref, ng_ref = (next(it) for _ in range(6))
    router_ref = next(it) if moe else None
    xo_ref, h2_ref = next(it), next(it)
    route_ref, cnt_ref = (next(it), next(it)) if moe else (None, None)

    i = pl.program_id(0)
    pos = (lax.broadcasted_iota(jnp.int32, (tm, 1), 0) + i * tm) & (seq - 1)
    u = scc_ref[...].astype(F32) * sch_ref[...].astype(F32)
    up = sccp_ref[HALO - 1:HALO, :].astype(F32) * schp_ref[HALO - 1:HALO, :].astype(F32)
    un = sccn_ref[0:1, :].astype(F32) * schn_ref[0:1, :].astype(F32)
    r = lax.broadcasted_iota(jnp.int32, (tm, 1), 0)
    um1 = jnp.where(pos == 0, 0.0, jnp.where(r == 0, up, pltpu.roll(u, 1, 0)))
    up1 = jnp.where(pos == seq - 1, 0.0, jnp.where(r == tm - 1, un, pltpu.roll(u, tm - 1, 0)))
    w = scw_ref[...]
    sc = scb_ref[...].astype(F32) * (w[0:1, :] * um1 + w[1:2, :] * u + w[2:3, :] * up1)

    gate = lambda g_ref: _sigmoid(g_ref[...].astype(F32))
    merged = gate(g0_ref) * jnp.dot(attn_ref[...], wa_ref[...], preferred_element_type=F32)
    merged = merged + gate(g1_ref) * jnp.dot(ssm_ref[...], ws_ref[...], preferred_element_type=F32)
    merged = merged + gate(g2_ref) * jnp.dot(sc.astype(BF16), wc_ref[...], preferred_element_type=F32)
    mod = mod_ref[0]
    xn = x_ref[...] + mod[2:3, :] * jnp.dot(merged.astype(BF16), wm_ref[...], preferred_element_type=F32)
    xo_ref[...] = xn
    h2 = _rms(xn) * ng_ref[...] * (1.0 + mod[4:5, :]) + mod[3:4, :]
    if moe:
        h2_ref[...] = _pack_bf16_pairs(h2)
        h_hi = h2.astype(BF16)
        h_lo = (h2 - h_hi.astype(F32)).astype(BF16)
        both = jnp.dot(h_hi, router_ref[...], preferred_element_type=F32)
        logits = both[:, :LANES] + both[:, LANES:] + jnp.dot(h_lo, router_ref[:, :LANES], preferred_element_type=F32)
        rec, counts = _top2_route(logits)
        route_ref[...] = rec
        cnt_ref[...] = jnp.broadcast_to(counts, cnt_ref.shape)
    else:
        h2_ref[...] = h2.astype(BF16)


def _merge(attn, ssm, proj, x, mod, lp, seq, tm):
    rows = x.shape[0]
    per_mod = rows // mod.shape[0]
    tm = min(tm, per_mod)
    moe = lp["router"] is not None
    hb = tm // HALO
    prev8 = lambda i: jnp.maximum(i * hb - 1, 0)
    next8 = lambda i: jnp.minimum((i + 1) * hb, rows // HALO - 1)
    tile = lambda col: pl.BlockSpec((tm, D_MODEL), lambda i: (i, col // D_MODEL))
    const = lambda shape: pl.BlockSpec(shape, lambda i: (0, 0))
    in_specs = [tile(0), tile(0), tile(COL_SCB), tile(COL_SCC), tile(COL_SCH),
                pl.BlockSpec((HALO, D_MODEL), lambda i: (prev8(i), COL_SCC // D_MODEL)),
                pl.BlockSpec((HALO, D_MODEL), lambda i: (next8(i), COL_SCC // D_MODEL)),
                pl.BlockSpec((HALO, D_MODEL), lambda i: (prev8(i), COL_SCH // D_MODEL)),
                pl.BlockSpec((HALO, D_MODEL), lambda i: (next8(i), COL_SCH // D_MODEL)),
                tile(COL_G), tile(COL_G + D_MODEL), tile(COL_G + 2 * D_MODEL), tile(0),
                pl.BlockSpec((1, SUBLANES, D_MODEL), lambda i: (i * tm // per_mod, 0, 0)),
                _resident((D_MODEL, D_MODEL)), _resident((D_MODEL, D_MODEL)), _resident((D_MODEL, D_MODEL)),
                _resident((D_MODEL, D_MODEL)), const((3, D_MODEL)), const((1, D_MODEL))]
    args = [attn, ssm, proj, proj, proj, proj, proj, proj, proj, proj, proj, proj, x, mod,
            lp["w_attn_out"], lp["w_ssm_out"], lp["w_sconv_out"], lp["w_merge"], lp["sconv_w"], lp["norm_ffn_g"]]
    out_shape = [jax.ShapeDtypeStruct((rows, D_MODEL), F32), jax.ShapeDtypeStruct((rows, D_MODEL), BF16)]
    out_specs = [tile(0), tile(0)]
    if moe:
        in_specs.append(_resident((D_MODEL, 2 * LANES)))
        args.append(lp["router"])
        out_shape[1] = jax.ShapeDtypeStruct((rows, D_MODEL // 2), jnp.int32)
        out_specs[1] = pl.BlockSpec((tm, D_MODEL // 2), lambda i: (i, 0))
        out_shape += [jax.ShapeDtypeStruct((rows, LANES), F32),
                      jax.ShapeDtypeStruct((rows // tm * SUBLANES, LANES), F32)]
        out_specs += [pl.BlockSpec((tm, LANES), lambda i: (i, 0)), pl.BlockSpec((SUBLANES, LANES), lambda i: (i, 0))]
    return pl.pallas_call(
        functools.partial(_merge_kernel, seq=seq, tm=tm, moe=moe),
        out_shape=tuple(out_shape),
        grid=(rows // tm,),
        in_specs=in_specs,
        out_specs=tuple(out_specs),
        compiler_params=_params(("parallel",), 56),
        name="merge",
    )(*args)


def _swiglu_acc(h, w1_ref, w3_ref, w2_ref, fs):
    a = _silu(jnp.dot(h, w1_ref[:, fs], preferred_element_type=F32)) * jnp.dot(h, w3_ref[:, fs],
                                                                                preferred_element_type=F32)
    return jnp.dot(a.astype(BF16), w2_ref[fs, :], preferred_element_type=F32)


def _swiglu(h, w1_ref, w3_ref, w2_ref):
    cut = (D_FF // 2 + MXU_WIDTH - 1) // MXU_WIDTH * MXU_WIDTH
    return (_swiglu_acc(h, w1_ref, w3_ref, w2_ref, slice(0, cut))
            + _swiglu_acc(h, w1_ref, w3_ref, w2_ref, slice(cut, D_FF)))


def _ffn_kernel(h_ref, x_ref, mod_ref, w1_ref, w3_ref, w2_ref, o_ref):
    o_ref[...] = x_ref[...] + mod_ref[0, 5:6, :] * _swiglu(h_ref[...], w1_ref, w3_ref, w2_ref)


def _resident(shape):
    return pl.BlockSpec(shape, lambda *_: (0,) * len(shape), pipeline_mode=pl.Buffered(1))


def _ffn(h2, x, mod, w1, w3, w2, tm):
    rows = x.shape[0]
    per_mod = rows // mod.shape[0]
    tm = min(tm, per_mod)
    return pl.pallas_call(
        _ffn_kernel,
        out_shape=jax.ShapeDtypeStruct((rows, D_MODEL), F32),
        grid=(rows // tm,),
        in_specs=[pl.BlockSpec((tm, D_MODEL), lambda i: (i, 0)),
                  pl.BlockSpec((tm, D_MODEL), lambda i: (i, 0)),
                  pl.BlockSpec((1, SUBLANES, D_MODEL), lambda i: (i * tm // per_mod, 0, 0)),
                  _resident((D_MODEL, D_FF)), _resident((D_MODEL, D_FF)), _resident((D_FF, D_MODEL))],
        out_specs=pl.BlockSpec((tm, D_MODEL), lambda i: (i, 0)),
        compiler_params=_params(("parallel",), 56),
        name="ffn",
    )(h2, x, mod, w1, w3, w2)


def _cast_kernel(*refs):
    n = len(refs) // 2
    for src, dst in zip(refs[:n], refs[n:]):
        dst[...] = src[...].astype(dst.dtype)


def _cast_bf16(ws):
    e, r, c = ws[0].shape
    rb = max(d for d in range(16, r + 1, 16) if r % d == 0 and d * c * 4 <= 3 * MIB)
    spec = pl.BlockSpec((None, rb, c), lambda i, j: (i, j, 0))
    return pl.pallas_call(
        _cast_kernel,
        out_shape=tuple(jax.ShapeDtypeStruct(w.shape, BF16) for w in ws),
        grid=(e, r // rb),
        in_specs=[spec] * len(ws),
        out_specs=tuple([spec] * len(ws)),
        compiler_params=_params(("parallel", "parallel"), 48),
        name="cast_bf16",
    )(*ws)


SC_CORES = 2
SC_SUBCORES = 16
SC_CHUNK = 128


def _sc_gather(table, idx):
    nw = SC_CORES * SC_SUBCORES
    b, d = idx.shape[0], table.shape[1]
    per_w = b // nw
    ch = SC_CHUNK // 2
    n = per_w // ch
    assert per_w * nw == b and n * ch == per_w
    mesh = plsc.VectorSubcoreMesh(core_axis_name="c", subcore_axis_name="s")

    def body(table_hbm, idx_hbm, out_hbm, idx_a, idx_b, rows_a, rows_b, sem_a, sem_b):
        wid = lax.axis_index("s") * SC_CORES + lax.axis_index("c")
        bufs = ((idx_a, rows_a, sem_a), (idx_b, rows_b, sem_b))
        offset = lambda j: pl.multiple_of(wid * per_w + j * ch, ch)

        def fetch(j):
            idx_v, rows_v, sem = bufs[j % 2]
            pltpu.sync_copy(idx_hbm.at[pl.ds(offset(j), ch)], idx_v)
            return pltpu.async_copy(table_hbm.at[idx_v], rows_v, sem)

        pending = fetch(0)
        for j in range(n):
            following = fetch(j + 1) if j + 1 < n else None
            pending.wait()
            pltpu.sync_copy(bufs[j % 2][1], out_hbm.at[pl.ds(offset(j), ch)])
            pending = following

    return pl.kernel(
        body,
        out_type=jax.ShapeDtypeStruct((b, d), table.dtype),
        mesh=mesh,
        scratch_types=[pltpu.VMEM((ch,), jnp.int32)] * 2 + [pltpu.VMEM((ch, d), table.dtype)] * 2
        + [pltpu.SemaphoreType.DMA] * 2,
        name="sc_gather",
    )(table, idx)


def _sc_dispatch(table, slot, n_slots):
    nw = SC_CORES * SC_SUBCORES
    t, d = table.shape
    per_w = t // nw
    assert per_w * nw == t and per_w % SC_CHUNK == 0 and slot.shape[0] == 2 * t
    mesh = plsc.VectorSubcoreMesh(core_axis_name="c", subcore_axis_name="s")

    def body(table_hbm, slot_hbm, out_hbm, idx_v, rows_v, sem):
        wid = lax.axis_index("s") * SC_CORES + lax.axis_index("c")

        @pl.loop(0, per_w // SC_CHUNK)
        def _(j):
            off = pl.multiple_of(wid * per_w + j * SC_CHUNK, SC_CHUNK)
            pltpu.sync_copy(table_hbm.at[pl.ds(off, SC_CHUNK)], rows_v)
            for k in range(2):
                pltpu.sync_copy(slot_hbm.at[pl.ds(pl.multiple_of(k * t + off, SC_CHUNK), SC_CHUNK)], idx_v)
                pltpu.async_copy(rows_v, out_hbm.at[idx_v], sem).wait()

    return pl.kernel(
        body,
        out_type=jax.ShapeDtypeStruct((n_slots, d), table.dtype),
        mesh=mesh,
        scratch_types=[pltpu.VMEM((SC_CHUNK,), jnp.int32), pltpu.VMEM((SC_CHUNK, d), table.dtype),
                       pltpu.SemaphoreType.DMA],
        name="sc_dispatch",
    )(table, slot)


def _route_tables(route, counts, tile):
    t = route.shape[0]
    n_mt = counts.shape[0] // SUBLANES
    cnt = counts.reshape(n_mt, SUBLANES, LANES)[:, 0, :N_EXPERTS].astype(jnp.int32)
    incl = jnp.cumsum(cnt, axis=0)
    padded = (incl[-1] + tile - 1) // tile * tile
    gend = jnp.cumsum(padded)
    base = jnp.repeat((gend - padded)[None, :] + incl - cnt, t // n_mt, axis=0)
    e = route[:, ROUTE_I1:ROUTE_I2 + 1].astype(jnp.int32)
    r = route[:, ROUTE_R1:ROUTE_R2 + 1].astype(jnp.int32)
    pick = e[:, :, None] == jnp.arange(N_EXPERTS)[None, None, :]
    slot = jnp.sum(jnp.where(pick, base[:, None, :], 0), axis=-1) + r
    n_slots = 2 * t + N_EXPERTS * tile
    n_valid = gend[-1] // tile
    n_tiles = n_slots // tile
    tile_expert = jnp.sum(jnp.arange(n_tiles)[:, None] * tile >= gend[None, :], axis=1)
    last = jnp.take(tile_expert, jnp.maximum(n_valid - 1, 0))
    tile_expert = jnp.where(jnp.arange(n_tiles) < n_valid, tile_expert, last).astype(jnp.int32)
    return slot.T.reshape(-1).astype(jnp.int32), n_slots, tile_expert, n_valid.reshape(1).astype(jnp.int32)


def _moe_ffn_kernel(te_ref, nv_ref, h_ref, w1_ref, w3_ref, w2_ref, o_ref):
    i = pl.program_id(0)

    @pl.when(i < nv_ref[0])
    def _():
        h = _unpack_bf16_pairs(h_ref[...]).astype(BF16)
        o_ref[...] = _pack_bf16_pairs(_swiglu(h, w1_ref, w3_ref, w2_ref))

    @pl.when(i >= nv_ref[0])
    def _():
        o_ref[...] = jnp.zeros(o_ref.shape, jnp.int32)


def _moe_ffn(hs, tile_expert, n_valid, w1, w3, w2, tile):
    n_slots = hs.shape[0]
    grid_spec = pltpu.PrefetchScalarGridSpec(
        num_scalar_prefetch=2,
        grid=(n_slots // tile,),
        in_specs=[pl.BlockSpec((tile, D_MODEL // 2), lambda i, te, nv: (i, 0)),
                  pl.BlockSpec((None, D_MODEL, D_FF), lambda i, te, nv: (te[i], 0, 0)),
                  pl.BlockSpec((None, D_MODEL, D_FF), lambda i, te, nv: (te[i], 0, 0)),
                  pl.BlockSpec((None, D_FF, D_MODEL), lambda i, te, nv: (te[i], 0, 0))],
        out_specs=pl.BlockSpec((tile, D_MODEL // 2), lambda i, te, nv: (i, 0)),
    )
    return pl.pallas_call(
        _moe_ffn_kernel,
        out_shape=jax.ShapeDtypeStruct((n_slots, D_MODEL // 2), jnp.int32),
        grid_spec=grid_spec,
        compiler_params=_params(("arbitrary",), 56),
        name="moe_ffn",
    )(tile_expert, n_valid, hs, w1, w3, w2)


def _combine_kernel(x_ref, y1_ref, y2_ref, route_ref, mod_ref, o_ref):
    p1 = route_ref[:, ROUTE_P1:ROUTE_P1 + 1]
    p2 = route_ref[:, ROUTE_P2:ROUTE_P2 + 1]
    f = p1 * _unpack_bf16_pairs(y1_ref[...]) + p2 * _unpack_bf16_pairs(y2_ref[...])
    o_ref[...] = x_ref[...] + mod_ref[0, 5:6, :] * f


def _combine(x, yg, route, mod, tm):
    rows = x.shape[0]
    per_mod = rows // mod.shape[0]
    tm = min(tm, per_mod)
    return pl.pallas_call(
        _combine_kernel,
        out_shape=jax.ShapeDtypeStruct((rows, D_MODEL), F32),
        grid=(rows // tm,),
        in_specs=[pl.BlockSpec((tm, D_MODEL), lambda i: (i, 0)),
                  pl.BlockSpec((tm, D_MODEL // 2), lambda i: (i, 0)),
                  pl.BlockSpec((tm, D_MODEL // 2), lambda i: (i + rows // tm, 0)),
                  pl.BlockSpec((tm, LANES), lambda i: (i, 0)),
                  pl.BlockSpec((1, SUBLANES, D_MODEL), lambda i: (i * tm // per_mod, 0, 0))],
        out_specs=pl.BlockSpec((tm, D_MODEL), lambda i: (i, 0)),
        compiler_params=_params(("parallel",), 32),
        name="moe_combine",
    )(x, yg, yg, route, mod)


def _moe(h2p, x, mod, route, counts, w1, w3, w2, tile):
    slot, n_slots, tile_expert, n_valid = _route_tables(route, counts, tile)
    hs = _sc_dispatch(h2p, slot, n_slots)
    ys = _moe_ffn(hs, tile_expert, n_valid, w1, w3, w2, tile)
    return _combine(x, _sc_gather(ys, slot), route, mod, tm=512)


W_IN_SEGMENTS = ((0, 1024), (1536, 1024), (4128, 1024), (5152, 1024), (6176, 1024), (7200, 3072),
                 (2560, 1024), (3584, 512), (1024, 256), (1280, 256), (4096, 32))


def _permute_kernel(w_ref, o_ref):
    dst = 0
    for src, n in W_IN_SEGMENTS:
        o_ref[dst:dst + n, :] = w_ref[src:src + n, :].astype(BF16)
        dst += n
    o_ref[dst:, :] = jnp.zeros((o_ref.shape[0] - dst, o_ref.shape[1]), BF16)


def _permute_w_in(w_in, l, cb=128):
    w_t = jnp.swapaxes(w_in, 1, 2)
    _, feats, d = w_t.shape
    return pl.pallas_call(
        _permute_kernel,
        out_shape=jax.ShapeDtypeStruct((N_PROJ, d), BF16),
        grid=(d // cb,),
        in_specs=[pl.BlockSpec((None, feats, cb), lambda i: (l, 0, i))],
        out_specs=pl.BlockSpec((N_PROJ, cb), lambda i: (0, i)),
        compiler_params=_params(("parallel",), 48),
        name="permute_w_in",
    )(w_t)


def _layer_params(l, w_in, q_norm_g, k_norm_g, w_attn_out, ssm_conv_w, ssm_conv_b, ssm_dt_bias, ssm_a_log,
                  ssm_d, ssm_norm_g, w_ssm_out, sconv_w, w_sconv_out, w_merge, norm_mix_g, norm_ffn_g,
                  ffn_w1, ffn_w3, ffn_w2, moe_router, moe_w1, moe_w3, moe_w2):
    w_perm = _permute_w_in(w_in, l)
    row = lambda v: v.reshape(1, -1)
    lane_row = lambda v, off: jnp.zeros((1, LANES), F32).at[0, off:off + SSM_HEADS].set(v)
    head_of_col = np.arange(D_MODEL) // SSM_HEADDIM
    expand = [jnp.asarray(np.tile(np.arange(LANES)[:, None] == head_of_col[None, :] + off, (3, 1)), BF16)
              for off in (0, SSM_HEADS)]
    r = np.arange(2 * SSM_CHUNK)[:, None]
    cidx = np.arange(SSM_CHUNK + 2 * HALO)[None, :]
    up = (r < SSM_CHUNK) & (((cidx == r - 1) & (r >= 1)) | ((r == 0) & (cidx == SSM_CHUNK + HALO - 1)))
    dn = (r >= SSM_CHUNK) & (((cidx == r - SSM_CHUNK + 1) & (r < 2 * SSM_CHUNK - 1))
                             | ((r == 2 * SSM_CHUNK - 1) & (cidx == SSM_CHUNK + HALO)))
    shift = jnp.asarray(up | dn, BF16)
    lp = {
        "w_in": w_perm,
        "norm_mix_g": row(norm_mix_g[l]), "norm_ffn_g": row(norm_ffn_g[l]),
        "q_norm_g": row(q_norm_g[l]), "k_norm_g": row(k_norm_g[l]),
        "w_attn_out": w_attn_out[l].astype(BF16), "w_ssm_out": w_ssm_out[l].astype(BF16),
        "w_sconv_out": w_sconv_out[l].astype(BF16), "w_merge": w_merge[l].astype(BF16),
        "cw_x": ssm_conv_w[l][:, :D_MODEL], "cw_bc": ssm_conv_w[l][:, D_MODEL:],
        "cb_x": row(ssm_conv_b[l][:D_MODEL]), "cb_bc": row(ssm_conv_b[l][D_MODEL:]),
        "alog": [lane_row(ssm_a_log[l, d], d * SSM_HEADS) for d in range(2)],
        "dtb": [lane_row(ssm_dt_bias[l, d], d * SSM_HEADS) for d in range(2)],
        "expand": expand,
        "shift": shift,
        "d_row": row(jnp.repeat(ssm_d[l], SSM_HEADDIM)),
        "ssm_norm_g": row(ssm_norm_g[l]),
        "sconv_w": sconv_w[l],
    }
    if l % 2 == 0:
        lp["router"] = None
        lp["ffn"] = tuple(t[l // 2].astype(BF16) for t in (ffn_w1, ffn_w3, ffn_w2))
    else:
        router = jnp.pad(moe_router[l // 2], ((0, 0), (0, LANES - N_EXPERTS)))
        r_hi = router.astype(BF16)
        lp["router"] = jnp.concatenate([r_hi, (router - r_hi.astype(F32)).astype(BF16)], axis=1)
        lp["ffn"] = _cast_bf16((moe_w1[l // 2], moe_w3[l // 2])) + _cast_bf16((moe_w2[l // 2],))
    return lp


def _rope_tables(seq):
    t = np.arange(seq)
    n = HEAD_DIM // 4
    inv = ROPE_THETA ** (-np.arange(n, dtype=np.float64) / n)
    ang_r = (t // GRID_W)[:, None] * inv[None, :]
    ang_c = (t % GRID_W)[:, None] * inv[None, :]
    zero = np.zeros_like(ang_r)
    cos = np.concatenate([np.cos(ang_r)] * 2 + [np.cos(ang_c)] * 2, axis=1)
    sa = np.concatenate([-np.sin(ang_r), zero, -np.sin(ang_c), zero], axis=1)
    sb = np.concatenate([zero, np.sin(ang_r), zero, np.sin(ang_c)], axis=1)
    return tuple(jnp.asarray(a, F32) for a in (cos, sa, sb))


def _trunk(x, mod, lp, nseq, seq, lat, layer=0, depth=1, carry=None):
    rows = nseq * seq
    proj = _inproj(x, mod, lp["norm_mix_g"], lp["w_in"], tm=min(2048, rows), tn=6 * MXU_WIDTH)
    if lat is None:
        attn, k_new, v_new = _attn_ctx(proj, lp["q_norm_g"], lp["k_norm_g"], nseq, seq, layer, depth,
                                       () if carry is None else carry[:2])
        *fwd, h_f = _ssd_pass(proj, lp, 0, nseq, seq, None, None, layer, depth, None if carry is None else carry[2])
        ssm, h_b = _ssd_pass(proj, lp, 1, nseq, seq, None, fwd, layer, depth, None if carry is None else carry[3])
    else:
        cache_k, cache_v, h0_f, h0_b, tables = lat
        kall, vall = _prep_lat(proj, cache_k, cache_v, *tables, lp["k_norm_g"], nseq, seq)
        attn = _attn_lat(proj, kall, vall, *tables, lp["q_norm_g"], nseq, seq, tq=256)
        k_new = v_new = None
        *fwd, h_f = _ssd_pass(proj, lp, 0, nseq, seq, h0_f)
        ssm, h_b = _ssd_pass(proj, lp, 1, nseq, seq, h0_b, fwd)
    outs = _merge(attn, ssm, proj, x, mod, lp, seq, tm=512)
    w1, w3, w2 = lp["ffn"]
    if lp["router"] is None:
        x = _ffn(outs[1], outs[0], mod, w1, w3, w2, tm=512)
    else:
        x = _moe(outs[1], outs[0], mod, outs[2], outs[3], w1, w3, w2, tile=512)
    return x, (k_new, v_new, h_f, h_b)


def kernel(x_prompt, x_sample, cache_k, cache_v, state_ssm_fwd, state_ssm_bwd, c, c_ctx, w_mod, b_mod, norm_mix_g, norm_ffn_g, w_in, q_norm_g, k_norm_g, w_attn_out, ssm_conv_w, ssm_conv_b, ssm_dt_bias, ssm_a_log, ssm_d, ssm_norm_g, w_ssm_out, sconv_w, w_sconv_out, w_merge, ffn_w1, ffn_w3, ffn_w2, moe_router, moe_w1, moe_w3, moe_w2):
    batch, seq, _ = x_prompt.shape
    nb, dseq, _ = x_sample.shape
    depth = w_in.shape[0]
    past = cache_k.shape[2]
    kvw = N_KV_HEADS * HEAD_DIM
    npair = SSM_HEADS // 2
    assert nb + 1 <= SUBLANES

    cond = jnp.zeros((SUBLANES, D_MODEL), F32).at[0].set(c_ctx).at[1:1 + nb].set(c)
    tables = _rope_tables(dseq)
    y_ctx = x_prompt.reshape(batch * seq, D_MODEL)
    y_lat = x_sample.reshape(nb * dseq, D_MODEL)
    mods = _modulation(cond, w_mod, b_mod.reshape(depth, 1, -1))
    carry = None
    for l in range(depth):
        lp = _layer_params(l, w_in, q_norm_g, k_norm_g, w_attn_out, ssm_conv_w, ssm_conv_b, ssm_dt_bias,
                           ssm_a_log, ssm_d, ssm_norm_g, w_ssm_out, sconv_w, w_sconv_out, w_merge,
                           norm_mix_g, norm_ffn_g, ffn_w1, ffn_w3, ffn_w2, moe_router, moe_w1, moe_w3, moe_w2)
        m = jnp.pad(mods[l].reshape(SUBLANES, 6, D_MODEL), ((0, 0), (0, SUBLANES - 6), (0, 0)))
        y_ctx, carry = _trunk(y_ctx, m[0:1], lp, batch, seq, None, l, depth, carry)
        lat = (cache_k[:, l].reshape(nb, past, kvw), cache_v[:, l].reshape(nb, past, kvw),
               state_ssm_fwd[:, l].reshape(nb, npair, SSM_CHUNK, SSM_STATE),
               state_ssm_bwd[:, l].reshape(nb, npair, SSM_CHUNK, SSM_STATE), tables)
        y_lat = _trunk(y_lat, m[1:1 + nb], lp, nb, dseq, lat)[0]
    k_all, v_all, hf_all, hb_all = carry
    state_shape = (batch, depth, SSM_HEADS, SSM_HEADDIM, SSM_STATE)
    return (y_ctx.reshape(batch, seq, D_MODEL), y_lat.reshape(nb, dseq, D_MODEL),
            k_all, v_all,
            hf_all.reshape(state_shape), hb_all.reshape(state_shape))
```
